```python
import functools
import jax
import jax.numpy as jnp
from jax import lax
import numpy as np


D_MODEL = 2048
BATCH = 8
SEQ = 4096
DEPTH = 1

POOL_WIDTH = D_MODEL // 4
POOL_WINDOWS = (2, 4, 8, 16)
N_POOL_GROUPS = len(POOL_WINDOWS)
POOL_GROUP = POOL_WIDTH // N_POOL_GROUPS
RWKV_WIDTH = D_MODEL - POOL_WIDTH
HEAD_SIZE = 64
N_RWKV_HEADS = RWKV_WIDTH // HEAD_SIZE
DECAY_LORA = 64
AAA_LORA = 64
GATE_LORA = 224
RWKV_IN = 3 * RWKV_WIDTH + DECAY_LORA + AAA_LORA + GATE_LORA
IN_WIDTH = POOL_WIDTH + RWKV_IN
D_FF = 5632
MACARON_WEIGHT = 0.5
N_SUBLAYERS = 3
N_MOD = 3
NORM_EPS = 1e-6
LN_X_EPS = 1e-5 * HEAD_SIZE

kernel_name = 'hybrid_pool_rwkv7_macaron_block'


def rms_norm(x, gain):
    xf = x.astype(jnp.float32)
    y = xf * lax.rsqrt(jnp.mean(xf * xf, axis=-1, keepdims=True) + NORM_EPS)
    return (y * gain.astype(jnp.float32)).astype(x.dtype)


def modulate(h, shift, scale):
    return h * (1 + scale[:, None, :]) + shift[:, None, :]


def token_shift(p, mu):
    prev = jnp.pad(p, ((0, 0), (1, 0), (0, 0)))[:, :-1]
    return p + mu * (prev - p)


def swiglu(h, w_gate, w_up, w_down):
    return (jax.nn.silu(h @ w_gate) * (h @ w_up)) @ w_down


def multiscale_pool(u, pool_w, pool_scale):
    B, S, _ = u.shape
    cs = jnp.cumsum(u.astype(jnp.float32), axis=1)
    t = jnp.arange(1, S + 1, dtype=jnp.float32)
    outs = []
    for gi, win in enumerate(POOL_WINDOWS):
        lo, hi = gi * POOL_GROUP, (gi + 1) * POOL_GROUP
        c_g = cs[..., lo:hi]
        lagged = jnp.pad(c_g, ((0, 0), (win, 0), (0, 0)))[:, :S]
        count = jnp.minimum(t, float(win))[None, :, None]
        mean = (c_g - lagged) / count
        outs.append(mean.astype(u.dtype) - u[..., lo:hi])
    pooled = jnp.stack(outs, axis=2)
    mixed = jnp.einsum('bsgc,gcd->bsgd', pooled, pool_w)
    return mixed.reshape(B, S, POOL_WIDTH) * pool_scale


def rwkv7_time_mix(p, mu, w0, w2, a0, a2, g2, k_k, k_a, r_k, lnx_w, lnx_b):
    B, S, _ = p.shape
    H, N, R = N_RWKV_HEADS, HEAD_SIZE, RWKV_WIDTH
    f32 = jnp.float32
    p = token_shift(p, mu)
    r = p[..., :R]
    k = p[..., R:2 * R]
    v = p[..., 2 * R:3 * R]
    o = 3 * R
    xw = p[..., o:o + DECAY_LORA]
    o = o + DECAY_LORA
    xa = p[..., o:o + AAA_LORA]
    o = o + AAA_LORA
    xg = p[..., o:]
    w_log = -jax.nn.softplus(-(w0 + jnp.tanh(xw) @ w2)) - 0.5
    decay = jnp.exp(-jnp.exp(w_log.astype(f32)))
    a = jax.nn.sigmoid(a0 + xa @ a2)
    g = jax.nn.sigmoid(xg) @ g2
    kk = (k * k_k).astype(f32).reshape(B, S, H, N)
    kk = kk / jnp.maximum(jnp.linalg.norm(kk, axis=-1, keepdims=True), 1e-12)
    k = k * (1 + (a - 1) * k_a)
    r_h = r.astype(f32).reshape(B, S, H, N)
    k_h = k.astype(f32).reshape(B, S, H, N)
    v_h = v.astype(f32).reshape(B, S, H, N)
    a_h = a.astype(f32).reshape(B, S, H, N)
    w_h = decay.reshape(B, S, H, N)

    def step(state, inp):
        r_t, w_t, k_t, v_t, kk_t, a_t = inp
        sa = jnp.einsum('bhvk,bhk->bhv', state, -kk_t)
        state = (state * w_t[:, :, None, :]
                 + sa[..., None] * (kk_t * a_t)[:, :, None, :]
                 + v_t[..., None] * k_t[:, :, None, :])
        return state, jnp.einsum('bhvk,bhk->bhv', state, r_t)

    xs = tuple(jnp.moveaxis(t, 1, 0) for t in (r_h, w_h, k_h, v_h, kk, a_h))
    state0 = jnp.zeros((B, H, N, N), f32)
    _, y = lax.scan(step, state0, xs)
    y = jnp.moveaxis(y, 0, 1)
    mean = jnp.mean(y, axis=-1, keepdims=True)
    var = jnp.mean(jnp.square(y - mean), axis=-1, keepdims=True)
    y = ((y - mean) * lax.rsqrt(var + LN_X_EPS) * lnx_w.astype(f32).reshape(H, N)
         + lnx_b.astype(f32).reshape(H, N))
    bonus = jnp.sum(r_h * k_h * r_k.astype(f32), axis=-1, keepdims=True) * v_h
    y = (y + bonus).reshape(B, S, R).astype(p.dtype)
    return y * g


def hybrid_mixer(h, w_in, mu_shift, pool_w, pool_scale, w0, w2, a0, a2, g2,
                 k_k, k_a, r_k, lnx_w, lnx_b, w_out):
    p = h @ w_in
    y_pool = multiscale_pool(p[..., :POOL_WIDTH], pool_w, pool_scale)
    y_rwkv = rwkv7_time_mix(p[..., POOL_WIDTH:], mu_shift, w0, w2, a0, a2, g2,
                            k_k, k_a, r_k, lnx_w, lnx_b)
    return jnp.concatenate([y_pool, y_rwkv], axis=-1) @ w_out


def sandwich_sublayer(x, fn, gain_pre, gain_post, shift, scale, gate, weight):
    h = modulate(rms_norm(x, gain_pre), shift, scale)
    y = rms_norm(fn(h), gain_post)
    return x + weight * (1 + gate[:, None, :]) * y


def _fwd_setup_inputs(seed: int = 0) -> dict:
    key = jax.random.key(seed)
    ks = jax.random.split(key, 32)
    f32 = jnp.float32
    L, D = DEPTH, D_MODEL

    def nrm(k, shape, scale):
        return jax.random.normal(k, shape, f32) * scale

    return {
        'x': nrm(ks[0], (BATCH, SEQ, D), 1.0),
        'c': nrm(ks[1], (BATCH, D), 1.0),
        'w_ada': nrm(ks[2], (L, D, N_SUBLAYERS * N_MOD * D), 0.1 * D ** -0.5),
        'b_ada': nrm(ks[3], (L, N_SUBLAYERS * N_MOD * D), 0.02),
        'norm_pre': 1.0 + nrm(ks[4], (L, N_SUBLAYERS, D), 0.05),
        'norm_post': 1.0 + nrm(ks[5], (L, N_SUBLAYERS, D), 0.05),
        'ffn1_w_gate': nrm(ks[6], (L, D, D_FF), D ** -0.5),
        'ffn1_w_up': nrm(ks[7], (L, D, D_FF), D ** -0.5),
        'ffn1_w_down': nrm(ks[8], (L, D_FF, D), D_FF ** -0.5),
        'w_in': nrm(ks[9], (L, D, IN_WIDTH), D ** -0.5),
        'mu_shift': jax.random.uniform(ks[10], (L, RWKV_IN), f32, 0.0, 1.0),
        'pool_w': nrm(ks[11], (L, N_POOL_GROUPS, POOL_GROUP, POOL_GROUP), POOL_GROUP ** -0.5),
        'pool_scale': 1.0 + nrm(ks[12], (L, POOL_WIDTH), 0.1),
        'w0': jax.random.uniform(ks[13], (L, RWKV_WIDTH), f32, -6.0, 1.0),
        'w2': nrm(ks[14], (L, DECAY_LORA, RWKV_WIDTH), 0.1 * DECAY_LORA ** -0.5),
        'a0': nrm(ks[15], (L, RWKV_WIDTH), 0.1),
        'a2': nrm(ks[16], (L, AAA_LORA, RWKV_WIDTH), 0.5 * AAA_LORA ** -0.5),
        'g2': nrm(ks[17], (L, GATE_LORA, RWKV_WIDTH), GATE_LORA ** -0.5),
        'k_k': 0.85 + nrm(ks[18], (L, RWKV_WIDTH), 0.05),
        'k_a': 1.0 + nrm(ks[19], (L, RWKV_WIDTH), 0.05),
        'r_k': nrm(ks[20], (L, N_RWKV_HEADS, HEAD_SIZE), 0.1),
        'lnx_w': 1.0 + nrm(ks[21], (L, RWKV_WIDTH), 0.05),
        'lnx_b': nrm(ks[22], (L, RWKV_WIDTH), 0.02),
        'w_out': nrm(ks[23], (L, D, D), D ** -0.5),
        'ffn2_w_gate': nrm(ks[24], (L, D, D_FF), D ** -0.5),
        'ffn2_w_up': nrm(ks[25], (L, D, D_FF), D ** -0.5),
        'ffn2_w_down': nrm(ks[26], (L, D_FF, D), D_FF ** -0.5),
    }


def _fwd_reference(x, c, w_ada, b_ada, norm_pre, norm_post, ffn1_w_gate, ffn1_w_up, ffn1_w_down,
              w_in, mu_shift, pool_w, pool_scale, w0, w2, a0, a2, g2, k_k, k_a, r_k,
              lnx_w, lnx_b, w_out, ffn2_w_gate, ffn2_w_up, ffn2_w_down):
    B = x.shape[0]
    for l in range(DEPTH):
        mod = (jax.nn.silu(c) @ w_ada[l] + b_ada[l]).reshape(B, N_SUBLAYERS, N_MOD, D_MODEL)
        ffn1 = functools.partial(swiglu, w_gate=ffn1_w_gate[l], w_up=ffn1_w_up[l], w_down=ffn1_w_down[l])
        mixer = functools.partial(
            hybrid_mixer, w_in=w_in[l], mu_shift=mu_shift[l], pool_w=pool_w[l],
            pool_scale=pool_scale[l], w0=w0[l], w2=w2[l], a0=a0[l], a2=a2[l], g2=g2[l],
            k_k=k_k[l], k_a=k_a[l], r_k=r_k[l], lnx_w=lnx_w[l], lnx_b=lnx_b[l], w_out=w_out[l])
        ffn2 = functools.partial(swiglu, w_gate=ffn2_w_gate[l], w_up=ffn2_w_up[l], w_down=ffn2_w_down[l])
        x = sandwich_sublayer(x, ffn1, norm_pre[l, 0], norm_post[l, 0],
                              mod[:, 0, 0], mod[:, 0, 1], mod[:, 0, 2], MACARON_WEIGHT)
        x = sandwich_sublayer(x, mixer, norm_pre[l, 1], norm_post[l, 1],
                              mod[:, 1, 0], mod[:, 1, 1], mod[:, 1, 2], 1.0)
        x = sandwich_sublayer(x, ffn2, norm_pre[l, 2], norm_post[l, 2],
                              mod[:, 2, 0], mod[:, 2, 1], mod[:, 2, 2], MACARON_WEIGHT)
    return x


import jax as _jax
import jax.numpy as _jnp

TWIN_FORMAT = 'train_step'
FWD_PARAMS = ['x', 'c', 'w_ada', 'b_ada', 'norm_pre', 'norm_post', 'ffn1_w_gate', 'ffn1_w_up', 'ffn1_w_down', 'w_in', 'mu_shift', 'pool_w', 'pool_scale', 'w0', 'w2', 'a0', 'a2', 'g2', 'k_k', 'k_a', 'r_k', 'lnx_w', 'lnx_b', 'w_out', 'ffn2_w_gate', 'ffn2_w_up', 'ffn2_w_down']
TWIN_WEIGHTS = ['w_ada', 'b_ada', 'norm_pre', 'norm_post', 'ffn1_w_gate', 'ffn1_w_up', 'ffn1_w_down', 'w_in', 'mu_shift', 'pool_w', 'pool_scale', 'w0', 'w2', 'a0', 'a2', 'g2', 'k_k', 'k_a', 'r_k', 'lnx_w', 'lnx_b', 'w_out', 'ffn2_w_gate', 'ffn2_w_up', 'ffn2_w_down']
TWIN_DIFF_INPUT = 'x'
TWIN_INPUTS = ['x', 'c', 'w_ada', 'b_ada', 'norm_pre', 'norm_post', 'ffn1_w_gate', 'ffn1_w_up', 'ffn1_w_down', 'w_in', 'mu_shift', 'pool_w', 'pool_scale', 'w0', 'w2', 'a0', 'a2', 'g2', 'k_k', 'k_a', 'r_k', 'lnx_w', 'lnx_b', 'w_out', 'ffn2_w_gate', 'ffn2_w_up', 'ffn2_w_down', 'loss_target', 'm_w_ada', 'm_b_ada', 'm_norm_pre', 'm_norm_post', 'm_ffn1_w_gate', 'm_ffn1_w_up', 'm_ffn1_w_down', 'm_w_in', 'm_mu_shift', 'm_pool_w', 'm_pool_scale', 'm_w0', 'm_w2', 'm_a0', 'm_a2', 'm_g2', 'm_k_k', 'm_k_a', 'm_r_k', 'm_lnx_w', 'm_lnx_b', 'm_w_out', 'm_ffn2_w_gate', 'm_ffn2_w_up', 'm_ffn2_w_down', 'v_w_ada', 'v_b_ada', 'v_norm_pre', 'v_norm_post', 'v_ffn1_w_gate', 'v_ffn1_w_up', 'v_ffn1_w_down', 'v_w_in', 'v_mu_shift', 'v_pool_w', 'v_pool_scale', 'v_w0', 'v_w2', 'v_a0', 'v_a2', 'v_g2', 'v_k_k', 'v_k_a', 'v_r_k', 'v_lnx_w', 'v_lnx_b', 'v_w_out', 'v_ffn2_w_gate', 'v_ffn2_w_up', 'v_ffn2_w_down']
TWIN_OUTPUTS = ['loss', 'grad_x', 'grad_w_ada', 'grad_b_ada', 'grad_norm_pre', 'grad_norm_post', 'grad_ffn1_w_gate', 'grad_ffn1_w_up', 'grad_ffn1_w_down', 'grad_w_in', 'grad_mu_shift', 'grad_pool_w', 'grad_pool_scale', 'grad_w0', 'grad_w2', 'grad_a0', 'grad_a2', 'grad_g2', 'grad_k_k', 'grad_k_a', 'grad_r_k', 'grad_lnx_w', 'grad_lnx_b', 'grad_w_out', 'grad_ffn2_w_gate', 'grad_ffn2_w_up', 'grad_ffn2_w_down', 'delta_w_ada', 'delta_b_ada', 'delta_norm_pre', 'delta_norm_post', 'delta_ffn1_w_gate', 'delta_ffn1_w_up', 'delta_ffn1_w_down', 'delta_w_in', 'delta_mu_shift', 'delta_pool_w', 'delta_pool_scale', 'delta_w0', 'delta_w2', 'delta_a0', 'delta_a2', 'delta_g2', 'delta_k_k', 'delta_k_a', 'delta_r_k', 'delta_lnx_w', 'delta_lnx_b', 'delta_w_out', 'delta_ffn2_w_gate', 'delta_ffn2_w_up', 'delta_ffn2_w_down', 'new_m_w_ada', 'new_m_b_ada', 'new_m_norm_pre', 'new_m_norm_post', 'new_m_ffn1_w_gate', 'new_m_ffn1_w_up', 'new_m_ffn1_w_down', 'new_m_w_in', 'new_m_mu_shift', 'new_m_pool_w', 'new_m_pool_scale', 'new_m_w0', 'new_m_w2', 'new_m_a0', 'new_m_a2', 'new_m_g2', 'new_m_k_k', 'new_m_k_a', 'new_m_r_k', 'new_m_lnx_w', 'new_m_lnx_b', 'new_m_w_out', 'new_m_ffn2_w_gate', 'new_m_ffn2_w_up', 'new_m_ffn2_w_down', 'new_v_w_ada', 'new_v_b_ada', 'new_v_norm_pre', 'new_v_norm_post', 'new_v_ffn1_w_gate', 'new_v_ffn1_w_up', 'new_v_ffn1_w_down', 'new_v_w_in', 'new_v_mu_shift', 'new_v_pool_w', 'new_v_pool_scale', 'new_v_w0', 'new_v_w2', 'new_v_a0', 'new_v_a2', 'new_v_g2', 'new_v_k_k', 'new_v_k_a', 'new_v_r_k', 'new_v_lnx_w', 'new_v_lnx_b', 'new_v_w_out', 'new_v_ffn2_w_gate', 'new_v_ffn2_w_up', 'new_v_ffn2_w_down']
TWIN_LEAF_KINDS = {'loss': 'loss', 'grad_x': 'grad_x', 'grad_w_ada': 'grad_w', 'grad_b_ada': 'grad_w', 'grad_norm_pre': 'grad_w', 'grad_norm_post': 'grad_w', 'grad_ffn1_w_gate': 'grad_w', 'grad_ffn1_w_up': 'grad_w', 'grad_ffn1_w_down': 'grad_w', 'grad_w_in': 'grad_w', 'grad_mu_shift': 'grad_w', 'grad_pool_w': 'grad_w', 'grad_pool_scale': 'grad_w', 'grad_w0': 'grad_w', 'grad_w2': 'grad_w', 'grad_a0': 'grad_w', 'grad_a2': 'grad_w', 'grad_g2': 'grad_w', 'grad_k_k': 'grad_w', 'grad_k_a': 'grad_w', 'grad_r_k': 'grad_w', 'grad_lnx_w': 'grad_w', 'grad_lnx_b': 'grad_w', 'grad_w_out': 'grad_w', 'grad_ffn2_w_gate': 'grad_w', 'grad_ffn2_w_up': 'grad_w', 'grad_ffn2_w_down': 'grad_w', 'delta_w_ada': 'delta_w', 'delta_b_ada': 'delta_w', 'delta_norm_pre': 'delta_w', 'delta_norm_post': 'delta_w', 'delta_ffn1_w_gate': 'delta_w', 'delta_ffn1_w_up': 'delta_w', 'delta_ffn1_w_down': 'delta_w', 'delta_w_in': 'delta_w', 'delta_mu_shift': 'delta_w', 'delta_pool_w': 'delta_w', 'delta_pool_scale': 'delta_w', 'delta_w0': 'delta_w', 'delta_w2': 'delta_w', 'delta_a0': 'delta_w', 'delta_a2': 'delta_w', 'delta_g2': 'delta_w', 'delta_k_k': 'delta_w', 'delta_k_a': 'delta_w', 'delta_r_k': 'delta_w', 'delta_lnx_w': 'delta_w', 'delta_lnx_b': 'delta_w', 'delta_w_out': 'delta_w', 'delta_ffn2_w_gate': 'delta_w', 'delta_ffn2_w_up': 'delta_w', 'delta_ffn2_w_down': 'delta_w', 'new_m_w_ada': 'new_m', 'new_m_b_ada': 'new_m', 'new_m_norm_pre': 'new_m', 'new_m_norm_post': 'new_m', 'new_m_ffn1_w_gate': 'new_m', 'new_m_ffn1_w_up': 'new_m', 'new_m_ffn1_w_down': 'new_m', 'new_m_w_in': 'new_m', 'new_m_mu_shift': 'new_m', 'new_m_pool_w': 'new_m', 'new_m_pool_scale': 'new_m', 'new_m_w0': 'new_m', 'new_m_w2': 'new_m', 'new_m_a0': 'new_m', 'new_m_a2': 'new_m', 'new_m_g2': 'new_m', 'new_m_k_k': 'new_m', 'new_m_k_a': 'new_m', 'new_m_r_k': 'new_m', 'new_m_lnx_w': 'new_m', 'new_m_lnx_b': 'new_m', 'new_m_w_out': 'new_m', 'new_m_ffn2_w_gate': 'new_m', 'new_m_ffn2_w_up': 'new_m', 'new_m_ffn2_w_down': 'new_m', 'new_v_w_ada': 'new_v', 'new_v_b_ada': 'new_v', 'new_v_norm_pre': 'new_v', 'new_v_norm_post': 'new_v', 'new_v_ffn1_w_gate': 'new_v', 'new_v_ffn1_w_up': 'new_v', 'new_v_ffn1_w_down': 'new_v', 'new_v_w_in': 'new_v', 'new_v_mu_shift': 'new_v', 'new_v_pool_w': 'new_v', 'new_v_pool_scale': 'new_v', 'new_v_w0': 'new_v', 'new_v_w2': 'new_v', 'new_v_a0': 'new_v', 'new_v_a2': 'new_v', 'new_v_g2': 'new_v', 'new_v_k_k': 'new_v', 'new_v_k_a': 'new_v', 'new_v_r_k': 'new_v', 'new_v_lnx_w': 'new_v', 'new_v_lnx_b': 'new_v', 'new_v_w_out': 'new_v', 'new_v_ffn2_w_gate': 'new_v', 'new_v_ffn2_w_up': 'new_v', 'new_v_ffn2_w_down': 'new_v'}


def _forward(args):
    return _fwd_reference(*[args[k] for k in FWD_PARAMS])


def _output_shape():
    out = _jax.eval_shape(lambda: _forward(_fwd_setup_inputs(0)))
    return out.shape, out.dtype

N_MICROBATCH = 1
ADAM_LR = 0.001
ADAM_B1 = 0.9
ADAM_B2 = 0.999
ADAM_EPS = 1e-08
ADAM_WD = 0.01
ADAM_STEP = 10
PER_EXAMPLE_BATCH_AXIS = {'x': 0, 'c': 0, 'loss_target': 0}
SHARED_INPUTS = []
_WEIGHT_DTYPES = {'w_ada': _jnp.float32, 'b_ada': _jnp.float32, 'norm_pre': _jnp.float32, 'norm_post': _jnp.float32, 'ffn1_w_gate': _jnp.float32, 'ffn1_w_up': _jnp.float32, 'ffn1_w_down': _jnp.float32, 'w_in': _jnp.float32, 'mu_shift': _jnp.float32, 'pool_w': _jnp.float32, 'pool_scale': _jnp.float32, 'w0': _jnp.float32, 'w2': _jnp.float32, 'a0': _jnp.float32, 'a2': _jnp.float32, 'g2': _jnp.float32, 'k_k': _jnp.float32, 'k_a': _jnp.float32, 'r_k': _jnp.float32, 'lnx_w': _jnp.float32, 'lnx_b': _jnp.float32, 'w_out': _jnp.float32, 'ffn2_w_gate': _jnp.float32, 'ffn2_w_up': _jnp.float32, 'ffn2_w_down': _jnp.float32}
MOMENT_SCALE = {'w_ada': 1.897619e+00, 'b_ada': 5.624576e+00, 'norm_pre': 2.007370e-01, 'norm_post': 9.854890e+00, 'ffn1_w_gate': 8.894889e-02, 'ffn1_w_up': 8.637895e-02, 'ffn1_w_down': 1.438946e-01, 'w_in': 1.632014e-01, 'mu_shift': 2.304929e-01, 'pool_w': 3.898000e-01, 'pool_scale': 4.118364e-01, 'w0': 5.315567e-02, 'w2': 7.043189e-03, 'a0': 5.886978e-02, 'a2': 5.656376e-02, 'g2': 1.589950e-01, 'k_k': 1.484684e-01, 'k_a': 1.467421e-01, 'r_k': 2.931116e-01, 'lnx_w': 1.624148e-01, 'lnx_b': 2.318525e-01, 'w_out': 2.279290e-01, 'ffn2_w_gate': 5.031427e-02, 'ffn2_w_up': 6.408720e-02, 'ffn2_w_down': 1.062790e-01}


def _to_microbatches(a, axis):
    t = _jnp.moveaxis(a, axis, 0)
    t = t.reshape((N_MICROBATCH, t.shape[0] // N_MICROBATCH) + t.shape[1:])
    return _jnp.moveaxis(t, 1, axis + 1)


def setup_inputs(seed: int = 0) -> dict:
    inp = _fwd_setup_inputs(seed)
    key = _jax.random.fold_in(_jax.random.key(seed), 7919)
    shape, _ = _output_shape()
    out = dict(inp)
    out["loss_target"] = _jax.random.normal(_jax.random.fold_in(key, 0), shape, _jnp.float32)
    for i, name in enumerate(TWIN_WEIGHTS):
        w = inp[name].astype(_jnp.float32)
        if MOMENT_SCALE is None:
            s = _jnp.sqrt(_jnp.mean(_jnp.square(w)) + 1e-30)
        else:
            s = MOMENT_SCALE[name]
        km, kv = _jax.random.split(_jax.random.fold_in(key, i + 1))
        out[name] = w
        out["m_" + name] = s * _jax.random.normal(km, w.shape, _jnp.float32)
        out["v_" + name] = (s * s) * _jax.random.uniform(kv, w.shape, _jnp.float32, 0.5, 1.5)
    if N_MICROBATCH > 1:
        for name, axis in PER_EXAMPLE_BATCH_AXIS.items():
            out[name] = _to_microbatches(out[name], axis)
    return {'x': out['x'], 'c': out['c'], 'w_ada': out['w_ada'], 'b_ada': out['b_ada'], 'norm_pre': out['norm_pre'], 'norm_post': out['norm_post'], 'ffn1_w_gate': out['ffn1_w_gate'], 'ffn1_w_up': out['ffn1_w_up'], 'ffn1_w_down': out['ffn1_w_down'], 'w_in': out['w_in'], 'mu_shift': out['mu_shift'], 'pool_w': out['pool_w'], 'pool_scale': out['pool_scale'], 'w0': out['w0'], 'w2': out['w2'], 'a0': out['a0'], 'a2': out['a2'], 'g2': out['g2'], 'k_k': out['k_k'], 'k_a': out['k_a'], 'r_k': out['r_k'], 'lnx_w': out['lnx_w'], 'lnx_b': out['lnx_b'], 'w_out': out['w_out'], 'ffn2_w_gate': out['ffn2_w_gate'], 'ffn2_w_up': out['ffn2_w_up'], 'ffn2_w_down': out['ffn2_w_down'], 'loss_target': out['loss_target'], 'm_w_ada': out['m_w_ada'], 'm_b_ada': out['m_b_ada'], 'm_norm_pre': out['m_norm_pre'], 'm_norm_post': out['m_norm_post'], 'm_ffn1_w_gate': out['m_ffn1_w_gate'], 'm_ffn1_w_up': out['m_ffn1_w_up'], 'm_ffn1_w_down': out['m_ffn1_w_down'], 'm_w_in': out['m_w_in'], 'm_mu_shift': out['m_mu_shift'], 'm_pool_w': out['m_pool_w'], 'm_pool_scale': out['m_pool_scale'], 'm_w0': out['m_w0'], 'm_w2': out['m_w2'], 'm_a0': out['m_a0'], 'm_a2': out['m_a2'], 'm_g2': out['m_g2'], 'm_k_k': out['m_k_k'], 'm_k_a': out['m_k_a'], 'm_r_k': out['m_r_k'], 'm_lnx_w': out['m_lnx_w'], 'm_lnx_b': out['m_lnx_b'], 'm_w_out': out['m_w_out'], 'm_ffn2_w_gate': out['m_ffn2_w_gate'], 'm_ffn2_w_up': out['m_ffn2_w_up'], 'm_ffn2_w_down': out['m_ffn2_w_down'], 'v_w_ada': out['v_w_ada'], 'v_b_ada': out['v_b_ada'], 'v_norm_pre': out['v_norm_pre'], 'v_norm_post': out['v_norm_post'], 'v_ffn1_w_gate': out['v_ffn1_w_gate'], 'v_ffn1_w_up': out['v_ffn1_w_up'], 'v_ffn1_w_down': out['v_ffn1_w_down'], 'v_w_in': out['v_w_in'], 'v_mu_shift': out['v_mu_shift'], 'v_pool_w': out['v_pool_w'], 'v_pool_scale': out['v_pool_scale'], 'v_w0': out['v_w0'], 'v_w2': out['v_w2'], 'v_a0': out['v_a0'], 'v_a2': out['v_a2'], 'v_g2': out['v_g2'], 'v_k_k': out['v_k_k'], 'v_k_a': out['v_k_a'], 'v_r_k': out['v_r_k'], 'v_lnx_w': out['v_lnx_w'], 'v_lnx_b': out['v_lnx_b'], 'v_w_out': out['v_w_out'], 'v_ffn2_w_gate': out['v_ffn2_w_gate'], 'v_ffn2_w_up': out['v_ffn2_w_up'], 'v_ffn2_w_down': out['v_ffn2_w_down']}


def _loss(weights, diff, rest, loss_target):
    with _jax.named_scope("forward"):
        args = {**rest, TWIN_DIFF_INPUT: diff, **{k: w.astype(_WEIGHT_DTYPES[k]) for k, w in weights.items()}}
        y = _forward(args)
    with _jax.named_scope("loss_head"):
        err = _jnp.square(y.astype(_jnp.float32) - loss_target)
        return 0.5 * _jnp.sum(_jnp.mean(err, axis=-1)) if err.ndim else 0.5 * err


def _adamw(w, g, m, v):
    m = ADAM_B1 * m + (1.0 - ADAM_B1) * g
    v = ADAM_B2 * v + (1.0 - ADAM_B2) * _jnp.square(g)
    m_hat = m / (1.0 - ADAM_B1 ** ADAM_STEP)
    v_hat = v / (1.0 - ADAM_B2 ** ADAM_STEP)
    delta = -ADAM_LR * (m_hat / (_jnp.sqrt(v_hat) + ADAM_EPS) + ADAM_WD * w)
    return delta, m, v


def reference(x, c, w_ada, b_ada, norm_pre, norm_post, ffn1_w_gate, ffn1_w_up, ffn1_w_down, w_in, mu_shift, pool_w, pool_scale, w0, w2, a0, a2, g2, k_k, k_a, r_k, lnx_w, lnx_b, w_out, ffn2_w_gate, ffn2_w_up, ffn2_w_down, loss_target, m_w_ada, m_b_ada, m_norm_pre, m_norm_post, m_ffn1_w_gate, m_ffn1_w_up, m_ffn1_w_down, m_w_in, m_mu_shift, m_pool_w, m_pool_scale, m_w0, m_w2, m_a0, m_a2, m_g2, m_k_k, m_k_a, m_r_k, m_lnx_w, m_lnx_b, m_w_out, m_ffn2_w_gate, m_ffn2_w_up, m_ffn2_w_down, v_w_ada, v_b_ada, v_norm_pre, v_norm_post, v_ffn1_w_gate, v_ffn1_w_up, v_ffn1_w_down, v_w_in, v_mu_shift, v_pool_w, v_pool_scale, v_w0, v_w2, v_a0, v_a2, v_g2, v_k_k, v_k_a, v_r_k, v_lnx_w, v_lnx_b, v_w_out, v_ffn2_w_gate, v_ffn2_w_up, v_ffn2_w_down):
    given = dict(x=x, c=c, w_ada=w_ada, b_ada=b_ada, norm_pre=norm_pre, norm_post=norm_post, ffn1_w_gate=ffn1_w_gate, ffn1_w_up=ffn1_w_up, ffn1_w_down=ffn1_w_down, w_in=w_in, mu_shift=mu_shift, pool_w=pool_w, pool_scale=pool_scale, w0=w0, w2=w2, a0=a0, a2=a2, g2=g2, k_k=k_k, k_a=k_a, r_k=r_k, lnx_w=lnx_w, lnx_b=lnx_b, w_out=w_out, ffn2_w_gate=ffn2_w_gate, ffn2_w_up=ffn2_w_up, ffn2_w_down=ffn2_w_down, loss_target=loss_target, m_w_ada=m_w_ada, m_b_ada=m_b_ada, m_norm_pre=m_norm_pre, m_norm_post=m_norm_post, m_ffn1_w_gate=m_ffn1_w_gate, m_ffn1_w_up=m_ffn1_w_up, m_ffn1_w_down=m_ffn1_w_down, m_w_in=m_w_in, m_mu_shift=m_mu_shift, m_pool_w=m_pool_w, m_pool_scale=m_pool_scale, m_w0=m_w0, m_w2=m_w2, m_a0=m_a0, m_a2=m_a2, m_g2=m_g2, m_k_k=m_k_k, m_k_a=m_k_a, m_r_k=m_r_k, m_lnx_w=m_lnx_w, m_lnx_b=m_lnx_b, m_w_out=m_w_out, m_ffn2_w_gate=m_ffn2_w_gate, m_ffn2_w_up=m_ffn2_w_up, m_ffn2_w_down=m_ffn2_w_down, v_w_ada=v_w_ada, v_b_ada=v_b_ada, v_norm_pre=v_norm_pre, v_norm_post=v_norm_post, v_ffn1_w_gate=v_ffn1_w_gate, v_ffn1_w_up=v_ffn1_w_up, v_ffn1_w_down=v_ffn1_w_down, v_w_in=v_w_in, v_mu_shift=v_mu_shift, v_pool_w=v_pool_w, v_pool_scale=v_pool_scale, v_w0=v_w0, v_w2=v_w2, v_a0=v_a0, v_a2=v_a2, v_g2=v_g2, v_k_k=v_k_k, v_k_a=v_k_a, v_r_k=v_r_k, v_lnx_w=v_lnx_w, v_lnx_b=v_lnx_b, v_w_out=v_w_out, v_ffn2_w_gate=v_ffn2_w_gate, v_ffn2_w_up=v_ffn2_w_up, v_ffn2_w_down=v_ffn2_w_down)
    weights = {n: given[n] for n in TWIN_WEIGHTS}
    shared = {n: given[n] for n in SHARED_INPUTS}
    per_example = {n: given[n] for n in ['x', 'c']}
    grad_fn = _jax.value_and_grad(_loss, argnums=(0, 1))

    def one_microbatch(ex, loss_target):
        ex = dict(ex)
        diff = ex.pop(TWIN_DIFF_INPUT)
        return grad_fn(weights, diff, {**shared, **ex}, loss_target)

    if N_MICROBATCH == 1:
        loss, (grad_w, grad_x) = one_microbatch(per_example, given["loss_target"])
    else:
        def body(carry, xs):
            loss_sum, grad_sum = carry
            l_k, (gw_k, gx_k) = one_microbatch(xs[0], xs[1])
            with _jax.named_scope("update"):
                return (loss_sum + l_k, _jax.tree.map(_jnp.add, grad_sum, gw_k)), gx_k

        init = (_jnp.zeros((), _jnp.float32), _jax.tree.map(_jnp.zeros_like, weights))
        (loss, grad_w), grad_x = _jax.lax.scan(body, init, (per_example, given["loss_target"]))
    with _jax.named_scope("update"):
        delta_w, new_m, new_v = {}, {}, {}
        for n in TWIN_WEIGHTS:
            delta_w[n], new_m[n], new_v[n] = _adamw(weights[n], grad_w[n], given["m_" + n], given["v_" + n])
    return (loss, grad_x, *[grad_w[n] for n in TWIN_WEIGHTS], *[delta_w[n] for n in TWIN_WEIGHTS],
            *[new_m[n] for n in TWIN_WEIGHTS], *[new_v[n] for n in TWIN_WEIGHTS])
```

```python
import functools

import jax
import jax.numpy as jnp
from jax import lax
from jax.experimental import pallas as pl
from jax.experimental.pallas import tpu as pltpu

F32 = jnp.float32
BF16 = jnp.bfloat16
N_DEV = 8
MESH_AXES = ("x", "y", "c")

NORM_EPS = 1e-6
HEAD = 64
LN_X_EPS = 1e-5 * HEAD
POOL_GROUPS = 4
POOL_GROUP = 128
MACARON = 0.5
LORA_W, LORA_A, LORA_G = 64, 64, 224
LORA_PAD = 384
ADAM_LR, ADAM_B1, ADAM_B2, ADAM_EPS, ADAM_WD, ADAM_STEP = 0.001, 0.9, 0.999, 1e-08, 0.01, 10

FF_TILE = 512
ROW_TILE = 256
SCAN_T = 128
SCAN_G = 3
VMEM_CAP = 56 * 1024 * 1024


def _cp(sem, vmem_mb):
    return pltpu.CompilerParams(dimension_semantics=sem, vmem_limit_bytes=min(vmem_mb * 1024 * 1024, VMEM_CAP))


def _my_index():
    return 4 * lax.axis_index("x") + 2 * lax.axis_index("y") + lax.axis_index("c")


def _exchange(arrays, *, scatter, name):
    n = len(arrays)
    out_shapes = []
    for a in arrays:
        shp = a.shape if scatter else (N_DEV,) + a.shape
        out_shapes.append(jax.ShapeDtypeStruct(shp, a.dtype))

    def body(*refs):
        ins, outs = refs[:n], refs[n:2 * n]
        send_sems, recv_sems, local_sems = refs[2 * n:]
        me = _my_index()

        def dev(p):
            return (p // 4, (p // 2) % 2, p % 2)

        def copy(i, d):
            peer = (me + d) % N_DEV
            src = ins[i].at[peer] if scatter else ins[i]
            return pltpu.make_async_remote_copy(
                src_ref=src, dst_ref=outs[i].at[me], send_sem=send_sems.at[i, d - 1],
                recv_sem=recv_sems.at[i, d - 1], device_id=dev(peer), device_id_type=pl.DeviceIdType.MESH)

        def arrival(i, d):
            frm = (me + N_DEV - d) % N_DEV
            src = ins[i].at[frm] if scatter else ins[i]
            return pltpu.make_async_remote_copy(
                src_ref=src, dst_ref=outs[i].at[frm], send_sem=send_sems.at[i, d - 1],
                recv_sem=recv_sems.at[i, d - 1], device_id=dev(frm), device_id_type=pl.DeviceIdType.MESH)

        locals_ = []
        for i in range(n):
            src = ins[i].at[me] if scatter else ins[i]
            lc = pltpu.make_async_copy(src, outs[i].at[me], local_sems.at[i])
            lc.start()
            locals_.append(lc)
        sends = [copy(i, d) for d in range(1, N_DEV) for i in range(n)]
        for cp in sends:
            cp.start()
        for d in range(1, N_DEV):
            for i in range(n):
                arrival(i, d).wait_recv()
        for cp in sends:
            cp.wait_send()
        for lc in locals_:
            lc.wait()

    hbm = pl.BlockSpec(memory_space=pltpu.HBM)
    return pl.pallas_call(
        body, name=name, out_shape=tuple(out_shapes), in_specs=[hbm] * n, out_specs=tuple([hbm] * n),
        scratch_shapes=[pltpu.SemaphoreType.DMA((n, N_DEV - 1)), pltpu.SemaphoreType.DMA((n, N_DEV - 1)),
                        pltpu.SemaphoreType.DMA((n,))],
    )(*arrays)


def _mm(a, b, *, ta=False, tb=False, tm, tn, tk, out_dtype=F32, name):
    M = a.shape[1] if ta else a.shape[0]
    K = a.shape[0] if ta else a.shape[1]
    N = b.shape[0] if tb else b.shape[1]
    tm, tn, tk = min(tm, M), min(tn, N), min(tk, K)
    assert M % tm == 0 and N % tn == 0 and K % tk == 0, (name, M, N, K, tm, tn, tk)
    nk = K // tk
    dims = (((0 if ta else 1,), (1 if tb else 0,)), ((), ()))

    def body(a_ref, b_ref, o_ref, acc_ref):
        k = pl.program_id(2)

        @pl.when(k == 0)
        def _():
            acc_ref[...] = jnp.zeros_like(acc_ref)

        acc_ref[...] += lax.dot_general(a_ref[...].astype(BF16), b_ref[...].astype(BF16), dims,
                                        preferred_element_type=F32)

        @pl.when(k == nk - 1)
        def _():
            o_ref[...] = acc_ref[...].astype(out_dtype)

    a_spec = pl.BlockSpec((tk, tm), lambda i, j, k: (k, i)) if ta else pl.BlockSpec((tm, tk), lambda i, j, k: (i, k))
    b_spec = pl.BlockSpec((tn, tk), lambda i, j, k: (j, k)) if tb else pl.BlockSpec((tk, tn), lambda i, j, k: (k, j))
    blk = 2 * (tm * tk * a.dtype.itemsize + tk * tn * b.dtype.itemsize + tm * tn * jnp.dtype(out_dtype).itemsize)
    return pl.pallas_call(
        body, name=name, grid=(M // tm, N // tn, nk), in_specs=[a_spec, b_spec],
        out_specs=pl.BlockSpec((tm, tn), lambda i, j, k: (i, j)),
        out_shape=jax.ShapeDtypeStruct((M, N), out_dtype),
        scratch_shapes=[pltpu.VMEM((tm, tn), F32)],
        compiler_params=_cp(("parallel", "parallel", "arbitrary"), (blk + tm * tn * 4) // (1024 * 1024) + 12),
    )(a, b)


def _ffn_up(h, wgu, *, tm, tk, name):
    S, D = h.shape
    F2 = wgu.shape[1]
    tm = min(tm, S)
    tn = FF_TILE
    nk = D // tk

    def body(h_ref, w_ref, au_ref, s_ref, acc_ref):
        k = pl.program_id(2)

        @pl.when(k == 0)
        def _():
            acc_ref[...] = jnp.zeros_like(acc_ref)

        acc_ref[...] += jnp.dot(h_ref[...], w_ref[...], preferred_element_type=F32)

        @pl.when(k == nk - 1)
        def _():
            acc = acc_ref[...]
            a = acc[:, :tn]
            u = acc[:, tn:]
            au_ref[...] = acc.astype(BF16)
            s_ref[...] = (a * jax.nn.sigmoid(a) * u).astype(BF16)

    return pl.pallas_call(
        body, name=name, grid=(S // tm, F2 // (2 * tn), nk),
        in_specs=[pl.BlockSpec((tm, tk), lambda i, j, k: (i, k)), pl.BlockSpec((tk, 2 * tn), lambda i, j, k: (k, j))],
        out_specs=(pl.BlockSpec((tm, 2 * tn), lambda i, j, k: (i, j)), pl.BlockSpec((tm, tn), lambda i, j, k: (i, j))),
        out_shape=(jax.ShapeDtypeStruct((S, F2), BF16), jax.ShapeDtypeStruct((S, F2 // 2), BF16)),
        scratch_shapes=[pltpu.VMEM((tm, 2 * tn), F32)],
        compiler_params=_cp(("parallel", "parallel", "arbitrary"), 40),
    )(h, wgu)


def _ffn_down_bwd(df, wd, au, *, tm, tk, name):
    S, D = df.shape
    F = wd.shape[0]
    tm = min(tm, S)
    tn = FF_TILE
    nk = D // tk

    def body(df_ref, w_ref, au_ref, dau_ref, acc_ref):
        k = pl.program_id(2)

        @pl.when(k == 0)
        def _():
            acc_ref[...] = jnp.zeros_like(acc_ref)

        acc_ref[...] += lax.dot_general(df_ref[...], w_ref[...], (((1,), (1,)), ((), ())), preferred_element_type=F32)

        @pl.when(k == nk - 1)
        def _():
            ds = acc_ref[...]
            au_v = au_ref[...].astype(F32)
            a = au_v[:, :tn]
            u = au_v[:, tn:]
            sg = jax.nn.sigmoid(a)
            da = ds * u * (sg * (1.0 + a * (1.0 - sg)))
            du = ds * (a * sg)
            dau_ref[:, :tn] = da.astype(BF16)
            dau_ref[:, tn:] = du.astype(BF16)

    return pl.pallas_call(
        body, name=name, grid=(S // tm, F // tn, nk),
        in_specs=[pl.BlockSpec((tm, tk), lambda i, j, k: (i, k)), pl.BlockSpec((tn, tk), lambda i, j, k: (j, k)),
                  pl.BlockSpec((tm, 2 * tn), lambda i, j, k: (i, j))],
        out_specs=pl.BlockSpec((tm, 2 * tn), lambda i, j, k: (i, j)),
        out_shape=jax.ShapeDtypeStruct((S, 2 * F), BF16),
        scratch_shapes=[pltpu.VMEM((tm, tn), F32)],
        compiler_params=_cp(("parallel", "parallel", "arbitrary"), 40),
    )(df, wd, au)


def _fold8(x):
    tm, w = x.shape
    return jnp.sum(x.reshape(tm // 8, 8, w), axis=0)


def _row_spec(tm, w):
    return pl.BlockSpec((tm, w), lambda i: (i, 0))


def _vec_spec(rows, w):
    return pl.BlockSpec((rows, w), lambda i: (0, 0))


def _pre_norm_mod(x, gain, shift, scale1p, *, name):
    S, D = x.shape
    tm = ROW_TILE

    def body(x_ref, g_ref, sh_ref, sc_ref, h_ref):
        xv = x_ref[...]
        rinv = lax.rsqrt(jnp.mean(xv * xv, axis=-1, keepdims=True) + NORM_EPS)
        h_ref[...] = ((xv * rinv) * g_ref[...] * sc_ref[...] + sh_ref[...]).astype(BF16)

    return pl.pallas_call(
        body, name=name, grid=(S // tm,),
        in_specs=[_row_spec(tm, D), _vec_spec(1, D), _vec_spec(1, D), _vec_spec(1, D)],
        out_specs=_row_spec(tm, D), out_shape=jax.ShapeDtypeStruct((S, D), BF16),
        compiler_params=_cp(("parallel",), 32),
    )(x, gain, shift, scale1p)


def _pre_norm_mod_bwd(dh, x, dres, gain, scale1p, *, name):
    S, D = x.shape
    tm = ROW_TILE
    n = S // tm

    def body(dh_ref, x_ref, dr_ref, g_ref, sc_ref, dx_ref, sums_ref, acc_ref):
        i = pl.program_id(0)

        @pl.when(i == 0)
        def _():
            acc_ref[...] = jnp.zeros_like(acc_ref)

        xv = x_ref[...]
        dhv = dh_ref[...]
        g = g_ref[...]
        rinv = lax.rsqrt(jnp.mean(xv * xv, axis=-1, keepdims=True) + NORM_EPS)
        xn = xv * rinv
        dn = dhv * sc_ref[...]
        dxn = dn * g
        dx_ref[...] = dr_ref[...] + rinv * (dxn - xn * jnp.mean(dxn * xn, axis=-1, keepdims=True))
        acc_ref[0] += _fold8(dhv)
        acc_ref[1] += _fold8(dhv * (xn * g))
        acc_ref[2] += _fold8(dn * xn)

        @pl.when(i == n - 1)
        def _():
            for q in range(3):
                sums_ref[q:q + 1, :] = jnp.sum(acc_ref[q], axis=0, keepdims=True)

    return pl.pallas_call(
        body, name=name, grid=(n,),
        in_specs=[_row_spec(tm, D), _row_spec(tm, D), _row_spec(tm, D), _vec_spec(1, D), _vec_spec(1, D)],
        out_specs=(_row_spec(tm, D), _vec_spec(3, D)),
        out_shape=(jax.ShapeDtypeStruct((S, D), F32), jax.ShapeDtypeStruct((3, D), F32)),
        scratch_shapes=[pltpu.VMEM((3, 8, D), F32)],
        compiler_params=_cp(("arbitrary",), 40),
    )(dh, x, dres, gain, scale1p)


def _post_norm_res(x, f, gain, gw, *, name):
    S, D = x.shape
    tm = ROW_TILE

    def body(x_ref, f_ref, g_ref, gw_ref, o_ref):
        fv = f_ref[...]
        rinv = lax.rsqrt(jnp.mean(fv * fv, axis=-1, keepdims=True) + NORM_EPS)
        o_ref[...] = x_ref[...] + gw_ref[...] * ((fv * rinv) * g_ref[...])

    return pl.pallas_call(
        body, name=name, grid=(S // tm,),
        in_specs=[_row_spec(tm, D), _row_spec(tm, D), _vec_spec(1, D), _vec_spec(1, D)],
        out_specs=_row_spec(tm, D), out_shape=jax.ShapeDtypeStruct((S, D), F32),
        compiler_params=_cp(("parallel",), 32),
    )(x, f, gain, gw)


def _post_norm_res_bwd(dxo, f, gain, gw, weight, *, name):
    S, D = f.shape
    tm = ROW_TILE
    n = S // tm

    def body(d_ref, f_ref, g_ref, gw_ref, df_ref, sums_ref, acc_ref):
        i = pl.program_id(0)

        @pl.when(i == 0)
        def _():
            acc_ref[...] = jnp.zeros_like(acc_ref)

        fv = f_ref[...]
        dv = d_ref[...]
        g = g_ref[...]
        rinv = lax.rsqrt(jnp.mean(fv * fv, axis=-1, keepdims=True) + NORM_EPS)
        fh = fv * rinv
        dy = dv * gw_ref[...]
        dfh = dy * g
        df_ref[...] = (rinv * (dfh - fh * jnp.mean(dfh * fh, axis=-1, keepdims=True))).astype(BF16)
        acc_ref[0] += _fold8(weight * dv * (fh * g))
        acc_ref[1] += _fold8(dy * fh)

        @pl.when(i == n - 1)
        def _():
            for q in range(2):
                sums_ref[q:q + 1, :] = jnp.sum(acc_ref[q], axis=0, keepdims=True)

    return pl.pallas_call(
        body, name=name, grid=(n,),
        in_specs=[_row_spec(tm, D), _row_spec(tm, D), _vec_spec(1, D), _vec_spec(1, D)],
        out_specs=(_row_spec(tm, D), _vec_spec(2, D)),
        out_shape=(jax.ShapeDtypeStruct((S, D), BF16), jax.ShapeDtypeStruct((2, D), F32)),
        scratch_shapes=[pltpu.VMEM((2, 8, D), F32)],
        compiler_params=_cp(("arbitrary",), 40),
    )(dxo, f, gain, gw)


def _loss_head(y, target, *, name):
    S, D = y.shape
    tm = ROW_TILE

    def body(y_ref, t_ref, l_ref, dy_ref):
        i = pl.program_id(0)

        @pl.when(i == 0)
        def _():
            l_ref[...] = jnp.zeros_like(l_ref)

        err = y_ref[...] - t_ref[...]
        dy_ref[...] = err * (1.0 / D)
        row = jnp.sum(err * err, axis=-1, keepdims=True) * (0.5 / D)
        l_ref[...] += jnp.sum(row, axis=0, keepdims=True)

    return pl.pallas_call(
        body, name=name, grid=(S // tm,),
        in_specs=[_row_spec(tm, D), _row_spec(tm, D)],
        out_specs=(_vec_spec(1, 1), _row_spec(tm, D)),
        out_shape=(jax.ShapeDtypeStruct((1, 1), F32), jax.ShapeDtypeStruct((S, D), F32)),
        compiler_params=_cp(("arbitrary",), 32),
    )(y, target)


def _shift_down(z, j, row):
    return jnp.where(row >= j, pltpu.roll(z, j, 0), 0.0)


def _shift_up(z, j, row, n):
    return jnp.where(row < n - j, pltpu.roll(z, n - j, 0), 0.0)


def _pool_fwd(p, pool_w, pool_scale, *, name):
    S = p.shape[0]
    C = POOL_GROUP

    def body(u_ref, w_ref, sc_ref, o_ref, y_ref):
        g = pl.program_id(0)
        u = u_ref[...]
        row = lax.broadcasted_iota(jnp.int32, (S, C), 0)
        s1 = u + _shift_down(u, 1, row)
        s2 = s1 + _shift_down(s1, 2, row)
        s3 = s2 + _shift_down(s2, 4, row)
        s4 = s3 + _shift_down(s3, 8, row)
        gi = jnp.zeros((S, C), jnp.int32) + g
        win = jnp.where(gi == 0, s1, jnp.where(gi == 1, s2, jnp.where(gi == 2, s3, s4)))
        width = jnp.where(gi == 0, 2, jnp.where(gi == 1, 4, jnp.where(gi == 2, 8, 16)))
        count = jnp.minimum(row + 1, width).astype(F32)
        o = win / count - u
        o_ref[...] = o
        y_ref[...] = jnp.dot(o.astype(BF16), w_ref[0].astype(BF16), preferred_element_type=F32) * sc_ref[...]

    col = pl.BlockSpec((S, C), lambda g: (0, g))
    return pl.pallas_call(
        body, name=name, grid=(POOL_GROUPS,),
        in_specs=[col, pl.BlockSpec((1, C, C), lambda g: (g, 0, 0)), pl.BlockSpec((1, C), lambda g: (0, g))],
        out_specs=(col, col),
        out_shape=(jax.ShapeDtypeStruct((S, POOL_GROUPS * C), F32), jax.ShapeDtypeStruct((S, POOL_GROUPS * C), F32)),
        compiler_params=_cp(("parallel",), 48),
    )(p, pool_w, pool_scale)


def _pool_bwd(dcat, o, pool_w, pool_scale, *, name):
    S = o.shape[0]
    C = POOL_GROUP

    def body(dy_ref, o_ref, w_ref, sc_ref, du_ref, dw_ref, dsc_ref):
        g = pl.program_id(0)
        dy = dy_ref[...]
        ob = o_ref[...].astype(BF16)
        wb = w_ref[0].astype(BF16)
        mixed = jnp.dot(ob, wb, preferred_element_type=F32)
        dsc_ref[...] = jnp.sum(_fold8(dy * mixed), axis=0, keepdims=True)
        dmix = (dy * sc_ref[...]).astype(BF16)
        dw_ref[0] = lax.dot_general(ob, dmix, (((0,), (0,)), ((), ())), preferred_element_type=F32)
        do = lax.dot_general(dmix, wb, (((1,), (1,)), ((), ())), preferred_element_type=F32)
        row = lax.broadcasted_iota(jnp.int32, (S, C), 0)
        gi = jnp.zeros((S, C), jnp.int32) + g
        width = jnp.where(gi == 0, 2, jnp.where(gi == 1, 4, jnp.where(gi == 2, 8, 16)))
        z = do / jnp.minimum(row + 1, width).astype(F32)
        s1 = z + _shift_up(z, 1, row, S)
        s2 = s1 + _shift_up(s1, 2, row, S)
        s3 = s2 + _shift_up(s2, 4, row, S)
        s4 = s3 + _shift_up(s3, 8, row, S)
        win = jnp.where(gi == 0, s1, jnp.where(gi == 1, s2, jnp.where(gi == 2, s3, s4)))
        du_ref[...] = (win - do).astype(BF16)

    col = pl.BlockSpec((S, C), lambda g: (0, g))
    return pl.pallas_call(
        body, name=name, grid=(POOL_GROUPS,),
        in_specs=[col, col, pl.BlockSpec((1, C, C), lambda g: (g, 0, 0)), pl.BlockSpec((1, C), lambda g: (0, g))],
        out_specs=(col, pl.BlockSpec((1, C, C), lambda g: (g, 0, 0)), pl.BlockSpec((1, C), lambda g: (0, g))),
        out_shape=(jax.ShapeDtypeStruct((S, POOL_GROUPS * C), BF16), jax.ShapeDtypeStruct((POOL_GROUPS, C, C), F32),
                   jax.ShapeDtypeStruct((1, POOL_GROUPS * C), F32)),
        compiler_params=_cp(("parallel",), 48),
    )(dcat, o, pool_w, pool_scale)


def _block_ones():
    r = lax.broadcasted_iota(jnp.int32, (128, 128), 0) // HEAD
    c = lax.broadcasted_iota(jnp.int32, (128, 128), 1) // HEAD
    return jnp.where(r == c, 1.0, 0.0).astype(BF16)


def _segsum(x, bd):
    outs = []
    for j in range(x.shape[1] // 128):
        xs = x[:, j * 128:(j + 1) * 128]
        hi = xs.astype(BF16)
        lo = (xs - hi.astype(F32)).astype(BF16)
        outs.append(jnp.dot(hi, bd, preferred_element_type=F32) + jnp.dot(lo, bd, preferred_element_type=F32))
    return jnp.concatenate(outs, axis=1)


def _prep_common(q, qprev, first, mu, wl, w0, a0, kkw, kaw, R):
    tm = q.shape[0]
    row = lax.broadcasted_iota(jnp.int32, q.shape, 0)
    last = qprev[7:8, :] * first
    prev = jnp.where(row == 0, last, pltpu.roll(q, 1, 0))
    ps = q + mu * (prev - q)
    r = ps[:, 0:R]
    k = ps[:, R:2 * R]
    v = ps[:, 2 * R:3 * R]
    lo_in = ps[:, 3 * R:3 * R + LORA_PAD]
    lane = lax.broadcasted_iota(jnp.int32, (tm, LORA_PAD), 1)
    m_w = lane < LORA_W
    m_a = lane < LORA_W + LORA_A
    m_g = lane < LORA_W + LORA_A + LORA_G
    act = jnp.where(m_w, jnp.tanh(lo_in), jnp.where(m_a, lo_in, jnp.where(m_g, jax.nn.sigmoid(lo_in), 0.0)))
    lo = jnp.dot(act.astype(BF16), wl, preferred_element_type=F32)
    wpre = w0 + lo[:, 0:R]
    apre = a0 + lo[:, R:2 * R]
    g = lo[:, 2 * R:3 * R]
    neg = -wpre
    softplus = jnp.maximum(neg, 0.0) + jnp.log(1.0 + jnp.exp(-jnp.abs(neg)))
    wlog = -softplus - 0.5
    ew = jnp.exp(wlog)
    decay = jnp.exp(-ew)
    a = jax.nn.sigmoid(apre)
    kk = k * kkw
    bd = _block_ones()
    n2 = _segsum(kk * kk, bd)
    nrm = jnp.maximum(jnp.sqrt(n2), 1e-12)
    kap = kk / nrm
    kmul = 1.0 + (a - 1.0) * kaw
    k2 = k * kmul
    return dict(prev=prev, r=r, k=k, v=v, act=act, m_w=m_w, m_a=m_a, m_g=m_g, wpre=wpre, g=g, ew=ew, decay=decay,
                a=a, n2=n2, nrm=nrm, kap=kap, kmul=kmul, k2=k2, bd=bd)


def _prev_rows_spec(tm, w):
    return pl.BlockSpec((8, w), lambda i: (jnp.maximum(i * (tm // 8) - 1, 0), 0))


def _rwkv_prep(q, mu, wl, w0, a0, kkw, kaw, *, name):
    S, QW = q.shape
    R = w0.shape[1]
    tm = ROW_TILE // 2

    def body(q_ref, qp_ref, mu_ref, wl_ref, w0_ref, a0_ref, kk_ref, ka_ref, r_ref, w_ref, k_ref, v_ref, kap_ref,
             b_ref, g_ref):
        first = jnp.where(pl.program_id(0) > 0, 1.0, 0.0)
        t = _prep_common(q_ref[...], qp_ref[...], first, mu_ref[...], wl_ref[...], w0_ref[...], a0_ref[...],
                         kk_ref[...], ka_ref[...], R)
        r_ref[...] = t["r"]
        w_ref[...] = t["decay"]
        k_ref[...] = t["k2"]
        v_ref[...] = t["v"]
        kap_ref[...] = t["kap"]
        b_ref[...] = t["kap"] * t["a"]
        g_ref[...] = t["g"]

    vec = _vec_spec(1, R)
    return pl.pallas_call(
        body, name=name, grid=(S // tm,),
        in_specs=[_row_spec(tm, QW), _prev_rows_spec(tm, QW), _vec_spec(1, QW), _vec_spec(LORA_PAD, 3 * R), vec, vec,
                  vec, vec],
        out_specs=tuple([_row_spec(tm, R)] * 7),
        out_shape=tuple([jax.ShapeDtypeStruct((S, R), F32)] * 7),
        compiler_params=_cp(("parallel",), 48),
    )(q, q, mu, wl, w0, a0, kkw, kaw)


def _rwkv_prep_bwd(q, mu, wl, w0, a0, kkw, kaw, grads, *, name):
    S, QW = q.shape
    R = w0.shape[1]
    tm = ROW_TILE // 2
    n = S // tm

    def body(q_ref, qp_ref, mu_ref, wl_ref, w0_ref, a0_ref, kk_ref, ka_ref, dr_ref, dw_ref, dk2_ref, dv_ref, dkap_ref,
             db_ref, dg_ref, drb_ref, dk2b_ref, dvb_ref, dps_ref, dwl_ref, sums_ref, acc_ref):
        i = pl.program_id(0)

        @pl.when(i == 0)
        def _():
            acc_ref[...] = jnp.zeros_like(acc_ref)
            dwl_ref[...] = jnp.zeros_like(dwl_ref)

        first = jnp.where(i > 0, 1.0, 0.0)
        wl = wl_ref[...]
        kkw = kk_ref[...]
        kaw = ka_ref[...]
        t = _prep_common(q_ref[...], qp_ref[...], first, mu_ref[...], wl, w0_ref[...], a0_ref[...], kkw, kaw, R)
        a, kap, k, act = t["a"], t["kap"], t["k"], t["act"]
        db = db_ref[...]
        dk2 = dk2_ref[...] + dk2b_ref[...]
        dkap = dkap_ref[...] + db * a
        da = db * kap + dk2 * k * kaw
        dk = dk2 * t["kmul"]
        proj = jnp.where(jnp.sqrt(t["n2"]) > 1e-12, _segsum(kap * dkap, t["bd"]), 0.0)
        dkk = (dkap - kap * proj) / t["nrm"]
        dk = dk + dkk * kkw
        dapre = da * a * (1.0 - a)
        dwlog = dw_ref[...] * t["decay"] * (-t["ew"])
        dwpre = dwlog * jax.nn.sigmoid(-t["wpre"])
        acc_ref[0] += _fold8(dwpre)
        acc_ref[1] += _fold8(dapre)
        acc_ref[2] += _fold8(dkk * k)
        acc_ref[3] += _fold8(dk2 * k * (a - 1.0))
        dlo = jnp.concatenate([dwpre, dapre, dg_ref[...]], axis=1).astype(BF16)
        dwl_ref[...] += lax.dot_general(act.astype(BF16), dlo, (((0,), (0,)), ((), ())), preferred_element_type=F32)
        dact = lax.dot_general(dlo, wl, (((1,), (1,)), ((), ())), preferred_element_type=F32)
        dlin = jnp.where(t["m_w"], dact * (1.0 - act * act),
                         jnp.where(t["m_a"], dact, jnp.where(t["m_g"], dact * act * (1.0 - act), 0.0)))
        dps_ref[:, 0:R] = dr_ref[...] + drb_ref[...]
        dps_ref[:, R:2 * R] = dk
        dps_ref[:, 2 * R:3 * R] = dv_ref[...] + dvb_ref[...]
        dps_ref[:, 3 * R:3 * R + LORA_PAD] = dlin
        dps_ref[:, 3 * R + LORA_PAD:] = jnp.zeros((tm, QW - 3 * R - LORA_PAD), F32)

        @pl.when(i == n - 1)
        def _():
            for j in range(4):
                sums_ref[j:j + 1, :] = jnp.sum(acc_ref[j], axis=0, keepdims=True)

    vec = _vec_spec(1, R)
    return pl.pallas_call(
        body, name=name, grid=(n,),
        in_specs=[_row_spec(tm, QW), _prev_rows_spec(tm, QW), _vec_spec(1, QW), _vec_spec(LORA_PAD, 3 * R), vec, vec,
                  vec, vec] + [_row_spec(tm, R)] * 10,
        out_specs=(_row_spec(tm, QW), _vec_spec(LORA_PAD, 3 * R), _vec_spec(4, R)),
        out_shape=(jax.ShapeDtypeStruct((S, QW), F32), jax.ShapeDtypeStruct((LORA_PAD, 3 * R), F32),
                   jax.ShapeDtypeStruct((4, R), F32)),
        scratch_shapes=[pltpu.VMEM((4, 8, R), F32)],
        compiler_params=_cp(("arbitrary",), 56),
    )(q, q, mu, wl, w0, a0, kkw, kaw, *grads)


def _tshift_bwd(dps, q, mu, *, name):
    S, QW = q.shape
    tm = ROW_TILE // 2
    n = S // tm

    def body(d_ref, dn_ref, q_ref, qp_ref, mu_ref, dq_ref, dmu_ref, acc_ref):
        i = pl.program_id(0)

        @pl.when(i == 0)
        def _():
            acc_ref[...] = jnp.zeros_like(acc_ref)

        mu = mu_ref[...]
        d = d_ref[...]
        qv = q_ref[...]
        row = lax.broadcasted_iota(jnp.int32, d.shape, 0)
        first = jnp.where(i > 0, 1.0, 0.0)
        notlast = jnp.where(i < n - 1, 1.0, 0.0)
        prev = jnp.where(row == 0, qp_ref[7:8, :] * first, pltpu.roll(qv, 1, 0))
        z = d * mu
        nxt = jnp.where(row == tm - 1, dn_ref[0:1, :] * mu * notlast, pltpu.roll(z, tm - 1, 0))
        dq_ref[...] = (d * (1.0 - mu) + nxt).astype(BF16)
        acc_ref[...] += _fold8(d * (prev - qv))

        @pl.when(i == n - 1)
        def _():
            dmu_ref[...] = jnp.sum(acc_ref[...], axis=0, keepdims=True)

    nblk8 = S // 8
    next_spec = pl.BlockSpec((8, QW), lambda i: (jnp.minimum((i + 1) * (tm // 8), nblk8 - 1), 0))
    return pl.pallas_call(
        body, name=name, grid=(n,),
        in_specs=[_row_spec(tm, QW), next_spec, _row_spec(tm, QW), _prev_rows_spec(tm, QW), _vec_spec(1, QW)],
        out_specs=(_row_spec(tm, QW), _vec_spec(1, QW)),
        out_shape=(jax.ShapeDtypeStruct((S, QW), BF16), jax.ShapeDtypeStruct((1, QW), F32)),
        scratch_shapes=[pltpu.VMEM((8, QW), F32)],
        compiler_params=_cp(("arbitrary",), 48),
    )(dps, dps, q, q, mu)


def _post_common(ysc, r, k2, v, lnw, lnb, rk):
    bd = _block_ones()
    mean = _segsum(ysc, bd) * (1.0 / HEAD)
    d = ysc - mean
    var = _segsum(d * d, bd) * (1.0 / HEAD)
    rstd = lax.rsqrt(var + LN_X_EPS)
    yh = d * rstd
    rkk = _segsum(r * k2 * rk, bd)
    z = yh * lnw + lnb + rkk * v
    return bd, rstd, yh, rkk, z


def _rwkv_post(ysc, r, k2, v, g, ypool, lnw, lnb, rk, *, name):
    S, R = ysc.shape
    PW = ypool.shape[1]
    tm = ROW_TILE

    def body(y_ref, r_ref, k_ref, v_ref, g_ref, yp_ref, lw_ref, lb_ref, rk_ref, cat_ref):
        _, _, _, _, z = _post_common(y_ref[...], r_ref[...], k_ref[...], v_ref[...], lw_ref[...], lb_ref[...],
                                     rk_ref[...])
        cat_ref[:, 0:PW] = yp_ref[...].astype(BF16)
        cat_ref[:, PW:] = (z * g_ref[...]).astype(BF16)

    vec = _vec_spec(1, R)
    return pl.pallas_call(
        body, name=name, grid=(S // tm,),
        in_specs=[_row_spec(tm, R)] * 5 + [_row_spec(tm, PW), vec, vec, vec],
        out_specs=_row_spec(tm, PW + R), out_shape=jax.ShapeDtypeStruct((S, PW + R), BF16),
        compiler_params=_cp(("parallel",), 48),
    )(ysc, r, k2, v, g, ypool, lnw, lnb, rk)


def _rwkv_post_bwd(dcat, ysc, r, k2, v, g, lnw, lnb, rk, *, name):
    S, R = ysc.shape
    tm = ROW_TILE
    n = S // tm

    def body(d_ref, y_ref, r_ref, k_ref, v_ref, g_ref, lw_ref, lb_ref, rk_ref, dy_ref, dg_ref, drb_ref, dkb_ref,
             dvb_ref, sums_ref, acc_ref):
        i = pl.program_id(0)

        @pl.when(i == 0)
        def _():
            acc_ref[...] = jnp.zeros_like(acc_ref)

        rv, kv, vv, lw, rkw = r_ref[...], k_ref[...], v_ref[...], lw_ref[...], rk_ref[...]
        bd, rstd, yh, rkk, z = _post_common(y_ref[...], rv, kv, vv, lw, lb_ref[...], rkw)
        dyr = d_ref[...]
        dg_ref[...] = dyr * z
        dz = dyr * g_ref[...]
        dyh = dz * lw
        dy_ref[...] = rstd * (dyh - _segsum(dyh, bd) * (1.0 / HEAD) - yh * (_segsum(dyh * yh, bd) * (1.0 / HEAD)))
        dvb_ref[...] = dz * rkk
        drkk = _segsum(dz * vv, bd)
        drb_ref[...] = drkk * kv * rkw
        dkb_ref[...] = drkk * rv * rkw
        acc_ref[0] += _fold8(dz * yh)
        acc_ref[1] += _fold8(dz)
        acc_ref[2] += _fold8(drkk * rv * kv)

        @pl.when(i == n - 1)
        def _():
            for j in range(3):
                sums_ref[j:j + 1, :] = jnp.sum(acc_ref[j], axis=0, keepdims=True)

    vec = _vec_spec(1, R)
    dspec = _row_spec(tm, R)
    return pl.pallas_call(
        body, name=name, grid=(n,),
        in_specs=[dspec] + [_row_spec(tm, R)] * 5 + [vec, vec, vec],
        out_specs=tuple([_row_spec(tm, R)] * 5) + (_vec_spec(3, R),),
        out_shape=tuple([jax.ShapeDtypeStruct((S, R), F32)] * 5) + (jax.ShapeDtypeStruct((3, R), F32),),
        scratch_shapes=[pltpu.VMEM((3, 8, R), F32)],
        compiler_params=_cp(("arbitrary",), 56),
    )(dcat, ysc, r, k2, v, g, lnw, lnb, rk)


def _half_sums(x, m_a):
    s_a = jnp.sum(jnp.where(m_a, x, 0.0), axis=1, keepdims=True)
    s_b = jnp.sum(jnp.where(m_a, 0.0, x), axis=1, keepdims=True)
    return s_a, s_b


def _scan_fwd(r, w, k, v, kap, b, *, name):
    S, R = r.shape
    G, T = SCAN_G, SCAN_T
    NP = R // 128
    assert NP % G == 0 and S % T == 0
    GW = 128 * G

    def body(r_ref, w_ref, k_ref, v_ref, kap_ref, b_ref, y_ref, st_ref, s_scr, vt_scr, yt_scr):
        c = pl.program_id(1)

        @pl.when(c == 0)
        def _():
            s_scr[...] = jnp.zeros_like(s_scr)

        for g in range(G):
            vt_scr[g] = v_ref[:, g * 128:(g + 1) * 128].T
        yt_scr[...] = jnp.zeros_like(yt_scr)
        lane = lax.broadcasted_iota(jnp.int32, (HEAD, 128), 1)
        m_a = lane < HEAD

        def block(tb, carry):
            t0 = pl.multiple_of(tb * 8, 8)
            rb, wb, kb = r_ref[pl.ds(t0, 8), :], w_ref[pl.ds(t0, 8), :], k_ref[pl.ds(t0, 8), :]
            pb, bb = kap_ref[pl.ds(t0, 8), :], b_ref[pl.ds(t0, 8), :]
            for j in range(8):
                t = t0 + j
                hot = lane == t
                for g in range(G):
                    sl = slice(g * 128, (g + 1) * 128)
                    st = s_scr[g]
                    sa_a, sa_b = _half_sums(st * pb[j:j + 1, sl], m_a)
                    sa = -jnp.where(m_a, sa_a, sa_b)
                    v_a = jnp.sum(jnp.where(hot, vt_scr[g, 0:HEAD, :], 0.0), axis=1, keepdims=True)
                    v_b = jnp.sum(jnp.where(hot, vt_scr[g, HEAD:, :], 0.0), axis=1, keepdims=True)
                    vcol = jnp.where(m_a, v_a, v_b)
                    st = st * wb[j:j + 1, sl] + sa * bb[j:j + 1, sl] + vcol * kb[j:j + 1, sl]
                    s_scr[g] = st
                    st_ref[g, t] = st
                    y_a, y_b = _half_sums(st * rb[j:j + 1, sl], m_a)
                    yt_scr[g, 0:HEAD, :] = jnp.where(hot, y_a, yt_scr[g, 0:HEAD, :])
                    yt_scr[g, HEAD:, :] = jnp.where(hot, y_b, yt_scr[g, HEAD:, :])
            return carry

        lax.fori_loop(0, T // 8, block, 0)
        for g in range(G):
            y_ref[:, g * 128:(g + 1) * 128] = yt_scr[g].T

    tspec = pl.BlockSpec((T, GW), lambda p, c: (c, p))
    return pl.pallas_call(
        body, name=name, grid=(NP // G, S // T),
        in_specs=[tspec] * 6,
        out_specs=(tspec, pl.BlockSpec((G, T, HEAD, 128), lambda p, c: (p, c, 0, 0))),
        out_shape=(jax.ShapeDtypeStruct((S, R), F32), jax.ShapeDtypeStruct((NP, S, HEAD, 128), F32)),
        scratch_shapes=[pltpu.VMEM((G, HEAD, 128), F32), pltpu.VMEM((G, 128, 128), F32),
                        pltpu.VMEM((G, 128, 128), F32)],
        compiler_params=_cp(("parallel", "arbitrary"), 48),
    )(r, w, k, v, kap, b)


def _scan_bwd(r, w, k, v, kap, b, dy, states, *, name):
    S, R = r.shape
    G, T = SCAN_G, SCAN_T
    NP = R // 128
    NC = S // T
    GW = 128 * G

    def body(r_ref, w_ref, k_ref, v_ref, kap_ref, b_ref, dy_ref, st_ref, sp_ref, dr_ref, dw_ref, dk_ref, dv_ref,
             dkap_ref, db_ref, ds_scr, vt_scr, dyt_scr, dvt_scr):
        ci = pl.program_id(1)

        @pl.when(ci == 0)
        def _():
            ds_scr[...] = jnp.zeros_like(ds_scr)

        for g in range(G):
            vt_scr[g] = v_ref[:, g * 128:(g + 1) * 128].T
            dyt_scr[g] = dy_ref[:, g * 128:(g + 1) * 128].T
        dvt_scr[...] = jnp.zeros_like(dvt_scr)
        lane = lax.broadcasted_iota(jnp.int32, (HEAD, 128), 1)
        m_a = lane < HEAD
        sub = lax.broadcasted_iota(jnp.int32, (8, 128), 0)
        zero_i = jnp.zeros((HEAD, 128), jnp.int32)
        has_prev = jnp.where(ci < NC - 1, 1.0, 0.0)

        def block(it, carry):
            tb = T // 8 - 1 - it
            t0 = pl.multiple_of(tb * 8, 8)
            rb, wb, kb = r_ref[pl.ds(t0, 8), :], w_ref[pl.ds(t0, 8), :], k_ref[pl.ds(t0, 8), :]
            pb, bb = kap_ref[pl.ds(t0, 8), :], b_ref[pl.ds(t0, 8), :]
            outs = [[jnp.zeros((8, 128), F32) for _ in range(5)] for _ in range(G)]
            for j in range(7, -1, -1):
                t = t0 + j
                hot = lane == t
                at_start = (zero_i + t) == 0
                for g in range(G):
                    sl = slice(g * 128, (g + 1) * 128)
                    s_t = st_ref[g, t]
                    s_p = jnp.where(at_start, sp_ref[g, 0] * has_prev, st_ref[g, jnp.maximum(t - 1, 0)])
                    r_r, w_r, k_r, p_r, b_r = (x[j:j + 1, sl] for x in (rb, wb, kb, pb, bb))
                    dy_a = jnp.sum(jnp.where(hot, dyt_scr[g, 0:HEAD, :], 0.0), axis=1, keepdims=True)
                    dy_b = jnp.sum(jnp.where(hot, dyt_scr[g, HEAD:, :], 0.0), axis=1, keepdims=True)
                    dycol = jnp.where(m_a, dy_a, dy_b)
                    v_a = jnp.sum(jnp.where(hot, vt_scr[g, 0:HEAD, :], 0.0), axis=1, keepdims=True)
                    v_b = jnp.sum(jnp.where(hot, vt_scr[g, HEAD:, :], 0.0), axis=1, keepdims=True)
                    vcol = jnp.where(m_a, v_a, v_b)
                    sa_a, sa_b = _half_sums(s_p * p_r, m_a)
                    sa = -jnp.where(m_a, sa_a, sa_b)
                    ds = ds_scr[g] + dycol * r_r
                    dr_row = jnp.sum(s_t * dycol, axis=0, keepdims=True)
                    dv_a, dv_b = _half_sums(ds * k_r, m_a)
                    dvt_scr[g, 0:HEAD, :] = jnp.where(hot, dv_a, dvt_scr[g, 0:HEAD, :])
                    dvt_scr[g, HEAD:, :] = jnp.where(hot, dv_b, dvt_scr[g, HEAD:, :])
                    dk_row = jnp.sum(ds * vcol, axis=0, keepdims=True)
                    dsa_a, dsa_b = _half_sums(ds * b_r, m_a)
                    dsa = jnp.where(m_a, dsa_a, dsa_b)
                    db_row = jnp.sum(ds * sa, axis=0, keepdims=True)
                    dw_row = jnp.sum(ds * s_p, axis=0, keepdims=True)
                    dkap_row = -jnp.sum(s_p * dsa, axis=0, keepdims=True)
                    ds_scr[g] = ds * w_r - dsa * p_r
                    pick = sub == j
                    for q, row in enumerate((dr_row, dw_row, dk_row, dkap_row, db_row)):
                        outs[g][q] = jnp.where(pick, row, outs[g][q])
            for g in range(G):
                sl = slice(g * 128, (g + 1) * 128)
                for q, ref in enumerate((dr_ref, dw_ref, dk_ref, dkap_ref, db_ref)):
                    ref[pl.ds(t0, 8), sl] = outs[g][q]
            return carry

        lax.fori_loop(0, T // 8, block, 0)
        for g in range(G):
            dv_ref[:, g * 128:(g + 1) * 128] = dvt_scr[g].T

    tspec = pl.BlockSpec((T, GW), lambda p, c: (NC - 1 - c, p))
    st_spec = pl.BlockSpec((G, T, HEAD, 128), lambda p, c: (p, NC - 1 - c, 0, 0))
    prev_spec = pl.BlockSpec((G, 1, HEAD, 128), lambda p, c: (p, jnp.maximum((NC - 1 - c) * T - 1, 0), 0, 0))
    return pl.pallas_call(
        body, name=name, grid=(NP // G, NC),
        in_specs=[tspec] * 7 + [st_spec, prev_spec],
        out_specs=tuple([tspec] * 6),
        out_shape=tuple([jax.ShapeDtypeStruct((S, R), F32)] * 6),
        scratch_shapes=[pltpu.VMEM((G, HEAD, 128), F32), pltpu.VMEM((G, 128, 128), F32),
                        pltpu.VMEM((G, 128, 128), F32), pltpu.VMEM((G, 128, 128), F32)],
        compiler_params=_cp(("parallel", "arbitrary"), 48),
    )(r, w, k, v, kap, b, dy, states, states)


def _sum_parts(parts, *, name):
    P, rows, W = parts.shape
    tr = rows
    for cand in (1024, 512, 256, 128, 64, 32, 16, 8):
        if rows % cand == 0:
            tr = cand
            break

    def body(p_ref, o_ref):
        acc = p_ref[0]
        for s in range(1, P):
            acc = acc + p_ref[s]
        o_ref[...] = acc

    return pl.pallas_call(
        body, name=name, grid=(rows // tr,),
        in_specs=[pl.BlockSpec((P, tr, W), lambda i: (0, i, 0))],
        out_specs=pl.BlockSpec((tr, W), lambda i: (i, 0)), out_shape=jax.ShapeDtypeStruct((rows, W), F32),
        compiler_params=_cp(("parallel",), 32),
    )(parts)


def _adamw(w, m, v, parts, *, name):
    R, C = w.shape
    P = parts.shape[0]
    tr = R
    for cand in (1024, 512, 256, 128, 64, 32, 16, 8):
        if R % cand == 0 and cand * C * 4 * (7 + P) <= 10 * 1024 * 1024:
            tr = cand
            break
    bc1 = 1.0 - ADAM_B1 ** ADAM_STEP
    bc2 = 1.0 - ADAM_B2 ** ADAM_STEP

    def body(w_ref, m_ref, v_ref, p_ref, g_ref, d_ref, nm_ref, nv_ref):
        g = p_ref[0]
        for s in range(1, P):
            g = g + p_ref[s]
        m1 = ADAM_B1 * m_ref[...] + (1.0 - ADAM_B1) * g
        v1 = ADAM_B2 * v_ref[...] + (1.0 - ADAM_B2) * (g * g)
        m_hat = m1 / bc1
        v_hat = v1 / bc2
        g_ref[...] = g
        d_ref[...] = -ADAM_LR * (m_hat / (jnp.sqrt(v_hat) + ADAM_EPS) + ADAM_WD * w_ref[...])
        nm_ref[...] = m1
        nv_ref[...] = v1

    spec = pl.BlockSpec((tr, C), lambda i: (i, 0))
    return pl.pallas_call(
        body, name=name, grid=(R // tr,),
        in_specs=[spec, spec, spec, pl.BlockSpec((P, tr, C), lambda i: (0, i, 0))],
        out_specs=(spec, spec, spec, spec), out_shape=tuple([jax.ShapeDtypeStruct((R, C), F32)] * 4),
        compiler_params=_cp(("parallel",), 40),
    )(w, m, v, parts)


def _cols_full(g8):
    n, rows, c = g8.shape
    return jnp.transpose(g8, (1, 0, 2)).reshape(rows, n * c)


def _cols_split(full):
    rows, cols = full.shape
    return jnp.transpose(full.reshape(rows, N_DEV, cols // N_DEV), (1, 0, 2))


def _interleave(gate, up):
    D, F = gate.shape
    nj = F // FF_TILE
    return jnp.stack([gate.reshape(D, nj, FF_TILE), up.reshape(D, nj, FF_TILE)], axis=2).reshape(D, 2 * F)


def _deinterleave(gu):
    D, F2 = gu.shape
    nj = F2 // (2 * FF_TILE)
    t = gu.reshape(D, nj, 2, FF_TILE)
    return t[:, :, 0, :].reshape(D, F2 // 2), t[:, :, 1, :].reshape(D, F2 // 2)


def _pack(vals, rows_multiple=512):
    flat = jnp.concatenate([v.reshape(-1).astype(F32) for v in vals])
    n = flat.shape[0]
    unit = 128 * rows_multiple
    padded = ((n + unit - 1) // unit) * unit
    return jnp.pad(flat, (0, padded - n)).reshape(padded // 128, 128)


def _unpack(packed, shapes):
    flat = packed.reshape(-1)
    out, off = [], 0
    for shp in shapes:
        size = 1
        for d in shp:
            size *= d
        out.append(flat[off:off + size].reshape(shp))
        off += size
    return out


def _ffn_forward(x, wgu, wd, gpre, gpost, shift, scale1p, gw, tag):
    h = _pre_norm_mod(x, gpre, shift, scale1p, name=f"{tag}_pre")
    au, s = _ffn_up(h, wgu, tm=1024, tk=512, name=f"{tag}_up")
    f = _mm(s, wd, tm=1024, tn=1024, tk=512, name=f"{tag}_down")
    xo = _post_norm_res(x, f, gpost, gw, name=f"{tag}_post")
    return xo, (h, au, s, f)


def _ffn_backward(dxo, x, saved, wgu, wd, gpre, gpost, scale1p, gw, tag):
    h, au, s, f = saved
    df, post_sums = _post_norm_res_bwd(dxo, f, gpost, gw, MACARON, name=f"{tag}_post_bwd")
    dwd = _mm(s, df, ta=True, tm=512, tn=2048, tk=512, name=f"{tag}_dwd")
    dau = _ffn_down_bwd(df, wd, au, tm=1024, tk=512, name=f"{tag}_down_bwd")
    dwgu = _mm(h, dau, ta=True, tm=1024, tn=1024, tk=512, name=f"{tag}_dwgu")
    dh = _mm(dau, wgu, tb=True, tm=1024, tn=1024, tk=512, name=f"{tag}_dh")
    dx, pre_sums = _pre_norm_mod_bwd(dh, x, dxo, gpre, scale1p, name=f"{tag}_pre_bwd")
    return dx, dwgu, dwd, pre_sums, post_sums


def kernel(x, c, w_ada, b_ada, norm_pre, norm_post, ffn1_w_gate, ffn1_w_up, ffn1_w_down, w_in, mu_shift, pool_w, pool_scale, w0, w2, a0, a2, g2, k_k, k_a, r_k, lnx_w, lnx_b, w_out, ffn2_w_gate, ffn2_w_up, ffn2_w_down, loss_target, m_w_ada, m_b_ada, m_norm_pre, m_norm_post, m_ffn1_w_gate, m_ffn1_w_up, m_ffn1_w_down, m_w_in, m_mu_shift, m_pool_w, m_pool_scale, m_w0, m_w2, m_a0, m_a2, m_g2, m_k_k, m_k_a, m_r_k, m_lnx_w, m_lnx_b, m_w_out, m_ffn2_w_gate, m_ffn2_w_up, m_ffn2_w_down, v_w_ada, v_b_ada, v_norm_pre, v_norm_post, v_ffn1_w_gate, v_ffn1_w_up, v_ffn1_w_down, v_w_in, v_mu_shift, v_pool_w, v_pool_scale, v_w0, v_w2, v_a0, v_a2, v_g2, v_k_k, v_k_a, v_r_k, v_lnx_w, v_lnx_b, v_w_out, v_ffn2_w_gate, v_ffn2_w_up, v_ffn2_w_down):
    names = ["w_ada", "b_ada", "norm_pre", "norm_post", "ffn1_w_gate", "ffn1_w_up", "ffn1_w_down", "w_in", "mu_shift",
             "pool_w", "pool_scale", "w0", "w2", "a0", "a2", "g2", "k_k", "k_a", "r_k", "lnx_w", "lnx_b", "w_out",
             "ffn2_w_gate", "ffn2_w_up", "ffn2_w_down"]
    env = dict(locals())
    W = {n: env[n][0] for n in names}
    M1 = {n: env["m_" + n][0] for n in names}
    V1 = {n: env["v_" + n][0] for n in names}

    me = _my_index()
    xs = x[0]
    tgt = loss_target[0]
    S, D = xs.shape
    F = W["ffn1_w_gate"].shape[1] * N_DEV
    R = W["w0"].shape[0]
    PW = D - R
    IN_W = W["w_in"].shape[1] * N_DEV
    P_W = F
    QW = P_W - PW
    NMOD = 9 * D
    ada_c = W["w_ada"].shape[1]

    c_all, npre8, npost8, w2_8, a2_8, g2_8 = _exchange(
        [c, W["norm_pre"], W["norm_post"], W["w2"].astype(BF16), W["a2"].astype(BF16), W["g2"].astype(BF16)],
        scatter=False, name="gather_small")
    c_all = c_all.reshape(N_DEV, D)
    gpre = _cols_full(npre8)
    gpost = _cols_full(npost8)
    wl = jnp.zeros((LORA_PAD, 3 * R), BF16)
    wl = wl.at[0:LORA_W, 0:R].set(_cols_full(w2_8))
    wl = wl.at[LORA_W:LORA_W + LORA_A, R:2 * R].set(_cols_full(a2_8))
    wl = wl.at[LORA_W + LORA_A:LORA_W + LORA_A + LORA_G, 2 * R:3 * R].set(_cols_full(g2_8))

    sc_all = jax.nn.silu(c_all)
    sc_pad = jnp.concatenate([sc_all, jnp.zeros((8, D), F32)], axis=0).astype(BF16)
    modcols = _mm(sc_pad, W["w_ada"], tm=16, tn=ada_c, tk=256, name="ada_fwd")[0:N_DEV]
    modcols = modcols + lax.dynamic_slice(W["b_ada"], (me * ada_c,), (ada_c,))[None, :]
    (mod8,) = _exchange([modcols], scatter=False, name="gather_mod")
    mod = lax.dynamic_index_in_dim(mod8, me, axis=1, keepdims=False).reshape(9, D)

    def mod_row(i):
        return mod[i:i + 1, :]

    def gather_ffn(tag, name):
        g8, u8, d8 = _exchange([W[f"{tag}_w_gate"].astype(BF16), W[f"{tag}_w_up"].astype(BF16),
                                W[f"{tag}_w_down"].astype(BF16)], scatter=False, name=name)
        return _interleave(_cols_full(g8), _cols_full(u8)), d8.reshape(F, D)

    wgu1, wd1 = gather_ffn("ffn1", "gather_ffn")
    win8, wout8 = _exchange([W["w_in"].astype(BF16), W["w_out"].astype(BF16)], scatter=False, name="gather_mixer")
    w_in_p = jnp.pad(_cols_full(win8), ((0, 0), (0, P_W - IN_W)))
    w_out_f = wout8.reshape(D, D)
    wgu2, wd2 = gather_ffn("ffn2", "gather_ffn")

    mu_p = jnp.pad(W["mu_shift"], (0, QW - W["mu_shift"].shape[0]))[None, :]
    vec = lambda a: a.reshape(1, -1)
    w0r, a0r, kkr, kar = vec(W["w0"]), vec(W["a0"]), vec(W["k_k"]), vec(W["k_a"])
    lnw, lnb, rkr = vec(W["lnx_w"]), vec(W["lnx_b"]), vec(W["r_k"])
    pscale = vec(W["pool_scale"])

    sc1p = [1.0 + mod_row(3 * s + 1) for s in range(3)]
    shifts = [mod_row(3 * s) for s in range(3)]
    wgts = [MACARON, 1.0, MACARON]
    gws = [wgts[s] * (1.0 + mod_row(3 * s + 2)) for s in range(3)]
    gp = [gpre[s:s + 1] for s in range(3)]
    gq = [gpost[s:s + 1] for s in range(3)]

    x1, sv1 = _ffn_forward(xs, wgu1, wd1, gp[0], gq[0], shifts[0], sc1p[0], gws[0], "ffn")

    h2 = _pre_norm_mod(x1, gp[1], shifts[1], sc1p[1], name="mix_pre")
    p = _mm(h2, w_in_p, tm=1024, tn=512, tk=512, name="mix_in")
    q = p[:, PW:]
    o_pool, y_pool = _pool_fwd(p, W["pool_w"], pscale, name="pool_fwd")
    r_s, w_s, k_s, v_s, kap_s, b_s, g_s = _rwkv_prep(q, mu_p, wl, w0r, a0r, kkr, kar, name="rwkv_prep")
    y_scan, states = _scan_fwd(r_s, w_s, k_s, v_s, kap_s, b_s, name="scan_fwd")
    cat = _rwkv_post(y_scan, r_s, k_s, v_s, g_s, y_pool, lnw, lnb, rkr, name="rwkv_post")
    f2 = _mm(cat, w_out_f, tm=1024, tn=1024, tk=512, name="mix_out")
    x2 = _post_norm_res(x1, f2, gq[1], gws[1], name="mix_post")

    x3, sv3 = _ffn_forward(x2, wgu2, wd2, gp[2], gq[2], shifts[2], sc1p[2], gws[2], "ffn")

    loss_part, dx3 = _loss_head(x3, tgt, name="loss_head")
    loss = lax.psum(loss_part[0, 0], MESH_AXES)

    dx2, dwgu2, dwd2, pre3, post3 = _ffn_backward(dx3, x2, sv3, wgu2, wd2, gp[2], gq[2], sc1p[2], gws[2], "ffn")

    df2, post2 = _post_norm_res_bwd(dx2, f2, gq[1], gws[1], 1.0, name="mix_post_bwd")
    dw_out = _mm(cat, df2, ta=True, tm=1024, tn=1024, tk=512, name="mix_dwout")
    dcat = _mm(df2, w_out_f, tb=True, tm=1024, tn=1024, tk=512, name="mix_dcat")
    dyr = dcat[:, PW:]
    dysc, dg, dr_b, dk2_b, dv_b, post_sums = _rwkv_post_bwd(dyr, y_scan, r_s, k_s, v_s, g_s, lnw, lnb, rkr,
                                                             name="rwkv_post_bwd")
    dr, dw, dk2, dv, dkap, db = _scan_bwd(r_s, w_s, k_s, v_s, kap_s, b_s, dysc, states, name="scan_bwd")
    dps, dwl, prep_sums = _rwkv_prep_bwd(q, mu_p, wl, w0r, a0r, kkr, kar,
                                         (dr, dw, dk2, dv, dkap, db, dg, dr_b, dk2_b, dv_b), name="rwkv_prep_bwd")
    dq, dmu = _tshift_bwd(dps, q, mu_p, name="tshift_bwd")
    du_pool, dpool_w, dpool_scale = _pool_bwd(dcat, o_pool, W["pool_w"], pscale, name="pool_bwd")
    dp = jnp.concatenate([du_pool, dq], axis=1)
    dw_in = _mm(h2, dp, ta=True, tm=1024, tn=512, tk=512, name="mix_dwin")
    dh2 = _mm(dp, w_in_p, tb=True, tm=1024, tn=1024, tk=512, name="mix_dh")
    dx1, pre2 = _pre_norm_mod_bwd(dh2, x1, dx2, gp[1], sc1p[1], name="mix_pre_bwd")

    dx0, dwgu1, dwd1, pre1, post1 = _ffn_backward(dx1, xs, sv1, wgu1, wd1, gp[0], gq[0], sc1p[0], gws[0], "ffn")

    pres, posts = [pre1, pre2, pre3], [post1, post2, post3]
    dmod = jnp.stack([jnp.stack([pres[s][0], pres[s][1], posts[s][0]]) for s in range(3)]).reshape(NMOD // 128, 128)
    dnorm_pre = jnp.stack([pres[s][2] for s in range(3)])
    dnorm_post = jnp.stack([posts[s][1] for s in range(3)])

    small = [dmu[0, :W["mu_shift"].shape[0]], dpool_w, dpool_scale, prep_sums[0], prep_sums[1], prep_sums[2],
             prep_sums[3], post_sums[2], post_sums[0], post_sums[1], dnorm_pre, dnorm_post,
             dwl[0:LORA_W, 0:R], dwl[LORA_W:LORA_W + LORA_A, R:2 * R],
             dwl[LORA_W + LORA_A:LORA_W + LORA_A + LORA_G, 2 * R:3 * R]]
    small_shapes = [a.shape for a in small]
    dmod8, small8 = _exchange([dmod, _pack(small)], scatter=False, name="gather_grads")
    g_b_ada = _sum_parts(dmod8, name="sum_dmod").reshape(NMOD)
    red = _unpack(_sum_parts(small8, name="sum_small"), small_shapes)
    (g_mu, g_pool_w, g_pool_scale, g_w0, g_a0, g_kk, g_ka, g_rk, g_lnw, g_lnb, g_npre, g_npost, g_w2, g_a2,
     g_g2) = red

    dmod_all = dmod8.reshape(N_DEV, NMOD)
    dmod_cols = lax.dynamic_slice(dmod_all, (0, me * ada_c), (N_DEV, ada_c))
    dmod_cols = jnp.concatenate([dmod_cols, jnp.zeros_like(dmod_cols)], axis=0)
    g_w_ada = _mm(sc_pad, dmod_cols, ta=True, tm=D, tn=ada_c // 9, tk=16, name="ada_bwd")

    def scatter_ffn(dwgu, dwd, name):
        dgate, dup = _deinterleave(dwgu)
        return _exchange([_cols_split(dgate), _cols_split(dup), dwd.reshape(N_DEV, F // N_DEV, D)], scatter=True,
                         name=name)

    pg2, pu2, pd2 = scatter_ffn(dwgu2, dwd2, "scatter_ffn")
    pin, pout = _exchange([_cols_split(dw_in[:, :IN_W]), dw_out.reshape(N_DEV, D // N_DEV, D)], scatter=True,
                          name="scatter_mixer")
    pg1, pu1, pd1 = scatter_ffn(dwgu1, dwd1, "scatter_ffn")

    res = {}

    def big(nm, parts, tag):
        res[nm] = _adamw(W[nm], M1[nm], V1[nm], parts, name=tag)

    big("ffn1_w_gate", pg1, "adamw_cols")
    big("ffn1_w_up", pu1, "adamw_cols")
    big("ffn1_w_down", pd1, "adamw_rows")
    big("ffn2_w_gate", pg2, "adamw_cols")
    big("ffn2_w_up", pu2, "adamw_cols")
    big("ffn2_w_down", pd2, "adamw_rows")
    big("w_in", pin, "adamw_w_in")
    big("w_out", pout, "adamw_w_out")
    big("w_ada", g_w_ada[None], "adamw_w_ada")

    def my_cols(full, width):
        return lax.dynamic_slice_in_dim(full, me * width, width, axis=full.ndim - 1)

    small_names = ["b_ada", "mu_shift", "pool_w", "pool_scale", "w0", "a0", "k_k", "k_a", "r_k", "lnx_w", "lnx_b",
                   "norm_pre", "norm_post", "w2", "a2", "g2"]
    small_grads = [g_b_ada, g_mu, g_pool_w, g_pool_scale, g_w0, g_a0, g_kk, g_ka, g_rk.reshape(W["r_k"].shape), g_lnw,
                   g_lnb, my_cols(g_npre, D // N_DEV), my_cols(g_npost, D // N_DEV), my_cols(g_w2, R // N_DEV),
                   my_cols(g_a2, R // N_DEV), my_cols(g_g2, R // N_DEV)]
    shapes = [W[n].shape for n in small_names]
    packed = _adamw(_pack([W[n] for n in small_names]), _pack([M1[n] for n in small_names]),
                    _pack([V1[n] for n in small_names]), _pack(small_grads)[None], name="adamw_small")
    unpacked = [_unpack(t, shapes) for t in packed]
    for i, nm in enumerate(small_names):
        res[nm] = tuple(unpacked[k][i] for k in range(4))

    outs = [loss, dx0[None]]
    for k in range(4):
        outs.extend(res[nm][k][None] for nm in names)
    return tuple(outs)
```

```python
import functools

import jax
import jax.numpy as jnp
from jax import lax
from jax.experimental import pallas as pl
from jax.experimental.pallas import tpu as pltpu

F32 = jnp.float32
BF16 = jnp.bfloat16
N_DEV = 8
MESH_AXES = ("x", "y", "c")

NORM_EPS = 1e-6
HEAD = 64
LN_X_EPS = 1e-5 * HEAD
POOL_GROUPS = 4
POOL_GROUP = 128
MACARON = 0.5
LORA_W, LORA_A, LORA_G = 64, 64, 224
LORA_PAD = 384
ADAM_LR, ADAM_B1, ADAM_B2, ADAM_EPS, ADAM_WD, ADAM_STEP = 0.001, 0.9, 0.999, 1e-08, 0.01, 10

FF_TILE = 512
ROW_TILE = 256
SCAN_T = 128
SCAN_G = 3
VMEM_CAP = 56 * 1024 * 1024


def _cp(sem, vmem_mb):
    return pltpu.CompilerParams(dimension_semantics=sem, vmem_limit_bytes=min(vmem_mb * 1024 * 1024, VMEM_CAP))


def _my_index():
    return 4 * lax.axis_index("x") + 2 * lax.axis_index("y") + lax.axis_index("c")


def _exchange(arrays, *, scatter, name):
    n = len(arrays)
    out_shapes = []
    for a in arrays:
        shp = a.shape if scatter else (N_DEV,) + a.shape
        out_shapes.append(jax.ShapeDtypeStruct(shp, a.dtype))

    def body(*refs):
        ins, outs = refs[:n], refs[n:2 * n]
        send_sems, recv_sems, local_sems = refs[2 * n:]
        me = _my_index()

        def dev(p):
            return (p // 4, (p // 2) % 2, p % 2)

        def copy(i, d):
            peer = (me + d) % N_DEV
            src = ins[i].at[peer] if scatter else ins[i]
            return pltpu.make_async_remote_copy(
                src_ref=src, dst_ref=outs[i].at[me], send_sem=send_sems.at[i, d - 1],
                recv_sem=recv_sems.at[i, d - 1], device_id=dev(peer), device_id_type=pl.DeviceIdType.MESH)

        def arrival(i, d):
            frm = (me + N_DEV - d) % N_DEV
            src = ins[i].at[frm] if scatter else ins[i]
            return pltpu.make_async_remote_copy(
                src_ref=src, dst_ref=outs[i].at[frm], send_sem=send_sems.at[i, d - 1],
                recv_sem=recv_sems.at[i, d - 1], device_id=dev(frm), device_id_type=pl.DeviceIdType.MESH)

        locals_ = []
        for i in range(n):
            src = ins[i].at[me] if scatter else ins[i]
            lc = pltpu.make_async_copy(src, outs[i].at[me], local_sems.at[i])
            lc.start()
            locals_.append(lc)
        sends = [copy(i, d) for d in range(1, N_DEV) for i in range(n)]
        for cp in sends:
            cp.start()
        for d in range(1, N_DEV):
            for i in range(n):
                arrival(i, d).wait_recv()
        for cp in sends:
            cp.wait_send()
        for lc in locals_:
            lc.wait()

    hbm = pl.BlockSpec(memory_space=pltpu.HBM)
    return pl.pallas_call(
        body, name=name, out_shape=tuple(out_shapes), in_specs=[hbm] * n, out_specs=tuple([hbm] * n),
        scratch_shapes=[pltpu.SemaphoreType.DMA((n, N_DEV - 1)), pltpu.SemaphoreType.DMA((n, N_DEV - 1)),
                        pltpu.SemaphoreType.DMA((n,))],
    )(*arrays)


def _remote(src, dst, send_sem, recv_sem, to):
    return pltpu.make_async_remote_copy(src_ref=src, dst_ref=dst, send_sem=send_sem, recv_sem=recv_sem,
                                        device_id=to, device_id_type=pl.DeviceIdType.MESH)


def _gather_two_level(arrays, *, name):
    n = len(arrays)
    out_shapes = [jax.ShapeDtypeStruct((N_DEV,) + a.shape, a.dtype) for a in arrays]

    def body(*refs):
        ins, outs = refs[:n], refs[n:2 * n]
        send_sems, recv_sems, local_sems = refs[2 * n:]
        x, y, c = lax.axis_index("x"), lax.axis_index("y"), lax.axis_index("c")
        sibling = (x, y, 1 - c)
        chips = [(1 - x, y), (x, 1 - y), (1 - x, 1 - y)]

        def slot(i, px, py, pc):
            return outs[i].at[4 * px + 2 * py + pc]

        def copy(i, k, block, to, src=None):
            dst = slot(i, *block)
            return _remote(dst if src is None else src, dst, send_sems.at[i, k], recv_sems.at[i, k], to)

        locals_ = []
        for i in range(n):
            lc = pltpu.make_async_copy(ins[i], slot(i, x, y, c), local_sems.at[i])
            lc.start()
            locals_.append(lc)
        sends = []
        for j, chip in enumerate(chips):
            for i in range(n):
                sends.append(copy(i, 1 + j, (x, y, c), (*chip, c), src=ins[i]))
        for i in range(n):
            sends.append(copy(i, 0, (x, y, c), sibling, src=ins[i]))
        for cp in sends:
            cp.start()
        for j, chip in enumerate(chips):
            for i in range(n):
                copy(i, 1 + j, (*chip, c), (x, y, c)).wait_recv()
                fwd = copy(i, 4 + j, (*chip, c), sibling)
                fwd.start()
                sends.append(fwd)
        for i in range(n):
            copy(i, 0, (x, y, 1 - c), (x, y, c)).wait_recv()
        for j, chip in enumerate(chips):
            for i in range(n):
                copy(i, 4 + j, (*chip, 1 - c), (x, y, c)).wait_recv()
        for cp in sends:
            cp.wait_send()
        for lc in locals_:
            lc.wait()

    hbm = pl.BlockSpec(memory_space=pltpu.HBM)
    return pl.pallas_call(
        body, name=name, out_shape=tuple(out_shapes), in_specs=[hbm] * n, out_specs=tuple([hbm] * n),
        scratch_shapes=[pltpu.SemaphoreType.DMA((n, 7)), pltpu.SemaphoreType.DMA((n, 7)),
                        pltpu.SemaphoreType.DMA((n,))],
    )(*arrays)


def _sibling_swap(arrays, *, name):
    n = len(arrays)
    out_shapes = [jax.ShapeDtypeStruct(a.shape[1:], a.dtype) for a in arrays]

    def body(*refs):
        ins, outs = refs[:n], refs[n:2 * n]
        send_sems, recv_sems = refs[2 * n:]
        x, y, c = lax.axis_index("x"), lax.axis_index("y"), lax.axis_index("c")
        copies = [_remote(ins[i].at[1 - c], outs[i], send_sems.at[i], recv_sems.at[i], (x, y, 1 - c))
                  for i in range(n)]
        for cp in copies:
            cp.start()
        for cp in copies:
            cp.wait_recv()
        for cp in copies:
            cp.wait_send()

    hbm = pl.BlockSpec(memory_space=pltpu.HBM)
    return pl.pallas_call(
        body, name=name, out_shape=tuple(out_shapes), in_specs=[hbm] * n, out_specs=tuple([hbm] * n),
        scratch_shapes=[pltpu.SemaphoreType.DMA((n,)), pltpu.SemaphoreType.DMA((n,))],
    )(*arrays)


def _chips_all_to_all(arrays, *, name):
    n = len(arrays)
    out_shapes = [jax.ShapeDtypeStruct(a.shape, a.dtype) for a in arrays]

    def body(*refs):
        ins, outs = refs[:n], refs[n:2 * n]
        send_sems, recv_sems, local_sems = refs[2 * n:]
        x, y, c = lax.axis_index("x"), lax.axis_index("y"), lax.axis_index("c")
        mine = 2 * x + y
        chips = [(1 - x, y), (x, 1 - y), (1 - x, 1 - y)]
        locals_ = []
        for i in range(n):
            lc = pltpu.make_async_copy(ins[i].at[mine], outs[i].at[mine], local_sems.at[i])
            lc.start()
            locals_.append(lc)
        sends = [_remote(ins[i].at[2 * chip[0] + chip[1]], outs[i].at[mine], send_sems.at[i, j], recv_sems.at[i, j],
                         (*chip, c)) for j, chip in enumerate(chips) for i in range(n)]
        for cp in sends:
            cp.start()
        for j, chip in enumerate(chips):
            for i in range(n):
                q = 2 * chip[0] + chip[1]
                _remote(ins[i].at[q], outs[i].at[q], send_sems.at[i, j], recv_sems.at[i, j], (*chip, c)).wait_recv()
        for cp in sends:
            cp.wait_send()
        for lc in locals_:
            lc.wait()

    hbm = pl.BlockSpec(memory_space=pltpu.HBM)
    return pl.pallas_call(
        body, name=name, out_shape=tuple(out_shapes), in_specs=[hbm] * n, out_specs=tuple([hbm] * n),
        scratch_shapes=[pltpu.SemaphoreType.DMA((n, 3)), pltpu.SemaphoreType.DMA((n, 3)),
                        pltpu.SemaphoreType.DMA((n,))],
    )(*arrays)


def _pair_add(mine, got, *, name):
    _, nq, R, C = mine.shape
    tr = R
    for cand in (512, 256, 128, 64, 32, 16):
        if R % cand == 0 and cand * C * 2 * 3 * 2 <= 12 * 1024 * 1024:
            tr = cand
            break

    def body(core_ref, m_ref, g_ref, o_ref):
        o_ref[0] = (m_ref[0, 0].astype(F32) + g_ref[0].astype(F32)).astype(BF16)

    core = lax.axis_index("c").astype(jnp.int32).reshape(1)
    return pl.pallas_call(
        body, name=name,
        grid_spec=pltpu.PrefetchScalarGridSpec(
            num_scalar_prefetch=1, grid=(nq, R // tr),
            in_specs=[pl.BlockSpec((1, 1, tr, C), lambda q, i, core_ref: (core_ref[0], q, i, 0)),
                      pl.BlockSpec((1, tr, C), lambda q, i, core_ref: (q, i, 0))],
            out_specs=pl.BlockSpec((1, tr, C), lambda q, i, core_ref: (q, i, 0))),
        out_shape=jax.ShapeDtypeStruct((nq, R, C), BF16),
        compiler_params=_cp(("parallel", "parallel"), 40),
    )(core, mine, got)


def _mm(a, b, *, ta=False, tb=False, tm, tn, tk, out_dtype=F32, name):
    M = a.shape[1] if ta else a.shape[0]
    K = a.shape[0] if ta else a.shape[1]
    N = b.shape[0] if tb else b.shape[1]
    tm, tn, tk = min(tm, M), min(tn, N), min(tk, K)
    assert M % tm == 0 and N % tn == 0 and K % tk == 0, (name, M, N, K, tm, tn, tk)
    nk = K // tk
    dims = (((0 if ta else 1,), (1 if tb else 0,)), ((), ()))

    def body(a_ref, b_ref, o_ref, acc_ref):
        k = pl.program_id(2)

        @pl.when(k == 0)
        def _():
            acc_ref[...] = jnp.zeros_like(acc_ref)

        acc_ref[...] += lax.dot_general(a_ref[...].astype(BF16), b_ref[...].astype(BF16), dims,
                                        preferred_element_type=F32)

        @pl.when(k == nk - 1)
        def _():
            o_ref[...] = acc_ref[...].astype(out_dtype)

    a_spec = pl.BlockSpec((tk, tm), lambda i, j, k: (k, i)) if ta else pl.BlockSpec((tm, tk), lambda i, j, k: (i, k))
    b_spec = pl.BlockSpec((tn, tk), lambda i, j, k: (j, k)) if tb else pl.BlockSpec((tk, tn), lambda i, j, k: (k, j))
    blk = 2 * (tm * tk * a.dtype.itemsize + tk * tn * b.dtype.itemsize + tm * tn * jnp.dtype(out_dtype).itemsize)
    return pl.pallas_call(
        body, name=name, grid=(M // tm, N // tn, nk), in_specs=[a_spec, b_spec],
        out_specs=pl.BlockSpec((tm, tn), lambda i, j, k: (i, j)),
        out_shape=jax.ShapeDtypeStruct((M, N), out_dtype),
        scratch_shapes=[pltpu.VMEM((tm, tn), F32)],
        compiler_params=_cp(("parallel", "parallel", "arbitrary"), (blk + tm * tn * 4) // (1024 * 1024) + 12),
    )(a, b)


def _ffn_up(h, wgu, *, tm, tk, name):
    S, D = h.shape
    F2 = wgu.shape[1]
    tm = min(tm, S)
    tn = FF_TILE
    nk = D // tk

    def body(h_ref, w_ref, au_ref, s_ref, acc_ref):
        k = pl.program_id(2)

        @pl.when(k == 0)
        def _():
            acc_ref[...] = jnp.zeros_like(acc_ref)

        acc_ref[...] += jnp.dot(h_ref[...], w_ref[...], preferred_element_type=F32)

        @pl.when(k == nk - 1)
        def _():
            acc = acc_ref[...]
            a = acc[:, :tn]
            u = acc[:, tn:]
            au_ref[...] = acc.astype(BF16)
            s_ref[...] = (a * jax.nn.sigmoid(a) * u).astype(BF16)

    return pl.pallas_call(
        body, name=name, grid=(S // tm, F2 // (2 * tn), nk),
        in_specs=[pl.BlockSpec((tm, tk), lambda i, j, k: (i, k)), pl.BlockSpec((tk, 2 * tn), lambda i, j, k: (k, j))],
        out_specs=(pl.BlockSpec((tm, 2 * tn), lambda i, j, k: (i, j)), pl.BlockSpec((tm, tn), lambda i, j, k: (i, j))),
        out_shape=(jax.ShapeDtypeStruct((S, F2), BF16), jax.ShapeDtypeStruct((S, F2 // 2), BF16)),
        scratch_shapes=[pltpu.VMEM((tm, 2 * tn), F32)],
        compiler_params=_cp(("parallel", "parallel", "arbitrary"), 40),
    )(h, wgu)


def _ffn_down_bwd(df, wd, au, *, tm, tk, name):
    S, D = df.shape
    F = wd.shape[0]
    tm = min(tm, S)
    tn = FF_TILE
    nk = D // tk

    def body(df_ref, w_ref, au_ref, dau_ref, acc_ref):
        k = pl.program_id(2)

        @pl.when(k == 0)
        def _():
            acc_ref[...] = jnp.zeros_like(acc_ref)

        acc_ref[...] += lax.dot_general(df_ref[...], w_ref[...], (((1,), (1,)), ((), ())), preferred_element_type=F32)

        @pl.when(k == nk - 1)
        def _():
            ds = acc_ref[...]
            au_v = au_ref[...].astype(F32)
            a = au_v[:, :tn]
            u = au_v[:, tn:]
            sg = jax.nn.sigmoid(a)
            da = ds * u * (sg * (1.0 + a * (1.0 - sg)))
            du = ds * (a * sg)
            dau_ref[:, :tn] = da.astype(BF16)
            dau_ref[:, tn:] = du.astype(BF16)

    return pl.pallas_call(
        body, name=name, grid=(S // tm, F // tn, nk),
        in_specs=[pl.BlockSpec((tm, tk), lambda i, j, k: (i, k)), pl.BlockSpec((tn, tk), lambda i, j, k: (j, k)),
                  pl.BlockSpec((tm, 2 * tn), lambda i, j, k: (i, j))],
        out_specs=pl.BlockSpec((tm, 2 * tn), lambda i, j, k: (i, j)),
        out_shape=jax.ShapeDtypeStruct((S, 2 * F), BF16),
        scratch_shapes=[pltpu.VMEM((tm, tn), F32)],
        compiler_params=_cp(("parallel", "parallel", "arbitrary"), 40),
    )(df, wd, au)


def _fold8(x):
    tm, w = x.shape
    return jnp.sum(x.reshape(tm // 8, 8, w), axis=0)


def _row_spec(tm, w):
    return pl.BlockSpec((tm, w), lambda i: (i, 0))


def _vec_spec(rows, w):
    return pl.BlockSpec((rows, w), lambda i: (0, 0))


def _pre_norm_mod(x, gain, shift, scale1p, *, name):
    S, D = x.shape
    tm = ROW_TILE

    def body(x_ref, g_ref, sh_ref, sc_ref, h_ref):
        xv = x_ref[...]
        rinv = lax.rsqrt(jnp.mean(xv * xv, axis=-1, keepdims=True) + NORM_EPS)
        h_ref[...] = ((xv * rinv) * g_ref[...] * sc_ref[...] + sh_ref[...]).astype(BF16)

    return pl.pallas_call(
        body, name=name, grid=(S // tm,),
        in_specs=[_row_spec(tm, D), _vec_spec(1, D), _vec_spec(1, D), _vec_spec(1, D)],
        out_specs=_row_spec(tm, D), out_shape=jax.ShapeDtypeStruct((S, D), BF16),
        compiler_params=_cp(("parallel",), 32),
    )(x, gain, shift, scale1p)


def _pre_norm_mod_bwd(dh, x, dres, gain, scale1p, *, name):
    S, D = x.shape
    tm = ROW_TILE
    n = S // tm

    def body(dh_ref, x_ref, dr_ref, g_ref, sc_ref, dx_ref, sums_ref, acc_ref):
        i = pl.program_id(0)

        @pl.when(i == 0)
        def _():
            acc_ref[...] = jnp.zeros_like(acc_ref)

        xv = x_ref[...]
        dhv = dh_ref[...]
        g = g_ref[...]
        rinv = lax.rsqrt(jnp.mean(xv * xv, axis=-1, keepdims=True) + NORM_EPS)
        xn = xv * rinv
        dn = dhv * sc_ref[...]
        dxn = dn * g
        dx_ref[...] = dr_ref[...] + rinv * (dxn - xn * jnp.mean(dxn * xn, axis=-1, keepdims=True))
        acc_ref[0] += _fold8(dhv)
        acc_ref[1] += _fold8(dhv * (xn * g))
        acc_ref[2] += _fold8(dn * xn)

        @pl.when(i == n - 1)
        def _():
            for q in range(3):
                sums_ref[q:q + 1, :] = jnp.sum(acc_ref[q], axis=0, keepdims=True)

    return pl.pallas_call(
        body, name=name, grid=(n,),
        in_specs=[_row_spec(tm, D), _row_spec(tm, D), _row_spec(tm, D), _vec_spec(1, D), _vec_spec(1, D)],
        out_specs=(_row_spec(tm, D), _vec_spec(3, D)),
        out_shape=(jax.ShapeDtypeStruct((S, D), F32), jax.ShapeDtypeStruct((3, D), F32)),
        scratch_shapes=[pltpu.VMEM((3, 8, D), F32)],
        compiler_params=_cp(("arbitrary",), 40),
    )(dh, x, dres, gain, scale1p)


def _post_norm_res(x, f, gain, gw, *, name):
    S, D = x.shape
    tm = ROW_TILE

    def body(x_ref, f_ref, g_ref, gw_ref, o_ref):
        fv = f_ref[...]
        rinv = lax.rsqrt(jnp.mean(fv * fv, axis=-1, keepdims=True) + NORM_EPS)
        o_ref[...] = x_ref[...] + gw_ref[...] * ((fv * rinv) * g_ref[...])

    return pl.pallas_call(
        body, name=name, grid=(S // tm,),
        in_specs=[_row_spec(tm, D), _row_spec(tm, D), _vec_spec(1, D), _vec_spec(1, D)],
        out_specs=_row_spec(tm, D), out_shape=jax.ShapeDtypeStruct((S, D), F32),
        compiler_params=_cp(("parallel",), 32),
    )(x, f, gain, gw)


def _post_norm_res_bwd(dxo, f, gain, gw, weight, *, name):
    S, D = f.shape
    tm = ROW_TILE
    n = S // tm

    def body(d_ref, f_ref, g_ref, gw_ref, df_ref, sums_ref, acc_ref):
        i = pl.program_id(0)

        @pl.when(i == 0)
        def _():
            acc_ref[...] = jnp.zeros_like(acc_ref)

        fv = f_ref[...]
        dv = d_ref[...]
        g = g_ref[...]
        rinv = lax.rsqrt(jnp.mean(fv * fv, axis=-1, keepdims=True) + NORM_EPS)
        fh = fv * rinv
        dy = dv * gw_ref[...]
        dfh = dy * g
        df_ref[...] = (rinv * (dfh - fh * jnp.mean(dfh * fh, axis=-1, keepdims=True))).astype(BF16)
        acc_ref[0] += _fold8(weight * dv * (fh * g))
        acc_ref[1] += _fold8(dy * fh)

        @pl.when(i == n - 1)
        def _():
            for q in range(2):
                sums_ref[q:q + 1, :] = jnp.sum(acc_ref[q], axis=0, keepdims=True)

    return pl.pallas_call(
        body, name=name, grid=(n,),
        in_specs=[_row_spec(tm, D), _row_spec(tm, D), _vec_spec(1, D), _vec_spec(1, D)],
        out_specs=(_row_spec(tm, D), _vec_spec(2, D)),
        out_shape=(jax.ShapeDtypeStruct((S, D), BF16), jax.ShapeDtypeStruct((2, D), F32)),
        scratch_shapes=[pltpu.VMEM((2, 8, D), F32)],
        compiler_params=_cp(("arbitrary",), 40),
    )(dxo, f, gain, gw)


def _loss_head(y, target, *, name):
    S, D = y.shape
    tm = ROW_TILE

    def body(y_ref, t_ref, l_ref, dy_ref):
        i = pl.program_id(0)

        @pl.when(i == 0)
        def _():
            l_ref[...] = jnp.zeros_like(l_ref)

        err = y_ref[...] - t_ref[...]
        dy_ref[...] = err * (1.0 / D)
        row = jnp.sum(err * err, axis=-1, keepdims=True) * (0.5 / D)
        l_ref[...] += jnp.sum(row, axis=0, keepdims=True)

    return pl.pallas_call(
        body, name=name, grid=(S // tm,),
        in_specs=[_row_spec(tm, D), _row_spec(tm, D)],
        out_specs=(_vec_spec(1, 1), _row_spec(tm, D)),
        out_shape=(jax.ShapeDtypeStruct((1, 1), F32), jax.ShapeDtypeStruct((S, D), F32)),
        compiler_params=_cp(("arbitrary",), 32),
    )(y, target)


def _shift_down(z, j, row):
    return jnp.where(row >= j, pltpu.roll(z, j, 0), 0.0)


def _shift_up(z, j, row, n):
    return jnp.where(row < n - j, pltpu.roll(z, n - j, 0), 0.0)


def _pool_fwd(p, pool_w, pool_scale, *, name):
    S = p.shape[0]
    C = POOL_GROUP

    def body(u_ref, w_ref, sc_ref, o_ref, y_ref):
        g = pl.program_id(0)
        u = u_ref[...]
        row = lax.broadcasted_iota(jnp.int32, (S, C), 0)
        s1 = u + _shift_down(u, 1, row)
        s2 = s1 + _shift_down(s1, 2, row)
        s3 = s2 + _shift_down(s2, 4, row)
        s4 = s3 + _shift_down(s3, 8, row)
        gi = jnp.zeros((S, C), jnp.int32) + g
        win = jnp.where(gi == 0, s1, jnp.where(gi == 1, s2, jnp.where(gi == 2, s3, s4)))
        width = jnp.where(gi == 0, 2, jnp.where(gi == 1, 4, jnp.where(gi == 2, 8, 16)))
        count = jnp.minimum(row + 1, width).astype(F32)
        o = win / count - u
        o_ref[...] = o
        y_ref[...] = jnp.dot(o.astype(BF16), w_ref[0].astype(BF16), preferred_element_type=F32) * sc_ref[...]

    col = pl.BlockSpec((S, C), lambda g: (0, g))
    return pl.pallas_call(
        body, name=name, grid=(POOL_GROUPS,),
        in_specs=[col, pl.BlockSpec((1, C, C), lambda g: (g, 0, 0)), pl.BlockSpec((1, C), lambda g: (0, g))],
        out_specs=(col, col),
        out_shape=(jax.ShapeDtypeStruct((S, POOL_GROUPS * C), F32), jax.ShapeDtypeStruct((S, POOL_GROUPS * C), F32)),
        compiler_params=_cp(("parallel",), 48),
    )(p, pool_w, pool_scale)


def _pool_bwd(dcat, o, pool_w, pool_scale, *, name):
    S = o.shape[0]
    C = POOL_GROUP

    def body(dy_ref, o_ref, w_ref, sc_ref, du_ref, dw_ref, dsc_ref):
        g = pl.program_id(0)
        dy = dy_ref[...]
        ob = o_ref[...].astype(BF16)
        wb = w_ref[0].astype(BF16)
        mixed = jnp.dot(ob, wb, preferred_element_type=F32)
        dsc_ref[...] = jnp.sum(_fold8(dy * mixed), axis=0, keepdims=True)
        dmix = (dy * sc_ref[...]).astype(BF16)
        dw_ref[0] = lax.dot_general(ob, dmix, (((0,), (0,)), ((), ())), preferred_element_type=F32)
        do = lax.dot_general(dmix, wb, (((1,), (1,)), ((), ())), preferred_element_type=F32)
        row = lax.broadcasted_iota(jnp.int32, (S, C), 0)
        gi = jnp.zeros((S, C), jnp.int32) + g
        width = jnp.where(gi == 0, 2, jnp.where(gi == 1, 4, jnp.where(gi == 2, 8, 16)))
        z = do / jnp.minimum(row + 1, width).astype(F32)
        s1 = z + _shift_up(z, 1, row, S)
        s2 = s1 + _shift_up(s1, 2, row, S)
        s3 = s2 + _shift_up(s2, 4, row, S)
        s4 = s3 + _shift_up(s3, 8, row, S)
        win = jnp.where(gi == 0, s1, jnp.where(gi == 1, s2, jnp.where(gi == 2, s3, s4)))
        du_ref[...] = (win - do).astype(BF16)

    col = pl.BlockSpec((S, C), lambda g: (0, g))
    return pl.pallas_call(
        body, name=name, grid=(POOL_GROUPS,),
        in_specs=[col, col, pl.BlockSpec((1, C, C), lambda g: (g, 0, 0)), pl.BlockSpec((1, C), lambda g: (0, g))],
        out_specs=(col, pl.BlockSpec((1, C, C), lambda g: (g, 0, 0)), pl.BlockSpec((1, C), lambda g: (0, g))),
        out_shape=(jax.ShapeDtypeStruct((S, POOL_GROUPS * C), BF16), jax.ShapeDtypeStruct((POOL_GROUPS, C, C), F32),
                   jax.ShapeDtypeStruct((1, POOL_GROUPS * C), F32)),
        compiler_params=_cp(("parallel",), 48),
    )(dcat, o, pool_w, pool_scale)


def _block_ones():
    r = lax.broadcasted_iota(jnp.int32, (128, 128), 0) // HEAD
    c = lax.broadcasted_iota(jnp.int32, (128, 128), 1) // HEAD
    return jnp.where(r == c, 1.0, 0.0).astype(BF16)


def _segsum(x, bd):
    outs = []
    for j in range(x.shape[1] // 128):
        xs = x[:, j * 128:(j + 1) * 128]
        hi = xs.astype(BF16)
        lo = (xs - hi.astype(F32)).astype(BF16)
        outs.append(jnp.dot(hi, bd, preferred_element_type=F32) + jnp.dot(lo, bd, preferred_element_type=F32))
    return jnp.concatenate(outs, axis=1)


def _prep_common(q, qprev, first, mu, wl, w0, a0, kkw, kaw, R):
    tm = q.shape[0]
    row = lax.broadcasted_iota(jnp.int32, q.shape, 0)
    last = qprev[7:8, :] * first
    prev = jnp.where(row == 0, last, pltpu.roll(q, 1, 0))
    ps = q + mu * (prev - q)
    r = ps[:, 0:R]
    k = ps[:, R:2 * R]
    v = ps[:, 2 * R:3 * R]
    lo_in = ps[:, 3 * R:3 * R + LORA_PAD]
    lane = lax.broadcasted_iota(jnp.int32, (tm, LORA_PAD), 1)
    m_w = lane < LORA_W
    m_a = lane < LORA_W + LORA_A
    m_g = lane < LORA_W + LORA_A + LORA_G
    act = jnp.where(m_w, jnp.tanh(lo_in), jnp.where(m_a, lo_in, jnp.where(m_g, jax.nn.sigmoid(lo_in), 0.0)))
    lo = jnp.dot(act.astype(BF16), wl, preferred_element_type=F32)
    wpre = w0 + lo[:, 0:R]
    apre = a0 + lo[:, R:2 * R]
    g = lo[:, 2 * R:3 * R]
    neg = -wpre
    softplus = jnp.maximum(neg, 0.0) + jnp.log(1.0 + jnp.exp(-jnp.abs(neg)))
    wlog = -softplus - 0.5
    ew = jnp.exp(wlog)
    decay = jnp.exp(-ew)
    a = jax.nn.sigmoid(apre)
    kk = k * kkw
    bd = _block_ones()
    n2 = _segsum(kk * kk, bd)
    nrm = jnp.maximum(jnp.sqrt(n2), 1e-12)
    kap = kk / nrm
    kmul = 1.0 + (a - 1.0) * kaw
    k2 = k * kmul
    return dict(prev=prev, r=r, k=k, v=v, act=act, m_w=m_w, m_a=m_a, m_g=m_g, wpre=wpre, g=g, ew=ew, decay=decay,
                a=a, n2=n2, nrm=nrm, kap=kap, kmul=kmul, k2=k2, bd=bd)


def _prev_rows_spec(tm, w):
    return pl.BlockSpec((8, w), lambda i: (jnp.maximum(i * (tm // 8) - 1, 0), 0))


def _rwkv_prep(q, mu, wl, w0, a0, kkw, kaw, *, name):
    S, QW = q.shape
    R = w0.shape[1]
    tm = ROW_TILE // 2

    def body(q_ref, qp_ref, mu_ref, wl_ref, w0_ref, a0_ref, kk_ref, ka_ref, r_ref, w_ref, k_ref, v_ref, kap_ref,
             b_ref, g_ref):
        first = jnp.where(pl.program_id(0) > 0, 1.0, 0.0)
        t = _prep_common(q_ref[...], qp_ref[...], first, mu_ref[...], wl_ref[...], w0_ref[...], a0_ref[...],
                         kk_ref[...], ka_ref[...], R)
        r_ref[...] = t["r"]
        w_ref[...] = t["decay"]
        k_ref[...] = t["k2"]
        v_ref[...] = t["v"]
        kap_ref[...] = t["kap"]
        b_ref[...] = t["kap"] * t["a"]
        g_ref[...] = t["g"]

    vec = _vec_spec(1, R)
    return pl.pallas_call(
        body, name=name, grid=(S // tm,),
        in_specs=[_row_spec(tm, QW), _prev_rows_spec(tm, QW), _vec_spec(1, QW), _vec_spec(LORA_PAD, 3 * R), vec, vec,
                  vec, vec],
        out_specs=tuple([_row_spec(tm, R)] * 7),
        out_shape=tuple([jax.ShapeDtypeStruct((S, R), F32)] * 7),
        compiler_params=_cp(("parallel",), 48),
    )(q, q, mu, wl, w0, a0, kkw, kaw)


def _rwkv_prep_bwd(q, mu, wl, w0, a0, kkw, kaw, grads, *, name):
    S, QW = q.shape
    R = w0.shape[1]
    tm = ROW_TILE // 2
    n = S // tm

    def body(q_ref, qp_ref, mu_ref, wl_ref, w0_ref, a0_ref, kk_ref, ka_ref, dr_ref, dw_ref, dk2_ref, dv_ref, dkap_ref,
             db_ref, dg_ref, drb_ref, dk2b_ref, dvb_ref, dps_ref, dwl_ref, sums_ref, acc_ref):
        i = pl.program_id(0)

        @pl.when(i == 0)
        def _():
            acc_ref[...] = jnp.zeros_like(acc_ref)
            dwl_ref[...] = jnp.zeros_like(dwl_ref)

        first = jnp.where(i > 0, 1.0, 0.0)
        wl = wl_ref[...]
        kkw = kk_ref[...]
        kaw = ka_ref[...]
        t = _prep_common(q_ref[...], qp_ref[...], first, mu_ref[...], wl, w0_ref[...], a0_ref[...], kkw, kaw, R)
        a, kap, k, act = t["a"], t["kap"], t["k"], t["act"]
        db = db_ref[...]
        dk2 = dk2_ref[...] + dk2b_ref[...]
        dkap = dkap_ref[...] + db * a
        da = db * kap + dk2 * k * kaw
        dk = dk2 * t["kmul"]
        proj = jnp.where(jnp.sqrt(t["n2"]) > 1e-12, _segsum(kap * dkap, t["bd"]), 0.0)
        dkk = (dkap - kap * proj) / t["nrm"]
        dk = dk + dkk * kkw
        dapre = da * a * (1.0 - a)
        dwlog = dw_ref[...] * t["decay"] * (-t["ew"])
        dwpre = dwlog * jax.nn.sigmoid(-t["wpre"])
        acc_ref[0] += _fold8(dwpre)
        acc_ref[1] += _fold8(dapre)
        acc_ref[2] += _fold8(dkk * k)
        acc_ref[3] += _fold8(dk2 * k * (a - 1.0))
        dlo = jnp.concatenate([dwpre, dapre, dg_ref[...]], axis=1).astype(BF16)
        dwl_ref[...] += lax.dot_general(act.astype(BF16), dlo, (((0,), (0,)), ((), ())), preferred_element_type=F32)
        dact = lax.dot_general(dlo, wl, (((1,), (1,)), ((), ())), preferred_element_type=F32)
        dlin = jnp.where(t["m_w"], dact * (1.0 - act * act),
                         jnp.where(t["m_a"], dact, jnp.where(t["m_g"], dact * act * (1.0 - act), 0.0)))
        dps_ref[:, 0:R] = dr_ref[...] + drb_ref[...]
        dps_ref[:, R:2 * R] = dk
        dps_ref[:, 2 * R:3 * R] = dv_ref[...] + dvb_ref[...]
        dps_ref[:, 3 * R:3 * R + LORA_PAD] = dlin
        dps_ref[:, 3 * R + LORA_PAD:] = jnp.zeros((tm, QW - 3 * R - LORA_PAD), F32)

        @pl.when(i == n - 1)
        def _():
            for j in range(4):
                sums_ref[j:j + 1, :] = jnp.sum(acc_ref[j], axis=0, keepdims=True)

    vec = _vec_spec(1, R)
    return pl.pallas_call(
        body, name=name, grid=(n,),
        in_specs=[_row_spec(tm, QW), _prev_rows_spec(tm, QW), _vec_spec(1, QW), _vec_spec(LORA_PAD, 3 * R), vec, vec,
                  vec, vec] + [_row_spec(tm, R)] * 10,
        out_specs=(_row_spec(tm, QW), _vec_spec(LORA_PAD, 3 * R), _vec_spec(4, R)),
        out_shape=(jax.ShapeDtypeStruct((S, QW), F32), jax.ShapeDtypeStruct((LORA_PAD, 3 * R), F32),
                   jax.ShapeDtypeStruct((4, R), F32)),
        scratch_shapes=[pltpu.VMEM((4, 8, R), F32)],
        compiler_params=_cp(("arbitrary",), 56),
    )(q, q, mu, wl, w0, a0, kkw, kaw, *grads)


def _tshift_bwd(dps, q, mu, *, name):
    S, QW = q.shape
    tm = ROW_TILE // 2
    n = S // tm

    def body(d_ref, dn_ref, q_ref, qp_ref, mu_ref, dq_ref, dmu_ref, acc_ref):
        i = pl.program_id(0)

        @pl.when(i == 0)
        def _():
            acc_ref[...] = jnp.zeros_like(acc_ref)

        mu = mu_ref[...]
        d = d_ref[...]
        qv = q_ref[...]
        row = lax.broadcasted_iota(jnp.int32, d.shape, 0)
        first = jnp.where(i > 0, 1.0, 0.0)
        notlast = jnp.where(i < n - 1, 1.0, 0.0)
        prev = jnp.where(row == 0, qp_ref[7:8, :] * first, pltpu.roll(qv, 1, 0))
        z = d * mu
        nxt = jnp.where(row == tm - 1, dn_ref[0:1, :] * mu * notlast, pltpu.roll(z, tm - 1, 0))
        dq_ref[...] = (d * (1.0 - mu) + nxt).astype(BF16)
        acc_ref[...] += _fold8(d * (prev - qv))

        @pl.when(i == n - 1)
        def _():
            dmu_ref[...] = jnp.sum(acc_ref[...], axis=0, keepdims=True)

    nblk8 = S // 8
    next_spec = pl.BlockSpec((8, QW), lambda i: (jnp.minimum((i + 1) * (tm // 8), nblk8 - 1), 0))
    return pl.pallas_call(
        body, name=name, grid=(n,),
        in_specs=[_row_spec(tm, QW), next_spec, _row_spec(tm, QW), _prev_rows_spec(tm, QW), _vec_spec(1, QW)],
        out_specs=(_row_spec(tm, QW), _vec_spec(1, QW)),
        out_shape=(jax.ShapeDtypeStruct((S, QW), BF16), jax.ShapeDtypeStruct((1, QW), F32)),
        scratch_shapes=[pltpu.VMEM((8, QW), F32)],
        compiler_params=_cp(("arbitrary",), 48),
    )(dps, dps, q, q, mu)


def _post_common(ysc, r, k2, v, lnw, lnb, rk):
    bd = _block_ones()
    mean = _segsum(ysc, bd) * (1.0 / HEAD)
    d = ysc - mean
    var = _segsum(d * d, bd) * (1.0 / HEAD)
    rstd = lax.rsqrt(var + LN_X_EPS)
    yh = d * rstd
    rkk = _segsum(r * k2 * rk, bd)
    z = yh * lnw + lnb + rkk * v
    return bd, rstd, yh, rkk, z


def _rwkv_post(ysc, r, k2, v, g, ypool, lnw, lnb, rk, *, name):
    S, R = ysc.shape
    PW = ypool.shape[1]
    tm = ROW_TILE

    def body(y_ref, r_ref, k_ref, v_ref, g_ref, yp_ref, lw_ref, lb_ref, rk_ref, cat_ref):
        _, _, _, _, z = _post_common(y_ref[...], r_ref[...], k_ref[...], v_ref[...], lw_ref[...], lb_ref[...],
                                     rk_ref[...])
        cat_ref[:, 0:PW] = yp_ref[...].astype(BF16)
        cat_ref[:, PW:] = (z * g_ref[...]).astype(BF16)

    vec = _vec_spec(1, R)
    return pl.pallas_call(
        body, name=name, grid=(S // tm,),
        in_specs=[_row_spec(tm, R)] * 5 + [_row_spec(tm, PW), vec, vec, vec],
        out_specs=_row_spec(tm, PW + R), out_shape=jax.ShapeDtypeStruct((S, PW + R), BF16),
        compiler_params=_cp(("parallel",), 48),
    )(ysc, r, k2, v, g, ypool, lnw, lnb, rk)


def _rwkv_post_bwd(dcat, ysc, r, k2, v, g, lnw, lnb, rk, *, name):
    S, R = ysc.shape
    tm = ROW_TILE
    n = S // tm

    def body(d_ref, y_ref, r_ref, k_ref, v_ref, g_ref, lw_ref, lb_ref, rk_ref, dy_ref, dg_ref, drb_ref, dkb_ref,
             dvb_ref, sums_ref, acc_ref):
        i = pl.program_id(0)

        @pl.when(i == 0)
        def _():
            acc_ref[...] = jnp.zeros_like(acc_ref)

        rv, kv, vv, lw, rkw = r_ref[...], k_ref[...], v_ref[...], lw_ref[...], rk_ref[...]
        bd, rstd, yh, rkk, z = _post_common(y_ref[...], rv, kv, vv, lw, lb_ref[...], rkw)
        dyr = d_ref[...]
        dg_ref[...] = dyr * z
        dz = dyr * g_ref[...]
        dyh = dz * lw
        dy_ref[...] = rstd * (dyh - _segsum(dyh, bd) * (1.0 / HEAD) - yh * (_segsum(dyh * yh, bd) * (1.0 / HEAD)))
        dvb_ref[...] = dz * rkk
        drkk = _segsum(dz * vv, bd)
        drb_ref[...] = drkk * kv * rkw
        dkb_ref[...] = drkk * rv * rkw
        acc_ref[0] += _fold8(dz * yh)
        acc_ref[1] += _fold8(dz)
        acc_ref[2] += _fold8(drkk * rv * kv)

        @pl.when(i == n - 1)
        def _():
            for j in range(3):
                sums_ref[j:j + 1, :] = jnp.sum(acc_ref[j], axis=0, keepdims=True)

    vec = _vec_spec(1, R)
    dspec = _row_spec(tm, R)
    return pl.pallas_call(
        body, name=name, grid=(n,),
        in_specs=[dspec] + [_row_spec(tm, R)] * 5 + [vec, vec, vec],
        out_specs=tuple([_row_spec(tm, R)] * 5) + (_vec_spec(3, R),),
        out_shape=tuple([jax.ShapeDtypeStruct((S, R), F32)] * 5) + (jax.ShapeDtypeStruct((3, R), F32),),
        scratch_shapes=[pltpu.VMEM((3, 8, R), F32)],
        compiler_params=_cp(("arbitrary",), 56),
    )(dcat, ysc, r, k2, v, g, lnw, lnb, rk)


def _half_sums(x, m_a):
    s_a = jnp.sum(jnp.where(m_a, x, 0.0), axis=1, keepdims=True)
    s_b = jnp.sum(jnp.where(m_a, 0.0, x), axis=1, keepdims=True)
    return s_a, s_b


def _scan_fwd(r, w, k, v, kap, b, *, name):
    S, R = r.shape
    G, T = SCAN_G, SCAN_T
    NP = R // 128
    assert NP % G == 0 and S % T == 0
    GW = 128 * G

    def body(r_ref, w_ref, k_ref, v_ref, kap_ref, b_ref, y_ref, st_ref, s_scr, vt_scr, yt_scr):
        c = pl.program_id(1)

        @pl.when(c == 0)
        def _():
            s_scr[...] = jnp.zeros_like(s_scr)

        for g in range(G):
            vt_scr[g] = v_ref[:, g * 128:(g + 1) * 128].T
        yt_scr[...] = jnp.zeros_like(yt_scr)
        lane = lax.broadcasted_iota(jnp.int32, (HEAD, 128), 1)
        m_a = lane < HEAD

        def block(tb, carry):
            t0 = pl.multiple_of(tb * 8, 8)
            rb, wb, kb = r_ref[pl.ds(t0, 8), :], w_ref[pl.ds(t0, 8), :], k_ref[pl.ds(t0, 8), :]
            pb, bb = kap_ref[pl.ds(t0, 8), :], b_ref[pl.ds(t0, 8), :]
            def put_y(g, parts, hot_y):
                yt_scr[g, 0:HEAD, :] = jnp.where(hot_y, parts[0], yt_scr[g, 0:HEAD, :])
                yt_scr[g, HEAD:, :] = jnp.where(hot_y, parts[1], yt_scr[g, HEAD:, :])

            for j in range(8):
                t = t0 + j
                hot = lane == t
                sa_parts, v_parts, y_parts = [], [], []
                for g in range(G):
                    sl = slice(g * 128, (g + 1) * 128)
                    sa_parts.append(_half_sums(s_scr[g] * pb[j:j + 1, sl], m_a))
                    v_parts.append((jnp.sum(jnp.where(hot, vt_scr[g, 0:HEAD, :], 0.0), axis=1, keepdims=True),
                                    jnp.sum(jnp.where(hot, vt_scr[g, HEAD:, :], 0.0), axis=1, keepdims=True)))
                if j > 0:
                    for g in range(G):
                        sl = slice(g * 128, (g + 1) * 128)
                        y_parts.append(_half_sums(s_scr[g] * rb[j - 1:j, sl], m_a))
                for g in range(G):
                    sl = slice(g * 128, (g + 1) * 128)
                    sa = -jnp.where(m_a, sa_parts[g][0], sa_parts[g][1])
                    vcol = jnp.where(m_a, v_parts[g][0], v_parts[g][1])
                    st = s_scr[g] * wb[j:j + 1, sl] + sa * bb[j:j + 1, sl] + vcol * kb[j:j + 1, sl]
                    s_scr[g] = st
                    st_ref[g, t] = st
                if j > 0:
                    hot_y = lane == t - 1
                    for g in range(G):
                        put_y(g, y_parts[g], hot_y)
            hot_y = lane == t0 + 7
            for g in range(G):
                sl = slice(g * 128, (g + 1) * 128)
                put_y(g, _half_sums(s_scr[g] * rb[7:8, sl], m_a), hot_y)
            return carry

        lax.fori_loop(0, T // 8, block, 0)
        for g in range(G):
            y_ref[:, g * 128:(g + 1) * 128] = yt_scr[g].T

    tspec = pl.BlockSpec((T, GW), lambda p, c: (c, p))
    return pl.pallas_call(
        body, name=name, grid=(NP // G, S // T),
        in_specs=[tspec] * 6,
        out_specs=(tspec, pl.BlockSpec((G, T, HEAD, 128), lambda p, c: (p, c, 0, 0))),
        out_shape=(jax.ShapeDtypeStruct((S, R), F32), jax.ShapeDtypeStruct((NP, S, HEAD, 128), F32)),
        scratch_shapes=[pltpu.VMEM((G, HEAD, 128), F32), pltpu.VMEM((G, 128, 128), F32),
                        pltpu.VMEM((G, 128, 128), F32)],
        compiler_params=_cp(("parallel", "arbitrary"), 48),
    )(r, w, k, v, kap, b)


def _scan_bwd(r, w, k, v, kap, b, dy, states, *, name):
    S, R = r.shape
    G, T = SCAN_G, SCAN_T
    NP = R // 128
    NC = S // T
    GW = 128 * G

    def body(r_ref, w_ref, k_ref, v_ref, kap_ref, b_ref, dy_ref, st_ref, sp_ref, dr_ref, dw_ref, dk_ref, dv_ref,
             dkap_ref, db_ref, ds_scr, vt_scr, dyt_scr, dvt_scr, col_scr):
        ci = pl.program_id(1)

        @pl.when(ci == 0)
        def _():
            ds_scr[...] = jnp.zeros_like(ds_scr)

        for g in range(G):
            vt_scr[g] = v_ref[:, g * 128:(g + 1) * 128].T
            dyt_scr[g] = dy_ref[:, g * 128:(g + 1) * 128].T
        dvt_scr[...] = jnp.zeros_like(dvt_scr)
        lane = lax.broadcasted_iota(jnp.int32, (HEAD, 128), 1)
        m_a = lane < HEAD
        sub = lax.broadcasted_iota(jnp.int32, (8, 128), 0)
        zero_i = jnp.zeros((HEAD, 128), jnp.int32)
        has_prev = jnp.where(ci < NC - 1, 1.0, 0.0)

        def state_before(g, t):
            at_start = (zero_i + t) == 0
            return jnp.where(at_start, sp_ref[g, 0] * has_prev, st_ref[g, jnp.maximum(t - 1, 0)])

        def column_sums(g, t, p_row):
            hot_t = lane == t
            return (jnp.sum(jnp.where(hot_t, dyt_scr[g, 0:HEAD, :], 0.0), axis=1, keepdims=True),
                    jnp.sum(jnp.where(hot_t, dyt_scr[g, HEAD:, :], 0.0), axis=1, keepdims=True),
                    jnp.sum(jnp.where(hot_t, vt_scr[g, 0:HEAD, :], 0.0), axis=1, keepdims=True),
                    jnp.sum(jnp.where(hot_t, vt_scr[g, HEAD:, :], 0.0), axis=1, keepdims=True),
                    *_half_sums(state_before(g, t) * p_row, m_a))

        def put_columns(g, parts):
            col_scr[g, 0] = jnp.where(m_a, parts[0], parts[1])
            col_scr[g, 1] = jnp.where(m_a, parts[2], parts[3])
            col_scr[g, 2] = -jnp.where(m_a, parts[4], parts[5])

        pb_last = kap_ref[pl.ds(T - 8, 8), :]
        for g in range(G):
            put_columns(g, column_sums(g, T - 1, pb_last[7:8, g * 128:(g + 1) * 128]))

        def block(it, carry):
            tb = T // 8 - 1 - it
            t0 = pl.multiple_of(tb * 8, 8)
            rb, wb, kb = r_ref[pl.ds(t0, 8), :], w_ref[pl.ds(t0, 8), :], k_ref[pl.ds(t0, 8), :]
            pb, bb = kap_ref[pl.ds(t0, 8), :], b_ref[pl.ds(t0, 8), :]
            pb_prev = kap_ref[pl.ds(pl.multiple_of(jnp.maximum(t0 - 8, 0), 8), 8), :]
            outs = [[jnp.zeros((8, 128), F32) for _ in range(5)] for _ in range(G)]
            for j in range(7, -1, -1):
                t = t0 + j
                hot = lane == t
                ds_parts, next_cols = [], []
                for g in range(G):
                    sl = slice(g * 128, (g + 1) * 128)
                    ds = ds_scr[g] + col_scr[g, 0] * rb[j:j + 1, sl]
                    ds_scr[g] = ds
                    ds_parts.append((_half_sums(ds * kb[j:j + 1, sl], m_a), _half_sums(ds * bb[j:j + 1, sl], m_a)))
                for g in range(G):
                    sl = slice(g * 128, (g + 1) * 128)
                    p_row = pb[j - 1:j, sl] if j > 0 else pb_prev[7:8, sl]
                    next_cols.append(column_sums(g, jnp.maximum(t - 1, 0), p_row))
                for g in range(G):
                    sl = slice(g * 128, (g + 1) * 128)
                    w_r, p_r = wb[j:j + 1, sl], pb[j:j + 1, sl]
                    ds = ds_scr[g]
                    s_p = state_before(g, t)
                    dycol, vcol, sa = col_scr[g, 0], col_scr[g, 1], col_scr[g, 2]
                    dr_row = jnp.sum(st_ref[g, t] * dycol, axis=0, keepdims=True)
                    dk_row = jnp.sum(ds * vcol, axis=0, keepdims=True)
                    db_row = jnp.sum(ds * sa, axis=0, keepdims=True)
                    dw_row = jnp.sum(ds * s_p, axis=0, keepdims=True)
                    (dv_a, dv_b), (dsa_a, dsa_b) = ds_parts[g]
                    dvt_scr[g, 0:HEAD, :] = jnp.where(hot, dv_a, dvt_scr[g, 0:HEAD, :])
                    dvt_scr[g, HEAD:, :] = jnp.where(hot, dv_b, dvt_scr[g, HEAD:, :])
                    dsa = jnp.where(m_a, dsa_a, dsa_b)
                    dkap_row = -jnp.sum(s_p * dsa, axis=0, keepdims=True)
                    ds_scr[g] = ds * w_r - dsa * p_r
                    pick = sub == j
                    for q, row in enumerate((dr_row, dw_row, dk_row, dkap_row, db_row)):
                        outs[g][q] = jnp.where(pick, row, outs[g][q])
                for g in range(G):
                    put_columns(g, next_cols[g])
            for g in range(G):
                sl = slice(g * 128, (g + 1) * 128)
                for q, ref in enumerate((dr_ref, dw_ref, dk_ref, dkap_ref, db_ref)):
                    ref[pl.ds(t0, 8), sl] = outs[g][q]
            return carry

        lax.fori_loop(0, T // 8, block, 0)
        for g in range(G):
            dv_ref[:, g * 128:(g + 1) * 128] = dvt_scr[g].T

    tspec = pl.BlockSpec((T, GW), lambda p, c: (NC - 1 - c, p))
    st_spec = pl.BlockSpec((G, T, HEAD, 128), lambda p, c: (p, NC - 1 - c, 0, 0))
    prev_spec = pl.BlockSpec((G, 1, HEAD, 128), lambda p, c: (p, jnp.maximum((NC - 1 - c) * T - 1, 0), 0, 0))
    return pl.pallas_call(
        body, name=name, grid=(NP // G, NC),
        in_specs=[tspec] * 7 + [st_spec, prev_spec],
        out_specs=tuple([tspec] * 6),
        out_shape=tuple([jax.ShapeDtypeStruct((S, R), F32)] * 6),
        scratch_shapes=[pltpu.VMEM((G, HEAD, 128), F32), pltpu.VMEM((G, 128, 128), F32),
                        pltpu.VMEM((G, 128, 128), F32), pltpu.VMEM((G, 128, 128), F32),
                        pltpu.VMEM((G, 3, HEAD, 128), F32)],
        compiler_params=_cp(("parallel", "arbitrary"), 48),
    )(r, w, k, v, kap, b, dy, states, states)


def _sum_parts(parts, *, name):
    P, rows, W = parts.shape
    tr = rows
    for cand in (1024, 512, 256, 128, 64, 32, 16, 8):
        if rows % cand == 0:
            tr = cand
            break

    def body(p_ref, o_ref):
        acc = p_ref[0]
        for s in range(1, P):
            acc = acc + p_ref[s]
        o_ref[...] = acc

    return pl.pallas_call(
        body, name=name, grid=(rows // tr,),
        in_specs=[pl.BlockSpec((P, tr, W), lambda i: (0, i, 0))],
        out_specs=pl.BlockSpec((tr, W), lambda i: (i, 0)), out_shape=jax.ShapeDtypeStruct((rows, W), F32),
        compiler_params=_cp(("parallel",), 32),
    )(parts)


def _adamw(w, m, v, parts, *, name):
    R, C = w.shape
    P = parts.shape[0]
    tr = R
    for cand in (1024, 512, 256, 128, 64, 32, 16, 8):
        if R % cand == 0 and cand * C * 4 * (7 + P) <= 10 * 1024 * 1024:
            tr = cand
            break
    bc1 = 1.0 - ADAM_B1 ** ADAM_STEP
    bc2 = 1.0 - ADAM_B2 ** ADAM_STEP

    def body(w_ref, m_ref, v_ref, p_ref, g_ref, d_ref, nm_ref, nv_ref):
        g = p_ref[0].astype(F32)
        for s in range(1, P):
            g = g + p_ref[s].astype(F32)
        m1 = ADAM_B1 * m_ref[...] + (1.0 - ADAM_B1) * g
        v1 = ADAM_B2 * v_ref[...] + (1.0 - ADAM_B2) * (g * g)
        m_hat = m1 / bc1
        v_hat = v1 / bc2
        g_ref[...] = g
        d_ref[...] = -ADAM_LR * (m_hat / (jnp.sqrt(v_hat) + ADAM_EPS) + ADAM_WD * w_ref[...])
        nm_ref[...] = m1
        nv_ref[...] = v1

    spec = pl.BlockSpec((tr, C), lambda i: (i, 0))
    return pl.pallas_call(
        body, name=name, grid=(R // tr,),
        in_specs=[spec, spec, spec, pl.BlockSpec((P, tr, C), lambda i: (0, i, 0))],
        out_specs=(spec, spec, spec, spec), out_shape=tuple([jax.ShapeDtypeStruct((R, C), F32)] * 4),
        compiler_params=_cp(("parallel",), 40),
    )(w, m, v, parts)


def _cols_full(g8):
    n, rows, c = g8.shape
    return jnp.transpose(g8, (1, 0, 2)).reshape(rows, n * c)


def _cols_split(full):
    rows, cols = full.shape
    return jnp.transpose(full.reshape(rows, N_DEV, cols // N_DEV), (1, 0, 2))


def _interleave(gate, up):
    D, F = gate.shape
    nj = F // FF_TILE
    return jnp.stack([gate.reshape(D, nj, FF_TILE), up.reshape(D, nj, FF_TILE)], axis=2).reshape(D, 2 * F)


def _deinterleave(gu):
    D, F2 = gu.shape
    nj = F2 // (2 * FF_TILE)
    t = gu.reshape(D, nj, 2, FF_TILE)
    return t[:, :, 0, :].reshape(D, F2 // 2), t[:, :, 1, :].reshape(D, F2 // 2)


def _pack(vals, rows_multiple=512):
    flat = jnp.concatenate([v.reshape(-1).astype(F32) for v in vals])
    n = flat.shape[0]
    unit = 128 * rows_multiple
    padded = ((n + unit - 1) // unit) * unit
    return jnp.pad(flat, (0, padded - n)).reshape(padded // 128, 128)


def _unpack(packed, shapes):
    flat = packed.reshape(-1)
    out, off = [], 0
    for shp in shapes:
        size = 1
        for d in shp:
            size *= d
        out.append(flat[off:off + size].reshape(shp))
        off += size
    return out


def _ffn_forward(x, wgu, wd, gpre, gpost, shift, scale1p, gw, tag):
    h = _pre_norm_mod(x, gpre, shift, scale1p, name=f"{tag}_pre")
    au, s = _ffn_up(h, wgu, tm=1024, tk=512, name=f"{tag}_up")
    f = _mm(s, wd, tm=1024, tn=1024, tk=512, name=f"{tag}_down")
    xo = _post_norm_res(x, f, gpost, gw, name=f"{tag}_post")
    return xo, (h, au, s, f)


def _ffn_backward(dxo, x, saved, wgu, wd, gpre, gpost, scale1p, gw, tag):
    h, au, s, f = saved
    df, post_sums = _post_norm_res_bwd(dxo, f, gpost, gw, MACARON, name=f"{tag}_post_bwd")
    dwd = _mm(s, df, ta=True, tm=512, tn=2048, tk=512, name=f"{tag}_dwd")
    dau = _ffn_down_bwd(df, wd, au, tm=1024, tk=512, name=f"{tag}_down_bwd")
    dwgu = _mm(h, dau, ta=True, tm=1024, tn=1024, tk=512, name=f"{tag}_dwgu")
    dh = _mm(dau, wgu, tb=True, tm=1024, tn=1024, tk=512, name=f"{tag}_dh")
    dx, pre_sums = _pre_norm_mod_bwd(dh, x, dxo, gpre, scale1p, name=f"{tag}_pre_bwd")
    return dx, dwgu, dwd, pre_sums, post_sums


def kernel(x, c, w_ada, b_ada, norm_pre, norm_post, ffn1_w_gate, ffn1_w_up, ffn1_w_down, w_in, mu_shift, pool_w, pool_scale, w0, w2, a0, a2, g2, k_k, k_a, r_k, lnx_w, lnx_b, w_out, ffn2_w_gate, ffn2_w_up, ffn2_w_down, loss_target, m_w_ada, m_b_ada, m_norm_pre, m_norm_post, m_ffn1_w_gate, m_ffn1_w_up, m_ffn1_w_down, m_w_in, m_mu_shift, m_pool_w, m_pool_scale, m_w0, m_w2, m_a0, m_a2, m_g2, m_k_k, m_k_a, m_r_k, m_lnx_w, m_lnx_b, m_w_out, m_ffn2_w_gate, m_ffn2_w_up, m_ffn2_w_down, v_w_ada, v_b_ada, v_norm_pre, v_norm_post, v_ffn1_w_gate, v_ffn1_w_up, v_ffn1_w_down, v_w_in, v_mu_shift, v_pool_w, v_pool_scale, v_w0, v_w2, v_a0, v_a2, v_g2, v_k_k, v_k_a, v_r_k, v_lnx_w, v_lnx_b, v_w_out, v_ffn2_w_gate, v_ffn2_w_up, v_ffn2_w_down):
    names = ["w_ada", "b_ada", "norm_pre", "norm_post", "ffn1_w_gate", "ffn1_w_up", "ffn1_w_down", "w_in", "mu_shift",
             "pool_w", "pool_scale", "w0", "w2", "a0", "a2", "g2", "k_k", "k_a", "r_k", "lnx_w", "lnx_b", "w_out",
             "ffn2_w_gate", "ffn2_w_up", "ffn2_w_down"]
    env = dict(locals())
    W = {n: env[n][0] for n in names}
    M1 = {n: env["m_" + n][0] for n in names}
    V1 = {n: env["v_" + n][0] for n in names}

    me = _my_index()
    xs = x[0]
    tgt = loss_target[0]
    S, D = xs.shape
    F = W["ffn1_w_gate"].shape[1] * N_DEV
    R = W["w0"].shape[0]
    PW = D - R
    IN_W = W["w_in"].shape[1] * N_DEV
    P_W = F
    QW = P_W - PW
    NMOD = 9 * D
    ada_c = W["w_ada"].shape[1]

    c_all, npre8, npost8, w2_8, a2_8, g2_8 = _exchange(
        [c, W["norm_pre"], W["norm_post"], W["w2"].astype(BF16), W["a2"].astype(BF16), W["g2"].astype(BF16)],
        scatter=False, name="gather_small")
    c_all = c_all.reshape(N_DEV, D)
    gpre = _cols_full(npre8)
    gpost = _cols_full(npost8)
    wl = jnp.zeros((LORA_PAD, 3 * R), BF16)
    wl = wl.at[0:LORA_W, 0:R].set(_cols_full(w2_8))
    wl = wl.at[LORA_W:LORA_W + LORA_A, R:2 * R].set(_cols_full(a2_8))
    wl = wl.at[LORA_W + LORA_A:LORA_W + LORA_A + LORA_G, 2 * R:3 * R].set(_cols_full(g2_8))

    sc_all = jax.nn.silu(c_all)
    sc_pad = jnp.concatenate([sc_all, jnp.zeros((8, D), F32)], axis=0).astype(BF16)
    modcols = _mm(sc_pad, W["w_ada"], tm=16, tn=ada_c, tk=256, name="ada_fwd")[0:N_DEV]
    modcols = modcols + lax.dynamic_slice(W["b_ada"], (me * ada_c,), (ada_c,))[None, :]
    (mod8,) = _exchange([modcols], scatter=False, name="gather_mod")
    mod = lax.dynamic_index_in_dim(mod8, me, axis=1, keepdims=False).reshape(9, D)

    def mod_row(i):
        return mod[i:i + 1, :]

    def gather_ffn(tag, name):
        g8, u8, d8 = _gather_two_level([W[f"{tag}_w_gate"].astype(BF16), W[f"{tag}_w_up"].astype(BF16),
                                        W[f"{tag}_w_down"].astype(BF16)], name=name)
        return _interleave(_cols_full(g8), _cols_full(u8)), d8.reshape(F, D)

    wgu1, wd1 = gather_ffn("ffn1", "gather_ffn")
    win8, wout8 = _gather_two_level([W["w_in"].astype(BF16), W["w_out"].astype(BF16)], name="gather_mixer")
    w_in_p = jnp.pad(_cols_full(win8), ((0, 0), (0, P_W - IN_W)))
    w_out_f = wout8.reshape(D, D)
    wgu2, wd2 = gather_ffn("ffn2", "gather_ffn")

    mu_p = jnp.pad(W["mu_shift"], (0, QW - W["mu_shift"].shape[0]))[None, :]
    vec = lambda a: a.reshape(1, -1)
    w0r, a0r, kkr, kar = vec(W["w0"]), vec(W["a0"]), vec(W["k_k"]), vec(W["k_a"])
    lnw, lnb, rkr = vec(W["lnx_w"]), vec(W["lnx_b"]), vec(W["r_k"])
    pscale = vec(W["pool_scale"])

    sc1p = [1.0 + mod_row(3 * s + 1) for s in range(3)]
    shifts = [mod_row(3 * s) for s in range(3)]
    wgts = [MACARON, 1.0, MACARON]
    gws = [wgts[s] * (1.0 + mod_row(3 * s + 2)) for s in range(3)]
    gp = [gpre[s:s + 1] for s in range(3)]
    gq = [gpost[s:s + 1] for s in range(3)]

    x1, sv1 = _ffn_forward(xs, wgu1, wd1, gp[0], gq[0], shifts[0], sc1p[0], gws[0], "ffn")

    h2 = _pre_norm_mod(x1, gp[1], shifts[1], sc1p[1], name="mix_pre")
    p = _mm(h2, w_in_p, tm=1024, tn=512, tk=512, name="mix_in")
    q = p[:, PW:]
    o_pool, y_pool = _pool_fwd(p, W["pool_w"], pscale, name="pool_fwd")
    r_s, w_s, k_s, v_s, kap_s, b_s, g_s = _rwkv_prep(q, mu_p, wl, w0r, a0r, kkr, kar, name="rwkv_prep")
    y_scan, states = _scan_fwd(r_s, w_s, k_s, v_s, kap_s, b_s, name="scan_fwd")
    cat = _rwkv_post(y_scan, r_s, k_s, v_s, g_s, y_pool, lnw, lnb, rkr, name="rwkv_post")
    f2 = _mm(cat, w_out_f, tm=1024, tn=1024, tk=512, name="mix_out")
    x2 = _post_norm_res(x1, f2, gq[1], gws[1], name="mix_post")

    x3, sv3 = _ffn_forward(x2, wgu2, wd2, gp[2], gq[2], shifts[2], sc1p[2], gws[2], "ffn")

    loss_part, dx3 = _loss_head(x3, tgt, name="loss_head")
    loss = lax.psum(loss_part[0, 0], MESH_AXES)

    dx2, dwgu2, dwd2, pre3, post3 = _ffn_backward(dx3, x2, sv3, wgu2, wd2, gp[2], gq[2], sc1p[2], gws[2], "ffn")

    df2, post2 = _post_norm_res_bwd(dx2, f2, gq[1], gws[1], 1.0, name="mix_post_bwd")
    dw_out = _mm(cat, df2, ta=True, tm=1024, tn=1024, tk=512, name="mix_dwout")
    dcat = _mm(df2, w_out_f, tb=True, tm=1024, tn=1024, tk=512, name="mix_dcat")
    dyr = dcat[:, PW:]
    dysc, dg, dr_b, dk2_b, dv_b, post_sums = _rwkv_post_bwd(dyr, y_scan, r_s, k_s, v_s, g_s, lnw, lnb, rkr,
                                                             name="rwkv_post_bwd")
    dr, dw, dk2, dv, dkap, db = _scan_bwd(r_s, w_s, k_s, v_s, kap_s, b_s, dysc, states, name="scan_bwd")
    dps, dwl, prep_sums = _rwkv_prep_bwd(q, mu_p, wl, w0r, a0r, kkr, kar,
                                         (dr, dw, dk2, dv, dkap, db, dg, dr_b, dk2_b, dv_b), name="rwkv_prep_bwd")
    dq, dmu = _tshift_bwd(dps, q, mu_p, name="tshift_bwd")
    du_pool, dpool_w, dpool_scale = _pool_bwd(dcat, o_pool, W["pool_w"], pscale, name="pool_bwd")
    dp = jnp.concatenate([du_pool, dq], axis=1)
    dw_in = _mm(h2, dp, ta=True, tm=1024, tn=512, tk=512, name="mix_dwin")
    dh2 = _mm(dp, w_in_p, tb=True, tm=1024, tn=1024, tk=512, name="mix_dh")
    dx1, pre2 = _pre_norm_mod_bwd(dh2, x1, dx2, gp[1], sc1p[1], name="mix_pre_bwd")

    dx0, dwgu1, dwd1, pre1, post1 = _ffn_backward(dx1, xs, sv1, wgu1, wd1, gp[0], gq[0], sc1p[0], gws[0], "ffn")

    pres, posts = [pre1, pre2, pre3], [post1, post2, post3]
    dmod = jnp.stack([jnp.stack([pres[s][0], pres[s][1], posts[s][0]]) for s in range(3)]).reshape(NMOD // 128, 128)
    dnorm_pre = jnp.stack([pres[s][2] for s in range(3)])
    dnorm_post = jnp.stack([posts[s][1] for s in range(3)])

    small = [dmu[0, :W["mu_shift"].shape[0]], dpool_w, dpool_scale, prep_sums[0], prep_sums[1], prep_sums[2],
             prep_sums[3], post_sums[2], post_sums[0], post_sums[1], dnorm_pre, dnorm_post,
             dwl[0:LORA_W, 0:R], dwl[LORA_W:LORA_W + LORA_A, R:2 * R],
             dwl[LORA_W + LORA_A:LORA_W + LORA_A + LORA_G, 2 * R:3 * R]]
    small_shapes = [a.shape for a in small]
    dmod8, small8 = _exchange([dmod, _pack(small)], scatter=False, name="gather_grads")
    g_b_ada = _sum_parts(dmod8, name="sum_dmod").reshape(NMOD)
    red = _unpack(_sum_parts(small8, name="sum_small"), small_shapes)
    (g_mu, g_pool_w, g_pool_scale, g_w0, g_a0, g_kk, g_ka, g_rk, g_lnw, g_lnb, g_npre, g_npost, g_w2, g_a2,
     g_g2) = red

    dmod_all = dmod8.reshape(N_DEV, NMOD)
    dmod_cols = lax.dynamic_slice(dmod_all, (0, me * ada_c), (N_DEV, ada_c))
    dmod_cols = jnp.concatenate([dmod_cols, jnp.zeros_like(dmod_cols)], axis=0)
    g_w_ada = _mm(sc_pad, dmod_cols, ta=True, tm=D, tn=ada_c // 9, tk=16, name="ada_bwd")

    def by_core_chip(blocks):
        shp = blocks.shape
        t = blocks.astype(BF16).reshape((N_DEV // 2, 2) + shp[1:])
        return jnp.swapaxes(t, 0, 1)

    def scatter(blocks, tag):
        mine = [by_core_chip(b) for b in blocks]
        got = _sibling_swap(mine, name=f"{tag}_swap")
        sums = [_pair_add(m, g, name=f"{tag}_add{i}") for i, (m, g) in enumerate(zip(mine, got))]
        return _chips_all_to_all(sums, name=f"{tag}_chips")

    def scatter_ffn(dwgu, dwd, tag):
        dgate, dup = _deinterleave(dwgu)
        return scatter([_cols_split(dgate), _cols_split(dup), dwd.reshape(N_DEV, F // N_DEV, D)], tag)

    pg2, pu2, pd2 = scatter_ffn(dwgu2, dwd2, "scatter_ffn")
    pin, pout = scatter([_cols_split(dw_in[:, :IN_W]), dw_out.reshape(N_DEV, D // N_DEV, D)], "scatter_mixer")
    pg1, pu1, pd1 = scatter_ffn(dwgu1, dwd1, "scatter_ffn")

    res = {}

    def big(nm, parts, tag):
        res[nm] = _adamw(W[nm], M1[nm], V1[nm], parts, name=tag)

    big("ffn1_w_gate", pg1, "adamw_cols")
    big("ffn1_w_up", pu1, "adamw_cols")
    big("ffn1_w_down", pd1, "adamw_rows")
    big("ffn2_w_gate", pg2, "adamw_cols")
    big("ffn2_w_up", pu2, "adamw_cols")
    big("ffn2_w_down", pd2, "adamw_rows")
    big("w_in", pin, "adamw_w_in")
    big("w_out", pout, "adamw_w_out")
    big("w_ada", g_w_ada[None], "adamw_w_ada")

    def my_cols(full, width):
        return lax.dynamic_slice_in_dim(full, me * width, width, axis=full.ndim - 1)

    small_names = ["b_ada", "mu_shift", "pool_w", "pool_scale", "w0", "a0", "k_k", "k_a", "r_k", "lnx_w", "lnx_b",
                   "norm_pre", "norm_post", "w2", "a2", "g2"]
    small_grads = [g_b_ada, g_mu, g_pool_w, g_pool_scale, g_w0, g_a0, g_kk, g_ka, g_rk.reshape(W["r_k"].shape), g_lnw,
                   g_lnb, my_cols(g_npre, D // N_DEV), my_cols(g_npost, D // N_DEV), my_cols(g_w2, R // N_DEV),
                   my_cols(g_a2, R // N_DEV), my_cols(g_g2, R // N_DEV)]
    shapes = [W[n].shape for n in small_names]
    packed = _adamw(_pack([W[n] for n in small_names]), _pack([M1[n] for n in small_names]),
                    _pack([V1[n] for n in small_names]), _pack(small_grads)[None], name="adamw_small")
    unpacked = [_unpack(t, shapes) for t in packed]
    for i, nm in enumerate(small_names):
        res[nm] = tuple(unpacked[k][i] for k in range(4))

    outs = [loss, dx0[None]]
    for k in range(4):
        outs.extend(res[nm][k][None] for nm in names)
    return tuple(outs)
```

```python
import functools

import jax
import jax.numpy as jnp
from jax import lax
from jax.experimental import pallas as pl
from jax.experimental.pallas import tpu as pltpu

F32 = jnp.float32
BF16 = jnp.bfloat16
N_DEV = 8
MESH_AXES = ("x", "y", "c")

NORM_EPS = 1e-6
HEAD = 64
LN_X_EPS = 1e-5 * HEAD
POOL_GROUPS = 4
POOL_GROUP = 128
MACARON = 0.5
LORA_W, LORA_A, LORA_G = 64, 64, 224
LORA_PAD = 384
ADAM_LR, ADAM_B1, ADAM_B2, ADAM_EPS, ADAM_WD, ADAM_STEP = 0.001, 0.9, 0.999, 1e-08, 0.01, 10

FF_TILE = 512
ROW_TILE = 256
SCAN_T = 64
SCAN_G = 6
VMEM_CAP = 56 * 1024 * 1024


def _cp(sem, vmem_mb):
    return pltpu.CompilerParams(dimension_semantics=sem, vmem_limit_bytes=min(vmem_mb * 1024 * 1024, VMEM_CAP))


def _my_index():
    return 4 * lax.axis_index("x") + 2 * lax.axis_index("y") + lax.axis_index("c")


def _exchange(arrays, *, scatter, name):
    n = len(arrays)
    out_shapes = []
    for a in arrays:
        shp = a.shape if scatter else (N_DEV,) + a.shape
        out_shapes.append(jax.ShapeDtypeStruct(shp, a.dtype))

    def body(*refs):
        ins, outs = refs[:n], refs[n:2 * n]
        send_sems, recv_sems, local_sems = refs[2 * n:]
        me = _my_index()

        def dev(p):
            return (p // 4, (p // 2) % 2, p % 2)

        def copy(i, d):
            peer = (me + d) % N_DEV
            src = ins[i].at[peer] if scatter else ins[i]
            return pltpu.make_async_remote_copy(
                src_ref=src, dst_ref=outs[i].at[me], send_sem=send_sems.at[i, d - 1],
                recv_sem=recv_sems.at[i, d - 1], device_id=dev(peer), device_id_type=pl.DeviceIdType.MESH)

        def arrival(i, d):
            frm = (me + N_DEV - d) % N_DEV
            src = ins[i].at[frm] if scatter else ins[i]
            return pltpu.make_async_remote_copy(
                src_ref=src, dst_ref=outs[i].at[frm], send_sem=send_sems.at[i, d - 1],
                recv_sem=recv_sems.at[i, d - 1], device_id=dev(frm), device_id_type=pl.DeviceIdType.MESH)

        locals_ = []
        for i in range(n):
            src = ins[i].at[me] if scatter else ins[i]
            lc = pltpu.make_async_copy(src, outs[i].at[me], local_sems.at[i])
            lc.start()
            locals_.append(lc)
        sends = [copy(i, d) for d in range(1, N_DEV) for i in range(n)]
        for cp in sends:
            cp.start()
        for d in range(1, N_DEV):
            for i in range(n):
                arrival(i, d).wait_recv()
        for cp in sends:
            cp.wait_send()
        for lc in locals_:
            lc.wait()

    hbm = pl.BlockSpec(memory_space=pltpu.HBM)
    return pl.pallas_call(
        body, name=name, out_shape=tuple(out_shapes), in_specs=[hbm] * n, out_specs=tuple([hbm] * n),
        scratch_shapes=[pltpu.SemaphoreType.DMA((n, N_DEV - 1)), pltpu.SemaphoreType.DMA((n, N_DEV - 1)),
                        pltpu.SemaphoreType.DMA((n,))],
    )(*arrays)


def _remote(src, dst, send_sem, recv_sem, to):
    return pltpu.make_async_remote_copy(src_ref=src, dst_ref=dst, send_sem=send_sem, recv_sem=recv_sem,
                                        device_id=to, device_id_type=pl.DeviceIdType.MESH)


def _gather_two_level(arrays, *, name):
    n = len(arrays)
    out_shapes = [jax.ShapeDtypeStruct((N_DEV,) + a.shape, a.dtype) for a in arrays]

    def body(*refs):
        ins, outs = refs[:n], refs[n:2 * n]
        send_sems, recv_sems, local_sems = refs[2 * n:]
        x, y, c = lax.axis_index("x"), lax.axis_index("y"), lax.axis_index("c")
        sibling = (x, y, 1 - c)
        chips = [(1 - x, y), (x, 1 - y), (1 - x, 1 - y)]

        def slot(i, px, py, pc):
            return outs[i].at[4 * px + 2 * py + pc]

        def copy(i, k, block, to, src=None):
            dst = slot(i, *block)
            return _remote(dst if src is None else src, dst, send_sems.at[i, k], recv_sems.at[i, k], to)

        locals_ = []
        for i in range(n):
            lc = pltpu.make_async_copy(ins[i], slot(i, x, y, c), local_sems.at[i])
            lc.start()
            locals_.append(lc)
        sends = []
        for j, chip in enumerate(chips):
            for i in range(n):
                sends.append(copy(i, 1 + j, (x, y, c), (*chip, c), src=ins[i]))
        for i in range(n):
            sends.append(copy(i, 0, (x, y, c), sibling, src=ins[i]))
        for cp in sends:
            cp.start()
        for j, chip in enumerate(chips):
            for i in range(n):
                copy(i, 1 + j, (*chip, c), (x, y, c)).wait_recv()
                fwd = copy(i, 4 + j, (*chip, c), sibling)
                fwd.start()
                sends.append(fwd)
        for i in range(n):
            copy(i, 0, (x, y, 1 - c), (x, y, c)).wait_recv()
        for j, chip in enumerate(chips):
            for i in range(n):
                copy(i, 4 + j, (*chip, 1 - c), (x, y, c)).wait_recv()
        for cp in sends:
            cp.wait_send()
        for lc in locals_:
            lc.wait()

    hbm = pl.BlockSpec(memory_space=pltpu.HBM)
    return pl.pallas_call(
        body, name=name, out_shape=tuple(out_shapes), in_specs=[hbm] * n, out_specs=tuple([hbm] * n),
        scratch_shapes=[pltpu.SemaphoreType.DMA((n, 7)), pltpu.SemaphoreType.DMA((n, 7)),
                        pltpu.SemaphoreType.DMA((n,))],
    )(*arrays)


def _sibling_swap(arrays, *, name):
    n = len(arrays)
    out_shapes = [jax.ShapeDtypeStruct(a.shape[1:], a.dtype) for a in arrays]

    def body(*refs):
        ins, outs = refs[:n], refs[n:2 * n]
        send_sems, recv_sems = refs[2 * n:]
        x, y, c = lax.axis_index("x"), lax.axis_index("y"), lax.axis_index("c")
        copies = [_remote(ins[i].at[1 - c], outs[i], send_sems.at[i], recv_sems.at[i], (x, y, 1 - c))
                  for i in range(n)]
        for cp in copies:
            cp.start()
        for cp in copies:
            cp.wait_recv()
        for cp in copies:
            cp.wait_send()

    hbm = pl.BlockSpec(memory_space=pltpu.HBM)
    return pl.pallas_call(
        body, name=name, out_shape=tuple(out_shapes), in_specs=[hbm] * n, out_specs=tuple([hbm] * n),
        scratch_shapes=[pltpu.SemaphoreType.DMA((n,)), pltpu.SemaphoreType.DMA((n,))],
    )(*arrays)


def _chips_all_to_all(arrays, *, name):
    n = len(arrays)
    out_shapes = [jax.ShapeDtypeStruct(a.shape, a.dtype) for a in arrays]

    def body(*refs):
        ins, outs = refs[:n], refs[n:2 * n]
        send_sems, recv_sems, local_sems = refs[2 * n:]
        x, y, c = lax.axis_index("x"), lax.axis_index("y"), lax.axis_index("c")
        mine = 2 * x + y
        chips = [(1 - x, y), (x, 1 - y), (1 - x, 1 - y)]
        locals_ = []
        for i in range(n):
            lc = pltpu.make_async_copy(ins[i].at[mine], outs[i].at[mine], local_sems.at[i])
            lc.start()
            locals_.append(lc)
        sends = [_remote(ins[i].at[2 * chip[0] + chip[1]], outs[i].at[mine], send_sems.at[i, j], recv_sems.at[i, j],
                         (*chip, c)) for j, chip in enumerate(chips) for i in range(n)]
        for cp in sends:
            cp.start()
        for j, chip in enumerate(chips):
            for i in range(n):
                q = 2 * chip[0] + chip[1]
                _remote(ins[i].at[q], outs[i].at[q], send_sems.at[i, j], recv_sems.at[i, j], (*chip, c)).wait_recv()
        for cp in sends:
            cp.wait_send()
        for lc in locals_:
            lc.wait()

    hbm = pl.BlockSpec(memory_space=pltpu.HBM)
    return pl.pallas_call(
        body, name=name, out_shape=tuple(out_shapes), in_specs=[hbm] * n, out_specs=tuple([hbm] * n),
        scratch_shapes=[pltpu.SemaphoreType.DMA((n, 3)), pltpu.SemaphoreType.DMA((n, 3)),
                        pltpu.SemaphoreType.DMA((n,))],
    )(*arrays)


def _pair_add(mine, got, *, name):
    _, nq, R, C = mine.shape
    tr = R
    for cand in (512, 256, 128, 64, 32, 16):
        if R % cand == 0 and cand * C * 2 * 3 * 2 <= 12 * 1024 * 1024:
            tr = cand
            break

    def body(core_ref, m_ref, g_ref, o_ref):
        o_ref[0] = (m_ref[0, 0].astype(F32) + g_ref[0].astype(F32)).astype(BF16)

    core = lax.axis_index("c").astype(jnp.int32).reshape(1)
    return pl.pallas_call(
        body, name=name,
        grid_spec=pltpu.PrefetchScalarGridSpec(
            num_scalar_prefetch=1, grid=(nq, R // tr),
            in_specs=[pl.BlockSpec((1, 1, tr, C), lambda q, i, core_ref: (core_ref[0], q, i, 0)),
                      pl.BlockSpec((1, tr, C), lambda q, i, core_ref: (q, i, 0))],
            out_specs=pl.BlockSpec((1, tr, C), lambda q, i, core_ref: (q, i, 0))),
        out_shape=jax.ShapeDtypeStruct((nq, R, C), BF16),
        compiler_params=_cp(("parallel", "parallel"), 40),
    )(core, mine, got)


def _mm(a, b, *, ta=False, tb=False, tm, tn, tk, out_dtype=F32, name):
    M = a.shape[1] if ta else a.shape[0]
    K = a.shape[0] if ta else a.shape[1]
    N = b.shape[0] if tb else b.shape[1]
    tm, tn, tk = min(tm, M), min(tn, N), min(tk, K)
    assert M % tm == 0 and N % tn == 0 and K % tk == 0, (name, M, N, K, tm, tn, tk)
    nk = K // tk
    dims = (((0 if ta else 1,), (1 if tb else 0,)), ((), ()))

    def body(a_ref, b_ref, o_ref, acc_ref):
        k = pl.program_id(2)

        @pl.when(k == 0)
        def _():
            acc_ref[...] = jnp.zeros_like(acc_ref)

        acc_ref[...] += lax.dot_general(a_ref[...].astype(BF16), b_ref[...].astype(BF16), dims,
                                        preferred_element_type=F32)

        @pl.when(k == nk - 1)
        def _():
            o_ref[...] = acc_ref[...].astype(out_dtype)

    a_spec = pl.BlockSpec((tk, tm), lambda i, j, k: (k, i)) if ta else pl.BlockSpec((tm, tk), lambda i, j, k: (i, k))
    b_spec = pl.BlockSpec((tn, tk), lambda i, j, k: (j, k)) if tb else pl.BlockSpec((tk, tn), lambda i, j, k: (k, j))
    blk = 2 * (tm * tk * a.dtype.itemsize + tk * tn * b.dtype.itemsize + tm * tn * jnp.dtype(out_dtype).itemsize)
    return pl.pallas_call(
        body, name=name, grid=(M // tm, N // tn, nk), in_specs=[a_spec, b_spec],
        out_specs=pl.BlockSpec((tm, tn), lambda i, j, k: (i, j)),
        out_shape=jax.ShapeDtypeStruct((M, N), out_dtype),
        scratch_shapes=[pltpu.VMEM((tm, tn), F32)],
        compiler_params=_cp(("parallel", "parallel", "arbitrary"), (blk + tm * tn * 4) // (1024 * 1024) + 12),
    )(a, b)


def _ffn_up(h, wgu, *, tm, tk, name):
    S, D = h.shape
    F2 = wgu.shape[1]
    tm = min(tm, S)
    tn = FF_TILE
    nk = D // tk

    def body(h_ref, w_ref, au_ref, s_ref, acc_ref):
        k = pl.program_id(2)

        @pl.when(k == 0)
        def _():
            acc_ref[...] = jnp.zeros_like(acc_ref)

        acc_ref[...] += jnp.dot(h_ref[...], w_ref[...], preferred_element_type=F32)

        @pl.when(k == nk - 1)
        def _():
            acc = acc_ref[...]
            a = acc[:, :tn]
            u = acc[:, tn:]
            au_ref[...] = acc.astype(BF16)
            s_ref[...] = (a * jax.nn.sigmoid(a) * u).astype(BF16)

    return pl.pallas_call(
        body, name=name, grid=(S // tm, F2 // (2 * tn), nk),
        in_specs=[pl.BlockSpec((tm, tk), lambda i, j, k: (i, k)), pl.BlockSpec((tk, 2 * tn), lambda i, j, k: (k, j))],
        out_specs=(pl.BlockSpec((tm, 2 * tn), lambda i, j, k: (i, j)), pl.BlockSpec((tm, tn), lambda i, j, k: (i, j))),
        out_shape=(jax.ShapeDtypeStruct((S, F2), BF16), jax.ShapeDtypeStruct((S, F2 // 2), BF16)),
        scratch_shapes=[pltpu.VMEM((tm, 2 * tn), F32)],
        compiler_params=_cp(("parallel", "parallel", "arbitrary"), 40),
    )(h, wgu)


def _ffn_down_bwd(df, wd, au, *, tm, tk, name):
    S, D = df.shape
    F = wd.shape[0]
    tm = min(tm, S)
    tn = FF_TILE
    nk = D // tk

    def body(df_ref, w_ref, au_ref, dau_ref, acc_ref):
        k = pl.program_id(2)

        @pl.when(k == 0)
        def _():
            acc_ref[...] = jnp.zeros_like(acc_ref)

        acc_ref[...] += lax.dot_general(df_ref[...], w_ref[...], (((1,), (1,)), ((), ())), preferred_element_type=F32)

        @pl.when(k == nk - 1)
        def _():
            ds = acc_ref[...]
            au_v = au_ref[...].astype(F32)
            a = au_v[:, :tn]
            u = au_v[:, tn:]
            sg = jax.nn.sigmoid(a)
            da = ds * u * (sg * (1.0 + a * (1.0 - sg)))
            du = ds * (a * sg)
            dau_ref[:, :tn] = da.astype(BF16)
            dau_ref[:, tn:] = du.astype(BF16)

    return pl.pallas_call(
        body, name=name, grid=(S // tm, F // tn, nk),
        in_specs=[pl.BlockSpec((tm, tk), lambda i, j, k: (i, k)), pl.BlockSpec((tn, tk), lambda i, j, k: (j, k)),
                  pl.BlockSpec((tm, 2 * tn), lambda i, j, k: (i, j))],
        out_specs=pl.BlockSpec((tm, 2 * tn), lambda i, j, k: (i, j)),
        out_shape=jax.ShapeDtypeStruct((S, 2 * F), BF16),
        scratch_shapes=[pltpu.VMEM((tm, tn), F32)],
        compiler_params=_cp(("parallel", "parallel", "arbitrary"), 40),
    )(df, wd, au)


def _fold8(x):
    tm, w = x.shape
    return jnp.sum(x.reshape(tm // 8, 8, w), axis=0)


def _row_spec(tm, w):
    return pl.BlockSpec((tm, w), lambda i: (i, 0))


def _vec_spec(rows, w):
    return pl.BlockSpec((rows, w), lambda i: (0, 0))


def _pre_norm_mod(x, gain, shift, scale1p, *, name):
    S, D = x.shape
    tm = ROW_TILE

    def body(x_ref, g_ref, sh_ref, sc_ref, h_ref):
        xv = x_ref[...]
        rinv = lax.rsqrt(jnp.mean(xv * xv, axis=-1, keepdims=True) + NORM_EPS)
        h_ref[...] = ((xv * rinv) * g_ref[...] * sc_ref[...] + sh_ref[...]).astype(BF16)

    return pl.pallas_call(
        body, name=name, grid=(S // tm,),
        in_specs=[_row_spec(tm, D), _vec_spec(1, D), _vec_spec(1, D), _vec_spec(1, D)],
        out_specs=_row_spec(tm, D), out_shape=jax.ShapeDtypeStruct((S, D), BF16),
        compiler_params=_cp(("parallel",), 32),
    )(x, gain, shift, scale1p)


def _pre_norm_mod_bwd(dh, x, dres, gain, scale1p, *, name):
    S, D = x.shape
    tm = ROW_TILE
    n = S // tm

    def body(dh_ref, x_ref, dr_ref, g_ref, sc_ref, dx_ref, sums_ref, acc_ref):
        i = pl.program_id(0)

        @pl.when(i == 0)
        def _():
            acc_ref[...] = jnp.zeros_like(acc_ref)

        xv = x_ref[...]
        dhv = dh_ref[...]
        g = g_ref[...]
        rinv = lax.rsqrt(jnp.mean(xv * xv, axis=-1, keepdims=True) + NORM_EPS)
        xn = xv * rinv
        dn = dhv * sc_ref[...]
        dxn = dn * g
        dx_ref[...] = dr_ref[...] + rinv * (dxn - xn * jnp.mean(dxn * xn, axis=-1, keepdims=True))
        acc_ref[0] += _fold8(dhv)
        acc_ref[1] += _fold8(dhv * (xn * g))
        acc_ref[2] += _fold8(dn * xn)

        @pl.when(i == n - 1)
        def _():
            for q in range(3):
                sums_ref[q:q + 1, :] = jnp.sum(acc_ref[q], axis=0, keepdims=True)

    return pl.pallas_call(
        body, name=name, grid=(n,),
        in_specs=[_row_spec(tm, D), _row_spec(tm, D), _row_spec(tm, D), _vec_spec(1, D), _vec_spec(1, D)],
        out_specs=(_row_spec(tm, D), _vec_spec(3, D)),
        out_shape=(jax.ShapeDtypeStruct((S, D), F32), jax.ShapeDtypeStruct((3, D), F32)),
        scratch_shapes=[pltpu.VMEM((3, 8, D), F32)],
        compiler_params=_cp(("arbitrary",), 40),
    )(dh, x, dres, gain, scale1p)


def _post_norm_res(x, f, gain, gw, *, name):
    S, D = x.shape
    tm = ROW_TILE

    def body(x_ref, f_ref, g_ref, gw_ref, o_ref):
        fv = f_ref[...]
        rinv = lax.rsqrt(jnp.mean(fv * fv, axis=-1, keepdims=True) + NORM_EPS)
        o_ref[...] = x_ref[...] + gw_ref[...] * ((fv * rinv) * g_ref[...])

    return pl.pallas_call(
        body, name=name, grid=(S // tm,),
        in_specs=[_row_spec(tm, D), _row_spec(tm, D), _vec_spec(1, D), _vec_spec(1, D)],
        out_specs=_row_spec(tm, D), out_shape=jax.ShapeDtypeStruct((S, D), F32),
        compiler_params=_cp(("parallel",), 32),
    )(x, f, gain, gw)


def _post_norm_res_bwd(dxo, f, gain, gw, weight, *, name):
    S, D = f.shape
    tm = ROW_TILE
    n = S // tm

    def body(d_ref, f_ref, g_ref, gw_ref, df_ref, sums_ref, acc_ref):
        i = pl.program_id(0)

        @pl.when(i == 0)
        def _():
            acc_ref[...] = jnp.zeros_like(acc_ref)

        fv = f_ref[...]
        dv = d_ref[...]
        g = g_ref[...]
        rinv = lax.rsqrt(jnp.mean(fv * fv, axis=-1, keepdims=True) + NORM_EPS)
        fh = fv * rinv
        dy = dv * gw_ref[...]
        dfh = dy * g
        df_ref[...] = (rinv * (dfh - fh * jnp.mean(dfh * fh, axis=-1, keepdims=True))).astype(BF16)
        acc_ref[0] += _fold8(weight * dv * (fh * g))
        acc_ref[1] += _fold8(dy * fh)

        @pl.when(i == n - 1)
        def _():
            for q in range(2):
                sums_ref[q:q + 1, :] = jnp.sum(acc_ref[q], axis=0, keepdims=True)

    return pl.pallas_call(
        body, name=name, grid=(n,),
        in_specs=[_row_spec(tm, D), _row_spec(tm, D), _vec_spec(1, D), _vec_spec(1, D)],
        out_specs=(_row_spec(tm, D), _vec_spec(2, D)),
        out_shape=(jax.ShapeDtypeStruct((S, D), BF16), jax.ShapeDtypeStruct((2, D), F32)),
        scratch_shapes=[pltpu.VMEM((2, 8, D), F32)],
        compiler_params=_cp(("arbitrary",), 40),
    )(dxo, f, gain, gw)


def _loss_head(y, target, *, name):
    S, D = y.shape
    tm = ROW_TILE

    def body(y_ref, t_ref, l_ref, dy_ref):
        i = pl.program_id(0)

        @pl.when(i == 0)
        def _():
            l_ref[...] = jnp.zeros_like(l_ref)

        err = y_ref[...] - t_ref[...]
        dy_ref[...] = err * (1.0 / D)
        row = jnp.sum(err * err, axis=-1, keepdims=True) * (0.5 / D)
        l_ref[...] += jnp.sum(row, axis=0, keepdims=True)

    return pl.pallas_call(
        body, name=name, grid=(S // tm,),
        in_specs=[_row_spec(tm, D), _row_spec(tm, D)],
        out_specs=(_vec_spec(1, 1), _row_spec(tm, D)),
        out_shape=(jax.ShapeDtypeStruct((1, 1), F32), jax.ShapeDtypeStruct((S, D), F32)),
        compiler_params=_cp(("arbitrary",), 32),
    )(y, target)


def _shift_down(z, j, row):
    return jnp.where(row >= j, pltpu.roll(z, j, 0), 0.0)


def _shift_up(z, j, row, n):
    return jnp.where(row < n - j, pltpu.roll(z, n - j, 0), 0.0)


def _pool_fwd(p, pool_w, pool_scale, *, name):
    S = p.shape[0]
    C = POOL_GROUP

    def body(u_ref, w_ref, sc_ref, o_ref, y_ref):
        g = pl.program_id(0)
        u = u_ref[...]
        row = lax.broadcasted_iota(jnp.int32, (S, C), 0)
        s1 = u + _shift_down(u, 1, row)
        s2 = s1 + _shift_down(s1, 2, row)
        s3 = s2 + _shift_down(s2, 4, row)
        s4 = s3 + _shift_down(s3, 8, row)
        gi = jnp.zeros((S, C), jnp.int32) + g
        win = jnp.where(gi == 0, s1, jnp.where(gi == 1, s2, jnp.where(gi == 2, s3, s4)))
        width = jnp.where(gi == 0, 2, jnp.where(gi == 1, 4, jnp.where(gi == 2, 8, 16)))
        count = jnp.minimum(row + 1, width).astype(F32)
        o = win / count - u
        o_ref[...] = o
        y_ref[...] = jnp.dot(o.astype(BF16), w_ref[0].astype(BF16), preferred_element_type=F32) * sc_ref[...]

    col = pl.BlockSpec((S, C), lambda g: (0, g))
    return pl.pallas_call(
        body, name=name, grid=(POOL_GROUPS,),
        in_specs=[col, pl.BlockSpec((1, C, C), lambda g: (g, 0, 0)), pl.BlockSpec((1, C), lambda g: (0, g))],
        out_specs=(col, col),
        out_shape=(jax.ShapeDtypeStruct((S, POOL_GROUPS * C), F32), jax.ShapeDtypeStruct((S, POOL_GROUPS * C), F32)),
        compiler_params=_cp(("parallel",), 48),
    )(p, pool_w, pool_scale)


def _pool_bwd(dcat, o, pool_w, pool_scale, *, name):
    S = o.shape[0]
    C = POOL_GROUP

    def body(dy_ref, o_ref, w_ref, sc_ref, du_ref, dw_ref, dsc_ref):
        g = pl.program_id(0)
        dy = dy_ref[...]
        ob = o_ref[...].astype(BF16)
        wb = w_ref[0].astype(BF16)
        mixed = jnp.dot(ob, wb, preferred_element_type=F32)
        dsc_ref[...] = jnp.sum(_fold8(dy * mixed), axis=0, keepdims=True)
        dmix = (dy * sc_ref[...]).astype(BF16)
        dw_ref[0] = lax.dot_general(ob, dmix, (((0,), (0,)), ((), ())), preferred_element_type=F32)
        do = lax.dot_general(dmix, wb, (((1,), (1,)), ((), ())), preferred_element_type=F32)
        row = lax.broadcasted_iota(jnp.int32, (S, C), 0)
        gi = jnp.zeros((S, C), jnp.int32) + g
        width = jnp.where(gi == 0, 2, jnp.where(gi == 1, 4, jnp.where(gi == 2, 8, 16)))
        z = do / jnp.minimum(row + 1, width).astype(F32)
        s1 = z + _shift_up(z, 1, row, S)
        s2 = s1 + _shift_up(s1, 2, row, S)
        s3 = s2 + _shift_up(s2, 4, row, S)
        s4 = s3 + _shift_up(s3, 8, row, S)
        win = jnp.where(gi == 0, s1, jnp.where(gi == 1, s2, jnp.where(gi == 2, s3, s4)))
        du_ref[...] = (win - do).astype(BF16)

    col = pl.BlockSpec((S, C), lambda g: (0, g))
    return pl.pallas_call(
        body, name=name, grid=(POOL_GROUPS,),
        in_specs=[col, col, pl.BlockSpec((1, C, C), lambda g: (g, 0, 0)), pl.BlockSpec((1, C), lambda g: (0, g))],
        out_specs=(col, pl.BlockSpec((1, C, C), lambda g: (g, 0, 0)), pl.BlockSpec((1, C), lambda g: (0, g))),
        out_shape=(jax.ShapeDtypeStruct((S, POOL_GROUPS * C), BF16), jax.ShapeDtypeStruct((POOL_GROUPS, C, C), F32),
                   jax.ShapeDtypeStruct((1, POOL_GROUPS * C), F32)),
        compiler_params=_cp(("parallel",), 48),
    )(dcat, o, pool_w, pool_scale)


def _block_ones():
    r = lax.broadcasted_iota(jnp.int32, (128, 128), 0) // HEAD
    c = lax.broadcasted_iota(jnp.int32, (128, 128), 1) // HEAD
    return jnp.where(r == c, 1.0, 0.0).astype(BF16)


def _segsum(x, bd):
    outs = []
    for j in range(x.shape[1] // 128):
        xs = x[:, j * 128:(j + 1) * 128]
        hi = xs.astype(BF16)
        lo = (xs - hi.astype(F32)).astype(BF16)
        outs.append(jnp.dot(hi, bd, preferred_element_type=F32) + jnp.dot(lo, bd, preferred_element_type=F32))
    return jnp.concatenate(outs, axis=1)


def _prep_common(q, qprev, first, mu, wl, w0, a0, kkw, kaw, R):
    tm = q.shape[0]
    row = lax.broadcasted_iota(jnp.int32, q.shape, 0)
    last = qprev[7:8, :] * first
    prev = jnp.where(row == 0, last, pltpu.roll(q, 1, 0))
    ps = q + mu * (prev - q)
    r = ps[:, 0:R]
    k = ps[:, R:2 * R]
    v = ps[:, 2 * R:3 * R]
    lo_in = ps[:, 3 * R:3 * R + LORA_PAD]
    lane = lax.broadcasted_iota(jnp.int32, (tm, LORA_PAD), 1)
    m_w = lane < LORA_W
    m_a = lane < LORA_W + LORA_A
    m_g = lane < LORA_W + LORA_A + LORA_G
    act = jnp.where(m_w, jnp.tanh(lo_in), jnp.where(m_a, lo_in, jnp.where(m_g, jax.nn.sigmoid(lo_in), 0.0)))
    lo = jnp.dot(act.astype(BF16), wl, preferred_element_type=F32)
    wpre = w0 + lo[:, 0:R]
    apre = a0 + lo[:, R:2 * R]
    g = lo[:, 2 * R:3 * R]
    neg = -wpre
    softplus = jnp.maximum(neg, 0.0) + jnp.log(1.0 + jnp.exp(-jnp.abs(neg)))
    wlog = -softplus - 0.5
    ew = jnp.exp(wlog)
    decay = jnp.exp(-ew)
    a = jax.nn.sigmoid(apre)
    kk = k * kkw
    bd = _block_ones()
    n2 = _segsum(kk * kk, bd)
    nrm = jnp.maximum(jnp.sqrt(n2), 1e-12)
    kap = kk / nrm
    kmul = 1.0 + (a - 1.0) * kaw
    k2 = k * kmul
    return dict(prev=prev, r=r, k=k, v=v, act=act, m_w=m_w, m_a=m_a, m_g=m_g, wpre=wpre, g=g, ew=ew, decay=decay,
                a=a, n2=n2, nrm=nrm, kap=kap, kmul=kmul, k2=k2, bd=bd)


def _prev_rows_spec(tm, w):
    return pl.BlockSpec((8, w), lambda i: (jnp.maximum(i * (tm // 8) - 1, 0), 0))


def _rwkv_prep(q, mu, wl, w0, a0, kkw, kaw, *, name):
    S, QW = q.shape
    R = w0.shape[1]
    tm = ROW_TILE // 2

    def body(q_ref, qp_ref, mu_ref, wl_ref, w0_ref, a0_ref, kk_ref, ka_ref, r_ref, w_ref, k_ref, v_ref, kap_ref,
             b_ref, g_ref):
        first = jnp.where(pl.program_id(0) > 0, 1.0, 0.0)
        t = _prep_common(q_ref[...], qp_ref[...], first, mu_ref[...], wl_ref[...], w0_ref[...], a0_ref[...],
                         kk_ref[...], ka_ref[...], R)
        r_ref[...] = t["r"]
        w_ref[...] = t["decay"]
        k_ref[...] = t["k2"]
        v_ref[...] = t["v"]
        kap_ref[...] = t["kap"]
        b_ref[...] = t["kap"] * t["a"]
        g_ref[...] = t["g"]

    vec = _vec_spec(1, R)
    return pl.pallas_call(
        body, name=name, grid=(S // tm,),
        in_specs=[_row_spec(tm, QW), _prev_rows_spec(tm, QW), _vec_spec(1, QW), _vec_spec(LORA_PAD, 3 * R), vec, vec,
                  vec, vec],
        out_specs=tuple([_row_spec(tm, R)] * 7),
        out_shape=tuple([jax.ShapeDtypeStruct((S, R), F32)] * 7),
        compiler_params=_cp(("parallel",), 48),
    )(q, q, mu, wl, w0, a0, kkw, kaw)


def _rwkv_prep_bwd(q, mu, wl, w0, a0, kkw, kaw, grads, *, name):
    S, QW = q.shape
    R = w0.shape[1]
    tm = ROW_TILE // 2
    n = S // tm

    def body(q_ref, qp_ref, mu_ref, wl_ref, w0_ref, a0_ref, kk_ref, ka_ref, dr_ref, dw_ref, dk2_ref, dv_ref, dkap_ref,
             db_ref, dg_ref, drb_ref, dk2b_ref, dvb_ref, dps_ref, dwl_ref, sums_ref, acc_ref):
        i = pl.program_id(0)

        @pl.when(i == 0)
        def _():
            acc_ref[...] = jnp.zeros_like(acc_ref)
            dwl_ref[...] = jnp.zeros_like(dwl_ref)

        first = jnp.where(i > 0, 1.0, 0.0)
        wl = wl_ref[...]
        kkw = kk_ref[...]
        kaw = ka_ref[...]
        t = _prep_common(q_ref[...], qp_ref[...], first, mu_ref[...], wl, w0_ref[...], a0_ref[...], kkw, kaw, R)
        a, kap, k, act = t["a"], t["kap"], t["k"], t["act"]
        db = db_ref[...]
        dk2 = dk2_ref[...] + dk2b_ref[...]
        dkap = dkap_ref[...] + db * a
        da = db * kap + dk2 * k * kaw
        dk = dk2 * t["kmul"]
        proj = jnp.where(jnp.sqrt(t["n2"]) > 1e-12, _segsum(kap * dkap, t["bd"]), 0.0)
        dkk = (dkap - kap * proj) / t["nrm"]
        dk = dk + dkk * kkw
        dapre = da * a * (1.0 - a)
        dwlog = dw_ref[...] * t["decay"] * (-t["ew"])
        dwpre = dwlog * jax.nn.sigmoid(-t["wpre"])
        acc_ref[0] += _fold8(dwpre)
        acc_ref[1] += _fold8(dapre)
        acc_ref[2] += _fold8(dkk * k)
        acc_ref[3] += _fold8(dk2 * k * (a - 1.0))
        dlo = jnp.concatenate([dwpre, dapre, dg_ref[...]], axis=1).astype(BF16)
        dwl_ref[...] += lax.dot_general(act.astype(BF16), dlo, (((0,), (0,)), ((), ())), preferred_element_type=F32)
        dact = lax.dot_general(dlo, wl, (((1,), (1,)), ((), ())), preferred_element_type=F32)
        dlin = jnp.where(t["m_w"], dact * (1.0 - act * act),
                         jnp.where(t["m_a"], dact, jnp.where(t["m_g"], dact * act * (1.0 - act), 0.0)))
        dps_ref[:, 0:R] = dr_ref[...] + drb_ref[...]
        dps_ref[:, R:2 * R] = dk
        dps_ref[:, 2 * R:3 * R] = dv_ref[...] + dvb_ref[...]
        dps_ref[:, 3 * R:3 * R + LORA_PAD] = dlin
        dps_ref[:, 3 * R + LORA_PAD:] = jnp.zeros((tm, QW - 3 * R - LORA_PAD), F32)

        @pl.when(i == n - 1)
        def _():
            for j in range(4):
                sums_ref[j:j + 1, :] = jnp.sum(acc_ref[j], axis=0, keepdims=True)

    vec = _vec_spec(1, R)
    return pl.pallas_call(
        body, name=name, grid=(n,),
        in_specs=[_row_spec(tm, QW), _prev_rows_spec(tm, QW), _vec_spec(1, QW), _vec_spec(LORA_PAD, 3 * R), vec, vec,
                  vec, vec] + [_row_spec(tm, R)] * 10,
        out_specs=(_row_spec(tm, QW), _vec_spec(LORA_PAD, 3 * R), _vec_spec(4, R)),
        out_shape=(jax.ShapeDtypeStruct((S, QW), F32), jax.ShapeDtypeStruct((LORA_PAD, 3 * R), F32),
                   jax.ShapeDtypeStruct((4, R), F32)),
        scratch_shapes=[pltpu.VMEM((4, 8, R), F32)],
        compiler_params=_cp(("arbitrary",), 56),
    )(q, q, mu, wl, w0, a0, kkw, kaw, *grads)


def _tshift_bwd(dps, q, mu, *, name):
    S, QW = q.shape
    tm = ROW_TILE // 2
    n = S // tm

    def body(d_ref, dn_ref, q_ref, qp_ref, mu_ref, dq_ref, dmu_ref, acc_ref):
        i = pl.program_id(0)

        @pl.when(i == 0)
        def _():
            acc_ref[...] = jnp.zeros_like(acc_ref)

        mu = mu_ref[...]
        d = d_ref[...]
        qv = q_ref[...]
        row = lax.broadcasted_iota(jnp.int32, d.shape, 0)
        first = jnp.where(i > 0, 1.0, 0.0)
        notlast = jnp.where(i < n - 1, 1.0, 0.0)
        prev = jnp.where(row == 0, qp_ref[7:8, :] * first, pltpu.roll(qv, 1, 0))
        z = d * mu
        nxt = jnp.where(row == tm - 1, dn_ref[0:1, :] * mu * notlast, pltpu.roll(z, tm - 1, 0))
        dq_ref[...] = (d * (1.0 - mu) + nxt).astype(BF16)
        acc_ref[...] += _fold8(d * (prev - qv))

        @pl.when(i == n - 1)
        def _():
            dmu_ref[...] = jnp.sum(acc_ref[...], axis=0, keepdims=True)

    nblk8 = S // 8
    next_spec = pl.BlockSpec((8, QW), lambda i: (jnp.minimum((i + 1) * (tm // 8), nblk8 - 1), 0))
    return pl.pallas_call(
        body, name=name, grid=(n,),
        in_specs=[_row_spec(tm, QW), next_spec, _row_spec(tm, QW), _prev_rows_spec(tm, QW), _vec_spec(1, QW)],
        out_specs=(_row_spec(tm, QW), _vec_spec(1, QW)),
        out_shape=(jax.ShapeDtypeStruct((S, QW), BF16), jax.ShapeDtypeStruct((1, QW), F32)),
        scratch_shapes=[pltpu.VMEM((8, QW), F32)],
        compiler_params=_cp(("arbitrary",), 48),
    )(dps, dps, q, q, mu)


def _post_common(ysc, r, k2, v, lnw, lnb, rk):
    bd = _block_ones()
    mean = _segsum(ysc, bd) * (1.0 / HEAD)
    d = ysc - mean
    var = _segsum(d * d, bd) * (1.0 / HEAD)
    rstd = lax.rsqrt(var + LN_X_EPS)
    yh = d * rstd
    rkk = _segsum(r * k2 * rk, bd)
    z = yh * lnw + lnb + rkk * v
    return bd, rstd, yh, rkk, z


def _rwkv_post(ysc, r, k2, v, g, ypool, lnw, lnb, rk, *, name):
    S, R = ysc.shape
    PW = ypool.shape[1]
    tm = ROW_TILE

    def body(y_ref, r_ref, k_ref, v_ref, g_ref, yp_ref, lw_ref, lb_ref, rk_ref, cat_ref):
        _, _, _, _, z = _post_common(y_ref[...], r_ref[...], k_ref[...], v_ref[...], lw_ref[...], lb_ref[...],
                                     rk_ref[...])
        cat_ref[:, 0:PW] = yp_ref[...].astype(BF16)
        cat_ref[:, PW:] = (z * g_ref[...]).astype(BF16)

    vec = _vec_spec(1, R)
    return pl.pallas_call(
        body, name=name, grid=(S // tm,),
        in_specs=[_row_spec(tm, R)] * 5 + [_row_spec(tm, PW), vec, vec, vec],
        out_specs=_row_spec(tm, PW + R), out_shape=jax.ShapeDtypeStruct((S, PW + R), BF16),
        compiler_params=_cp(("parallel",), 48),
    )(ysc, r, k2, v, g, ypool, lnw, lnb, rk)


def _rwkv_post_bwd(dcat, ysc, r, k2, v, g, lnw, lnb, rk, *, name):
    S, R = ysc.shape
    tm = ROW_TILE
    n = S // tm

    def body(d_ref, y_ref, r_ref, k_ref, v_ref, g_ref, lw_ref, lb_ref, rk_ref, dy_ref, dg_ref, drb_ref, dkb_ref,
             dvb_ref, sums_ref, acc_ref):
        i = pl.program_id(0)

        @pl.when(i == 0)
        def _():
            acc_ref[...] = jnp.zeros_like(acc_ref)

        rv, kv, vv, lw, rkw = r_ref[...], k_ref[...], v_ref[...], lw_ref[...], rk_ref[...]
        bd, rstd, yh, rkk, z = _post_common(y_ref[...], rv, kv, vv, lw, lb_ref[...], rkw)
        dyr = d_ref[...]
        dg_ref[...] = dyr * z
        dz = dyr * g_ref[...]
        dyh = dz * lw
        dy_ref[...] = rstd * (dyh - _segsum(dyh, bd) * (1.0 / HEAD) - yh * (_segsum(dyh * yh, bd) * (1.0 / HEAD)))
        dvb_ref[...] = dz * rkk
        drkk = _segsum(dz * vv, bd)
        drb_ref[...] = drkk * kv * rkw
        dkb_ref[...] = drkk * rv * rkw
        acc_ref[0] += _fold8(dz * yh)
        acc_ref[1] += _fold8(dz)
        acc_ref[2] += _fold8(drkk * rv * kv)

        @pl.when(i == n - 1)
        def _():
            for j in range(3):
                sums_ref[j:j + 1, :] = jnp.sum(acc_ref[j], axis=0, keepdims=True)

    vec = _vec_spec(1, R)
    dspec = _row_spec(tm, R)
    return pl.pallas_call(
        body, name=name, grid=(n,),
        in_specs=[dspec] + [_row_spec(tm, R)] * 5 + [vec, vec, vec],
        out_specs=tuple([_row_spec(tm, R)] * 5) + (_vec_spec(3, R),),
        out_shape=tuple([jax.ShapeDtypeStruct((S, R), F32)] * 5) + (jax.ShapeDtypeStruct((3, R), F32),),
        scratch_shapes=[pltpu.VMEM((3, 8, R), F32)],
        compiler_params=_cp(("arbitrary",), 56),
    )(dcat, ysc, r, k2, v, g, lnw, lnb, rk)


def _half_sums(x, m_a):
    s_a = jnp.sum(jnp.where(m_a, x, 0.0), axis=1, keepdims=True)
    s_b = jnp.sum(jnp.where(m_a, 0.0, x), axis=1, keepdims=True)
    return s_a, s_b


SEL_ROWS = 32


def _column_selector():
    row = lax.broadcasted_iota(jnp.int32, (SEL_ROWS, 8 * 128), 0)
    col = lax.broadcasted_iota(jnp.int32, (SEL_ROWS, 8 * 128), 1)
    return jnp.where((row < 24) & (row % 8 == col // 128), 1.0, 0.0).astype(BF16)


def _expand_columns(x, sel):
    hi = x.astype(BF16).astype(F32)
    r1 = x - hi
    mid = r1.astype(BF16).astype(F32)
    lo = (r1 - mid).astype(BF16).astype(F32)
    terms = jnp.concatenate([hi, mid, lo, jnp.zeros_like(x)], axis=0).astype(BF16)
    return lax.dot_general(terms, sel, (((0,), (0,)), ((), ())), preferred_element_type=F32)


def _scan_fwd(r, w, k, v, kap, b, *, name):
    S, R = r.shape
    G, T = SCAN_G, SCAN_T
    NP = R // 128
    assert NP % G == 0 and S % T == 0
    GW = 128 * G

    def body(r_ref, w_ref, k_ref, v_ref, kap_ref, b_ref, sel_ref, y_ref, st_ref, s_scr, vc_scr, yt_scr):
        c = pl.program_id(1)

        @pl.when(c == 0)
        def _():
            s_scr[...] = jnp.zeros_like(s_scr)

        yt_scr[...] = jnp.zeros_like(yt_scr)
        lane = lax.broadcasted_iota(jnp.int32, (HEAD, 128), 1)
        m_a = lane < HEAD

        def block(tb, carry):
            t0 = pl.multiple_of(tb * 8, 8)
            rb, wb, kb = r_ref[pl.ds(t0, 8), :], w_ref[pl.ds(t0, 8), :], k_ref[pl.ds(t0, 8), :]
            pb, bb, vb = kap_ref[pl.ds(t0, 8), :], b_ref[pl.ds(t0, 8), :], v_ref[pl.ds(t0, 8), :]
            for g in range(G):
                vc_scr[g] = _expand_columns(vb[:, g * 128:(g + 1) * 128], sel_ref[...])

            def put_y(g, parts, hot_y):
                yt_scr[g, 0:HEAD, :] = jnp.where(hot_y, parts[0], yt_scr[g, 0:HEAD, :])
                yt_scr[g, HEAD:, :] = jnp.where(hot_y, parts[1], yt_scr[g, HEAD:, :])

            for j in range(8):
                t = t0 + j
                cols = slice(j * 128, (j + 1) * 128)
                sa_parts, y_parts = [], []
                for g in range(G):
                    sl = slice(g * 128, (g + 1) * 128)
                    sa_parts.append(_half_sums(s_scr[g] * pb[j:j + 1, sl], m_a))
                if j > 0:
                    for g in range(G):
                        sl = slice(g * 128, (g + 1) * 128)
                        y_parts.append(_half_sums(s_scr[g] * rb[j - 1:j, sl], m_a))
                for g in range(G):
                    sl = slice(g * 128, (g + 1) * 128)
                    sa = -jnp.where(m_a, sa_parts[g][0], sa_parts[g][1])
                    vcol = jnp.where(m_a, vc_scr[g, 0:HEAD, cols], vc_scr[g, HEAD:, cols])
                    st = s_scr[g] * wb[j:j + 1, sl] + sa * bb[j:j + 1, sl] + vcol * kb[j:j + 1, sl]
                    s_scr[g] = st
                    st_ref[g, t] = st
                if j > 0:
                    hot_y = lane == t - 1
                    for g in range(G):
                        put_y(g, y_parts[g], hot_y)
            hot_y = lane == t0 + 7
            for g in range(G):
                sl = slice(g * 128, (g + 1) * 128)
                put_y(g, _half_sums(s_scr[g] * rb[7:8, sl], m_a), hot_y)
            return carry

        lax.fori_loop(0, T // 8, block, 0)
        for g in range(G):
            y_ref[:, g * 128:(g + 1) * 128] = yt_scr[g].T[0:T, :]

    tspec = pl.BlockSpec((T, GW), lambda p, c: (c, p))
    sel_spec = pl.BlockSpec((SEL_ROWS, 8 * 128), lambda p, c: (0, 0))
    return pl.pallas_call(
        body, name=name, grid=(NP // G, S // T),
        in_specs=[tspec] * 6 + [sel_spec],
        out_specs=(tspec, pl.BlockSpec((G, T, HEAD, 128), lambda p, c: (p, c, 0, 0))),
        out_shape=(jax.ShapeDtypeStruct((S, R), F32), jax.ShapeDtypeStruct((NP, S, HEAD, 128), F32)),
        scratch_shapes=[pltpu.VMEM((G, HEAD, 128), F32), pltpu.VMEM((G, 128, 8 * 128), F32),
                        pltpu.VMEM((G, 128, 128), F32)],
        compiler_params=_cp(("parallel", "arbitrary"), 48),
    )(r, w, k, v, kap, b, _column_selector())


def _scan_bwd(r, w, k, v, kap, b, dy, states, *, name):
    S, R = r.shape
    G, T = SCAN_G, SCAN_T
    NP = R // 128
    NC = S // T
    GW = 128 * G

    def body(r_ref, w_ref, k_ref, v_ref, kap_ref, b_ref, dy_ref, st_ref, sp_ref, sel_ref, dr_ref, dw_ref, dk_ref,
             dv_ref, dkap_ref, db_ref, ds_scr, vc_scr, dyc_scr, dvt_scr, sa_scr):
        ci = pl.program_id(1)

        @pl.when(ci == 0)
        def _():
            ds_scr[...] = jnp.zeros_like(ds_scr)

        dvt_scr[...] = jnp.zeros_like(dvt_scr)
        lane = lax.broadcasted_iota(jnp.int32, (HEAD, 128), 1)
        m_a = lane < HEAD
        sub = lax.broadcasted_iota(jnp.int32, (8, 128), 0)
        zero_i = jnp.zeros((HEAD, 128), jnp.int32)
        has_prev = jnp.where(ci < NC - 1, 1.0, 0.0)

        def state_before(g, t):
            at_start = (zero_i + t) == 0
            return jnp.where(at_start, sp_ref[g, 0] * has_prev, st_ref[g, jnp.maximum(t - 1, 0)])

        def sa_sums(g, t, p_row):
            return _half_sums(state_before(g, t) * p_row, m_a)

        def put_sa(g, parts):
            sa_scr[g] = -jnp.where(m_a, parts[0], parts[1])

        pb_last = kap_ref[pl.ds(T - 8, 8), :]
        for g in range(G):
            put_sa(g, sa_sums(g, T - 1, pb_last[7:8, g * 128:(g + 1) * 128]))

        def block(it, carry):
            tb = T // 8 - 1 - it
            t0 = pl.multiple_of(tb * 8, 8)
            rb, wb, kb = r_ref[pl.ds(t0, 8), :], w_ref[pl.ds(t0, 8), :], k_ref[pl.ds(t0, 8), :]
            pb, bb = kap_ref[pl.ds(t0, 8), :], b_ref[pl.ds(t0, 8), :]
            vb, dyb = v_ref[pl.ds(t0, 8), :], dy_ref[pl.ds(t0, 8), :]
            pb_prev = kap_ref[pl.ds(pl.multiple_of(jnp.maximum(t0 - 8, 0), 8), 8), :]
            for g in range(G):
                vc_scr[g] = _expand_columns(vb[:, g * 128:(g + 1) * 128], sel_ref[...])
                dyc_scr[g] = _expand_columns(dyb[:, g * 128:(g + 1) * 128], sel_ref[...])
            outs = [[jnp.zeros((8, 128), F32) for _ in range(5)] for _ in range(G)]
            for j in range(7, -1, -1):
                t = t0 + j
                hot = lane == t
                cols = slice(j * 128, (j + 1) * 128)
                ds_parts, next_sa = [], []
                for g in range(G):
                    sl = slice(g * 128, (g + 1) * 128)
                    dycol = jnp.where(m_a, dyc_scr[g, 0:HEAD, cols], dyc_scr[g, HEAD:, cols])
                    ds = ds_scr[g] + dycol * rb[j:j + 1, sl]
                    ds_scr[g] = ds
                    ds_parts.append((_half_sums(ds * kb[j:j + 1, sl], m_a), _half_sums(ds * bb[j:j + 1, sl], m_a)))
                for g in range(G):
                    sl = slice(g * 128, (g + 1) * 128)
                    p_row = pb[j - 1:j, sl] if j > 0 else pb_prev[7:8, sl]
                    next_sa.append(sa_sums(g, jnp.maximum(t - 1, 0), p_row))
                for g in range(G):
                    sl = slice(g * 128, (g + 1) * 128)
                    w_r, p_r = wb[j:j + 1, sl], pb[j:j + 1, sl]
                    ds = ds_scr[g]
                    s_p = state_before(g, t)
                    dycol = jnp.where(m_a, dyc_scr[g, 0:HEAD, cols], dyc_scr[g, HEAD:, cols])
                    vcol = jnp.where(m_a, vc_scr[g, 0:HEAD, cols], vc_scr[g, HEAD:, cols])
                    sa = sa_scr[g]
                    dr_row = jnp.sum(st_ref[g, t] * dycol, axis=0, keepdims=True)
                    dk_row = jnp.sum(ds * vcol, axis=0, keepdims=True)
                    db_row = jnp.sum(ds * sa, axis=0, keepdims=True)
                    dw_row = jnp.sum(ds * s_p, axis=0, keepdims=True)
                    (dv_a, dv_b), (dsa_a, dsa_b) = ds_parts[g]
                    dvt_scr[g, 0:HEAD, :] = jnp.where(hot, dv_a, dvt_scr[g, 0:HEAD, :])
                    dvt_scr[g, HEAD:, :] = jnp.where(hot, dv_b, dvt_scr[g, HEAD:, :])
                    dsa = jnp.where(m_a, dsa_a, dsa_b)
                    dkap_row = -jnp.sum(s_p * dsa, axis=0, keepdims=True)
                    ds_scr[g] = ds * w_r - dsa * p_r
                    pick = sub == j
                    for q, row in enumerate((dr_row, dw_row, dk_row, dkap_row, db_row)):
                        outs[g][q] = jnp.where(pick, row, outs[g][q])
                for g in range(G):
                    put_sa(g, next_sa[g])
            for g in range(G):
                sl = slice(g * 128, (g + 1) * 128)
                for q, ref in enumerate((dr_ref, dw_ref, dk_ref, dkap_ref, db_ref)):
                    ref[pl.ds(t0, 8), sl] = outs[g][q]
            return carry

        lax.fori_loop(0, T // 8, block, 0)
        for g in range(G):
            dv_ref[:, g * 128:(g + 1) * 128] = dvt_scr[g].T[0:T, :]

    tspec = pl.BlockSpec((T, GW), lambda p, c: (NC - 1 - c, p))
    st_spec = pl.BlockSpec((G, T, HEAD, 128), lambda p, c: (p, NC - 1 - c, 0, 0))
    prev_spec = pl.BlockSpec((G, 1, HEAD, 128), lambda p, c: (p, jnp.maximum((NC - 1 - c) * T - 1, 0), 0, 0))
    sel_spec = pl.BlockSpec((SEL_ROWS, 8 * 128), lambda p, c: (0, 0))
    return pl.pallas_call(
        body, name=name, grid=(NP // G, NC),
        in_specs=[tspec] * 7 + [st_spec, prev_spec, sel_spec],
        out_specs=tuple([tspec] * 6),
        out_shape=tuple([jax.ShapeDtypeStruct((S, R), F32)] * 6),
        scratch_shapes=[pltpu.VMEM((G, HEAD, 128), F32), pltpu.VMEM((G, 128, 8 * 128), F32),
                        pltpu.VMEM((G, 128, 8 * 128), F32), pltpu.VMEM((G, 128, 128), F32),
                        pltpu.VMEM((G, HEAD, 128), F32)],
        compiler_params=_cp(("parallel", "arbitrary"), 48),
    )(r, w, k, v, kap, b, dy, states, states, _column_selector())


def _sum_parts(parts, *, name):
    P, rows, W = parts.shape
    tr = rows
    for cand in (1024, 512, 256, 128, 64, 32, 16, 8):
        if rows % cand == 0:
            tr = cand
            break

    def body(p_ref, o_ref):
        acc = p_ref[0]
        for s in range(1, P):
            acc = acc + p_ref[s]
        o_ref[...] = acc

    return pl.pallas_call(
        body, name=name, grid=(rows // tr,),
        in_specs=[pl.BlockSpec((P, tr, W), lambda i: (0, i, 0))],
        out_specs=pl.BlockSpec((tr, W), lambda i: (i, 0)), out_shape=jax.ShapeDtypeStruct((rows, W), F32),
        compiler_params=_cp(("parallel",), 32),
    )(parts)


def _adamw(w, m, v, parts, *, name):
    R, C = w.shape
    P = parts.shape[0]
    tr = R
    for cand in (1024, 512, 256, 128, 64, 32, 16, 8):
        if R % cand == 0 and cand * C * 4 * (7 + P) <= 10 * 1024 * 1024:
            tr = cand
            break
    bc1 = 1.0 - ADAM_B1 ** ADAM_STEP
    bc2 = 1.0 - ADAM_B2 ** ADAM_STEP

    def body(w_ref, m_ref, v_ref, p_ref, g_ref, d_ref, nm_ref, nv_ref):
        g = p_ref[0].astype(F32)
        for s in range(1, P):
            g = g + p_ref[s].astype(F32)
        m1 = ADAM_B1 * m_ref[...] + (1.0 - ADAM_B1) * g
        v1 = ADAM_B2 * v_ref[...] + (1.0 - ADAM_B2) * (g * g)
        m_hat = m1 / bc1
        v_hat = v1 / bc2
        g_ref[...] = g
        d_ref[...] = -ADAM_LR * (m_hat / (jnp.sqrt(v_hat) + ADAM_EPS) + ADAM_WD * w_ref[...])
        nm_ref[...] = m1
        nv_ref[...] = v1

    spec = pl.BlockSpec((tr, C), lambda i: (i, 0))
    return pl.pallas_call(
        body, name=name, grid=(R // tr,),
        in_specs=[spec, spec, spec, pl.BlockSpec((P, tr, C), lambda i: (0, i, 0))],
        out_specs=(spec, spec, spec, spec), out_shape=tuple([jax.ShapeDtypeStruct((R, C), F32)] * 4),
        compiler_params=_cp(("parallel",), 40),
    )(w, m, v, parts)


def _cols_full(g8):
    n, rows, c = g8.shape
    return jnp.transpose(g8, (1, 0, 2)).reshape(rows, n * c)


def _cols_split(full):
    rows, cols = full.shape
    return jnp.transpose(full.reshape(rows, N_DEV, cols // N_DEV), (1, 0, 2))


def _interleave(gate, up):
    D, F = gate.shape
    nj = F // FF_TILE
    return jnp.stack([gate.reshape(D, nj, FF_TILE), up.reshape(D, nj, FF_TILE)], axis=2).reshape(D, 2 * F)


def _deinterleave(gu):
    D, F2 = gu.shape
    nj = F2 // (2 * FF_TILE)
    t = gu.reshape(D, nj, 2, FF_TILE)
    return t[:, :, 0, :].reshape(D, F2 // 2), t[:, :, 1, :].reshape(D, F2 // 2)


def _pack(vals, rows_multiple=512):
    flat = jnp.concatenate([v.reshape(-1).astype(F32) for v in vals])
    n = flat.shape[0]
    unit = 128 * rows_multiple
    padded = ((n + unit - 1) // unit) * unit
    return jnp.pad(flat, (0, padded - n)).reshape(padded // 128, 128)


def _unpack(packed, shapes):
    flat = packed.reshape(-1)
    out, off = [], 0
    for shp in shapes:
        size = 1
        for d in shp:
            size *= d
        out.append(flat[off:off + size].reshape(shp))
        off += size
    return out


def _ffn_forward(x, wgu, wd, gpre, gpost, shift, scale1p, gw, tag):
    h = _pre_norm_mod(x, gpre, shift, scale1p, name=f"{tag}_pre")
    au, s = _ffn_up(h, wgu, tm=1024, tk=512, name=f"{tag}_up")
    f = _mm(s, wd, tm=1024, tn=1024, tk=512, name=f"{tag}_down")
    xo = _post_norm_res(x, f, gpost, gw, name=f"{tag}_post")
    return xo, (h, au, s, f)


def _ffn_backward(dxo, x, saved, wgu, wd, gpre, gpost, scale1p, gw, tag):
    h, au, s, f = saved
    df, post_sums = _post_norm_res_bwd(dxo, f, gpost, gw, MACARON, name=f"{tag}_post_bwd")
    dwd = _mm(s, df, ta=True, tm=512, tn=2048, tk=512, name=f"{tag}_dwd")
    dau = _ffn_down_bwd(df, wd, au, tm=1024, tk=512, name=f"{tag}_down_bwd")
    dwgu = _mm(h, dau, ta=True, tm=1024, tn=1024, tk=512, name=f"{tag}_dwgu")
    dh = _mm(dau, wgu, tb=True, tm=1024, tn=1024, tk=512, name=f"{tag}_dh")
    dx, pre_sums = _pre_norm_mod_bwd(dh, x, dxo, gpre, scale1p, name=f"{tag}_pre_bwd")
    return dx, dwgu, dwd, pre_sums, post_sums


def kernel(x, c, w_ada, b_ada, norm_pre, norm_post, ffn1_w_gate, ffn1_w_up, ffn1_w_down, w_in, mu_shift, pool_w, pool_scale, w0, w2, a0, a2, g2, k_k, k_a, r_k, lnx_w, lnx_b, w_out, ffn2_w_gate, ffn2_w_up, ffn2_w_down, loss_target, m_w_ada, m_b_ada, m_norm_pre, m_norm_post, m_ffn1_w_gate, m_ffn1_w_up, m_ffn1_w_down, m_w_in, m_mu_shift, m_pool_w, m_pool_scale, m_w0, m_w2, m_a0, m_a2, m_g2, m_k_k, m_k_a, m_r_k, m_lnx_w, m_lnx_b, m_w_out, m_ffn2_w_gate, m_ffn2_w_up, m_ffn2_w_down, v_w_ada, v_b_ada, v_norm_pre, v_norm_post, v_ffn1_w_gate, v_ffn1_w_up, v_ffn1_w_down, v_w_in, v_mu_shift, v_pool_w, v_pool_scale, v_w0, v_w2, v_a0, v_a2, v_g2, v_k_k, v_k_a, v_r_k, v_lnx_w, v_lnx_b, v_w_out, v_ffn2_w_gate, v_ffn2_w_up, v_ffn2_w_down):
    names = ["w_ada", "b_ada", "norm_pre", "norm_post", "ffn1_w_gate", "ffn1_w_up", "ffn1_w_down", "w_in", "mu_shift",
             "pool_w", "pool_scale", "w0", "w2", "a0", "a2", "g2", "k_k", "k_a", "r_k", "lnx_w", "lnx_b", "w_out",
             "ffn2_w_gate", "ffn2_w_up", "ffn2_w_down"]
    env = dict(locals())
    W = {n: env[n][0] for n in names}
    M1 = {n: env["m_" + n][0] for n in names}
    V1 = {n: env["v_" + n][0] for n in names}

    me = _my_index()
    xs = x[0]
    tgt = loss_target[0]
    S, D = xs.shape
    F = W["ffn1_w_gate"].shape[1] * N_DEV
    R = W["w0"].shape[0]
    PW = D - R
    IN_W = W["w_in"].shape[1] * N_DEV
    P_W = F
    QW = P_W - PW
    NMOD = 9 * D
    ada_c = W["w_ada"].shape[1]

    c_all, npre8, npost8, w2_8, a2_8, g2_8 = _exchange(
        [c, W["norm_pre"], W["norm_post"], W["w2"].astype(BF16), W["a2"].astype(BF16), W["g2"].astype(BF16)],
        scatter=False, name="gather_small")
    c_all = c_all.reshape(N_DEV, D)
    gpre = _cols_full(npre8)
    gpost = _cols_full(npost8)
    wl = jnp.zeros((LORA_PAD, 3 * R), BF16)
    wl = wl.at[0:LORA_W, 0:R].set(_cols_full(w2_8))
    wl = wl.at[LORA_W:LORA_W + LORA_A, R:2 * R].set(_cols_full(a2_8))
    wl = wl.at[LORA_W + LORA_A:LORA_W + LORA_A + LORA_G, 2 * R:3 * R].set(_cols_full(g2_8))

    sc_all = jax.nn.silu(c_all)
    sc_pad = jnp.concatenate([sc_all, jnp.zeros((8, D), F32)], axis=0).astype(BF16)
    modcols = _mm(sc_pad, W["w_ada"], tm=16, tn=ada_c, tk=256, name="ada_fwd")[0:N_DEV]
    modcols = modcols + lax.dynamic_slice(W["b_ada"], (me * ada_c,), (ada_c,))[None, :]
    (mod8,) = _exchange([modcols], scatter=False, name="gather_mod")
    mod = lax.dynamic_index_in_dim(mod8, me, axis=1, keepdims=False).reshape(9, D)

    def mod_row(i):
        return mod[i:i + 1, :]

    def gather_ffn(tag, name):
        g8, u8, d8 = _gather_two_level([W[f"{tag}_w_gate"].astype(BF16), W[f"{tag}_w_up"].astype(BF16),
                                        W[f"{tag}_w_down"].astype(BF16)], name=name)
        return _interleave(_cols_full(g8), _cols_full(u8)), d8.reshape(F, D)

    wgu1, wd1 = gather_ffn("ffn1", "gather_ffn")
    win8, wout8 = _gather_two_level([W["w_in"].astype(BF16), W["w_out"].astype(BF16)], name="gather_mixer")
    w_in_p = jnp.pad(_cols_full(win8), ((0, 0), (0, P_W - IN_W)))
    w_out_f = wout8.reshape(D, D)
    wgu2, wd2 = gather_ffn("ffn2", "gather_ffn")

    mu_p = jnp.pad(W["mu_shift"], (0, QW - W["mu_shift"].shape[0]))[None, :]
    vec = lambda a: a.reshape(1, -1)
    w0r, a0r, kkr, kar = vec(W["w0"]), vec(W["a0"]), vec(W["k_k"]), vec(W["k_a"])
    lnw, lnb, rkr = vec(W["lnx_w"]), vec(W["lnx_b"]), vec(W["r_k"])
    pscale = vec(W["pool_scale"])

    sc1p = [1.0 + mod_row(3 * s + 1) for s in range(3)]
    shifts = [mod_row(3 * s) for s in range(3)]
    wgts = [MACARON, 1.0, MACARON]
    gws = [wgts[s] * (1.0 + mod_row(3 * s + 2)) for s in range(3)]
    gp = [gpre[s:s + 1] for s in range(3)]
    gq = [gpost[s:s + 1] for s in range(3)]

    x1, sv1 = _ffn_forward(xs, wgu1, wd1, gp[0], gq[0], shifts[0], sc1p[0], gws[0], "ffn")

    h2 = _pre_norm_mod(x1, gp[1], shifts[1], sc1p[1], name="mix_pre")
    p = _mm(h2, w_in_p, tm=1024, tn=512, tk=512, name="mix_in")
    q = p[:, PW:]
    o_pool, y_pool = _pool_fwd(p, W["pool_w"], pscale, name="pool_fwd")
    r_s, w_s, k_s, v_s, kap_s, b_s, g_s = _rwkv_prep(q, mu_p, wl, w0r, a0r, kkr, kar, name="rwkv_prep")
    y_scan, states = _scan_fwd(r_s, w_s, k_s, v_s, kap_s, b_s, name="scan_fwd")
    cat = _rwkv_post(y_scan, r_s, k_s, v_s, g_s, y_pool, lnw, lnb, rkr, name="rwkv_post")
    f2 = _mm(cat, w_out_f, tm=1024, tn=1024, tk=512, name="mix_out")
    x2 = _post_norm_res(x1, f2, gq[1], gws[1], name="mix_post")

    x3, sv3 = _ffn_forward(x2, wgu2, wd2, gp[2], gq[2], shifts[2], sc1p[2], gws[2], "ffn")

    loss_part, dx3 = _loss_head(x3, tgt, name="loss_head")
    loss = lax.psum(loss_part[0, 0], MESH_AXES)

    dx2, dwgu2, dwd2, pre3, post3 = _ffn_backward(dx3, x2, sv3, wgu2, wd2, gp[2], gq[2], sc1p[2], gws[2], "ffn")

    df2, post2 = _post_norm_res_bwd(dx2, f2, gq[1], gws[1], 1.0, name="mix_post_bwd")
    dw_out = _mm(cat, df2, ta=True, tm=1024, tn=1024, tk=512, name="mix_dwout")
    dcat = _mm(df2, w_out_f, tb=True, tm=1024, tn=1024, tk=512, name="mix_dcat")
    dyr = dcat[:, PW:]
    dysc, dg, dr_b, dk2_b, dv_b, post_sums = _rwkv_post_bwd(dyr, y_scan, r_s, k_s, v_s, g_s, lnw, lnb, rkr,
                                                             name="rwkv_post_bwd")
    dr, dw, dk2, dv, dkap, db = _scan_bwd(r_s, w_s, k_s, v_s, kap_s, b_s, dysc, states, name="scan_bwd")
    dps, dwl, prep_sums = _rwkv_prep_bwd(q, mu_p, wl, w0r, a0r, kkr, kar,
                                         (dr, dw, dk2, dv, dkap, db, dg, dr_b, dk2_b, dv_b), name="rwkv_prep_bwd")
    dq, dmu = _tshift_bwd(dps, q, mu_p, name="tshift_bwd")
    du_pool, dpool_w, dpool_scale = _pool_bwd(dcat, o_pool, W["pool_w"], pscale, name="pool_bwd")
    dp = jnp.concatenate([du_pool, dq], axis=1)
    dw_in = _mm(h2, dp, ta=True, tm=1024, tn=512, tk=512, name="mix_dwin")
    dh2 = _mm(dp, w_in_p, tb=True, tm=1024, tn=1024, tk=512, name="mix_dh")
    dx1, pre2 = _pre_norm_mod_bwd(dh2, x1, dx2, gp[1], sc1p[1], name="mix_pre_bwd")

    dx0, dwgu1, dwd1, pre1, post1 = _ffn_backward(dx1, xs, sv1, wgu1, wd1, gp[0], gq[0], sc1p[0], gws[0], "ffn")

    pres, posts = [pre1, pre2, pre3], [post1, post2, post3]
    dmod = jnp.stack([jnp.stack([pres[s][0], pres[s][1], posts[s][0]]) for s in range(3)]).reshape(NMOD // 128, 128)
    dnorm_pre = jnp.stack([pres[s][2] for s in range(3)])
    dnorm_post = jnp.stack([posts[s][1] for s in range(3)])

    small = [dmu[0, :W["mu_shift"].shape[0]], dpool_w, dpool_scale, prep_sums[0], prep_sums[1], prep_sums[2],
             prep_sums[3], post_sums[2], post_sums[0], post_sums[1], dnorm_pre, dnorm_post,
             dwl[0:LORA_W, 0:R], dwl[LORA_W:LORA_W + LORA_A, R:2 * R],
             dwl[LORA_W + LORA_A:LORA_W + LORA_A + LORA_G, 2 * R:3 * R]]
    small_shapes = [a.shape for a in small]
    dmod8, small8 = _gather_two_level([dmod, _pack(small)], name="gather_grads")
    g_b_ada = _sum_parts(dmod8, name="sum_dmod").reshape(NMOD)
    red = _unpack(_sum_parts(small8, name="sum_small"), small_shapes)
    (g_mu, g_pool_w, g_pool_scale, g_w0, g_a0, g_kk, g_ka, g_rk, g_lnw, g_lnb, g_npre, g_npost, g_w2, g_a2,
     g_g2) = red

    dmod_all = dmod8.reshape(N_DEV, NMOD)
    dmod_cols = lax.dynamic_slice(dmod_all, (0, me * ada_c), (N_DEV, ada_c))
    dmod_cols = jnp.concatenate([dmod_cols, jnp.zeros_like(dmod_cols)], axis=0)
    g_w_ada = _mm(sc_pad, dmod_cols, ta=True, tm=D, tn=ada_c // 9, tk=16, name="ada_bwd")

    def by_core_chip(blocks):
        shp = blocks.shape
        t = blocks.astype(BF16).reshape((N_DEV // 2, 2) + shp[1:])
        return jnp.swapaxes(t, 0, 1)

    def scatter(blocks, tag):
        mine = [by_core_chip(b) for b in blocks]
        got = _sibling_swap(mine, name=f"{tag}_swap")
        sums = [_pair_add(m, g, name=f"{tag}_add{i}") for i, (m, g) in enumerate(zip(mine, got))]
        return _chips_all_to_all(sums, name=f"{tag}_chips")

    def scatter_ffn(dwgu, dwd, tag):
        dgate, dup = _deinterleave(dwgu)
        return scatter([_cols_split(dgate), _cols_split(dup), dwd.reshape(N_DEV, F // N_DEV, D)], tag)

    pg2, pu2, pd2 = scatter_ffn(dwgu2, dwd2, "scatter_ffn")
    pin, pout = scatter([_cols_split(dw_in[:, :IN_W]), dw_out.reshape(N_DEV, D // N_DEV, D)], "scatter_mixer")
    pg1, pu1, pd1 = scatter_ffn(dwgu1, dwd1, "scatter_ffn")

    res = {}

    def big(nm, parts, tag):
        res[nm] = _adamw(W[nm], M1[nm], V1[nm], parts, name=tag)

    big("ffn1_w_gate", pg1, "adamw_cols")
    big("ffn1_w_up", pu1, "adamw_cols")
    big("ffn1_w_down", pd1, "adamw_rows")
    big("ffn2_w_gate", pg2, "adamw_cols")
    big("ffn2_w_up", pu2, "adamw_cols")
    big("ffn2_w_down", pd2, "adamw_rows")
    big("w_in", pin, "adamw_w_in")
    big("w_out", pout, "adamw_w_out")
    big("w_ada", g_w_ada[None], "adamw_w_ada")

    def my_cols(full, width):
        return lax.dynamic_slice_in_dim(full, me * width, width, axis=full.ndim - 1)

    small_names = ["b_ada", "mu_shift", "pool_w", "pool_scale", "w0", "a0", "k_k", "k_a", "r_k", "lnx_w", "lnx_b",
                   "norm_pre", "norm_post", "w2", "a2", "g2"]
    small_grads = [g_b_ada, g_mu, g_pool_w, g_pool_scale, g_w0, g_a0, g_kk, g_ka, g_rk.reshape(W["r_k"].shape), g_lnw,
                   g_lnb, my_cols(g_npre, D // N_DEV), my_cols(g_npost, D // N_DEV), my_cols(g_w2, R // N_DEV),
                   my_cols(g_a2, R // N_DEV), my_cols(g_g2, R // N_DEV)]
    shapes = [W[n].shape for n in small_names]
    packed = _adamw(_pack([W[n] for n in small_names]), _pack([M1[n] for n in small_names]),
                    _pack([V1[n] for n in small_names]), _pack(small_grads)[None], name="adamw_small")
    unpacked = [_unpack(t, shapes) for t in packed]
    for i, nm in enumerate(small_names):
        res[nm] = tuple(unpacked[k][i] for k in range(4))

    outs = [loss, dx0[None]]
    for k in range(4):
        outs.extend(res[nm][k][None] for nm in names)
    return tuple(outs)
```

```python
import functools

import jax
import jax.numpy as jnp
from jax import lax
from jax.experimental import pallas as pl
from jax.experimental.pallas import tpu as pltpu

F32 = jnp.float32
BF16 = jnp.bfloat16
N_DEV = 8
MESH_AXES = ("x", "y", "c")

NORM_EPS = 1e-6
HEAD = 64
LN_X_EPS = 1e-5 * HEAD
POOL_GROUPS = 4
POOL_GROUP = 128
MACARON = 0.5
LORA_W, LORA_A, LORA_G = 64, 64, 224
LORA_PAD = 384
ADAM_LR, ADAM_B1, ADAM_B2, ADAM_EPS, ADAM_WD, ADAM_STEP = 0.001, 0.9, 0.999, 1e-08, 0.01, 10

FF_TILE = 768
ROW_TILE = 256
SCAN_T = 64
SCAN_G = 6
VMEM_CAP = 56 * 1024 * 1024


def _cp(sem, vmem_mb):
    return pltpu.CompilerParams(dimension_semantics=sem, vmem_limit_bytes=min(vmem_mb * 1024 * 1024, VMEM_CAP))


def _my_index():
    return 4 * lax.axis_index("x") + 2 * lax.axis_index("y") + lax.axis_index("c")


def _exchange(arrays, *, scatter, name):
    n = len(arrays)
    out_shapes = []
    for a in arrays:
        shp = a.shape if scatter else (N_DEV,) + a.shape
        out_shapes.append(jax.ShapeDtypeStruct(shp, a.dtype))

    def body(*refs):
        ins, outs = refs[:n], refs[n:2 * n]
        send_sems, recv_sems, local_sems = refs[2 * n:]
        me = _my_index()

        def dev(p):
            return (p // 4, (p // 2) % 2, p % 2)

        def copy(i, d):
            peer = (me + d) % N_DEV
            src = ins[i].at[peer] if scatter else ins[i]
            return pltpu.make_async_remote_copy(
                src_ref=src, dst_ref=outs[i].at[me], send_sem=send_sems.at[i, d - 1],
                recv_sem=recv_sems.at[i, d - 1], device_id=dev(peer), device_id_type=pl.DeviceIdType.MESH)

        def arrival(i, d):
            frm = (me + N_DEV - d) % N_DEV
            src = ins[i].at[frm] if scatter else ins[i]
            return pltpu.make_async_remote_copy(
                src_ref=src, dst_ref=outs[i].at[frm], send_sem=send_sems.at[i, d - 1],
                recv_sem=recv_sems.at[i, d - 1], device_id=dev(frm), device_id_type=pl.DeviceIdType.MESH)

        locals_ = []
        for i in range(n):
            src = ins[i].at[me] if scatter else ins[i]
            lc = pltpu.make_async_copy(src, outs[i].at[me], local_sems.at[i])
            lc.start()
            locals_.append(lc)
        sends = [copy(i, d) for d in range(1, N_DEV) for i in range(n)]
        for cp in sends:
            cp.start()
        for d in range(1, N_DEV):
            for i in range(n):
                arrival(i, d).wait_recv()
        for cp in sends:
            cp.wait_send()
        for lc in locals_:
            lc.wait()

    hbm = pl.BlockSpec(memory_space=pltpu.HBM)
    return pl.pallas_call(
        body, name=name, out_shape=tuple(out_shapes), in_specs=[hbm] * n, out_specs=tuple([hbm] * n),
        scratch_shapes=[pltpu.SemaphoreType.DMA((n, N_DEV - 1)), pltpu.SemaphoreType.DMA((n, N_DEV - 1)),
                        pltpu.SemaphoreType.DMA((n,))],
    )(*arrays)


def _remote(src, dst, send_sem, recv_sem, to):
    return pltpu.make_async_remote_copy(src_ref=src, dst_ref=dst, send_sem=send_sem, recv_sem=recv_sem,
                                        device_id=to, device_id_type=pl.DeviceIdType.MESH)


def _gather_two_level(arrays, *, name):
    n = len(arrays)
    out_shapes = [jax.ShapeDtypeStruct((N_DEV,) + a.shape, a.dtype) for a in arrays]

    def body(*refs):
        ins, outs = refs[:n], refs[n:2 * n]
        send_sems, recv_sems, local_sems = refs[2 * n:]
        x, y, c = lax.axis_index("x"), lax.axis_index("y"), lax.axis_index("c")
        sibling = (x, y, 1 - c)
        chips = [(1 - x, y), (x, 1 - y), (1 - x, 1 - y)]

        def slot(i, px, py, pc):
            return outs[i].at[4 * px + 2 * py + pc]

        def copy(i, k, block, to, src=None):
            dst = slot(i, *block)
            return _remote(dst if src is None else src, dst, send_sems.at[i, k], recv_sems.at[i, k], to)

        locals_ = []
        for i in range(n):
            lc = pltpu.make_async_copy(ins[i], slot(i, x, y, c), local_sems.at[i])
            lc.start()
            locals_.append(lc)
        sends = []
        for j, chip in enumerate(chips):
            for i in range(n):
                sends.append(copy(i, 1 + j, (x, y, c), (*chip, c), src=ins[i]))
        for i in range(n):
            sends.append(copy(i, 0, (x, y, c), sibling, src=ins[i]))
        for cp in sends:
            cp.start()
        for j, chip in enumerate(chips):
            for i in range(n):
                copy(i, 1 + j, (*chip, c), (x, y, c)).wait_recv()
                fwd = copy(i, 4 + j, (*chip, c), sibling)
                fwd.start()
                sends.append(fwd)
        for i in range(n):
            copy(i, 0, (x, y, 1 - c), (x, y, c)).wait_recv()
        for j, chip in enumerate(chips):
            for i in range(n):
                copy(i, 4 + j, (*chip, 1 - c), (x, y, c)).wait_recv()
        for cp in sends:
            cp.wait_send()
        for lc in locals_:
            lc.wait()

    hbm = pl.BlockSpec(memory_space=pltpu.HBM)
    return pl.pallas_call(
        body, name=name, out_shape=tuple(out_shapes), in_specs=[hbm] * n, out_specs=tuple([hbm] * n),
        scratch_shapes=[pltpu.SemaphoreType.DMA((n, 7)), pltpu.SemaphoreType.DMA((n, 7)),
                        pltpu.SemaphoreType.DMA((n,))],
    )(*arrays)


def _sibling_swap(arrays, *, name):
    n = len(arrays)
    out_shapes = [jax.ShapeDtypeStruct(a.shape[1:], a.dtype) for a in arrays]

    def body(*refs):
        ins, outs = refs[:n], refs[n:2 * n]
        send_sems, recv_sems = refs[2 * n:]
        x, y, c = lax.axis_index("x"), lax.axis_index("y"), lax.axis_index("c")
        copies = [_remote(ins[i].at[1 - c], outs[i], send_sems.at[i], recv_sems.at[i], (x, y, 1 - c))
                  for i in range(n)]
        for cp in copies:
            cp.start()
        for cp in copies:
            cp.wait_recv()
        for cp in copies:
            cp.wait_send()

    hbm = pl.BlockSpec(memory_space=pltpu.HBM)
    return pl.pallas_call(
        body, name=name, out_shape=tuple(out_shapes), in_specs=[hbm] * n, out_specs=tuple([hbm] * n),
        scratch_shapes=[pltpu.SemaphoreType.DMA((n,)), pltpu.SemaphoreType.DMA((n,))],
    )(*arrays)


def _chips_all_to_all(arrays, *, name):
    n = len(arrays)
    out_shapes = [jax.ShapeDtypeStruct(a.shape, a.dtype) for a in arrays]

    def body(*refs):
        ins, outs = refs[:n], refs[n:2 * n]
        send_sems, recv_sems, local_sems = refs[2 * n:]
        x, y, c = lax.axis_index("x"), lax.axis_index("y"), lax.axis_index("c")
        mine = 2 * x + y
        chips = [(1 - x, y), (x, 1 - y), (1 - x, 1 - y)]
        locals_ = []
        for i in range(n):
            lc = pltpu.make_async_copy(ins[i].at[mine], outs[i].at[mine], local_sems.at[i])
            lc.start()
            locals_.append(lc)
        sends = [_remote(ins[i].at[2 * chip[0] + chip[1]], outs[i].at[mine], send_sems.at[i, j], recv_sems.at[i, j],
                         (*chip, c)) for j, chip in enumerate(chips) for i in range(n)]
        for cp in sends:
            cp.start()
        for j, chip in enumerate(chips):
            for i in range(n):
                q = 2 * chip[0] + chip[1]
                _remote(ins[i].at[q], outs[i].at[q], send_sems.at[i, j], recv_sems.at[i, j], (*chip, c)).wait_recv()
        for cp in sends:
            cp.wait_send()
        for lc in locals_:
            lc.wait()

    hbm = pl.BlockSpec(memory_space=pltpu.HBM)
    return pl.pallas_call(
        body, name=name, out_shape=tuple(out_shapes), in_specs=[hbm] * n, out_specs=tuple([hbm] * n),
        scratch_shapes=[pltpu.SemaphoreType.DMA((n, 3)), pltpu.SemaphoreType.DMA((n, 3)),
                        pltpu.SemaphoreType.DMA((n,))],
    )(*arrays)


def _pair_add(mine, got, *, name):
    _, nq, R, C = mine.shape
    tr = R
    for cand in (512, 256, 128, 64, 32, 16):
        if R % cand == 0 and cand * C * 2 * 3 * 2 <= 12 * 1024 * 1024:
            tr = cand
            break

    def body(core_ref, m_ref, g_ref, o_ref):
        o_ref[0] = (m_ref[0, 0].astype(F32) + g_ref[0].astype(F32)).astype(BF16)

    core = lax.axis_index("c").astype(jnp.int32).reshape(1)
    return pl.pallas_call(
        body, name=name,
        grid_spec=pltpu.PrefetchScalarGridSpec(
            num_scalar_prefetch=1, grid=(nq, R // tr),
            in_specs=[pl.BlockSpec((1, 1, tr, C), lambda q, i, core_ref: (core_ref[0], q, i, 0)),
                      pl.BlockSpec((1, tr, C), lambda q, i, core_ref: (q, i, 0))],
            out_specs=pl.BlockSpec((1, tr, C), lambda q, i, core_ref: (q, i, 0))),
        out_shape=jax.ShapeDtypeStruct((nq, R, C), BF16),
        compiler_params=_cp(("parallel", "parallel"), 40),
    )(core, mine, got)


def _mm(a, b, *, ta=False, tb=False, tm, tn, tk, out_dtype=F32, name):
    M = a.shape[1] if ta else a.shape[0]
    K = a.shape[0] if ta else a.shape[1]
    N = b.shape[0] if tb else b.shape[1]
    tm, tn, tk = min(tm, M), min(tn, N), min(tk, K)
    assert M % tm == 0 and N % tn == 0 and K % tk == 0, (name, M, N, K, tm, tn, tk)
    nk = K // tk
    dims = (((0 if ta else 1,), (1 if tb else 0,)), ((), ()))

    def body(a_ref, b_ref, o_ref, acc_ref):
        k = pl.program_id(2)

        @pl.when(k == 0)
        def _():
            acc_ref[...] = jnp.zeros_like(acc_ref)

        acc_ref[...] += lax.dot_general(a_ref[...].astype(BF16), b_ref[...].astype(BF16), dims,
                                        preferred_element_type=F32)

        @pl.when(k == nk - 1)
        def _():
            o_ref[...] = acc_ref[...].astype(out_dtype)

    a_spec = pl.BlockSpec((tk, tm), lambda i, j, k: (k, i)) if ta else pl.BlockSpec((tm, tk), lambda i, j, k: (i, k))
    b_spec = pl.BlockSpec((tn, tk), lambda i, j, k: (j, k)) if tb else pl.BlockSpec((tk, tn), lambda i, j, k: (k, j))
    blk = 2 * (tm * tk * a.dtype.itemsize + tk * tn * b.dtype.itemsize + tm * tn * jnp.dtype(out_dtype).itemsize)
    return pl.pallas_call(
        body, name=name, grid=(M // tm, N // tn, nk), in_specs=[a_spec, b_spec],
        out_specs=pl.BlockSpec((tm, tn), lambda i, j, k: (i, j)),
        out_shape=jax.ShapeDtypeStruct((M, N), out_dtype),
        scratch_shapes=[pltpu.VMEM((tm, tn), F32)],
        compiler_params=_cp(("parallel", "parallel", "arbitrary"), (blk + tm * tn * 4) // (1024 * 1024) + 12),
    )(a, b)


def _ffn_up(h, wgu, *, tm, tk, name):
    S, D = h.shape
    F2 = wgu.shape[1]
    tm = min(tm, S)
    tn = FF_TILE
    nk = D // tk

    def body(h_ref, w_ref, au_ref, s_ref, acc_ref):
        k = pl.program_id(2)

        @pl.when(k == 0)
        def _():
            acc_ref[...] = jnp.zeros_like(acc_ref)

        acc_ref[...] += jnp.dot(h_ref[...], w_ref[...], preferred_element_type=F32)

        @pl.when(k == nk - 1)
        def _():
            acc = acc_ref[...]
            a = acc[:, :tn]
            u = acc[:, tn:]
            au_ref[...] = acc.astype(BF16)
            s_ref[...] = (a * jax.nn.sigmoid(a) * u).astype(BF16)

    return pl.pallas_call(
        body, name=name, grid=(S // tm, F2 // (2 * tn), nk),
        in_specs=[pl.BlockSpec((tm, tk), lambda i, j, k: (i, k)), pl.BlockSpec((tk, 2 * tn), lambda i, j, k: (k, j))],
        out_specs=(pl.BlockSpec((tm, 2 * tn), lambda i, j, k: (i, j)), pl.BlockSpec((tm, tn), lambda i, j, k: (i, j))),
        out_shape=(jax.ShapeDtypeStruct((S, F2), BF16), jax.ShapeDtypeStruct((S, F2 // 2), BF16)),
        scratch_shapes=[pltpu.VMEM((tm, 2 * tn), F32)],
        compiler_params=_cp(("parallel", "parallel", "arbitrary"), 52),
    )(h, wgu)


def _ffn_down_bwd(df, wd, au, *, tm, tk, name):
    S, D = df.shape
    F = wd.shape[0]
    tm = min(tm, S)
    tn = FF_TILE
    nk = D // tk

    def body(df_ref, w_ref, au_ref, dau_ref, acc_ref):
        k = pl.program_id(2)

        @pl.when(k == 0)
        def _():
            acc_ref[...] = jnp.zeros_like(acc_ref)

        acc_ref[...] += lax.dot_general(df_ref[...], w_ref[...], (((1,), (1,)), ((), ())), preferred_element_type=F32)

        @pl.when(k == nk - 1)
        def _():
            ds = acc_ref[...]
            au_v = au_ref[...].astype(F32)
            a = au_v[:, :tn]
            u = au_v[:, tn:]
            sg = jax.nn.sigmoid(a)
            da = ds * u * (sg * (1.0 + a * (1.0 - sg)))
            du = ds * (a * sg)
            dau_ref[:, :tn] = da.astype(BF16)
            dau_ref[:, tn:] = du.astype(BF16)

    return pl.pallas_call(
        body, name=name, grid=(S // tm, F // tn, nk),
        in_specs=[pl.BlockSpec((tm, tk), lambda i, j, k: (i, k)), pl.BlockSpec((tn, tk), lambda i, j, k: (j, k)),
                  pl.BlockSpec((tm, 2 * tn), lambda i, j, k: (i, j))],
        out_specs=pl.BlockSpec((tm, 2 * tn), lambda i, j, k: (i, j)),
        out_shape=jax.ShapeDtypeStruct((S, 2 * F), BF16),
        scratch_shapes=[pltpu.VMEM((tm, tn), F32)],
        compiler_params=_cp(("parallel", "parallel", "arbitrary"), 52),
    )(df, wd, au)


def _fold8(x):
    tm, w = x.shape
    return jnp.sum(x.reshape(tm // 8, 8, w), axis=0)


def _row_spec(tm, w):
    return pl.BlockSpec((tm, w), lambda i: (i, 0))


def _vec_spec(rows, w):
    return pl.BlockSpec((rows, w), lambda i: (0, 0))


def _pre_norm_mod(x, gain, shift, scale1p, *, name):
    S, D = x.shape
    tm = ROW_TILE

    def body(x_ref, g_ref, sh_ref, sc_ref, h_ref):
        xv = x_ref[...]
        rinv = lax.rsqrt(jnp.mean(xv * xv, axis=-1, keepdims=True) + NORM_EPS)
        h_ref[...] = ((xv * rinv) * g_ref[...] * sc_ref[...] + sh_ref[...]).astype(BF16)

    return pl.pallas_call(
        body, name=name, grid=(S // tm,),
        in_specs=[_row_spec(tm, D), _vec_spec(1, D), _vec_spec(1, D), _vec_spec(1, D)],
        out_specs=_row_spec(tm, D), out_shape=jax.ShapeDtypeStruct((S, D), BF16),
        compiler_params=_cp(("parallel",), 32),
    )(x, gain, shift, scale1p)


def _pre_norm_mod_bwd(dh, x, dres, gain, scale1p, *, name):
    S, D = x.shape
    tm = ROW_TILE
    n = S // tm

    def body(dh_ref, x_ref, dr_ref, g_ref, sc_ref, dx_ref, sums_ref, acc_ref):
        i = pl.program_id(0)

        @pl.when(i == 0)
        def _():
            acc_ref[...] = jnp.zeros_like(acc_ref)

        xv = x_ref[...]
        dhv = dh_ref[...]
        g = g_ref[...]
        rinv = lax.rsqrt(jnp.mean(xv * xv, axis=-1, keepdims=True) + NORM_EPS)
        xn = xv * rinv
        dn = dhv * sc_ref[...]
        dxn = dn * g
        dx_ref[...] = dr_ref[...] + rinv * (dxn - xn * jnp.mean(dxn * xn, axis=-1, keepdims=True))
        acc_ref[0] += _fold8(dhv)
        acc_ref[1] += _fold8(dhv * (xn * g))
        acc_ref[2] += _fold8(dn * xn)

        @pl.when(i == n - 1)
        def _():
            for q in range(3):
                sums_ref[q:q + 1, :] = jnp.sum(acc_ref[q], axis=0, keepdims=True)

    return pl.pallas_call(
        body, name=name, grid=(n,),
        in_specs=[_row_spec(tm, D), _row_spec(tm, D), _row_spec(tm, D), _vec_spec(1, D), _vec_spec(1, D)],
        out_specs=(_row_spec(tm, D), _vec_spec(3, D)),
        out_shape=(jax.ShapeDtypeStruct((S, D), F32), jax.ShapeDtypeStruct((3, D), F32)),
        scratch_shapes=[pltpu.VMEM((3, 8, D), F32)],
        compiler_params=_cp(("arbitrary",), 40),
    )(dh, x, dres, gain, scale1p)


def _post_norm_res(x, f, gain, gw, *, name):
    S, D = x.shape
    tm = ROW_TILE

    def body(x_ref, f_ref, g_ref, gw_ref, o_ref):
        fv = f_ref[...]
        rinv = lax.rsqrt(jnp.mean(fv * fv, axis=-1, keepdims=True) + NORM_EPS)
        o_ref[...] = x_ref[...] + gw_ref[...] * ((fv * rinv) * g_ref[...])

    return pl.pallas_call(
        body, name=name, grid=(S // tm,),
        in_specs=[_row_spec(tm, D), _row_spec(tm, D), _vec_spec(1, D), _vec_spec(1, D)],
        out_specs=_row_spec(tm, D), out_shape=jax.ShapeDtypeStruct((S, D), F32),
        compiler_params=_cp(("parallel",), 32),
    )(x, f, gain, gw)


def _post_norm_res_bwd(dxo, f, gain, gw, weight, *, name):
    S, D = f.shape
    tm = ROW_TILE
    n = S // tm

    def body(d_ref, f_ref, g_ref, gw_ref, df_ref, sums_ref, acc_ref):
        i = pl.program_id(0)

        @pl.when(i == 0)
        def _():
            acc_ref[...] = jnp.zeros_like(acc_ref)

        fv = f_ref[...]
        dv = d_ref[...]
        g = g_ref[...]
        rinv = lax.rsqrt(jnp.mean(fv * fv, axis=-1, keepdims=True) + NORM_EPS)
        fh = fv * rinv
        dy = dv * gw_ref[...]
        dfh = dy * g
        df_ref[...] = (rinv * (dfh - fh * jnp.mean(dfh * fh, axis=-1, keepdims=True))).astype(BF16)
        acc_ref[0] += _fold8(weight * dv * (fh * g))
        acc_ref[1] += _fold8(dy * fh)

        @pl.when(i == n - 1)
        def _():
            for q in range(2):
                sums_ref[q:q + 1, :] = jnp.sum(acc_ref[q], axis=0, keepdims=True)

    return pl.pallas_call(
        body, name=name, grid=(n,),
        in_specs=[_row_spec(tm, D), _row_spec(tm, D), _vec_spec(1, D), _vec_spec(1, D)],
        out_specs=(_row_spec(tm, D), _vec_spec(2, D)),
        out_shape=(jax.ShapeDtypeStruct((S, D), BF16), jax.ShapeDtypeStruct((2, D), F32)),
        scratch_shapes=[pltpu.VMEM((2, 8, D), F32)],
        compiler_params=_cp(("arbitrary",), 40),
    )(dxo, f, gain, gw)


def _loss_head(y, target, *, name):
    S, D = y.shape
    tm = ROW_TILE

    def body(y_ref, t_ref, l_ref, dy_ref):
        i = pl.program_id(0)

        @pl.when(i == 0)
        def _():
            l_ref[...] = jnp.zeros_like(l_ref)

        err = y_ref[...] - t_ref[...]
        dy_ref[...] = err * (1.0 / D)
        row = jnp.sum(err * err, axis=-1, keepdims=True) * (0.5 / D)
        l_ref[...] += jnp.sum(row, axis=0, keepdims=True)

    return pl.pallas_call(
        body, name=name, grid=(S // tm,),
        in_specs=[_row_spec(tm, D), _row_spec(tm, D)],
        out_specs=(_vec_spec(1, 1), _row_spec(tm, D)),
        out_shape=(jax.ShapeDtypeStruct((1, 1), F32), jax.ShapeDtypeStruct((S, D), F32)),
        compiler_params=_cp(("arbitrary",), 32),
    )(y, target)


def _shift_down(z, j, row):
    return jnp.where(row >= j, pltpu.roll(z, j, 0), 0.0)


def _shift_up(z, j, row, n):
    return jnp.where(row < n - j, pltpu.roll(z, n - j, 0), 0.0)


def _pool_fwd(p, pool_w, pool_scale, *, name):
    S = p.shape[0]
    C = POOL_GROUP

    def body(u_ref, w_ref, sc_ref, o_ref, y_ref):
        g = pl.program_id(0)
        u = u_ref[...]
        row = lax.broadcasted_iota(jnp.int32, (S, C), 0)
        s1 = u + _shift_down(u, 1, row)
        s2 = s1 + _shift_down(s1, 2, row)
        s3 = s2 + _shift_down(s2, 4, row)
        s4 = s3 + _shift_down(s3, 8, row)
        gi = jnp.zeros((S, C), jnp.int32) + g
        win = jnp.where(gi == 0, s1, jnp.where(gi == 1, s2, jnp.where(gi == 2, s3, s4)))
        width = jnp.where(gi == 0, 2, jnp.where(gi == 1, 4, jnp.where(gi == 2, 8, 16)))
        count = jnp.minimum(row + 1, width).astype(F32)
        o = win / count - u
        o_ref[...] = o
        y_ref[...] = jnp.dot(o.astype(BF16), w_ref[0].astype(BF16), preferred_element_type=F32) * sc_ref[...]

    col = pl.BlockSpec((S, C), lambda g: (0, g))
    return pl.pallas_call(
        body, name=name, grid=(POOL_GROUPS,),
        in_specs=[col, pl.BlockSpec((1, C, C), lambda g: (g, 0, 0)), pl.BlockSpec((1, C), lambda g: (0, g))],
        out_specs=(col, col),
        out_shape=(jax.ShapeDtypeStruct((S, POOL_GROUPS * C), F32), jax.ShapeDtypeStruct((S, POOL_GROUPS * C), F32)),
        compiler_params=_cp(("parallel",), 48),
    )(p, pool_w, pool_scale)


def _pool_bwd(dcat, o, pool_w, pool_scale, *, name):
    S = o.shape[0]
    C = POOL_GROUP

    def body(dy_ref, o_ref, w_ref, sc_ref, du_ref, dw_ref, dsc_ref):
        g = pl.program_id(0)
        dy = dy_ref[...]
        ob = o_ref[...].astype(BF16)
        wb = w_ref[0].astype(BF16)
        mixed = jnp.dot(ob, wb, preferred_element_type=F32)
        dsc_ref[...] = jnp.sum(_fold8(dy * mixed), axis=0, keepdims=True)
        dmix = (dy * sc_ref[...]).astype(BF16)
        dw_ref[0] = lax.dot_general(ob, dmix, (((0,), (0,)), ((), ())), preferred_element_type=F32)
        do = lax.dot_general(dmix, wb, (((1,), (1,)), ((), ())), preferred_element_type=F32)
        row = lax.broadcasted_iota(jnp.int32, (S, C), 0)
        gi = jnp.zeros((S, C), jnp.int32) + g
        width = jnp.where(gi == 0, 2, jnp.where(gi == 1, 4, jnp.where(gi == 2, 8, 16)))
        z = do / jnp.minimum(row + 1, width).astype(F32)
        s1 = z + _shift_up(z, 1, row, S)
        s2 = s1 + _shift_up(s1, 2, row, S)
        s3 = s2 + _shift_up(s2, 4, row, S)
        s4 = s3 + _shift_up(s3, 8, row, S)
        win = jnp.where(gi == 0, s1, jnp.where(gi == 1, s2, jnp.where(gi == 2, s3, s4)))
        du_ref[...] = (win - do).astype(BF16)

    col = pl.BlockSpec((S, C), lambda g: (0, g))
    return pl.pallas_call(
        body, name=name, grid=(POOL_GROUPS,),
        in_specs=[col, col, pl.BlockSpec((1, C, C), lambda g: (g, 0, 0)), pl.BlockSpec((1, C), lambda g: (0, g))],
        out_specs=(col, pl.BlockSpec((1, C, C), lambda g: (g, 0, 0)), pl.BlockSpec((1, C), lambda g: (0, g))),
        out_shape=(jax.ShapeDtypeStruct((S, POOL_GROUPS * C), BF16), jax.ShapeDtypeStruct((POOL_GROUPS, C, C), F32),
                   jax.ShapeDtypeStruct((1, POOL_GROUPS * C), F32)),
        compiler_params=_cp(("parallel",), 48),
    )(dcat, o, pool_w, pool_scale)


def _block_ones():
    r = lax.broadcasted_iota(jnp.int32, (128, 128), 0) // HEAD
    c = lax.broadcasted_iota(jnp.int32, (128, 128), 1) // HEAD
    return jnp.where(r == c, 1.0, 0.0).astype(BF16)


def _segsum(x, bd):
    outs = []
    for j in range(x.shape[1] // 128):
        xs = x[:, j * 128:(j + 1) * 128]
        hi = xs.astype(BF16)
        lo = (xs - hi.astype(F32)).astype(BF16)
        outs.append(jnp.dot(hi, bd, preferred_element_type=F32) + jnp.dot(lo, bd, preferred_element_type=F32))
    return jnp.concatenate(outs, axis=1)


def _prep_common(q, qprev, first, mu, wl, w0, a0, kkw, kaw, R):
    tm = q.shape[0]
    row = lax.broadcasted_iota(jnp.int32, q.shape, 0)
    last = qprev[7:8, :] * first
    prev = jnp.where(row == 0, last, pltpu.roll(q, 1, 0))
    ps = q + mu * (prev - q)
    r = ps[:, 0:R]
    k = ps[:, R:2 * R]
    v = ps[:, 2 * R:3 * R]
    lo_in = ps[:, 3 * R:3 * R + LORA_PAD]
    lane = lax.broadcasted_iota(jnp.int32, (tm, LORA_PAD), 1)
    m_w = lane < LORA_W
    m_a = lane < LORA_W + LORA_A
    m_g = lane < LORA_W + LORA_A + LORA_G
    act = jnp.where(m_w, jnp.tanh(lo_in), jnp.where(m_a, lo_in, jnp.where(m_g, jax.nn.sigmoid(lo_in), 0.0)))
    lo = jnp.dot(act.astype(BF16), wl, preferred_element_type=F32)
    wpre = w0 + lo[:, 0:R]
    apre = a0 + lo[:, R:2 * R]
    g = lo[:, 2 * R:3 * R]
    neg = -wpre
    softplus = jnp.maximum(neg, 0.0) + jnp.log(1.0 + jnp.exp(-jnp.abs(neg)))
    wlog = -softplus - 0.5
    ew = jnp.exp(wlog)
    decay = jnp.exp(-ew)
    a = jax.nn.sigmoid(apre)
    kk = k * kkw
    bd = _block_ones()
    n2 = _segsum(kk * kk, bd)
    nrm = jnp.maximum(jnp.sqrt(n2), 1e-12)
    kap = kk / nrm
    kmul = 1.0 + (a - 1.0) * kaw
    k2 = k * kmul
    return dict(prev=prev, r=r, k=k, v=v, act=act, m_w=m_w, m_a=m_a, m_g=m_g, wpre=wpre, g=g, ew=ew, decay=decay,
                a=a, n2=n2, nrm=nrm, kap=kap, kmul=kmul, k2=k2, bd=bd)


def _prev_rows_spec(tm, w):
    return pl.BlockSpec((8, w), lambda i: (jnp.maximum(i * (tm // 8) - 1, 0), 0))


def _rwkv_prep(q, mu, wl, w0, a0, kkw, kaw, *, name):
    S, QW = q.shape
    R = w0.shape[1]
    tm = ROW_TILE // 2

    def body(q_ref, qp_ref, mu_ref, wl_ref, w0_ref, a0_ref, kk_ref, ka_ref, r_ref, w_ref, k_ref, v_ref, kap_ref,
             b_ref, g_ref):
        first = jnp.where(pl.program_id(0) > 0, 1.0, 0.0)
        t = _prep_common(q_ref[...], qp_ref[...], first, mu_ref[...], wl_ref[...], w0_ref[...], a0_ref[...],
                         kk_ref[...], ka_ref[...], R)
        r_ref[...] = t["r"]
        w_ref[...] = t["decay"]
        k_ref[...] = t["k2"]
        v_ref[...] = t["v"]
        kap_ref[...] = t["kap"]
        b_ref[...] = t["kap"] * t["a"]
        g_ref[...] = t["g"]

    vec = _vec_spec(1, R)
    return pl.pallas_call(
        body, name=name, grid=(S // tm,),
        in_specs=[_row_spec(tm, QW), _prev_rows_spec(tm, QW), _vec_spec(1, QW), _vec_spec(LORA_PAD, 3 * R), vec, vec,
                  vec, vec],
        out_specs=tuple([_row_spec(tm, R)] * 7),
        out_shape=tuple([jax.ShapeDtypeStruct((S, R), F32)] * 7),
        compiler_params=_cp(("parallel",), 48),
    )(q, q, mu, wl, w0, a0, kkw, kaw)


def _rwkv_prep_bwd(q, mu, wl, w0, a0, kkw, kaw, grads, *, name):
    S, QW = q.shape
    R = w0.shape[1]
    tm = ROW_TILE // 2
    n = S // tm

    def body(q_ref, qp_ref, mu_ref, wl_ref, w0_ref, a0_ref, kk_ref, ka_ref, dr_ref, dw_ref, dk2_ref, dv_ref, dkap_ref,
             db_ref, dg_ref, drb_ref, dk2b_ref, dvb_ref, dps_ref, dwl_ref, sums_ref, acc_ref):
        i = pl.program_id(0)

        @pl.when(i == 0)
        def _():
            acc_ref[...] = jnp.zeros_like(acc_ref)
            dwl_ref[...] = jnp.zeros_like(dwl_ref)

        first = jnp.where(i > 0, 1.0, 0.0)
        wl = wl_ref[...]
        kkw = kk_ref[...]
        kaw = ka_ref[...]
        t = _prep_common(q_ref[...], qp_ref[...], first, mu_ref[...], wl, w0_ref[...], a0_ref[...], kkw, kaw, R)
        a, kap, k, act = t["a"], t["kap"], t["k"], t["act"]
        db = db_ref[...]
        dk2 = dk2_ref[...] + dk2b_ref[...]
        dkap = dkap_ref[...] + db * a
        da = db * kap + dk2 * k * kaw
        dk = dk2 * t["kmul"]
        proj = jnp.where(jnp.sqrt(t["n2"]) > 1e-12, _segsum(kap * dkap, t["bd"]), 0.0)
        dkk = (dkap - kap * proj) / t["nrm"]
        dk = dk + dkk * kkw
        dapre = da * a * (1.0 - a)
        dwlog = dw_ref[...] * t["decay"] * (-t["ew"])
        dwpre = dwlog * jax.nn.sigmoid(-t["wpre"])
        acc_ref[0] += _fold8(dwpre)
        acc_ref[1] += _fold8(dapre)
        acc_ref[2] += _fold8(dkk * k)
        acc_ref[3] += _fold8(dk2 * k * (a - 1.0))
        dlo = jnp.concatenate([dwpre, dapre, dg_ref[...]], axis=1).astype(BF16)
        dwl_ref[...] += lax.dot_general(act.astype(BF16), dlo, (((0,), (0,)), ((), ())), preferred_element_type=F32)
        dact = lax.dot_general(dlo, wl, (((1,), (1,)), ((), ())), preferred_element_type=F32)
        dlin = jnp.where(t["m_w"], dact * (1.0 - act * act),
                         jnp.where(t["m_a"], dact, jnp.where(t["m_g"], dact * act * (1.0 - act), 0.0)))
        dps_ref[:, 0:R] = dr_ref[...] + drb_ref[...]
        dps_ref[:, R:2 * R] = dk
        dps_ref[:, 2 * R:3 * R] = dv_ref[...] + dvb_ref[...]
        dps_ref[:, 3 * R:3 * R + LORA_PAD] = dlin
        dps_ref[:, 3 * R + LORA_PAD:] = jnp.zeros((tm, QW - 3 * R - LORA_PAD), F32)

        @pl.when(i == n - 1)
        def _():
            for j in range(4):
                sums_ref[j:j + 1, :] = jnp.sum(acc_ref[j], axis=0, keepdims=True)

    vec = _vec_spec(1, R)
    return pl.pallas_call(
        body, name=name, grid=(n,),
        in_specs=[_row_spec(tm, QW), _prev_rows_spec(tm, QW), _vec_spec(1, QW), _vec_spec(LORA_PAD, 3 * R), vec, vec,
                  vec, vec] + [_row_spec(tm, R)] * 10,
        out_specs=(_row_spec(tm, QW), _vec_spec(LORA_PAD, 3 * R), _vec_spec(4, R)),
        out_shape=(jax.ShapeDtypeStruct((S, QW), F32), jax.ShapeDtypeStruct((LORA_PAD, 3 * R), F32),
                   jax.ShapeDtypeStruct((4, R), F32)),
        scratch_shapes=[pltpu.VMEM((4, 8, R), F32)],
        compiler_params=_cp(("arbitrary",), 56),
    )(q, q, mu, wl, w0, a0, kkw, kaw, *grads)


def _tshift_bwd(dps, q, mu, *, name):
    S, QW = q.shape
    tm = ROW_TILE // 2
    n = S // tm

    def body(d_ref, dn_ref, q_ref, qp_ref, mu_ref, dq_ref, dmu_ref, acc_ref):
        i = pl.program_id(0)

        @pl.when(i == 0)
        def _():
            acc_ref[...] = jnp.zeros_like(acc_ref)

        mu = mu_ref[...]
        d = d_ref[...]
        qv = q_ref[...]
        row = lax.broadcasted_iota(jnp.int32, d.shape, 0)
        first = jnp.where(i > 0, 1.0, 0.0)
        notlast = jnp.where(i < n - 1, 1.0, 0.0)
        prev = jnp.where(row == 0, qp_ref[7:8, :] * first, pltpu.roll(qv, 1, 0))
        z = d * mu
        nxt = jnp.where(row == tm - 1, dn_ref[0:1, :] * mu * notlast, pltpu.roll(z, tm - 1, 0))
        dq_ref[...] = (d * (1.0 - mu) + nxt).astype(BF16)
        acc_ref[...] += _fold8(d * (prev - qv))

        @pl.when(i == n - 1)
        def _():
            dmu_ref[...] = jnp.sum(acc_ref[...], axis=0, keepdims=True)

    nblk8 = S // 8
    next_spec = pl.BlockSpec((8, QW), lambda i: (jnp.minimum((i + 1) * (tm // 8), nblk8 - 1), 0))
    return pl.pallas_call(
        body, name=name, grid=(n,),
        in_specs=[_row_spec(tm, QW), next_spec, _row_spec(tm, QW), _prev_rows_spec(tm, QW), _vec_spec(1, QW)],
        out_specs=(_row_spec(tm, QW), _vec_spec(1, QW)),
        out_shape=(jax.ShapeDtypeStruct((S, QW), BF16), jax.ShapeDtypeStruct((1, QW), F32)),
        scratch_shapes=[pltpu.VMEM((8, QW), F32)],
        compiler_params=_cp(("arbitrary",), 48),
    )(dps, dps, q, q, mu)


def _post_common(ysc, r, k2, v, lnw, lnb, rk):
    bd = _block_ones()
    mean = _segsum(ysc, bd) * (1.0 / HEAD)
    d = ysc - mean
    var = _segsum(d * d, bd) * (1.0 / HEAD)
    rstd = lax.rsqrt(var + LN_X_EPS)
    yh = d * rstd
    rkk = _segsum(r * k2 * rk, bd)
    z = yh * lnw + lnb + rkk * v
    return bd, rstd, yh, rkk, z


def _rwkv_post(ysc, r, k2, v, g, ypool, lnw, lnb, rk, *, name):
    S, R = ysc.shape
    PW = ypool.shape[1]
    tm = ROW_TILE

    def body(y_ref, r_ref, k_ref, v_ref, g_ref, yp_ref, lw_ref, lb_ref, rk_ref, cat_ref):
        _, _, _, _, z = _post_common(y_ref[...], r_ref[...], k_ref[...], v_ref[...], lw_ref[...], lb_ref[...],
                                     rk_ref[...])
        cat_ref[:, 0:PW] = yp_ref[...].astype(BF16)
        cat_ref[:, PW:] = (z * g_ref[...]).astype(BF16)

    vec = _vec_spec(1, R)
    return pl.pallas_call(
        body, name=name, grid=(S // tm,),
        in_specs=[_row_spec(tm, R)] * 5 + [_row_spec(tm, PW), vec, vec, vec],
        out_specs=_row_spec(tm, PW + R), out_shape=jax.ShapeDtypeStruct((S, PW + R), BF16),
        compiler_params=_cp(("parallel",), 48),
    )(ysc, r, k2, v, g, ypool, lnw, lnb, rk)


def _rwkv_post_bwd(dcat, ysc, r, k2, v, g, lnw, lnb, rk, *, name):
    S, R = ysc.shape
    tm = ROW_TILE
    n = S // tm

    def body(d_ref, y_ref, r_ref, k_ref, v_ref, g_ref, lw_ref, lb_ref, rk_ref, dy_ref, dg_ref, drb_ref, dkb_ref,
             dvb_ref, sums_ref, acc_ref):
        i = pl.program_id(0)

        @pl.when(i == 0)
        def _():
            acc_ref[...] = jnp.zeros_like(acc_ref)

        rv, kv, vv, lw, rkw = r_ref[...], k_ref[...], v_ref[...], lw_ref[...], rk_ref[...]
        bd, rstd, yh, rkk, z = _post_common(y_ref[...], rv, kv, vv, lw, lb_ref[...], rkw)
        dyr = d_ref[...]
        dg_ref[...] = dyr * z
        dz = dyr * g_ref[...]
        dyh = dz * lw
        dy_ref[...] = rstd * (dyh - _segsum(dyh, bd) * (1.0 / HEAD) - yh * (_segsum(dyh * yh, bd) * (1.0 / HEAD)))
        dvb_ref[...] = dz * rkk
        drkk = _segsum(dz * vv, bd)
        drb_ref[...] = drkk * kv * rkw
        dkb_ref[...] = drkk * rv * rkw
        acc_ref[0] += _fold8(dz * yh)
        acc_ref[1] += _fold8(dz)
        acc_ref[2] += _fold8(drkk * rv * kv)

        @pl.when(i == n - 1)
        def _():
            for j in range(3):
                sums_ref[j:j + 1, :] = jnp.sum(acc_ref[j], axis=0, keepdims=True)

    vec = _vec_spec(1, R)
    dspec = _row_spec(tm, R)
    return pl.pallas_call(
        body, name=name, grid=(n,),
        in_specs=[dspec] + [_row_spec(tm, R)] * 5 + [vec, vec, vec],
        out_specs=tuple([_row_spec(tm, R)] * 5) + (_vec_spec(3, R),),
        out_shape=tuple([jax.ShapeDtypeStruct((S, R), F32)] * 5) + (jax.ShapeDtypeStruct((3, R), F32),),
        scratch_shapes=[pltpu.VMEM((3, 8, R), F32)],
        compiler_params=_cp(("arbitrary",), 56),
    )(dcat, ysc, r, k2, v, g, lnw, lnb, rk)


def _half_sums(x, m_a):
    s_a = jnp.sum(jnp.where(m_a, x, 0.0), axis=1, keepdims=True)
    s_b = jnp.sum(jnp.where(m_a, 0.0, x), axis=1, keepdims=True)
    return s_a, s_b


SEL_ROWS = 32


def _column_selector():
    row = lax.broadcasted_iota(jnp.int32, (SEL_ROWS, 8 * 128), 0)
    col = lax.broadcasted_iota(jnp.int32, (SEL_ROWS, 8 * 128), 1)
    return jnp.where((row < 24) & (row % 8 == col // 128), 1.0, 0.0).astype(BF16)


def _expand_columns(x, sel):
    hi = x.astype(BF16).astype(F32)
    r1 = x - hi
    mid = r1.astype(BF16).astype(F32)
    lo = (r1 - mid).astype(BF16).astype(F32)
    terms = jnp.concatenate([hi, mid, lo, jnp.zeros_like(x)], axis=0).astype(BF16)
    return lax.dot_general(terms, sel, (((0,), (0,)), ((), ())), preferred_element_type=F32)


def _scan_fwd(r, w, k, v, kap, b, *, name):
    S, R = r.shape
    G, T = SCAN_G, SCAN_T
    NP = R // 128
    assert NP % G == 0 and S % T == 0
    GW = 128 * G

    def body(r_ref, w_ref, k_ref, v_ref, kap_ref, b_ref, sel_ref, y_ref, st_ref, s_scr, vc_scr, yt_scr):
        c = pl.program_id(1)

        @pl.when(c == 0)
        def _():
            s_scr[...] = jnp.zeros_like(s_scr)

        yt_scr[...] = jnp.zeros_like(yt_scr)
        lane = lax.broadcasted_iota(jnp.int32, (HEAD, 128), 1)
        m_a = lane < HEAD

        def block(tb, carry):
            t0 = pl.multiple_of(tb * 8, 8)
            rb, wb, kb = r_ref[pl.ds(t0, 8), :], w_ref[pl.ds(t0, 8), :], k_ref[pl.ds(t0, 8), :]
            pb, bb, vb = kap_ref[pl.ds(t0, 8), :], b_ref[pl.ds(t0, 8), :], v_ref[pl.ds(t0, 8), :]
            for g in range(G):
                vc_scr[g] = _expand_columns(vb[:, g * 128:(g + 1) * 128], sel_ref[...])

            def put_y(g, parts, hot_y):
                yt_scr[g, 0:HEAD, :] = jnp.where(hot_y, parts[0], yt_scr[g, 0:HEAD, :])
                yt_scr[g, HEAD:, :] = jnp.where(hot_y, parts[1], yt_scr[g, HEAD:, :])

            for j in range(8):
                t = t0 + j
                cols = slice(j * 128, (j + 1) * 128)
                sa_parts, y_parts = [], []
                for g in range(G):
                    sl = slice(g * 128, (g + 1) * 128)
                    sa_parts.append(_half_sums(s_scr[g] * pb[j:j + 1, sl], m_a))
                if j > 0:
                    for g in range(G):
                        sl = slice(g * 128, (g + 1) * 128)
                        y_parts.append(_half_sums(s_scr[g] * rb[j - 1:j, sl], m_a))
                for g in range(G):
                    sl = slice(g * 128, (g + 1) * 128)
                    sa = -jnp.where(m_a, sa_parts[g][0], sa_parts[g][1])
                    vcol = jnp.where(m_a, vc_scr[g, 0:HEAD, cols], vc_scr[g, HEAD:, cols])
                    st = s_scr[g] * wb[j:j + 1, sl] + sa * bb[j:j + 1, sl] + vcol * kb[j:j + 1, sl]
                    s_scr[g] = st
                    st_ref[g, t] = st
                if j > 0:
                    hot_y = lane == t - 1
                    for g in range(G):
                        put_y(g, y_parts[g], hot_y)
            hot_y = lane == t0 + 7
            for g in range(G):
                sl = slice(g * 128, (g + 1) * 128)
                put_y(g, _half_sums(s_scr[g] * rb[7:8, sl], m_a), hot_y)
            return carry

        lax.fori_loop(0, T // 8, block, 0)
        for g in range(G):
            y_ref[:, g * 128:(g + 1) * 128] = yt_scr[g].T[0:T, :]

    tspec = pl.BlockSpec((T, GW), lambda p, c: (c, p))
    sel_spec = pl.BlockSpec((SEL_ROWS, 8 * 128), lambda p, c: (0, 0))
    return pl.pallas_call(
        body, name=name, grid=(NP // G, S // T),
        in_specs=[tspec] * 6 + [sel_spec],
        out_specs=(tspec, pl.BlockSpec((G, T, HEAD, 128), lambda p, c: (p, c, 0, 0))),
        out_shape=(jax.ShapeDtypeStruct((S, R), F32), jax.ShapeDtypeStruct((NP, S, HEAD, 128), F32)),
        scratch_shapes=[pltpu.VMEM((G, HEAD, 128), F32), pltpu.VMEM((G, 128, 8 * 128), F32),
                        pltpu.VMEM((G, 128, 128), F32)],
        compiler_params=_cp(("parallel", "arbitrary"), 48),
    )(r, w, k, v, kap, b, _column_selector())


def _scan_bwd(r, w, k, v, kap, b, dy, states, *, name):
    S, R = r.shape
    G, T = SCAN_G, SCAN_T
    NP = R // 128
    NC = S // T
    GW = 128 * G

    def body(r_ref, w_ref, k_ref, v_ref, kap_ref, b_ref, dy_ref, st_ref, sp_ref, sel_ref, dr_ref, dw_ref, dk_ref,
             dv_ref, dkap_ref, db_ref, ds_scr, vc_scr, dyc_scr, dvt_scr, sa_scr):
        ci = pl.program_id(1)

        @pl.when(ci == 0)
        def _():
            ds_scr[...] = jnp.zeros_like(ds_scr)

        dvt_scr[...] = jnp.zeros_like(dvt_scr)
        lane = lax.broadcasted_iota(jnp.int32, (HEAD, 128), 1)
        m_a = lane < HEAD
        sub = lax.broadcasted_iota(jnp.int32, (8, 128), 0)
        zero_i = jnp.zeros((HEAD, 128), jnp.int32)
        has_prev = jnp.where(ci < NC - 1, 1.0, 0.0)

        def state_before(g, t):
            at_start = (zero_i + t) == 0
            return jnp.where(at_start, sp_ref[g, 0] * has_prev, st_ref[g, jnp.maximum(t - 1, 0)])

        def sa_sums(g, t, p_row):
            return _half_sums(state_before(g, t) * p_row, m_a)

        def put_sa(g, parts):
            sa_scr[g] = -jnp.where(m_a, parts[0], parts[1])

        pb_last = kap_ref[pl.ds(T - 8, 8), :]
        for g in range(G):
            put_sa(g, sa_sums(g, T - 1, pb_last[7:8, g * 128:(g + 1) * 128]))

        def block(it, carry):
            tb = T // 8 - 1 - it
            t0 = pl.multiple_of(tb * 8, 8)
            rb, wb, kb = r_ref[pl.ds(t0, 8), :], w_ref[pl.ds(t0, 8), :], k_ref[pl.ds(t0, 8), :]
            pb, bb = kap_ref[pl.ds(t0, 8), :], b_ref[pl.ds(t0, 8), :]
            vb, dyb = v_ref[pl.ds(t0, 8), :], dy_ref[pl.ds(t0, 8), :]
            pb_prev = kap_ref[pl.ds(pl.multiple_of(jnp.maximum(t0 - 8, 0), 8), 8), :]
            for g in range(G):
                vc_scr[g] = _expand_columns(vb[:, g * 128:(g + 1) * 128], sel_ref[...])
                dyc_scr[g] = _expand_columns(dyb[:, g * 128:(g + 1) * 128], sel_ref[...])
            outs = [[jnp.zeros((8, 128), F32) for _ in range(5)] for _ in range(G)]
            for j in range(7, -1, -1):
                t = t0 + j
                hot = lane == t
                cols = slice(j * 128, (j + 1) * 128)
                ds_parts, next_sa = [], []
                for g in range(G):
                    sl = slice(g * 128, (g + 1) * 128)
                    dycol = jnp.where(m_a, dyc_scr[g, 0:HEAD, cols], dyc_scr[g, HEAD:, cols])
                    ds = ds_scr[g] + dycol * rb[j:j + 1, sl]
                    ds_scr[g] = ds
                    ds_parts.append((_half_sums(ds * kb[j:j + 1, sl], m_a), _half_sums(ds * bb[j:j + 1, sl], m_a)))
                for g in range(G):
                    sl = slice(g * 128, (g + 1) * 128)
                    p_row = pb[j - 1:j, sl] if j > 0 else pb_prev[7:8, sl]
                    next_sa.append(sa_sums(g, jnp.maximum(t - 1, 0), p_row))
                for g in range(G):
                    sl = slice(g * 128, (g + 1) * 128)
                    w_r, p_r = wb[j:j + 1, sl], pb[j:j + 1, sl]
                    ds = ds_scr[g]
                    s_p = state_before(g, t)
                    dycol = jnp.where(m_a, dyc_scr[g, 0:HEAD, cols], dyc_scr[g, HEAD:, cols])
                    vcol = jnp.where(m_a, vc_scr[g, 0:HEAD, cols], vc_scr[g, HEAD:, cols])
                    sa = sa_scr[g]
                    dr_row = jnp.sum(st_ref[g, t] * dycol, axis=0, keepdims=True)
                    dk_row = jnp.sum(ds * vcol, axis=0, keepdims=True)
                    db_row = jnp.sum(ds * sa, axis=0, keepdims=True)
                    dw_row = jnp.sum(ds * s_p, axis=0, keepdims=True)
                    (dv_a, dv_b), (dsa_a, dsa_b) = ds_parts[g]
                    dvt_scr[g, 0:HEAD, :] = jnp.where(hot, dv_a, dvt_scr[g, 0:HEAD, :])
                    dvt_scr[g, HEAD:, :] = jnp.where(hot, dv_b, dvt_scr[g, HEAD:, :])
                    dsa = jnp.where(m_a, dsa_a, dsa_b)
                    dkap_row = -jnp.sum(s_p * dsa, axis=0, keepdims=True)
                    ds_scr[g] = ds * w_r - dsa * p_r
                    pick = sub == j
                    for q, row in enumerate((dr_row, dw_row, dk_row, dkap_row, db_row)):
                        outs[g][q] = jnp.where(pick, row, outs[g][q])
                for g in range(G):
                    put_sa(g, next_sa[g])
            for g in range(G):
                sl = slice(g * 128, (g + 1) * 128)
                for q, ref in enumerate((dr_ref, dw_ref, dk_ref, dkap_ref, db_ref)):
                    ref[pl.ds(t0, 8), sl] = outs[g][q]
            return carry

        lax.fori_loop(0, T // 8, block, 0)
        for g in range(G):
            dv_ref[:, g * 128:(g + 1) * 128] = dvt_scr[g].T[0:T, :]

    tspec = pl.BlockSpec((T, GW), lambda p, c: (NC - 1 - c, p))
    st_spec = pl.BlockSpec((G, T, HEAD, 128), lambda p, c: (p, NC - 1 - c, 0, 0))
    prev_spec = pl.BlockSpec((G, 1, HEAD, 128), lambda p, c: (p, jnp.maximum((NC - 1 - c) * T - 1, 0), 0, 0))
    sel_spec = pl.BlockSpec((SEL_ROWS, 8 * 128), lambda p, c: (0, 0))
    return pl.pallas_call(
        body, name=name, grid=(NP // G, NC),
        in_specs=[tspec] * 7 + [st_spec, prev_spec, sel_spec],
        out_specs=tuple([tspec] * 6),
        out_shape=tuple([jax.ShapeDtypeStruct((S, R), F32)] * 6),
        scratch_shapes=[pltpu.VMEM((G, HEAD, 128), F32), pltpu.VMEM((G, 128, 8 * 128), F32),
                        pltpu.VMEM((G, 128, 8 * 128), F32), pltpu.VMEM((G, 128, 128), F32),
                        pltpu.VMEM((G, HEAD, 128), F32)],
        compiler_params=_cp(("parallel", "arbitrary"), 48),
    )(r, w, k, v, kap, b, dy, states, states, _column_selector())


def _sum_parts(parts, *, name):
    P, rows, W = parts.shape
    tr = rows
    for cand in (1024, 512, 256, 128, 64, 32, 16, 8):
        if rows % cand == 0:
            tr = cand
            break

    def body(p_ref, o_ref):
        acc = p_ref[0]
        for s in range(1, P):
            acc = acc + p_ref[s]
        o_ref[...] = acc

    return pl.pallas_call(
        body, name=name, grid=(rows // tr,),
        in_specs=[pl.BlockSpec((P, tr, W), lambda i: (0, i, 0))],
        out_specs=pl.BlockSpec((tr, W), lambda i: (i, 0)), out_shape=jax.ShapeDtypeStruct((rows, W), F32),
        compiler_params=_cp(("parallel",), 32),
    )(parts)


def _adamw(w, m, v, parts, *, name):
    R, C = w.shape
    P = parts.shape[0]
    tr = R
    for cand in (1024, 512, 256, 128, 64, 32, 16, 8):
        if R % cand == 0 and cand * C * 4 * (7 + P) <= 10 * 1024 * 1024:
            tr = cand
            break
    bc1 = 1.0 - ADAM_B1 ** ADAM_STEP
    bc2 = 1.0 - ADAM_B2 ** ADAM_STEP

    def body(w_ref, m_ref, v_ref, p_ref, g_ref, d_ref, nm_ref, nv_ref):
        g = p_ref[0].astype(F32)
        for s in range(1, P):
            g = g + p_ref[s].astype(F32)
        m1 = ADAM_B1 * m_ref[...] + (1.0 - ADAM_B1) * g
        v1 = ADAM_B2 * v_ref[...] + (1.0 - ADAM_B2) * (g * g)
        m_hat = m1 / bc1
        v_hat = v1 / bc2
        g_ref[...] = g
        d_ref[...] = -ADAM_LR * (m_hat / (jnp.sqrt(v_hat) + ADAM_EPS) + ADAM_WD * w_ref[...])
        nm_ref[...] = m1
        nv_ref[...] = v1

    spec = pl.BlockSpec((tr, C), lambda i: (i, 0))
    return pl.pallas_call(
        body, name=name, grid=(R // tr,),
        in_specs=[spec, spec, spec, pl.BlockSpec((P, tr, C), lambda i: (0, i, 0))],
        out_specs=(spec, spec, spec, spec), out_shape=tuple([jax.ShapeDtypeStruct((R, C), F32)] * 4),
        compiler_params=_cp(("parallel",), 40),
    )(w, m, v, parts)


def _cols_full(g8):
    n, rows, c = g8.shape
    return jnp.transpose(g8, (1, 0, 2)).reshape(rows, n * c)


def _cols_split(full):
    rows, cols = full.shape
    return jnp.transpose(full.reshape(rows, N_DEV, cols // N_DEV), (1, 0, 2))


def _interleave(gate, up):
    D, F = gate.shape
    nj = F // FF_TILE
    return jnp.stack([gate.reshape(D, nj, FF_TILE), up.reshape(D, nj, FF_TILE)], axis=2).reshape(D, 2 * F)


def _deinterleave(gu):
    D, F2 = gu.shape
    nj = F2 // (2 * FF_TILE)
    t = gu.reshape(D, nj, 2, FF_TILE)
    return t[:, :, 0, :].reshape(D, F2 // 2), t[:, :, 1, :].reshape(D, F2 // 2)


def _pack(vals, rows_multiple=512):
    flat = jnp.concatenate([v.reshape(-1).astype(F32) for v in vals])
    n = flat.shape[0]
    unit = 128 * rows_multiple
    padded = ((n + unit - 1) // unit) * unit
    return jnp.pad(flat, (0, padded - n)).reshape(padded // 128, 128)


def _unpack(packed, shapes):
    flat = packed.reshape(-1)
    out, off = [], 0
    for shp in shapes:
        size = 1
        for d in shp:
            size *= d
        out.append(flat[off:off + size].reshape(shp))
        off += size
    return out


def _ffn_forward(x, wgu, wd, gpre, gpost, shift, scale1p, gw, tag):
    h = _pre_norm_mod(x, gpre, shift, scale1p, name=f"{tag}_pre")
    au, s = _ffn_up(h, wgu, tm=1024, tk=2048, name=f"{tag}_up")
    f = _mm(s, wd, tm=1024, tn=1024, tk=2048, name=f"{tag}_down")
    xo = _post_norm_res(x, f, gpost, gw, name=f"{tag}_post")
    return xo, (h, au, s, f)


def _ffn_backward(dxo, x, saved, wgu, wd, gpre, gpost, scale1p, gw, tag):
    h, au, s, f = saved
    df, post_sums = _post_norm_res_bwd(dxo, f, gpost, gw, MACARON, name=f"{tag}_post_bwd")
    dwd = _mm(s, df, ta=True, tm=1024, tn=1024, tk=1024, name=f"{tag}_dwd")
    dau = _ffn_down_bwd(df, wd, au, tm=1024, tk=2048, name=f"{tag}_down_bwd")
    dwgu = _mm(h, dau, ta=True, tm=1024, tn=1024, tk=1024, name=f"{tag}_dwgu")
    dh = _mm(dau, wgu, tb=True, tm=1024, tn=1024, tk=2048, name=f"{tag}_dh")
    dx, pre_sums = _pre_norm_mod_bwd(dh, x, dxo, gpre, scale1p, name=f"{tag}_pre_bwd")
    return dx, dwgu, dwd, pre_sums, post_sums


def kernel(x, c, w_ada, b_ada, norm_pre, norm_post, ffn1_w_gate, ffn1_w_up, ffn1_w_down, w_in, mu_shift, pool_w, pool_scale, w0, w2, a0, a2, g2, k_k, k_a, r_k, lnx_w, lnx_b, w_out, ffn2_w_gate, ffn2_w_up, ffn2_w_down, loss_target, m_w_ada, m_b_ada, m_norm_pre, m_norm_post, m_ffn1_w_gate, m_ffn1_w_up, m_ffn1_w_down, m_w_in, m_mu_shift, m_pool_w, m_pool_scale, m_w0, m_w2, m_a0, m_a2, m_g2, m_k_k, m_k_a, m_r_k, m_lnx_w, m_lnx_b, m_w_out, m_ffn2_w_gate, m_ffn2_w_up, m_ffn2_w_down, v_w_ada, v_b_ada, v_norm_pre, v_norm_post, v_ffn1_w_gate, v_ffn1_w_up, v_ffn1_w_down, v_w_in, v_mu_shift, v_pool_w, v_pool_scale, v_w0, v_w2, v_a0, v_a2, v_g2, v_k_k, v_k_a, v_r_k, v_lnx_w, v_lnx_b, v_w_out, v_ffn2_w_gate, v_ffn2_w_up, v_ffn2_w_down):
    names = ["w_ada", "b_ada", "norm_pre", "norm_post", "ffn1_w_gate", "ffn1_w_up", "ffn1_w_down", "w_in", "mu_shift",
             "pool_w", "pool_scale", "w0", "w2", "a0", "a2", "g2", "k_k", "k_a", "r_k", "lnx_w", "lnx_b", "w_out",
             "ffn2_w_gate", "ffn2_w_up", "ffn2_w_down"]
    env = dict(locals())
    W = {n: env[n][0] for n in names}
    M1 = {n: env["m_" + n][0] for n in names}
    V1 = {n: env["v_" + n][0] for n in names}

    me = _my_index()
    xs = x[0]
    tgt = loss_target[0]
    S, D = xs.shape
    F = W["ffn1_w_gate"].shape[1] * N_DEV
    R = W["w0"].shape[0]
    PW = D - R
    IN_W = W["w_in"].shape[1] * N_DEV
    P_W = F
    QW = P_W - PW
    NMOD = 9 * D
    ada_c = W["w_ada"].shape[1]

    c_all, npre8, npost8, w2_8, a2_8, g2_8 = _exchange(
        [c, W["norm_pre"], W["norm_post"], W["w2"].astype(BF16), W["a2"].astype(BF16), W["g2"].astype(BF16)],
        scatter=False, name="gather_small")
    c_all = c_all.reshape(N_DEV, D)
    gpre = _cols_full(npre8)
    gpost = _cols_full(npost8)
    wl = jnp.zeros((LORA_PAD, 3 * R), BF16)
    wl = wl.at[0:LORA_W, 0:R].set(_cols_full(w2_8))
    wl = wl.at[LORA_W:LORA_W + LORA_A, R:2 * R].set(_cols_full(a2_8))
    wl = wl.at[LORA_W + LORA_A:LORA_W + LORA_A + LORA_G, 2 * R:3 * R].set(_cols_full(g2_8))

    sc_all = jax.nn.silu(c_all)
    sc_pad = jnp.concatenate([sc_all, jnp.zeros((8, D), F32)], axis=0).astype(BF16)
    modcols = _mm(sc_pad, W["w_ada"], tm=16, tn=ada_c, tk=256, name="ada_fwd")[0:N_DEV]
    modcols = modcols + lax.dynamic_slice(W["b_ada"], (me * ada_c,), (ada_c,))[None, :]
    (mod8,) = _exchange([modcols], scatter=False, name="gather_mod")
    mod = lax.dynamic_index_in_dim(mod8, me, axis=1, keepdims=False).reshape(9, D)

    def mod_row(i):
        return mod[i:i + 1, :]

    f_pad = FF_TILE - F // N_DEV
    f_all = FF_TILE * N_DEV

    def gather_ffn(tag, name):
        g8, u8, d8 = _gather_two_level(
            [jnp.pad(W[f"{tag}_w_gate"].astype(BF16), ((0, 0), (0, f_pad))),
             jnp.pad(W[f"{tag}_w_up"].astype(BF16), ((0, 0), (0, f_pad))),
             jnp.pad(W[f"{tag}_w_down"].astype(BF16), ((0, f_pad), (0, 0)))], name=name)
        return _interleave(_cols_full(g8), _cols_full(u8)), d8.reshape(f_all, D)

    wgu1, wd1 = gather_ffn("ffn1", "gather_ffn")
    win8, wout8 = _gather_two_level([W["w_in"].astype(BF16), W["w_out"].astype(BF16)], name="gather_mixer")
    w_in_p = jnp.pad(_cols_full(win8), ((0, 0), (0, P_W - IN_W)))
    w_out_f = wout8.reshape(D, D)
    wgu2, wd2 = gather_ffn("ffn2", "gather_ffn")

    mu_p = jnp.pad(W["mu_shift"], (0, QW - W["mu_shift"].shape[0]))[None, :]
    vec = lambda a: a.reshape(1, -1)
    w0r, a0r, kkr, kar = vec(W["w0"]), vec(W["a0"]), vec(W["k_k"]), vec(W["k_a"])
    lnw, lnb, rkr = vec(W["lnx_w"]), vec(W["lnx_b"]), vec(W["r_k"])
    pscale = vec(W["pool_scale"])

    sc1p = [1.0 + mod_row(3 * s + 1) for s in range(3)]
    shifts = [mod_row(3 * s) for s in range(3)]
    wgts = [MACARON, 1.0, MACARON]
    gws = [wgts[s] * (1.0 + mod_row(3 * s + 2)) for s in range(3)]
    gp = [gpre[s:s + 1] for s in range(3)]
    gq = [gpost[s:s + 1] for s in range(3)]

    x1, sv1 = _ffn_forward(xs, wgu1, wd1, gp[0], gq[0], shifts[0], sc1p[0], gws[0], "ffn")

    h2 = _pre_norm_mod(x1, gp[1], shifts[1], sc1p[1], name="mix_pre")
    p = _mm(h2, w_in_p, tm=1024, tn=512, tk=2048, name="mix_in")
    q = p[:, PW:]
    o_pool, y_pool = _pool_fwd(p, W["pool_w"], pscale, name="pool_fwd")
    r_s, w_s, k_s, v_s, kap_s, b_s, g_s = _rwkv_prep(q, mu_p, wl, w0r, a0r, kkr, kar, name="rwkv_prep")
    y_scan, states = _scan_fwd(r_s, w_s, k_s, v_s, kap_s, b_s, name="scan_fwd")
    cat = _rwkv_post(y_scan, r_s, k_s, v_s, g_s, y_pool, lnw, lnb, rkr, name="rwkv_post")
    f2 = _mm(cat, w_out_f, tm=1024, tn=1024, tk=2048, name="mix_out")
    x2 = _post_norm_res(x1, f2, gq[1], gws[1], name="mix_post")

    x3, sv3 = _ffn_forward(x2, wgu2, wd2, gp[2], gq[2], shifts[2], sc1p[2], gws[2], "ffn")

    loss_part, dx3 = _loss_head(x3, tgt, name="loss_head")
    loss = lax.psum(loss_part[0, 0], MESH_AXES)

    dx2, dwgu2, dwd2, pre3, post3 = _ffn_backward(dx3, x2, sv3, wgu2, wd2, gp[2], gq[2], sc1p[2], gws[2], "ffn")

    df2, post2 = _post_norm_res_bwd(dx2, f2, gq[1], gws[1], 1.0, name="mix_post_bwd")
    dw_out = _mm(cat, df2, ta=True, tm=1024, tn=1024, tk=1024, name="mix_dwout")
    dcat = _mm(df2, w_out_f, tb=True, tm=1024, tn=1024, tk=2048, name="mix_dcat")
    dyr = dcat[:, PW:]
    dysc, dg, dr_b, dk2_b, dv_b, post_sums = _rwkv_post_bwd(dyr, y_scan, r_s, k_s, v_s, g_s, lnw, lnb, rkr,
                                                             name="rwkv_post_bwd")
    dr, dw, dk2, dv, dkap, db = _scan_bwd(r_s, w_s, k_s, v_s, kap_s, b_s, dysc, states, name="scan_bwd")
    dps, dwl, prep_sums = _rwkv_prep_bwd(q, mu_p, wl, w0r, a0r, kkr, kar,
                                         (dr, dw, dk2, dv, dkap, db, dg, dr_b, dk2_b, dv_b), name="rwkv_prep_bwd")
    dq, dmu = _tshift_bwd(dps, q, mu_p, name="tshift_bwd")
    du_pool, dpool_w, dpool_scale = _pool_bwd(dcat, o_pool, W["pool_w"], pscale, name="pool_bwd")
    dp = jnp.concatenate([du_pool, dq], axis=1)
    dw_in = _mm(h2, dp, ta=True, tm=1024, tn=512, tk=1024, name="mix_dwin")
    dh2 = _mm(dp, w_in_p, tb=True, tm=1024, tn=1024, tk=2816, name="mix_dh")
    dx1, pre2 = _pre_norm_mod_bwd(dh2, x1, dx2, gp[1], sc1p[1], name="mix_pre_bwd")

    dx0, dwgu1, dwd1, pre1, post1 = _ffn_backward(dx1, xs, sv1, wgu1, wd1, gp[0], gq[0], sc1p[0], gws[0], "ffn")

    pres, posts = [pre1, pre2, pre3], [post1, post2, post3]
    dmod = jnp.stack([jnp.stack([pres[s][0], pres[s][1], posts[s][0]]) for s in range(3)]).reshape(NMOD // 128, 128)
    dnorm_pre = jnp.stack([pres[s][2] for s in range(3)])
    dnorm_post = jnp.stack([posts[s][1] for s in range(3)])

    small = [dmu[0, :W["mu_shift"].shape[0]], dpool_w, dpool_scale, prep_sums[0], prep_sums[1], prep_sums[2],
             prep_sums[3], post_sums[2], post_sums[0], post_sums[1], dnorm_pre, dnorm_post,
             dwl[0:LORA_W, 0:R], dwl[LORA_W:LORA_W + LORA_A, R:2 * R],
             dwl[LORA_W + LORA_A:LORA_W + LORA_A + LORA_G, 2 * R:3 * R]]
    small_shapes = [a.shape for a in small]
    dmod8, small8 = _gather_two_level([dmod, _pack(small)], name="gather_grads")
    g_b_ada = _sum_parts(dmod8, name="sum_dmod").reshape(NMOD)
    red = _unpack(_sum_parts(small8, name="sum_small"), small_shapes)
    (g_mu, g_pool_w, g_pool_scale, g_w0, g_a0, g_kk, g_ka, g_rk, g_lnw, g_lnb, g_npre, g_npost, g_w2, g_a2,
     g_g2) = red

    dmod_all = dmod8.reshape(N_DEV, NMOD)
    dmod_cols = lax.dynamic_slice(dmod_all, (0, me * ada_c), (N_DEV, ada_c))
    dmod_cols = jnp.concatenate([dmod_cols, jnp.zeros_like(dmod_cols)], axis=0)
    g_w_ada = _mm(sc_pad, dmod_cols, ta=True, tm=D, tn=ada_c // 9, tk=16, name="ada_bwd")

    def by_core_chip(blocks):
        shp = blocks.shape
        t = blocks.astype(BF16).reshape((N_DEV // 2, 2) + shp[1:])
        return jnp.swapaxes(t, 0, 1)

    def scatter(blocks, tag):
        mine = [by_core_chip(b) for b in blocks]
        got = _sibling_swap(mine, name=f"{tag}_swap")
        sums = [_pair_add(m, g, name=f"{tag}_add{i}") for i, (m, g) in enumerate(zip(mine, got))]
        return _chips_all_to_all(sums, name=f"{tag}_chips")

    def scatter_ffn(dwgu, dwd, tag):
        dgate, dup = _deinterleave(dwgu)
        pg, pu, pd = scatter([_cols_split(dgate), _cols_split(dup), dwd.reshape(N_DEV, FF_TILE, D)], tag)
        fs = F // N_DEV
        return pg[:, :, :fs], pu[:, :, :fs], pd[:, :fs, :]

    pg2, pu2, pd2 = scatter_ffn(dwgu2, dwd2, "scatter_ffn")
    pin, pout = scatter([_cols_split(dw_in[:, :IN_W]), dw_out.reshape(N_DEV, D // N_DEV, D)], "scatter_mixer")
    pg1, pu1, pd1 = scatter_ffn(dwgu1, dwd1, "scatter_ffn")

    res = {}

    def big(nm, parts, tag):
        res[nm] = _adamw(W[nm], M1[nm], V1[nm], parts, name=tag)

    big("ffn1_w_gate", pg1, "adamw_cols")
    big("ffn1_w_up", pu1, "adamw_cols")
    big("ffn1_w_down", pd1, "adamw_rows")
    big("ffn2_w_gate", pg2, "adamw_cols")
    big("ffn2_w_up", pu2, "adamw_cols")
    big("ffn2_w_down", pd2, "adamw_rows")
    big("w_in", pin, "adamw_w_in")
    big("w_out", pout, "adamw_w_out")
    big("w_ada", g_w_ada[None], "adamw_w_ada")

    def my_cols(full, width):
        return lax.dynamic_slice_in_dim(full, me * width, width, axis=full.ndim - 1)

    small_names = ["b_ada", "mu_shift", "pool_w", "pool_scale", "w0", "a0", "k_k", "k_a", "r_k", "lnx_w", "lnx_b",
                   "norm_pre", "norm_post", "w2", "a2", "g2"]
    small_grads = [g_b_ada, g_mu, g_pool_w, g_pool_scale, g_w0, g_a0, g_kk, g_ka, g_rk.reshape(W["r_k"].shape), g_lnw,
                   g_lnb, my_cols(g_npre, D // N_DEV), my_cols(g_npost, D // N_DEV), my_cols(g_w2, R // N_DEV),
                   my_cols(g_a2, R // N_DEV), my_cols(g_g2, R // N_DEV)]
    shapes = [W[n].shape for n in small_names]
    packed = _adamw(_pack([W[n] for n in small_names]), _pack([M1[n] for n in small_names]),
                    _pack([V1[n] for n in small_names]), _pack(small_grads)[None], name="adamw_small")
    unpacked = [_unpack(t, shapes) for t in packed]
    for i, nm in enumerate(small_names):
        res[nm] = tuple(unpacked[k][i] for k in range(4))

    outs = [loss, dx0[None]]
    for k in range(4):
        outs.extend(res[nm][k][None] for nm in names)
    return tuple(outs)
```

```python
import functools

import jax
import jax.numpy as jnp
from jax import lax
from jax.experimental import pallas as pl
from jax.experimental.pallas import tpu as pltpu

F32 = jnp.float32
BF16 = jnp.bfloat16
N_DEV = 8
MESH_AXES = ("x", "y", "c")

NORM_EPS = 1e-6
HEAD = 64
LN_X_EPS = 1e-5 * HEAD
POOL_GROUPS = 4
POOL_GROUP = 128
MACARON = 0.5
LORA_W, LORA_A, LORA_G = 64, 64, 224
LORA_PAD = 384
ADAM_LR, ADAM_B1, ADAM_B2, ADAM_EPS, ADAM_WD, ADAM_STEP = 0.001, 0.9, 0.999, 1e-08, 0.01, 10

FF_TILE = 768
ROW_TILE = 256
SCAN_T = 64
SCAN_G = 6
VMEM_CAP = 56 * 1024 * 1024


def _cp(sem, vmem_mb):
    return pltpu.CompilerParams(dimension_semantics=sem, vmem_limit_bytes=min(vmem_mb * 1024 * 1024, VMEM_CAP))


def _my_index():
    return 4 * lax.axis_index("x") + 2 * lax.axis_index("y") + lax.axis_index("c")


def _exchange(arrays, *, scatter, name):
    n = len(arrays)
    out_shapes = []
    for a in arrays:
        shp = a.shape if scatter else (N_DEV,) + a.shape
        out_shapes.append(jax.ShapeDtypeStruct(shp, a.dtype))

    def body(*refs):
        ins, outs = refs[:n], refs[n:2 * n]
        send_sems, recv_sems, local_sems = refs[2 * n:]
        me = _my_index()

        def dev(p):
            return (p // 4, (p // 2) % 2, p % 2)

        def copy(i, d):
            peer = (me + d) % N_DEV
            src = ins[i].at[peer] if scatter else ins[i]
            return pltpu.make_async_remote_copy(
                src_ref=src, dst_ref=outs[i].at[me], send_sem=send_sems.at[i, d - 1],
                recv_sem=recv_sems.at[i, d - 1], device_id=dev(peer), device_id_type=pl.DeviceIdType.MESH)

        def arrival(i, d):
            frm = (me + N_DEV - d) % N_DEV
            src = ins[i].at[frm] if scatter else ins[i]
            return pltpu.make_async_remote_copy(
                src_ref=src, dst_ref=outs[i].at[frm], send_sem=send_sems.at[i, d - 1],
                recv_sem=recv_sems.at[i, d - 1], device_id=dev(frm), device_id_type=pl.DeviceIdType.MESH)

        locals_ = []
        for i in range(n):
            src = ins[i].at[me] if scatter else ins[i]
            lc = pltpu.make_async_copy(src, outs[i].at[me], local_sems.at[i])
            lc.start()
            locals_.append(lc)
        sends = [copy(i, d) for d in range(1, N_DEV) for i in range(n)]
        for cp in sends:
            cp.start()
        for d in range(1, N_DEV):
            for i in range(n):
                arrival(i, d).wait_recv()
        for cp in sends:
            cp.wait_send()
        for lc in locals_:
            lc.wait()

    hbm = pl.BlockSpec(memory_space=pltpu.HBM)
    return pl.pallas_call(
        body, name=name, out_shape=tuple(out_shapes), in_specs=[hbm] * n, out_specs=tuple([hbm] * n),
        scratch_shapes=[pltpu.SemaphoreType.DMA((n, N_DEV - 1)), pltpu.SemaphoreType.DMA((n, N_DEV - 1)),
                        pltpu.SemaphoreType.DMA((n,))],
    )(*arrays)


def _remote(src, dst, send_sem, recv_sem, to):
    return pltpu.make_async_remote_copy(src_ref=src, dst_ref=dst, send_sem=send_sem, recv_sem=recv_sem,
                                        device_id=to, device_id_type=pl.DeviceIdType.MESH)


def _gather_two_level(arrays, *, name):
    n = len(arrays)
    out_shapes = [jax.ShapeDtypeStruct((N_DEV,) + a.shape, a.dtype) for a in arrays]

    def body(*refs):
        ins, outs = refs[:n], refs[n:2 * n]
        send_sems, recv_sems, local_sems = refs[2 * n:]
        x, y, c = lax.axis_index("x"), lax.axis_index("y"), lax.axis_index("c")
        sibling = (x, y, 1 - c)
        chips = [(1 - x, y), (x, 1 - y), (1 - x, 1 - y)]

        def slot(i, px, py, pc):
            return outs[i].at[4 * px + 2 * py + pc]

        def copy(i, k, block, to, src=None):
            dst = slot(i, *block)
            return _remote(dst if src is None else src, dst, send_sems.at[i, k], recv_sems.at[i, k], to)

        locals_ = []
        for i in range(n):
            lc = pltpu.make_async_copy(ins[i], slot(i, x, y, c), local_sems.at[i])
            lc.start()
            locals_.append(lc)
        sends = []
        for j, chip in enumerate(chips):
            for i in range(n):
                sends.append(copy(i, 1 + j, (x, y, c), (*chip, c), src=ins[i]))
        for i in range(n):
            sends.append(copy(i, 0, (x, y, c), sibling, src=ins[i]))
        for cp in sends:
            cp.start()
        for j, chip in enumerate(chips):
            for i in range(n):
                copy(i, 1 + j, (*chip, c), (x, y, c)).wait_recv()
                fwd = copy(i, 4 + j, (*chip, c), sibling)
                fwd.start()
                sends.append(fwd)
        for i in range(n):
            copy(i, 0, (x, y, 1 - c), (x, y, c)).wait_recv()
        for j, chip in enumerate(chips):
            for i in range(n):
                copy(i, 4 + j, (*chip, 1 - c), (x, y, c)).wait_recv()
        for cp in sends:
            cp.wait_send()
        for lc in locals_:
            lc.wait()

    hbm = pl.BlockSpec(memory_space=pltpu.HBM)
    return pl.pallas_call(
        body, name=name, out_shape=tuple(out_shapes), in_specs=[hbm] * n, out_specs=tuple([hbm] * n),
        scratch_shapes=[pltpu.SemaphoreType.DMA((n, 7)), pltpu.SemaphoreType.DMA((n, 7)),
                        pltpu.SemaphoreType.DMA((n,))],
    )(*arrays)


def _sibling_swap(arrays, *, name):
    n = len(arrays)
    out_shapes = [jax.ShapeDtypeStruct(a.shape[1:], a.dtype) for a in arrays]

    def body(*refs):
        ins, outs = refs[:n], refs[n:2 * n]
        send_sems, recv_sems = refs[2 * n:]
        x, y, c = lax.axis_index("x"), lax.axis_index("y"), lax.axis_index("c")
        copies = [_remote(ins[i].at[1 - c], outs[i], send_sems.at[i], recv_sems.at[i], (x, y, 1 - c))
                  for i in range(n)]
        for cp in copies:
            cp.start()
        for cp in copies:
            cp.wait_recv()
        for cp in copies:
            cp.wait_send()

    hbm = pl.BlockSpec(memory_space=pltpu.HBM)
    return pl.pallas_call(
        body, name=name, out_shape=tuple(out_shapes), in_specs=[hbm] * n, out_specs=tuple([hbm] * n),
        scratch_shapes=[pltpu.SemaphoreType.DMA((n,)), pltpu.SemaphoreType.DMA((n,))],
    )(*arrays)


def _chips_all_to_all(arrays, *, name):
    n = len(arrays)
    out_shapes = [jax.ShapeDtypeStruct(a.shape, a.dtype) for a in arrays]

    def body(*refs):
        ins, outs = refs[:n], refs[n:2 * n]
        send_sems, recv_sems, local_sems = refs[2 * n:]
        x, y, c = lax.axis_index("x"), lax.axis_index("y"), lax.axis_index("c")
        mine = 2 * x + y
        chips = [(1 - x, y), (x, 1 - y), (1 - x, 1 - y)]
        locals_ = []
        for i in range(n):
            lc = pltpu.make_async_copy(ins[i].at[mine], outs[i].at[mine], local_sems.at[i])
            lc.start()
            locals_.append(lc)
        sends = [_remote(ins[i].at[2 * chip[0] + chip[1]], outs[i].at[mine], send_sems.at[i, j], recv_sems.at[i, j],
                         (*chip, c)) for j, chip in enumerate(chips) for i in range(n)]
        for cp in sends:
            cp.start()
        for j, chip in enumerate(chips):
            for i in range(n):
                q = 2 * chip[0] + chip[1]
                _remote(ins[i].at[q], outs[i].at[q], send_sems.at[i, j], recv_sems.at[i, j], (*chip, c)).wait_recv()
        for cp in sends:
            cp.wait_send()
        for lc in locals_:
            lc.wait()

    hbm = pl.BlockSpec(memory_space=pltpu.HBM)
    return pl.pallas_call(
        body, name=name, out_shape=tuple(out_shapes), in_specs=[hbm] * n, out_specs=tuple([hbm] * n),
        scratch_shapes=[pltpu.SemaphoreType.DMA((n, 3)), pltpu.SemaphoreType.DMA((n, 3)),
                        pltpu.SemaphoreType.DMA((n,))],
    )(*arrays)


def _pair_add(mine, got, *, name):
    _, nq, R, C = mine.shape
    tr = R
    for cand in (512, 256, 128, 64, 32, 16):
        if R % cand == 0 and cand * C * 2 * 3 * 2 <= 12 * 1024 * 1024:
            tr = cand
            break

    def body(core_ref, m_ref, g_ref, o_ref):
        o_ref[0] = (m_ref[0, 0].astype(F32) + g_ref[0].astype(F32)).astype(BF16)

    core = lax.axis_index("c").astype(jnp.int32).reshape(1)
    return pl.pallas_call(
        body, name=name,
        grid_spec=pltpu.PrefetchScalarGridSpec(
            num_scalar_prefetch=1, grid=(nq, R // tr),
            in_specs=[pl.BlockSpec((1, 1, tr, C), lambda q, i, core_ref: (core_ref[0], q, i, 0)),
                      pl.BlockSpec((1, tr, C), lambda q, i, core_ref: (q, i, 0))],
            out_specs=pl.BlockSpec((1, tr, C), lambda q, i, core_ref: (q, i, 0))),
        out_shape=jax.ShapeDtypeStruct((nq, R, C), BF16),
        compiler_params=_cp(("parallel", "parallel"), 40),
    )(core, mine, got)


def _mm(a, b, *, ta=False, tb=False, tm, tn, tk, out_dtype=F32, name):
    M = a.shape[1] if ta else a.shape[0]
    K = a.shape[0] if ta else a.shape[1]
    N = b.shape[0] if tb else b.shape[1]
    tm, tn, tk = min(tm, M), min(tn, N), min(tk, K)
    assert M % tm == 0 and N % tn == 0 and K % tk == 0, (name, M, N, K, tm, tn, tk)
    nk = K // tk
    dims = (((0 if ta else 1,), (1 if tb else 0,)), ((), ()))

    def body(a_ref, b_ref, o_ref, acc_ref):
        k = pl.program_id(2)

        @pl.when(k == 0)
        def _():
            acc_ref[...] = jnp.zeros_like(acc_ref)

        acc_ref[...] += lax.dot_general(a_ref[...].astype(BF16), b_ref[...].astype(BF16), dims,
                                        preferred_element_type=F32)

        @pl.when(k == nk - 1)
        def _():
            o_ref[...] = acc_ref[...].astype(out_dtype)

    a_spec = pl.BlockSpec((tk, tm), lambda i, j, k: (k, i)) if ta else pl.BlockSpec((tm, tk), lambda i, j, k: (i, k))
    b_spec = pl.BlockSpec((tn, tk), lambda i, j, k: (j, k)) if tb else pl.BlockSpec((tk, tn), lambda i, j, k: (k, j))
    blk = 2 * (tm * tk * a.dtype.itemsize + tk * tn * b.dtype.itemsize + tm * tn * jnp.dtype(out_dtype).itemsize)
    return pl.pallas_call(
        body, name=name, grid=(M // tm, N // tn, nk), in_specs=[a_spec, b_spec],
        out_specs=pl.BlockSpec((tm, tn), lambda i, j, k: (i, j)),
        out_shape=jax.ShapeDtypeStruct((M, N), out_dtype),
        scratch_shapes=[pltpu.VMEM((tm, tn), F32)],
        compiler_params=_cp(("parallel", "parallel", "arbitrary"), (blk + tm * tn * 4) // (1024 * 1024) + 12),
    )(a, b)


def _ffn_up(h, g8, u8, *, tm, tk, name):
    S, D = h.shape
    nb, _, tn = g8.shape
    tm = min(tm, S)
    tk = min(tk, D)
    nk = D // tk

    def body(h_ref, g_ref, u_ref, au_ref, s_ref, acc_ref):
        k = pl.program_id(2)

        @pl.when(k == 0)
        def _():
            acc_ref[...] = jnp.zeros_like(acc_ref)

        hv = h_ref[...]
        acc_ref[:, :tn] += jnp.dot(hv, g_ref[0], preferred_element_type=F32)
        acc_ref[:, tn:] += jnp.dot(hv, u_ref[0], preferred_element_type=F32)

        @pl.when(k == nk - 1)
        def _():
            acc = acc_ref[...]
            a = acc[:, :tn]
            u = acc[:, tn:]
            au_ref[...] = acc.astype(BF16)
            s_ref[...] = (a * jax.nn.sigmoid(a) * u).astype(BF16)

    wspec = pl.BlockSpec((1, tk, tn), lambda i, j, k: (j, k, 0))
    return pl.pallas_call(
        body, name=name, grid=(S // tm, nb, nk),
        in_specs=[pl.BlockSpec((tm, tk), lambda i, j, k: (i, k)), wspec, wspec],
        out_specs=(pl.BlockSpec((tm, 2 * tn), lambda i, j, k: (i, j)), pl.BlockSpec((tm, tn), lambda i, j, k: (i, j))),
        out_shape=(jax.ShapeDtypeStruct((S, 2 * nb * tn), BF16), jax.ShapeDtypeStruct((S, nb * tn), BF16)),
        scratch_shapes=[pltpu.VMEM((tm, 2 * tn), F32)],
        compiler_params=_cp(("parallel", "parallel", "arbitrary"), 52),
    )(h, g8, u8)


def _ffn_dh(dau, g8, u8, *, tm, tn, name):
    S = dau.shape[0]
    nb, D, tf = g8.shape
    tm, tn = min(tm, S), min(tn, D)
    nk = 2 * nb
    nt = (((1,), (1,)), ((), ()))

    def body(a_ref, g_ref, u_ref, o_ref, acc_ref):
        k = pl.program_id(2)

        @pl.when(k == 0)
        def _():
            acc_ref[...] = jnp.zeros_like(acc_ref)

        @pl.when(k % 2 == 0)
        def _():
            acc_ref[...] += lax.dot_general(a_ref[...], g_ref[0], nt, preferred_element_type=F32)

        @pl.when(k % 2 == 1)
        def _():
            acc_ref[...] += lax.dot_general(a_ref[...], u_ref[0], nt, preferred_element_type=F32)

        @pl.when(k == nk - 1)
        def _():
            o_ref[...] = acc_ref[...]

    wspec = pl.BlockSpec((1, tn, tf), lambda i, n, k: (k // 2, n, 0))
    return pl.pallas_call(
        body, name=name, grid=(S // tm, D // tn, nk),
        in_specs=[pl.BlockSpec((tm, tf), lambda i, n, k: (i, k)), wspec, wspec],
        out_specs=pl.BlockSpec((tm, tn), lambda i, n, k: (i, n)),
        out_shape=jax.ShapeDtypeStruct((S, D), F32),
        scratch_shapes=[pltpu.VMEM((tm, tn), F32)],
        compiler_params=_cp(("parallel", "parallel", "arbitrary"), 40),
    )(dau, g8, u8)


def _ffn_dwgu(h, dau, *, tm, tk, name):
    S, D = h.shape
    tf = FF_TILE
    nt = dau.shape[1] // tf
    tm, tk = min(tm, D), min(tk, S)
    nk = S // tk
    ni = D // tm

    def body(a_ref, b_ref, o_ref, acc_ref):
        k = pl.program_id(2)

        @pl.when(k == 0)
        def _():
            acc_ref[...] = jnp.zeros_like(acc_ref)

        acc_ref[...] += lax.dot_general(a_ref[...], b_ref[...], (((0,), (0,)), ((), ())), preferred_element_type=F32)

        @pl.when(k == nk - 1)
        def _():
            o_ref[0, 0] = acc_ref[...].astype(BF16)

    return pl.pallas_call(
        body, name=name, grid=(ni, nt, nk),
        in_specs=[pl.BlockSpec((tk, tm), lambda i, j, k: (k, i)), pl.BlockSpec((tk, tf), lambda i, j, k: (k, j))],
        out_specs=pl.BlockSpec((1, 1, tm, tf), lambda i, j, k: ((j // 2) % 2, j // 4, (j % 2) * ni + i, 0)),
        out_shape=jax.ShapeDtypeStruct((2, nt // 4, 2 * D, tf), BF16),
        scratch_shapes=[pltpu.VMEM((tm, tf), F32)],
        compiler_params=_cp(("parallel", "parallel", "arbitrary"), 40),
    )(h, dau)


def _ffn_dwd(s, df, *, tn, tk, name):
    S, D = df.shape
    tf = FF_TILE
    nb = s.shape[1] // tf
    tn, tk = min(tn, D), min(tk, S)
    nk = S // tk

    def body(a_ref, b_ref, o_ref, acc_ref):
        k = pl.program_id(2)

        @pl.when(k == 0)
        def _():
            acc_ref[...] = jnp.zeros_like(acc_ref)

        acc_ref[...] += lax.dot_general(a_ref[...], b_ref[...], (((0,), (0,)), ((), ())), preferred_element_type=F32)

        @pl.when(k == nk - 1)
        def _():
            o_ref[0, 0] = acc_ref[...].astype(BF16)

    return pl.pallas_call(
        body, name=name, grid=(nb, D // tn, nk),
        in_specs=[pl.BlockSpec((tk, tf), lambda j, n, k: (k, j)), pl.BlockSpec((tk, tn), lambda j, n, k: (k, n))],
        out_specs=pl.BlockSpec((1, 1, tf, tn), lambda j, n, k: (j % 2, j // 2, 0, n)),
        out_shape=jax.ShapeDtypeStruct((2, nb // 2, tf, D), BF16),
        scratch_shapes=[pltpu.VMEM((tf, tn), F32)],
        compiler_params=_cp(("parallel", "parallel", "arbitrary"), 40),
    )(s, df)


def _ffn_down_bwd(df, d8, au, *, tm, tk, name):
    S, D = df.shape
    nb, tn, _ = d8.shape
    F = nb * tn
    tm = min(tm, S)
    tk = min(tk, D)
    nk = D // tk

    def body(df_ref, w_ref, au_ref, dau_ref, acc_ref):
        k = pl.program_id(2)

        @pl.when(k == 0)
        def _():
            acc_ref[...] = jnp.zeros_like(acc_ref)

        acc_ref[...] += lax.dot_general(df_ref[...], w_ref[0], (((1,), (1,)), ((), ())), preferred_element_type=F32)

        @pl.when(k == nk - 1)
        def _():
            ds = acc_ref[...]
            au_v = au_ref[...].astype(F32)
            a = au_v[:, :tn]
            u = au_v[:, tn:]
            sg = jax.nn.sigmoid(a)
            da = ds * u * (sg * (1.0 + a * (1.0 - sg)))
            du = ds * (a * sg)
            dau_ref[:, :tn] = da.astype(BF16)
            dau_ref[:, tn:] = du.astype(BF16)

    return pl.pallas_call(
        body, name=name, grid=(S // tm, F // tn, nk),
        in_specs=[pl.BlockSpec((tm, tk), lambda i, j, k: (i, k)), pl.BlockSpec((1, tn, tk), lambda i, j, k: (j, 0, k)),
                  pl.BlockSpec((tm, 2 * tn), lambda i, j, k: (i, j))],
        out_specs=pl.BlockSpec((tm, 2 * tn), lambda i, j, k: (i, j)),
        out_shape=jax.ShapeDtypeStruct((S, 2 * F), BF16),
        scratch_shapes=[pltpu.VMEM((tm, tn), F32)],
        compiler_params=_cp(("parallel", "parallel", "arbitrary"), 52),
    )(df, d8, au)


def _fold8(x):
    tm, w = x.shape
    return jnp.sum(x.reshape(tm // 8, 8, w), axis=0)


def _row_spec(tm, w):
    return pl.BlockSpec((tm, w), lambda i: (i, 0))


def _vec_spec(rows, w):
    return pl.BlockSpec((rows, w), lambda i: (0, 0))


def _pre_norm_mod(x, gain, shift, scale1p, *, name):
    S, D = x.shape
    tm = ROW_TILE

    def body(x_ref, g_ref, sh_ref, sc_ref, h_ref):
        xv = x_ref[...]
        rinv = lax.rsqrt(jnp.mean(xv * xv, axis=-1, keepdims=True) + NORM_EPS)
        h_ref[...] = ((xv * rinv) * g_ref[...] * sc_ref[...] + sh_ref[...]).astype(BF16)

    return pl.pallas_call(
        body, name=name, grid=(S // tm,),
        in_specs=[_row_spec(tm, D), _vec_spec(1, D), _vec_spec(1, D), _vec_spec(1, D)],
        out_specs=_row_spec(tm, D), out_shape=jax.ShapeDtypeStruct((S, D), BF16),
        compiler_params=_cp(("parallel",), 32),
    )(x, gain, shift, scale1p)


def _pre_norm_mod_bwd(dh, x, dres, gain, scale1p, *, name):
    S, D = x.shape
    tm = ROW_TILE
    n = S // tm

    def body(dh_ref, x_ref, dr_ref, g_ref, sc_ref, dx_ref, sums_ref, acc_ref):
        i = pl.program_id(0)

        @pl.when(i == 0)
        def _():
            acc_ref[...] = jnp.zeros_like(acc_ref)

        xv = x_ref[...]
        dhv = dh_ref[...]
        g = g_ref[...]
        rinv = lax.rsqrt(jnp.mean(xv * xv, axis=-1, keepdims=True) + NORM_EPS)
        xn = xv * rinv
        dn = dhv * sc_ref[...]
        dxn = dn * g
        dx_ref[...] = dr_ref[...] + rinv * (dxn - xn * jnp.mean(dxn * xn, axis=-1, keepdims=True))
        acc_ref[0] += _fold8(dhv)
        acc_ref[1] += _fold8(dhv * (xn * g))
        acc_ref[2] += _fold8(dn * xn)

        @pl.when(i == n - 1)
        def _():
            for q in range(3):
                sums_ref[q:q + 1, :] = jnp.sum(acc_ref[q], axis=0, keepdims=True)

    return pl.pallas_call(
        body, name=name, grid=(n,),
        in_specs=[_row_spec(tm, D), _row_spec(tm, D), _row_spec(tm, D), _vec_spec(1, D), _vec_spec(1, D)],
        out_specs=(_row_spec(tm, D), _vec_spec(3, D)),
        out_shape=(jax.ShapeDtypeStruct((S, D), F32), jax.ShapeDtypeStruct((3, D), F32)),
        scratch_shapes=[pltpu.VMEM((3, 8, D), F32)],
        compiler_params=_cp(("arbitrary",), 40),
    )(dh, x, dres, gain, scale1p)


def _post_norm_res(x, f, gain, gw, *, name):
    S, D = x.shape
    tm = ROW_TILE

    def body(x_ref, f_ref, g_ref, gw_ref, o_ref):
        fv = f_ref[...]
        rinv = lax.rsqrt(jnp.mean(fv * fv, axis=-1, keepdims=True) + NORM_EPS)
        o_ref[...] = x_ref[...] + gw_ref[...] * ((fv * rinv) * g_ref[...])

    return pl.pallas_call(
        body, name=name, grid=(S // tm,),
        in_specs=[_row_spec(tm, D), _row_spec(tm, D), _vec_spec(1, D), _vec_spec(1, D)],
        out_specs=_row_spec(tm, D), out_shape=jax.ShapeDtypeStruct((S, D), F32),
        compiler_params=_cp(("parallel",), 32),
    )(x, f, gain, gw)


def _post_norm_res_bwd(dxo, f, gain, gw, weight, *, name):
    S, D = f.shape
    tm = ROW_TILE
    n = S // tm

    def body(d_ref, f_ref, g_ref, gw_ref, df_ref, sums_ref, acc_ref):
        i = pl.program_id(0)

        @pl.when(i == 0)
        def _():
            acc_ref[...] = jnp.zeros_like(acc_ref)

        fv = f_ref[...]
        dv = d_ref[...]
        g = g_ref[...]
        rinv = lax.rsqrt(jnp.mean(fv * fv, axis=-1, keepdims=True) + NORM_EPS)
        fh = fv * rinv
        dy = dv * gw_ref[...]
        dfh = dy * g
        df_ref[...] = (rinv * (dfh - fh * jnp.mean(dfh * fh, axis=-1, keepdims=True))).astype(BF16)
        acc_ref[0] += _fold8(weight * dv * (fh * g))
        acc_ref[1] += _fold8(dy * fh)

        @pl.when(i == n - 1)
        def _():
            for q in range(2):
                sums_ref[q:q + 1, :] = jnp.sum(acc_ref[q], axis=0, keepdims=True)

    return pl.pallas_call(
        body, name=name, grid=(n,),
        in_specs=[_row_spec(tm, D), _row_spec(tm, D), _vec_spec(1, D), _vec_spec(1, D)],
        out_specs=(_row_spec(tm, D), _vec_spec(2, D)),
        out_shape=(jax.ShapeDtypeStruct((S, D), BF16), jax.ShapeDtypeStruct((2, D), F32)),
        scratch_shapes=[pltpu.VMEM((2, 8, D), F32)],
        compiler_params=_cp(("arbitrary",), 40),
    )(dxo, f, gain, gw)


def _loss_head(y, target, *, name):
    S, D = y.shape
    tm = ROW_TILE

    def body(y_ref, t_ref, l_ref, dy_ref):
        i = pl.program_id(0)

        @pl.when(i == 0)
        def _():
            l_ref[...] = jnp.zeros_like(l_ref)

        err = y_ref[...] - t_ref[...]
        dy_ref[...] = err * (1.0 / D)
        row = jnp.sum(err * err, axis=-1, keepdims=True) * (0.5 / D)
        l_ref[...] += jnp.sum(row, axis=0, keepdims=True)

    return pl.pallas_call(
        body, name=name, grid=(S // tm,),
        in_specs=[_row_spec(tm, D), _row_spec(tm, D)],
        out_specs=(_vec_spec(1, 1), _row_spec(tm, D)),
        out_shape=(jax.ShapeDtypeStruct((1, 1), F32), jax.ShapeDtypeStruct((S, D), F32)),
        compiler_params=_cp(("arbitrary",), 32),
    )(y, target)


def _shift_down(z, j, row):
    return jnp.where(row >= j, pltpu.roll(z, j, 0), 0.0)


def _shift_up(z, j, row, n):
    return jnp.where(row < n - j, pltpu.roll(z, n - j, 0), 0.0)


def _pool_fwd(p, pool_w, pool_scale, *, name):
    S = p.shape[0]
    C = POOL_GROUP

    def body(u_ref, w_ref, sc_ref, o_ref, y_ref):
        g = pl.program_id(0)
        u = u_ref[...]
        row = lax.broadcasted_iota(jnp.int32, (S, C), 0)
        s1 = u + _shift_down(u, 1, row)
        s2 = s1 + _shift_down(s1, 2, row)
        s3 = s2 + _shift_down(s2, 4, row)
        s4 = s3 + _shift_down(s3, 8, row)
        gi = jnp.zeros((S, C), jnp.int32) + g
        win = jnp.where(gi == 0, s1, jnp.where(gi == 1, s2, jnp.where(gi == 2, s3, s4)))
        width = jnp.where(gi == 0, 2, jnp.where(gi == 1, 4, jnp.where(gi == 2, 8, 16)))
        count = jnp.minimum(row + 1, width).astype(F32)
        o = win / count - u
        o_ref[...] = o
        y_ref[...] = jnp.dot(o.astype(BF16), w_ref[0].astype(BF16), preferred_element_type=F32) * sc_ref[...]

    col = pl.BlockSpec((S, C), lambda g: (0, g))
    return pl.pallas_call(
        body, name=name, grid=(POOL_GROUPS,),
        in_specs=[col, pl.BlockSpec((1, C, C), lambda g: (g, 0, 0)), pl.BlockSpec((1, C), lambda g: (0, g))],
        out_specs=(col, col),
        out_shape=(jax.ShapeDtypeStruct((S, POOL_GROUPS * C), F32), jax.ShapeDtypeStruct((S, POOL_GROUPS * C), F32)),
        compiler_params=_cp(("parallel",), 48),
    )(p, pool_w, pool_scale)


def _pool_bwd(dcat, o, pool_w, pool_scale, *, name):
    S = o.shape[0]
    C = POOL_GROUP

    def body(dy_ref, o_ref, w_ref, sc_ref, du_ref, dw_ref, dsc_ref):
        g = pl.program_id(0)
        dy = dy_ref[...]
        ob = o_ref[...].astype(BF16)
        wb = w_ref[0].astype(BF16)
        mixed = jnp.dot(ob, wb, preferred_element_type=F32)
        dsc_ref[...] = jnp.sum(_fold8(dy * mixed), axis=0, keepdims=True)
        dmix = (dy * sc_ref[...]).astype(BF16)
        dw_ref[0] = lax.dot_general(ob, dmix, (((0,), (0,)), ((), ())), preferred_element_type=F32)
        do = lax.dot_general(dmix, wb, (((1,), (1,)), ((), ())), preferred_element_type=F32)
        row = lax.broadcasted_iota(jnp.int32, (S, C), 0)
        gi = jnp.zeros((S, C), jnp.int32) + g
        width = jnp.where(gi == 0, 2, jnp.where(gi == 1, 4, jnp.where(gi == 2, 8, 16)))
        z = do / jnp.minimum(row + 1, width).astype(F32)
        s1 = z + _shift_up(z, 1, row, S)
        s2 = s1 + _shift_up(s1, 2, row, S)
        s3 = s2 + _shift_up(s2, 4, row, S)
        s4 = s3 + _shift_up(s3, 8, row, S)
        win = jnp.where(gi == 0, s1, jnp.where(gi == 1, s2, jnp.where(gi == 2, s3, s4)))
        du_ref[...] = (win - do).astype(BF16)

    col = pl.BlockSpec((S, C), lambda g: (0, g))
    return pl.pallas_call(
        body, name=name, grid=(POOL_GROUPS,),
        in_specs=[col, col, pl.BlockSpec((1, C, C), lambda g: (g, 0, 0)), pl.BlockSpec((1, C), lambda g: (0, g))],
        out_specs=(col, pl.BlockSpec((1, C, C), lambda g: (g, 0, 0)), pl.BlockSpec((1, C), lambda g: (0, g))),
        out_shape=(jax.ShapeDtypeStruct((S, POOL_GROUPS * C), BF16), jax.ShapeDtypeStruct((POOL_GROUPS, C, C), F32),
                   jax.ShapeDtypeStruct((1, POOL_GROUPS * C), F32)),
        compiler_params=_cp(("parallel",), 48),
    )(dcat, o, pool_w, pool_scale)


def _block_ones():
    r = lax.broadcasted_iota(jnp.int32, (128, 128), 0) // HEAD
    c = lax.broadcasted_iota(jnp.int32, (128, 128), 1) // HEAD
    return jnp.where(r == c, 1.0, 0.0).astype(BF16)


def _segsum(x, bd):
    outs = []
    for j in range(x.shape[1] // 128):
        xs = x[:, j * 128:(j + 1) * 128]
        hi = xs.astype(BF16)
        lo = (xs - hi.astype(F32)).astype(BF16)
        outs.append(jnp.dot(hi, bd, preferred_element_type=F32) + jnp.dot(lo, bd, preferred_element_type=F32))
    return jnp.concatenate(outs, axis=1)


def _prep_common(q, qprev, first, mu, wl, w0, a0, kkw, kaw, R):
    tm = q.shape[0]
    row = lax.broadcasted_iota(jnp.int32, q.shape, 0)
    last = qprev[7:8, :] * first
    prev = jnp.where(row == 0, last, pltpu.roll(q, 1, 0))
    ps = q + mu * (prev - q)
    r = ps[:, 0:R]
    k = ps[:, R:2 * R]
    v = ps[:, 2 * R:3 * R]
    lo_in = ps[:, 3 * R:3 * R + LORA_PAD]
    lane = lax.broadcasted_iota(jnp.int32, (tm, LORA_PAD), 1)
    m_w = lane < LORA_W
    m_a = lane < LORA_W + LORA_A
    m_g = lane < LORA_W + LORA_A + LORA_G
    act = jnp.where(m_w, jnp.tanh(lo_in), jnp.where(m_a, lo_in, jnp.where(m_g, jax.nn.sigmoid(lo_in), 0.0)))
    lo = jnp.dot(act.astype(BF16), wl, preferred_element_type=F32)
    wpre = w0 + lo[:, 0:R]
    apre = a0 + lo[:, R:2 * R]
    g = lo[:, 2 * R:3 * R]
    neg = -wpre
    softplus = jnp.maximum(neg, 0.0) + jnp.log(1.0 + jnp.exp(-jnp.abs(neg)))
    wlog = -softplus - 0.5
    ew = jnp.exp(wlog)
    decay = jnp.exp(-ew)
    a = jax.nn.sigmoid(apre)
    kk = k * kkw
    bd = _block_ones()
    n2 = _segsum(kk * kk, bd)
    nrm = jnp.maximum(jnp.sqrt(n2), 1e-12)
    kap = kk / nrm
    kmul = 1.0 + (a - 1.0) * kaw
    k2 = k * kmul
    return dict(prev=prev, r=r, k=k, v=v, act=act, m_w=m_w, m_a=m_a, m_g=m_g, wpre=wpre, g=g, ew=ew, decay=decay,
                a=a, n2=n2, nrm=nrm, kap=kap, kmul=kmul, k2=k2, bd=bd)


def _prev_rows_spec(tm, w):
    return pl.BlockSpec((8, w), lambda i: (jnp.maximum(i * (tm // 8) - 1, 0), 0))


def _rwkv_prep(q, mu, wl, w0, a0, kkw, kaw, *, name):
    S, QW = q.shape
    R = w0.shape[1]
    tm = ROW_TILE // 2

    def body(q_ref, qp_ref, mu_ref, wl_ref, w0_ref, a0_ref, kk_ref, ka_ref, r_ref, w_ref, k_ref, v_ref, kap_ref,
             b_ref, g_ref):
        first = jnp.where(pl.program_id(0) > 0, 1.0, 0.0)
        t = _prep_common(q_ref[...], qp_ref[...], first, mu_ref[...], wl_ref[...], w0_ref[...], a0_ref[...],
                         kk_ref[...], ka_ref[...], R)
        r_ref[...] = t["r"]
        w_ref[...] = t["decay"]
        k_ref[...] = t["k2"]
        v_ref[...] = t["v"]
        kap_ref[...] = t["kap"]
        b_ref[...] = t["kap"] * t["a"]
        g_ref[...] = t["g"]

    vec = _vec_spec(1, R)
    return pl.pallas_call(
        body, name=name, grid=(S // tm,),
        in_specs=[_row_spec(tm, QW), _prev_rows_spec(tm, QW), _vec_spec(1, QW), _vec_spec(LORA_PAD, 3 * R), vec, vec,
                  vec, vec],
        out_specs=tuple([_row_spec(tm, R)] * 7),
        out_shape=tuple([jax.ShapeDtypeStruct((S, R), F32)] * 7),
        compiler_params=_cp(("parallel",), 48),
    )(q, q, mu, wl, w0, a0, kkw, kaw)


def _rwkv_prep_bwd(q, mu, wl, w0, a0, kkw, kaw, grads, *, name):
    S, QW = q.shape
    R = w0.shape[1]
    tm = ROW_TILE // 2
    n = S // tm

    def body(q_ref, qp_ref, mu_ref, wl_ref, w0_ref, a0_ref, kk_ref, ka_ref, dr_ref, dw_ref, dk2_ref, dv_ref, dkap_ref,
             db_ref, dg_ref, drb_ref, dk2b_ref, dvb_ref, dps_ref, dwl_ref, sums_ref, acc_ref):
        i = pl.program_id(0)

        @pl.when(i == 0)
        def _():
            acc_ref[...] = jnp.zeros_like(acc_ref)
            dwl_ref[...] = jnp.zeros_like(dwl_ref)

        first = jnp.where(i > 0, 1.0, 0.0)
        wl = wl_ref[...]
        kkw = kk_ref[...]
        kaw = ka_ref[...]
        t = _prep_common(q_ref[...], qp_ref[...], first, mu_ref[...], wl, w0_ref[...], a0_ref[...], kkw, kaw, R)
        a, kap, k, act = t["a"], t["kap"], t["k"], t["act"]
        db = db_ref[...]
        dk2 = dk2_ref[...] + dk2b_ref[...]
        dkap = dkap_ref[...] + db * a
        da = db * kap + dk2 * k * kaw
        dk = dk2 * t["kmul"]
        proj = jnp.where(jnp.sqrt(t["n2"]) > 1e-12, _segsum(kap * dkap, t["bd"]), 0.0)
        dkk = (dkap - kap * proj) / t["nrm"]
        dk = dk + dkk * kkw
        dapre = da * a * (1.0 - a)
        dwlog = dw_ref[...] * t["decay"] * (-t["ew"])
        dwpre = dwlog * jax.nn.sigmoid(-t["wpre"])
        acc_ref[0] += _fold8(dwpre)
        acc_ref[1] += _fold8(dapre)
        acc_ref[2] += _fold8(dkk * k)
        acc_ref[3] += _fold8(dk2 * k * (a - 1.0))
        dlo = jnp.concatenate([dwpre, dapre, dg_ref[...]], axis=1).astype(BF16)
        dwl_ref[...] += lax.dot_general(act.astype(BF16), dlo, (((0,), (0,)), ((), ())), preferred_element_type=F32)
        dact = lax.dot_general(dlo, wl, (((1,), (1,)), ((), ())), preferred_element_type=F32)
        dlin = jnp.where(t["m_w"], dact * (1.0 - act * act),
                         jnp.where(t["m_a"], dact, jnp.where(t["m_g"], dact * act * (1.0 - act), 0.0)))
        dps_ref[:, 0:R] = dr_ref[...] + drb_ref[...]
        dps_ref[:, R:2 * R] = dk
        dps_ref[:, 2 * R:3 * R] = dv_ref[...] + dvb_ref[...]
        dps_ref[:, 3 * R:3 * R + LORA_PAD] = dlin
        dps_ref[:, 3 * R + LORA_PAD:] = jnp.zeros((tm, QW - 3 * R - LORA_PAD), F32)

        @pl.when(i == n - 1)
        def _():
            for j in range(4):
                sums_ref[j:j + 1, :] = jnp.sum(acc_ref[j], axis=0, keepdims=True)

    vec = _vec_spec(1, R)
    return pl.pallas_call(
        body, name=name, grid=(n,),
        in_specs=[_row_spec(tm, QW), _prev_rows_spec(tm, QW), _vec_spec(1, QW), _vec_spec(LORA_PAD, 3 * R), vec, vec,
                  vec, vec] + [_row_spec(tm, R)] * 10,
        out_specs=(_row_spec(tm, QW), _vec_spec(LORA_PAD, 3 * R), _vec_spec(4, R)),
        out_shape=(jax.ShapeDtypeStruct((S, QW), F32), jax.ShapeDtypeStruct((LORA_PAD, 3 * R), F32),
                   jax.ShapeDtypeStruct((4, R), F32)),
        scratch_shapes=[pltpu.VMEM((4, 8, R), F32)],
        compiler_params=_cp(("arbitrary",), 56),
    )(q, q, mu, wl, w0, a0, kkw, kaw, *grads)


def _tshift_bwd(dps, q, mu, *, name):
    S, QW = q.shape
    tm = ROW_TILE // 2
    n = S // tm

    def body(d_ref, dn_ref, q_ref, qp_ref, mu_ref, dq_ref, dmu_ref, acc_ref):
        i = pl.program_id(0)

        @pl.when(i == 0)
        def _():
            acc_ref[...] = jnp.zeros_like(acc_ref)

        mu = mu_ref[...]
        d = d_ref[...]
        qv = q_ref[...]
        row = lax.broadcasted_iota(jnp.int32, d.shape, 0)
        first = jnp.where(i > 0, 1.0, 0.0)
        notlast = jnp.where(i < n - 1, 1.0, 0.0)
        prev = jnp.where(row == 0, qp_ref[7:8, :] * first, pltpu.roll(qv, 1, 0))
        z = d * mu
        nxt = jnp.where(row == tm - 1, dn_ref[0:1, :] * mu * notlast, pltpu.roll(z, tm - 1, 0))
        dq_ref[...] = (d * (1.0 - mu) + nxt).astype(BF16)
        acc_ref[...] += _fold8(d * (prev - qv))

        @pl.when(i == n - 1)
        def _():
            dmu_ref[...] = jnp.sum(acc_ref[...], axis=0, keepdims=True)

    nblk8 = S // 8
    next_spec = pl.BlockSpec((8, QW), lambda i: (jnp.minimum((i + 1) * (tm // 8), nblk8 - 1), 0))
    return pl.pallas_call(
        body, name=name, grid=(n,),
        in_specs=[_row_spec(tm, QW), next_spec, _row_spec(tm, QW), _prev_rows_spec(tm, QW), _vec_spec(1, QW)],
        out_specs=(_row_spec(tm, QW), _vec_spec(1, QW)),
        out_shape=(jax.ShapeDtypeStruct((S, QW), BF16), jax.ShapeDtypeStruct((1, QW), F32)),
        scratch_shapes=[pltpu.VMEM((8, QW), F32)],
        compiler_params=_cp(("arbitrary",), 48),
    )(dps, dps, q, q, mu)


def _post_common(ysc, r, k2, v, lnw, lnb, rk):
    bd = _block_ones()
    mean = _segsum(ysc, bd) * (1.0 / HEAD)
    d = ysc - mean
    var = _segsum(d * d, bd) * (1.0 / HEAD)
    rstd = lax.rsqrt(var + LN_X_EPS)
    yh = d * rstd
    rkk = _segsum(r * k2 * rk, bd)
    z = yh * lnw + lnb + rkk * v
    return bd, rstd, yh, rkk, z


def _rwkv_post(ysc, r, k2, v, g, ypool, lnw, lnb, rk, *, name):
    S, R = ysc.shape
    PW = ypool.shape[1]
    tm = ROW_TILE

    def body(y_ref, r_ref, k_ref, v_ref, g_ref, yp_ref, lw_ref, lb_ref, rk_ref, cat_ref):
        _, _, _, _, z = _post_common(y_ref[...], r_ref[...], k_ref[...], v_ref[...], lw_ref[...], lb_ref[...],
                                     rk_ref[...])
        cat_ref[:, 0:PW] = yp_ref[...].astype(BF16)
        cat_ref[:, PW:] = (z * g_ref[...]).astype(BF16)

    vec = _vec_spec(1, R)
    return pl.pallas_call(
        body, name=name, grid=(S // tm,),
        in_specs=[_row_spec(tm, R)] * 5 + [_row_spec(tm, PW), vec, vec, vec],
        out_specs=_row_spec(tm, PW + R), out_shape=jax.ShapeDtypeStruct((S, PW + R), BF16),
        compiler_params=_cp(("parallel",), 48),
    )(ysc, r, k2, v, g, ypool, lnw, lnb, rk)


def _rwkv_post_bwd(dcat, ysc, r, k2, v, g, lnw, lnb, rk, *, name):
    S, R = ysc.shape
    tm = ROW_TILE
    n = S // tm

    def body(d_ref, y_ref, r_ref, k_ref, v_ref, g_ref, lw_ref, lb_ref, rk_ref, dy_ref, dg_ref, drb_ref, dkb_ref,
             dvb_ref, sums_ref, acc_ref):
        i = pl.program_id(0)

        @pl.when(i == 0)
        def _():
            acc_ref[...] = jnp.zeros_like(acc_ref)

        rv, kv, vv, lw, rkw = r_ref[...], k_ref[...], v_ref[...], lw_ref[...], rk_ref[...]
        bd, rstd, yh, rkk, z = _post_common(y_ref[...], rv, kv, vv, lw, lb_ref[...], rkw)
        dyr = d_ref[...]
        dg_ref[...] = dyr * z
        dz = dyr * g_ref[...]
        dyh = dz * lw
        dy_ref[...] = rstd * (dyh - _segsum(dyh, bd) * (1.0 / HEAD) - yh * (_segsum(dyh * yh, bd) * (1.0 / HEAD)))
        dvb_ref[...] = dz * rkk
        drkk = _segsum(dz * vv, bd)
        drb_ref[...] = drkk * kv * rkw
        dkb_ref[...] = drkk * rv * rkw
        acc_ref[0] += _fold8(dz * yh)
        acc_ref[1] += _fold8(dz)
        acc_ref[2] += _fold8(drkk * rv * kv)

        @pl.when(i == n - 1)
        def _():
            for j in range(3):
                sums_ref[j:j + 1, :] = jnp.sum(acc_ref[j], axis=0, keepdims=True)

    vec = _vec_spec(1, R)
    dspec = _row_spec(tm, R)
    return pl.pallas_call(
        body, name=name, grid=(n,),
        in_specs=[dspec] + [_row_spec(tm, R)] * 5 + [vec, vec, vec],
        out_specs=tuple([_row_spec(tm, R)] * 5) + (_vec_spec(3, R),),
        out_shape=tuple([jax.ShapeDtypeStruct((S, R), F32)] * 5) + (jax.ShapeDtypeStruct((3, R), F32),),
        scratch_shapes=[pltpu.VMEM((3, 8, R), F32)],
        compiler_params=_cp(("arbitrary",), 56),
    )(dcat, ysc, r, k2, v, g, lnw, lnb, rk)


def _half_sums(x, m_a):
    s_a = jnp.sum(jnp.where(m_a, x, 0.0), axis=1, keepdims=True)
    s_b = jnp.sum(jnp.where(m_a, 0.0, x), axis=1, keepdims=True)
    return s_a, s_b


SEL_ROWS = 32


def _column_selector():
    row = lax.broadcasted_iota(jnp.int32, (SEL_ROWS, 8 * 128), 0)
    col = lax.broadcasted_iota(jnp.int32, (SEL_ROWS, 8 * 128), 1)
    return jnp.where((row < 24) & (row % 8 == col // 128), 1.0, 0.0).astype(BF16)


def _expand_columns(x, sel):
    hi = x.astype(BF16).astype(F32)
    r1 = x - hi
    mid = r1.astype(BF16).astype(F32)
    lo = (r1 - mid).astype(BF16).astype(F32)
    terms = jnp.concatenate([hi, mid, lo, jnp.zeros_like(x)], axis=0).astype(BF16)
    return lax.dot_general(terms, sel, (((0,), (0,)), ((), ())), preferred_element_type=F32)


def _scan_fwd(r, w, k, v, kap, b, *, name):
    S, R = r.shape
    G, T = SCAN_G, SCAN_T
    NP = R // 128
    assert NP % G == 0 and S % T == 0
    GW = 128 * G

    def body(r_ref, w_ref, k_ref, v_ref, kap_ref, b_ref, sel_ref, y_ref, st_ref, s_scr, vc_scr, yt_scr):
        c = pl.program_id(1)

        @pl.when(c == 0)
        def _():
            s_scr[...] = jnp.zeros_like(s_scr)

        yt_scr[...] = jnp.zeros_like(yt_scr)
        lane = lax.broadcasted_iota(jnp.int32, (HEAD, 128), 1)
        m_a = lane < HEAD

        def block(tb, carry):
            t0 = pl.multiple_of(tb * 8, 8)
            rb, wb, kb = r_ref[pl.ds(t0, 8), :], w_ref[pl.ds(t0, 8), :], k_ref[pl.ds(t0, 8), :]
            pb, bb, vb = kap_ref[pl.ds(t0, 8), :], b_ref[pl.ds(t0, 8), :], v_ref[pl.ds(t0, 8), :]
            for g in range(G):
                vc_scr[g] = _expand_columns(vb[:, g * 128:(g + 1) * 128], sel_ref[...])

            def put_y(g, parts, hot_y):
                yt_scr[g, 0:HEAD, :] = jnp.where(hot_y, parts[0], yt_scr[g, 0:HEAD, :])
                yt_scr[g, HEAD:, :] = jnp.where(hot_y, parts[1], yt_scr[g, HEAD:, :])

            for j in range(8):
                t = t0 + j
                cols = slice(j * 128, (j + 1) * 128)
                sa_parts, y_parts = [], []
                for g in range(G):
                    sl = slice(g * 128, (g + 1) * 128)
                    sa_parts.append(_half_sums(s_scr[g] * pb[j:j + 1, sl], m_a))
                if j > 0:
                    for g in range(G):
                        sl = slice(g * 128, (g + 1) * 128)
                        y_parts.append(_half_sums(s_scr[g] * rb[j - 1:j, sl], m_a))
                for g in range(G):
                    sl = slice(g * 128, (g + 1) * 128)
                    sa = -jnp.where(m_a, sa_parts[g][0], sa_parts[g][1])
                    vcol = jnp.where(m_a, vc_scr[g, 0:HEAD, cols], vc_scr[g, HEAD:, cols])
                    st = s_scr[g] * wb[j:j + 1, sl] + sa * bb[j:j + 1, sl] + vcol * kb[j:j + 1, sl]
                    s_scr[g] = st
                    st_ref[g, t] = st
                if j > 0:
                    hot_y = lane == t - 1
                    for g in range(G):
                        put_y(g, y_parts[g], hot_y)
            hot_y = lane == t0 + 7
            for g in range(G):
                sl = slice(g * 128, (g + 1) * 128)
                put_y(g, _half_sums(s_scr[g] * rb[7:8, sl], m_a), hot_y)
            return carry

        lax.fori_loop(0, T // 8, block, 0)
        for g in range(G):
            y_ref[:, g * 128:(g + 1) * 128] = yt_scr[g].T[0:T, :]

    tspec = pl.BlockSpec((T, GW), lambda p, c: (c, p))
    sel_spec = pl.BlockSpec((SEL_ROWS, 8 * 128), lambda p, c: (0, 0))
    return pl.pallas_call(
        body, name=name, grid=(NP // G, S // T),
        in_specs=[tspec] * 6 + [sel_spec],
        out_specs=(tspec, pl.BlockSpec((G, T, HEAD, 128), lambda p, c: (p, c, 0, 0))),
        out_shape=(jax.ShapeDtypeStruct((S, R), F32), jax.ShapeDtypeStruct((NP, S, HEAD, 128), F32)),
        scratch_shapes=[pltpu.VMEM((G, HEAD, 128), F32), pltpu.VMEM((G, 128, 8 * 128), F32),
                        pltpu.VMEM((G, 128, 128), F32)],
        compiler_params=_cp(("parallel", "arbitrary"), 48),
    )(r, w, k, v, kap, b, _column_selector())


def _scan_bwd(r, w, k, v, kap, b, dy, states, *, name):
    S, R = r.shape
    G, T = SCAN_G, SCAN_T
    NP = R // 128
    NC = S // T
    GW = 128 * G

    def body(r_ref, w_ref, k_ref, v_ref, kap_ref, b_ref, dy_ref, st_ref, sp_ref, sel_ref, dr_ref, dw_ref, dk_ref,
             dv_ref, dkap_ref, db_ref, ds_scr, vc_scr, dyc_scr, dvt_scr, sa_scr):
        ci = pl.program_id(1)

        @pl.when(ci == 0)
        def _():
            ds_scr[...] = jnp.zeros_like(ds_scr)

        dvt_scr[...] = jnp.zeros_like(dvt_scr)
        lane = lax.broadcasted_iota(jnp.int32, (HEAD, 128), 1)
        m_a = lane < HEAD
        sub = lax.broadcasted_iota(jnp.int32, (8, 128), 0)
        zero_i = jnp.zeros((HEAD, 128), jnp.int32)
        has_prev = jnp.where(ci < NC - 1, 1.0, 0.0)

        def state_before(g, t):
            at_start = (zero_i + t) == 0
            return jnp.where(at_start, sp_ref[g, 0] * has_prev, st_ref[g, jnp.maximum(t - 1, 0)])

        def sa_sums(g, t, p_row):
            return _half_sums(state_before(g, t) * p_row, m_a)

        def put_sa(g, parts):
            sa_scr[g] = -jnp.where(m_a, parts[0], parts[1])

        pb_last = kap_ref[pl.ds(T - 8, 8), :]
        for g in range(G):
            put_sa(g, sa_sums(g, T - 1, pb_last[7:8, g * 128:(g + 1) * 128]))

        def block(it, carry):
            tb = T // 8 - 1 - it
            t0 = pl.multiple_of(tb * 8, 8)
            rb, wb, kb = r_ref[pl.ds(t0, 8), :], w_ref[pl.ds(t0, 8), :], k_ref[pl.ds(t0, 8), :]
            pb, bb = kap_ref[pl.ds(t0, 8), :], b_ref[pl.ds(t0, 8), :]
            vb, dyb = v_ref[pl.ds(t0, 8), :], dy_ref[pl.ds(t0, 8), :]
            pb_prev = kap_ref[pl.ds(pl.multiple_of(jnp.maximum(t0 - 8, 0), 8), 8), :]
            for g in range(G):
                vc_scr[g] = _expand_columns(vb[:, g * 128:(g + 1) * 128], sel_ref[...])
                dyc_scr[g] = _expand_columns(dyb[:, g * 128:(g + 1) * 128], sel_ref[...])
            outs = [[jnp.zeros((8, 128), F32) for _ in range(5)] for _ in range(G)]
            for j in range(7, -1, -1):
                t = t0 + j
                hot = lane == t
                cols = slice(j * 128, (j + 1) * 128)
                ds_parts, next_sa = [], []
                for g in range(G):
                    sl = slice(g * 128, (g + 1) * 128)
                    dycol = jnp.where(m_a, dyc_scr[g, 0:HEAD, cols], dyc_scr[g, HEAD:, cols])
                    ds = ds_scr[g] + dycol * rb[j:j + 1, sl]
                    ds_scr[g] = ds
                    ds_parts.append((_half_sums(ds * kb[j:j + 1, sl], m_a), _half_sums(ds * bb[j:j + 1, sl], m_a)))
                for g in range(G):
                    sl = slice(g * 128, (g + 1) * 128)
                    p_row = pb[j - 1:j, sl] if j > 0 else pb_prev[7:8, sl]
                    next_sa.append(sa_sums(g, jnp.maximum(t - 1, 0), p_row))
                for g in range(G):
                    sl = slice(g * 128, (g + 1) * 128)
                    w_r, p_r = wb[j:j + 1, sl], pb[j:j + 1, sl]
                    ds = ds_scr[g]
                    s_p = state_before(g, t)
                    dycol = jnp.where(m_a, dyc_scr[g, 0:HEAD, cols], dyc_scr[g, HEAD:, cols])
                    vcol = jnp.where(m_a, vc_scr[g, 0:HEAD, cols], vc_scr[g, HEAD:, cols])
                    sa = sa_scr[g]
                    dr_row = jnp.sum(st_ref[g, t] * dycol, axis=0, keepdims=True)
                    dk_row = jnp.sum(ds * vcol, axis=0, keepdims=True)
                    db_row = jnp.sum(ds * sa, axis=0, keepdims=True)
                    dw_row = jnp.sum(ds * s_p, axis=0, keepdims=True)
                    (dv_a, dv_b), (dsa_a, dsa_b) = ds_parts[g]
                    dvt_scr[g, 0:HEAD, :] = jnp.where(hot, dv_a, dvt_scr[g, 0:HEAD, :])
                    dvt_scr[g, HEAD:, :] = jnp.where(hot, dv_b, dvt_scr[g, HEAD:, :])
                    dsa = jnp.where(m_a, dsa_a, dsa_b)
                    dkap_row = -jnp.sum(s_p * dsa, axis=0, keepdims=True)
                    ds_scr[g] = ds * w_r - dsa * p_r
                    pick = sub == j
                    for q, row in enumerate((dr_row, dw_row, dk_row, dkap_row, db_row)):
                        outs[g][q] = jnp.where(pick, row, outs[g][q])
                for g in range(G):
                    put_sa(g, next_sa[g])
            for g in range(G):
                sl = slice(g * 128, (g + 1) * 128)
                for q, ref in enumerate((dr_ref, dw_ref, dk_ref, dkap_ref, db_ref)):
                    ref[pl.ds(t0, 8), sl] = outs[g][q]
            return carry

        lax.fori_loop(0, T // 8, block, 0)
        for g in range(G):
            dv_ref[:, g * 128:(g + 1) * 128] = dvt_scr[g].T[0:T, :]

    tspec = pl.BlockSpec((T, GW), lambda p, c: (NC - 1 - c, p))
    st_spec = pl.BlockSpec((G, T, HEAD, 128), lambda p, c: (p, NC - 1 - c, 0, 0))
    prev_spec = pl.BlockSpec((G, 1, HEAD, 128), lambda p, c: (p, jnp.maximum((NC - 1 - c) * T - 1, 0), 0, 0))
    sel_spec = pl.BlockSpec((SEL_ROWS, 8 * 128), lambda p, c: (0, 0))
    return pl.pallas_call(
        body, name=name, grid=(NP // G, NC),
        in_specs=[tspec] * 7 + [st_spec, prev_spec, sel_spec],
        out_specs=tuple([tspec] * 6),
        out_shape=tuple([jax.ShapeDtypeStruct((S, R), F32)] * 6),
        scratch_shapes=[pltpu.VMEM((G, HEAD, 128), F32), pltpu.VMEM((G, 128, 8 * 128), F32),
                        pltpu.VMEM((G, 128, 8 * 128), F32), pltpu.VMEM((G, 128, 128), F32),
                        pltpu.VMEM((G, HEAD, 128), F32)],
        compiler_params=_cp(("parallel", "arbitrary"), 48),
    )(r, w, k, v, kap, b, dy, states, states, _column_selector())


def _sum_parts(parts, *, name):
    P, rows, W = parts.shape
    tr = rows
    for cand in (1024, 512, 256, 128, 64, 32, 16, 8):
        if rows % cand == 0:
            tr = cand
            break

    def body(p_ref, o_ref):
        acc = p_ref[0]
        for s in range(1, P):
            acc = acc + p_ref[s]
        o_ref[...] = acc

    return pl.pallas_call(
        body, name=name, grid=(rows // tr,),
        in_specs=[pl.BlockSpec((P, tr, W), lambda i: (0, i, 0))],
        out_specs=pl.BlockSpec((tr, W), lambda i: (i, 0)), out_shape=jax.ShapeDtypeStruct((rows, W), F32),
        compiler_params=_cp(("parallel",), 32),
    )(parts)


def _adamw(w, m, v, parts, *, name):
    R, C = w.shape
    P = parts.shape[0]
    tr = R
    for cand in (1024, 512, 256, 128, 64, 32, 16, 8):
        if R % cand == 0 and cand * C * 4 * (7 + P) <= 10 * 1024 * 1024:
            tr = cand
            break
    bc1 = 1.0 - ADAM_B1 ** ADAM_STEP
    bc2 = 1.0 - ADAM_B2 ** ADAM_STEP

    def body(w_ref, m_ref, v_ref, p_ref, g_ref, d_ref, nm_ref, nv_ref):
        g = p_ref[0].astype(F32)
        for s in range(1, P):
            g = g + p_ref[s].astype(F32)
        m1 = ADAM_B1 * m_ref[...] + (1.0 - ADAM_B1) * g
        v1 = ADAM_B2 * v_ref[...] + (1.0 - ADAM_B2) * (g * g)
        m_hat = m1 / bc1
        v_hat = v1 / bc2
        g_ref[...] = g
        d_ref[...] = -ADAM_LR * (m_hat / (jnp.sqrt(v_hat) + ADAM_EPS) + ADAM_WD * w_ref[...])
        nm_ref[...] = m1
        nv_ref[...] = v1

    spec = pl.BlockSpec((tr, C), lambda i: (i, 0))
    return pl.pallas_call(
        body, name=name, grid=(R // tr,),
        in_specs=[spec, spec, spec, pl.BlockSpec((P, tr, C), lambda i: (0, i, 0))],
        out_specs=(spec, spec, spec, spec), out_shape=tuple([jax.ShapeDtypeStruct((R, C), F32)] * 4),
        compiler_params=_cp(("parallel",), 40),
    )(w, m, v, parts)


def _cols_full(g8):
    n, rows, c = g8.shape
    return jnp.transpose(g8, (1, 0, 2)).reshape(rows, n * c)


def _cols_split(full):
    rows, cols = full.shape
    return jnp.transpose(full.reshape(rows, N_DEV, cols // N_DEV), (1, 0, 2))


def _pack(vals, rows_multiple=512):
    flat = jnp.concatenate([v.reshape(-1).astype(F32) for v in vals])
    n = flat.shape[0]
    unit = 128 * rows_multiple
    padded = ((n + unit - 1) // unit) * unit
    return jnp.pad(flat, (0, padded - n)).reshape(padded // 128, 128)


def _unpack(packed, shapes):
    flat = packed.reshape(-1)
    out, off = [], 0
    for shp in shapes:
        size = 1
        for d in shp:
            size *= d
        out.append(flat[off:off + size].reshape(shp))
        off += size
    return out


def _ffn_forward(x, weights, gpre, gpost, shift, scale1p, gw, tag):
    g8, u8, d8 = weights
    h = _pre_norm_mod(x, gpre, shift, scale1p, name=f"{tag}_pre")
    au, s = _ffn_up(h, g8, u8, tm=1024, tk=2048, name=f"{tag}_up")
    f = _mm(s, d8.reshape(-1, d8.shape[2]), tm=1024, tn=1024, tk=2048, name=f"{tag}_down")
    xo = _post_norm_res(x, f, gpost, gw, name=f"{tag}_post")
    return xo, (h, au, s, f)


def _ffn_backward(dxo, x, saved, weights, gpre, gpost, scale1p, gw, tag):
    g8, u8, d8 = weights
    h, au, s, f = saved
    df, post_sums = _post_norm_res_bwd(dxo, f, gpost, gw, MACARON, name=f"{tag}_post_bwd")
    dwd = _ffn_dwd(s, df, tn=1024, tk=1024, name=f"{tag}_dwd")
    dau = _ffn_down_bwd(df, d8, au, tm=1024, tk=2048, name=f"{tag}_down_bwd")
    dwgu = _ffn_dwgu(h, dau, tm=1024, tk=1024, name=f"{tag}_dwgu")
    dh = _ffn_dh(dau, g8, u8, tm=1024, tn=1024, name=f"{tag}_dh")
    dx, pre_sums = _pre_norm_mod_bwd(dh, x, dxo, gpre, scale1p, name=f"{tag}_pre_bwd")
    return dx, dwgu, dwd, pre_sums, post_sums


def kernel(x, c, w_ada, b_ada, norm_pre, norm_post, ffn1_w_gate, ffn1_w_up, ffn1_w_down, w_in, mu_shift, pool_w, pool_scale, w0, w2, a0, a2, g2, k_k, k_a, r_k, lnx_w, lnx_b, w_out, ffn2_w_gate, ffn2_w_up, ffn2_w_down, loss_target, m_w_ada, m_b_ada, m_norm_pre, m_norm_post, m_ffn1_w_gate, m_ffn1_w_up, m_ffn1_w_down, m_w_in, m_mu_shift, m_pool_w, m_pool_scale, m_w0, m_w2, m_a0, m_a2, m_g2, m_k_k, m_k_a, m_r_k, m_lnx_w, m_lnx_b, m_w_out, m_ffn2_w_gate, m_ffn2_w_up, m_ffn2_w_down, v_w_ada, v_b_ada, v_norm_pre, v_norm_post, v_ffn1_w_gate, v_ffn1_w_up, v_ffn1_w_down, v_w_in, v_mu_shift, v_pool_w, v_pool_scale, v_w0, v_w2, v_a0, v_a2, v_g2, v_k_k, v_k_a, v_r_k, v_lnx_w, v_lnx_b, v_w_out, v_ffn2_w_gate, v_ffn2_w_up, v_ffn2_w_down):
    names = ["w_ada", "b_ada", "norm_pre", "norm_post", "ffn1_w_gate", "ffn1_w_up", "ffn1_w_down", "w_in", "mu_shift",
             "pool_w", "pool_scale", "w0", "w2", "a0", "a2", "g2", "k_k", "k_a", "r_k", "lnx_w", "lnx_b", "w_out",
             "ffn2_w_gate", "ffn2_w_up", "ffn2_w_down"]
    env = dict(locals())
    W = {n: env[n][0] for n in names}
    M1 = {n: env["m_" + n][0] for n in names}
    V1 = {n: env["v_" + n][0] for n in names}

    me = _my_index()
    xs = x[0]
    tgt = loss_target[0]
    S, D = xs.shape
    F = W["ffn1_w_gate"].shape[1] * N_DEV
    R = W["w0"].shape[0]
    PW = D - R
    IN_W = W["w_in"].shape[1] * N_DEV
    P_W = F
    QW = P_W - PW
    NMOD = 9 * D
    ada_c = W["w_ada"].shape[1]

    c_all, npre8, npost8, w2_8, a2_8, g2_8 = _exchange(
        [c, W["norm_pre"], W["norm_post"], W["w2"].astype(BF16), W["a2"].astype(BF16), W["g2"].astype(BF16)],
        scatter=False, name="gather_small")
    c_all = c_all.reshape(N_DEV, D)
    gpre = _cols_full(npre8)
    gpost = _cols_full(npost8)
    wl = jnp.zeros((LORA_PAD, 3 * R), BF16)
    wl = wl.at[0:LORA_W, 0:R].set(_cols_full(w2_8))
    wl = wl.at[LORA_W:LORA_W + LORA_A, R:2 * R].set(_cols_full(a2_8))
    wl = wl.at[LORA_W + LORA_A:LORA_W + LORA_A + LORA_G, 2 * R:3 * R].set(_cols_full(g2_8))

    sc_all = jax.nn.silu(c_all)
    sc_pad = jnp.concatenate([sc_all, jnp.zeros((8, D), F32)], axis=0).astype(BF16)
    modcols = _mm(sc_pad, W["w_ada"], tm=16, tn=ada_c, tk=256, name="ada_fwd")[0:N_DEV]
    modcols = modcols + lax.dynamic_slice(W["b_ada"], (me * ada_c,), (ada_c,))[None, :]
    (mod8,) = _exchange([modcols], scatter=False, name="gather_mod")
    mod = lax.dynamic_index_in_dim(mod8, me, axis=1, keepdims=False).reshape(9, D)

    def mod_row(i):
        return mod[i:i + 1, :]

    f_pad = FF_TILE - F // N_DEV

    def gather_ffn(tag, name):
        return _gather_two_level(
            [jnp.pad(W[f"{tag}_w_gate"].astype(BF16), ((0, 0), (0, f_pad))),
             jnp.pad(W[f"{tag}_w_up"].astype(BF16), ((0, 0), (0, f_pad))),
             jnp.pad(W[f"{tag}_w_down"].astype(BF16), ((0, f_pad), (0, 0)))], name=name)

    ffn1_w = gather_ffn("ffn1", "gather_ffn")
    win8, wout8 = _gather_two_level([W["w_in"].astype(BF16), W["w_out"].astype(BF16)], name="gather_mixer")
    w_in_p = jnp.pad(_cols_full(win8), ((0, 0), (0, P_W - IN_W)))
    w_out_f = wout8.reshape(D, D)
    ffn2_w = gather_ffn("ffn2", "gather_ffn")

    mu_p = jnp.pad(W["mu_shift"], (0, QW - W["mu_shift"].shape[0]))[None, :]
    vec = lambda a: a.reshape(1, -1)
    w0r, a0r, kkr, kar = vec(W["w0"]), vec(W["a0"]), vec(W["k_k"]), vec(W["k_a"])
    lnw, lnb, rkr = vec(W["lnx_w"]), vec(W["lnx_b"]), vec(W["r_k"])
    pscale = vec(W["pool_scale"])

    sc1p = [1.0 + mod_row(3 * s + 1) for s in range(3)]
    shifts = [mod_row(3 * s) for s in range(3)]
    wgts = [MACARON, 1.0, MACARON]
    gws = [wgts[s] * (1.0 + mod_row(3 * s + 2)) for s in range(3)]
    gp = [gpre[s:s + 1] for s in range(3)]
    gq = [gpost[s:s + 1] for s in range(3)]

    x1, sv1 = _ffn_forward(xs, ffn1_w, gp[0], gq[0], shifts[0], sc1p[0], gws[0], "ffn")

    h2 = _pre_norm_mod(x1, gp[1], shifts[1], sc1p[1], name="mix_pre")
    p = _mm(h2, w_in_p, tm=1024, tn=512, tk=2048, name="mix_in")
    q = p[:, PW:]
    o_pool, y_pool = _pool_fwd(p, W["pool_w"], pscale, name="pool_fwd")
    r_s, w_s, k_s, v_s, kap_s, b_s, g_s = _rwkv_prep(q, mu_p, wl, w0r, a0r, kkr, kar, name="rwkv_prep")
    y_scan, states = _scan_fwd(r_s, w_s, k_s, v_s, kap_s, b_s, name="scan_fwd")
    cat = _rwkv_post(y_scan, r_s, k_s, v_s, g_s, y_pool, lnw, lnb, rkr, name="rwkv_post")
    f2 = _mm(cat, w_out_f, tm=1024, tn=1024, tk=2048, name="mix_out")
    x2 = _post_norm_res(x1, f2, gq[1], gws[1], name="mix_post")

    x3, sv3 = _ffn_forward(x2, ffn2_w, gp[2], gq[2], shifts[2], sc1p[2], gws[2], "ffn")

    loss_part, dx3 = _loss_head(x3, tgt, name="loss_head")
    loss = lax.psum(loss_part[0, 0], MESH_AXES)

    dx2, dwgu2, dwd2, pre3, post3 = _ffn_backward(dx3, x2, sv3, ffn2_w, gp[2], gq[2], sc1p[2], gws[2], "ffn")

    df2, post2 = _post_norm_res_bwd(dx2, f2, gq[1], gws[1], 1.0, name="mix_post_bwd")
    dw_out = _mm(cat, df2, ta=True, tm=1024, tn=1024, tk=1024, name="mix_dwout")
    dcat = _mm(df2, w_out_f, tb=True, tm=1024, tn=1024, tk=2048, name="mix_dcat")
    dyr = dcat[:, PW:]
    dysc, dg, dr_b, dk2_b, dv_b, post_sums = _rwkv_post_bwd(dyr, y_scan, r_s, k_s, v_s, g_s, lnw, lnb, rkr,
                                                             name="rwkv_post_bwd")
    dr, dw, dk2, dv, dkap, db = _scan_bwd(r_s, w_s, k_s, v_s, kap_s, b_s, dysc, states, name="scan_bwd")
    dps, dwl, prep_sums = _rwkv_prep_bwd(q, mu_p, wl, w0r, a0r, kkr, kar,
                                         (dr, dw, dk2, dv, dkap, db, dg, dr_b, dk2_b, dv_b), name="rwkv_prep_bwd")
    dq, dmu = _tshift_bwd(dps, q, mu_p, name="tshift_bwd")
    du_pool, dpool_w, dpool_scale = _pool_bwd(dcat, o_pool, W["pool_w"], pscale, name="pool_bwd")
    dp = jnp.concatenate([du_pool, dq], axis=1)
    dw_in = _mm(h2, dp, ta=True, tm=1024, tn=512, tk=1024, name="mix_dwin")
    dh2 = _mm(dp, w_in_p, tb=True, tm=1024, tn=1024, tk=2816, name="mix_dh")
    dx1, pre2 = _pre_norm_mod_bwd(dh2, x1, dx2, gp[1], sc1p[1], name="mix_pre_bwd")

    dx0, dwgu1, dwd1, pre1, post1 = _ffn_backward(dx1, xs, sv1, ffn1_w, gp[0], gq[0], sc1p[0], gws[0], "ffn")

    pres, posts = [pre1, pre2, pre3], [post1, post2, post3]
    dmod = jnp.stack([jnp.stack([pres[s][0], pres[s][1], posts[s][0]]) for s in range(3)]).reshape(NMOD // 128, 128)
    dnorm_pre = jnp.stack([pres[s][2] for s in range(3)])
    dnorm_post = jnp.stack([posts[s][1] for s in range(3)])

    small = [dmu[0, :W["mu_shift"].shape[0]], dpool_w, dpool_scale, prep_sums[0], prep_sums[1], prep_sums[2],
             prep_sums[3], post_sums[2], post_sums[0], post_sums[1], dnorm_pre, dnorm_post,
             dwl[0:LORA_W, 0:R], dwl[LORA_W:LORA_W + LORA_A, R:2 * R],
             dwl[LORA_W + LORA_A:LORA_W + LORA_A + LORA_G, 2 * R:3 * R]]
    small_shapes = [a.shape for a in small]
    dmod8, small8 = _gather_two_level([dmod, _pack(small)], name="gather_grads")
    g_b_ada = _sum_parts(dmod8, name="sum_dmod").reshape(NMOD)
    red = _unpack(_sum_parts(small8, name="sum_small"), small_shapes)
    (g_mu, g_pool_w, g_pool_scale, g_w0, g_a0, g_kk, g_ka, g_rk, g_lnw, g_lnb, g_npre, g_npost, g_w2, g_a2,
     g_g2) = red

    dmod_all = dmod8.reshape(N_DEV, NMOD)
    dmod_cols = lax.dynamic_slice(dmod_all, (0, me * ada_c), (N_DEV, ada_c))
    dmod_cols = jnp.concatenate([dmod_cols, jnp.zeros_like(dmod_cols)], axis=0)
    g_w_ada = _mm(sc_pad, dmod_cols, ta=True, tm=D, tn=ada_c // 9, tk=16, name="ada_bwd")

    def by_core_chip(blocks):
        shp = blocks.shape
        t = blocks.astype(BF16).reshape((N_DEV // 2, 2) + shp[1:])
        return jnp.swapaxes(t, 0, 1)

    def scatter(mine, tag):
        got = _sibling_swap(mine, name=f"{tag}_swap")
        sums = [_pair_add(m, g, name=f"{tag}_add{i}") for i, (m, g) in enumerate(zip(mine, got))]
        return _chips_all_to_all(sums, name=f"{tag}_chips")

    def scatter_ffn(dwgu, dwd, tag):
        pgu, pd = scatter([dwgu, dwd], tag)
        fs = F // N_DEV
        return pgu[:, :D, :fs], pgu[:, D:, :fs], pd[:, :fs, :]

    pg2, pu2, pd2 = scatter_ffn(dwgu2, dwd2, "scatter_ffn")
    pin, pout = scatter([by_core_chip(_cols_split(dw_in[:, :IN_W])),
                         by_core_chip(dw_out.reshape(N_DEV, D // N_DEV, D))], "scatter_mixer")
    pg1, pu1, pd1 = scatter_ffn(dwgu1, dwd1, "scatter_ffn")

    res = {}

    def big(nm, parts, tag):
        res[nm] = _adamw(W[nm], M1[nm], V1[nm], parts, name=tag)

    big("ffn1_w_gate", pg1, "adamw_cols")
    big("ffn1_w_up", pu1, "adamw_cols")
    big("ffn1_w_down", pd1, "adamw_rows")
    big("ffn2_w_gate", pg2, "adamw_cols")
    big("ffn2_w_up", pu2, "adamw_cols")
    big("ffn2_w_down", pd2, "adamw_rows")
    big("w_in", pin, "adamw_w_in")
    big("w_out", pout, "adamw_w_out")
    big("w_ada", g_w_ada[None], "adamw_w_ada")

    def my_cols(full, width):
        return lax.dynamic_slice_in_dim(full, me * width, width, axis=full.ndim - 1)

    small_names = ["b_ada", "mu_shift", "pool_w", "pool_scale", "w0", "a0", "k_k", "k_a", "r_k", "lnx_w", "lnx_b",
                   "norm_pre", "norm_post", "w2", "a2", "g2"]
    small_grads = [g_b_ada, g_mu, g_pool_w, g_pool_scale, g_w0, g_a0, g_kk, g_ka, g_rk.reshape(W["r_k"].shape), g_lnw,
                   g_lnb, my_cols(g_npre, D // N_DEV), my_cols(g_npost, D // N_DEV), my_cols(g_w2, R // N_DEV),
                   my_cols(g_a2, R // N_DEV), my_cols(g_g2, R // N_DEV)]
    shapes = [W[n].shape for n in small_names]
    packed = _adamw(_pack([W[n] for n in small_names]), _pack([M1[n] for n in small_names]),
                    _pack([V1[n] for n in small_names]), _pack(small_grads)[None], name="adamw_small")
    unpacked = [_unpack(t, shapes) for t in packed]
    for i, nm in enumerate(small_names):
        res[nm] = tuple(unpacked[k][i] for k in range(4))

    outs = [loss, dx0[None]]
    for k in range(4):
        outs.extend(res[nm][k][None] for nm in names)
    return tuple(outs)
```

```python
import functools

import jax
import jax.numpy as jnp
from jax import lax
from jax.experimental import pallas as pl
from jax.experimental.pallas import tpu as pltpu

F32 = jnp.float32
BF16 = jnp.bfloat16
N_DEV = 8
MESH_AXES = ("x", "y", "c")

NORM_EPS = 1e-6
HEAD = 64
LN_X_EPS = 1e-5 * HEAD
POOL_GROUPS = 4
POOL_GROUP = 128
MACARON = 0.5
LORA_W, LORA_A, LORA_G = 64, 64, 224
LORA_PAD = 384
ADAM_LR, ADAM_B1, ADAM_B2, ADAM_EPS, ADAM_WD, ADAM_STEP = 0.001, 0.9, 0.999, 1e-08, 0.01, 10

FF_TILE = 768
ROW_TILE = 256
SCAN_T = 64
SCAN_G = 6
VMEM_CAP = 56 * 1024 * 1024


def _cp(sem, vmem_mb):
    return pltpu.CompilerParams(dimension_semantics=sem, vmem_limit_bytes=min(vmem_mb * 1024 * 1024, VMEM_CAP))


def _my_index():
    return 4 * lax.axis_index("x") + 2 * lax.axis_index("y") + lax.axis_index("c")


def _exchange(arrays, *, scatter, name):
    n = len(arrays)
    out_shapes = []
    for a in arrays:
        shp = a.shape if scatter else (N_DEV,) + a.shape
        out_shapes.append(jax.ShapeDtypeStruct(shp, a.dtype))

    def body(*refs):
        ins, outs = refs[:n], refs[n:2 * n]
        send_sems, recv_sems, local_sems = refs[2 * n:]
        me = _my_index()

        def dev(p):
            return (p // 4, (p // 2) % 2, p % 2)

        def copy(i, d):
            peer = (me + d) % N_DEV
            src = ins[i].at[peer] if scatter else ins[i]
            return pltpu.make_async_remote_copy(
                src_ref=src, dst_ref=outs[i].at[me], send_sem=send_sems.at[i, d - 1],
                recv_sem=recv_sems.at[i, d - 1], device_id=dev(peer), device_id_type=pl.DeviceIdType.MESH)

        def arrival(i, d):
            frm = (me + N_DEV - d) % N_DEV
            src = ins[i].at[frm] if scatter else ins[i]
            return pltpu.make_async_remote_copy(
                src_ref=src, dst_ref=outs[i].at[frm], send_sem=send_sems.at[i, d - 1],
                recv_sem=recv_sems.at[i, d - 1], device_id=dev(frm), device_id_type=pl.DeviceIdType.MESH)

        locals_ = []
        for i in range(n):
            src = ins[i].at[me] if scatter else ins[i]
            lc = pltpu.make_async_copy(src, outs[i].at[me], local_sems.at[i])
            lc.start()
            locals_.append(lc)
        sends = [copy(i, d) for d in range(1, N_DEV) for i in range(n)]
        for cp in sends:
            cp.start()
        for d in range(1, N_DEV):
            for i in range(n):
                arrival(i, d).wait_recv()
        for cp in sends:
            cp.wait_send()
        for lc in locals_:
            lc.wait()

    hbm = pl.BlockSpec(memory_space=pltpu.HBM)
    return pl.pallas_call(
        body, name=name, out_shape=tuple(out_shapes), in_specs=[hbm] * n, out_specs=tuple([hbm] * n),
        scratch_shapes=[pltpu.SemaphoreType.DMA((n, N_DEV - 1)), pltpu.SemaphoreType.DMA((n, N_DEV - 1)),
                        pltpu.SemaphoreType.DMA((n,))],
    )(*arrays)


def _remote(src, dst, send_sem, recv_sem, to):
    return pltpu.make_async_remote_copy(src_ref=src, dst_ref=dst, send_sem=send_sem, recv_sem=recv_sem,
                                        device_id=to, device_id_type=pl.DeviceIdType.MESH)


def _gather_two_level(arrays, *, name):
    n = len(arrays)
    out_shapes = [jax.ShapeDtypeStruct((N_DEV,) + a.shape, a.dtype) for a in arrays]

    def body(*refs):
        ins, outs = refs[:n], refs[n:2 * n]
        send_sems, recv_sems, local_sems = refs[2 * n:]
        x, y, c = lax.axis_index("x"), lax.axis_index("y"), lax.axis_index("c")
        sibling = (x, y, 1 - c)
        chips = [(1 - x, y), (x, 1 - y), (1 - x, 1 - y)]

        def slot(i, px, py, pc):
            return outs[i].at[4 * px + 2 * py + pc]

        def copy(i, k, block, to, src=None):
            dst = slot(i, *block)
            return _remote(dst if src is None else src, dst, send_sems.at[i, k], recv_sems.at[i, k], to)

        locals_ = []
        for i in range(n):
            lc = pltpu.make_async_copy(ins[i], slot(i, x, y, c), local_sems.at[i])
            lc.start()
            locals_.append(lc)
        sends = []
        for j, chip in enumerate(chips):
            for i in range(n):
                sends.append(copy(i, 1 + j, (x, y, c), (*chip, c), src=ins[i]))
        for i in range(n):
            sends.append(copy(i, 0, (x, y, c), sibling, src=ins[i]))
        for cp in sends:
            cp.start()
        for j, chip in enumerate(chips):
            for i in range(n):
                copy(i, 1 + j, (*chip, c), (x, y, c)).wait_recv()
                fwd = copy(i, 4 + j, (*chip, c), sibling)
                fwd.start()
                sends.append(fwd)
        for i in range(n):
            copy(i, 0, (x, y, 1 - c), (x, y, c)).wait_recv()
        for j, chip in enumerate(chips):
            for i in range(n):
                copy(i, 4 + j, (*chip, 1 - c), (x, y, c)).wait_recv()
        for cp in sends:
            cp.wait_send()
        for lc in locals_:
            lc.wait()

    hbm = pl.BlockSpec(memory_space=pltpu.HBM)
    return pl.pallas_call(
        body, name=name, out_shape=tuple(out_shapes), in_specs=[hbm] * n, out_specs=tuple([hbm] * n),
        scratch_shapes=[pltpu.SemaphoreType.DMA((n, 7)), pltpu.SemaphoreType.DMA((n, 7)),
                        pltpu.SemaphoreType.DMA((n,))],
    )(*arrays)


def _sibling_swap(arrays, *, name):
    n = len(arrays)
    out_shapes = [jax.ShapeDtypeStruct(a.shape[1:], a.dtype) for a in arrays]

    def body(*refs):
        ins, outs = refs[:n], refs[n:2 * n]
        send_sems, recv_sems = refs[2 * n:]
        x, y, c = lax.axis_index("x"), lax.axis_index("y"), lax.axis_index("c")
        copies = [_remote(ins[i].at[1 - c], outs[i], send_sems.at[i], recv_sems.at[i], (x, y, 1 - c))
                  for i in range(n)]
        for cp in copies:
            cp.start()
        for cp in copies:
            cp.wait_recv()
        for cp in copies:
            cp.wait_send()

    hbm = pl.BlockSpec(memory_space=pltpu.HBM)
    return pl.pallas_call(
        body, name=name, out_shape=tuple(out_shapes), in_specs=[hbm] * n, out_specs=tuple([hbm] * n),
        scratch_shapes=[pltpu.SemaphoreType.DMA((n,)), pltpu.SemaphoreType.DMA((n,))],
    )(*arrays)


def _chips_all_to_all(arrays, *, name):
    n = len(arrays)
    out_shapes = [jax.ShapeDtypeStruct(a.shape, a.dtype) for a in arrays]

    def body(*refs):
        ins, outs = refs[:n], refs[n:2 * n]
        send_sems, recv_sems, local_sems = refs[2 * n:]
        x, y, c = lax.axis_index("x"), lax.axis_index("y"), lax.axis_index("c")
        mine = 2 * x + y
        chips = [(1 - x, y), (x, 1 - y), (1 - x, 1 - y)]
        locals_ = []
        for i in range(n):
            lc = pltpu.make_async_copy(ins[i].at[mine], outs[i].at[mine], local_sems.at[i])
            lc.start()
            locals_.append(lc)
        sends = [_remote(ins[i].at[2 * chip[0] + chip[1]], outs[i].at[mine], send_sems.at[i, j], recv_sems.at[i, j],
                         (*chip, c)) for j, chip in enumerate(chips) for i in range(n)]
        for cp in sends:
            cp.start()
        for j, chip in enumerate(chips):
            for i in range(n):
                q = 2 * chip[0] + chip[1]
                _remote(ins[i].at[q], outs[i].at[q], send_sems.at[i, j], recv_sems.at[i, j], (*chip, c)).wait_recv()
        for cp in sends:
            cp.wait_send()
        for lc in locals_:
            lc.wait()

    hbm = pl.BlockSpec(memory_space=pltpu.HBM)
    return pl.pallas_call(
        body, name=name, out_shape=tuple(out_shapes), in_specs=[hbm] * n, out_specs=tuple([hbm] * n),
        scratch_shapes=[pltpu.SemaphoreType.DMA((n, 3)), pltpu.SemaphoreType.DMA((n, 3)),
                        pltpu.SemaphoreType.DMA((n,))],
    )(*arrays)


def _pair_add(mine, got, *, name):
    _, nq, R, C = mine.shape
    tr = R
    for cand in (512, 256, 128, 64, 32, 16):
        if R % cand == 0 and cand * C * 2 * 3 * 2 <= 12 * 1024 * 1024:
            tr = cand
            break

    def body(core_ref, m_ref, g_ref, o_ref):
        o_ref[0] = (m_ref[0, 0].astype(F32) + g_ref[0].astype(F32)).astype(BF16)

    core = lax.axis_index("c").astype(jnp.int32).reshape(1)
    return pl.pallas_call(
        body, name=name,
        grid_spec=pltpu.PrefetchScalarGridSpec(
            num_scalar_prefetch=1, grid=(nq, R // tr),
            in_specs=[pl.BlockSpec((1, 1, tr, C), lambda q, i, core_ref: (core_ref[0], q, i, 0)),
                      pl.BlockSpec((1, tr, C), lambda q, i, core_ref: (q, i, 0))],
            out_specs=pl.BlockSpec((1, tr, C), lambda q, i, core_ref: (q, i, 0))),
        out_shape=jax.ShapeDtypeStruct((nq, R, C), BF16),
        compiler_params=_cp(("parallel", "parallel"), 40),
    )(core, mine, got)


def _mm(a, b, *, ta=False, tb=False, tm, tn, tk, out_dtype=F32, name):
    M = a.shape[1] if ta else a.shape[0]
    K = a.shape[0] if ta else a.shape[1]
    N = b.shape[0] if tb else b.shape[1]
    tm, tn, tk = min(tm, M), min(tn, N), min(tk, K)
    assert M % tm == 0 and N % tn == 0 and K % tk == 0, (name, M, N, K, tm, tn, tk)
    nk = K // tk
    dims = (((0 if ta else 1,), (1 if tb else 0,)), ((), ()))

    def body(a_ref, b_ref, o_ref, acc_ref):
        k = pl.program_id(2)

        @pl.when(k == 0)
        def _():
            acc_ref[...] = jnp.zeros_like(acc_ref)

        acc_ref[...] += lax.dot_general(a_ref[...].astype(BF16), b_ref[...].astype(BF16), dims,
                                        preferred_element_type=F32)

        @pl.when(k == nk - 1)
        def _():
            o_ref[...] = acc_ref[...].astype(out_dtype)

    a_spec = pl.BlockSpec((tk, tm), lambda i, j, k: (k, i)) if ta else pl.BlockSpec((tm, tk), lambda i, j, k: (i, k))
    b_spec = pl.BlockSpec((tn, tk), lambda i, j, k: (j, k)) if tb else pl.BlockSpec((tk, tn), lambda i, j, k: (k, j))
    blk = 2 * (tm * tk * a.dtype.itemsize + tk * tn * b.dtype.itemsize + tm * tn * jnp.dtype(out_dtype).itemsize)
    return pl.pallas_call(
        body, name=name, grid=(M // tm, N // tn, nk), in_specs=[a_spec, b_spec],
        out_specs=pl.BlockSpec((tm, tn), lambda i, j, k: (i, j)),
        out_shape=jax.ShapeDtypeStruct((M, N), out_dtype),
        scratch_shapes=[pltpu.VMEM((tm, tn), F32)],
        compiler_params=_cp(("parallel", "parallel", "arbitrary"), (blk + tm * tn * 4) // (1024 * 1024) + 12),
    )(a, b)


def _ffn_up(h, g8, u8, *, tm, tk, name):
    S, D = h.shape
    nb, _, tn = g8.shape
    tm = min(tm, S)
    tk = min(tk, D)
    nk = D // tk

    def body(h_ref, g_ref, u_ref, au_ref, s_ref, acc_ref):
        k = pl.program_id(2)

        @pl.when(k == 0)
        def _():
            acc_ref[...] = jnp.zeros_like(acc_ref)

        hv = h_ref[...]
        acc_ref[:, :tn] += jnp.dot(hv, g_ref[0], preferred_element_type=F32)
        acc_ref[:, tn:] += jnp.dot(hv, u_ref[0], preferred_element_type=F32)

        @pl.when(k == nk - 1)
        def _():
            acc = acc_ref[...]
            a = acc[:, :tn]
            u = acc[:, tn:]
            au_ref[...] = acc.astype(BF16)
            s_ref[...] = (a * jax.nn.sigmoid(a) * u).astype(BF16)

    wspec = pl.BlockSpec((1, tk, tn), lambda i, j, k: (j, k, 0))
    return pl.pallas_call(
        body, name=name, grid=(S // tm, nb, nk),
        in_specs=[pl.BlockSpec((tm, tk), lambda i, j, k: (i, k)), wspec, wspec],
        out_specs=(pl.BlockSpec((tm, 2 * tn), lambda i, j, k: (i, j)), pl.BlockSpec((tm, tn), lambda i, j, k: (i, j))),
        out_shape=(jax.ShapeDtypeStruct((S, 2 * nb * tn), BF16), jax.ShapeDtypeStruct((S, nb * tn), BF16)),
        scratch_shapes=[pltpu.VMEM((tm, 2 * tn), F32)],
        compiler_params=_cp(("parallel", "parallel", "arbitrary"), 52),
    )(h, g8, u8)


def _ffn_dh(dau, g8, u8, *, tm, tn, name):
    S = dau.shape[0]
    nb, D, tf = g8.shape
    tm, tn = min(tm, S), min(tn, D)
    nk = 2 * nb
    nt = (((1,), (1,)), ((), ()))

    def body(a_ref, g_ref, u_ref, o_ref, acc_ref):
        k = pl.program_id(2)

        @pl.when(k == 0)
        def _():
            acc_ref[...] = jnp.zeros_like(acc_ref)

        @pl.when(k % 2 == 0)
        def _():
            acc_ref[...] += lax.dot_general(a_ref[...], g_ref[0], nt, preferred_element_type=F32)

        @pl.when(k % 2 == 1)
        def _():
            acc_ref[...] += lax.dot_general(a_ref[...], u_ref[0], nt, preferred_element_type=F32)

        @pl.when(k == nk - 1)
        def _():
            o_ref[...] = acc_ref[...]

    wspec = pl.BlockSpec((1, tn, tf), lambda i, n, k: (k // 2, n, 0))
    return pl.pallas_call(
        body, name=name, grid=(S // tm, D // tn, nk),
        in_specs=[pl.BlockSpec((tm, tf), lambda i, n, k: (i, k)), wspec, wspec],
        out_specs=pl.BlockSpec((tm, tn), lambda i, n, k: (i, n)),
        out_shape=jax.ShapeDtypeStruct((S, D), F32),
        scratch_shapes=[pltpu.VMEM((tm, tn), F32)],
        compiler_params=_cp(("parallel", "parallel", "arbitrary"), 40),
    )(dau, g8, u8)


def _ffn_dwgu(h, dau, *, tm, tk, name):
    S, D = h.shape
    tf = FF_TILE
    nt = dau.shape[1] // tf
    tm, tk = min(tm, D), min(tk, S)
    nk = S // tk
    ni = D // tm

    def body(a_ref, b_ref, o_ref, acc_ref):
        k = pl.program_id(2)

        @pl.when(k == 0)
        def _():
            acc_ref[...] = jnp.zeros_like(acc_ref)

        acc_ref[...] += lax.dot_general(a_ref[...], b_ref[...], (((0,), (0,)), ((), ())), preferred_element_type=F32)

        @pl.when(k == nk - 1)
        def _():
            o_ref[0, 0] = acc_ref[...].astype(BF16)

    return pl.pallas_call(
        body, name=name, grid=(ni, nt, nk),
        in_specs=[pl.BlockSpec((tk, tm), lambda i, j, k: (k, i)), pl.BlockSpec((tk, tf), lambda i, j, k: (k, j))],
        out_specs=pl.BlockSpec((1, 1, tm, tf), lambda i, j, k: ((j // 2) % 2, j // 4, (j % 2) * ni + i, 0)),
        out_shape=jax.ShapeDtypeStruct((2, nt // 4, 2 * D, tf), BF16),
        scratch_shapes=[pltpu.VMEM((tm, tf), F32)],
        compiler_params=_cp(("parallel", "parallel", "arbitrary"), 40),
    )(h, dau)


def _ffn_dwd(s, df, *, tn, tk, name):
    S, D = df.shape
    tf = FF_TILE
    nb = s.shape[1] // tf
    tn, tk = min(tn, D), min(tk, S)
    nk = S // tk

    def body(a_ref, b_ref, o_ref, acc_ref):
        k = pl.program_id(2)

        @pl.when(k == 0)
        def _():
            acc_ref[...] = jnp.zeros_like(acc_ref)

        acc_ref[...] += lax.dot_general(a_ref[...], b_ref[...], (((0,), (0,)), ((), ())), preferred_element_type=F32)

        @pl.when(k == nk - 1)
        def _():
            o_ref[0, 0] = acc_ref[...].astype(BF16)

    return pl.pallas_call(
        body, name=name, grid=(nb, D // tn, nk),
        in_specs=[pl.BlockSpec((tk, tf), lambda j, n, k: (k, j)), pl.BlockSpec((tk, tn), lambda j, n, k: (k, n))],
        out_specs=pl.BlockSpec((1, 1, tf, tn), lambda j, n, k: (j % 2, j // 2, 0, n)),
        out_shape=jax.ShapeDtypeStruct((2, nb // 2, tf, D), BF16),
        scratch_shapes=[pltpu.VMEM((tf, tn), F32)],
        compiler_params=_cp(("parallel", "parallel", "arbitrary"), 40),
    )(s, df)


def _ffn_down_bwd(df, d8, au, *, tm, tk, name):
    S, D = df.shape
    nb, tn, _ = d8.shape
    F = nb * tn
    tm = min(tm, S)
    tk = min(tk, D)
    nk = D // tk

    def body(df_ref, w_ref, au_ref, dau_ref, acc_ref):
        k = pl.program_id(2)

        @pl.when(k == 0)
        def _():
            acc_ref[...] = jnp.zeros_like(acc_ref)

        acc_ref[...] += lax.dot_general(df_ref[...], w_ref[0], (((1,), (1,)), ((), ())), preferred_element_type=F32)

        @pl.when(k == nk - 1)
        def _():
            ds = acc_ref[...]
            au_v = au_ref[...].astype(F32)
            a = au_v[:, :tn]
            u = au_v[:, tn:]
            sg = jax.nn.sigmoid(a)
            da = ds * u * (sg * (1.0 + a * (1.0 - sg)))
            du = ds * (a * sg)
            dau_ref[:, :tn] = da.astype(BF16)
            dau_ref[:, tn:] = du.astype(BF16)

    return pl.pallas_call(
        body, name=name, grid=(S // tm, F // tn, nk),
        in_specs=[pl.BlockSpec((tm, tk), lambda i, j, k: (i, k)), pl.BlockSpec((1, tn, tk), lambda i, j, k: (j, 0, k)),
                  pl.BlockSpec((tm, 2 * tn), lambda i, j, k: (i, j))],
        out_specs=pl.BlockSpec((tm, 2 * tn), lambda i, j, k: (i, j)),
        out_shape=jax.ShapeDtypeStruct((S, 2 * F), BF16),
        scratch_shapes=[pltpu.VMEM((tm, tn), F32)],
        compiler_params=_cp(("parallel", "parallel", "arbitrary"), 52),
    )(df, d8, au)


def _fold8(x):
    tm, w = x.shape
    return jnp.sum(x.reshape(tm // 8, 8, w), axis=0)


def _row_spec(tm, w):
    return pl.BlockSpec((tm, w), lambda i: (i, 0))


def _vec_spec(rows, w):
    return pl.BlockSpec((rows, w), lambda i: (0, 0))


def _pre_norm_mod(x, gain, shift, scale1p, *, name):
    S, D = x.shape
    tm = ROW_TILE

    def body(x_ref, g_ref, sh_ref, sc_ref, h_ref):
        xv = x_ref[...]
        rinv = lax.rsqrt(jnp.mean(xv * xv, axis=-1, keepdims=True) + NORM_EPS)
        h_ref[...] = ((xv * rinv) * g_ref[...] * sc_ref[...] + sh_ref[...]).astype(BF16)

    return pl.pallas_call(
        body, name=name, grid=(S // tm,),
        in_specs=[_row_spec(tm, D), _vec_spec(1, D), _vec_spec(1, D), _vec_spec(1, D)],
        out_specs=_row_spec(tm, D), out_shape=jax.ShapeDtypeStruct((S, D), BF16),
        compiler_params=_cp(("parallel",), 32),
    )(x, gain, shift, scale1p)


def _pre_norm_mod_bwd(dh, x, dres, gain, scale1p, *, name):
    S, D = x.shape
    tm = ROW_TILE
    n = S // tm

    def body(dh_ref, x_ref, dr_ref, g_ref, sc_ref, dx_ref, sums_ref, acc_ref):
        i = pl.program_id(0)

        @pl.when(i == 0)
        def _():
            acc_ref[...] = jnp.zeros_like(acc_ref)

        xv = x_ref[...]
        dhv = dh_ref[...]
        g = g_ref[...]
        rinv = lax.rsqrt(jnp.mean(xv * xv, axis=-1, keepdims=True) + NORM_EPS)
        xn = xv * rinv
        dn = dhv * sc_ref[...]
        dxn = dn * g
        dx_ref[...] = dr_ref[...] + rinv * (dxn - xn * jnp.mean(dxn * xn, axis=-1, keepdims=True))
        acc_ref[0] += _fold8(dhv)
        acc_ref[1] += _fold8(dhv * (xn * g))
        acc_ref[2] += _fold8(dn * xn)

        @pl.when(i == n - 1)
        def _():
            for q in range(3):
                sums_ref[q:q + 1, :] = jnp.sum(acc_ref[q], axis=0, keepdims=True)

    return pl.pallas_call(
        body, name=name, grid=(n,),
        in_specs=[_row_spec(tm, D), _row_spec(tm, D), _row_spec(tm, D), _vec_spec(1, D), _vec_spec(1, D)],
        out_specs=(_row_spec(tm, D), _vec_spec(3, D)),
        out_shape=(jax.ShapeDtypeStruct((S, D), F32), jax.ShapeDtypeStruct((3, D), F32)),
        scratch_shapes=[pltpu.VMEM((3, 8, D), F32)],
        compiler_params=_cp(("arbitrary",), 40),
    )(dh, x, dres, gain, scale1p)


def _post_norm_res(x, f, gain, gw, *, name):
    S, D = x.shape
    tm = ROW_TILE

    def body(x_ref, f_ref, g_ref, gw_ref, o_ref):
        fv = f_ref[...]
        rinv = lax.rsqrt(jnp.mean(fv * fv, axis=-1, keepdims=True) + NORM_EPS)
        o_ref[...] = x_ref[...] + gw_ref[...] * ((fv * rinv) * g_ref[...])

    return pl.pallas_call(
        body, name=name, grid=(S // tm,),
        in_specs=[_row_spec(tm, D), _row_spec(tm, D), _vec_spec(1, D), _vec_spec(1, D)],
        out_specs=_row_spec(tm, D), out_shape=jax.ShapeDtypeStruct((S, D), F32),
        compiler_params=_cp(("parallel",), 32),
    )(x, f, gain, gw)


def _post_norm_res_bwd(dxo, f, gain, gw, weight, *, name):
    S, D = f.shape
    tm = ROW_TILE
    n = S // tm

    def body(d_ref, f_ref, g_ref, gw_ref, df_ref, sums_ref, acc_ref):
        i = pl.program_id(0)

        @pl.when(i == 0)
        def _():
            acc_ref[...] = jnp.zeros_like(acc_ref)

        fv = f_ref[...]
        dv = d_ref[...]
        g = g_ref[...]
        rinv = lax.rsqrt(jnp.mean(fv * fv, axis=-1, keepdims=True) + NORM_EPS)
        fh = fv * rinv
        dy = dv * gw_ref[...]
        dfh = dy * g
        df_ref[...] = (rinv * (dfh - fh * jnp.mean(dfh * fh, axis=-1, keepdims=True))).astype(BF16)
        acc_ref[0] += _fold8(weight * dv * (fh * g))
        acc_ref[1] += _fold8(dy * fh)

        @pl.when(i == n - 1)
        def _():
            for q in range(2):
                sums_ref[q:q + 1, :] = jnp.sum(acc_ref[q], axis=0, keepdims=True)

    return pl.pallas_call(
        body, name=name, grid=(n,),
        in_specs=[_row_spec(tm, D), _row_spec(tm, D), _vec_spec(1, D), _vec_spec(1, D)],
        out_specs=(_row_spec(tm, D), _vec_spec(2, D)),
        out_shape=(jax.ShapeDtypeStruct((S, D), BF16), jax.ShapeDtypeStruct((2, D), F32)),
        scratch_shapes=[pltpu.VMEM((2, 8, D), F32)],
        compiler_params=_cp(("arbitrary",), 40),
    )(dxo, f, gain, gw)


def _loss_head(y, target, *, name):
    S, D = y.shape
    tm = ROW_TILE

    def body(y_ref, t_ref, l_ref, dy_ref):
        i = pl.program_id(0)

        @pl.when(i == 0)
        def _():
            l_ref[...] = jnp.zeros_like(l_ref)

        err = y_ref[...] - t_ref[...]
        dy_ref[...] = err * (1.0 / D)
        row = jnp.sum(err * err, axis=-1, keepdims=True) * (0.5 / D)
        l_ref[...] += jnp.sum(row, axis=0, keepdims=True)

    return pl.pallas_call(
        body, name=name, grid=(S // tm,),
        in_specs=[_row_spec(tm, D), _row_spec(tm, D)],
        out_specs=(_vec_spec(1, 1), _row_spec(tm, D)),
        out_shape=(jax.ShapeDtypeStruct((1, 1), F32), jax.ShapeDtypeStruct((S, D), F32)),
        compiler_params=_cp(("arbitrary",), 32),
    )(y, target)


def _shift_down(z, j, row):
    return jnp.where(row >= j, pltpu.roll(z, j, 0), 0.0)


def _shift_up(z, j, row, n):
    return jnp.where(row < n - j, pltpu.roll(z, n - j, 0), 0.0)


def _pool_fwd(p, pool_w, pool_scale, *, name):
    S = p.shape[0]
    C = POOL_GROUP

    def body(u_ref, w_ref, sc_ref, o_ref, y_ref):
        g = pl.program_id(0)
        u = u_ref[...]
        row = lax.broadcasted_iota(jnp.int32, (S, C), 0)
        s1 = u + _shift_down(u, 1, row)
        s2 = s1 + _shift_down(s1, 2, row)
        s3 = s2 + _shift_down(s2, 4, row)
        s4 = s3 + _shift_down(s3, 8, row)
        gi = jnp.zeros((S, C), jnp.int32) + g
        win = jnp.where(gi == 0, s1, jnp.where(gi == 1, s2, jnp.where(gi == 2, s3, s4)))
        width = jnp.where(gi == 0, 2, jnp.where(gi == 1, 4, jnp.where(gi == 2, 8, 16)))
        count = jnp.minimum(row + 1, width).astype(F32)
        o = win / count - u
        o_ref[...] = o
        y_ref[...] = jnp.dot(o.astype(BF16), w_ref[0].astype(BF16), preferred_element_type=F32) * sc_ref[...]

    col = pl.BlockSpec((S, C), lambda g: (0, g))
    return pl.pallas_call(
        body, name=name, grid=(POOL_GROUPS,),
        in_specs=[col, pl.BlockSpec((1, C, C), lambda g: (g, 0, 0)), pl.BlockSpec((1, C), lambda g: (0, g))],
        out_specs=(col, col),
        out_shape=(jax.ShapeDtypeStruct((S, POOL_GROUPS * C), F32), jax.ShapeDtypeStruct((S, POOL_GROUPS * C), F32)),
        compiler_params=_cp(("parallel",), 48),
    )(p, pool_w, pool_scale)


def _pool_bwd(dcat, o, pool_w, pool_scale, *, name):
    S = o.shape[0]
    C = POOL_GROUP

    def body(dy_ref, o_ref, w_ref, sc_ref, du_ref, dw_ref, dsc_ref):
        g = pl.program_id(0)
        dy = dy_ref[...]
        ob = o_ref[...].astype(BF16)
        wb = w_ref[0].astype(BF16)
        mixed = jnp.dot(ob, wb, preferred_element_type=F32)
        dsc_ref[...] = jnp.sum(_fold8(dy * mixed), axis=0, keepdims=True)
        dmix = (dy * sc_ref[...]).astype(BF16)
        dw_ref[0] = lax.dot_general(ob, dmix, (((0,), (0,)), ((), ())), preferred_element_type=F32)
        do = lax.dot_general(dmix, wb, (((1,), (1,)), ((), ())), preferred_element_type=F32)
        row = lax.broadcasted_iota(jnp.int32, (S, C), 0)
        gi = jnp.zeros((S, C), jnp.int32) + g
        width = jnp.where(gi == 0, 2, jnp.where(gi == 1, 4, jnp.where(gi == 2, 8, 16)))
        z = do / jnp.minimum(row + 1, width).astype(F32)
        s1 = z + _shift_up(z, 1, row, S)
        s2 = s1 + _shift_up(s1, 2, row, S)
        s3 = s2 + _shift_up(s2, 4, row, S)
        s4 = s3 + _shift_up(s3, 8, row, S)
        win = jnp.where(gi == 0, s1, jnp.where(gi == 1, s2, jnp.where(gi == 2, s3, s4)))
        du_ref[...] = (win - do).astype(BF16)

    col = pl.BlockSpec((S, C), lambda g: (0, g))
    return pl.pallas_call(
        body, name=name, grid=(POOL_GROUPS,),
        in_specs=[col, col, pl.BlockSpec((1, C, C), lambda g: (g, 0, 0)), pl.BlockSpec((1, C), lambda g: (0, g))],
        out_specs=(col, pl.BlockSpec((1, C, C), lambda g: (g, 0, 0)), pl.BlockSpec((1, C), lambda g: (0, g))),
        out_shape=(jax.ShapeDtypeStruct((S, POOL_GROUPS * C), BF16), jax.ShapeDtypeStruct((POOL_GROUPS, C, C), F32),
                   jax.ShapeDtypeStruct((1, POOL_GROUPS * C), F32)),
        compiler_params=_cp(("parallel",), 48),
    )(dcat, o, pool_w, pool_scale)


def _block_ones():
    r = lax.broadcasted_iota(jnp.int32, (128, 128), 0) // HEAD
    c = lax.broadcasted_iota(jnp.int32, (128, 128), 1) // HEAD
    return jnp.where(r == c, 1.0, 0.0).astype(BF16)


def _segsum(x, bd):
    outs = []
    for j in range(x.shape[1] // 128):
        xs = x[:, j * 128:(j + 1) * 128]
        hi = xs.astype(BF16)
        lo = (xs - hi.astype(F32)).astype(BF16)
        outs.append(jnp.dot(hi, bd, preferred_element_type=F32) + jnp.dot(lo, bd, preferred_element_type=F32))
    return jnp.concatenate(outs, axis=1)


def _prep_common(q, qprev, first, mu, wl, w0, a0, kkw, kaw, R):
    tm = q.shape[0]
    row = lax.broadcasted_iota(jnp.int32, q.shape, 0)
    last = qprev[7:8, :] * first
    prev = jnp.where(row == 0, last, pltpu.roll(q, 1, 0))
    ps = q + mu * (prev - q)
    r = ps[:, 0:R]
    k = ps[:, R:2 * R]
    v = ps[:, 2 * R:3 * R]
    lo_in = ps[:, 3 * R:3 * R + LORA_PAD]
    lane = lax.broadcasted_iota(jnp.int32, (tm, LORA_PAD), 1)
    m_w = lane < LORA_W
    m_a = lane < LORA_W + LORA_A
    m_g = lane < LORA_W + LORA_A + LORA_G
    act = jnp.where(m_w, jnp.tanh(lo_in), jnp.where(m_a, lo_in, jnp.where(m_g, jax.nn.sigmoid(lo_in), 0.0)))
    lo = jnp.dot(act.astype(BF16), wl, preferred_element_type=F32)
    wpre = w0 + lo[:, 0:R]
    apre = a0 + lo[:, R:2 * R]
    g = lo[:, 2 * R:3 * R]
    neg = -wpre
    softplus = jnp.maximum(neg, 0.0) + jnp.log(1.0 + jnp.exp(-jnp.abs(neg)))
    wlog = -softplus - 0.5
    ew = jnp.exp(wlog)
    decay = jnp.exp(-ew)
    a = jax.nn.sigmoid(apre)
    kk = k * kkw
    bd = _block_ones()
    n2 = _segsum(kk * kk, bd)
    nrm = jnp.maximum(jnp.sqrt(n2), 1e-12)
    kap = kk / nrm
    kmul = 1.0 + (a - 1.0) * kaw
    k2 = k * kmul
    return dict(prev=prev, r=r, k=k, v=v, act=act, m_w=m_w, m_a=m_a, m_g=m_g, wpre=wpre, g=g, ew=ew, decay=decay,
                a=a, n2=n2, nrm=nrm, kap=kap, kmul=kmul, k2=k2, bd=bd)


def _prev_rows_spec(tm, w):
    return pl.BlockSpec((8, w), lambda i: (jnp.maximum(i * (tm // 8) - 1, 0), 0))


def _rwkv_prep(q, mu, wl, w0, a0, kkw, kaw, *, name):
    S, QW = q.shape
    R = w0.shape[1]
    tm = ROW_TILE // 2

    def body(q_ref, qp_ref, mu_ref, wl_ref, w0_ref, a0_ref, kk_ref, ka_ref, r_ref, w_ref, k_ref, v_ref, kap_ref,
             b_ref, g_ref):
        first = jnp.where(pl.program_id(0) > 0, 1.0, 0.0)
        t = _prep_common(q_ref[...], qp_ref[...], first, mu_ref[...], wl_ref[...], w0_ref[...], a0_ref[...],
                         kk_ref[...], ka_ref[...], R)
        r_ref[...] = t["r"]
        w_ref[...] = t["decay"]
        k_ref[...] = t["k2"]
        v_ref[...] = t["v"]
        kap_ref[...] = t["kap"]
        b_ref[...] = t["kap"] * t["a"]
        g_ref[...] = t["g"]

    vec = _vec_spec(1, R)
    return pl.pallas_call(
        body, name=name, grid=(S // tm,),
        in_specs=[_row_spec(tm, QW), _prev_rows_spec(tm, QW), _vec_spec(1, QW), _vec_spec(LORA_PAD, 3 * R), vec, vec,
                  vec, vec],
        out_specs=tuple([_row_spec(tm, R)] * 7),
        out_shape=tuple([jax.ShapeDtypeStruct((S, R), F32)] * 7),
        compiler_params=_cp(("parallel",), 48),
    )(q, q, mu, wl, w0, a0, kkw, kaw)


def _rwkv_prep_bwd(q, mu, wl, w0, a0, kkw, kaw, grads, *, name):
    S, QW = q.shape
    R = w0.shape[1]
    tm = ROW_TILE // 2
    n = S // tm

    def body(q_ref, qp_ref, mu_ref, wl_ref, w0_ref, a0_ref, kk_ref, ka_ref, dr_ref, dw_ref, dk2_ref, dv_ref, dkap_ref,
             db_ref, dg_ref, drb_ref, dk2b_ref, dvb_ref, dps_ref, dwl_ref, sums_ref, acc_ref):
        i = pl.program_id(0)

        @pl.when(i == 0)
        def _():
            acc_ref[...] = jnp.zeros_like(acc_ref)
            dwl_ref[...] = jnp.zeros_like(dwl_ref)

        first = jnp.where(i > 0, 1.0, 0.0)
        wl = wl_ref[...]
        kkw = kk_ref[...]
        kaw = ka_ref[...]
        t = _prep_common(q_ref[...], qp_ref[...], first, mu_ref[...], wl, w0_ref[...], a0_ref[...], kkw, kaw, R)
        a, kap, k, act = t["a"], t["kap"], t["k"], t["act"]
        db = db_ref[...]
        dk2 = dk2_ref[...] + dk2b_ref[...]
        dkap = dkap_ref[...] + db * a
        da = db * kap + dk2 * k * kaw
        dk = dk2 * t["kmul"]
        proj = jnp.where(jnp.sqrt(t["n2"]) > 1e-12, _segsum(kap * dkap, t["bd"]), 0.0)
        dkk = (dkap - kap * proj) / t["nrm"]
        dk = dk + dkk * kkw
        dapre = da * a * (1.0 - a)
        dwlog = dw_ref[...] * t["decay"] * (-t["ew"])
        dwpre = dwlog * jax.nn.sigmoid(-t["wpre"])
        acc_ref[0] += _fold8(dwpre)
        acc_ref[1] += _fold8(dapre)
        acc_ref[2] += _fold8(dkk * k)
        acc_ref[3] += _fold8(dk2 * k * (a - 1.0))
        dlo = jnp.concatenate([dwpre, dapre, dg_ref[...]], axis=1).astype(BF16)
        dwl_ref[...] += lax.dot_general(act.astype(BF16), dlo, (((0,), (0,)), ((), ())), preferred_element_type=F32)
        dact = lax.dot_general(dlo, wl, (((1,), (1,)), ((), ())), preferred_element_type=F32)
        dlin = jnp.where(t["m_w"], dact * (1.0 - act * act),
                         jnp.where(t["m_a"], dact, jnp.where(t["m_g"], dact * act * (1.0 - act), 0.0)))
        dps_ref[:, 0:R] = dr_ref[...] + drb_ref[...]
        dps_ref[:, R:2 * R] = dk
        dps_ref[:, 2 * R:3 * R] = dv_ref[...] + dvb_ref[...]
        dps_ref[:, 3 * R:3 * R + LORA_PAD] = dlin
        dps_ref[:, 3 * R + LORA_PAD:] = jnp.zeros((tm, QW - 3 * R - LORA_PAD), F32)

        @pl.when(i == n - 1)
        def _():
            for j in range(4):
                sums_ref[j:j + 1, :] = jnp.sum(acc_ref[j], axis=0, keepdims=True)

    vec = _vec_spec(1, R)
    return pl.pallas_call(
        body, name=name, grid=(n,),
        in_specs=[_row_spec(tm, QW), _prev_rows_spec(tm, QW), _vec_spec(1, QW), _vec_spec(LORA_PAD, 3 * R), vec, vec,
                  vec, vec] + [_row_spec(tm, R)] * 10,
        out_specs=(_row_spec(tm, QW), _vec_spec(LORA_PAD, 3 * R), _vec_spec(4, R)),
        out_shape=(jax.ShapeDtypeStruct((S, QW), F32), jax.ShapeDtypeStruct((LORA_PAD, 3 * R), F32),
                   jax.ShapeDtypeStruct((4, R), F32)),
        scratch_shapes=[pltpu.VMEM((4, 8, R), F32)],
        compiler_params=_cp(("arbitrary",), 56),
    )(q, q, mu, wl, w0, a0, kkw, kaw, *grads)


def _tshift_bwd(dps, q, mu, *, name):
    S, QW = q.shape
    tm = ROW_TILE // 2
    n = S // tm

    def body(d_ref, dn_ref, q_ref, qp_ref, mu_ref, dq_ref, dmu_ref, acc_ref):
        i = pl.program_id(0)

        @pl.when(i == 0)
        def _():
            acc_ref[...] = jnp.zeros_like(acc_ref)

        mu = mu_ref[...]
        d = d_ref[...]
        qv = q_ref[...]
        row = lax.broadcasted_iota(jnp.int32, d.shape, 0)
        first = jnp.where(i > 0, 1.0, 0.0)
        notlast = jnp.where(i < n - 1, 1.0, 0.0)
        prev = jnp.where(row == 0, qp_ref[7:8, :] * first, pltpu.roll(qv, 1, 0))
        z = d * mu
        nxt = jnp.where(row == tm - 1, dn_ref[0:1, :] * mu * notlast, pltpu.roll(z, tm - 1, 0))
        dq_ref[...] = (d * (1.0 - mu) + nxt).astype(BF16)
        acc_ref[...] += _fold8(d * (prev - qv))

        @pl.when(i == n - 1)
        def _():
            dmu_ref[...] = jnp.sum(acc_ref[...], axis=0, keepdims=True)

    nblk8 = S // 8
    next_spec = pl.BlockSpec((8, QW), lambda i: (jnp.minimum((i + 1) * (tm // 8), nblk8 - 1), 0))
    return pl.pallas_call(
        body, name=name, grid=(n,),
        in_specs=[_row_spec(tm, QW), next_spec, _row_spec(tm, QW), _prev_rows_spec(tm, QW), _vec_spec(1, QW)],
        out_specs=(_row_spec(tm, QW), _vec_spec(1, QW)),
        out_shape=(jax.ShapeDtypeStruct((S, QW), BF16), jax.ShapeDtypeStruct((1, QW), F32)),
        scratch_shapes=[pltpu.VMEM((8, QW), F32)],
        compiler_params=_cp(("arbitrary",), 48),
    )(dps, dps, q, q, mu)


def _post_common(ysc, r, k2, v, lnw, lnb, rk):
    bd = _block_ones()
    mean = _segsum(ysc, bd) * (1.0 / HEAD)
    d = ysc - mean
    var = _segsum(d * d, bd) * (1.0 / HEAD)
    rstd = lax.rsqrt(var + LN_X_EPS)
    yh = d * rstd
    rkk = _segsum(r * k2 * rk, bd)
    z = yh * lnw + lnb + rkk * v
    return bd, rstd, yh, rkk, z


def _rwkv_post(ysc, r, k2, v, g, ypool, lnw, lnb, rk, *, name):
    S, R = ysc.shape
    PW = ypool.shape[1]
    tm = ROW_TILE

    def body(y_ref, r_ref, k_ref, v_ref, g_ref, yp_ref, lw_ref, lb_ref, rk_ref, cat_ref):
        _, _, _, _, z = _post_common(y_ref[...], r_ref[...], k_ref[...], v_ref[...], lw_ref[...], lb_ref[...],
                                     rk_ref[...])
        cat_ref[:, 0:PW] = yp_ref[...].astype(BF16)
        cat_ref[:, PW:] = (z * g_ref[...]).astype(BF16)

    vec = _vec_spec(1, R)
    return pl.pallas_call(
        body, name=name, grid=(S // tm,),
        in_specs=[_row_spec(tm, R)] * 5 + [_row_spec(tm, PW), vec, vec, vec],
        out_specs=_row_spec(tm, PW + R), out_shape=jax.ShapeDtypeStruct((S, PW + R), BF16),
        compiler_params=_cp(("parallel",), 48),
    )(ysc, r, k2, v, g, ypool, lnw, lnb, rk)


def _rwkv_post_bwd(dcat, ysc, r, k2, v, g, lnw, lnb, rk, *, name):
    S, R = ysc.shape
    tm = ROW_TILE
    n = S // tm

    def body(d_ref, y_ref, r_ref, k_ref, v_ref, g_ref, lw_ref, lb_ref, rk_ref, dy_ref, dg_ref, drb_ref, dkb_ref,
             dvb_ref, sums_ref, acc_ref):
        i = pl.program_id(0)

        @pl.when(i == 0)
        def _():
            acc_ref[...] = jnp.zeros_like(acc_ref)

        rv, kv, vv, lw, rkw = r_ref[...], k_ref[...], v_ref[...], lw_ref[...], rk_ref[...]
        bd, rstd, yh, rkk, z = _post_common(y_ref[...], rv, kv, vv, lw, lb_ref[...], rkw)
        dyr = d_ref[...]
        dg_ref[...] = dyr * z
        dz = dyr * g_ref[...]
        dyh = dz * lw
        dy_ref[...] = rstd * (dyh - _segsum(dyh, bd) * (1.0 / HEAD) - yh * (_segsum(dyh * yh, bd) * (1.0 / HEAD)))
        dvb_ref[...] = dz * rkk
        drkk = _segsum(dz * vv, bd)
        drb_ref[...] = drkk * kv * rkw
        dkb_ref[...] = drkk * rv * rkw
        acc_ref[0] += _fold8(dz * yh)
        acc_ref[1] += _fold8(dz)
        acc_ref[2] += _fold8(drkk * rv * kv)

        @pl.when(i == n - 1)
        def _():
            for j in range(3):
                sums_ref[j:j + 1, :] = jnp.sum(acc_ref[j], axis=0, keepdims=True)

    vec = _vec_spec(1, R)
    dspec = _row_spec(tm, R)
    return pl.pallas_call(
        body, name=name, grid=(n,),
        in_specs=[dspec] + [_row_spec(tm, R)] * 5 + [vec, vec, vec],
        out_specs=tuple([_row_spec(tm, R)] * 5) + (_vec_spec(3, R),),
        out_shape=tuple([jax.ShapeDtypeStruct((S, R), F32)] * 5) + (jax.ShapeDtypeStruct((3, R), F32),),
        scratch_shapes=[pltpu.VMEM((3, 8, R), F32)],
        compiler_params=_cp(("arbitrary",), 56),
    )(dcat, ysc, r, k2, v, g, lnw, lnb, rk)


def _half_sums(x, m_a):
    s_a = jnp.sum(jnp.where(m_a, x, 0.0), axis=1, keepdims=True)
    s_b = jnp.sum(jnp.where(m_a, 0.0, x), axis=1, keepdims=True)
    return s_a, s_b


SEL_ROWS = 64


def _column_selector():
    row = lax.broadcasted_iota(jnp.int32, (SEL_ROWS, 8 * 128), 0)
    col = lax.broadcasted_iota(jnp.int32, (SEL_ROWS, 8 * 128), 1)
    head, rest = row // 32, row % 32
    hit = (rest < 24) & (rest % 8 == col // 128) & (head == (col % 128) // HEAD)
    return jnp.where(hit, 1.0, 0.0).astype(BF16)


def _expand_columns(x, sel):
    hi = x.astype(BF16).astype(F32)
    r1 = x - hi
    mid = r1.astype(BF16).astype(F32)
    lo = (r1 - mid).astype(BF16).astype(F32)
    terms = jnp.concatenate([hi, mid, lo, jnp.zeros_like(x)], axis=0)
    both = jnp.concatenate([terms, pltpu.roll(terms, HEAD, 1)], axis=0)[:, 0:HEAD]
    return lax.dot_general(both.astype(BF16), sel, (((0,), (0,)), ((), ())), preferred_element_type=F32)


def _scan_fwd(r, w, k, v, kap, b, *, name):
    S, R = r.shape
    G, T = SCAN_G, SCAN_T
    NP = R // 128
    assert NP % G == 0 and S % T == 0
    GW = 128 * G

    def body(r_ref, w_ref, k_ref, v_ref, kap_ref, b_ref, sel_ref, y_ref, sa_ref, st_ref, s_scr, vc_scr, yt_scr,
             sat_scr):
        c = pl.program_id(1)

        @pl.when(c == 0)
        def _():
            s_scr[...] = jnp.zeros_like(s_scr)

        yt_scr[...] = jnp.zeros_like(yt_scr)
        sat_scr[...] = jnp.zeros_like(sat_scr)
        lane = lax.broadcasted_iota(jnp.int32, (HEAD, 128), 1)
        m_a = lane < HEAD

        def block(tb, carry):
            t0 = pl.multiple_of(tb * 8, 8)
            rb, wb, kb = r_ref[pl.ds(t0, 8), :], w_ref[pl.ds(t0, 8), :], k_ref[pl.ds(t0, 8), :]
            pb, bb, vb = kap_ref[pl.ds(t0, 8), :], b_ref[pl.ds(t0, 8), :], v_ref[pl.ds(t0, 8), :]
            for g in range(G):
                vc_scr[g] = _expand_columns(vb[:, g * 128:(g + 1) * 128], sel_ref[...])

            def put_y(g, parts, hot_y):
                yt_scr[g, 0:HEAD, :] = jnp.where(hot_y, parts[0], yt_scr[g, 0:HEAD, :])
                yt_scr[g, HEAD:, :] = jnp.where(hot_y, parts[1], yt_scr[g, HEAD:, :])

            def put_sa(g, parts, hot_t):
                sat_scr[g, 0:HEAD, :] = jnp.where(hot_t, -parts[0], sat_scr[g, 0:HEAD, :])
                sat_scr[g, HEAD:, :] = jnp.where(hot_t, -parts[1], sat_scr[g, HEAD:, :])

            for j in range(8):
                t = t0 + j
                cols = slice(j * 128, (j + 1) * 128)
                sa_parts, y_parts = [], []
                for g in range(G):
                    sl = slice(g * 128, (g + 1) * 128)
                    sa_parts.append(_half_sums(s_scr[g] * pb[j:j + 1, sl], m_a))
                if j > 0:
                    for g in range(G):
                        sl = slice(g * 128, (g + 1) * 128)
                        y_parts.append(_half_sums(s_scr[g] * rb[j - 1:j, sl], m_a))
                hot_t = lane == t
                for g in range(G):
                    sl = slice(g * 128, (g + 1) * 128)
                    sa = -jnp.where(m_a, sa_parts[g][0], sa_parts[g][1])
                    st = s_scr[g] * wb[j:j + 1, sl] + sa * bb[j:j + 1, sl] + vc_scr[g, :, cols] * kb[j:j + 1, sl]
                    s_scr[g] = st
                    st_ref[g, t] = st
                    put_sa(g, sa_parts[g], hot_t)
                if j > 0:
                    hot_y = lane == t - 1
                    for g in range(G):
                        put_y(g, y_parts[g], hot_y)
            hot_y = lane == t0 + 7
            for g in range(G):
                sl = slice(g * 128, (g + 1) * 128)
                put_y(g, _half_sums(s_scr[g] * rb[7:8, sl], m_a), hot_y)
            return carry

        lax.fori_loop(0, T // 8, block, 0)
        for g in range(G):
            y_ref[:, g * 128:(g + 1) * 128] = yt_scr[g].T[0:T, :]
            sa_ref[:, g * 128:(g + 1) * 128] = sat_scr[g].T[0:T, :]

    tspec = pl.BlockSpec((T, GW), lambda p, c: (c, p))
    sel_spec = pl.BlockSpec((SEL_ROWS, 8 * 128), lambda p, c: (0, 0))
    return pl.pallas_call(
        body, name=name, grid=(NP // G, S // T),
        in_specs=[tspec] * 6 + [sel_spec],
        out_specs=(tspec, tspec, pl.BlockSpec((G, T, HEAD, 128), lambda p, c: (p, c, 0, 0))),
        out_shape=(jax.ShapeDtypeStruct((S, R), F32), jax.ShapeDtypeStruct((S, R), F32),
                   jax.ShapeDtypeStruct((NP, S, HEAD, 128), F32)),
        scratch_shapes=[pltpu.VMEM((G, HEAD, 128), F32), pltpu.VMEM((G, HEAD, 8 * 128), F32),
                        pltpu.VMEM((G, 128, 128), F32), pltpu.VMEM((G, 128, 128), F32)],
        compiler_params=_cp(("parallel", "arbitrary"), 48),
    )(r, w, k, v, kap, b, _column_selector())


def _scan_bwd(r, w, k, v, kap, b, sa, dy, states, *, name):
    S, R = r.shape
    G, T = SCAN_G, SCAN_T
    NP = R // 128
    NC = S // T
    GW = 128 * G

    def body(r_ref, w_ref, k_ref, v_ref, kap_ref, b_ref, sa_ref, dy_ref, st_ref, sp_ref, sel_ref, dr_ref, dw_ref,
             dk_ref, dv_ref, dkap_ref, db_ref, ds_scr, vc_scr, dyc_scr, sac_scr, dvt_scr):
        ci = pl.program_id(1)

        @pl.when(ci == 0)
        def _():
            ds_scr[...] = jnp.zeros_like(ds_scr)

        dvt_scr[...] = jnp.zeros_like(dvt_scr)
        lane = lax.broadcasted_iota(jnp.int32, (HEAD, 128), 1)
        m_a = lane < HEAD
        sub = lax.broadcasted_iota(jnp.int32, (8, 128), 0)
        zero_i = jnp.zeros((HEAD, 128), jnp.int32)
        has_prev = jnp.where(ci < NC - 1, 1.0, 0.0)

        def state_before(g, t):
            at_start = (zero_i + t) == 0
            return jnp.where(at_start, sp_ref[g, 0] * has_prev, st_ref[g, jnp.maximum(t - 1, 0)])

        def block(it, carry):
            tb = T // 8 - 1 - it
            t0 = pl.multiple_of(tb * 8, 8)
            rb, wb, kb = r_ref[pl.ds(t0, 8), :], w_ref[pl.ds(t0, 8), :], k_ref[pl.ds(t0, 8), :]
            pb, bb = kap_ref[pl.ds(t0, 8), :], b_ref[pl.ds(t0, 8), :]
            vb, dyb, sab = v_ref[pl.ds(t0, 8), :], dy_ref[pl.ds(t0, 8), :], sa_ref[pl.ds(t0, 8), :]
            for g in range(G):
                sl = slice(g * 128, (g + 1) * 128)
                vc_scr[g] = _expand_columns(vb[:, sl], sel_ref[...])
                dyc_scr[g] = _expand_columns(dyb[:, sl], sel_ref[...])
                sac_scr[g] = _expand_columns(sab[:, sl], sel_ref[...])
            outs = [[jnp.zeros((8, 128), F32) for _ in range(5)] for _ in range(G)]
            for j in range(7, -1, -1):
                t = t0 + j
                hot = lane == t
                cols = slice(j * 128, (j + 1) * 128)
                dsa_parts, dv_parts = [], []
                for g in range(G):
                    sl = slice(g * 128, (g + 1) * 128)
                    ds = ds_scr[g] + dyc_scr[g, :, cols] * rb[j:j + 1, sl]
                    ds_scr[g] = ds
                    dsa_parts.append(_half_sums(ds * bb[j:j + 1, sl], m_a))
                for g in range(G):
                    sl = slice(g * 128, (g + 1) * 128)
                    dv_parts.append(_half_sums(ds_scr[g] * kb[j:j + 1, sl], m_a))
                ds_parts = list(zip(dv_parts, dsa_parts))
                for g in range(G):
                    sl = slice(g * 128, (g + 1) * 128)
                    w_r, p_r = wb[j:j + 1, sl], pb[j:j + 1, sl]
                    ds = ds_scr[g]
                    s_p = state_before(g, t)
                    dr_row = jnp.sum(st_ref[g, t] * dyc_scr[g, :, cols], axis=0, keepdims=True)
                    dk_row = jnp.sum(ds * vc_scr[g, :, cols], axis=0, keepdims=True)
                    db_row = jnp.sum(ds * sac_scr[g, :, cols], axis=0, keepdims=True)
                    dw_row = jnp.sum(ds * s_p, axis=0, keepdims=True)
                    (dv_a, dv_b), (dsa_a, dsa_b) = ds_parts[g]
                    dvt_scr[g, 0:HEAD, :] = jnp.where(hot, dv_a, dvt_scr[g, 0:HEAD, :])
                    dvt_scr[g, HEAD:, :] = jnp.where(hot, dv_b, dvt_scr[g, HEAD:, :])
                    dsa = jnp.where(m_a, dsa_a, dsa_b)
                    dkap_row = -jnp.sum(s_p * dsa, axis=0, keepdims=True)
                    ds_scr[g] = ds * w_r - dsa * p_r
                    pick = sub == j
                    for q, row in enumerate((dr_row, dw_row, dk_row, dkap_row, db_row)):
                        outs[g][q] = jnp.where(pick, row, outs[g][q])
            for g in range(G):
                sl = slice(g * 128, (g + 1) * 128)
                for q, ref in enumerate((dr_ref, dw_ref, dk_ref, dkap_ref, db_ref)):
                    ref[pl.ds(t0, 8), sl] = outs[g][q]
            return carry

        lax.fori_loop(0, T // 8, block, 0)
        for g in range(G):
            dv_ref[:, g * 128:(g + 1) * 128] = dvt_scr[g].T[0:T, :]

    tspec = pl.BlockSpec((T, GW), lambda p, c: (NC - 1 - c, p))
    st_spec = pl.BlockSpec((G, T, HEAD, 128), lambda p, c: (p, NC - 1 - c, 0, 0))
    prev_spec = pl.BlockSpec((G, 1, HEAD, 128), lambda p, c: (p, jnp.maximum((NC - 1 - c) * T - 1, 0), 0, 0))
    sel_spec = pl.BlockSpec((SEL_ROWS, 8 * 128), lambda p, c: (0, 0))
    return pl.pallas_call(
        body, name=name, grid=(NP // G, NC),
        in_specs=[tspec] * 8 + [st_spec, prev_spec, sel_spec],
        out_specs=tuple([tspec] * 6),
        out_shape=tuple([jax.ShapeDtypeStruct((S, R), F32)] * 6),
        scratch_shapes=[pltpu.VMEM((G, HEAD, 128), F32), pltpu.VMEM((G, HEAD, 8 * 128), F32),
                        pltpu.VMEM((G, HEAD, 8 * 128), F32), pltpu.VMEM((G, HEAD, 8 * 128), F32),
                        pltpu.VMEM((G, 128, 128), F32)],
        compiler_params=_cp(("parallel", "arbitrary"), 48),
    )(r, w, k, v, kap, b, sa, dy, states, states, _column_selector())


def _sum_parts(parts, *, name):
    P, rows, W = parts.shape
    tr = rows
    for cand in (1024, 512, 256, 128, 64, 32, 16, 8):
        if rows % cand == 0:
            tr = cand
            break

    def body(p_ref, o_ref):
        acc = p_ref[0]
        for s in range(1, P):
            acc = acc + p_ref[s]
        o_ref[...] = acc

    return pl.pallas_call(
        body, name=name, grid=(rows // tr,),
        in_specs=[pl.BlockSpec((P, tr, W), lambda i: (0, i, 0))],
        out_specs=pl.BlockSpec((tr, W), lambda i: (i, 0)), out_shape=jax.ShapeDtypeStruct((rows, W), F32),
        compiler_params=_cp(("parallel",), 32),
    )(parts)


def _adamw(w, m, v, parts, *, name):
    R, C = w.shape
    P = parts.shape[0]
    tr = R
    for cand in (1024, 512, 256, 128, 64, 32, 16, 8):
        if R % cand == 0 and cand * C * 4 * (7 + P) <= 10 * 1024 * 1024:
            tr = cand
            break
    bc1 = 1.0 - ADAM_B1 ** ADAM_STEP
    bc2 = 1.0 - ADAM_B2 ** ADAM_STEP

    def body(w_ref, m_ref, v_ref, p_ref, g_ref, d_ref, nm_ref, nv_ref):
        g = p_ref[0].astype(F32)
        for s in range(1, P):
            g = g + p_ref[s].astype(F32)
        m1 = ADAM_B1 * m_ref[...] + (1.0 - ADAM_B1) * g
        v1 = ADAM_B2 * v_ref[...] + (1.0 - ADAM_B2) * (g * g)
        m_hat = m1 / bc1
        v_hat = v1 / bc2
        g_ref[...] = g
        d_ref[...] = -ADAM_LR * (m_hat / (jnp.sqrt(v_hat) + ADAM_EPS) + ADAM_WD * w_ref[...])
        nm_ref[...] = m1
        nv_ref[...] = v1

    spec = pl.BlockSpec((tr, C), lambda i: (i, 0))
    return pl.pallas_call(
        body, name=name, grid=(R // tr,),
        in_specs=[spec, spec, spec, pl.BlockSpec((P, tr, C), lambda i: (0, i, 0))],
        out_specs=(spec, spec, spec, spec), out_shape=tuple([jax.ShapeDtypeStruct((R, C), F32)] * 4),
        compiler_params=_cp(("parallel",), 40),
    )(w, m, v, parts)


def _cols_full(g8):
    n, rows, c = g8.shape
    return jnp.transpose(g8, (1, 0, 2)).reshape(rows, n * c)


def _cols_split(full):
    rows, cols = full.shape
    return jnp.transpose(full.reshape(rows, N_DEV, cols // N_DEV), (1, 0, 2))


def _pack(vals, rows_multiple=512):
    flat = jnp.concatenate([v.reshape(-1).astype(F32) for v in vals])
    n = flat.shape[0]
    unit = 128 * rows_multiple
    padded = ((n + unit - 1) // unit) * unit
    return jnp.pad(flat, (0, padded - n)).reshape(padded // 128, 128)


def _unpack(packed, shapes):
    flat = packed.reshape(-1)
    out, off = [], 0
    for shp in shapes:
        size = 1
        for d in shp:
            size *= d
        out.append(flat[off:off + size].reshape(shp))
        off += size
    return out


def _ffn_forward(x, weights, gpre, gpost, shift, scale1p, gw, tag):
    g8, u8, d8 = weights
    h = _pre_norm_mod(x, gpre, shift, scale1p, name=f"{tag}_pre")
    au, s = _ffn_up(h, g8, u8, tm=1024, tk=2048, name=f"{tag}_up")
    f = _mm(s, d8.reshape(-1, d8.shape[2]), tm=1024, tn=1024, tk=2048, name=f"{tag}_down")
    xo = _post_norm_res(x, f, gpost, gw, name=f"{tag}_post")
    return xo, (h, au, s, f)


def _ffn_backward(dxo, x, saved, weights, gpre, gpost, scale1p, gw, tag):
    g8, u8, d8 = weights
    h, au, s, f = saved
    df, post_sums = _post_norm_res_bwd(dxo, f, gpost, gw, MACARON, name=f"{tag}_post_bwd")
    dwd = _ffn_dwd(s, df, tn=1024, tk=1024, name=f"{tag}_dwd")
    dau = _ffn_down_bwd(df, d8, au, tm=1024, tk=2048, name=f"{tag}_down_bwd")
    dwgu = _ffn_dwgu(h, dau, tm=1024, tk=1024, name=f"{tag}_dwgu")
    dh = _ffn_dh(dau, g8, u8, tm=1024, tn=1024, name=f"{tag}_dh")
    dx, pre_sums = _pre_norm_mod_bwd(dh, x, dxo, gpre, scale1p, name=f"{tag}_pre_bwd")
    return dx, dwgu, dwd, pre_sums, post_sums


def kernel(x, c, w_ada, b_ada, norm_pre, norm_post, ffn1_w_gate, ffn1_w_up, ffn1_w_down, w_in, mu_shift, pool_w, pool_scale, w0, w2, a0, a2, g2, k_k, k_a, r_k, lnx_w, lnx_b, w_out, ffn2_w_gate, ffn2_w_up, ffn2_w_down, loss_target, m_w_ada, m_b_ada, m_norm_pre, m_norm_post, m_ffn1_w_gate, m_ffn1_w_up, m_ffn1_w_down, m_w_in, m_mu_shift, m_pool_w, m_pool_scale, m_w0, m_w2, m_a0, m_a2, m_g2, m_k_k, m_k_a, m_r_k, m_lnx_w, m_lnx_b, m_w_out, m_ffn2_w_gate, m_ffn2_w_up, m_ffn2_w_down, v_w_ada, v_b_ada, v_norm_pre, v_norm_post, v_ffn1_w_gate, v_ffn1_w_up, v_ffn1_w_down, v_w_in, v_mu_shift, v_pool_w, v_pool_scale, v_w0, v_w2, v_a0, v_a2, v_g2, v_k_k, v_k_a, v_r_k, v_lnx_w, v_lnx_b, v_w_out, v_ffn2_w_gate, v_ffn2_w_up, v_ffn2_w_down):
    names = ["w_ada", "b_ada", "norm_pre", "norm_post", "ffn1_w_gate", "ffn1_w_up", "ffn1_w_down", "w_in", "mu_shift",
             "pool_w", "pool_scale", "w0", "w2", "a0", "a2", "g2", "k_k", "k_a", "r_k", "lnx_w", "lnx_b", "w_out",
             "ffn2_w_gate", "ffn2_w_up", "ffn2_w_down"]
    env = dict(locals())
    W = {n: env[n][0] for n in names}
    M1 = {n: env["m_" + n][0] for n in names}
    V1 = {n: env["v_" + n][0] for n in names}

    me = _my_index()
    xs = x[0]
    tgt = loss_target[0]
    S, D = xs.shape
    F = W["ffn1_w_gate"].shape[1] * N_DEV
    R = W["w0"].shape[0]
    PW = D - R
    IN_W = W["w_in"].shape[1] * N_DEV
    P_W = F
    QW = P_W - PW
    NMOD = 9 * D
    ada_c = W["w_ada"].shape[1]

    c_all, npre8, npost8, w2_8, a2_8, g2_8 = _exchange(
        [c, W["norm_pre"], W["norm_post"], W["w2"].astype(BF16), W["a2"].astype(BF16), W["g2"].astype(BF16)],
        scatter=False, name="gather_small")
    c_all = c_all.reshape(N_DEV, D)
    gpre = _cols_full(npre8)
    gpost = _cols_full(npost8)
    wl = jnp.zeros((LORA_PAD, 3 * R), BF16)
    wl = wl.at[0:LORA_W, 0:R].set(_cols_full(w2_8))
    wl = wl.at[LORA_W:LORA_W + LORA_A, R:2 * R].set(_cols_full(a2_8))
    wl = wl.at[LORA_W + LORA_A:LORA_W + LORA_A + LORA_G, 2 * R:3 * R].set(_cols_full(g2_8))

    sc_all = jax.nn.silu(c_all)
    sc_pad = jnp.concatenate([sc_all, jnp.zeros((8, D), F32)], axis=0).astype(BF16)
    modcols = _mm(sc_pad, W["w_ada"], tm=16, tn=ada_c, tk=256, name="ada_fwd")[0:N_DEV]
    modcols = modcols + lax.dynamic_slice(W["b_ada"], (me * ada_c,), (ada_c,))[None, :]
    (mod8,) = _exchange([modcols], scatter=False, name="gather_mod")
    mod = lax.dynamic_index_in_dim(mod8, me, axis=1, keepdims=False).reshape(9, D)

    def mod_row(i):
        return mod[i:i + 1, :]

    f_pad = FF_TILE - F // N_DEV

    def gather_ffn(tag, name):
        return _gather_two_level(
            [jnp.pad(W[f"{tag}_w_gate"].astype(BF16), ((0, 0), (0, f_pad))),
             jnp.pad(W[f"{tag}_w_up"].astype(BF16), ((0, 0), (0, f_pad))),
             jnp.pad(W[f"{tag}_w_down"].astype(BF16), ((0, f_pad), (0, 0)))], name=name)

    ffn1_w = gather_ffn("ffn1", "gather_ffn")
    win8, wout8 = _gather_two_level([W["w_in"].astype(BF16), W["w_out"].astype(BF16)], name="gather_mixer")
    w_in_p = jnp.pad(_cols_full(win8), ((0, 0), (0, P_W - IN_W)))
    w_out_f = wout8.reshape(D, D)
    ffn2_w = gather_ffn("ffn2", "gather_ffn")

    mu_p = jnp.pad(W["mu_shift"], (0, QW - W["mu_shift"].shape[0]))[None, :]
    vec = lambda a: a.reshape(1, -1)
    w0r, a0r, kkr, kar = vec(W["w0"]), vec(W["a0"]), vec(W["k_k"]), vec(W["k_a"])
    lnw, lnb, rkr = vec(W["lnx_w"]), vec(W["lnx_b"]), vec(W["r_k"])
    pscale = vec(W["pool_scale"])

    sc1p = [1.0 + mod_row(3 * s + 1) for s in range(3)]
    shifts = [mod_row(3 * s) for s in range(3)]
    wgts = [MACARON, 1.0, MACARON]
    gws = [wgts[s] * (1.0 + mod_row(3 * s + 2)) for s in range(3)]
    gp = [gpre[s:s + 1] for s in range(3)]
    gq = [gpost[s:s + 1] for s in range(3)]

    x1, sv1 = _ffn_forward(xs, ffn1_w, gp[0], gq[0], shifts[0], sc1p[0], gws[0], "ffn")

    h2 = _pre_norm_mod(x1, gp[1], shifts[1], sc1p[1], name="mix_pre")
    p = _mm(h2, w_in_p, tm=1024, tn=512, tk=2048, name="mix_in")
    q = p[:, PW:]
    o_pool, y_pool = _pool_fwd(p, W["pool_w"], pscale, name="pool_fwd")
    r_s, w_s, k_s, v_s, kap_s, b_s, g_s = _rwkv_prep(q, mu_p, wl, w0r, a0r, kkr, kar, name="rwkv_prep")
    y_scan, sa_s, states = _scan_fwd(r_s, w_s, k_s, v_s, kap_s, b_s, name="scan_fwd")
    cat = _rwkv_post(y_scan, r_s, k_s, v_s, g_s, y_pool, lnw, lnb, rkr, name="rwkv_post")
    f2 = _mm(cat, w_out_f, tm=1024, tn=1024, tk=2048, name="mix_out")
    x2 = _post_norm_res(x1, f2, gq[1], gws[1], name="mix_post")

    x3, sv3 = _ffn_forward(x2, ffn2_w, gp[2], gq[2], shifts[2], sc1p[2], gws[2], "ffn")

    loss_part, dx3 = _loss_head(x3, tgt, name="loss_head")
    loss = lax.psum(loss_part[0, 0], MESH_AXES)

    dx2, dwgu2, dwd2, pre3, post3 = _ffn_backward(dx3, x2, sv3, ffn2_w, gp[2], gq[2], sc1p[2], gws[2], "ffn")

    df2, post2 = _post_norm_res_bwd(dx2, f2, gq[1], gws[1], 1.0, name="mix_post_bwd")
    dw_out = _mm(cat, df2, ta=True, tm=1024, tn=1024, tk=1024, name="mix_dwout")
    dcat = _mm(df2, w_out_f, tb=True, tm=1024, tn=1024, tk=2048, name="mix_dcat")
    dyr = dcat[:, PW:]
    dysc, dg, dr_b, dk2_b, dv_b, post_sums = _rwkv_post_bwd(dyr, y_scan, r_s, k_s, v_s, g_s, lnw, lnb, rkr,
                                                             name="rwkv_post_bwd")
    dr, dw, dk2, dv, dkap, db = _scan_bwd(r_s, w_s, k_s, v_s, kap_s, b_s, sa_s, dysc, states, name="scan_bwd")
    dps, dwl, prep_sums = _rwkv_prep_bwd(q, mu_p, wl, w0r, a0r, kkr, kar,
                                         (dr, dw, dk2, dv, dkap, db, dg, dr_b, dk2_b, dv_b), name="rwkv_prep_bwd")
    dq, dmu = _tshift_bwd(dps, q, mu_p, name="tshift_bwd")
    du_pool, dpool_w, dpool_scale = _pool_bwd(dcat, o_pool, W["pool_w"], pscale, name="pool_bwd")
    dp = jnp.concatenate([du_pool, dq], axis=1)
    dw_in = _mm(h2, dp, ta=True, tm=1024, tn=512, tk=1024, name="mix_dwin")
    dh2 = _mm(dp, w_in_p, tb=True, tm=1024, tn=1024, tk=2816, name="mix_dh")
    dx1, pre2 = _pre_norm_mod_bwd(dh2, x1, dx2, gp[1], sc1p[1], name="mix_pre_bwd")

    dx0, dwgu1, dwd1, pre1, post1 = _ffn_backward(dx1, xs, sv1, ffn1_w, gp[0], gq[0], sc1p[0], gws[0], "ffn")

    pres, posts = [pre1, pre2, pre3], [post1, post2, post3]
    dmod = jnp.stack([jnp.stack([pres[s][0], pres[s][1], posts[s][0]]) for s in range(3)]).reshape(NMOD // 128, 128)
    dnorm_pre = jnp.stack([pres[s][2] for s in range(3)])
    dnorm_post = jnp.stack([posts[s][1] for s in range(3)])

    small = [dmu[0, :W["mu_shift"].shape[0]], dpool_w, dpool_scale, prep_sums[0], prep_sums[1], prep_sums[2],
             prep_sums[3], post_sums[2], post_sums[0], post_sums[1], dnorm_pre, dnorm_post,
             dwl[0:LORA_W, 0:R], dwl[LORA_W:LORA_W + LORA_A, R:2 * R],
             dwl[LORA_W + LORA_A:LORA_W + LORA_A + LORA_G, 2 * R:3 * R]]
    small_shapes = [a.shape for a in small]
    dmod8, small8 = _gather_two_level([dmod, _pack(small)], name="gather_grads")
    g_b_ada = _sum_parts(dmod8, name="sum_dmod").reshape(NMOD)
    red = _unpack(_sum_parts(small8, name="sum_small"), small_shapes)
    (g_mu, g_pool_w, g_pool_scale, g_w0, g_a0, g_kk, g_ka, g_rk, g_lnw, g_lnb, g_npre, g_npost, g_w2, g_a2,
     g_g2) = red

    dmod_all = dmod8.reshape(N_DEV, NMOD)
    dmod_cols = lax.dynamic_slice(dmod_all, (0, me * ada_c), (N_DEV, ada_c))
    dmod_cols = jnp.concatenate([dmod_cols, jnp.zeros_like(dmod_cols)], axis=0)
    g_w_ada = _mm(sc_pad, dmod_cols, ta=True, tm=D, tn=ada_c // 9, tk=16, name="ada_bwd")

    def by_core_chip(blocks):
        shp = blocks.shape
        t = blocks.astype(BF16).reshape((N_DEV // 2, 2) + shp[1:])
        return jnp.swapaxes(t, 0, 1)

    def scatter(mine, tag):
        got = _sibling_swap(mine, name=f"{tag}_swap")
        sums = [_pair_add(m, g, name=f"{tag}_add{i}") for i, (m, g) in enumerate(zip(mine, got))]
        return _chips_all_to_all(sums, name=f"{tag}_chips")

    def scatter_ffn(dwgu, dwd, tag):
        pgu, pd = scatter([dwgu, dwd], tag)
        fs = F // N_DEV
        return pgu[:, :D, :fs], pgu[:, D:, :fs], pd[:, :fs, :]

    pg2, pu2, pd2 = scatter_ffn(dwgu2, dwd2, "scatter_ffn")
    pin, pout = scatter([by_core_chip(_cols_split(dw_in[:, :IN_W])),
                         by_core_chip(dw_out.reshape(N_DEV, D // N_DEV, D))], "scatter_mixer")
    pg1, pu1, pd1 = scatter_ffn(dwgu1, dwd1, "scatter_ffn")

    res = {}

    def big(nm, parts, tag):
        res[nm] = _adamw(W[nm], M1[nm], V1[nm], parts, name=tag)

    big("ffn1_w_gate", pg1, "adamw_cols")
    big("ffn1_w_up", pu1, "adamw_cols")
    big("ffn1_w_down", pd1, "adamw_rows")
    big("ffn2_w_gate", pg2, "adamw_cols")
    big("ffn2_w_up", pu2, "adamw_cols")
    big("ffn2_w_down", pd2, "adamw_rows")
    big("w_in", pin, "adamw_w_in")
    big("w_out", pout, "adamw_w_out")
    big("w_ada", g_w_ada[None], "adamw_w_ada")

    def my_cols(full, width):
        return lax.dynamic_slice_in_dim(full, me * width, width, axis=full.ndim - 1)

    small_names = ["b_ada", "mu_shift", "pool_w", "pool_scale", "w0", "a0", "k_k", "k_a", "r_k", "lnx_w", "lnx_b",
                   "norm_pre", "norm_post", "w2", "a2", "g2"]
    small_grads = [g_b_ada, g_mu, g_pool_w, g_pool_scale, g_w0, g_a0, g_kk, g_ka, g_rk.reshape(W["r_k"].shape), g_lnw,
                   g_lnb, my_cols(g_npre, D // N_DEV), my_cols(g_npost, D // N_DEV), my_cols(g_w2, R // N_DEV),
                   my_cols(g_a2, R // N_DEV), my_cols(g_g2, R // N_DEV)]
    shapes = [W[n].shape for n in small_names]
    packed = _adamw(_pack([W[n] for n in small_names]), _pack([M1[n] for n in small_names]),
                    _pack([V1[n] for n in small_names]), _pack(small_grads)[None], name="adamw_small")
    unpacked = [_unpack(t, shapes) for t in packed]
    for i, nm in enumerate(small_names):
        res[nm] = tuple(unpacked[k][i] for k in range(4))

    outs = [loss, dx0[None]]
    for k in range(4):
        outs.extend(res[nm][k][None] for nm in names)
    return tuple(outs)
```

```python
import functools

import jax
import jax.numpy as jnp
from jax import lax
from jax.experimental import pallas as pl
from jax.experimental.pallas import tpu as pltpu

F32 = jnp.float32
BF16 = jnp.bfloat16
N_DEV = 8
MESH_AXES = ("x", "y", "c")

NORM_EPS = 1e-6
HEAD = 64
LN_X_EPS = 1e-5 * HEAD
POOL_GROUPS = 4
POOL_GROUP = 128
MACARON = 0.5
LORA_W, LORA_A, LORA_G = 64, 64, 224
LORA_PAD = 384
ADAM_LR, ADAM_B1, ADAM_B2, ADAM_EPS, ADAM_WD, ADAM_STEP = 0.001, 0.9, 0.999, 1e-08, 0.01, 10

FF_TILE = 768
ROW_TILE = 256
SCAN_T = 64
SCAN_G = 6
VMEM_CAP = 56 * 1024 * 1024


def _cp(sem, vmem_mb):
    return pltpu.CompilerParams(dimension_semantics=sem, vmem_limit_bytes=min(vmem_mb * 1024 * 1024, VMEM_CAP))


def _my_index():
    return 4 * lax.axis_index("x") + 2 * lax.axis_index("y") + lax.axis_index("c")


def _exchange(arrays, *, scatter, name):
    n = len(arrays)
    out_shapes = []
    for a in arrays:
        shp = a.shape if scatter else (N_DEV,) + a.shape
        out_shapes.append(jax.ShapeDtypeStruct(shp, a.dtype))

    def body(*refs):
        ins, outs = refs[:n], refs[n:2 * n]
        send_sems, recv_sems, local_sems = refs[2 * n:]
        me = _my_index()

        def dev(p):
            return (p // 4, (p // 2) % 2, p % 2)

        def copy(i, d):
            peer = (me + d) % N_DEV
            src = ins[i].at[peer] if scatter else ins[i]
            return pltpu.make_async_remote_copy(
                src_ref=src, dst_ref=outs[i].at[me], send_sem=send_sems.at[i, d - 1],
                recv_sem=recv_sems.at[i, d - 1], device_id=dev(peer), device_id_type=pl.DeviceIdType.MESH)

        def arrival(i, d):
            frm = (me + N_DEV - d) % N_DEV
            src = ins[i].at[frm] if scatter else ins[i]
            return pltpu.make_async_remote_copy(
                src_ref=src, dst_ref=outs[i].at[frm], send_sem=send_sems.at[i, d - 1],
                recv_sem=recv_sems.at[i, d - 1], device_id=dev(frm), device_id_type=pl.DeviceIdType.MESH)

        locals_ = []
        for i in range(n):
            src = ins[i].at[me] if scatter else ins[i]
            lc = pltpu.make_async_copy(src, outs[i].at[me], local_sems.at[i])
            lc.start()
            locals_.append(lc)
        sends = [copy(i, d) for d in range(1, N_DEV) for i in range(n)]
        for cp in sends:
            cp.start()
        for d in range(1, N_DEV):
            for i in range(n):
                arrival(i, d).wait_recv()
        for cp in sends:
            cp.wait_send()
        for lc in locals_:
            lc.wait()

    hbm = pl.BlockSpec(memory_space=pltpu.HBM)
    return pl.pallas_call(
        body, name=name, out_shape=tuple(out_shapes), in_specs=[hbm] * n, out_specs=tuple([hbm] * n),
        scratch_shapes=[pltpu.SemaphoreType.DMA((n, N_DEV - 1)), pltpu.SemaphoreType.DMA((n, N_DEV - 1)),
                        pltpu.SemaphoreType.DMA((n,))],
    )(*arrays)


def _remote(src, dst, send_sem, recv_sem, to):
    return pltpu.make_async_remote_copy(src_ref=src, dst_ref=dst, send_sem=send_sem, recv_sem=recv_sem,
                                        device_id=to, device_id_type=pl.DeviceIdType.MESH)


class _GatherPlan:
    def __init__(self, arrays):
        self.arrays = list(arrays)
        self.n = len(arrays)
        self.out_shapes = [jax.ShapeDtypeStruct((N_DEV,) + a.shape, a.dtype) for a in arrays]
        self.scratch = [pltpu.SemaphoreType.DMA((self.n, 7)), pltpu.SemaphoreType.DMA((self.n, 7)),
                        pltpu.SemaphoreType.DMA((self.n,))]

    def _parts(self, ins, outs, sems):
        send_sems, recv_sems, local_sems = sems
        x, y, c = lax.axis_index("x"), lax.axis_index("y"), lax.axis_index("c")
        chips = [(1 - x, y), (x, 1 - y), (1 - x, 1 - y)]

        def slot(i, px, py, pc):
            return outs[i].at[4 * px + 2 * py + pc]

        def copy(i, k, block, to, src=None):
            dst = slot(i, *block)
            return _remote(dst if src is None else src, dst, send_sems.at[i, k], recv_sems.at[i, k], to)

        n = self.n
        locals_ = [pltpu.make_async_copy(ins[i], slot(i, x, y, c), local_sems.at[i]) for i in range(n)]
        first = [copy(i, 1 + j, (x, y, c), (*chip, c), src=ins[i]) for j, chip in enumerate(chips) for i in range(n)]
        first += [copy(i, 0, (x, y, c), (x, y, 1 - c), src=ins[i]) for i in range(n)]
        return (x, y, c), chips, copy, locals_, first

    def start(self, ins, outs, sems):
        _, _, _, locals_, first = self._parts(ins, outs, sems)
        for lc in locals_:
            lc.start()
        for cp in first:
            cp.start()

    def finish(self, ins, outs, sems):
        (x, y, c), chips, copy, locals_, first = self._parts(ins, outs, sems)
        forwards = []
        for j, chip in enumerate(chips):
            for i in range(self.n):
                copy(i, 1 + j, (*chip, c), (x, y, c)).wait_recv()
                fwd = copy(i, 4 + j, (*chip, c), (x, y, 1 - c))
                fwd.start()
                forwards.append(fwd)
        for i in range(self.n):
            copy(i, 0, (x, y, 1 - c), (x, y, c)).wait_recv()
        for j, chip in enumerate(chips):
            for i in range(self.n):
                copy(i, 4 + j, (*chip, 1 - c), (x, y, c)).wait_recv()
        for cp in first + forwards:
            cp.wait_send()
        for lc in locals_:
            lc.wait()


class _ChipsPlan:
    def __init__(self, arrays):
        self.arrays = list(arrays)
        self.n = len(arrays)
        self.out_shapes = [jax.ShapeDtypeStruct(a.shape, a.dtype) for a in arrays]
        self.scratch = [pltpu.SemaphoreType.DMA((self.n, 3)), pltpu.SemaphoreType.DMA((self.n, 3)),
                        pltpu.SemaphoreType.DMA((self.n,))]

    def _parts(self, ins, outs, sems):
        send_sems, recv_sems, local_sems = sems
        x, y, c = lax.axis_index("x"), lax.axis_index("y"), lax.axis_index("c")
        mine = 2 * x + y
        chips = [(1 - x, y), (x, 1 - y), (1 - x, 1 - y)]
        n = self.n
        locals_ = [pltpu.make_async_copy(ins[i].at[mine], outs[i].at[mine], local_sems.at[i]) for i in range(n)]
        sends = [_remote(ins[i].at[2 * chip[0] + chip[1]], outs[i].at[mine], send_sems.at[i, j], recv_sems.at[i, j],
                         (*chip, c)) for j, chip in enumerate(chips) for i in range(n)]

        def arrivals():
            return [_remote(ins[i].at[2 * chip[0] + chip[1]], outs[i].at[2 * chip[0] + chip[1]], send_sems.at[i, j],
                            recv_sems.at[i, j], (*chip, c)) for j, chip in enumerate(chips) for i in range(n)]

        return locals_, sends, arrivals

    def start(self, ins, outs, sems):
        locals_, sends, _ = self._parts(ins, outs, sems)
        for lc in locals_:
            lc.start()
        for cp in sends:
            cp.start()

    def finish(self, ins, outs, sems):
        locals_, sends, arrivals = self._parts(ins, outs, sems)
        for cp in arrivals():
            cp.wait_recv()
        for cp in sends:
            cp.wait_send()
        for lc in locals_:
            lc.wait()


def _run_plan(plan, *, name):
    n = plan.n

    def body(*refs):
        ins, outs, sems = refs[:n], refs[n:2 * n], refs[2 * n:]
        plan.start(ins, outs, sems)
        plan.finish(ins, outs, sems)

    hbm = pl.BlockSpec(memory_space=pltpu.HBM)
    return pl.pallas_call(
        body, name=name, out_shape=tuple(plan.out_shapes), in_specs=[hbm] * n, out_specs=tuple([hbm] * n),
        scratch_shapes=plan.scratch,
    )(*plan.arrays)


def _host(body, plan, n_in, n_out, is_first, is_last):
    if plan is None:
        return body
    m = plan.n

    def wrapped(*refs):
        a, b = n_in, n_in + m
        c, d = b + n_out, b + n_out + m
        own_in, c_in, own_out, c_out, rest = refs[:a], refs[a:b], refs[b:c], refs[c:d], refs[d:]
        own_scr, c_sems = rest[:len(rest) - 3], rest[len(rest) - 3:]

        @pl.when(is_first())
        def _():
            plan.start(c_in, c_out, c_sems)

        body(*own_in, *own_out, *own_scr)

        @pl.when(is_last())
        def _():
            plan.finish(c_in, c_out, c_sems)

    return wrapped


def _host_args(plan):
    if plan is None:
        return [], [], [], [], []
    hbm = pl.BlockSpec(memory_space=pltpu.HBM)
    return [hbm] * plan.n, [hbm] * plan.n, list(plan.out_shapes), list(plan.scratch), list(plan.arrays)


def _gather_two_level(arrays, *, name):
    return _run_plan(_GatherPlan(arrays), name=name)


def _sibling_swap(arrays, *, name):
    n = len(arrays)
    out_shapes = [jax.ShapeDtypeStruct(a.shape[1:], a.dtype) for a in arrays]

    def body(*refs):
        ins, outs = refs[:n], refs[n:2 * n]
        send_sems, recv_sems = refs[2 * n:]
        x, y, c = lax.axis_index("x"), lax.axis_index("y"), lax.axis_index("c")
        copies = [_remote(ins[i].at[1 - c], outs[i], send_sems.at[i], recv_sems.at[i], (x, y, 1 - c))
                  for i in range(n)]
        for cp in copies:
            cp.start()
        for cp in copies:
            cp.wait_recv()
        for cp in copies:
            cp.wait_send()

    hbm = pl.BlockSpec(memory_space=pltpu.HBM)
    return pl.pallas_call(
        body, name=name, out_shape=tuple(out_shapes), in_specs=[hbm] * n, out_specs=tuple([hbm] * n),
        scratch_shapes=[pltpu.SemaphoreType.DMA((n,)), pltpu.SemaphoreType.DMA((n,))],
    )(*arrays)


def _chips_all_to_all(arrays, *, name):
    return _run_plan(_ChipsPlan(arrays), name=name)


def _pair_add(mine, got, *, name):
    _, nq, R, C = mine.shape
    tr = R
    for cand in (512, 256, 128, 64, 32, 16):
        if R % cand == 0 and cand * C * 2 * 3 * 2 <= 12 * 1024 * 1024:
            tr = cand
            break

    def body(core_ref, m_ref, g_ref, o_ref):
        o_ref[0] = (m_ref[0, 0].astype(F32) + g_ref[0].astype(F32)).astype(BF16)

    core = lax.axis_index("c").astype(jnp.int32).reshape(1)
    return pl.pallas_call(
        body, name=name,
        grid_spec=pltpu.PrefetchScalarGridSpec(
            num_scalar_prefetch=1, grid=(nq, R // tr),
            in_specs=[pl.BlockSpec((1, 1, tr, C), lambda q, i, core_ref: (core_ref[0], q, i, 0)),
                      pl.BlockSpec((1, tr, C), lambda q, i, core_ref: (q, i, 0))],
            out_specs=pl.BlockSpec((1, tr, C), lambda q, i, core_ref: (q, i, 0))),
        out_shape=jax.ShapeDtypeStruct((nq, R, C), BF16),
        compiler_params=_cp(("parallel", "parallel"), 40),
    )(core, mine, got)


def _mm(a, b, *, ta=False, tb=False, tm, tn, tk, out_dtype=F32, name):
    M = a.shape[1] if ta else a.shape[0]
    K = a.shape[0] if ta else a.shape[1]
    N = b.shape[0] if tb else b.shape[1]
    tm, tn, tk = min(tm, M), min(tn, N), min(tk, K)
    assert M % tm == 0 and N % tn == 0 and K % tk == 0, (name, M, N, K, tm, tn, tk)
    nk = K // tk
    dims = (((0 if ta else 1,), (1 if tb else 0,)), ((), ()))

    def body(a_ref, b_ref, o_ref, acc_ref):
        k = pl.program_id(2)

        @pl.when(k == 0)
        def _():
            acc_ref[...] = jnp.zeros_like(acc_ref)

        acc_ref[...] += lax.dot_general(a_ref[...].astype(BF16), b_ref[...].astype(BF16), dims,
                                        preferred_element_type=F32)

        @pl.when(k == nk - 1)
        def _():
            o_ref[...] = acc_ref[...].astype(out_dtype)

    a_spec = pl.BlockSpec((tk, tm), lambda i, j, k: (k, i)) if ta else pl.BlockSpec((tm, tk), lambda i, j, k: (i, k))
    b_spec = pl.BlockSpec((tn, tk), lambda i, j, k: (j, k)) if tb else pl.BlockSpec((tk, tn), lambda i, j, k: (k, j))
    blk = 2 * (tm * tk * a.dtype.itemsize + tk * tn * b.dtype.itemsize + tm * tn * jnp.dtype(out_dtype).itemsize)
    return pl.pallas_call(
        body, name=name, grid=(M // tm, N // tn, nk), in_specs=[a_spec, b_spec],
        out_specs=pl.BlockSpec((tm, tn), lambda i, j, k: (i, j)),
        out_shape=jax.ShapeDtypeStruct((M, N), out_dtype),
        scratch_shapes=[pltpu.VMEM((tm, tn), F32)],
        compiler_params=_cp(("parallel", "parallel", "arbitrary"), (blk + tm * tn * 4) // (1024 * 1024) + 12),
    )(a, b)


def _grid_ends(grid):
    def is_first():
        ok = pl.program_id(0) == 0
        for ax in range(1, len(grid)):
            ok = ok & (pl.program_id(ax) == 0)
        return ok

    def is_last():
        ok = pl.program_id(0) == grid[0] - 1
        for ax in range(1, len(grid)):
            ok = ok & (pl.program_id(ax) == grid[ax] - 1)
        return ok

    return is_first, is_last


def _semantics(plan, sem):
    return sem if plan is None else tuple("arbitrary" for _ in sem)


def _ffn_up(h, g8, u8, *, tm, tk, name, plan=None):
    S, D = h.shape
    nb, _, tn = g8.shape
    tm = min(tm, S)
    tk = min(tk, D)
    nk = D // tk

    def body(h_ref, g_ref, u_ref, au_ref, s_ref, acc_ref):
        k = pl.program_id(2)

        @pl.when(k == 0)
        def _():
            acc_ref[...] = jnp.zeros_like(acc_ref)

        hv = h_ref[...]
        acc_ref[:, :tn] += jnp.dot(hv, g_ref[0], preferred_element_type=F32)
        acc_ref[:, tn:] += jnp.dot(hv, u_ref[0], preferred_element_type=F32)

        @pl.when(k == nk - 1)
        def _():
            acc = acc_ref[...]
            a = acc[:, :tn]
            u = acc[:, tn:]
            au_ref[...] = acc.astype(BF16)
            s_ref[...] = (a * jax.nn.sigmoid(a) * u).astype(BF16)

    wspec = pl.BlockSpec((1, tk, tn), lambda i, j, k: (j, k, 0))
    grid = (S // tm, nb, nk)
    x_in, x_out, x_shapes, x_scr, x_ops = _host_args(plan)
    outs = pl.pallas_call(
        _host(body, plan, 3, 2, *_grid_ends(grid)), name=name, grid=grid,
        in_specs=[pl.BlockSpec((tm, tk), lambda i, j, k: (i, k)), wspec, wspec] + x_in,
        out_specs=tuple([pl.BlockSpec((tm, 2 * tn), lambda i, j, k: (i, j)),
                         pl.BlockSpec((tm, tn), lambda i, j, k: (i, j))] + x_out),
        out_shape=tuple([jax.ShapeDtypeStruct((S, 2 * nb * tn), BF16), jax.ShapeDtypeStruct((S, nb * tn), BF16)]
                        + x_shapes),
        scratch_shapes=[pltpu.VMEM((tm, 2 * tn), F32)] + x_scr,
        compiler_params=_cp(_semantics(plan, ("parallel", "parallel", "arbitrary")), 52),
    )(h, g8, u8, *x_ops)
    return outs[0], outs[1], tuple(outs[2:])


def _ffn_dh(dau, g8, u8, *, tm, tn, name, plan=None):
    S = dau.shape[0]
    nb, D, tf = g8.shape
    tm, tn = min(tm, S), min(tn, D)
    nk = 2 * nb
    nt = (((1,), (1,)), ((), ()))

    def body(a_ref, g_ref, u_ref, o_ref, acc_ref):
        k = pl.program_id(2)

        @pl.when(k == 0)
        def _():
            acc_ref[...] = jnp.zeros_like(acc_ref)

        @pl.when(k % 2 == 0)
        def _():
            acc_ref[...] += lax.dot_general(a_ref[...], g_ref[0], nt, preferred_element_type=F32)

        @pl.when(k % 2 == 1)
        def _():
            acc_ref[...] += lax.dot_general(a_ref[...], u_ref[0], nt, preferred_element_type=F32)

        @pl.when(k == nk - 1)
        def _():
            o_ref[...] = acc_ref[...]

    wspec = pl.BlockSpec((1, tn, tf), lambda i, n, k: (k // 2, n, 0))
    grid = (S // tm, D // tn, nk)
    x_in, x_out, x_shapes, x_scr, x_ops = _host_args(plan)
    outs = pl.pallas_call(
        _host(body, plan, 3, 1, *_grid_ends(grid)), name=name, grid=grid,
        in_specs=[pl.BlockSpec((tm, tf), lambda i, n, k: (i, k)), wspec, wspec] + x_in,
        out_specs=tuple([pl.BlockSpec((tm, tn), lambda i, n, k: (i, n))] + x_out),
        out_shape=tuple([jax.ShapeDtypeStruct((S, D), F32)] + x_shapes),
        scratch_shapes=[pltpu.VMEM((tm, tn), F32)] + x_scr,
        compiler_params=_cp(_semantics(plan, ("parallel", "parallel", "arbitrary")), 40),
    )(dau, g8, u8, *x_ops)
    return outs[0], tuple(outs[1:])


def _ffn_dwgu(h, dau, *, tm, tk, name, plan=None):
    S, D = h.shape
    tf = FF_TILE
    nt = dau.shape[1] // tf
    tm, tk = min(tm, D), min(tk, S)
    nk = S // tk
    ni = D // tm

    def body(a_ref, b_ref, o_ref, acc_ref):
        k = pl.program_id(2)

        @pl.when(k == 0)
        def _():
            acc_ref[...] = jnp.zeros_like(acc_ref)

        acc_ref[...] += lax.dot_general(a_ref[...], b_ref[...], (((0,), (0,)), ((), ())), preferred_element_type=F32)

        @pl.when(k == nk - 1)
        def _():
            o_ref[0, 0] = acc_ref[...].astype(BF16)

    grid = (ni, nt, nk)
    x_in, x_out, x_shapes, x_scr, x_ops = _host_args(plan)
    outs = pl.pallas_call(
        _host(body, plan, 2, 1, *_grid_ends(grid)), name=name, grid=grid,
        in_specs=[pl.BlockSpec((tk, tm), lambda i, j, k: (k, i)), pl.BlockSpec((tk, tf), lambda i, j, k: (k, j))] + x_in,
        out_specs=tuple([pl.BlockSpec((1, 1, tm, tf), lambda i, j, k: ((j // 2) % 2, j // 4, (j % 2) * ni + i, 0))]
                        + x_out),
        out_shape=tuple([jax.ShapeDtypeStruct((2, nt // 4, 2 * D, tf), BF16)] + x_shapes),
        scratch_shapes=[pltpu.VMEM((tm, tf), F32)] + x_scr,
        compiler_params=_cp(_semantics(plan, ("parallel", "parallel", "arbitrary")), 40),
    )(h, dau, *x_ops)
    return outs[0], tuple(outs[1:])


def _ffn_dwd(s, df, *, tn, tk, name):
    S, D = df.shape
    tf = FF_TILE
    nb = s.shape[1] // tf
    tn, tk = min(tn, D), min(tk, S)
    nk = S // tk

    def body(a_ref, b_ref, o_ref, acc_ref):
        k = pl.program_id(2)

        @pl.when(k == 0)
        def _():
            acc_ref[...] = jnp.zeros_like(acc_ref)

        acc_ref[...] += lax.dot_general(a_ref[...], b_ref[...], (((0,), (0,)), ((), ())), preferred_element_type=F32)

        @pl.when(k == nk - 1)
        def _():
            o_ref[0, 0] = acc_ref[...].astype(BF16)

    return pl.pallas_call(
        body, name=name, grid=(nb, D // tn, nk),
        in_specs=[pl.BlockSpec((tk, tf), lambda j, n, k: (k, j)), pl.BlockSpec((tk, tn), lambda j, n, k: (k, n))],
        out_specs=pl.BlockSpec((1, 1, tf, tn), lambda j, n, k: (j % 2, j // 2, 0, n)),
        out_shape=jax.ShapeDtypeStruct((2, nb // 2, tf, D), BF16),
        scratch_shapes=[pltpu.VMEM((tf, tn), F32)],
        compiler_params=_cp(("parallel", "parallel", "arbitrary"), 40),
    )(s, df)


def _ffn_down_bwd(df, d8, au, *, tm, tk, name):
    S, D = df.shape
    nb, tn, _ = d8.shape
    F = nb * tn
    tm = min(tm, S)
    tk = min(tk, D)
    nk = D // tk

    def body(df_ref, w_ref, au_ref, dau_ref, acc_ref):
        k = pl.program_id(2)

        @pl.when(k == 0)
        def _():
            acc_ref[...] = jnp.zeros_like(acc_ref)

        acc_ref[...] += lax.dot_general(df_ref[...], w_ref[0], (((1,), (1,)), ((), ())), preferred_element_type=F32)

        @pl.when(k == nk - 1)
        def _():
            ds = acc_ref[...]
            au_v = au_ref[...].astype(F32)
            a = au_v[:, :tn]
            u = au_v[:, tn:]
            sg = jax.nn.sigmoid(a)
            da = ds * u * (sg * (1.0 + a * (1.0 - sg)))
            du = ds * (a * sg)
            dau_ref[:, :tn] = da.astype(BF16)
            dau_ref[:, tn:] = du.astype(BF16)

    return pl.pallas_call(
        body, name=name, grid=(S // tm, F // tn, nk),
        in_specs=[pl.BlockSpec((tm, tk), lambda i, j, k: (i, k)), pl.BlockSpec((1, tn, tk), lambda i, j, k: (j, 0, k)),
                  pl.BlockSpec((tm, 2 * tn), lambda i, j, k: (i, j))],
        out_specs=pl.BlockSpec((tm, 2 * tn), lambda i, j, k: (i, j)),
        out_shape=jax.ShapeDtypeStruct((S, 2 * F), BF16),
        scratch_shapes=[pltpu.VMEM((tm, tn), F32)],
        compiler_params=_cp(("parallel", "parallel", "arbitrary"), 52),
    )(df, d8, au)


def _fold8(x):
    tm, w = x.shape
    return jnp.sum(x.reshape(tm // 8, 8, w), axis=0)


def _row_spec(tm, w):
    return pl.BlockSpec((tm, w), lambda i: (i, 0))


def _vec_spec(rows, w):
    return pl.BlockSpec((rows, w), lambda i: (0, 0))


def _pre_norm_mod(x, gain, shift, scale1p, *, name):
    S, D = x.shape
    tm = ROW_TILE

    def body(x_ref, g_ref, sh_ref, sc_ref, h_ref):
        xv = x_ref[...]
        rinv = lax.rsqrt(jnp.mean(xv * xv, axis=-1, keepdims=True) + NORM_EPS)
        h_ref[...] = ((xv * rinv) * g_ref[...] * sc_ref[...] + sh_ref[...]).astype(BF16)

    return pl.pallas_call(
        body, name=name, grid=(S // tm,),
        in_specs=[_row_spec(tm, D), _vec_spec(1, D), _vec_spec(1, D), _vec_spec(1, D)],
        out_specs=_row_spec(tm, D), out_shape=jax.ShapeDtypeStruct((S, D), BF16),
        compiler_params=_cp(("parallel",), 32),
    )(x, gain, shift, scale1p)


def _pre_norm_mod_bwd(dh, x, dres, gain, scale1p, *, name):
    S, D = x.shape
    tm = ROW_TILE
    n = S // tm

    def body(dh_ref, x_ref, dr_ref, g_ref, sc_ref, dx_ref, sums_ref, acc_ref):
        i = pl.program_id(0)

        @pl.when(i == 0)
        def _():
            acc_ref[...] = jnp.zeros_like(acc_ref)

        xv = x_ref[...]
        dhv = dh_ref[...]
        g = g_ref[...]
        rinv = lax.rsqrt(jnp.mean(xv * xv, axis=-1, keepdims=True) + NORM_EPS)
        xn = xv * rinv
        dn = dhv * sc_ref[...]
        dxn = dn * g
        dx_ref[...] = dr_ref[...] + rinv * (dxn - xn * jnp.mean(dxn * xn, axis=-1, keepdims=True))
        acc_ref[0] += _fold8(dhv)
        acc_ref[1] += _fold8(dhv * (xn * g))
        acc_ref[2] += _fold8(dn * xn)

        @pl.when(i == n - 1)
        def _():
            for q in range(3):
                sums_ref[q:q + 1, :] = jnp.sum(acc_ref[q], axis=0, keepdims=True)

    return pl.pallas_call(
        body, name=name, grid=(n,),
        in_specs=[_row_spec(tm, D), _row_spec(tm, D), _row_spec(tm, D), _vec_spec(1, D), _vec_spec(1, D)],
        out_specs=(_row_spec(tm, D), _vec_spec(3, D)),
        out_shape=(jax.ShapeDtypeStruct((S, D), F32), jax.ShapeDtypeStruct((3, D), F32)),
        scratch_shapes=[pltpu.VMEM((3, 8, D), F32)],
        compiler_params=_cp(("arbitrary",), 40),
    )(dh, x, dres, gain, scale1p)


def _post_norm_res(x, f, gain, gw, *, name):
    S, D = x.shape
    tm = ROW_TILE

    def body(x_ref, f_ref, g_ref, gw_ref, o_ref):
        fv = f_ref[...]
        rinv = lax.rsqrt(jnp.mean(fv * fv, axis=-1, keepdims=True) + NORM_EPS)
        o_ref[...] = x_ref[...] + gw_ref[...] * ((fv * rinv) * g_ref[...])

    return pl.pallas_call(
        body, name=name, grid=(S // tm,),
        in_specs=[_row_spec(tm, D), _row_spec(tm, D), _vec_spec(1, D), _vec_spec(1, D)],
        out_specs=_row_spec(tm, D), out_shape=jax.ShapeDtypeStruct((S, D), F32),
        compiler_params=_cp(("parallel",), 32),
    )(x, f, gain, gw)


def _post_norm_res_bwd(dxo, f, gain, gw, weight, *, name):
    S, D = f.shape
    tm = ROW_TILE
    n = S // tm

    def body(d_ref, f_ref, g_ref, gw_ref, df_ref, sums_ref, acc_ref):
        i = pl.program_id(0)

        @pl.when(i == 0)
        def _():
            acc_ref[...] = jnp.zeros_like(acc_ref)

        fv = f_ref[...]
        dv = d_ref[...]
        g = g_ref[...]
        rinv = lax.rsqrt(jnp.mean(fv * fv, axis=-1, keepdims=True) + NORM_EPS)
        fh = fv * rinv
        dy = dv * gw_ref[...]
        dfh = dy * g
        df_ref[...] = (rinv * (dfh - fh * jnp.mean(dfh * fh, axis=-1, keepdims=True))).astype(BF16)
        acc_ref[0] += _fold8(weight * dv * (fh * g))
        acc_ref[1] += _fold8(dy * fh)

        @pl.when(i == n - 1)
        def _():
            for q in range(2):
                sums_ref[q:q + 1, :] = jnp.sum(acc_ref[q], axis=0, keepdims=True)

    return pl.pallas_call(
        body, name=name, grid=(n,),
        in_specs=[_row_spec(tm, D), _row_spec(tm, D), _vec_spec(1, D), _vec_spec(1, D)],
        out_specs=(_row_spec(tm, D), _vec_spec(2, D)),
        out_shape=(jax.ShapeDtypeStruct((S, D), BF16), jax.ShapeDtypeStruct((2, D), F32)),
        scratch_shapes=[pltpu.VMEM((2, 8, D), F32)],
        compiler_params=_cp(("arbitrary",), 40),
    )(dxo, f, gain, gw)


def _loss_head(y, target, *, name):
    S, D = y.shape
    tm = ROW_TILE

    def body(y_ref, t_ref, l_ref, dy_ref):
        i = pl.program_id(0)

        @pl.when(i == 0)
        def _():
            l_ref[...] = jnp.zeros_like(l_ref)

        err = y_ref[...] - t_ref[...]
        dy_ref[...] = err * (1.0 / D)
        row = jnp.sum(err * err, axis=-1, keepdims=True) * (0.5 / D)
        l_ref[...] += jnp.sum(row, axis=0, keepdims=True)

    return pl.pallas_call(
        body, name=name, grid=(S // tm,),
        in_specs=[_row_spec(tm, D), _row_spec(tm, D)],
        out_specs=(_vec_spec(1, 1), _row_spec(tm, D)),
        out_shape=(jax.ShapeDtypeStruct((1, 1), F32), jax.ShapeDtypeStruct((S, D), F32)),
        compiler_params=_cp(("arbitrary",), 32),
    )(y, target)


def _shift_down(z, j, row):
    return jnp.where(row >= j, pltpu.roll(z, j, 0), 0.0)


def _shift_up(z, j, row, n):
    return jnp.where(row < n - j, pltpu.roll(z, n - j, 0), 0.0)


def _pool_fwd(p, pool_w, pool_scale, *, name):
    S = p.shape[0]
    C = POOL_GROUP

    def body(u_ref, w_ref, sc_ref, o_ref, y_ref):
        g = pl.program_id(0)
        u = u_ref[...]
        row = lax.broadcasted_iota(jnp.int32, (S, C), 0)
        s1 = u + _shift_down(u, 1, row)
        s2 = s1 + _shift_down(s1, 2, row)
        s3 = s2 + _shift_down(s2, 4, row)
        s4 = s3 + _shift_down(s3, 8, row)
        gi = jnp.zeros((S, C), jnp.int32) + g
        win = jnp.where(gi == 0, s1, jnp.where(gi == 1, s2, jnp.where(gi == 2, s3, s4)))
        width = jnp.where(gi == 0, 2, jnp.where(gi == 1, 4, jnp.where(gi == 2, 8, 16)))
        count = jnp.minimum(row + 1, width).astype(F32)
        o = win / count - u
        o_ref[...] = o
        y_ref[...] = jnp.dot(o.astype(BF16), w_ref[0].astype(BF16), preferred_element_type=F32) * sc_ref[...]

    col = pl.BlockSpec((S, C), lambda g: (0, g))
    return pl.pallas_call(
        body, name=name, grid=(POOL_GROUPS,),
        in_specs=[col, pl.BlockSpec((1, C, C), lambda g: (g, 0, 0)), pl.BlockSpec((1, C), lambda g: (0, g))],
        out_specs=(col, col),
        out_shape=(jax.ShapeDtypeStruct((S, POOL_GROUPS * C), F32), jax.ShapeDtypeStruct((S, POOL_GROUPS * C), F32)),
        compiler_params=_cp(("parallel",), 48),
    )(p, pool_w, pool_scale)


def _pool_bwd(dcat, o, pool_w, pool_scale, *, name):
    S = o.shape[0]
    C = POOL_GROUP

    def body(dy_ref, o_ref, w_ref, sc_ref, du_ref, dw_ref, dsc_ref):
        g = pl.program_id(0)
        dy = dy_ref[...]
        ob = o_ref[...].astype(BF16)
        wb = w_ref[0].astype(BF16)
        mixed = jnp.dot(ob, wb, preferred_element_type=F32)
        dsc_ref[...] = jnp.sum(_fold8(dy * mixed), axis=0, keepdims=True)
        dmix = (dy * sc_ref[...]).astype(BF16)
        dw_ref[0] = lax.dot_general(ob, dmix, (((0,), (0,)), ((), ())), preferred_element_type=F32)
        do = lax.dot_general(dmix, wb, (((1,), (1,)), ((), ())), preferred_element_type=F32)
        row = lax.broadcasted_iota(jnp.int32, (S, C), 0)
        gi = jnp.zeros((S, C), jnp.int32) + g
        width = jnp.where(gi == 0, 2, jnp.where(gi == 1, 4, jnp.where(gi == 2, 8, 16)))
        z = do / jnp.minimum(row + 1, width).astype(F32)
        s1 = z + _shift_up(z, 1, row, S)
        s2 = s1 + _shift_up(s1, 2, row, S)
        s3 = s2 + _shift_up(s2, 4, row, S)
        s4 = s3 + _shift_up(s3, 8, row, S)
        win = jnp.where(gi == 0, s1, jnp.where(gi == 1, s2, jnp.where(gi == 2, s3, s4)))
        du_ref[...] = (win - do).astype(BF16)

    col = pl.BlockSpec((S, C), lambda g: (0, g))
    return pl.pallas_call(
        body, name=name, grid=(POOL_GROUPS,),
        in_specs=[col, col, pl.BlockSpec((1, C, C), lambda g: (g, 0, 0)), pl.BlockSpec((1, C), lambda g: (0, g))],
        out_specs=(col, pl.BlockSpec((1, C, C), lambda g: (g, 0, 0)), pl.BlockSpec((1, C), lambda g: (0, g))),
        out_shape=(jax.ShapeDtypeStruct((S, POOL_GROUPS * C), BF16), jax.ShapeDtypeStruct((POOL_GROUPS, C, C), F32),
                   jax.ShapeDtypeStruct((1, POOL_GROUPS * C), F32)),
        compiler_params=_cp(("parallel",), 48),
    )(dcat, o, pool_w, pool_scale)


def _block_ones():
    r = lax.broadcasted_iota(jnp.int32, (128, 128), 0) // HEAD
    c = lax.broadcasted_iota(jnp.int32, (128, 128), 1) // HEAD
    return jnp.where(r == c, 1.0, 0.0).astype(BF16)


def _segsum(x, bd):
    outs = []
    for j in range(x.shape[1] // 128):
        xs = x[:, j * 128:(j + 1) * 128]
        hi = xs.astype(BF16)
        lo = (xs - hi.astype(F32)).astype(BF16)
        outs.append(jnp.dot(hi, bd, preferred_element_type=F32) + jnp.dot(lo, bd, preferred_element_type=F32))
    return jnp.concatenate(outs, axis=1)


def _prep_common(q, qprev, first, mu, wl, w0, a0, kkw, kaw, R):
    tm = q.shape[0]
    row = lax.broadcasted_iota(jnp.int32, q.shape, 0)
    last = qprev[7:8, :] * first
    prev = jnp.where(row == 0, last, pltpu.roll(q, 1, 0))
    ps = q + mu * (prev - q)
    r = ps[:, 0:R]
    k = ps[:, R:2 * R]
    v = ps[:, 2 * R:3 * R]
    lo_in = ps[:, 3 * R:3 * R + LORA_PAD]
    lane = lax.broadcasted_iota(jnp.int32, (tm, LORA_PAD), 1)
    m_w = lane < LORA_W
    m_a = lane < LORA_W + LORA_A
    m_g = lane < LORA_W + LORA_A + LORA_G
    act = jnp.where(m_w, jnp.tanh(lo_in), jnp.where(m_a, lo_in, jnp.where(m_g, jax.nn.sigmoid(lo_in), 0.0)))
    lo = jnp.dot(act.astype(BF16), wl, preferred_element_type=F32)
    wpre = w0 + lo[:, 0:R]
    apre = a0 + lo[:, R:2 * R]
    g = lo[:, 2 * R:3 * R]
    neg = -wpre
    softplus = jnp.maximum(neg, 0.0) + jnp.log(1.0 + jnp.exp(-jnp.abs(neg)))
    wlog = -softplus - 0.5
    ew = jnp.exp(wlog)
    decay = jnp.exp(-ew)
    a = jax.nn.sigmoid(apre)
    kk = k * kkw
    bd = _block_ones()
    n2 = _segsum(kk * kk, bd)
    nrm = jnp.maximum(jnp.sqrt(n2), 1e-12)
    kap = kk / nrm
    kmul = 1.0 + (a - 1.0) * kaw
    k2 = k * kmul
    return dict(prev=prev, r=r, k=k, v=v, act=act, m_w=m_w, m_a=m_a, m_g=m_g, wpre=wpre, g=g, ew=ew, decay=decay,
                a=a, n2=n2, nrm=nrm, kap=kap, kmul=kmul, k2=k2, bd=bd)


def _prev_rows_spec(tm, w):
    return pl.BlockSpec((8, w), lambda i: (jnp.maximum(i * (tm // 8) - 1, 0), 0))


def _rwkv_prep(q, mu, wl, w0, a0, kkw, kaw, *, name):
    S, QW = q.shape
    R = w0.shape[1]
    tm = ROW_TILE // 2

    def body(q_ref, qp_ref, mu_ref, wl_ref, w0_ref, a0_ref, kk_ref, ka_ref, r_ref, w_ref, k_ref, v_ref, kap_ref,
             b_ref, g_ref):
        first = jnp.where(pl.program_id(0) > 0, 1.0, 0.0)
        t = _prep_common(q_ref[...], qp_ref[...], first, mu_ref[...], wl_ref[...], w0_ref[...], a0_ref[...],
                         kk_ref[...], ka_ref[...], R)
        r_ref[...] = t["r"]
        w_ref[...] = t["decay"]
        k_ref[...] = t["k2"]
        v_ref[...] = t["v"]
        kap_ref[...] = t["kap"]
        b_ref[...] = t["kap"] * t["a"]
        g_ref[...] = t["g"]

    vec = _vec_spec(1, R)
    return pl.pallas_call(
        body, name=name, grid=(S // tm,),
        in_specs=[_row_spec(tm, QW), _prev_rows_spec(tm, QW), _vec_spec(1, QW), _vec_spec(LORA_PAD, 3 * R), vec, vec,
                  vec, vec],
        out_specs=tuple([_row_spec(tm, R)] * 7),
        out_shape=tuple([jax.ShapeDtypeStruct((S, R), F32)] * 7),
        compiler_params=_cp(("parallel",), 48),
    )(q, q, mu, wl, w0, a0, kkw, kaw)


def _rwkv_prep_bwd(q, mu, wl, w0, a0, kkw, kaw, grads, *, name):
    S, QW = q.shape
    R = w0.shape[1]
    tm = ROW_TILE // 2
    n = S // tm

    def body(q_ref, qp_ref, mu_ref, wl_ref, w0_ref, a0_ref, kk_ref, ka_ref, dr_ref, dw_ref, dk2_ref, dv_ref, dkap_ref,
             db_ref, dg_ref, drb_ref, dk2b_ref, dvb_ref, dps_ref, dwl_ref, sums_ref, acc_ref):
        i = pl.program_id(0)

        @pl.when(i == 0)
        def _():
            acc_ref[...] = jnp.zeros_like(acc_ref)
            dwl_ref[...] = jnp.zeros_like(dwl_ref)

        first = jnp.where(i > 0, 1.0, 0.0)
        wl = wl_ref[...]
        kkw = kk_ref[...]
        kaw = ka_ref[...]
        t = _prep_common(q_ref[...], qp_ref[...], first, mu_ref[...], wl, w0_ref[...], a0_ref[...], kkw, kaw, R)
        a, kap, k, act = t["a"], t["kap"], t["k"], t["act"]
        db = db_ref[...]
        dk2 = dk2_ref[...] + dk2b_ref[...]
        dkap = dkap_ref[...] + db * a
        da = db * kap + dk2 * k * kaw
        dk = dk2 * t["kmul"]
        proj = jnp.where(jnp.sqrt(t["n2"]) > 1e-12, _segsum(kap * dkap, t["bd"]), 0.0)
        dkk = (dkap - kap * proj) / t["nrm"]
        dk = dk + dkk * kkw
        dapre = da * a * (1.0 - a)
        dwlog = dw_ref[...] * t["decay"] * (-t["ew"])
        dwpre = dwlog * jax.nn.sigmoid(-t["wpre"])
        acc_ref[0] += _fold8(dwpre)
        acc_ref[1] += _fold8(dapre)
        acc_ref[2] += _fold8(dkk * k)
        acc_ref[3] += _fold8(dk2 * k * (a - 1.0))
        dlo = jnp.concatenate([dwpre, dapre, dg_ref[...]], axis=1).astype(BF16)
        dwl_ref[...] += lax.dot_general(act.astype(BF16), dlo, (((0,), (0,)), ((), ())), preferred_element_type=F32)
        dact = lax.dot_general(dlo, wl, (((1,), (1,)), ((), ())), preferred_element_type=F32)
        dlin = jnp.where(t["m_w"], dact * (1.0 - act * act),
                         jnp.where(t["m_a"], dact, jnp.where(t["m_g"], dact * act * (1.0 - act), 0.0)))
        dps_ref[:, 0:R] = dr_ref[...] + drb_ref[...]
        dps_ref[:, R:2 * R] = dk
        dps_ref[:, 2 * R:3 * R] = dv_ref[...] + dvb_ref[...]
        dps_ref[:, 3 * R:3 * R + LORA_PAD] = dlin
        dps_ref[:, 3 * R + LORA_PAD:] = jnp.zeros((tm, QW - 3 * R - LORA_PAD), F32)

        @pl.when(i == n - 1)
        def _():
            for j in range(4):
                sums_ref[j:j + 1, :] = jnp.sum(acc_ref[j], axis=0, keepdims=True)

    vec = _vec_spec(1, R)
    return pl.pallas_call(
        body, name=name, grid=(n,),
        in_specs=[_row_spec(tm, QW), _prev_rows_spec(tm, QW), _vec_spec(1, QW), _vec_spec(LORA_PAD, 3 * R), vec, vec,
                  vec, vec] + [_row_spec(tm, R)] * 10,
        out_specs=(_row_spec(tm, QW), _vec_spec(LORA_PAD, 3 * R), _vec_spec(4, R)),
        out_shape=(jax.ShapeDtypeStruct((S, QW), F32), jax.ShapeDtypeStruct((LORA_PAD, 3 * R), F32),
                   jax.ShapeDtypeStruct((4, R), F32)),
        scratch_shapes=[pltpu.VMEM((4, 8, R), F32)],
        compiler_params=_cp(("arbitrary",), 56),
    )(q, q, mu, wl, w0, a0, kkw, kaw, *grads)


def _tshift_bwd(dps, q, mu, *, name):
    S, QW = q.shape
    tm = ROW_TILE // 2
    n = S // tm

    def body(d_ref, dn_ref, q_ref, qp_ref, mu_ref, dq_ref, dmu_ref, acc_ref):
        i = pl.program_id(0)

        @pl.when(i == 0)
        def _():
            acc_ref[...] = jnp.zeros_like(acc_ref)

        mu = mu_ref[...]
        d = d_ref[...]
        qv = q_ref[...]
        row = lax.broadcasted_iota(jnp.int32, d.shape, 0)
        first = jnp.where(i > 0, 1.0, 0.0)
        notlast = jnp.where(i < n - 1, 1.0, 0.0)
        prev = jnp.where(row == 0, qp_ref[7:8, :] * first, pltpu.roll(qv, 1, 0))
        z = d * mu
        nxt = jnp.where(row == tm - 1, dn_ref[0:1, :] * mu * notlast, pltpu.roll(z, tm - 1, 0))
        dq_ref[...] = (d * (1.0 - mu) + nxt).astype(BF16)
        acc_ref[...] += _fold8(d * (prev - qv))

        @pl.when(i == n - 1)
        def _():
            dmu_ref[...] = jnp.sum(acc_ref[...], axis=0, keepdims=True)

    nblk8 = S // 8
    next_spec = pl.BlockSpec((8, QW), lambda i: (jnp.minimum((i + 1) * (tm // 8), nblk8 - 1), 0))
    return pl.pallas_call(
        body, name=name, grid=(n,),
        in_specs=[_row_spec(tm, QW), next_spec, _row_spec(tm, QW), _prev_rows_spec(tm, QW), _vec_spec(1, QW)],
        out_specs=(_row_spec(tm, QW), _vec_spec(1, QW)),
        out_shape=(jax.ShapeDtypeStruct((S, QW), BF16), jax.ShapeDtypeStruct((1, QW), F32)),
        scratch_shapes=[pltpu.VMEM((8, QW), F32)],
        compiler_params=_cp(("arbitrary",), 48),
    )(dps, dps, q, q, mu)


def _post_common(ysc, r, k2, v, lnw, lnb, rk):
    bd = _block_ones()
    mean = _segsum(ysc, bd) * (1.0 / HEAD)
    d = ysc - mean
    var = _segsum(d * d, bd) * (1.0 / HEAD)
    rstd = lax.rsqrt(var + LN_X_EPS)
    yh = d * rstd
    rkk = _segsum(r * k2 * rk, bd)
    z = yh * lnw + lnb + rkk * v
    return bd, rstd, yh, rkk, z


def _rwkv_post(ysc, r, k2, v, g, ypool, lnw, lnb, rk, *, name):
    S, R = ysc.shape
    PW = ypool.shape[1]
    tm = ROW_TILE

    def body(y_ref, r_ref, k_ref, v_ref, g_ref, yp_ref, lw_ref, lb_ref, rk_ref, cat_ref):
        _, _, _, _, z = _post_common(y_ref[...], r_ref[...], k_ref[...], v_ref[...], lw_ref[...], lb_ref[...],
                                     rk_ref[...])
        cat_ref[:, 0:PW] = yp_ref[...].astype(BF16)
        cat_ref[:, PW:] = (z * g_ref[...]).astype(BF16)

    vec = _vec_spec(1, R)
    return pl.pallas_call(
        body, name=name, grid=(S // tm,),
        in_specs=[_row_spec(tm, R)] * 5 + [_row_spec(tm, PW), vec, vec, vec],
        out_specs=_row_spec(tm, PW + R), out_shape=jax.ShapeDtypeStruct((S, PW + R), BF16),
        compiler_params=_cp(("parallel",), 48),
    )(ysc, r, k2, v, g, ypool, lnw, lnb, rk)


def _rwkv_post_bwd(dcat, ysc, r, k2, v, g, lnw, lnb, rk, *, name):
    S, R = ysc.shape
    tm = ROW_TILE
    n = S // tm

    def body(d_ref, y_ref, r_ref, k_ref, v_ref, g_ref, lw_ref, lb_ref, rk_ref, dy_ref, dg_ref, drb_ref, dkb_ref,
             dvb_ref, sums_ref, acc_ref):
        i = pl.program_id(0)

        @pl.when(i == 0)
        def _():
            acc_ref[...] = jnp.zeros_like(acc_ref)

        rv, kv, vv, lw, rkw = r_ref[...], k_ref[...], v_ref[...], lw_ref[...], rk_ref[...]
        bd, rstd, yh, rkk, z = _post_common(y_ref[...], rv, kv, vv, lw, lb_ref[...], rkw)
        dyr = d_ref[...]
        dg_ref[...] = dyr * z
        dz = dyr * g_ref[...]
        dyh = dz * lw
        dy_ref[...] = rstd * (dyh - _segsum(dyh, bd) * (1.0 / HEAD) - yh * (_segsum(dyh * yh, bd) * (1.0 / HEAD)))
        dvb_ref[...] = dz * rkk
        drkk = _segsum(dz * vv, bd)
        drb_ref[...] = drkk * kv * rkw
        dkb_ref[...] = drkk * rv * rkw
        acc_ref[0] += _fold8(dz * yh)
        acc_ref[1] += _fold8(dz)
        acc_ref[2] += _fold8(drkk * rv * kv)

        @pl.when(i == n - 1)
        def _():
            for j in range(3):
                sums_ref[j:j + 1, :] = jnp.sum(acc_ref[j], axis=0, keepdims=True)

    vec = _vec_spec(1, R)
    dspec = _row_spec(tm, R)
    return pl.pallas_call(
        body, name=name, grid=(n,),
        in_specs=[dspec] + [_row_spec(tm, R)] * 5 + [vec, vec, vec],
        out_specs=tuple([_row_spec(tm, R)] * 5) + (_vec_spec(3, R),),
        out_shape=tuple([jax.ShapeDtypeStruct((S, R), F32)] * 5) + (jax.ShapeDtypeStruct((3, R), F32),),
        scratch_shapes=[pltpu.VMEM((3, 8, R), F32)],
        compiler_params=_cp(("arbitrary",), 56),
    )(dcat, ysc, r, k2, v, g, lnw, lnb, rk)


def _half_sums(x, m_a):
    s_a = jnp.sum(jnp.where(m_a, x, 0.0), axis=1, keepdims=True)
    s_b = jnp.sum(jnp.where(m_a, 0.0, x), axis=1, keepdims=True)
    return s_a, s_b


SEL_ROWS = 64


def _column_selector():
    row = lax.broadcasted_iota(jnp.int32, (SEL_ROWS, 8 * 128), 0)
    col = lax.broadcasted_iota(jnp.int32, (SEL_ROWS, 8 * 128), 1)
    head, rest = row // 32, row % 32
    hit = (rest < 24) & (rest % 8 == col // 128) & (head == (col % 128) // HEAD)
    return jnp.where(hit, 1.0, 0.0).astype(BF16)


def _expand_columns(x, sel):
    hi = x.astype(BF16).astype(F32)
    r1 = x - hi
    mid = r1.astype(BF16).astype(F32)
    lo = (r1 - mid).astype(BF16).astype(F32)
    terms = jnp.concatenate([hi, mid, lo, jnp.zeros_like(x)], axis=0)
    both = jnp.concatenate([terms, pltpu.roll(terms, HEAD, 1)], axis=0)[:, 0:HEAD]
    return lax.dot_general(both.astype(BF16), sel, (((0,), (0,)), ((), ())), preferred_element_type=F32)


def _scan_fwd(r, w, k, v, kap, b, *, name, plan=None):
    S, R = r.shape
    G, T = SCAN_G, SCAN_T
    NP = R // 128
    assert NP % G == 0 and S % T == 0
    GW = 128 * G

    def body(r_ref, w_ref, k_ref, v_ref, kap_ref, b_ref, sel_ref, y_ref, sa_ref, st_ref, s_scr, vc_scr, yt_scr,
             sat_scr):
        c = pl.program_id(1)

        @pl.when(c == 0)
        def _():
            s_scr[...] = jnp.zeros_like(s_scr)

        yt_scr[...] = jnp.zeros_like(yt_scr)
        sat_scr[...] = jnp.zeros_like(sat_scr)
        lane = lax.broadcasted_iota(jnp.int32, (HEAD, 128), 1)
        m_a = lane < HEAD

        def block(tb, carry):
            t0 = pl.multiple_of(tb * 8, 8)
            rb, wb, kb = r_ref[pl.ds(t0, 8), :], w_ref[pl.ds(t0, 8), :], k_ref[pl.ds(t0, 8), :]
            pb, bb, vb = kap_ref[pl.ds(t0, 8), :], b_ref[pl.ds(t0, 8), :], v_ref[pl.ds(t0, 8), :]
            for g in range(G):
                vc_scr[g] = _expand_columns(vb[:, g * 128:(g + 1) * 128], sel_ref[...])

            def put_y(g, parts, hot_y):
                yt_scr[g, 0:HEAD, :] = jnp.where(hot_y, parts[0], yt_scr[g, 0:HEAD, :])
                yt_scr[g, HEAD:, :] = jnp.where(hot_y, parts[1], yt_scr[g, HEAD:, :])

            def put_sa(g, parts, hot_t):
                sat_scr[g, 0:HEAD, :] = jnp.where(hot_t, -parts[0], sat_scr[g, 0:HEAD, :])
                sat_scr[g, HEAD:, :] = jnp.where(hot_t, -parts[1], sat_scr[g, HEAD:, :])

            for j in range(8):
                t = t0 + j
                cols = slice(j * 128, (j + 1) * 128)
                sa_parts, y_parts = [], []
                for g in range(G):
                    sl = slice(g * 128, (g + 1) * 128)
                    sa_parts.append(_half_sums(s_scr[g] * pb[j:j + 1, sl], m_a))
                if j > 0:
                    for g in range(G):
                        sl = slice(g * 128, (g + 1) * 128)
                        y_parts.append(_half_sums(s_scr[g] * rb[j - 1:j, sl], m_a))
                hot_t = lane == t
                for g in range(G):
                    sl = slice(g * 128, (g + 1) * 128)
                    sa = -jnp.where(m_a, sa_parts[g][0], sa_parts[g][1])
                    st = s_scr[g] * wb[j:j + 1, sl] + sa * bb[j:j + 1, sl] + vc_scr[g, :, cols] * kb[j:j + 1, sl]
                    s_scr[g] = st
                    st_ref[g, t] = st
                    put_sa(g, sa_parts[g], hot_t)
                if j > 0:
                    hot_y = lane == t - 1
                    for g in range(G):
                        put_y(g, y_parts[g], hot_y)
            hot_y = lane == t0 + 7
            for g in range(G):
                sl = slice(g * 128, (g + 1) * 128)
                put_y(g, _half_sums(s_scr[g] * rb[7:8, sl], m_a), hot_y)
            return carry

        lax.fori_loop(0, T // 8, block, 0)
        for g in range(G):
            y_ref[:, g * 128:(g + 1) * 128] = yt_scr[g].T[0:T, :]
            sa_ref[:, g * 128:(g + 1) * 128] = sat_scr[g].T[0:T, :]

    tspec = pl.BlockSpec((T, GW), lambda p, c: (c, p))
    sel_spec = pl.BlockSpec((SEL_ROWS, 8 * 128), lambda p, c: (0, 0))
    grid = (NP // G, S // T)
    x_in, x_out, x_shapes, x_scr, x_ops = _host_args(plan)
    outs = pl.pallas_call(
        _host(body, plan, 7, 3, *_grid_ends(grid)), name=name, grid=grid,
        in_specs=[tspec] * 6 + [sel_spec] + x_in,
        out_specs=tuple([tspec, tspec, pl.BlockSpec((G, T, HEAD, 128), lambda p, c: (p, c, 0, 0))] + x_out),
        out_shape=tuple([jax.ShapeDtypeStruct((S, R), F32), jax.ShapeDtypeStruct((S, R), F32),
                         jax.ShapeDtypeStruct((NP, S, HEAD, 128), F32)] + x_shapes),
        scratch_shapes=[pltpu.VMEM((G, HEAD, 128), F32), pltpu.VMEM((G, HEAD, 8 * 128), F32),
                        pltpu.VMEM((G, 128, 128), F32), pltpu.VMEM((G, 128, 128), F32)] + x_scr,
        compiler_params=_cp(_semantics(plan, ("parallel", "arbitrary")), 48),
    )(r, w, k, v, kap, b, _column_selector(), *x_ops)
    return outs[0], outs[1], outs[2], tuple(outs[3:])


def _scan_bwd(r, w, k, v, kap, b, sa, dy, states, *, name):
    S, R = r.shape
    G, T = SCAN_G, SCAN_T
    NP = R // 128
    NC = S // T
    GW = 128 * G

    def body(r_ref, w_ref, k_ref, v_ref, kap_ref, b_ref, sa_ref, dy_ref, st_ref, sp_ref, sel_ref, dr_ref, dw_ref,
             dk_ref, dv_ref, dkap_ref, db_ref, ds_scr, vc_scr, dyc_scr, sac_scr, dvt_scr):
        ci = pl.program_id(1)

        @pl.when(ci == 0)
        def _():
            ds_scr[...] = jnp.zeros_like(ds_scr)

        dvt_scr[...] = jnp.zeros_like(dvt_scr)
        lane = lax.broadcasted_iota(jnp.int32, (HEAD, 128), 1)
        m_a = lane < HEAD
        sub = lax.broadcasted_iota(jnp.int32, (8, 128), 0)
        zero_i = jnp.zeros((HEAD, 128), jnp.int32)
        has_prev = jnp.where(ci < NC - 1, 1.0, 0.0)

        def state_before(g, t):
            at_start = (zero_i + t) == 0
            return jnp.where(at_start, sp_ref[g, 0] * has_prev, st_ref[g, jnp.maximum(t - 1, 0)])

        def block(it, carry):
            tb = T // 8 - 1 - it
            t0 = pl.multiple_of(tb * 8, 8)
            rb, wb, kb = r_ref[pl.ds(t0, 8), :], w_ref[pl.ds(t0, 8), :], k_ref[pl.ds(t0, 8), :]
            pb, bb = kap_ref[pl.ds(t0, 8), :], b_ref[pl.ds(t0, 8), :]
            vb, dyb, sab = v_ref[pl.ds(t0, 8), :], dy_ref[pl.ds(t0, 8), :], sa_ref[pl.ds(t0, 8), :]
            for g in range(G):
                sl = slice(g * 128, (g + 1) * 128)
                vc_scr[g] = _expand_columns(vb[:, sl], sel_ref[...])
                dyc_scr[g] = _expand_columns(dyb[:, sl], sel_ref[...])
                sac_scr[g] = _expand_columns(sab[:, sl], sel_ref[...])
            outs = [[jnp.zeros((8, 128), F32) for _ in range(5)] for _ in range(G)]
            for j in range(7, -1, -1):
                t = t0 + j
                hot = lane == t
                cols = slice(j * 128, (j + 1) * 128)
                dsa_parts, dv_parts = [], []
                for g in range(G):
                    sl = slice(g * 128, (g + 1) * 128)
                    ds = ds_scr[g] + dyc_scr[g, :, cols] * rb[j:j + 1, sl]
                    ds_scr[g] = ds
                    dsa_parts.append(_half_sums(ds * bb[j:j + 1, sl], m_a))
                for g in range(G):
                    sl = slice(g * 128, (g + 1) * 128)
                    dv_parts.append(_half_sums(ds_scr[g] * kb[j:j + 1, sl], m_a))
                ds_parts = list(zip(dv_parts, dsa_parts))
                for g in range(G):
                    sl = slice(g * 128, (g + 1) * 128)
                    w_r, p_r = wb[j:j + 1, sl], pb[j:j + 1, sl]
                    ds = ds_scr[g]
                    s_p = state_before(g, t)
                    dr_row = jnp.sum(st_ref[g, t] * dyc_scr[g, :, cols], axis=0, keepdims=True)
                    dk_row = jnp.sum(ds * vc_scr[g, :, cols], axis=0, keepdims=True)
                    db_row = jnp.sum(ds * sac_scr[g, :, cols], axis=0, keepdims=True)
                    dw_row = jnp.sum(ds * s_p, axis=0, keepdims=True)
                    (dv_a, dv_b), (dsa_a, dsa_b) = ds_parts[g]
                    dvt_scr[g, 0:HEAD, :] = jnp.where(hot, dv_a, dvt_scr[g, 0:HEAD, :])
                    dvt_scr[g, HEAD:, :] = jnp.where(hot, dv_b, dvt_scr[g, HEAD:, :])
                    dsa = jnp.where(m_a, dsa_a, dsa_b)
                    dkap_row = -jnp.sum(s_p * dsa, axis=0, keepdims=True)
                    ds_scr[g] = ds * w_r - dsa * p_r
                    pick = sub == j
                    for q, row in enumerate((dr_row, dw_row, dk_row, dkap_row, db_row)):
                        outs[g][q] = jnp.where(pick, row, outs[g][q])
            for g in range(G):
                sl = slice(g * 128, (g + 1) * 128)
                for q, ref in enumerate((dr_ref, dw_ref, dk_ref, dkap_ref, db_ref)):
                    ref[pl.ds(t0, 8), sl] = outs[g][q]
            return carry

        lax.fori_loop(0, T // 8, block, 0)
        for g in range(G):
            dv_ref[:, g * 128:(g + 1) * 128] = dvt_scr[g].T[0:T, :]

    tspec = pl.BlockSpec((T, GW), lambda p, c: (NC - 1 - c, p))
    st_spec = pl.BlockSpec((G, T, HEAD, 128), lambda p, c: (p, NC - 1 - c, 0, 0))
    prev_spec = pl.BlockSpec((G, 1, HEAD, 128), lambda p, c: (p, jnp.maximum((NC - 1 - c) * T - 1, 0), 0, 0))
    sel_spec = pl.BlockSpec((SEL_ROWS, 8 * 128), lambda p, c: (0, 0))
    return pl.pallas_call(
        body, name=name, grid=(NP // G, NC),
        in_specs=[tspec] * 8 + [st_spec, prev_spec, sel_spec],
        out_specs=tuple([tspec] * 6),
        out_shape=tuple([jax.ShapeDtypeStruct((S, R), F32)] * 6),
        scratch_shapes=[pltpu.VMEM((G, HEAD, 128), F32), pltpu.VMEM((G, HEAD, 8 * 128), F32),
                        pltpu.VMEM((G, HEAD, 8 * 128), F32), pltpu.VMEM((G, HEAD, 8 * 128), F32),
                        pltpu.VMEM((G, 128, 128), F32)],
        compiler_params=_cp(("parallel", "arbitrary"), 48),
    )(r, w, k, v, kap, b, sa, dy, states, states, _column_selector())


def _sum_parts(parts, *, name):
    P, rows, W = parts.shape
    tr = rows
    for cand in (1024, 512, 256, 128, 64, 32, 16, 8):
        if rows % cand == 0:
            tr = cand
            break

    def body(p_ref, o_ref):
        acc = p_ref[0]
        for s in range(1, P):
            acc = acc + p_ref[s]
        o_ref[...] = acc

    return pl.pallas_call(
        body, name=name, grid=(rows // tr,),
        in_specs=[pl.BlockSpec((P, tr, W), lambda i: (0, i, 0))],
        out_specs=pl.BlockSpec((tr, W), lambda i: (i, 0)), out_shape=jax.ShapeDtypeStruct((rows, W), F32),
        compiler_params=_cp(("parallel",), 32),
    )(parts)


def _adamw(w, m, v, parts, *, name):
    R, C = w.shape
    P = parts.shape[0]
    tr = R
    for cand in (1024, 512, 256, 128, 64, 32, 16, 8):
        if R % cand == 0 and cand * C * 4 * (7 + P) <= 10 * 1024 * 1024:
            tr = cand
            break
    bc1 = 1.0 - ADAM_B1 ** ADAM_STEP
    bc2 = 1.0 - ADAM_B2 ** ADAM_STEP

    def body(w_ref, m_ref, v_ref, p_ref, g_ref, d_ref, nm_ref, nv_ref):
        g = p_ref[0].astype(F32)
        for s in range(1, P):
            g = g + p_ref[s].astype(F32)
        m1 = ADAM_B1 * m_ref[...] + (1.0 - ADAM_B1) * g
        v1 = ADAM_B2 * v_ref[...] + (1.0 - ADAM_B2) * (g * g)
        m_hat = m1 / bc1
        v_hat = v1 / bc2
        g_ref[...] = g
        d_ref[...] = -ADAM_LR * (m_hat / (jnp.sqrt(v_hat) + ADAM_EPS) + ADAM_WD * w_ref[...])
        nm_ref[...] = m1
        nv_ref[...] = v1

    spec = pl.BlockSpec((tr, C), lambda i: (i, 0))
    return pl.pallas_call(
        body, name=name, grid=(R // tr,),
        in_specs=[spec, spec, spec, pl.BlockSpec((P, tr, C), lambda i: (0, i, 0))],
        out_specs=(spec, spec, spec, spec), out_shape=tuple([jax.ShapeDtypeStruct((R, C), F32)] * 4),
        compiler_params=_cp(("parallel",), 40),
    )(w, m, v, parts)


def _cols_full(g8):
    n, rows, c = g8.shape
    return jnp.transpose(g8, (1, 0, 2)).reshape(rows, n * c)


def _cols_split(full):
    rows, cols = full.shape
    return jnp.transpose(full.reshape(rows, N_DEV, cols // N_DEV), (1, 0, 2))


def _pack(vals, rows_multiple=512):
    flat = jnp.concatenate([v.reshape(-1).astype(F32) for v in vals])
    n = flat.shape[0]
    unit = 128 * rows_multiple
    padded = ((n + unit - 1) // unit) * unit
    return jnp.pad(flat, (0, padded - n)).reshape(padded // 128, 128)


def _unpack(packed, shapes):
    flat = packed.reshape(-1)
    out, off = [], 0
    for shp in shapes:
        size = 1
        for d in shp:
            size *= d
        out.append(flat[off:off + size].reshape(shp))
        off += size
    return out


def _ffn_forward(x, weights, gpre, gpost, shift, scale1p, gw, tag, up_plan=None):
    g8, u8, d8 = weights
    h = _pre_norm_mod(x, gpre, shift, scale1p, name=f"{tag}_pre")
    au, s, carried = _ffn_up(h, g8, u8, tm=1024, tk=2048, plan=up_plan,
                             name=f"{tag}_up" + ("_carry" if up_plan else ""))
    f = _mm(s, d8.reshape(-1, d8.shape[2]), tm=1024, tn=1024, tk=2048, name=f"{tag}_down")
    xo = _post_norm_res(x, f, gpost, gw, name=f"{tag}_post")
    return xo, (h, au, s, f), carried


def _ffn_backward(dxo, x, saved, weights, gpre, gpost, scale1p, gw, tag, dwgu_plan=None, dh_plan=None):
    g8, u8, d8 = weights
    h, au, s, f = saved
    df, post_sums = _post_norm_res_bwd(dxo, f, gpost, gw, MACARON, name=f"{tag}_post_bwd")
    dwd = _ffn_dwd(s, df, tn=1024, tk=1024, name=f"{tag}_dwd")
    dau = _ffn_down_bwd(df, d8, au, tm=1024, tk=2048, name=f"{tag}_down_bwd")
    dwgu, carried_a = _ffn_dwgu(h, dau, tm=1024, tk=1024, plan=dwgu_plan,
                                name=f"{tag}_dwgu" + ("_carry" if dwgu_plan else ""))
    dh, carried_b = _ffn_dh(dau, g8, u8, tm=1024, tn=1024, plan=dh_plan,
                            name=f"{tag}_dh" + ("_carry" if dh_plan else ""))
    dx, pre_sums = _pre_norm_mod_bwd(dh, x, dxo, gpre, scale1p, name=f"{tag}_pre_bwd")
    return dx, dwgu, dwd, pre_sums, post_sums, carried_a, carried_b


def kernel(x, c, w_ada, b_ada, norm_pre, norm_post, ffn1_w_gate, ffn1_w_up, ffn1_w_down, w_in, mu_shift, pool_w, pool_scale, w0, w2, a0, a2, g2, k_k, k_a, r_k, lnx_w, lnx_b, w_out, ffn2_w_gate, ffn2_w_up, ffn2_w_down, loss_target, m_w_ada, m_b_ada, m_norm_pre, m_norm_post, m_ffn1_w_gate, m_ffn1_w_up, m_ffn1_w_down, m_w_in, m_mu_shift, m_pool_w, m_pool_scale, m_w0, m_w2, m_a0, m_a2, m_g2, m_k_k, m_k_a, m_r_k, m_lnx_w, m_lnx_b, m_w_out, m_ffn2_w_gate, m_ffn2_w_up, m_ffn2_w_down, v_w_ada, v_b_ada, v_norm_pre, v_norm_post, v_ffn1_w_gate, v_ffn1_w_up, v_ffn1_w_down, v_w_in, v_mu_shift, v_pool_w, v_pool_scale, v_w0, v_w2, v_a0, v_a2, v_g2, v_k_k, v_k_a, v_r_k, v_lnx_w, v_lnx_b, v_w_out, v_ffn2_w_gate, v_ffn2_w_up, v_ffn2_w_down):
    names = ["w_ada", "b_ada", "norm_pre", "norm_post", "ffn1_w_gate", "ffn1_w_up", "ffn1_w_down", "w_in", "mu_shift",
             "pool_w", "pool_scale", "w0", "w2", "a0", "a2", "g2", "k_k", "k_a", "r_k", "lnx_w", "lnx_b", "w_out",
             "ffn2_w_gate", "ffn2_w_up", "ffn2_w_down"]
    env = dict(locals())
    W = {n: env[n][0] for n in names}
    M1 = {n: env["m_" + n][0] for n in names}
    V1 = {n: env["v_" + n][0] for n in names}

    me = _my_index()
    xs = x[0]
    tgt = loss_target[0]
    S, D = xs.shape
    F = W["ffn1_w_gate"].shape[1] * N_DEV
    R = W["w0"].shape[0]
    PW = D - R
    IN_W = W["w_in"].shape[1] * N_DEV
    P_W = F
    QW = P_W - PW
    NMOD = 9 * D
    ada_c = W["w_ada"].shape[1]

    c_all, npre8, npost8, w2_8, a2_8, g2_8 = _exchange(
        [c, W["norm_pre"], W["norm_post"], W["w2"].astype(BF16), W["a2"].astype(BF16), W["g2"].astype(BF16)],
        scatter=False, name="gather_small")
    c_all = c_all.reshape(N_DEV, D)
    gpre = _cols_full(npre8)
    gpost = _cols_full(npost8)
    wl = jnp.zeros((LORA_PAD, 3 * R), BF16)
    wl = wl.at[0:LORA_W, 0:R].set(_cols_full(w2_8))
    wl = wl.at[LORA_W:LORA_W + LORA_A, R:2 * R].set(_cols_full(a2_8))
    wl = wl.at[LORA_W + LORA_A:LORA_W + LORA_A + LORA_G, 2 * R:3 * R].set(_cols_full(g2_8))

    sc_all = jax.nn.silu(c_all)
    sc_pad = jnp.concatenate([sc_all, jnp.zeros((8, D), F32)], axis=0).astype(BF16)
    modcols = _mm(sc_pad, W["w_ada"], tm=16, tn=ada_c, tk=256, name="ada_fwd")[0:N_DEV]
    modcols = modcols + lax.dynamic_slice(W["b_ada"], (me * ada_c,), (ada_c,))[None, :]
    (mod8,) = _exchange([modcols], scatter=False, name="gather_mod")
    mod = lax.dynamic_index_in_dim(mod8, me, axis=1, keepdims=False).reshape(9, D)

    def mod_row(i):
        return mod[i:i + 1, :]

    f_pad = FF_TILE - F // N_DEV

    def ffn_shards(tag):
        return [jnp.pad(W[f"{tag}_w_gate"].astype(BF16), ((0, 0), (0, f_pad))),
                jnp.pad(W[f"{tag}_w_up"].astype(BF16), ((0, 0), (0, f_pad))),
                jnp.pad(W[f"{tag}_w_down"].astype(BF16), ((0, f_pad), (0, 0)))]

    ffn1_w = _gather_two_level(ffn_shards("ffn1"), name="gather_ffn")
    mixer_plan = _GatherPlan([W["w_in"].astype(BF16), W["w_out"].astype(BF16)])
    ffn2_plan = _GatherPlan(ffn_shards("ffn2"))

    mu_p = jnp.pad(W["mu_shift"], (0, QW - W["mu_shift"].shape[0]))[None, :]
    vec = lambda a: a.reshape(1, -1)
    w0r, a0r, kkr, kar = vec(W["w0"]), vec(W["a0"]), vec(W["k_k"]), vec(W["k_a"])
    lnw, lnb, rkr = vec(W["lnx_w"]), vec(W["lnx_b"]), vec(W["r_k"])
    pscale = vec(W["pool_scale"])

    sc1p = [1.0 + mod_row(3 * s + 1) for s in range(3)]
    shifts = [mod_row(3 * s) for s in range(3)]
    wgts = [MACARON, 1.0, MACARON]
    gws = [wgts[s] * (1.0 + mod_row(3 * s + 2)) for s in range(3)]
    gp = [gpre[s:s + 1] for s in range(3)]
    gq = [gpost[s:s + 1] for s in range(3)]

    x1, sv1, (win8, wout8) = _ffn_forward(xs, ffn1_w, gp[0], gq[0], shifts[0], sc1p[0], gws[0], "ffn",
                                          up_plan=mixer_plan)
    w_in_p = jnp.pad(_cols_full(win8), ((0, 0), (0, P_W - IN_W)))
    w_out_f = wout8.reshape(D, D)

    h2 = _pre_norm_mod(x1, gp[1], shifts[1], sc1p[1], name="mix_pre")
    p = _mm(h2, w_in_p, tm=1024, tn=512, tk=2048, name="mix_in")
    q = p[:, PW:]
    o_pool, y_pool = _pool_fwd(p, W["pool_w"], pscale, name="pool_fwd")
    r_s, w_s, k_s, v_s, kap_s, b_s, g_s = _rwkv_prep(q, mu_p, wl, w0r, a0r, kkr, kar, name="rwkv_prep")
    y_scan, sa_s, states, ffn2_w = _scan_fwd(r_s, w_s, k_s, v_s, kap_s, b_s, name="scan_fwd", plan=ffn2_plan)
    cat = _rwkv_post(y_scan, r_s, k_s, v_s, g_s, y_pool, lnw, lnb, rkr, name="rwkv_post")
    f2 = _mm(cat, w_out_f, tm=1024, tn=1024, tk=2048, name="mix_out")
    x2 = _post_norm_res(x1, f2, gq[1], gws[1], name="mix_post")

    x3, sv3, _ = _ffn_forward(x2, ffn2_w, gp[2], gq[2], shifts[2], sc1p[2], gws[2], "ffn")

    loss_part, dx3 = _loss_head(x3, tgt, name="loss_head")
    loss = lax.psum(loss_part[0, 0], MESH_AXES)

    def by_core_chip(blocks):
        shp = blocks.shape
        t = blocks.astype(BF16).reshape((N_DEV // 2, 2) + shp[1:])
        return jnp.swapaxes(t, 0, 1)

    def chip_sums(mine, tag):
        got = _sibling_swap(mine, name=f"{tag}_swap")
        return [_pair_add(m, g, name=f"{tag}_add{i}") for i, (m, g) in enumerate(zip(mine, got))]

    def ffn_parts(pgu, pd):
        fs = F // N_DEV
        return pgu[:, :D, :fs], pgu[:, D:, :fs], pd[:, :fs, :]

    dx2, dwgu2, dwd2, pre3, post3, _, _ = _ffn_backward(dx3, x2, sv3, ffn2_w, gp[2], gq[2], sc1p[2], gws[2], "ffn")
    sums2 = chip_sums([dwgu2, dwd2], "scatter_ffn")

    df2, post2 = _post_norm_res_bwd(dx2, f2, gq[1], gws[1], 1.0, name="mix_post_bwd")
    dw_out = _mm(cat, df2, ta=True, tm=1024, tn=1024, tk=1024, name="mix_dwout")
    dcat = _mm(df2, w_out_f, tb=True, tm=1024, tn=1024, tk=2048, name="mix_dcat")
    dyr = dcat[:, PW:]
    dysc, dg, dr_b, dk2_b, dv_b, post_sums = _rwkv_post_bwd(dyr, y_scan, r_s, k_s, v_s, g_s, lnw, lnb, rkr,
                                                             name="rwkv_post_bwd")
    dr, dw, dk2, dv, dkap, db = _scan_bwd(r_s, w_s, k_s, v_s, kap_s, b_s, sa_s, dysc, states, name="scan_bwd")
    dps, dwl, prep_sums = _rwkv_prep_bwd(q, mu_p, wl, w0r, a0r, kkr, kar,
                                         (dr, dw, dk2, dv, dkap, db, dg, dr_b, dk2_b, dv_b), name="rwkv_prep_bwd")
    dq, dmu = _tshift_bwd(dps, q, mu_p, name="tshift_bwd")
    du_pool, dpool_w, dpool_scale = _pool_bwd(dcat, o_pool, W["pool_w"], pscale, name="pool_bwd")
    dp = jnp.concatenate([du_pool, dq], axis=1)
    dw_in = _mm(h2, dp, ta=True, tm=1024, tn=512, tk=1024, name="mix_dwin")
    dh2 = _mm(dp, w_in_p, tb=True, tm=1024, tn=1024, tk=2816, name="mix_dh")
    dx1, pre2 = _pre_norm_mod_bwd(dh2, x1, dx2, gp[1], sc1p[1], name="mix_pre_bwd")

    sums_mix = chip_sums([by_core_chip(_cols_split(dw_in[:, :IN_W])),
                          by_core_chip(dw_out.reshape(N_DEV, D // N_DEV, D))], "scatter_mixer")
    dx0, dwgu1, dwd1, pre1, post1, parts2, parts_mix = _ffn_backward(
        dx1, xs, sv1, ffn1_w, gp[0], gq[0], sc1p[0], gws[0], "ffn",
        dwgu_plan=_ChipsPlan(sums2), dh_plan=_ChipsPlan(sums_mix))

    pres, posts = [pre1, pre2, pre3], [post1, post2, post3]
    dmod = jnp.stack([jnp.stack([pres[s][0], pres[s][1], posts[s][0]]) for s in range(3)]).reshape(NMOD // 128, 128)
    dnorm_pre = jnp.stack([pres[s][2] for s in range(3)])
    dnorm_post = jnp.stack([posts[s][1] for s in range(3)])

    small = [dmu[0, :W["mu_shift"].shape[0]], dpool_w, dpool_scale, prep_sums[0], prep_sums[1], prep_sums[2],
             prep_sums[3], post_sums[2], post_sums[0], post_sums[1], dnorm_pre, dnorm_post,
             dwl[0:LORA_W, 0:R], dwl[LORA_W:LORA_W + LORA_A, R:2 * R],
             dwl[LORA_W + LORA_A:LORA_W + LORA_A + LORA_G, 2 * R:3 * R]]
    small_shapes = [a.shape for a in small]
    dmod8, small8 = _gather_two_level([dmod, _pack(small)], name="gather_grads")
    g_b_ada = _sum_parts(dmod8, name="sum_dmod").reshape(NMOD)
    red = _unpack(_sum_parts(small8, name="sum_small"), small_shapes)
    (g_mu, g_pool_w, g_pool_scale, g_w0, g_a0, g_kk, g_ka, g_rk, g_lnw, g_lnb, g_npre, g_npost, g_w2, g_a2,
     g_g2) = red

    dmod_all = dmod8.reshape(N_DEV, NMOD)
    dmod_cols = lax.dynamic_slice(dmod_all, (0, me * ada_c), (N_DEV, ada_c))
    dmod_cols = jnp.concatenate([dmod_cols, jnp.zeros_like(dmod_cols)], axis=0)
    g_w_ada = _mm(sc_pad, dmod_cols, ta=True, tm=D, tn=ada_c // 9, tk=16, name="ada_bwd")

    pg2, pu2, pd2 = ffn_parts(*parts2)
    pin, pout = parts_mix
    pg1, pu1, pd1 = ffn_parts(*_chips_all_to_all(chip_sums([dwgu1, dwd1], "scatter_ffn"), name="scatter_ffn_chips"))

    res = {}

    def big(nm, parts, tag):
        res[nm] = _adamw(W[nm], M1[nm], V1[nm], parts, name=tag)

    big("ffn1_w_gate", pg1, "adamw_cols")
    big("ffn1_w_up", pu1, "adamw_cols")
    big("ffn1_w_down", pd1, "adamw_rows")
    big("ffn2_w_gate", pg2, "adamw_cols")
    big("ffn2_w_up", pu2, "adamw_cols")
    big("ffn2_w_down", pd2, "adamw_rows")
    big("w_in", pin, "adamw_w_in")
    big("w_out", pout, "adamw_w_out")
    big("w_ada", g_w_ada[None], "adamw_w_ada")

    def my_cols(full, width):
        return lax.dynamic_slice_in_dim(full, me * width, width, axis=full.ndim - 1)

    small_names = ["b_ada", "mu_shift", "pool_w", "pool_scale", "w0", "a0", "k_k", "k_a", "r_k", "lnx_w", "lnx_b",
                   "norm_pre", "norm_post", "w2", "a2", "g2"]
    small_grads = [g_b_ada, g_mu, g_pool_w, g_pool_scale, g_w0, g_a0, g_kk, g_ka, g_rk.reshape(W["r_k"].shape), g_lnw,
                   g_lnb, my_cols(g_npre, D // N_DEV), my_cols(g_npost, D // N_DEV), my_cols(g_w2, R // N_DEV),
                   my_cols(g_a2, R // N_DEV), my_cols(g_g2, R // N_DEV)]
    shapes = [W[n].shape for n in small_names]
    packed = _adamw(_pack([W[n] for n in small_names]), _pack([M1[n] for n in small_names]),
                    _pack([V1[n] for n in small_names]), _pack(small_grads)[None], name="adamw_small")
    unpacked = [_unpack(t, shapes) for t in packed]
    for i, nm in enumerate(small_names):
        res[nm] = tuple(unpacked[k][i] for k in range(4))

    outs = [loss, dx0[None]]
    for k in range(4):
        outs.extend(res[nm][k][None] for nm in names)
    return tuple(outs)
```

```python
import functools

import jax
import jax.numpy as jnp
from jax import lax
from jax.experimental import pallas as pl
from jax.experimental.pallas import tpu as pltpu

F32 = jnp.float32
BF16 = jnp.bfloat16
N_DEV = 8
MESH_AXES = ("x", "y", "c")

NORM_EPS = 1e-6
HEAD = 64
LN_X_EPS = 1e-5 * HEAD
POOL_GROUPS = 4
POOL_GROUP = 128
MACARON = 0.5
LORA_W, LORA_A, LORA_G = 64, 64, 224
LORA_PAD = 384
ADAM_LR, ADAM_B1, ADAM_B2, ADAM_EPS, ADAM_WD, ADAM_STEP = 0.001, 0.9, 0.999, 1e-08, 0.01, 10

FF_TILE = 768
ROW_TILE = 256
SCAN_T = 64
SCAN_G = 6
VMEM_CAP = 56 * 1024 * 1024


def _cp(sem, vmem_mb):
    return pltpu.CompilerParams(dimension_semantics=sem, vmem_limit_bytes=min(vmem_mb * 1024 * 1024, VMEM_CAP))


def _my_index():
    return 4 * lax.axis_index("x") + 2 * lax.axis_index("y") + lax.axis_index("c")


def _exchange(arrays, *, scatter, name):
    n = len(arrays)
    out_shapes = []
    for a in arrays:
        shp = a.shape if scatter else (N_DEV,) + a.shape
        out_shapes.append(jax.ShapeDtypeStruct(shp, a.dtype))

    def body(*refs):
        ins, outs = refs[:n], refs[n:2 * n]
        send_sems, recv_sems, local_sems = refs[2 * n:]
        me = _my_index()

        def dev(p):
            return (p // 4, (p // 2) % 2, p % 2)

        def copy(i, d):
            peer = (me + d) % N_DEV
            src = ins[i].at[peer] if scatter else ins[i]
            return pltpu.make_async_remote_copy(
                src_ref=src, dst_ref=outs[i].at[me], send_sem=send_sems.at[i, d - 1],
                recv_sem=recv_sems.at[i, d - 1], device_id=dev(peer), device_id_type=pl.DeviceIdType.MESH)

        def arrival(i, d):
            frm = (me + N_DEV - d) % N_DEV
            src = ins[i].at[frm] if scatter else ins[i]
            return pltpu.make_async_remote_copy(
                src_ref=src, dst_ref=outs[i].at[frm], send_sem=send_sems.at[i, d - 1],
                recv_sem=recv_sems.at[i, d - 1], device_id=dev(frm), device_id_type=pl.DeviceIdType.MESH)

        locals_ = []
        for i in range(n):
            src = ins[i].at[me] if scatter else ins[i]
            lc = pltpu.make_async_copy(src, outs[i].at[me], local_sems.at[i])
            lc.start()
            locals_.append(lc)
        sends = [copy(i, d) for d in range(1, N_DEV) for i in range(n)]
        for cp in sends:
            cp.start()
        for d in range(1, N_DEV):
            for i in range(n):
                arrival(i, d).wait_recv()
        for cp in sends:
            cp.wait_send()
        for lc in locals_:
            lc.wait()

    hbm = pl.BlockSpec(memory_space=pltpu.HBM)
    return pl.pallas_call(
        body, name=name, out_shape=tuple(out_shapes), in_specs=[hbm] * n, out_specs=tuple([hbm] * n),
        scratch_shapes=[pltpu.SemaphoreType.DMA((n, N_DEV - 1)), pltpu.SemaphoreType.DMA((n, N_DEV - 1)),
                        pltpu.SemaphoreType.DMA((n,))],
    )(*arrays)


def _remote(src, dst, send_sem, recv_sem, to):
    return pltpu.make_async_remote_copy(src_ref=src, dst_ref=dst, send_sem=send_sem, recv_sem=recv_sem,
                                        device_id=to, device_id_type=pl.DeviceIdType.MESH)


class _GatherPlan:
    def __init__(self, arrays):
        self.arrays = list(arrays)
        self.n = len(arrays)
        self.out_shapes = [jax.ShapeDtypeStruct((N_DEV,) + a.shape, a.dtype) for a in arrays]
        self.scratch = [pltpu.SemaphoreType.DMA((self.n, 7)), pltpu.SemaphoreType.DMA((self.n, 7)),
                        pltpu.SemaphoreType.DMA((self.n,))]

    def _parts(self, ins, outs, sems):
        send_sems, recv_sems, local_sems = sems
        x, y, c = lax.axis_index("x"), lax.axis_index("y"), lax.axis_index("c")
        chips = [(1 - x, y), (x, 1 - y), (1 - x, 1 - y)]

        def slot(i, px, py, pc):
            return outs[i].at[4 * px + 2 * py + pc]

        def copy(i, k, block, to, src=None):
            dst = slot(i, *block)
            return _remote(dst if src is None else src, dst, send_sems.at[i, k], recv_sems.at[i, k], to)

        n = self.n
        locals_ = [pltpu.make_async_copy(ins[i], slot(i, x, y, c), local_sems.at[i]) for i in range(n)]
        first = [copy(i, 1 + j, (x, y, c), (*chip, c), src=ins[i]) for j, chip in enumerate(chips) for i in range(n)]
        first += [copy(i, 0, (x, y, c), (x, y, 1 - c), src=ins[i]) for i in range(n)]
        return (x, y, c), chips, copy, locals_, first

    def start(self, ins, outs, sems):
        _, _, _, locals_, first = self._parts(ins, outs, sems)
        for lc in locals_:
            lc.start()
        for cp in first:
            cp.start()

    def finish(self, ins, outs, sems):
        (x, y, c), chips, copy, locals_, first = self._parts(ins, outs, sems)
        forwards = []
        for j, chip in enumerate(chips):
            for i in range(self.n):
                copy(i, 1 + j, (*chip, c), (x, y, c)).wait_recv()
                fwd = copy(i, 4 + j, (*chip, c), (x, y, 1 - c))
                fwd.start()
                forwards.append(fwd)
        for i in range(self.n):
            copy(i, 0, (x, y, 1 - c), (x, y, c)).wait_recv()
        for j, chip in enumerate(chips):
            for i in range(self.n):
                copy(i, 4 + j, (*chip, 1 - c), (x, y, c)).wait_recv()
        for cp in first + forwards:
            cp.wait_send()
        for lc in locals_:
            lc.wait()


class _ChipsPlan:
    def __init__(self, arrays):
        self.arrays = list(arrays)
        self.n = len(arrays)
        self.out_shapes = [jax.ShapeDtypeStruct(a.shape, a.dtype) for a in arrays]
        self.scratch = [pltpu.SemaphoreType.DMA((self.n, 3)), pltpu.SemaphoreType.DMA((self.n, 3)),
                        pltpu.SemaphoreType.DMA((self.n,))]

    def _parts(self, ins, outs, sems):
        send_sems, recv_sems, local_sems = sems
        x, y, c = lax.axis_index("x"), lax.axis_index("y"), lax.axis_index("c")
        mine = 2 * x + y
        chips = [(1 - x, y), (x, 1 - y), (1 - x, 1 - y)]
        n = self.n
        locals_ = [pltpu.make_async_copy(ins[i].at[mine], outs[i].at[mine], local_sems.at[i]) for i in range(n)]
        sends = [_remote(ins[i].at[2 * chip[0] + chip[1]], outs[i].at[mine], send_sems.at[i, j], recv_sems.at[i, j],
                         (*chip, c)) for j, chip in enumerate(chips) for i in range(n)]

        def arrivals():
            return [_remote(ins[i].at[2 * chip[0] + chip[1]], outs[i].at[2 * chip[0] + chip[1]], send_sems.at[i, j],
                            recv_sems.at[i, j], (*chip, c)) for j, chip in enumerate(chips) for i in range(n)]

        return locals_, sends, arrivals

    def start(self, ins, outs, sems):
        locals_, sends, _ = self._parts(ins, outs, sems)
        for lc in locals_:
            lc.start()
        for cp in sends:
            cp.start()

    def finish(self, ins, outs, sems):
        locals_, sends, arrivals = self._parts(ins, outs, sems)
        for cp in arrivals():
            cp.wait_recv()
        for cp in sends:
            cp.wait_send()
        for lc in locals_:
            lc.wait()


def _run_plan(plan, *, name):
    n = plan.n

    def body(*refs):
        ins, outs, sems = refs[:n], refs[n:2 * n], refs[2 * n:]
        plan.start(ins, outs, sems)
        plan.finish(ins, outs, sems)

    hbm = pl.BlockSpec(memory_space=pltpu.HBM)
    return pl.pallas_call(
        body, name=name, out_shape=tuple(plan.out_shapes), in_specs=[hbm] * n, out_specs=tuple([hbm] * n),
        scratch_shapes=plan.scratch,
    )(*plan.arrays)


def _host(body, plan, n_in, n_out, is_first, is_last):
    if plan is None:
        return body
    m = plan.n

    def wrapped(*refs):
        a, b = n_in, n_in + m
        c, d = b + n_out, b + n_out + m
        own_in, c_in, own_out, c_out, rest = refs[:a], refs[a:b], refs[b:c], refs[c:d], refs[d:]
        own_scr, c_sems = rest[:len(rest) - 3], rest[len(rest) - 3:]

        @pl.when(is_first())
        def _():
            plan.start(c_in, c_out, c_sems)

        body(*own_in, *own_out, *own_scr)

        @pl.when(is_last())
        def _():
            plan.finish(c_in, c_out, c_sems)

    return wrapped


def _host_args(plan):
    if plan is None:
        return [], [], [], [], []
    hbm = pl.BlockSpec(memory_space=pltpu.HBM)
    return [hbm] * plan.n, [hbm] * plan.n, list(plan.out_shapes), list(plan.scratch), list(plan.arrays)


def _gather_two_level(arrays, *, name):
    return _run_plan(_GatherPlan(arrays), name=name)


def _sibling_swap(arrays, *, name):
    n = len(arrays)
    out_shapes = [jax.ShapeDtypeStruct(a.shape[1:], a.dtype) for a in arrays]

    def body(*refs):
        ins, outs = refs[:n], refs[n:2 * n]
        send_sems, recv_sems = refs[2 * n:]
        x, y, c = lax.axis_index("x"), lax.axis_index("y"), lax.axis_index("c")
        copies = [_remote(ins[i].at[1 - c], outs[i], send_sems.at[i], recv_sems.at[i], (x, y, 1 - c))
                  for i in range(n)]
        for cp in copies:
            cp.start()
        for cp in copies:
            cp.wait_recv()
        for cp in copies:
            cp.wait_send()

    hbm = pl.BlockSpec(memory_space=pltpu.HBM)
    return pl.pallas_call(
        body, name=name, out_shape=tuple(out_shapes), in_specs=[hbm] * n, out_specs=tuple([hbm] * n),
        scratch_shapes=[pltpu.SemaphoreType.DMA((n,)), pltpu.SemaphoreType.DMA((n,))],
    )(*arrays)


def _chips_all_to_all(arrays, *, name):
    return _run_plan(_ChipsPlan(arrays), name=name)


def _pair_add(mine, got, *, name):
    _, nq, R, C = mine.shape
    tr = R
    for cand in (512, 256, 128, 64, 32, 16):
        if R % cand == 0 and cand * C * 2 * 3 * 2 <= 12 * 1024 * 1024:
            tr = cand
            break

    def body(core_ref, m_ref, g_ref, o_ref):
        o_ref[0] = (m_ref[0, 0].astype(F32) + g_ref[0].astype(F32)).astype(BF16)

    core = lax.axis_index("c").astype(jnp.int32).reshape(1)
    return pl.pallas_call(
        body, name=name,
        grid_spec=pltpu.PrefetchScalarGridSpec(
            num_scalar_prefetch=1, grid=(nq, R // tr),
            in_specs=[pl.BlockSpec((1, 1, tr, C), lambda q, i, core_ref: (core_ref[0], q, i, 0)),
                      pl.BlockSpec((1, tr, C), lambda q, i, core_ref: (q, i, 0))],
            out_specs=pl.BlockSpec((1, tr, C), lambda q, i, core_ref: (q, i, 0))),
        out_shape=jax.ShapeDtypeStruct((nq, R, C), BF16),
        compiler_params=_cp(("parallel", "parallel"), 40),
    )(core, mine, got)


def _mm(a, b, *, ta=False, tb=False, tm, tn, tk, out_dtype=F32, name):
    M = a.shape[1] if ta else a.shape[0]
    K = a.shape[0] if ta else a.shape[1]
    N = b.shape[0] if tb else b.shape[1]
    tm, tn, tk = min(tm, M), min(tn, N), min(tk, K)
    assert M % tm == 0 and N % tn == 0 and K % tk == 0, (name, M, N, K, tm, tn, tk)
    nk = K // tk
    dims = (((0 if ta else 1,), (1 if tb else 0,)), ((), ()))

    def body(a_ref, b_ref, o_ref, acc_ref):
        k = pl.program_id(2)

        @pl.when(k == 0)
        def _():
            acc_ref[...] = jnp.zeros_like(acc_ref)

        acc_ref[...] += lax.dot_general(a_ref[...].astype(BF16), b_ref[...].astype(BF16), dims,
                                        preferred_element_type=F32)

        @pl.when(k == nk - 1)
        def _():
            o_ref[...] = acc_ref[...].astype(out_dtype)

    a_spec = pl.BlockSpec((tk, tm), lambda i, j, k: (k, i)) if ta else pl.BlockSpec((tm, tk), lambda i, j, k: (i, k))
    b_spec = pl.BlockSpec((tn, tk), lambda i, j, k: (j, k)) if tb else pl.BlockSpec((tk, tn), lambda i, j, k: (k, j))
    blk = 2 * (tm * tk * a.dtype.itemsize + tk * tn * b.dtype.itemsize + tm * tn * jnp.dtype(out_dtype).itemsize)
    return pl.pallas_call(
        body, name=name, grid=(M // tm, N // tn, nk), in_specs=[a_spec, b_spec],
        out_specs=pl.BlockSpec((tm, tn), lambda i, j, k: (i, j)),
        out_shape=jax.ShapeDtypeStruct((M, N), out_dtype),
        scratch_shapes=[pltpu.VMEM((tm, tn), F32)],
        compiler_params=_cp(("parallel", "parallel", "arbitrary"), (blk + tm * tn * 4) // (1024 * 1024) + 12),
    )(a, b)


def _grid_ends(grid):
    def is_first():
        ok = pl.program_id(0) == 0
        for ax in range(1, len(grid)):
            ok = ok & (pl.program_id(ax) == 0)
        return ok

    def is_last():
        ok = pl.program_id(0) == grid[0] - 1
        for ax in range(1, len(grid)):
            ok = ok & (pl.program_id(ax) == grid[ax] - 1)
        return ok

    return is_first, is_last


def _semantics(plan, sem):
    return sem if plan is None else tuple("arbitrary" for _ in sem)


def _ffn_up(h, g8, u8, *, tm, tk, name, plan=None):
    S, D = h.shape
    nb, _, tn = g8.shape
    tm = min(tm, S)
    tk = min(tk, D)
    nk = D // tk

    def body(h_ref, g_ref, u_ref, au_ref, s_ref, acc_ref):
        k = pl.program_id(2)

        @pl.when(k == 0)
        def _():
            acc_ref[...] = jnp.zeros_like(acc_ref)

        hv = h_ref[...]
        acc_ref[:, :tn] += jnp.dot(hv, g_ref[0], preferred_element_type=F32)
        acc_ref[:, tn:] += jnp.dot(hv, u_ref[0], preferred_element_type=F32)

        @pl.when(k == nk - 1)
        def _():
            acc = acc_ref[...]
            a = acc[:, :tn]
            u = acc[:, tn:]
            au_ref[...] = acc.astype(BF16)
            s_ref[...] = (a * jax.nn.sigmoid(a) * u).astype(BF16)

    wspec = pl.BlockSpec((1, tk, tn), lambda i, j, k: (j, k, 0))
    grid = (S // tm, nb, nk)
    x_in, x_out, x_shapes, x_scr, x_ops = _host_args(plan)
    outs = pl.pallas_call(
        _host(body, plan, 3, 2, *_grid_ends(grid)), name=name, grid=grid,
        in_specs=[pl.BlockSpec((tm, tk), lambda i, j, k: (i, k)), wspec, wspec] + x_in,
        out_specs=tuple([pl.BlockSpec((tm, 2 * tn), lambda i, j, k: (i, j)),
                         pl.BlockSpec((tm, tn), lambda i, j, k: (i, j))] + x_out),
        out_shape=tuple([jax.ShapeDtypeStruct((S, 2 * nb * tn), BF16), jax.ShapeDtypeStruct((S, nb * tn), BF16)]
                        + x_shapes),
        scratch_shapes=[pltpu.VMEM((tm, 2 * tn), F32)] + x_scr,
        compiler_params=_cp(_semantics(plan, ("parallel", "parallel", "arbitrary")), 52),
    )(h, g8, u8, *x_ops)
    return outs[0], outs[1], tuple(outs[2:])


def _ffn_dh(dau, g8, u8, *, tm, tn, name, plan=None):
    S = dau.shape[0]
    nb, D, tf = g8.shape
    tm, tn = min(tm, S), min(tn, D)
    nk = 2 * nb
    nt = (((1,), (1,)), ((), ()))

    def body(a_ref, g_ref, u_ref, o_ref, acc_ref):
        k = pl.program_id(2)

        @pl.when(k == 0)
        def _():
            acc_ref[...] = jnp.zeros_like(acc_ref)

        @pl.when(k % 2 == 0)
        def _():
            acc_ref[...] += lax.dot_general(a_ref[...], g_ref[0], nt, preferred_element_type=F32)

        @pl.when(k % 2 == 1)
        def _():
            acc_ref[...] += lax.dot_general(a_ref[...], u_ref[0], nt, preferred_element_type=F32)

        @pl.when(k == nk - 1)
        def _():
            o_ref[...] = acc_ref[...]

    wspec = pl.BlockSpec((1, tn, tf), lambda i, n, k: (k // 2, n, 0))
    grid = (S // tm, D // tn, nk)
    x_in, x_out, x_shapes, x_scr, x_ops = _host_args(plan)
    outs = pl.pallas_call(
        _host(body, plan, 3, 1, *_grid_ends(grid)), name=name, grid=grid,
        in_specs=[pl.BlockSpec((tm, tf), lambda i, n, k: (i, k)), wspec, wspec] + x_in,
        out_specs=tuple([pl.BlockSpec((tm, tn), lambda i, n, k: (i, n))] + x_out),
        out_shape=tuple([jax.ShapeDtypeStruct((S, D), F32)] + x_shapes),
        scratch_shapes=[pltpu.VMEM((tm, tn), F32)] + x_scr,
        compiler_params=_cp(_semantics(plan, ("parallel", "parallel", "arbitrary")), 40),
    )(dau, g8, u8, *x_ops)
    return outs[0], tuple(outs[1:])


def _ffn_dwgu(h, dau, *, tm, tk, name, plan=None):
    S, D = h.shape
    tf = FF_TILE
    nt = dau.shape[1] // tf
    tm, tk = min(tm, D), min(tk, S)
    nk = S // tk
    ni = D // tm

    def body(a_ref, b_ref, o_ref, acc_ref):
        k = pl.program_id(2)

        @pl.when(k == 0)
        def _():
            acc_ref[...] = jnp.zeros_like(acc_ref)

        acc_ref[...] += lax.dot_general(a_ref[...], b_ref[...], (((0,), (0,)), ((), ())), preferred_element_type=F32)

        @pl.when(k == nk - 1)
        def _():
            o_ref[0, 0] = acc_ref[...].astype(BF16)

    grid = (ni, nt, nk)
    x_in, x_out, x_shapes, x_scr, x_ops = _host_args(plan)
    outs = pl.pallas_call(
        _host(body, plan, 2, 1, *_grid_ends(grid)), name=name, grid=grid,
        in_specs=[pl.BlockSpec((tk, tm), lambda i, j, k: (k, i)), pl.BlockSpec((tk, tf), lambda i, j, k: (k, j))] + x_in,
        out_specs=tuple([pl.BlockSpec((1, 1, tm, tf), lambda i, j, k: ((j // 2) % 2, j // 4, (j % 2) * ni + i, 0))]
                        + x_out),
        out_shape=tuple([jax.ShapeDtypeStruct((2, nt // 4, 2 * D, tf), BF16)] + x_shapes),
        scratch_shapes=[pltpu.VMEM((tm, tf), F32)] + x_scr,
        compiler_params=_cp(_semantics(plan, ("parallel", "parallel", "arbitrary")), 40),
    )(h, dau, *x_ops)
    return outs[0], tuple(outs[1:])


def _ffn_dwd(s, df, *, tn, tk, name, plan=None):
    S, D = df.shape
    tf = FF_TILE
    nb = s.shape[1] // tf
    tn, tk = min(tn, D), min(tk, S)
    nk = S // tk

    def body(a_ref, b_ref, o_ref, acc_ref):
        k = pl.program_id(2)

        @pl.when(k == 0)
        def _():
            acc_ref[...] = jnp.zeros_like(acc_ref)

        acc_ref[...] += lax.dot_general(a_ref[...], b_ref[...], (((0,), (0,)), ((), ())), preferred_element_type=F32)

        @pl.when(k == nk - 1)
        def _():
            o_ref[0, 0] = acc_ref[...].astype(BF16)

    grid = (nb, D // tn, nk)
    x_in, x_out, x_shapes, x_scr, x_ops = _host_args(plan)
    outs = pl.pallas_call(
        _host(body, plan, 2, 1, *_grid_ends(grid)), name=name, grid=grid,
        in_specs=[pl.BlockSpec((tk, tf), lambda j, n, k: (k, j)), pl.BlockSpec((tk, tn), lambda j, n, k: (k, n))] + x_in,
        out_specs=tuple([pl.BlockSpec((1, 1, tf, tn), lambda j, n, k: (j % 2, j // 2, 0, n))] + x_out),
        out_shape=tuple([jax.ShapeDtypeStruct((2, nb // 2, tf, D), BF16)] + x_shapes),
        scratch_shapes=[pltpu.VMEM((tf, tn), F32)] + x_scr,
        compiler_params=_cp(_semantics(plan, ("parallel", "parallel", "arbitrary")), 40),
    )(s, df, *x_ops)
    return outs[0], tuple(outs[1:])


def _ffn_down_bwd(df, d8, au, *, tm, tk, name, plan=None):
    S, D = df.shape
    nb, tn, _ = d8.shape
    F = nb * tn
    tm = min(tm, S)
    tk = min(tk, D)
    nk = D // tk

    def body(df_ref, w_ref, au_ref, dau_ref, acc_ref):
        k = pl.program_id(2)

        @pl.when(k == 0)
        def _():
            acc_ref[...] = jnp.zeros_like(acc_ref)

        acc_ref[...] += lax.dot_general(df_ref[...], w_ref[0], (((1,), (1,)), ((), ())), preferred_element_type=F32)

        @pl.when(k == nk - 1)
        def _():
            ds = acc_ref[...]
            au_v = au_ref[...].astype(F32)
            a = au_v[:, :tn]
            u = au_v[:, tn:]
            sg = jax.nn.sigmoid(a)
            da = ds * u * (sg * (1.0 + a * (1.0 - sg)))
            du = ds * (a * sg)
            dau_ref[:, :tn] = da.astype(BF16)
            dau_ref[:, tn:] = du.astype(BF16)

    grid = (S // tm, F // tn, nk)
    x_in, x_out, x_shapes, x_scr, x_ops = _host_args(plan)
    outs = pl.pallas_call(
        _host(body, plan, 3, 1, *_grid_ends(grid)), name=name, grid=grid,
        in_specs=[pl.BlockSpec((tm, tk), lambda i, j, k: (i, k)), pl.BlockSpec((1, tn, tk), lambda i, j, k: (j, 0, k)),
                  pl.BlockSpec((tm, 2 * tn), lambda i, j, k: (i, j))] + x_in,
        out_specs=tuple([pl.BlockSpec((tm, 2 * tn), lambda i, j, k: (i, j))] + x_out),
        out_shape=tuple([jax.ShapeDtypeStruct((S, 2 * F), BF16)] + x_shapes),
        scratch_shapes=[pltpu.VMEM((tm, tn), F32)] + x_scr,
        compiler_params=_cp(_semantics(plan, ("parallel", "parallel", "arbitrary")), 52),
    )(df, d8, au, *x_ops)
    return outs[0], tuple(outs[1:])


def _fold8(x):
    tm, w = x.shape
    return jnp.sum(x.reshape(tm // 8, 8, w), axis=0)


def _row_spec(tm, w):
    return pl.BlockSpec((tm, w), lambda i: (i, 0))


def _vec_spec(rows, w):
    return pl.BlockSpec((rows, w), lambda i: (0, 0))


def _pre_norm_mod(x, gain, shift, scale1p, *, name):
    S, D = x.shape
    tm = ROW_TILE

    def body(x_ref, g_ref, sh_ref, sc_ref, h_ref):
        xv = x_ref[...]
        rinv = lax.rsqrt(jnp.mean(xv * xv, axis=-1, keepdims=True) + NORM_EPS)
        h_ref[...] = ((xv * rinv) * g_ref[...] * sc_ref[...] + sh_ref[...]).astype(BF16)

    return pl.pallas_call(
        body, name=name, grid=(S // tm,),
        in_specs=[_row_spec(tm, D), _vec_spec(1, D), _vec_spec(1, D), _vec_spec(1, D)],
        out_specs=_row_spec(tm, D), out_shape=jax.ShapeDtypeStruct((S, D), BF16),
        compiler_params=_cp(("parallel",), 32),
    )(x, gain, shift, scale1p)


def _pre_norm_mod_bwd(dh, x, dres, gain, scale1p, *, name):
    S, D = x.shape
    tm = ROW_TILE
    n = S // tm

    def body(dh_ref, x_ref, dr_ref, g_ref, sc_ref, dx_ref, sums_ref, acc_ref):
        i = pl.program_id(0)

        @pl.when(i == 0)
        def _():
            acc_ref[...] = jnp.zeros_like(acc_ref)

        xv = x_ref[...]
        dhv = dh_ref[...]
        g = g_ref[...]
        rinv = lax.rsqrt(jnp.mean(xv * xv, axis=-1, keepdims=True) + NORM_EPS)
        xn = xv * rinv
        dn = dhv * sc_ref[...]
        dxn = dn * g
        dx_ref[...] = dr_ref[...] + rinv * (dxn - xn * jnp.mean(dxn * xn, axis=-1, keepdims=True))
        acc_ref[0] += _fold8(dhv)
        acc_ref[1] += _fold8(dhv * (xn * g))
        acc_ref[2] += _fold8(dn * xn)

        @pl.when(i == n - 1)
        def _():
            for q in range(3):
                sums_ref[q:q + 1, :] = jnp.sum(acc_ref[q], axis=0, keepdims=True)

    return pl.pallas_call(
        body, name=name, grid=(n,),
        in_specs=[_row_spec(tm, D), _row_spec(tm, D), _row_spec(tm, D), _vec_spec(1, D), _vec_spec(1, D)],
        out_specs=(_row_spec(tm, D), _vec_spec(3, D)),
        out_shape=(jax.ShapeDtypeStruct((S, D), F32), jax.ShapeDtypeStruct((3, D), F32)),
        scratch_shapes=[pltpu.VMEM((3, 8, D), F32)],
        compiler_params=_cp(("arbitrary",), 40),
    )(dh, x, dres, gain, scale1p)


def _post_norm_res(x, f, gain, gw, *, name):
    S, D = x.shape
    tm = ROW_TILE

    def body(x_ref, f_ref, g_ref, gw_ref, o_ref):
        fv = f_ref[...]
        rinv = lax.rsqrt(jnp.mean(fv * fv, axis=-1, keepdims=True) + NORM_EPS)
        o_ref[...] = x_ref[...] + gw_ref[...] * ((fv * rinv) * g_ref[...])

    return pl.pallas_call(
        body, name=name, grid=(S // tm,),
        in_specs=[_row_spec(tm, D), _row_spec(tm, D), _vec_spec(1, D), _vec_spec(1, D)],
        out_specs=_row_spec(tm, D), out_shape=jax.ShapeDtypeStruct((S, D), F32),
        compiler_params=_cp(("parallel",), 32),
    )(x, f, gain, gw)


def _post_norm_res_bwd(dxo, f, gain, gw, weight, *, name):
    S, D = f.shape
    tm = ROW_TILE
    n = S // tm

    def body(d_ref, f_ref, g_ref, gw_ref, df_ref, sums_ref, acc_ref):
        i = pl.program_id(0)

        @pl.when(i == 0)
        def _():
            acc_ref[...] = jnp.zeros_like(acc_ref)

        fv = f_ref[...]
        dv = d_ref[...]
        g = g_ref[...]
        rinv = lax.rsqrt(jnp.mean(fv * fv, axis=-1, keepdims=True) + NORM_EPS)
        fh = fv * rinv
        dy = dv * gw_ref[...]
        dfh = dy * g
        df_ref[...] = (rinv * (dfh - fh * jnp.mean(dfh * fh, axis=-1, keepdims=True))).astype(BF16)
        acc_ref[0] += _fold8(weight * dv * (fh * g))
        acc_ref[1] += _fold8(dy * fh)

        @pl.when(i == n - 1)
        def _():
            for q in range(2):
                sums_ref[q:q + 1, :] = jnp.sum(acc_ref[q], axis=0, keepdims=True)

    return pl.pallas_call(
        body, name=name, grid=(n,),
        in_specs=[_row_spec(tm, D), _row_spec(tm, D), _vec_spec(1, D), _vec_spec(1, D)],
        out_specs=(_row_spec(tm, D), _vec_spec(2, D)),
        out_shape=(jax.ShapeDtypeStruct((S, D), BF16), jax.ShapeDtypeStruct((2, D), F32)),
        scratch_shapes=[pltpu.VMEM((2, 8, D), F32)],
        compiler_params=_cp(("arbitrary",), 40),
    )(dxo, f, gain, gw)


def _loss_head(y, target, *, name):
    S, D = y.shape
    tm = ROW_TILE

    def body(y_ref, t_ref, l_ref, dy_ref):
        i = pl.program_id(0)

        @pl.when(i == 0)
        def _():
            l_ref[...] = jnp.zeros_like(l_ref)

        err = y_ref[...] - t_ref[...]
        dy_ref[...] = err * (1.0 / D)
        row = jnp.sum(err * err, axis=-1, keepdims=True) * (0.5 / D)
        l_ref[...] += jnp.sum(row, axis=0, keepdims=True)

    return pl.pallas_call(
        body, name=name, grid=(S // tm,),
        in_specs=[_row_spec(tm, D), _row_spec(tm, D)],
        out_specs=(_vec_spec(1, 1), _row_spec(tm, D)),
        out_shape=(jax.ShapeDtypeStruct((1, 1), F32), jax.ShapeDtypeStruct((S, D), F32)),
        compiler_params=_cp(("arbitrary",), 32),
    )(y, target)


def _shift_down(z, j, row):
    return jnp.where(row >= j, pltpu.roll(z, j, 0), 0.0)


def _shift_up(z, j, row, n):
    return jnp.where(row < n - j, pltpu.roll(z, n - j, 0), 0.0)


def _pool_fwd(p, pool_w, pool_scale, *, name):
    S = p.shape[0]
    C = POOL_GROUP

    def body(u_ref, w_ref, sc_ref, o_ref, y_ref):
        g = pl.program_id(0)
        u = u_ref[...]
        row = lax.broadcasted_iota(jnp.int32, (S, C), 0)
        s1 = u + _shift_down(u, 1, row)
        s2 = s1 + _shift_down(s1, 2, row)
        s3 = s2 + _shift_down(s2, 4, row)
        s4 = s3 + _shift_down(s3, 8, row)
        gi = jnp.zeros((S, C), jnp.int32) + g
        win = jnp.where(gi == 0, s1, jnp.where(gi == 1, s2, jnp.where(gi == 2, s3, s4)))
        width = jnp.where(gi == 0, 2, jnp.where(gi == 1, 4, jnp.where(gi == 2, 8, 16)))
        count = jnp.minimum(row + 1, width).astype(F32)
        o = win / count - u
        o_ref[...] = o
        y_ref[...] = jnp.dot(o.astype(BF16), w_ref[0].astype(BF16), preferred_element_type=F32) * sc_ref[...]

    col = pl.BlockSpec((S, C), lambda g: (0, g))
    return pl.pallas_call(
        body, name=name, grid=(POOL_GROUPS,),
        in_specs=[col, pl.BlockSpec((1, C, C), lambda g: (g, 0, 0)), pl.BlockSpec((1, C), lambda g: (0, g))],
        out_specs=(col, col),
        out_shape=(jax.ShapeDtypeStruct((S, POOL_GROUPS * C), F32), jax.ShapeDtypeStruct((S, POOL_GROUPS * C), F32)),
        compiler_params=_cp(("parallel",), 48),
    )(p, pool_w, pool_scale)


def _pool_bwd(dcat, o, pool_w, pool_scale, *, name):
    S = o.shape[0]
    C = POOL_GROUP

    def body(dy_ref, o_ref, w_ref, sc_ref, du_ref, dw_ref, dsc_ref):
        g = pl.program_id(0)
        dy = dy_ref[...]
        ob = o_ref[...].astype(BF16)
        wb = w_ref[0].astype(BF16)
        mixed = jnp.dot(ob, wb, preferred_element_type=F32)
        dsc_ref[...] = jnp.sum(_fold8(dy * mixed), axis=0, keepdims=True)
        dmix = (dy * sc_ref[...]).astype(BF16)
        dw_ref[0] = lax.dot_general(ob, dmix, (((0,), (0,)), ((), ())), preferred_element_type=F32)
        do = lax.dot_general(dmix, wb, (((1,), (1,)), ((), ())), preferred_element_type=F32)
        row = lax.broadcasted_iota(jnp.int32, (S, C), 0)
        gi = jnp.zeros((S, C), jnp.int32) + g
        width = jnp.where(gi == 0, 2, jnp.where(gi == 1, 4, jnp.where(gi == 2, 8, 16)))
        z = do / jnp.minimum(row + 1, width).astype(F32)
        s1 = z + _shift_up(z, 1, row, S)
        s2 = s1 + _shift_up(s1, 2, row, S)
        s3 = s2 + _shift_up(s2, 4, row, S)
        s4 = s3 + _shift_up(s3, 8, row, S)
        win = jnp.where(gi == 0, s1, jnp.where(gi == 1, s2, jnp.where(gi == 2, s3, s4)))
        du_ref[...] = (win - do).astype(BF16)

    col = pl.BlockSpec((S, C), lambda g: (0, g))
    return pl.pallas_call(
        body, name=name, grid=(POOL_GROUPS,),
        in_specs=[col, col, pl.BlockSpec((1, C, C), lambda g: (g, 0, 0)), pl.BlockSpec((1, C), lambda g: (0, g))],
        out_specs=(col, pl.BlockSpec((1, C, C), lambda g: (g, 0, 0)), pl.BlockSpec((1, C), lambda g: (0, g))),
        out_shape=(jax.ShapeDtypeStruct((S, POOL_GROUPS * C), BF16), jax.ShapeDtypeStruct((POOL_GROUPS, C, C), F32),
                   jax.ShapeDtypeStruct((1, POOL_GROUPS * C), F32)),
        compiler_params=_cp(("parallel",), 48),
    )(dcat, o, pool_w, pool_scale)


def _block_ones():
    r = lax.broadcasted_iota(jnp.int32, (128, 128), 0) // HEAD
    c = lax.broadcasted_iota(jnp.int32, (128, 128), 1) // HEAD
    return jnp.where(r == c, 1.0, 0.0).astype(BF16)


def _segsum(x, bd):
    outs = []
    for j in range(x.shape[1] // 128):
        xs = x[:, j * 128:(j + 1) * 128]
        hi = xs.astype(BF16)
        lo = (xs - hi.astype(F32)).astype(BF16)
        outs.append(jnp.dot(hi, bd, preferred_element_type=F32) + jnp.dot(lo, bd, preferred_element_type=F32))
    return jnp.concatenate(outs, axis=1)


def _prep_common(q, qprev, first, mu, wl, w0, a0, kkw, kaw, R):
    tm = q.shape[0]
    row = lax.broadcasted_iota(jnp.int32, q.shape, 0)
    last = qprev[7:8, :] * first
    prev = jnp.where(row == 0, last, pltpu.roll(q, 1, 0))
    ps = q + mu * (prev - q)
    r = ps[:, 0:R]
    k = ps[:, R:2 * R]
    v = ps[:, 2 * R:3 * R]
    lo_in = ps[:, 3 * R:3 * R + LORA_PAD]
    lane = lax.broadcasted_iota(jnp.int32, (tm, LORA_PAD), 1)
    m_w = lane < LORA_W
    m_a = lane < LORA_W + LORA_A
    m_g = lane < LORA_W + LORA_A + LORA_G
    act = jnp.where(m_w, jnp.tanh(lo_in), jnp.where(m_a, lo_in, jnp.where(m_g, jax.nn.sigmoid(lo_in), 0.0)))
    lo = jnp.dot(act.astype(BF16), wl, preferred_element_type=F32)
    wpre = w0 + lo[:, 0:R]
    apre = a0 + lo[:, R:2 * R]
    g = lo[:, 2 * R:3 * R]
    neg = -wpre
    softplus = jnp.maximum(neg, 0.0) + jnp.log(1.0 + jnp.exp(-jnp.abs(neg)))
    wlog = -softplus - 0.5
    ew = jnp.exp(wlog)
    decay = jnp.exp(-ew)
    a = jax.nn.sigmoid(apre)
    kk = k * kkw
    bd = _block_ones()
    n2 = _segsum(kk * kk, bd)
    nrm = jnp.maximum(jnp.sqrt(n2), 1e-12)
    kap = kk / nrm
    kmul = 1.0 + (a - 1.0) * kaw
    k2 = k * kmul
    return dict(prev=prev, r=r, k=k, v=v, act=act, m_w=m_w, m_a=m_a, m_g=m_g, wpre=wpre, g=g, ew=ew, decay=decay,
                a=a, n2=n2, nrm=nrm, kap=kap, kmul=kmul, k2=k2, bd=bd)


def _prev_rows_spec(tm, w):
    return pl.BlockSpec((8, w), lambda i: (jnp.maximum(i * (tm // 8) - 1, 0), 0))


def _rwkv_prep(q, mu, wl, w0, a0, kkw, kaw, *, name):
    S, QW = q.shape
    R = w0.shape[1]
    tm = ROW_TILE // 2

    def body(q_ref, qp_ref, mu_ref, wl_ref, w0_ref, a0_ref, kk_ref, ka_ref, r_ref, w_ref, k_ref, v_ref, kap_ref,
             b_ref, g_ref):
        first = jnp.where(pl.program_id(0) > 0, 1.0, 0.0)
        t = _prep_common(q_ref[...], qp_ref[...], first, mu_ref[...], wl_ref[...], w0_ref[...], a0_ref[...],
                         kk_ref[...], ka_ref[...], R)
        r_ref[...] = t["r"]
        w_ref[...] = t["decay"]
        k_ref[...] = t["k2"]
        v_ref[...] = t["v"]
        kap_ref[...] = t["kap"]
        b_ref[...] = t["kap"] * t["a"]
        g_ref[...] = t["g"]

    vec = _vec_spec(1, R)
    return pl.pallas_call(
        body, name=name, grid=(S // tm,),
        in_specs=[_row_spec(tm, QW), _prev_rows_spec(tm, QW), _vec_spec(1, QW), _vec_spec(LORA_PAD, 3 * R), vec, vec,
                  vec, vec],
        out_specs=tuple([_row_spec(tm, R)] * 7),
        out_shape=tuple([jax.ShapeDtypeStruct((S, R), F32)] * 7),
        compiler_params=_cp(("parallel",), 48),
    )(q, q, mu, wl, w0, a0, kkw, kaw)


def _rwkv_prep_bwd(q, mu, wl, w0, a0, kkw, kaw, grads, *, name):
    S, QW = q.shape
    R = w0.shape[1]
    tm = ROW_TILE // 2
    n = S // tm

    def body(q_ref, qp_ref, mu_ref, wl_ref, w0_ref, a0_ref, kk_ref, ka_ref, dr_ref, dw_ref, dk2_ref, dv_ref, dkap_ref,
             db_ref, dg_ref, drb_ref, dk2b_ref, dvb_ref, dps_ref, dwl_ref, sums_ref, acc_ref):
        i = pl.program_id(0)

        @pl.when(i == 0)
        def _():
            acc_ref[...] = jnp.zeros_like(acc_ref)
            dwl_ref[...] = jnp.zeros_like(dwl_ref)

        first = jnp.where(i > 0, 1.0, 0.0)
        wl = wl_ref[...]
        kkw = kk_ref[...]
        kaw = ka_ref[...]
        t = _prep_common(q_ref[...], qp_ref[...], first, mu_ref[...], wl, w0_ref[...], a0_ref[...], kkw, kaw, R)
        a, kap, k, act = t["a"], t["kap"], t["k"], t["act"]
        db = db_ref[...]
        dk2 = dk2_ref[...] + dk2b_ref[...]
        dkap = dkap_ref[...] + db * a
        da = db * kap + dk2 * k * kaw
        dk = dk2 * t["kmul"]
        proj = jnp.where(jnp.sqrt(t["n2"]) > 1e-12, _segsum(kap * dkap, t["bd"]), 0.0)
        dkk = (dkap - kap * proj) / t["nrm"]
        dk = dk + dkk * kkw
        dapre = da * a * (1.0 - a)
        dwlog = dw_ref[...] * t["decay"] * (-t["ew"])
        dwpre = dwlog * jax.nn.sigmoid(-t["wpre"])
        acc_ref[0] += _fold8(dwpre)
        acc_ref[1] += _fold8(dapre)
        acc_ref[2] += _fold8(dkk * k)
        acc_ref[3] += _fold8(dk2 * k * (a - 1.0))
        dlo = jnp.concatenate([dwpre, dapre, dg_ref[...]], axis=1).astype(BF16)
        dwl_ref[...] += lax.dot_general(act.astype(BF16), dlo, (((0,), (0,)), ((), ())), preferred_element_type=F32)
        dact = lax.dot_general(dlo, wl, (((1,), (1,)), ((), ())), preferred_element_type=F32)
        dlin = jnp.where(t["m_w"], dact * (1.0 - act * act),
                         jnp.where(t["m_a"], dact, jnp.where(t["m_g"], dact * act * (1.0 - act), 0.0)))
        dps_ref[:, 0:R] = dr_ref[...] + drb_ref[...]
        dps_ref[:, R:2 * R] = dk
        dps_ref[:, 2 * R:3 * R] = dv_ref[...] + dvb_ref[...]
        dps_ref[:, 3 * R:3 * R + LORA_PAD] = dlin
        dps_ref[:, 3 * R + LORA_PAD:] = jnp.zeros((tm, QW - 3 * R - LORA_PAD), F32)

        @pl.when(i == n - 1)
        def _():
            for j in range(4):
                sums_ref[j:j + 1, :] = jnp.sum(acc_ref[j], axis=0, keepdims=True)

    vec = _vec_spec(1, R)
    return pl.pallas_call(
        body, name=name, grid=(n,),
        in_specs=[_row_spec(tm, QW), _prev_rows_spec(tm, QW), _vec_spec(1, QW), _vec_spec(LORA_PAD, 3 * R), vec, vec,
                  vec, vec] + [_row_spec(tm, R)] * 10,
        out_specs=(_row_spec(tm, QW), _vec_spec(LORA_PAD, 3 * R), _vec_spec(4, R)),
        out_shape=(jax.ShapeDtypeStruct((S, QW), F32), jax.ShapeDtypeStruct((LORA_PAD, 3 * R), F32),
                   jax.ShapeDtypeStruct((4, R), F32)),
        scratch_shapes=[pltpu.VMEM((4, 8, R), F32)],
        compiler_params=_cp(("arbitrary",), 56),
    )(q, q, mu, wl, w0, a0, kkw, kaw, *grads)


def _tshift_bwd(dps, q, mu, *, name):
    S, QW = q.shape
    tm = ROW_TILE // 2
    n = S // tm

    def body(d_ref, dn_ref, q_ref, qp_ref, mu_ref, dq_ref, dmu_ref, acc_ref):
        i = pl.program_id(0)

        @pl.when(i == 0)
        def _():
            acc_ref[...] = jnp.zeros_like(acc_ref)

        mu = mu_ref[...]
        d = d_ref[...]
        qv = q_ref[...]
        row = lax.broadcasted_iota(jnp.int32, d.shape, 0)
        first = jnp.where(i > 0, 1.0, 0.0)
        notlast = jnp.where(i < n - 1, 1.0, 0.0)
        prev = jnp.where(row == 0, qp_ref[7:8, :] * first, pltpu.roll(qv, 1, 0))
        z = d * mu
        nxt = jnp.where(row == tm - 1, dn_ref[0:1, :] * mu * notlast, pltpu.roll(z, tm - 1, 0))
        dq_ref[...] = (d * (1.0 - mu) + nxt).astype(BF16)
        acc_ref[...] += _fold8(d * (prev - qv))

        @pl.when(i == n - 1)
        def _():
            dmu_ref[...] = jnp.sum(acc_ref[...], axis=0, keepdims=True)

    nblk8 = S // 8
    next_spec = pl.BlockSpec((8, QW), lambda i: (jnp.minimum((i + 1) * (tm // 8), nblk8 - 1), 0))
    return pl.pallas_call(
        body, name=name, grid=(n,),
        in_specs=[_row_spec(tm, QW), next_spec, _row_spec(tm, QW), _prev_rows_spec(tm, QW), _vec_spec(1, QW)],
        out_specs=(_row_spec(tm, QW), _vec_spec(1, QW)),
        out_shape=(jax.ShapeDtypeStruct((S, QW), BF16), jax.ShapeDtypeStruct((1, QW), F32)),
        scratch_shapes=[pltpu.VMEM((8, QW), F32)],
        compiler_params=_cp(("arbitrary",), 48),
    )(dps, dps, q, q, mu)


def _post_common(ysc, r, k2, v, lnw, lnb, rk):
    bd = _block_ones()
    mean = _segsum(ysc, bd) * (1.0 / HEAD)
    d = ysc - mean
    var = _segsum(d * d, bd) * (1.0 / HEAD)
    rstd = lax.rsqrt(var + LN_X_EPS)
    yh = d * rstd
    rkk = _segsum(r * k2 * rk, bd)
    z = yh * lnw + lnb + rkk * v
    return bd, rstd, yh, rkk, z


def _rwkv_post(ysc, r, k2, v, g, ypool, lnw, lnb, rk, *, name):
    S, R = ysc.shape
    PW = ypool.shape[1]
    tm = ROW_TILE

    def body(y_ref, r_ref, k_ref, v_ref, g_ref, yp_ref, lw_ref, lb_ref, rk_ref, cat_ref):
        _, _, _, _, z = _post_common(y_ref[...], r_ref[...], k_ref[...], v_ref[...], lw_ref[...], lb_ref[...],
                                     rk_ref[...])
        cat_ref[:, 0:PW] = yp_ref[...].astype(BF16)
        cat_ref[:, PW:] = (z * g_ref[...]).astype(BF16)

    vec = _vec_spec(1, R)
    return pl.pallas_call(
        body, name=name, grid=(S // tm,),
        in_specs=[_row_spec(tm, R)] * 5 + [_row_spec(tm, PW), vec, vec, vec],
        out_specs=_row_spec(tm, PW + R), out_shape=jax.ShapeDtypeStruct((S, PW + R), BF16),
        compiler_params=_cp(("parallel",), 48),
    )(ysc, r, k2, v, g, ypool, lnw, lnb, rk)


def _rwkv_post_bwd(dcat, ysc, r, k2, v, g, lnw, lnb, rk, *, name):
    S, R = ysc.shape
    tm = ROW_TILE
    n = S // tm

    def body(d_ref, y_ref, r_ref, k_ref, v_ref, g_ref, lw_ref, lb_ref, rk_ref, dy_ref, dg_ref, drb_ref, dkb_ref,
             dvb_ref, sums_ref, acc_ref):
        i = pl.program_id(0)

        @pl.when(i == 0)
        def _():
            acc_ref[...] = jnp.zeros_like(acc_ref)

        rv, kv, vv, lw, rkw = r_ref[...], k_ref[...], v_ref[...], lw_ref[...], rk_ref[...]
        bd, rstd, yh, rkk, z = _post_common(y_ref[...], rv, kv, vv, lw, lb_ref[...], rkw)
        dyr = d_ref[...]
        dg_ref[...] = dyr * z
        dz = dyr * g_ref[...]
        dyh = dz * lw
        dy_ref[...] = rstd * (dyh - _segsum(dyh, bd) * (1.0 / HEAD) - yh * (_segsum(dyh * yh, bd) * (1.0 / HEAD)))
        dvb_ref[...] = dz * rkk
        drkk = _segsum(dz * vv, bd)
        drb_ref[...] = drkk * kv * rkw
        dkb_ref[...] = drkk * rv * rkw
        acc_ref[0] += _fold8(dz * yh)
        acc_ref[1] += _fold8(dz)
        acc_ref[2] += _fold8(drkk * rv * kv)

        @pl.when(i == n - 1)
        def _():
            for j in range(3):
                sums_ref[j:j + 1, :] = jnp.sum(acc_ref[j], axis=0, keepdims=True)

    vec = _vec_spec(1, R)
    dspec = _row_spec(tm, R)
    return pl.pallas_call(
        body, name=name, grid=(n,),
        in_specs=[dspec] + [_row_spec(tm, R)] * 5 + [vec, vec, vec],
        out_specs=tuple([_row_spec(tm, R)] * 5) + (_vec_spec(3, R),),
        out_shape=tuple([jax.ShapeDtypeStruct((S, R), F32)] * 5) + (jax.ShapeDtypeStruct((3, R), F32),),
        scratch_shapes=[pltpu.VMEM((3, 8, R), F32)],
        compiler_params=_cp(("arbitrary",), 56),
    )(dcat, ysc, r, k2, v, g, lnw, lnb, rk)


def _half_sums(x, m_a):
    s_a = jnp.sum(jnp.where(m_a, x, 0.0), axis=1, keepdims=True)
    s_b = jnp.sum(jnp.where(m_a, 0.0, x), axis=1, keepdims=True)
    return s_a, s_b


SEL_ROWS = 64


def _column_selector():
    row = lax.broadcasted_iota(jnp.int32, (SEL_ROWS, 8 * 128), 0)
    col = lax.broadcasted_iota(jnp.int32, (SEL_ROWS, 8 * 128), 1)
    head, rest = row // 32, row % 32
    hit = (rest < 24) & (rest % 8 == col // 128) & (head == (col % 128) // HEAD)
    return jnp.where(hit, 1.0, 0.0).astype(BF16)


def _expand_columns(x, sel):
    hi = x.astype(BF16).astype(F32)
    r1 = x - hi
    mid = r1.astype(BF16).astype(F32)
    lo = (r1 - mid).astype(BF16).astype(F32)
    terms = jnp.concatenate([hi, mid, lo, jnp.zeros_like(x)], axis=0)
    both = jnp.concatenate([terms, pltpu.roll(terms, HEAD, 1)], axis=0)[:, 0:HEAD]
    return lax.dot_general(both.astype(BF16), sel, (((0,), (0,)), ((), ())), preferred_element_type=F32)


def _scan_fwd(r, w, k, v, kap, b, *, name, plan=None):
    S, R = r.shape
    G, T = SCAN_G, SCAN_T
    NP = R // 128
    assert NP % G == 0 and S % T == 0
    GW = 128 * G

    def body(r_ref, w_ref, k_ref, v_ref, kap_ref, b_ref, sel_ref, y_ref, sa_ref, st_ref, s_scr, vc_scr, yt_scr,
             sat_scr):
        c = pl.program_id(1)

        @pl.when(c == 0)
        def _():
            s_scr[...] = jnp.zeros_like(s_scr)

        yt_scr[...] = jnp.zeros_like(yt_scr)
        sat_scr[...] = jnp.zeros_like(sat_scr)
        lane = lax.broadcasted_iota(jnp.int32, (HEAD, 128), 1)
        m_a = lane < HEAD

        def block(tb, carry):
            t0 = pl.multiple_of(tb * 8, 8)
            rb, wb, kb = r_ref[pl.ds(t0, 8), :], w_ref[pl.ds(t0, 8), :], k_ref[pl.ds(t0, 8), :]
            pb, bb, vb = kap_ref[pl.ds(t0, 8), :], b_ref[pl.ds(t0, 8), :], v_ref[pl.ds(t0, 8), :]
            for g in range(G):
                vc_scr[g] = _expand_columns(vb[:, g * 128:(g + 1) * 128], sel_ref[...])

            def put_y(g, parts, hot_y):
                yt_scr[g, 0:HEAD, :] = jnp.where(hot_y, parts[0], yt_scr[g, 0:HEAD, :])
                yt_scr[g, HEAD:, :] = jnp.where(hot_y, parts[1], yt_scr[g, HEAD:, :])

            def put_sa(g, parts, hot_t):
                sat_scr[g, 0:HEAD, :] = jnp.where(hot_t, -parts[0], sat_scr[g, 0:HEAD, :])
                sat_scr[g, HEAD:, :] = jnp.where(hot_t, -parts[1], sat_scr[g, HEAD:, :])

            for j in range(8):
                t = t0 + j
                cols = slice(j * 128, (j + 1) * 128)
                sa_parts, y_parts = [], []
                for g in range(G):
                    sl = slice(g * 128, (g + 1) * 128)
                    sa_parts.append(_half_sums(s_scr[g] * pb[j:j + 1, sl], m_a))
                if j > 0:
                    for g in range(G):
                        sl = slice(g * 128, (g + 1) * 128)
                        y_parts.append(_half_sums(s_scr[g] * rb[j - 1:j, sl], m_a))
                hot_t = lane == t
                for g in range(G):
                    sl = slice(g * 128, (g + 1) * 128)
                    sa = -jnp.where(m_a, sa_parts[g][0], sa_parts[g][1])
                    st = s_scr[g] * wb[j:j + 1, sl] + sa * bb[j:j + 1, sl] + vc_scr[g, :, cols] * kb[j:j + 1, sl]
                    s_scr[g] = st
                    st_ref[g, t] = st
                    put_sa(g, sa_parts[g], hot_t)
                if j > 0:
                    hot_y = lane == t - 1
                    for g in range(G):
                        put_y(g, y_parts[g], hot_y)
            hot_y = lane == t0 + 7
            for g in range(G):
                sl = slice(g * 128, (g + 1) * 128)
                put_y(g, _half_sums(s_scr[g] * rb[7:8, sl], m_a), hot_y)
            return carry

        lax.fori_loop(0, T // 8, block, 0)
        for g in range(G):
            y_ref[:, g * 128:(g + 1) * 128] = yt_scr[g].T[0:T, :]
            sa_ref[:, g * 128:(g + 1) * 128] = sat_scr[g].T[0:T, :]

    tspec = pl.BlockSpec((T, GW), lambda p, c: (c, p))
    sel_spec = pl.BlockSpec((SEL_ROWS, 8 * 128), lambda p, c: (0, 0))
    grid = (NP // G, S // T)
    x_in, x_out, x_shapes, x_scr, x_ops = _host_args(plan)
    outs = pl.pallas_call(
        _host(body, plan, 7, 3, *_grid_ends(grid)), name=name, grid=grid,
        in_specs=[tspec] * 6 + [sel_spec] + x_in,
        out_specs=tuple([tspec, tspec, pl.BlockSpec((G, T, HEAD, 128), lambda p, c: (p, c, 0, 0))] + x_out),
        out_shape=tuple([jax.ShapeDtypeStruct((S, R), F32), jax.ShapeDtypeStruct((S, R), F32),
                         jax.ShapeDtypeStruct((NP, S, HEAD, 128), F32)] + x_shapes),
        scratch_shapes=[pltpu.VMEM((G, HEAD, 128), F32), pltpu.VMEM((G, HEAD, 8 * 128), F32),
                        pltpu.VMEM((G, 128, 128), F32), pltpu.VMEM((G, 128, 128), F32)] + x_scr,
        compiler_params=_cp(_semantics(plan, ("parallel", "arbitrary")), 48),
    )(r, w, k, v, kap, b, _column_selector(), *x_ops)
    return outs[0], outs[1], outs[2], tuple(outs[3:])


def _scan_bwd(r, w, k, v, kap, b, sa, dy, states, *, name):
    S, R = r.shape
    G, T = SCAN_G, SCAN_T
    NP = R // 128
    NC = S // T
    GW = 128 * G

    def body(r_ref, w_ref, k_ref, v_ref, kap_ref, b_ref, sa_ref, dy_ref, st_ref, sp_ref, sel_ref, dr_ref, dw_ref,
             dk_ref, dv_ref, dkap_ref, db_ref, ds_scr, vc_scr, dyc_scr, sac_scr, dvt_scr):
        ci = pl.program_id(1)

        @pl.when(ci == 0)
        def _():
            ds_scr[...] = jnp.zeros_like(ds_scr)

        dvt_scr[...] = jnp.zeros_like(dvt_scr)
        lane = lax.broadcasted_iota(jnp.int32, (HEAD, 128), 1)
        m_a = lane < HEAD
        sub = lax.broadcasted_iota(jnp.int32, (8, 128), 0)
        zero_i = jnp.zeros((HEAD, 128), jnp.int32)
        has_prev = jnp.where(ci < NC - 1, 1.0, 0.0)

        def state_before(g, t):
            at_start = (zero_i + t) == 0
            return jnp.where(at_start, sp_ref[g, 0] * has_prev, st_ref[g, jnp.maximum(t - 1, 0)])

        def block(it, carry):
            tb = T // 8 - 1 - it
            t0 = pl.multiple_of(tb * 8, 8)
            rb, wb, kb = r_ref[pl.ds(t0, 8), :], w_ref[pl.ds(t0, 8), :], k_ref[pl.ds(t0, 8), :]
            pb, bb = kap_ref[pl.ds(t0, 8), :], b_ref[pl.ds(t0, 8), :]
            vb, dyb, sab = v_ref[pl.ds(t0, 8), :], dy_ref[pl.ds(t0, 8), :], sa_ref[pl.ds(t0, 8), :]
            for g in range(G):
                sl = slice(g * 128, (g + 1) * 128)
                vc_scr[g] = _expand_columns(vb[:, sl], sel_ref[...])
                dyc_scr[g] = _expand_columns(dyb[:, sl], sel_ref[...])
                sac_scr[g] = _expand_columns(sab[:, sl], sel_ref[...])
            outs = [[jnp.zeros((8, 128), F32) for _ in range(5)] for _ in range(G)]
            for j in range(7, -1, -1):
                t = t0 + j
                hot = lane == t
                cols = slice(j * 128, (j + 1) * 128)
                dsa_parts, dv_parts = [], []
                for g in range(G):
                    sl = slice(g * 128, (g + 1) * 128)
                    ds = ds_scr[g] + dyc_scr[g, :, cols] * rb[j:j + 1, sl]
                    ds_scr[g] = ds
                    dsa_parts.append(_half_sums(ds * bb[j:j + 1, sl], m_a))
                for g in range(G):
                    sl = slice(g * 128, (g + 1) * 128)
                    dv_parts.append(_half_sums(ds_scr[g] * kb[j:j + 1, sl], m_a))
                ds_parts = list(zip(dv_parts, dsa_parts))
                for g in range(G):
                    sl = slice(g * 128, (g + 1) * 128)
                    w_r, p_r = wb[j:j + 1, sl], pb[j:j + 1, sl]
                    ds = ds_scr[g]
                    s_p = state_before(g, t)
                    dr_row = jnp.sum(st_ref[g, t] * dyc_scr[g, :, cols], axis=0, keepdims=True)
                    dk_row = jnp.sum(ds * vc_scr[g, :, cols], axis=0, keepdims=True)
                    db_row = jnp.sum(ds * sac_scr[g, :, cols], axis=0, keepdims=True)
                    dw_row = jnp.sum(ds * s_p, axis=0, keepdims=True)
                    (dv_a, dv_b), (dsa_a, dsa_b) = ds_parts[g]
                    dvt_scr[g, 0:HEAD, :] = jnp.where(hot, dv_a, dvt_scr[g, 0:HEAD, :])
                    dvt_scr[g, HEAD:, :] = jnp.where(hot, dv_b, dvt_scr[g, HEAD:, :])
                    dsa = jnp.where(m_a, dsa_a, dsa_b)
                    dkap_row = -jnp.sum(s_p * dsa, axis=0, keepdims=True)
                    ds_scr[g] = ds * w_r - dsa * p_r
                    pick = sub == j
                    for q, row in enumerate((dr_row, dw_row, dk_row, dkap_row, db_row)):
                        outs[g][q] = jnp.where(pick, row, outs[g][q])
            for g in range(G):
                sl = slice(g * 128, (g + 1) * 128)
                for q, ref in enumerate((dr_ref, dw_ref, dk_ref, dkap_ref, db_ref)):
                    ref[pl.ds(t0, 8), sl] = outs[g][q]
            return carry

        lax.fori_loop(0, T // 8, block, 0)
        for g in range(G):
            dv_ref[:, g * 128:(g + 1) * 128] = dvt_scr[g].T[0:T, :]

    tspec = pl.BlockSpec((T, GW), lambda p, c: (NC - 1 - c, p))
    st_spec = pl.BlockSpec((G, T, HEAD, 128), lambda p, c: (p, NC - 1 - c, 0, 0))
    prev_spec = pl.BlockSpec((G, 1, HEAD, 128), lambda p, c: (p, jnp.maximum((NC - 1 - c) * T - 1, 0), 0, 0))
    sel_spec = pl.BlockSpec((SEL_ROWS, 8 * 128), lambda p, c: (0, 0))
    return pl.pallas_call(
        body, name=name, grid=(NP // G, NC),
        in_specs=[tspec] * 8 + [st_spec, prev_spec, sel_spec],
        out_specs=tuple([tspec] * 6),
        out_shape=tuple([jax.ShapeDtypeStruct((S, R), F32)] * 6),
        scratch_shapes=[pltpu.VMEM((G, HEAD, 128), F32), pltpu.VMEM((G, HEAD, 8 * 128), F32),
                        pltpu.VMEM((G, HEAD, 8 * 128), F32), pltpu.VMEM((G, HEAD, 8 * 128), F32),
                        pltpu.VMEM((G, 128, 128), F32)],
        compiler_params=_cp(("parallel", "arbitrary"), 48),
    )(r, w, k, v, kap, b, sa, dy, states, states, _column_selector())


def _sum_parts(parts, *, name):
    P, rows, W = parts.shape
    tr = rows
    for cand in (1024, 512, 256, 128, 64, 32, 16, 8):
        if rows % cand == 0:
            tr = cand
            break

    def body(p_ref, o_ref):
        acc = p_ref[0]
        for s in range(1, P):
            acc = acc + p_ref[s]
        o_ref[...] = acc

    return pl.pallas_call(
        body, name=name, grid=(rows // tr,),
        in_specs=[pl.BlockSpec((P, tr, W), lambda i: (0, i, 0))],
        out_specs=pl.BlockSpec((tr, W), lambda i: (i, 0)), out_shape=jax.ShapeDtypeStruct((rows, W), F32),
        compiler_params=_cp(("parallel",), 32),
    )(parts)


def _adamw(w, m, v, parts, *, name):
    R, C = w.shape
    P = parts.shape[0]
    tr = R
    for cand in (1024, 512, 256, 128, 64, 32, 16, 8):
        if R % cand == 0 and cand * C * 4 * (7 + P) <= 10 * 1024 * 1024:
            tr = cand
            break
    bc1 = 1.0 - ADAM_B1 ** ADAM_STEP
    bc2 = 1.0 - ADAM_B2 ** ADAM_STEP

    def body(w_ref, m_ref, v_ref, p_ref, g_ref, d_ref, nm_ref, nv_ref):
        g = p_ref[0].astype(F32)
        for s in range(1, P):
            g = g + p_ref[s].astype(F32)
        m1 = ADAM_B1 * m_ref[...] + (1.0 - ADAM_B1) * g
        v1 = ADAM_B2 * v_ref[...] + (1.0 - ADAM_B2) * (g * g)
        m_hat = m1 / bc1
        v_hat = v1 / bc2
        g_ref[...] = g
        d_ref[...] = -ADAM_LR * (m_hat / (jnp.sqrt(v_hat) + ADAM_EPS) + ADAM_WD * w_ref[...])
        nm_ref[...] = m1
        nv_ref[...] = v1

    spec = pl.BlockSpec((tr, C), lambda i: (i, 0))
    return pl.pallas_call(
        body, name=name, grid=(R // tr,),
        in_specs=[spec, spec, spec, pl.BlockSpec((P, tr, C), lambda i: (0, i, 0))],
        out_specs=(spec, spec, spec, spec), out_shape=tuple([jax.ShapeDtypeStruct((R, C), F32)] * 4),
        compiler_params=_cp(("parallel",), 40),
    )(w, m, v, parts)


def _cols_full(g8):
    n, rows, c = g8.shape
    return jnp.transpose(g8, (1, 0, 2)).reshape(rows, n * c)


def _cols_split(full):
    rows, cols = full.shape
    return jnp.transpose(full.reshape(rows, N_DEV, cols // N_DEV), (1, 0, 2))


def _pack(vals, rows_multiple=512):
    flat = jnp.concatenate([v.reshape(-1).astype(F32) for v in vals])
    n = flat.shape[0]
    unit = 128 * rows_multiple
    padded = ((n + unit - 1) // unit) * unit
    return jnp.pad(flat, (0, padded - n)).reshape(padded // 128, 128)


def _unpack(packed, shapes):
    flat = packed.reshape(-1)
    out, off = [], 0
    for shp in shapes:
        size = 1
        for d in shp:
            size *= d
        out.append(flat[off:off + size].reshape(shp))
        off += size
    return out


def _ffn_forward(x, weights, gpre, gpost, shift, scale1p, gw, tag, up_plan=None, down_from_plan=None):
    g8, u8, d8 = weights
    h = _pre_norm_mod(x, gpre, shift, scale1p, name=f"{tag}_pre")
    au, s, carried = _ffn_up(h, g8, u8, tm=1024, tk=2048, plan=up_plan,
                             name=f"{tag}_up" + ("_carry" if up_plan else ""))
    if down_from_plan is not None:
        d8 = carried[down_from_plan]
    f = _mm(s, d8.reshape(-1, d8.shape[2]), tm=1024, tn=1024, tk=2048, name=f"{tag}_down")
    xo = _post_norm_res(x, f, gpost, gw, name=f"{tag}_post")
    return xo, (h, au, s, f), d8, carried


def _ffn_backward(dxo, x, saved, weights, gpre, gpost, scale1p, gw, tag, plans=None, own_sums=None):
    g8, u8, d8 = weights
    h, au, s, f = saved
    plans = plans or {}

    def nm(key):
        return f"{tag}_{key}" + ("_carry" if key in plans or (key == "dh" and own_sums) else "")

    df, post_sums = _post_norm_res_bwd(dxo, f, gpost, gw, MACARON, name=f"{tag}_post_bwd")
    dwd, got_dwd = _ffn_dwd(s, df, tn=1024, tk=1024, plan=plans.get("dwd"), name=nm("dwd"))
    dau, got_down = _ffn_down_bwd(df, d8, au, tm=1024, tk=2048, plan=plans.get("down_bwd"), name=nm("down_bwd"))
    dwgu, got_dwgu = _ffn_dwgu(h, dau, tm=1024, tk=1024, plan=plans.get("dwgu"), name=nm("dwgu"))
    dh_plan = _ChipsPlan(own_sums([dwgu, dwd])) if own_sums else None
    dh, got_dh = _ffn_dh(dau, g8, u8, tm=1024, tn=1024, plan=dh_plan, name=nm("dh"))
    dx, pre_sums = _pre_norm_mod_bwd(dh, x, dxo, gpre, scale1p, name=f"{tag}_pre_bwd")
    carried = {"dwd": got_dwd, "down_bwd": got_down, "dwgu": got_dwgu, "dh": got_dh}
    return dx, dwgu, dwd, pre_sums, post_sums, carried


def kernel(x, c, w_ada, b_ada, norm_pre, norm_post, ffn1_w_gate, ffn1_w_up, ffn1_w_down, w_in, mu_shift, pool_w, pool_scale, w0, w2, a0, a2, g2, k_k, k_a, r_k, lnx_w, lnx_b, w_out, ffn2_w_gate, ffn2_w_up, ffn2_w_down, loss_target, m_w_ada, m_b_ada, m_norm_pre, m_norm_post, m_ffn1_w_gate, m_ffn1_w_up, m_ffn1_w_down, m_w_in, m_mu_shift, m_pool_w, m_pool_scale, m_w0, m_w2, m_a0, m_a2, m_g2, m_k_k, m_k_a, m_r_k, m_lnx_w, m_lnx_b, m_w_out, m_ffn2_w_gate, m_ffn2_w_up, m_ffn2_w_down, v_w_ada, v_b_ada, v_norm_pre, v_norm_post, v_ffn1_w_gate, v_ffn1_w_up, v_ffn1_w_down, v_w_in, v_mu_shift, v_pool_w, v_pool_scale, v_w0, v_w2, v_a0, v_a2, v_g2, v_k_k, v_k_a, v_r_k, v_lnx_w, v_lnx_b, v_w_out, v_ffn2_w_gate, v_ffn2_w_up, v_ffn2_w_down):
    names = ["w_ada", "b_ada", "norm_pre", "norm_post", "ffn1_w_gate", "ffn1_w_up", "ffn1_w_down", "w_in", "mu_shift",
             "pool_w", "pool_scale", "w0", "w2", "a0", "a2", "g2", "k_k", "k_a", "r_k", "lnx_w", "lnx_b", "w_out",
             "ffn2_w_gate", "ffn2_w_up", "ffn2_w_down"]
    env = dict(locals())
    W = {n: env[n][0] for n in names}
    M1 = {n: env["m_" + n][0] for n in names}
    V1 = {n: env["v_" + n][0] for n in names}

    me = _my_index()
    xs = x[0]
    tgt = loss_target[0]
    S, D = xs.shape
    F = W["ffn1_w_gate"].shape[1] * N_DEV
    R = W["w0"].shape[0]
    PW = D - R
    IN_W = W["w_in"].shape[1] * N_DEV
    P_W = F
    QW = P_W - PW
    NMOD = 9 * D
    ada_c = W["w_ada"].shape[1]

    c_all, npre8, npost8, w2_8, a2_8, g2_8 = _exchange(
        [c, W["norm_pre"], W["norm_post"], W["w2"].astype(BF16), W["a2"].astype(BF16), W["g2"].astype(BF16)],
        scatter=False, name="gather_small")
    c_all = c_all.reshape(N_DEV, D)
    gpre = _cols_full(npre8)
    gpost = _cols_full(npost8)
    wl = jnp.zeros((LORA_PAD, 3 * R), BF16)
    wl = wl.at[0:LORA_W, 0:R].set(_cols_full(w2_8))
    wl = wl.at[LORA_W:LORA_W + LORA_A, R:2 * R].set(_cols_full(a2_8))
    wl = wl.at[LORA_W + LORA_A:LORA_W + LORA_A + LORA_G, 2 * R:3 * R].set(_cols_full(g2_8))

    sc_all = jax.nn.silu(c_all)
    sc_pad = jnp.concatenate([sc_all, jnp.zeros((8, D), F32)], axis=0).astype(BF16)
    modcols = _mm(sc_pad, W["w_ada"], tm=16, tn=ada_c, tk=256, name="ada_fwd")[0:N_DEV]
    modcols = modcols + lax.dynamic_slice(W["b_ada"], (me * ada_c,), (ada_c,))[None, :]
    (mod8,) = _exchange([modcols], scatter=False, name="gather_mod")
    mod = lax.dynamic_index_in_dim(mod8, me, axis=1, keepdims=False).reshape(9, D)

    def mod_row(i):
        return mod[i:i + 1, :]

    f_pad = FF_TILE - F // N_DEV

    def ffn_shards(tag):
        return [jnp.pad(W[f"{tag}_w_gate"].astype(BF16), ((0, 0), (0, f_pad))),
                jnp.pad(W[f"{tag}_w_up"].astype(BF16), ((0, 0), (0, f_pad))),
                jnp.pad(W[f"{tag}_w_down"].astype(BF16), ((0, f_pad), (0, 0)))]

    ffn1_shards = ffn_shards("ffn1")
    g8_1, u8_1 = _gather_two_level(ffn1_shards[:2], name="gather_ffn_up")
    up_plan = _GatherPlan([W["w_in"].astype(BF16), ffn1_shards[2]])
    scan_plan = _GatherPlan(ffn_shards("ffn2") + [W["w_out"].astype(BF16)])

    mu_p = jnp.pad(W["mu_shift"], (0, QW - W["mu_shift"].shape[0]))[None, :]
    vec = lambda a: a.reshape(1, -1)
    w0r, a0r, kkr, kar = vec(W["w0"]), vec(W["a0"]), vec(W["k_k"]), vec(W["k_a"])
    lnw, lnb, rkr = vec(W["lnx_w"]), vec(W["lnx_b"]), vec(W["r_k"])
    pscale = vec(W["pool_scale"])

    sc1p = [1.0 + mod_row(3 * s + 1) for s in range(3)]
    shifts = [mod_row(3 * s) for s in range(3)]
    wgts = [MACARON, 1.0, MACARON]
    gws = [wgts[s] * (1.0 + mod_row(3 * s + 2)) for s in range(3)]
    gp = [gpre[s:s + 1] for s in range(3)]
    gq = [gpost[s:s + 1] for s in range(3)]

    x1, sv1, d8_1, (win8, _) = _ffn_forward(xs, (g8_1, u8_1, None), gp[0], gq[0], shifts[0], sc1p[0], gws[0], "ffn",
                                            up_plan=up_plan, down_from_plan=1)
    ffn1_w = (g8_1, u8_1, d8_1)
    w_in_p = jnp.pad(_cols_full(win8), ((0, 0), (0, P_W - IN_W)))

    h2 = _pre_norm_mod(x1, gp[1], shifts[1], sc1p[1], name="mix_pre")
    p = _mm(h2, w_in_p, tm=1024, tn=512, tk=2048, name="mix_in")
    q = p[:, PW:]
    o_pool, y_pool = _pool_fwd(p, W["pool_w"], pscale, name="pool_fwd")
    r_s, w_s, k_s, v_s, kap_s, b_s, g_s = _rwkv_prep(q, mu_p, wl, w0r, a0r, kkr, kar, name="rwkv_prep")
    y_scan, sa_s, states, gathered = _scan_fwd(r_s, w_s, k_s, v_s, kap_s, b_s, name="scan_fwd", plan=scan_plan)
    ffn2_w = gathered[:3]
    w_out_f = gathered[3].reshape(D, D)
    cat = _rwkv_post(y_scan, r_s, k_s, v_s, g_s, y_pool, lnw, lnb, rkr, name="rwkv_post")
    f2 = _mm(cat, w_out_f, tm=1024, tn=1024, tk=2048, name="mix_out")
    x2 = _post_norm_res(x1, f2, gq[1], gws[1], name="mix_post")

    x3, sv3, _, _ = _ffn_forward(x2, ffn2_w, gp[2], gq[2], shifts[2], sc1p[2], gws[2], "ffn")

    loss_part, dx3 = _loss_head(x3, tgt, name="loss_head")
    loss = lax.psum(loss_part[0, 0], MESH_AXES)

    def by_core_chip(blocks):
        shp = blocks.shape
        t = blocks.astype(BF16).reshape((N_DEV // 2, 2) + shp[1:])
        return jnp.swapaxes(t, 0, 1)

    def chip_sums(mine, tag):
        got = _sibling_swap(mine, name=f"{tag}_swap")
        return [_pair_add(m, g, name=f"{tag}_add{i}") for i, (m, g) in enumerate(zip(mine, got))]

    def ffn_parts(pgu, pd):
        fs = F // N_DEV
        return pgu[:, :D, :fs], pgu[:, D:, :fs], pd[:, :fs, :]

    dx2, dwgu2, dwd2, pre3, post3, _ = _ffn_backward(dx3, x2, sv3, ffn2_w, gp[2], gq[2], sc1p[2], gws[2], "ffn")
    sums2_gu, sums2_d = chip_sums([dwgu2, dwd2], "scatter_ffn")

    df2, post2 = _post_norm_res_bwd(dx2, f2, gq[1], gws[1], 1.0, name="mix_post_bwd")
    dw_out = _mm(cat, df2, ta=True, tm=1024, tn=1024, tk=1024, name="mix_dwout")
    dcat = _mm(df2, w_out_f, tb=True, tm=1024, tn=1024, tk=2048, name="mix_dcat")
    dyr = dcat[:, PW:]
    dysc, dg, dr_b, dk2_b, dv_b, post_sums = _rwkv_post_bwd(dyr, y_scan, r_s, k_s, v_s, g_s, lnw, lnb, rkr,
                                                             name="rwkv_post_bwd")
    dr, dw, dk2, dv, dkap, db = _scan_bwd(r_s, w_s, k_s, v_s, kap_s, b_s, sa_s, dysc, states, name="scan_bwd")
    dps, dwl, prep_sums = _rwkv_prep_bwd(q, mu_p, wl, w0r, a0r, kkr, kar,
                                         (dr, dw, dk2, dv, dkap, db, dg, dr_b, dk2_b, dv_b), name="rwkv_prep_bwd")
    dq, dmu = _tshift_bwd(dps, q, mu_p, name="tshift_bwd")
    du_pool, dpool_w, dpool_scale = _pool_bwd(dcat, o_pool, W["pool_w"], pscale, name="pool_bwd")
    dp = jnp.concatenate([du_pool, dq], axis=1)
    dw_in = _mm(h2, dp, ta=True, tm=1024, tn=512, tk=1024, name="mix_dwin")
    dh2 = _mm(dp, w_in_p, tb=True, tm=1024, tn=1024, tk=2816, name="mix_dh")
    dx1, pre2 = _pre_norm_mod_bwd(dh2, x1, dx2, gp[1], sc1p[1], name="mix_pre_bwd")

    sums_mix = chip_sums([by_core_chip(_cols_split(dw_in[:, :IN_W])),
                          by_core_chip(dw_out.reshape(N_DEV, D // N_DEV, D))], "scatter_mixer")
    dx0, _, _, pre1, post1, got = _ffn_backward(
        dx1, xs, sv1, ffn1_w, gp[0], gq[0], sc1p[0], gws[0], "ffn",
        plans={"dwd": _ChipsPlan([sums2_d]), "down_bwd": _ChipsPlan(sums_mix), "dwgu": _ChipsPlan([sums2_gu])},
        own_sums=lambda blocks: chip_sums(blocks, "scatter_ffn"))

    pres, posts = [pre1, pre2, pre3], [post1, post2, post3]
    dmod = jnp.stack([jnp.stack([pres[s][0], pres[s][1], posts[s][0]]) for s in range(3)]).reshape(NMOD // 128, 128)
    dnorm_pre = jnp.stack([pres[s][2] for s in range(3)])
    dnorm_post = jnp.stack([posts[s][1] for s in range(3)])

    small = [dmu[0, :W["mu_shift"].shape[0]], dpool_w, dpool_scale, prep_sums[0], prep_sums[1], prep_sums[2],
             prep_sums[3], post_sums[2], post_sums[0], post_sums[1], dnorm_pre, dnorm_post,
             dwl[0:LORA_W, 0:R], dwl[LORA_W:LORA_W + LORA_A, R:2 * R],
             dwl[LORA_W + LORA_A:LORA_W + LORA_A + LORA_G, 2 * R:3 * R]]
    small_shapes = [a.shape for a in small]
    dmod8, small8 = _gather_two_level([dmod, _pack(small)], name="gather_grads")
    g_b_ada = _sum_parts(dmod8, name="sum_dmod").reshape(NMOD)
    red = _unpack(_sum_parts(small8, name="sum_small"), small_shapes)
    (g_mu, g_pool_w, g_pool_scale, g_w0, g_a0, g_kk, g_ka, g_rk, g_lnw, g_lnb, g_npre, g_npost, g_w2, g_a2,
     g_g2) = red

    dmod_all = dmod8.reshape(N_DEV, NMOD)
    dmod_cols = lax.dynamic_slice(dmod_all, (0, me * ada_c), (N_DEV, ada_c))
    dmod_cols = jnp.concatenate([dmod_cols, jnp.zeros_like(dmod_cols)], axis=0)
    g_w_ada = _mm(sc_pad, dmod_cols, ta=True, tm=D, tn=ada_c // 9, tk=16, name="ada_bwd")

    pg2, pu2, pd2 = ffn_parts(got["dwgu"][0], got["dwd"][0])
    pin, pout = got["down_bwd"]
    pg1, pu1, pd1 = ffn_parts(*got["dh"])

    res = {}

    def big(nm, parts, tag):
        res[nm] = _adamw(W[nm], M1[nm], V1[nm], parts, name=tag)

    big("ffn1_w_gate", pg1, "adamw_cols")
    big("ffn1_w_up", pu1, "adamw_cols")
    big("ffn1_w_down", pd1, "adamw_rows")
    big("ffn2_w_gate", pg2, "adamw_cols")
    big("ffn2_w_up", pu2, "adamw_cols")
    big("ffn2_w_down", pd2, "adamw_rows")
    big("w_in", pin, "adamw_w_in")
    big("w_out", pout, "adamw_w_out")
    big("w_ada", g_w_ada[None], "adamw_w_ada")

    def my_cols(full, width):
        return lax.dynamic_slice_in_dim(full, me * width, width, axis=full.ndim - 1)

    small_names = ["b_ada", "mu_shift", "pool_w", "pool_scale", "w0", "a0", "k_k", "k_a", "r_k", "lnx_w", "lnx_b",
                   "norm_pre", "norm_post", "w2", "a2", "g2"]
    small_grads = [g_b_ada, g_mu, g_pool_w, g_pool_scale, g_w0, g_a0, g_kk, g_ka, g_rk.reshape(W["r_k"].shape), g_lnw,
                   g_lnb, my_cols(g_npre, D // N_DEV), my_cols(g_npost, D // N_DEV), my_cols(g_w2, R // N_DEV),
                   my_cols(g_a2, R // N_DEV), my_cols(g_g2, R // N_DEV)]
    shapes = [W[n].shape for n in small_names]
    packed = _adamw(_pack([W[n] for n in small_names]), _pack([M1[n] for n in small_names]),
                    _pack([V1[n] for n in small_names]), _pack(small_grads)[None], name="adamw_small")
    unpacked = [_unpack(t, shapes) for t in packed]
    for i, nm in enumerate(small_names):
        res[nm] = tuple(unpacked[k][i] for k in range(4))

    outs = [loss, dx0[None]]
    for k in range(4):
        outs.extend(res[nm][k][None] for nm in names)
    return tuple(outs)
```

```python
import functools

import jax
import jax.numpy as jnp
from jax import lax
from jax.experimental import pallas as pl
from jax.experimental.pallas import tpu as pltpu

F32 = jnp.float32
BF16 = jnp.bfloat16
N_DEV = 8
MESH_AXES = ("x", "y", "c")

NORM_EPS = 1e-6
HEAD = 64
LN_X_EPS = 1e-5 * HEAD
POOL_GROUPS = 4
POOL_GROUP = 128
MACARON = 0.5
LORA_W, LORA_A, LORA_G = 64, 64, 224
LORA_PAD = 384
ADAM_LR, ADAM_B1, ADAM_B2, ADAM_EPS, ADAM_WD, ADAM_STEP = 0.001, 0.9, 0.999, 1e-08, 0.01, 10

FF_TILE = 768
ROW_TILE = 256
SCAN_T = 64
SCAN_G = 6
SCAN_G_BWD = 6
VMEM_CAP = 56 * 1024 * 1024


def _cp(sem, vmem_mb):
    return pltpu.CompilerParams(dimension_semantics=sem, vmem_limit_bytes=min(vmem_mb * 1024 * 1024, VMEM_CAP))


def _my_index():
    return 4 * lax.axis_index("x") + 2 * lax.axis_index("y") + lax.axis_index("c")


def _exchange(arrays, *, scatter, name):
    n = len(arrays)
    out_shapes = []
    for a in arrays:
        shp = a.shape if scatter else (N_DEV,) + a.shape
        out_shapes.append(jax.ShapeDtypeStruct(shp, a.dtype))

    def body(*refs):
        ins, outs = refs[:n], refs[n:2 * n]
        send_sems, recv_sems, local_sems = refs[2 * n:]
        me = _my_index()

        def dev(p):
            return (p // 4, (p // 2) % 2, p % 2)

        def copy(i, d):
            peer = (me + d) % N_DEV
            src = ins[i].at[peer] if scatter else ins[i]
            return pltpu.make_async_remote_copy(
                src_ref=src, dst_ref=outs[i].at[me], send_sem=send_sems.at[i, d - 1],
                recv_sem=recv_sems.at[i, d - 1], device_id=dev(peer), device_id_type=pl.DeviceIdType.MESH)

        def arrival(i, d):
            frm = (me + N_DEV - d) % N_DEV
            src = ins[i].at[frm] if scatter else ins[i]
            return pltpu.make_async_remote_copy(
                src_ref=src, dst_ref=outs[i].at[frm], send_sem=send_sems.at[i, d - 1],
                recv_sem=recv_sems.at[i, d - 1], device_id=dev(frm), device_id_type=pl.DeviceIdType.MESH)

        locals_ = []
        for i in range(n):
            src = ins[i].at[me] if scatter else ins[i]
            lc = pltpu.make_async_copy(src, outs[i].at[me], local_sems.at[i])
            lc.start()
            locals_.append(lc)
        sends = [copy(i, d) for d in range(1, N_DEV) for i in range(n)]
        for cp in sends:
            cp.start()
        for d in range(1, N_DEV):
            for i in range(n):
                arrival(i, d).wait_recv()
        for cp in sends:
            cp.wait_send()
        for lc in locals_:
            lc.wait()

    hbm = pl.BlockSpec(memory_space=pltpu.HBM)
    return pl.pallas_call(
        body, name=name, out_shape=tuple(out_shapes), in_specs=[hbm] * n, out_specs=tuple([hbm] * n),
        scratch_shapes=[pltpu.SemaphoreType.DMA((n, N_DEV - 1)), pltpu.SemaphoreType.DMA((n, N_DEV - 1)),
                        pltpu.SemaphoreType.DMA((n,))],
    )(*arrays)


def _remote(src, dst, send_sem, recv_sem, to):
    return pltpu.make_async_remote_copy(src_ref=src, dst_ref=dst, send_sem=send_sem, recv_sem=recv_sem,
                                        device_id=to, device_id_type=pl.DeviceIdType.MESH)


class _GatherPlan:
    def __init__(self, arrays):
        self.arrays = list(arrays)
        self.n = len(arrays)
        self.out_shapes = [jax.ShapeDtypeStruct((N_DEV,) + a.shape, a.dtype) for a in arrays]
        self.scratch = [pltpu.SemaphoreType.DMA((self.n, 7)), pltpu.SemaphoreType.DMA((self.n, 7)),
                        pltpu.SemaphoreType.DMA((self.n,))]

    def _parts(self, ins, outs, sems):
        send_sems, recv_sems, local_sems = sems
        x, y, c = lax.axis_index("x"), lax.axis_index("y"), lax.axis_index("c")
        chips = [(1 - x, y), (x, 1 - y), (1 - x, 1 - y)]

        def slot(i, px, py, pc):
            return outs[i].at[4 * px + 2 * py + pc]

        def copy(i, k, block, to, src=None):
            dst = slot(i, *block)
            return _remote(dst if src is None else src, dst, send_sems.at[i, k], recv_sems.at[i, k], to)

        n = self.n
        locals_ = [pltpu.make_async_copy(ins[i], slot(i, x, y, c), local_sems.at[i]) for i in range(n)]
        first = [copy(i, 1 + j, (x, y, c), (*chip, c), src=ins[i]) for j, chip in enumerate(chips) for i in range(n)]
        first += [copy(i, 0, (x, y, c), (x, y, 1 - c), src=ins[i]) for i in range(n)]
        return (x, y, c), chips, copy, locals_, first

    def start(self, ins, outs, sems):
        _, _, _, locals_, first = self._parts(ins, outs, sems)
        for lc in locals_:
            lc.start()
        for cp in first:
            cp.start()

    def finish(self, ins, outs, sems):
        (x, y, c), chips, copy, locals_, first = self._parts(ins, outs, sems)
        forwards = []
        for j, chip in enumerate(chips):
            for i in range(self.n):
                copy(i, 1 + j, (*chip, c), (x, y, c)).wait_recv()
                fwd = copy(i, 4 + j, (*chip, c), (x, y, 1 - c))
                fwd.start()
                forwards.append(fwd)
        for i in range(self.n):
            copy(i, 0, (x, y, 1 - c), (x, y, c)).wait_recv()
        for j, chip in enumerate(chips):
            for i in range(self.n):
                copy(i, 4 + j, (*chip, 1 - c), (x, y, c)).wait_recv()
        for cp in first + forwards:
            cp.wait_send()
        for lc in locals_:
            lc.wait()


class _ChipsPlan:
    def __init__(self, arrays):
        self.arrays = list(arrays)
        self.n = len(arrays)
        self.out_shapes = [jax.ShapeDtypeStruct(a.shape, a.dtype) for a in arrays]
        self.scratch = [pltpu.SemaphoreType.DMA((self.n, 3)), pltpu.SemaphoreType.DMA((self.n, 3)),
                        pltpu.SemaphoreType.DMA((self.n,))]

    def _parts(self, ins, outs, sems):
        send_sems, recv_sems, local_sems = sems
        x, y, c = lax.axis_index("x"), lax.axis_index("y"), lax.axis_index("c")
        mine = 2 * x + y
        chips = [(1 - x, y), (x, 1 - y), (1 - x, 1 - y)]
        n = self.n
        locals_ = [pltpu.make_async_copy(ins[i].at[mine], outs[i].at[mine], local_sems.at[i]) for i in range(n)]
        sends = [_remote(ins[i].at[2 * chip[0] + chip[1]], outs[i].at[mine], send_sems.at[i, j], recv_sems.at[i, j],
                         (*chip, c)) for j, chip in enumerate(chips) for i in range(n)]

        def arrivals():
            return [_remote(ins[i].at[2 * chip[0] + chip[1]], outs[i].at[2 * chip[0] + chip[1]], send_sems.at[i, j],
                            recv_sems.at[i, j], (*chip, c)) for j, chip in enumerate(chips) for i in range(n)]

        return locals_, sends, arrivals

    def start(self, ins, outs, sems):
        locals_, sends, _ = self._parts(ins, outs, sems)
        for lc in locals_:
            lc.start()
        for cp in sends:
            cp.start()

    def finish(self, ins, outs, sems):
        locals_, sends, arrivals = self._parts(ins, outs, sems)
        for cp in arrivals():
            cp.wait_recv()
        for cp in sends:
            cp.wait_send()
        for lc in locals_:
            lc.wait()


def _run_plan(plan, *, name):
    n = plan.n

    def body(*refs):
        ins, outs, sems = refs[:n], refs[n:2 * n], refs[2 * n:]
        plan.start(ins, outs, sems)
        plan.finish(ins, outs, sems)

    hbm = pl.BlockSpec(memory_space=pltpu.HBM)
    return pl.pallas_call(
        body, name=name, out_shape=tuple(plan.out_shapes), in_specs=[hbm] * n, out_specs=tuple([hbm] * n),
        scratch_shapes=plan.scratch,
    )(*plan.arrays)


def _host(body, plan, n_in, n_out, is_first, is_last):
    if plan is None:
        return body
    m = plan.n

    def wrapped(*refs):
        a, b = n_in, n_in + m
        c, d = b + n_out, b + n_out + m
        own_in, c_in, own_out, c_out, rest = refs[:a], refs[a:b], refs[b:c], refs[c:d], refs[d:]
        own_scr, c_sems = rest[:len(rest) - 3], rest[len(rest) - 3:]

        @pl.when(is_first())
        def _():
            plan.start(c_in, c_out, c_sems)

        body(*own_in, *own_out, *own_scr)

        @pl.when(is_last())
        def _():
            plan.finish(c_in, c_out, c_sems)

    return wrapped


def _host_args(plan):
    if plan is None:
        return [], [], [], [], []
    hbm = pl.BlockSpec(memory_space=pltpu.HBM)
    return [hbm] * plan.n, [hbm] * plan.n, list(plan.out_shapes), list(plan.scratch), list(plan.arrays)


def _gather_two_level(arrays, *, name):
    return _run_plan(_GatherPlan(arrays), name=name)


def _sibling_swap(arrays, *, name):
    n = len(arrays)
    out_shapes = [jax.ShapeDtypeStruct(a.shape[1:], a.dtype) for a in arrays]

    def body(*refs):
        ins, outs = refs[:n], refs[n:2 * n]
        send_sems, recv_sems = refs[2 * n:]
        x, y, c = lax.axis_index("x"), lax.axis_index("y"), lax.axis_index("c")
        copies = [_remote(ins[i].at[1 - c], outs[i], send_sems.at[i], recv_sems.at[i], (x, y, 1 - c))
                  for i in range(n)]
        for cp in copies:
            cp.start()
        for cp in copies:
            cp.wait_recv()
        for cp in copies:
            cp.wait_send()

    hbm = pl.BlockSpec(memory_space=pltpu.HBM)
    return pl.pallas_call(
        body, name=name, out_shape=tuple(out_shapes), in_specs=[hbm] * n, out_specs=tuple([hbm] * n),
        scratch_shapes=[pltpu.SemaphoreType.DMA((n,)), pltpu.SemaphoreType.DMA((n,))],
    )(*arrays)


def _chips_all_to_all(arrays, *, name):
    return _run_plan(_ChipsPlan(arrays), name=name)


def _pair_add(mine, got, *, name):
    _, nq, R, C = mine.shape
    tr = R
    for cand in (512, 256, 128, 64, 32, 16):
        if R % cand == 0 and cand * C * 2 * 3 * 2 <= 12 * 1024 * 1024:
            tr = cand
            break

    def body(core_ref, m_ref, g_ref, o_ref):
        o_ref[0] = (m_ref[0, 0].astype(F32) + g_ref[0].astype(F32)).astype(BF16)

    core = lax.axis_index("c").astype(jnp.int32).reshape(1)
    return pl.pallas_call(
        body, name=name,
        grid_spec=pltpu.PrefetchScalarGridSpec(
            num_scalar_prefetch=1, grid=(nq, R // tr),
            in_specs=[pl.BlockSpec((1, 1, tr, C), lambda q, i, core_ref: (core_ref[0], q, i, 0)),
                      pl.BlockSpec((1, tr, C), lambda q, i, core_ref: (q, i, 0))],
            out_specs=pl.BlockSpec((1, tr, C), lambda q, i, core_ref: (q, i, 0))),
        out_shape=jax.ShapeDtypeStruct((nq, R, C), BF16),
        compiler_params=_cp(("parallel", "parallel"), 40),
    )(core, mine, got)


def _mm(a, b, *, ta=False, tb=False, tm, tn, tk, out_dtype=F32, name):
    M = a.shape[1] if ta else a.shape[0]
    K = a.shape[0] if ta else a.shape[1]
    N = b.shape[0] if tb else b.shape[1]
    tm, tn, tk = min(tm, M), min(tn, N), min(tk, K)
    assert M % tm == 0 and N % tn == 0 and K % tk == 0, (name, M, N, K, tm, tn, tk)
    nk = K // tk
    dims = (((0 if ta else 1,), (1 if tb else 0,)), ((), ()))

    def body(a_ref, b_ref, o_ref, acc_ref):
        k = pl.program_id(2)

        @pl.when(k == 0)
        def _():
            acc_ref[...] = jnp.zeros_like(acc_ref)

        acc_ref[...] += lax.dot_general(a_ref[...].astype(BF16), b_ref[...].astype(BF16), dims,
                                        preferred_element_type=F32)

        @pl.when(k == nk - 1)
        def _():
            o_ref[...] = acc_ref[...].astype(out_dtype)

    a_spec = pl.BlockSpec((tk, tm), lambda i, j, k: (k, i)) if ta else pl.BlockSpec((tm, tk), lambda i, j, k: (i, k))
    b_spec = pl.BlockSpec((tn, tk), lambda i, j, k: (j, k)) if tb else pl.BlockSpec((tk, tn), lambda i, j, k: (k, j))
    blk = 2 * (tm * tk * a.dtype.itemsize + tk * tn * b.dtype.itemsize + tm * tn * jnp.dtype(out_dtype).itemsize)
    return pl.pallas_call(
        body, name=name, grid=(M // tm, N // tn, nk), in_specs=[a_spec, b_spec],
        out_specs=pl.BlockSpec((tm, tn), lambda i, j, k: (i, j)),
        out_shape=jax.ShapeDtypeStruct((M, N), out_dtype),
        scratch_shapes=[pltpu.VMEM((tm, tn), F32)],
        compiler_params=_cp(("parallel", "parallel", "arbitrary"), (blk + tm * tn * 4) // (1024 * 1024) + 12),
    )(a, b)


def _grid_ends(grid):
    def is_first():
        ok = pl.program_id(0) == 0
        for ax in range(1, len(grid)):
            ok = ok & (pl.program_id(ax) == 0)
        return ok

    def is_last():
        ok = pl.program_id(0) == grid[0] - 1
        for ax in range(1, len(grid)):
            ok = ok & (pl.program_id(ax) == grid[ax] - 1)
        return ok

    return is_first, is_last


def _semantics(plan, sem):
    return sem if plan is None else tuple("arbitrary" for _ in sem)


def _ffn_up(h, g8, u8, *, tm, tk, name, plan=None):
    S, D = h.shape
    nb, _, tn = g8.shape
    tm = min(tm, S)
    tk = min(tk, D)
    nk = D // tk

    def body(h_ref, g_ref, u_ref, au_ref, s_ref, acc_ref):
        k = pl.program_id(2)

        @pl.when(k == 0)
        def _():
            acc_ref[...] = jnp.zeros_like(acc_ref)

        hv = h_ref[...]
        acc_ref[:, :tn] += jnp.dot(hv, g_ref[0], preferred_element_type=F32)
        acc_ref[:, tn:] += jnp.dot(hv, u_ref[0], preferred_element_type=F32)

        @pl.when(k == nk - 1)
        def _():
            acc = acc_ref[...]
            a = acc[:, :tn]
            u = acc[:, tn:]
            au_ref[...] = acc.astype(BF16)
            s_ref[...] = (a * jax.nn.sigmoid(a) * u).astype(BF16)

    wspec = pl.BlockSpec((1, tk, tn), lambda i, j, k: (j, k, 0))
    grid = (S // tm, nb, nk)
    x_in, x_out, x_shapes, x_scr, x_ops = _host_args(plan)
    outs = pl.pallas_call(
        _host(body, plan, 3, 2, *_grid_ends(grid)), name=name, grid=grid,
        in_specs=[pl.BlockSpec((tm, tk), lambda i, j, k: (i, k)), wspec, wspec] + x_in,
        out_specs=tuple([pl.BlockSpec((tm, 2 * tn), lambda i, j, k: (i, j)),
                         pl.BlockSpec((tm, tn), lambda i, j, k: (i, j))] + x_out),
        out_shape=tuple([jax.ShapeDtypeStruct((S, 2 * nb * tn), BF16), jax.ShapeDtypeStruct((S, nb * tn), BF16)]
                        + x_shapes),
        scratch_shapes=[pltpu.VMEM((tm, 2 * tn), F32)] + x_scr,
        compiler_params=_cp(_semantics(plan, ("parallel", "parallel", "arbitrary")), 52),
    )(h, g8, u8, *x_ops)
    return outs[0], outs[1], tuple(outs[2:])


def _ffn_dh(dau, g8, u8, *, tm, tn, name, plan=None):
    S = dau.shape[0]
    nb, D, tf = g8.shape
    tm, tn = min(tm, S), min(tn, D)
    nk = 2 * nb
    nt = (((1,), (1,)), ((), ()))

    def body(a_ref, g_ref, u_ref, o_ref, acc_ref):
        k = pl.program_id(2)

        @pl.when(k == 0)
        def _():
            acc_ref[...] = jnp.zeros_like(acc_ref)

        @pl.when(k % 2 == 0)
        def _():
            acc_ref[...] += lax.dot_general(a_ref[...], g_ref[0], nt, preferred_element_type=F32)

        @pl.when(k % 2 == 1)
        def _():
            acc_ref[...] += lax.dot_general(a_ref[...], u_ref[0], nt, preferred_element_type=F32)

        @pl.when(k == nk - 1)
        def _():
            o_ref[...] = acc_ref[...]

    wspec = pl.BlockSpec((1, tn, tf), lambda i, n, k: (k // 2, n, 0))
    grid = (S // tm, D // tn, nk)
    x_in, x_out, x_shapes, x_scr, x_ops = _host_args(plan)
    outs = pl.pallas_call(
        _host(body, plan, 3, 1, *_grid_ends(grid)), name=name, grid=grid,
        in_specs=[pl.BlockSpec((tm, tf), lambda i, n, k: (i, k)), wspec, wspec] + x_in,
        out_specs=tuple([pl.BlockSpec((tm, tn), lambda i, n, k: (i, n))] + x_out),
        out_shape=tuple([jax.ShapeDtypeStruct((S, D), F32)] + x_shapes),
        scratch_shapes=[pltpu.VMEM((tm, tn), F32)] + x_scr,
        compiler_params=_cp(_semantics(plan, ("parallel", "parallel", "arbitrary")), 40),
    )(dau, g8, u8, *x_ops)
    return outs[0], tuple(outs[1:])


def _ffn_dwgu(h, dau, *, tm, tk, name, plan=None):
    S, D = h.shape
    tf = FF_TILE
    nt = dau.shape[1] // tf
    tm, tk = min(tm, D), min(tk, S)
    nk = S // tk
    ni = D // tm

    def body(a_ref, b_ref, o_ref, acc_ref):
        k = pl.program_id(2)

        @pl.when(k == 0)
        def _():
            acc_ref[...] = jnp.zeros_like(acc_ref)

        acc_ref[...] += lax.dot_general(a_ref[...], b_ref[...], (((0,), (0,)), ((), ())), preferred_element_type=F32)

        @pl.when(k == nk - 1)
        def _():
            o_ref[0, 0] = acc_ref[...].astype(BF16)

    grid = (ni, nt, nk)
    x_in, x_out, x_shapes, x_scr, x_ops = _host_args(plan)
    outs = pl.pallas_call(
        _host(body, plan, 2, 1, *_grid_ends(grid)), name=name, grid=grid,
        in_specs=[pl.BlockSpec((tk, tm), lambda i, j, k: (k, i)), pl.BlockSpec((tk, tf), lambda i, j, k: (k, j))] + x_in,
        out_specs=tuple([pl.BlockSpec((1, 1, tm, tf), lambda i, j, k: ((j // 2) % 2, j // 4, (j % 2) * ni + i, 0))]
                        + x_out),
        out_shape=tuple([jax.ShapeDtypeStruct((2, nt // 4, 2 * D, tf), BF16)] + x_shapes),
        scratch_shapes=[pltpu.VMEM((tm, tf), F32)] + x_scr,
        compiler_params=_cp(_semantics(plan, ("parallel", "parallel", "arbitrary")), 40),
    )(h, dau, *x_ops)
    return outs[0], tuple(outs[1:])


def _ffn_dwd(s, df, *, tn, tk, name, plan=None):
    S, D = df.shape
    tf = FF_TILE
    nb = s.shape[1] // tf
    tn, tk = min(tn, D), min(tk, S)
    nk = S // tk

    def body(a_ref, b_ref, o_ref, acc_ref):
        k = pl.program_id(2)

        @pl.when(k == 0)
        def _():
            acc_ref[...] = jnp.zeros_like(acc_ref)

        acc_ref[...] += lax.dot_general(a_ref[...], b_ref[...], (((0,), (0,)), ((), ())), preferred_element_type=F32)

        @pl.when(k == nk - 1)
        def _():
            o_ref[0, 0] = acc_ref[...].astype(BF16)

    grid = (nb, D // tn, nk)
    x_in, x_out, x_shapes, x_scr, x_ops = _host_args(plan)
    outs = pl.pallas_call(
        _host(body, plan, 2, 1, *_grid_ends(grid)), name=name, grid=grid,
        in_specs=[pl.BlockSpec((tk, tf), lambda j, n, k: (k, j)), pl.BlockSpec((tk, tn), lambda j, n, k: (k, n))] + x_in,
        out_specs=tuple([pl.BlockSpec((1, 1, tf, tn), lambda j, n, k: (j % 2, j // 2, 0, n))] + x_out),
        out_shape=tuple([jax.ShapeDtypeStruct((2, nb // 2, tf, D), BF16)] + x_shapes),
        scratch_shapes=[pltpu.VMEM((tf, tn), F32)] + x_scr,
        compiler_params=_cp(_semantics(plan, ("parallel", "parallel", "arbitrary")), 40),
    )(s, df, *x_ops)
    return outs[0], tuple(outs[1:])


def _ffn_down_bwd(df, d8, au, *, tm, tk, name, plan=None):
    S, D = df.shape
    nb, tn, _ = d8.shape
    F = nb * tn
    tm = min(tm, S)
    tk = min(tk, D)
    nk = D // tk

    def body(df_ref, w_ref, au_ref, dau_ref, acc_ref):
        k = pl.program_id(2)

        @pl.when(k == 0)
        def _():
            acc_ref[...] = jnp.zeros_like(acc_ref)

        acc_ref[...] += lax.dot_general(df_ref[...], w_ref[0], (((1,), (1,)), ((), ())), preferred_element_type=F32)

        @pl.when(k == nk - 1)
        def _():
            ds = acc_ref[...]
            au_v = au_ref[...].astype(F32)
            a = au_v[:, :tn]
            u = au_v[:, tn:]
            sg = jax.nn.sigmoid(a)
            da = ds * u * (sg * (1.0 + a * (1.0 - sg)))
            du = ds * (a * sg)
            dau_ref[:, :tn] = da.astype(BF16)
            dau_ref[:, tn:] = du.astype(BF16)

    grid = (S // tm, F // tn, nk)
    x_in, x_out, x_shapes, x_scr, x_ops = _host_args(plan)
    outs = pl.pallas_call(
        _host(body, plan, 3, 1, *_grid_ends(grid)), name=name, grid=grid,
        in_specs=[pl.BlockSpec((tm, tk), lambda i, j, k: (i, k)), pl.BlockSpec((1, tn, tk), lambda i, j, k: (j, 0, k)),
                  pl.BlockSpec((tm, 2 * tn), lambda i, j, k: (i, j))] + x_in,
        out_specs=tuple([pl.BlockSpec((tm, 2 * tn), lambda i, j, k: (i, j))] + x_out),
        out_shape=tuple([jax.ShapeDtypeStruct((S, 2 * F), BF16)] + x_shapes),
        scratch_shapes=[pltpu.VMEM((tm, tn), F32)] + x_scr,
        compiler_params=_cp(_semantics(plan, ("parallel", "parallel", "arbitrary")), 52),
    )(df, d8, au, *x_ops)
    return outs[0], tuple(outs[1:])


def _fold8(x):
    tm, w = x.shape
    return jnp.sum(x.reshape(tm // 8, 8, w), axis=0)


def _row_spec(tm, w):
    return pl.BlockSpec((tm, w), lambda i: (i, 0))


def _vec_spec(rows, w):
    return pl.BlockSpec((rows, w), lambda i: (0, 0))


def _pre_norm_mod(x, gain, shift, scale1p, *, name):
    S, D = x.shape
    tm = ROW_TILE

    def body(x_ref, g_ref, sh_ref, sc_ref, h_ref):
        xv = x_ref[...]
        rinv = lax.rsqrt(jnp.mean(xv * xv, axis=-1, keepdims=True) + NORM_EPS)
        h_ref[...] = ((xv * rinv) * g_ref[...] * sc_ref[...] + sh_ref[...]).astype(BF16)

    return pl.pallas_call(
        body, name=name, grid=(S // tm,),
        in_specs=[_row_spec(tm, D), _vec_spec(1, D), _vec_spec(1, D), _vec_spec(1, D)],
        out_specs=_row_spec(tm, D), out_shape=jax.ShapeDtypeStruct((S, D), BF16),
        compiler_params=_cp(("parallel",), 32),
    )(x, gain, shift, scale1p)


def _pre_norm_mod_bwd(dh, x, dres, gain, scale1p, *, name):
    S, D = x.shape
    tm = ROW_TILE
    n = S // tm

    def body(dh_ref, x_ref, dr_ref, g_ref, sc_ref, dx_ref, sums_ref, acc_ref):
        i = pl.program_id(0)

        @pl.when(i == 0)
        def _():
            acc_ref[...] = jnp.zeros_like(acc_ref)

        xv = x_ref[...]
        dhv = dh_ref[...]
        g = g_ref[...]
        rinv = lax.rsqrt(jnp.mean(xv * xv, axis=-1, keepdims=True) + NORM_EPS)
        xn = xv * rinv
        dn = dhv * sc_ref[...]
        dxn = dn * g
        dx_ref[...] = dr_ref[...] + rinv * (dxn - xn * jnp.mean(dxn * xn, axis=-1, keepdims=True))
        acc_ref[0] += _fold8(dhv)
        acc_ref[1] += _fold8(dhv * (xn * g))
        acc_ref[2] += _fold8(dn * xn)

        @pl.when(i == n - 1)
        def _():
            for q in range(3):
                sums_ref[q:q + 1, :] = jnp.sum(acc_ref[q], axis=0, keepdims=True)

    return pl.pallas_call(
        body, name=name, grid=(n,),
        in_specs=[_row_spec(tm, D), _row_spec(tm, D), _row_spec(tm, D), _vec_spec(1, D), _vec_spec(1, D)],
        out_specs=(_row_spec(tm, D), _vec_spec(3, D)),
        out_shape=(jax.ShapeDtypeStruct((S, D), F32), jax.ShapeDtypeStruct((3, D), F32)),
        scratch_shapes=[pltpu.VMEM((3, 8, D), F32)],
        compiler_params=_cp(("arbitrary",), 40),
    )(dh, x, dres, gain, scale1p)


def _post_norm_res(x, f, gain, gw, *, name):
    S, D = x.shape
    tm = ROW_TILE

    def body(x_ref, f_ref, g_ref, gw_ref, o_ref):
        fv = f_ref[...]
        rinv = lax.rsqrt(jnp.mean(fv * fv, axis=-1, keepdims=True) + NORM_EPS)
        o_ref[...] = x_ref[...] + gw_ref[...] * ((fv * rinv) * g_ref[...])

    return pl.pallas_call(
        body, name=name, grid=(S // tm,),
        in_specs=[_row_spec(tm, D), _row_spec(tm, D), _vec_spec(1, D), _vec_spec(1, D)],
        out_specs=_row_spec(tm, D), out_shape=jax.ShapeDtypeStruct((S, D), F32),
        compiler_params=_cp(("parallel",), 32),
    )(x, f, gain, gw)


def _post_norm_res_bwd(dxo, f, gain, gw, weight, *, name):
    S, D = f.shape
    tm = ROW_TILE
    n = S // tm

    def body(d_ref, f_ref, g_ref, gw_ref, df_ref, sums_ref, acc_ref):
        i = pl.program_id(0)

        @pl.when(i == 0)
        def _():
            acc_ref[...] = jnp.zeros_like(acc_ref)

        fv = f_ref[...]
        dv = d_ref[...]
        g = g_ref[...]
        rinv = lax.rsqrt(jnp.mean(fv * fv, axis=-1, keepdims=True) + NORM_EPS)
        fh = fv * rinv
        dy = dv * gw_ref[...]
        dfh = dy * g
        df_ref[...] = (rinv * (dfh - fh * jnp.mean(dfh * fh, axis=-1, keepdims=True))).astype(BF16)
        acc_ref[0] += _fold8(weight * dv * (fh * g))
        acc_ref[1] += _fold8(dy * fh)

        @pl.when(i == n - 1)
        def _():
            for q in range(2):
                sums_ref[q:q + 1, :] = jnp.sum(acc_ref[q], axis=0, keepdims=True)

    return pl.pallas_call(
        body, name=name, grid=(n,),
        in_specs=[_row_spec(tm, D), _row_spec(tm, D), _vec_spec(1, D), _vec_spec(1, D)],
        out_specs=(_row_spec(tm, D), _vec_spec(2, D)),
        out_shape=(jax.ShapeDtypeStruct((S, D), BF16), jax.ShapeDtypeStruct((2, D), F32)),
        scratch_shapes=[pltpu.VMEM((2, 8, D), F32)],
        compiler_params=_cp(("arbitrary",), 40),
    )(dxo, f, gain, gw)


def _loss_head(y, target, *, name):
    S, D = y.shape
    tm = ROW_TILE

    def body(y_ref, t_ref, l_ref, dy_ref):
        i = pl.program_id(0)

        @pl.when(i == 0)
        def _():
            l_ref[...] = jnp.zeros_like(l_ref)

        err = y_ref[...] - t_ref[...]
        dy_ref[...] = err * (1.0 / D)
        row = jnp.sum(err * err, axis=-1, keepdims=True) * (0.5 / D)
        l_ref[...] += jnp.sum(row, axis=0, keepdims=True)

    return pl.pallas_call(
        body, name=name, grid=(S // tm,),
        in_specs=[_row_spec(tm, D), _row_spec(tm, D)],
        out_specs=(_vec_spec(1, 1), _row_spec(tm, D)),
        out_shape=(jax.ShapeDtypeStruct((1, 1), F32), jax.ShapeDtypeStruct((S, D), F32)),
        compiler_params=_cp(("arbitrary",), 32),
    )(y, target)


def _shift_down(z, j, row):
    return jnp.where(row >= j, pltpu.roll(z, j, 0), 0.0)


def _shift_up(z, j, row, n):
    return jnp.where(row < n - j, pltpu.roll(z, n - j, 0), 0.0)


def _pool_fwd(p, pool_w, pool_scale, *, name):
    S = p.shape[0]
    C = POOL_GROUP

    def body(u_ref, w_ref, sc_ref, o_ref, y_ref):
        g = pl.program_id(0)
        u = u_ref[...]
        row = lax.broadcasted_iota(jnp.int32, (S, C), 0)
        s1 = u + _shift_down(u, 1, row)
        s2 = s1 + _shift_down(s1, 2, row)
        s3 = s2 + _shift_down(s2, 4, row)
        s4 = s3 + _shift_down(s3, 8, row)
        gi = jnp.zeros((S, C), jnp.int32) + g
        win = jnp.where(gi == 0, s1, jnp.where(gi == 1, s2, jnp.where(gi == 2, s3, s4)))
        width = jnp.where(gi == 0, 2, jnp.where(gi == 1, 4, jnp.where(gi == 2, 8, 16)))
        count = jnp.minimum(row + 1, width).astype(F32)
        o = win / count - u
        o_ref[...] = o
        y_ref[...] = jnp.dot(o.astype(BF16), w_ref[0].astype(BF16), preferred_element_type=F32) * sc_ref[...]

    col = pl.BlockSpec((S, C), lambda g: (0, g))
    return pl.pallas_call(
        body, name=name, grid=(POOL_GROUPS,),
        in_specs=[col, pl.BlockSpec((1, C, C), lambda g: (g, 0, 0)), pl.BlockSpec((1, C), lambda g: (0, g))],
        out_specs=(col, col),
        out_shape=(jax.ShapeDtypeStruct((S, POOL_GROUPS * C), F32), jax.ShapeDtypeStruct((S, POOL_GROUPS * C), F32)),
        compiler_params=_cp(("parallel",), 48),
    )(p, pool_w, pool_scale)


def _pool_bwd(dcat, o, pool_w, pool_scale, *, name):
    S = o.shape[0]
    C = POOL_GROUP

    def body(dy_ref, o_ref, w_ref, sc_ref, du_ref, dw_ref, dsc_ref):
        g = pl.program_id(0)
        dy = dy_ref[...]
        ob = o_ref[...].astype(BF16)
        wb = w_ref[0].astype(BF16)
        mixed = jnp.dot(ob, wb, preferred_element_type=F32)
        dsc_ref[...] = jnp.sum(_fold8(dy * mixed), axis=0, keepdims=True)
        dmix = (dy * sc_ref[...]).astype(BF16)
        dw_ref[0] = lax.dot_general(ob, dmix, (((0,), (0,)), ((), ())), preferred_element_type=F32)
        do = lax.dot_general(dmix, wb, (((1,), (1,)), ((), ())), preferred_element_type=F32)
        row = lax.broadcasted_iota(jnp.int32, (S, C), 0)
        gi = jnp.zeros((S, C), jnp.int32) + g
        width = jnp.where(gi == 0, 2, jnp.where(gi == 1, 4, jnp.where(gi == 2, 8, 16)))
        z = do / jnp.minimum(row + 1, width).astype(F32)
        s1 = z + _shift_up(z, 1, row, S)
        s2 = s1 + _shift_up(s1, 2, row, S)
        s3 = s2 + _shift_up(s2, 4, row, S)
        s4 = s3 + _shift_up(s3, 8, row, S)
        win = jnp.where(gi == 0, s1, jnp.where(gi == 1, s2, jnp.where(gi == 2, s3, s4)))
        du_ref[...] = (win - do).astype(BF16)

    col = pl.BlockSpec((S, C), lambda g: (0, g))
    return pl.pallas_call(
        body, name=name, grid=(POOL_GROUPS,),
        in_specs=[col, col, pl.BlockSpec((1, C, C), lambda g: (g, 0, 0)), pl.BlockSpec((1, C), lambda g: (0, g))],
        out_specs=(col, pl.BlockSpec((1, C, C), lambda g: (g, 0, 0)), pl.BlockSpec((1, C), lambda g: (0, g))),
        out_shape=(jax.ShapeDtypeStruct((S, POOL_GROUPS * C), BF16), jax.ShapeDtypeStruct((POOL_GROUPS, C, C), F32),
                   jax.ShapeDtypeStruct((1, POOL_GROUPS * C), F32)),
        compiler_params=_cp(("parallel",), 48),
    )(dcat, o, pool_w, pool_scale)


def _block_ones():
    r = lax.broadcasted_iota(jnp.int32, (128, 128), 0) // HEAD
    c = lax.broadcasted_iota(jnp.int32, (128, 128), 1) // HEAD
    return jnp.where(r == c, 1.0, 0.0).astype(BF16)


def _segsum(x, bd):
    outs = []
    for j in range(x.shape[1] // 128):
        xs = x[:, j * 128:(j + 1) * 128]
        hi = xs.astype(BF16)
        lo = (xs - hi.astype(F32)).astype(BF16)
        outs.append(jnp.dot(hi, bd, preferred_element_type=F32) + jnp.dot(lo, bd, preferred_element_type=F32))
    return jnp.concatenate(outs, axis=1)


def _prep_common(q, qprev, first, mu, wl, w0, a0, kkw, kaw, R):
    tm = q.shape[0]
    row = lax.broadcasted_iota(jnp.int32, q.shape, 0)
    last = qprev[7:8, :] * first
    prev = jnp.where(row == 0, last, pltpu.roll(q, 1, 0))
    ps = q + mu * (prev - q)
    r = ps[:, 0:R]
    k = ps[:, R:2 * R]
    v = ps[:, 2 * R:3 * R]
    lo_in = ps[:, 3 * R:3 * R + LORA_PAD]
    lane = lax.broadcasted_iota(jnp.int32, (tm, LORA_PAD), 1)
    m_w = lane < LORA_W
    m_a = lane < LORA_W + LORA_A
    m_g = lane < LORA_W + LORA_A + LORA_G
    act = jnp.where(m_w, jnp.tanh(lo_in), jnp.where(m_a, lo_in, jnp.where(m_g, jax.nn.sigmoid(lo_in), 0.0)))
    lo = jnp.dot(act.astype(BF16), wl, preferred_element_type=F32)
    wpre = w0 + lo[:, 0:R]
    apre = a0 + lo[:, R:2 * R]
    g = lo[:, 2 * R:3 * R]
    neg = -wpre
    softplus = jnp.maximum(neg, 0.0) + jnp.log(1.0 + jnp.exp(-jnp.abs(neg)))
    wlog = -softplus - 0.5
    ew = jnp.exp(wlog)
    decay = jnp.exp(-ew)
    a = jax.nn.sigmoid(apre)
    kk = k * kkw
    bd = _block_ones()
    n2 = _segsum(kk * kk, bd)
    nrm = jnp.maximum(jnp.sqrt(n2), 1e-12)
    kap = kk / nrm
    kmul = 1.0 + (a - 1.0) * kaw
    k2 = k * kmul
    return dict(prev=prev, r=r, k=k, v=v, act=act, m_w=m_w, m_a=m_a, m_g=m_g, wpre=wpre, g=g, ew=ew, decay=decay,
                a=a, n2=n2, nrm=nrm, kap=kap, kmul=kmul, k2=k2, bd=bd)


def _prev_rows_spec(tm, w):
    return pl.BlockSpec((8, w), lambda i: (jnp.maximum(i * (tm // 8) - 1, 0), 0))


def _rwkv_prep(q, mu, wl, w0, a0, kkw, kaw, *, name):
    S, QW = q.shape
    R = w0.shape[1]
    tm = ROW_TILE // 2

    def body(q_ref, qp_ref, mu_ref, wl_ref, w0_ref, a0_ref, kk_ref, ka_ref, r_ref, w_ref, k_ref, v_ref, kap_ref,
             b_ref, g_ref):
        first = jnp.where(pl.program_id(0) > 0, 1.0, 0.0)
        t = _prep_common(q_ref[...], qp_ref[...], first, mu_ref[...], wl_ref[...], w0_ref[...], a0_ref[...],
                         kk_ref[...], ka_ref[...], R)
        r_ref[...] = t["r"]
        w_ref[...] = t["decay"]
        k_ref[...] = t["k2"]
        v_ref[...] = t["v"]
        kap_ref[...] = t["kap"]
        b_ref[...] = t["kap"] * t["a"]
        g_ref[...] = t["g"]

    vec = _vec_spec(1, R)
    return pl.pallas_call(
        body, name=name, grid=(S // tm,),
        in_specs=[_row_spec(tm, QW), _prev_rows_spec(tm, QW), _vec_spec(1, QW), _vec_spec(LORA_PAD, 3 * R), vec, vec,
                  vec, vec],
        out_specs=tuple([_row_spec(tm, R)] * 7),
        out_shape=tuple([jax.ShapeDtypeStruct((S, R), F32)] * 7),
        compiler_params=_cp(("parallel",), 48),
    )(q, q, mu, wl, w0, a0, kkw, kaw)


def _rwkv_prep_bwd(q, mu, wl, w0, a0, kkw, kaw, grads, *, name):
    S, QW = q.shape
    R = w0.shape[1]
    tm = ROW_TILE // 2
    n = S // tm

    def body(q_ref, qp_ref, mu_ref, wl_ref, w0_ref, a0_ref, kk_ref, ka_ref, dr_ref, dw_ref, dk2_ref, dv_ref, dkap_ref,
             db_ref, dg_ref, drb_ref, dk2b_ref, dvb_ref, dps_ref, dwl_ref, sums_ref, acc_ref):
        i = pl.program_id(0)

        @pl.when(i == 0)
        def _():
            acc_ref[...] = jnp.zeros_like(acc_ref)
            dwl_ref[...] = jnp.zeros_like(dwl_ref)

        first = jnp.where(i > 0, 1.0, 0.0)
        wl = wl_ref[...]
        kkw = kk_ref[...]
        kaw = ka_ref[...]
        t = _prep_common(q_ref[...], qp_ref[...], first, mu_ref[...], wl, w0_ref[...], a0_ref[...], kkw, kaw, R)
        a, kap, k, act = t["a"], t["kap"], t["k"], t["act"]
        db = db_ref[...]
        dk2 = dk2_ref[...] + dk2b_ref[...]
        dkap = dkap_ref[...] + db * a
        da = db * kap + dk2 * k * kaw
        dk = dk2 * t["kmul"]
        proj = jnp.where(jnp.sqrt(t["n2"]) > 1e-12, _segsum(kap * dkap, t["bd"]), 0.0)
        dkk = (dkap - kap * proj) / t["nrm"]
        dk = dk + dkk * kkw
        dapre = da * a * (1.0 - a)
        dwlog = dw_ref[...] * t["decay"] * (-t["ew"])
        dwpre = dwlog * jax.nn.sigmoid(-t["wpre"])
        acc_ref[0] += _fold8(dwpre)
        acc_ref[1] += _fold8(dapre)
        acc_ref[2] += _fold8(dkk * k)
        acc_ref[3] += _fold8(dk2 * k * (a - 1.0))
        dlo = jnp.concatenate([dwpre, dapre, dg_ref[...]], axis=1).astype(BF16)
        dwl_ref[...] += lax.dot_general(act.astype(BF16), dlo, (((0,), (0,)), ((), ())), preferred_element_type=F32)
        dact = lax.dot_general(dlo, wl, (((1,), (1,)), ((), ())), preferred_element_type=F32)
        dlin = jnp.where(t["m_w"], dact * (1.0 - act * act),
                         jnp.where(t["m_a"], dact, jnp.where(t["m_g"], dact * act * (1.0 - act), 0.0)))
        dps_ref[:, 0:R] = dr_ref[...] + drb_ref[...]
        dps_ref[:, R:2 * R] = dk
        dps_ref[:, 2 * R:3 * R] = dv_ref[...] + dvb_ref[...]
        dps_ref[:, 3 * R:3 * R + LORA_PAD] = dlin
        dps_ref[:, 3 * R + LORA_PAD:] = jnp.zeros((tm, QW - 3 * R - LORA_PAD), F32)

        @pl.when(i == n - 1)
        def _():
            for j in range(4):
                sums_ref[j:j + 1, :] = jnp.sum(acc_ref[j], axis=0, keepdims=True)

    vec = _vec_spec(1, R)
    return pl.pallas_call(
        body, name=name, grid=(n,),
        in_specs=[_row_spec(tm, QW), _prev_rows_spec(tm, QW), _vec_spec(1, QW), _vec_spec(LORA_PAD, 3 * R), vec, vec,
                  vec, vec] + [_row_spec(tm, R)] * 10,
        out_specs=(_row_spec(tm, QW), _vec_spec(LORA_PAD, 3 * R), _vec_spec(4, R)),
        out_shape=(jax.ShapeDtypeStruct((S, QW), F32), jax.ShapeDtypeStruct((LORA_PAD, 3 * R), F32),
                   jax.ShapeDtypeStruct((4, R), F32)),
        scratch_shapes=[pltpu.VMEM((4, 8, R), F32)],
        compiler_params=_cp(("arbitrary",), 56),
    )(q, q, mu, wl, w0, a0, kkw, kaw, *grads)


def _tshift_bwd(dps, q, mu, *, name):
    S, QW = q.shape
    tm = ROW_TILE // 2
    n = S // tm

    def body(d_ref, dn_ref, q_ref, qp_ref, mu_ref, dq_ref, dmu_ref, acc_ref):
        i = pl.program_id(0)

        @pl.when(i == 0)
        def _():
            acc_ref[...] = jnp.zeros_like(acc_ref)

        mu = mu_ref[...]
        d = d_ref[...]
        qv = q_ref[...]
        row = lax.broadcasted_iota(jnp.int32, d.shape, 0)
        first = jnp.where(i > 0, 1.0, 0.0)
        notlast = jnp.where(i < n - 1, 1.0, 0.0)
        prev = jnp.where(row == 0, qp_ref[7:8, :] * first, pltpu.roll(qv, 1, 0))
        z = d * mu
        nxt = jnp.where(row == tm - 1, dn_ref[0:1, :] * mu * notlast, pltpu.roll(z, tm - 1, 0))
        dq_ref[...] = (d * (1.0 - mu) + nxt).astype(BF16)
        acc_ref[...] += _fold8(d * (prev - qv))

        @pl.when(i == n - 1)
        def _():
            dmu_ref[...] = jnp.sum(acc_ref[...], axis=0, keepdims=True)

    nblk8 = S // 8
    next_spec = pl.BlockSpec((8, QW), lambda i: (jnp.minimum((i + 1) * (tm // 8), nblk8 - 1), 0))
    return pl.pallas_call(
        body, name=name, grid=(n,),
        in_specs=[_row_spec(tm, QW), next_spec, _row_spec(tm, QW), _prev_rows_spec(tm, QW), _vec_spec(1, QW)],
        out_specs=(_row_spec(tm, QW), _vec_spec(1, QW)),
        out_shape=(jax.ShapeDtypeStruct((S, QW), BF16), jax.ShapeDtypeStruct((1, QW), F32)),
        scratch_shapes=[pltpu.VMEM((8, QW), F32)],
        compiler_params=_cp(("arbitrary",), 48),
    )(dps, dps, q, q, mu)


def _post_common(ysc, r, k2, v, lnw, lnb, rk):
    bd = _block_ones()
    mean = _segsum(ysc, bd) * (1.0 / HEAD)
    d = ysc - mean
    var = _segsum(d * d, bd) * (1.0 / HEAD)
    rstd = lax.rsqrt(var + LN_X_EPS)
    yh = d * rstd
    rkk = _segsum(r * k2 * rk, bd)
    z = yh * lnw + lnb + rkk * v
    return bd, rstd, yh, rkk, z


def _rwkv_post(ysc, r, k2, v, g, ypool, lnw, lnb, rk, *, name):
    S, R = ysc.shape
    PW = ypool.shape[1]
    tm = ROW_TILE

    def body(y_ref, r_ref, k_ref, v_ref, g_ref, yp_ref, lw_ref, lb_ref, rk_ref, cat_ref):
        _, _, _, _, z = _post_common(y_ref[...], r_ref[...], k_ref[...], v_ref[...], lw_ref[...], lb_ref[...],
                                     rk_ref[...])
        cat_ref[:, 0:PW] = yp_ref[...].astype(BF16)
        cat_ref[:, PW:] = (z * g_ref[...]).astype(BF16)

    vec = _vec_spec(1, R)
    return pl.pallas_call(
        body, name=name, grid=(S // tm,),
        in_specs=[_row_spec(tm, R)] * 5 + [_row_spec(tm, PW), vec, vec, vec],
        out_specs=_row_spec(tm, PW + R), out_shape=jax.ShapeDtypeStruct((S, PW + R), BF16),
        compiler_params=_cp(("parallel",), 48),
    )(ysc, r, k2, v, g, ypool, lnw, lnb, rk)


def _rwkv_post_bwd(dcat, ysc, r, k2, v, g, lnw, lnb, rk, *, name):
    S, R = ysc.shape
    tm = ROW_TILE
    n = S // tm

    def body(d_ref, y_ref, r_ref, k_ref, v_ref, g_ref, lw_ref, lb_ref, rk_ref, dy_ref, dg_ref, drb_ref, dkb_ref,
             dvb_ref, sums_ref, acc_ref):
        i = pl.program_id(0)

        @pl.when(i == 0)
        def _():
            acc_ref[...] = jnp.zeros_like(acc_ref)

        rv, kv, vv, lw, rkw = r_ref[...], k_ref[...], v_ref[...], lw_ref[...], rk_ref[...]
        bd, rstd, yh, rkk, z = _post_common(y_ref[...], rv, kv, vv, lw, lb_ref[...], rkw)
        dyr = d_ref[...]
        dg_ref[...] = dyr * z
        dz = dyr * g_ref[...]
        dyh = dz * lw
        dy_ref[...] = rstd * (dyh - _segsum(dyh, bd) * (1.0 / HEAD) - yh * (_segsum(dyh * yh, bd) * (1.0 / HEAD)))
        dvb_ref[...] = dz * rkk
        drkk = _segsum(dz * vv, bd)
        drb_ref[...] = drkk * kv * rkw
        dkb_ref[...] = drkk * rv * rkw
        acc_ref[0] += _fold8(dz * yh)
        acc_ref[1] += _fold8(dz)
        acc_ref[2] += _fold8(drkk * rv * kv)

        @pl.when(i == n - 1)
        def _():
            for j in range(3):
                sums_ref[j:j + 1, :] = jnp.sum(acc_ref[j], axis=0, keepdims=True)

    vec = _vec_spec(1, R)
    dspec = _row_spec(tm, R)
    return pl.pallas_call(
        body, name=name, grid=(n,),
        in_specs=[dspec] + [_row_spec(tm, R)] * 5 + [vec, vec, vec],
        out_specs=tuple([_row_spec(tm, R)] * 5) + (_vec_spec(3, R),),
        out_shape=tuple([jax.ShapeDtypeStruct((S, R), F32)] * 5) + (jax.ShapeDtypeStruct((3, R), F32),),
        scratch_shapes=[pltpu.VMEM((3, 8, R), F32)],
        compiler_params=_cp(("arbitrary",), 56),
    )(dcat, ysc, r, k2, v, g, lnw, lnb, rk)


SEL_ROWS = 64


def _column_selector():
    row = lax.broadcasted_iota(jnp.int32, (SEL_ROWS, 8 * 128), 0)
    col = lax.broadcasted_iota(jnp.int32, (SEL_ROWS, 8 * 128), 1)
    head, rest = row // 32, row % 32
    hit = (rest < 24) & (rest % 8 == col // 128) & (head == (col % 128) // HEAD)
    return jnp.where(hit, 1.0, 0.0).astype(BF16)


def _expand_columns(x, sel):
    hi = x.astype(BF16).astype(F32)
    r1 = x - hi
    mid = r1.astype(BF16).astype(F32)
    lo = (r1 - mid).astype(BF16).astype(F32)
    terms = jnp.concatenate([hi, mid, lo, jnp.zeros_like(x)], axis=0)
    both = jnp.concatenate([terms, pltpu.roll(terms, HEAD, 1)], axis=0)[:, 0:HEAD]
    return lax.dot_general(both.astype(BF16), sel, (((0,), (0,)), ((), ())), preferred_element_type=F32)


def _head_sum_weights():
    row = lax.broadcasted_iota(jnp.int32, (256, 256), 0)
    col = lax.broadcasted_iota(jnp.int32, (256, 256), 1)
    return jnp.where((row % 128) // HEAD == col // 128, 1.0, 0.0).astype(BF16)


def _head_sums_mxu(products, w2):
    rows = []
    for p in products:
        hi = p.astype(BF16)
        rows.append(jnp.concatenate([hi, (p - hi.astype(F32)).astype(BF16)], axis=1))
    out = jnp.dot(jnp.concatenate(rows, axis=0), w2, preferred_element_type=F32)
    return [(out[i * HEAD:(i + 1) * HEAD, 0:128], out[i * HEAD:(i + 1) * HEAD, 128:256]) for i in range(len(products))]


def _masked_rows(rows, negate=False):
    head_a = (lax.broadcasted_iota(jnp.int32, rows.shape, 1) % 128) < HEAD
    v = -rows if negate else rows
    return jnp.where(head_a, v, 0.0), jnp.where(head_a, 0.0, v)


def _lane_sums(x, row_a, row_b):
    return jnp.sum(x * row_a, axis=1, keepdims=True), jnp.sum(x * row_b, axis=1, keepdims=True)


def _scan_fwd(r, w, k, v, kap, b, *, name, plan=None):
    S, R = r.shape
    G, T = SCAN_G, SCAN_T
    NP = R // 128
    assert NP % G == 0 and S % T == 0
    GW = 128 * G

    def body(r_ref, w_ref, k_ref, v_ref, kap_ref, b_ref, sel_ref, w2_ref, y_ref, sa_ref, st_ref, s_scr, vc_scr,
             yt_scr, sat_scr):
        c = pl.program_id(1)

        @pl.when(c == 0)
        def _():
            s_scr[...] = jnp.zeros_like(s_scr)

        yt_scr[...] = jnp.zeros_like(yt_scr)
        sat_scr[...] = jnp.zeros_like(sat_scr)
        lane = lax.broadcasted_iota(jnp.int32, (HEAD, 128), 1)
        m_a = lane < HEAD

        def block(tb, carry):
            t0 = pl.multiple_of(tb * 8, 8)
            rb, wb, kb = r_ref[pl.ds(t0, 8), :], w_ref[pl.ds(t0, 8), :], k_ref[pl.ds(t0, 8), :]
            pb, bb, vb = kap_ref[pl.ds(t0, 8), :], b_ref[pl.ds(t0, 8), :], v_ref[pl.ds(t0, 8), :]
            for g in range(G):
                vc_scr[g] = _expand_columns(vb[:, g * 128:(g + 1) * 128], sel_ref[...])

            def put_y(g, parts, hot_y):
                yt_scr[g, 0:HEAD, :] = jnp.where(hot_y, parts[0], yt_scr[g, 0:HEAD, :])
                yt_scr[g, HEAD:, :] = jnp.where(hot_y, parts[1], yt_scr[g, HEAD:, :])

            def put_sa(g, parts, hot_t):
                sat_scr[g, 0:HEAD, :] = jnp.where(hot_t, parts[0], sat_scr[g, 0:HEAD, :])
                sat_scr[g, HEAD:, :] = jnp.where(hot_t, parts[1], sat_scr[g, HEAD:, :])

            npa, npb = _masked_rows(pb, negate=True)
            for j in range(8):
                t = t0 + j
                cols = slice(j * 128, (j + 1) * 128)
                sa_parts, products = [], []
                for g in range(G):
                    sl = slice(g * 128, (g + 1) * 128)
                    sa_parts.append(_lane_sums(s_scr[g], npa[j:j + 1, sl], npb[j:j + 1, sl]))
                hot_t = lane == t
                for g in range(G):
                    sl = slice(g * 128, (g + 1) * 128)
                    sa = jnp.where(m_a, sa_parts[g][0], sa_parts[g][1])
                    st = s_scr[g] * wb[j:j + 1, sl] + sa * bb[j:j + 1, sl] + vc_scr[g, :, cols] * kb[j:j + 1, sl]
                    s_scr[g] = st
                    st_ref[g, t] = st
                    put_sa(g, sa_parts[g], hot_t)
                    products.append(st * rb[j:j + 1, sl])
                for g, parts in enumerate(_head_sums_mxu(products, w2_ref[...])):
                    put_y(g, parts, hot_t)
            return carry

        lax.fori_loop(0, T // 8, block, 0)
        for g in range(G):
            y_ref[:, g * 128:(g + 1) * 128] = yt_scr[g].T[0:T, :]
            sa_ref[:, g * 128:(g + 1) * 128] = sat_scr[g].T[0:T, :]

    tspec = pl.BlockSpec((T, GW), lambda p, c: (c, p))
    sel_spec = pl.BlockSpec((SEL_ROWS, 8 * 128), lambda p, c: (0, 0))
    grid = (NP // G, S // T)
    x_in, x_out, x_shapes, x_scr, x_ops = _host_args(plan)
    w2_spec = pl.BlockSpec((256, 256), lambda p, c: (0, 0))
    outs = pl.pallas_call(
        _host(body, plan, 8, 3, *_grid_ends(grid)), name=name, grid=grid,
        in_specs=[tspec] * 6 + [sel_spec, w2_spec] + x_in,
        out_specs=tuple([tspec, tspec, pl.BlockSpec((G, T, HEAD, 128), lambda p, c: (p, c, 0, 0))] + x_out),
        out_shape=tuple([jax.ShapeDtypeStruct((S, R), F32), jax.ShapeDtypeStruct((S, R), F32),
                         jax.ShapeDtypeStruct((NP, S, HEAD, 128), F32)] + x_shapes),
        scratch_shapes=[pltpu.VMEM((G, HEAD, 128), F32), pltpu.VMEM((G, HEAD, 8 * 128), F32),
                        pltpu.VMEM((G, 128, 128), F32), pltpu.VMEM((G, 128, 128), F32)] + x_scr,
        compiler_params=_cp(_semantics(plan, ("parallel", "arbitrary")), 48),
    )(r, w, k, v, kap, b, _column_selector(), _head_sum_weights(), *x_ops)
    return outs[0], outs[1], outs[2], tuple(outs[3:])


def _scan_bwd(r, w, k, v, kap, b, sa, dy, states, *, name):
    S, R = r.shape
    G, T = SCAN_G_BWD, SCAN_T
    NP = R // 128
    NC = S // T
    GW = 128 * G
    assert NP % G == 0

    def body(r_ref, w_ref, k_ref, v_ref, kap_ref, b_ref, sa_ref, dy_ref, st_ref, sp_ref, sel_ref, w2_ref, dr_ref,
             dw_ref, dk_ref, dv_ref, dkap_ref, db_ref, ds_scr, vc_scr, dyc_scr, sac_scr, dvt_scr):
        ci = pl.program_id(1)

        @pl.when(ci == 0)
        def _():
            ds_scr[...] = jnp.zeros_like(ds_scr)

        dvt_scr[...] = jnp.zeros_like(dvt_scr)
        lane = lax.broadcasted_iota(jnp.int32, (HEAD, 128), 1)
        m_a = lane < HEAD
        sub = lax.broadcasted_iota(jnp.int32, (8, 128), 0)
        zero_i = jnp.zeros((HEAD, 128), jnp.int32)
        has_prev = jnp.where(ci < NC - 1, 1.0, 0.0)

        def state_before(g, t):
            at_start = (zero_i + t) == 0
            return jnp.where(at_start, sp_ref[g, 0] * has_prev, st_ref[g, jnp.maximum(t - 1, 0)])

        def block(it, carry):
            tb = T // 8 - 1 - it
            t0 = pl.multiple_of(tb * 8, 8)
            rb, wb, kb = r_ref[pl.ds(t0, 8), :], w_ref[pl.ds(t0, 8), :], k_ref[pl.ds(t0, 8), :]
            pb, bb = kap_ref[pl.ds(t0, 8), :], b_ref[pl.ds(t0, 8), :]
            vb, dyb, sab = v_ref[pl.ds(t0, 8), :], dy_ref[pl.ds(t0, 8), :], sa_ref[pl.ds(t0, 8), :]
            for g in range(G):
                sl = slice(g * 128, (g + 1) * 128)
                vc_scr[g] = _expand_columns(vb[:, sl], sel_ref[...])
                dyc_scr[g] = _expand_columns(dyb[:, sl], sel_ref[...])
                sac_scr[g] = _expand_columns(sab[:, sl], sel_ref[...])
            outs = [[jnp.zeros((8, 128), F32) for _ in range(5)] for _ in range(G)]
            bba, bbb = _masked_rows(bb)
            for j in range(7, -1, -1):
                t = t0 + j
                hot = lane == t
                cols = slice(j * 128, (j + 1) * 128)
                dsa_parts, products = [], []
                for g in range(G):
                    sl = slice(g * 128, (g + 1) * 128)
                    ds = ds_scr[g] + dyc_scr[g, :, cols] * rb[j:j + 1, sl]
                    ds_scr[g] = ds
                    dsa_parts.append(_lane_sums(ds, bba[j:j + 1, sl], bbb[j:j + 1, sl]))
                    products.append(ds * kb[j:j + 1, sl])
                ds_parts = list(zip(_head_sums_mxu(products, w2_ref[...]), dsa_parts))
                for g in range(G):
                    sl = slice(g * 128, (g + 1) * 128)
                    w_r, p_r = wb[j:j + 1, sl], pb[j:j + 1, sl]
                    ds = ds_scr[g]
                    s_p = st_ref[g, t - 1] if j > 0 else state_before(g, t)
                    dr_row = jnp.sum(st_ref[g, t] * dyc_scr[g, :, cols], axis=0, keepdims=True)
                    dk_row = jnp.sum(ds * vc_scr[g, :, cols], axis=0, keepdims=True)
                    db_row = jnp.sum(ds * sac_scr[g, :, cols], axis=0, keepdims=True)
                    dw_row = jnp.sum(ds * s_p, axis=0, keepdims=True)
                    (dv_a, dv_b), (dsa_a, dsa_b) = ds_parts[g]
                    dvt_scr[g, 0:HEAD, :] = jnp.where(hot, dv_a, dvt_scr[g, 0:HEAD, :])
                    dvt_scr[g, HEAD:, :] = jnp.where(hot, dv_b, dvt_scr[g, HEAD:, :])
                    dsa = jnp.where(m_a, dsa_a, dsa_b)
                    dkap_row = -jnp.sum(s_p * dsa, axis=0, keepdims=True)
                    ds_scr[g] = ds * w_r - dsa * p_r
                    pick = sub == j
                    for q, row in enumerate((dr_row, dw_row, dk_row, dkap_row, db_row)):
                        outs[g][q] = jnp.where(pick, row, outs[g][q])
            for g in range(G):
                sl = slice(g * 128, (g + 1) * 128)
                for q, ref in enumerate((dr_ref, dw_ref, dk_ref, dkap_ref, db_ref)):
                    ref[pl.ds(t0, 8), sl] = outs[g][q]
            return carry

        lax.fori_loop(0, T // 8, block, 0)
        for g in range(G):
            dv_ref[:, g * 128:(g + 1) * 128] = dvt_scr[g].T[0:T, :]

    tspec = pl.BlockSpec((T, GW), lambda p, c: (NC - 1 - c, p))
    st_spec = pl.BlockSpec((G, T, HEAD, 128), lambda p, c: (p, NC - 1 - c, 0, 0))
    prev_spec = pl.BlockSpec((G, 1, HEAD, 128), lambda p, c: (p, jnp.maximum((NC - 1 - c) * T - 1, 0), 0, 0))
    sel_spec = pl.BlockSpec((SEL_ROWS, 8 * 128), lambda p, c: (0, 0))
    return pl.pallas_call(
        body, name=name, grid=(NP // G, NC),
        in_specs=[tspec] * 8 + [st_spec, prev_spec, sel_spec, pl.BlockSpec((256, 256), lambda p, c: (0, 0))],
        out_specs=tuple([tspec] * 6),
        out_shape=tuple([jax.ShapeDtypeStruct((S, R), F32)] * 6),
        scratch_shapes=[pltpu.VMEM((G, HEAD, 128), F32), pltpu.VMEM((G, HEAD, 8 * 128), F32),
                        pltpu.VMEM((G, HEAD, 8 * 128), F32), pltpu.VMEM((G, HEAD, 8 * 128), F32),
                        pltpu.VMEM((G, 128, 128), F32)],
        compiler_params=_cp(("parallel", "arbitrary"), 48),
    )(r, w, k, v, kap, b, sa, dy, states, states, _column_selector(), _head_sum_weights())


def _sum_parts(parts, *, name):
    P, rows, W = parts.shape
    tr = rows
    for cand in (1024, 512, 256, 128, 64, 32, 16, 8):
        if rows % cand == 0:
            tr = cand
            break

    def body(p_ref, o_ref):
        acc = p_ref[0]
        for s in range(1, P):
            acc = acc + p_ref[s]
        o_ref[...] = acc

    return pl.pallas_call(
        body, name=name, grid=(rows // tr,),
        in_specs=[pl.BlockSpec((P, tr, W), lambda i: (0, i, 0))],
        out_specs=pl.BlockSpec((tr, W), lambda i: (i, 0)), out_shape=jax.ShapeDtypeStruct((rows, W), F32),
        compiler_params=_cp(("parallel",), 32),
    )(parts)


def _adamw(w, m, v, parts, *, name):
    R, C = w.shape
    P = parts.shape[0]
    tr = R
    for cand in (1024, 512, 256, 128, 64, 32, 16, 8):
        if R % cand == 0 and cand * C * 4 * (7 + P) <= 10 * 1024 * 1024:
            tr = cand
            break
    bc1 = 1.0 - ADAM_B1 ** ADAM_STEP
    bc2 = 1.0 - ADAM_B2 ** ADAM_STEP

    def body(w_ref, m_ref, v_ref, p_ref, g_ref, d_ref, nm_ref, nv_ref):
        g = p_ref[0].astype(F32)
        for s in range(1, P):
            g = g + p_ref[s].astype(F32)
        m1 = ADAM_B1 * m_ref[...] + (1.0 - ADAM_B1) * g
        v1 = ADAM_B2 * v_ref[...] + (1.0 - ADAM_B2) * (g * g)
        m_hat = m1 / bc1
        v_hat = v1 / bc2
        g_ref[...] = g
        d_ref[...] = -ADAM_LR * (m_hat / (jnp.sqrt(v_hat) + ADAM_EPS) + ADAM_WD * w_ref[...])
        nm_ref[...] = m1
        nv_ref[...] = v1

    spec = pl.BlockSpec((tr, C), lambda i: (i, 0))
    return pl.pallas_call(
        body, name=name, grid=(R // tr,),
        in_specs=[spec, spec, spec, pl.BlockSpec((P, tr, C), lambda i: (0, i, 0))],
        out_specs=(spec, spec, spec, spec), out_shape=tuple([jax.ShapeDtypeStruct((R, C), F32)] * 4),
        compiler_params=_cp(("parallel",), 40),
    )(w, m, v, parts)


def _cols_full(g8):
    n, rows, c = g8.shape
    return jnp.transpose(g8, (1, 0, 2)).reshape(rows, n * c)


def _cols_split(full):
    rows, cols = full.shape
    return jnp.transpose(full.reshape(rows, N_DEV, cols // N_DEV), (1, 0, 2))


def _pack(vals, rows_multiple=512):
    flat = jnp.concatenate([v.reshape(-1).astype(F32) for v in vals])
    n = flat.shape[0]
    unit = 128 * rows_multiple
    padded = ((n + unit - 1) // unit) * unit
    return jnp.pad(flat, (0, padded - n)).reshape(padded // 128, 128)


def _unpack(packed, shapes):
    flat = packed.reshape(-1)
    out, off = [], 0
    for shp in shapes:
        size = 1
        for d in shp:
            size *= d
        out.append(flat[off:off + size].reshape(shp))
        off += size
    return out


def _ffn_forward(x, weights, gpre, gpost, shift, scale1p, gw, tag, up_plan=None, down_from_plan=None):
    g8, u8, d8 = weights
    h = _pre_norm_mod(x, gpre, shift, scale1p, name=f"{tag}_pre")
    au, s, carried = _ffn_up(h, g8, u8, tm=1024, tk=2048, plan=up_plan,
                             name=f"{tag}_up" + ("_carry" if up_plan else ""))
    if down_from_plan is not None:
        d8 = carried[down_from_plan]
    f = _mm(s, d8.reshape(-1, d8.shape[2]), tm=1024, tn=1024, tk=2048, name=f"{tag}_down")
    xo = _post_norm_res(x, f, gpost, gw, name=f"{tag}_post")
    return xo, (h, au, s, f), d8, carried


def _ffn_backward(dxo, x, saved, weights, gpre, gpost, scale1p, gw, tag, plans=None, own_sums=None):
    g8, u8, d8 = weights
    h, au, s, f = saved
    plans = plans or {}

    def nm(key):
        return f"{tag}_{key}" + ("_carry" if key in plans or (key == "dh" and own_sums) else "")

    df, post_sums = _post_norm_res_bwd(dxo, f, gpost, gw, MACARON, name=f"{tag}_post_bwd")
    dwd, got_dwd = _ffn_dwd(s, df, tn=1024, tk=1024, plan=plans.get("dwd"), name=nm("dwd"))
    dau, got_down = _ffn_down_bwd(df, d8, au, tm=1024, tk=2048, plan=plans.get("down_bwd"), name=nm("down_bwd"))
    dwgu, got_dwgu = _ffn_dwgu(h, dau, tm=1024, tk=1024, plan=plans.get("dwgu"), name=nm("dwgu"))
    dh_plan = _ChipsPlan(own_sums([dwgu, dwd])) if own_sums else None
    dh, got_dh = _ffn_dh(dau, g8, u8, tm=1024, tn=1024, plan=dh_plan, name=nm("dh"))
    dx, pre_sums = _pre_norm_mod_bwd(dh, x, dxo, gpre, scale1p, name=f"{tag}_pre_bwd")
    carried = {"dwd": got_dwd, "down_bwd": got_down, "dwgu": got_dwgu, "dh": got_dh}
    return dx, dwgu, dwd, pre_sums, post_sums, carried


def kernel(x, c, w_ada, b_ada, norm_pre, norm_post, ffn1_w_gate, ffn1_w_up, ffn1_w_down, w_in, mu_shift, pool_w, pool_scale, w0, w2, a0, a2, g2, k_k, k_a, r_k, lnx_w, lnx_b, w_out, ffn2_w_gate, ffn2_w_up, ffn2_w_down, loss_target, m_w_ada, m_b_ada, m_norm_pre, m_norm_post, m_ffn1_w_gate, m_ffn1_w_up, m_ffn1_w_down, m_w_in, m_mu_shift, m_pool_w, m_pool_scale, m_w0, m_w2, m_a0, m_a2, m_g2, m_k_k, m_k_a, m_r_k, m_lnx_w, m_lnx_b, m_w_out, m_ffn2_w_gate, m_ffn2_w_up, m_ffn2_w_down, v_w_ada, v_b_ada, v_norm_pre, v_norm_post, v_ffn1_w_gate, v_ffn1_w_up, v_ffn1_w_down, v_w_in, v_mu_shift, v_pool_w, v_pool_scale, v_w0, v_w2, v_a0, v_a2, v_g2, v_k_k, v_k_a, v_r_k, v_lnx_w, v_lnx_b, v_w_out, v_ffn2_w_gate, v_ffn2_w_up, v_ffn2_w_down):
    names = ["w_ada", "b_ada", "norm_pre", "norm_post", "ffn1_w_gate", "ffn1_w_up", "ffn1_w_down", "w_in", "mu_shift",
             "pool_w", "pool_scale", "w0", "w2", "a0", "a2", "g2", "k_k", "k_a", "r_k", "lnx_w", "lnx_b", "w_out",
             "ffn2_w_gate", "ffn2_w_up", "ffn2_w_down"]
    env = dict(locals())
    W = {n: env[n][0] for n in names}
    M1 = {n: env["m_" + n][0] for n in names}
    V1 = {n: env["v_" + n][0] for n in names}

    me = _my_index()
    xs = x[0]
    tgt = loss_target[0]
    S, D = xs.shape
    F = W["ffn1_w_gate"].shape[1] * N_DEV
    R = W["w0"].shape[0]
    PW = D - R
    IN_W = W["w_in"].shape[1] * N_DEV
    P_W = F
    QW = P_W - PW
    NMOD = 9 * D
    ada_c = W["w_ada"].shape[1]

    c_all, npre8, npost8, w2_8, a2_8, g2_8 = _exchange(
        [c, W["norm_pre"], W["norm_post"], W["w2"].astype(BF16), W["a2"].astype(BF16), W["g2"].astype(BF16)],
        scatter=False, name="gather_small")
    c_all = c_all.reshape(N_DEV, D)
    gpre = _cols_full(npre8)
    gpost = _cols_full(npost8)
    wl = jnp.zeros((LORA_PAD, 3 * R), BF16)
    wl = wl.at[0:LORA_W, 0:R].set(_cols_full(w2_8))
    wl = wl.at[LORA_W:LORA_W + LORA_A, R:2 * R].set(_cols_full(a2_8))
    wl = wl.at[LORA_W + LORA_A:LORA_W + LORA_A + LORA_G, 2 * R:3 * R].set(_cols_full(g2_8))

    sc_all = jax.nn.silu(c_all)
    sc_pad = jnp.concatenate([sc_all, jnp.zeros((8, D), F32)], axis=0).astype(BF16)
    modcols = _mm(sc_pad, W["w_ada"], tm=16, tn=ada_c, tk=256, name="ada_fwd")[0:N_DEV]
    modcols = modcols + lax.dynamic_slice(W["b_ada"], (me * ada_c,), (ada_c,))[None, :]
    (mod8,) = _exchange([modcols], scatter=False, name="gather_mod")
    mod = lax.dynamic_index_in_dim(mod8, me, axis=1, keepdims=False).reshape(9, D)

    def mod_row(i):
        return mod[i:i + 1, :]

    f_pad = FF_TILE - F // N_DEV

    def ffn_shards(tag):
        return [jnp.pad(W[f"{tag}_w_gate"].astype(BF16), ((0, 0), (0, f_pad))),
                jnp.pad(W[f"{tag}_w_up"].astype(BF16), ((0, 0), (0, f_pad))),
                jnp.pad(W[f"{tag}_w_down"].astype(BF16), ((0, f_pad), (0, 0)))]

    ffn1_shards = ffn_shards("ffn1")
    g8_1, u8_1 = _gather_two_level(ffn1_shards[:2], name="gather_ffn_up")
    up_plan = _GatherPlan([W["w_in"].astype(BF16), ffn1_shards[2]])
    scan_plan = _GatherPlan(ffn_shards("ffn2") + [W["w_out"].astype(BF16)])

    mu_p = jnp.pad(W["mu_shift"], (0, QW - W["mu_shift"].shape[0]))[None, :]
    vec = lambda a: a.reshape(1, -1)
    w0r, a0r, kkr, kar = vec(W["w0"]), vec(W["a0"]), vec(W["k_k"]), vec(W["k_a"])
    lnw, lnb, rkr = vec(W["lnx_w"]), vec(W["lnx_b"]), vec(W["r_k"])
    pscale = vec(W["pool_scale"])

    sc1p = [1.0 + mod_row(3 * s + 1) for s in range(3)]
    shifts = [mod_row(3 * s) for s in range(3)]
    wgts = [MACARON, 1.0, MACARON]
    gws = [wgts[s] * (1.0 + mod_row(3 * s + 2)) for s in range(3)]
    gp = [gpre[s:s + 1] for s in range(3)]
    gq = [gpost[s:s + 1] for s in range(3)]

    x1, sv1, d8_1, (win8, _) = _ffn_forward(xs, (g8_1, u8_1, None), gp[0], gq[0], shifts[0], sc1p[0], gws[0], "ffn",
                                            up_plan=up_plan, down_from_plan=1)
    ffn1_w = (g8_1, u8_1, d8_1)
    w_in_p = jnp.pad(_cols_full(win8), ((0, 0), (0, P_W - IN_W)))

    h2 = _pre_norm_mod(x1, gp[1], shifts[1], sc1p[1], name="mix_pre")
    p = _mm(h2, w_in_p, tm=1024, tn=512, tk=2048, name="mix_in")
    q = p[:, PW:]
    o_pool, y_pool = _pool_fwd(p, W["pool_w"], pscale, name="pool_fwd")
    r_s, w_s, k_s, v_s, kap_s, b_s, g_s = _rwkv_prep(q, mu_p, wl, w0r, a0r, kkr, kar, name="rwkv_prep")
    y_scan, sa_s, states, gathered = _scan_fwd(r_s, w_s, k_s, v_s, kap_s, b_s, name="scan_fwd", plan=scan_plan)
    ffn2_w = gathered[:3]
    w_out_f = gathered[3].reshape(D, D)
    cat = _rwkv_post(y_scan, r_s, k_s, v_s, g_s, y_pool, lnw, lnb, rkr, name="rwkv_post")
    f2 = _mm(cat, w_out_f, tm=1024, tn=1024, tk=2048, name="mix_out")
    x2 = _post_norm_res(x1, f2, gq[1], gws[1], name="mix_post")

    x3, sv3, _, _ = _ffn_forward(x2, ffn2_w, gp[2], gq[2], shifts[2], sc1p[2], gws[2], "ffn")

    loss_part, dx3 = _loss_head(x3, tgt, name="loss_head")
    loss = lax.psum(loss_part[0, 0], MESH_AXES)

    def by_core_chip(blocks):
        shp = blocks.shape
        t = blocks.astype(BF16).reshape((N_DEV // 2, 2) + shp[1:])
        return jnp.swapaxes(t, 0, 1)

    def chip_sums(mine, tag):
        got = _sibling_swap(mine, name=f"{tag}_swap")
        return [_pair_add(m, g, name=f"{tag}_add{i}") for i, (m, g) in enumerate(zip(mine, got))]

    def ffn_parts(pgu, pd):
        fs = F // N_DEV
        return pgu[:, :D, :fs], pgu[:, D:, :fs], pd[:, :fs, :]

    dx2, dwgu2, dwd2, pre3, post3, _ = _ffn_backward(dx3, x2, sv3, ffn2_w, gp[2], gq[2], sc1p[2], gws[2], "ffn")
    sums2_gu, sums2_d = chip_sums([dwgu2, dwd2], "scatter_ffn")

    df2, post2 = _post_norm_res_bwd(dx2, f2, gq[1], gws[1], 1.0, name="mix_post_bwd")
    dw_out = _mm(cat, df2, ta=True, tm=1024, tn=1024, tk=1024, name="mix_dwout")
    dcat = _mm(df2, w_out_f, tb=True, tm=1024, tn=1024, tk=2048, name="mix_dcat")
    dyr = dcat[:, PW:]
    dysc, dg, dr_b, dk2_b, dv_b, post_sums = _rwkv_post_bwd(dyr, y_scan, r_s, k_s, v_s, g_s, lnw, lnb, rkr,
                                                             name="rwkv_post_bwd")
    dr, dw, dk2, dv, dkap, db = _scan_bwd(r_s, w_s, k_s, v_s, kap_s, b_s, sa_s, dysc, states, name="scan_bwd")
    dps, dwl, prep_sums = _rwkv_prep_bwd(q, mu_p, wl, w0r, a0r, kkr, kar,
                                         (dr, dw, dk2, dv, dkap, db, dg, dr_b, dk2_b, dv_b), name="rwkv_prep_bwd")
    dq, dmu = _tshift_bwd(dps, q, mu_p, name="tshift_bwd")
    du_pool, dpool_w, dpool_scale = _pool_bwd(dcat, o_pool, W["pool_w"], pscale, name="pool_bwd")
    dp = jnp.concatenate([du_pool, dq], axis=1)
    dw_in = _mm(h2, dp, ta=True, tm=1024, tn=512, tk=1024, name="mix_dwin")
    dh2 = _mm(dp, w_in_p, tb=True, tm=1024, tn=1024, tk=2816, name="mix_dh")
    dx1, pre2 = _pre_norm_mod_bwd(dh2, x1, dx2, gp[1], sc1p[1], name="mix_pre_bwd")

    sums_mix = chip_sums([by_core_chip(_cols_split(dw_in[:, :IN_W])),
                          by_core_chip(dw_out.reshape(N_DEV, D // N_DEV, D))], "scatter_mixer")
    dx0, _, _, pre1, post1, got = _ffn_backward(
        dx1, xs, sv1, ffn1_w, gp[0], gq[0], sc1p[0], gws[0], "ffn",
        plans={"dwd": _ChipsPlan([sums2_d]), "down_bwd": _ChipsPlan(sums_mix), "dwgu": _ChipsPlan([sums2_gu])},
        own_sums=lambda blocks: chip_sums(blocks, "scatter_ffn"))

    pres, posts = [pre1, pre2, pre3], [post1, post2, post3]
    dmod = jnp.stack([jnp.stack([pres[s][0], pres[s][1], posts[s][0]]) for s in range(3)]).reshape(NMOD // 128, 128)
    dnorm_pre = jnp.stack([pres[s][2] for s in range(3)])
    dnorm_post = jnp.stack([posts[s][1] for s in range(3)])

    small = [dmu[0, :W["mu_shift"].shape[0]], dpool_w, dpool_scale, prep_sums[0], prep_sums[1], prep_sums[2],
             prep_sums[3], post_sums[2], post_sums[0], post_sums[1], dnorm_pre, dnorm_post,
             dwl[0:LORA_W, 0:R], dwl[LORA_W:LORA_W + LORA_A, R:2 * R],
             dwl[LORA_W + LORA_A:LORA_W + LORA_A + LORA_G, 2 * R:3 * R]]
    small_shapes = [a.shape for a in small]
    dmod8, small8 = _gather_two_level([dmod, _pack(small)], name="gather_grads")
    g_b_ada = _sum_parts(dmod8, name="sum_dmod").reshape(NMOD)
    red = _unpack(_sum_parts(small8, name="sum_small"), small_shapes)
    (g_mu, g_pool_w, g_pool_scale, g_w0, g_a0, g_kk, g_ka, g_rk, g_lnw, g_lnb, g_npre, g_npost, g_w2, g_a2,
     g_g2) = red

    dmod_all = dmod8.reshape(N_DEV, NMOD)
    dmod_cols = lax.dynamic_slice(dmod_all, (0, me * ada_c), (N_DEV, ada_c))
    dmod_cols = jnp.concatenate([dmod_cols, jnp.zeros_like(dmod_cols)], axis=0)
    g_w_ada = _mm(sc_pad, dmod_cols, ta=True, tm=D, tn=ada_c // 9, tk=16, name="ada_bwd")

    pg2, pu2, pd2 = ffn_parts(got["dwgu"][0], got["dwd"][0])
    pin, pout = got["down_bwd"]
    pg1, pu1, pd1 = ffn_parts(*got["dh"])

    res = {}

    def big(nm, parts, tag):
        res[nm] = _adamw(W[nm], M1[nm], V1[nm], parts, name=tag)

    big("ffn1_w_gate", pg1, "adamw_cols")
    big("ffn1_w_up", pu1, "adamw_cols")
    big("ffn1_w_down", pd1, "adamw_rows")
    big("ffn2_w_gate", pg2, "adamw_cols")
    big("ffn2_w_up", pu2, "adamw_cols")
    big("ffn2_w_down", pd2, "adamw_rows")
    big("w_in", pin, "adamw_w_in")
    big("w_out", pout, "adamw_w_out")
    big("w_ada", g_w_ada[None], "adamw_w_ada")

    def my_cols(full, width):
        return lax.dynamic_slice_in_dim(full, me * width, width, axis=full.ndim - 1)

    small_names = ["b_ada", "mu_shift", "pool_w", "pool_scale", "w0", "a0", "k_k", "k_a", "r_k", "lnx_w", "lnx_b",
                   "norm_pre", "norm_post", "w2", "a2", "g2"]
    small_grads = [g_b_ada, g_mu, g_pool_w, g_pool_scale, g_w0, g_a0, g_kk, g_ka, g_rk.reshape(W["r_k"].shape), g_lnw,
                   g_lnb, my_cols(g_npre, D // N_DEV), my_cols(g_npost, D // N_DEV), my_cols(g_w2, R // N_DEV),
                   my_cols(g_a2, R // N_DEV), my_cols(g_g2, R // N_DEV)]
    shapes = [W[n].shape for n in small_names]
    packed = _adamw(_pack([W[n] for n in small_names]), _pack([M1[n] for n in small_names]),
                    _pack([V1[n] for n in small_names]), _pack(small_grads)[None], name="adamw_small")
    unpacked = [_unpack(t, shapes) for t in packed]
    for i, nm in enumerate(small_names):
        res[nm] = tuple(unpacked[k][i] for k in range(4))

    outs = [loss, dx0[None]]
    for k in range(4):
        outs.extend(res[nm][k][None] for nm in names)
    return tuple(outs)
```

```python
import functools

import jax
import jax.numpy as jnp
from jax import lax
from jax.experimental import pallas as pl
from jax.experimental.pallas import tpu as pltpu

F32 = jnp.float32
BF16 = jnp.bfloat16
N_DEV = 8
MESH_AXES = ("x", "y", "c")

NORM_EPS = 1e-6
HEAD = 64
LN_X_EPS = 1e-5 * HEAD
POOL_GROUPS = 4
POOL_GROUP = 128
MACARON = 0.5
LORA_W, LORA_A, LORA_G = 64, 64, 224
LORA_PAD = 384
ADAM_LR, ADAM_B1, ADAM_B2, ADAM_EPS, ADAM_WD, ADAM_STEP = 0.001, 0.9, 0.999, 1e-08, 0.01, 10

FF_TILE = 768
ROW_TILE = 256
SCAN_T = 64
SCAN_T_BWD = 64
SCAN_G = 6
SCAN_G_BWD = 6
VMEM_CAP = 56 * 1024 * 1024


def _cp(sem, vmem_mb):
    return pltpu.CompilerParams(dimension_semantics=sem, vmem_limit_bytes=min(vmem_mb * 1024 * 1024, VMEM_CAP))


def _my_index():
    return 4 * lax.axis_index("x") + 2 * lax.axis_index("y") + lax.axis_index("c")


def _exchange(arrays, *, scatter, name):
    n = len(arrays)
    out_shapes = []
    for a in arrays:
        shp = a.shape if scatter else (N_DEV,) + a.shape
        out_shapes.append(jax.ShapeDtypeStruct(shp, a.dtype))

    def body(*refs):
        ins, outs = refs[:n], refs[n:2 * n]
        send_sems, recv_sems, local_sems = refs[2 * n:]
        me = _my_index()

        def dev(p):
            return (p // 4, (p // 2) % 2, p % 2)

        def copy(i, d):
            peer = (me + d) % N_DEV
            src = ins[i].at[peer] if scatter else ins[i]
            return pltpu.make_async_remote_copy(
                src_ref=src, dst_ref=outs[i].at[me], send_sem=send_sems.at[i, d - 1],
                recv_sem=recv_sems.at[i, d - 1], device_id=dev(peer), device_id_type=pl.DeviceIdType.MESH)

        def arrival(i, d):
            frm = (me + N_DEV - d) % N_DEV
            src = ins[i].at[frm] if scatter else ins[i]
            return pltpu.make_async_remote_copy(
                src_ref=src, dst_ref=outs[i].at[frm], send_sem=send_sems.at[i, d - 1],
                recv_sem=recv_sems.at[i, d - 1], device_id=dev(frm), device_id_type=pl.DeviceIdType.MESH)

        locals_ = []
        for i in range(n):
            src = ins[i].at[me] if scatter else ins[i]
            lc = pltpu.make_async_copy(src, outs[i].at[me], local_sems.at[i])
            lc.start()
            locals_.append(lc)
        sends = [copy(i, d) for d in range(1, N_DEV) for i in range(n)]
        for cp in sends:
            cp.start()
        for d in range(1, N_DEV):
            for i in range(n):
                arrival(i, d).wait_recv()
        for cp in sends:
            cp.wait_send()
        for lc in locals_:
            lc.wait()

    hbm = pl.BlockSpec(memory_space=pltpu.HBM)
    return pl.pallas_call(
        body, name=name, out_shape=tuple(out_shapes), in_specs=[hbm] * n, out_specs=tuple([hbm] * n),
        scratch_shapes=[pltpu.SemaphoreType.DMA((n, N_DEV - 1)), pltpu.SemaphoreType.DMA((n, N_DEV - 1)),
                        pltpu.SemaphoreType.DMA((n,))],
    )(*arrays)


def _remote(src, dst, send_sem, recv_sem, to):
    return pltpu.make_async_remote_copy(src_ref=src, dst_ref=dst, send_sem=send_sem, recv_sem=recv_sem,
                                        device_id=to, device_id_type=pl.DeviceIdType.MESH)


class _GatherPlan:
    def __init__(self, arrays):
        self.arrays = list(arrays)
        self.n = len(arrays)
        self.out_shapes = [jax.ShapeDtypeStruct((N_DEV,) + a.shape, a.dtype) for a in arrays]
        self.scratch = [pltpu.SemaphoreType.DMA((self.n, 7)), pltpu.SemaphoreType.DMA((self.n, 7)),
                        pltpu.SemaphoreType.DMA((self.n,))]

    def _parts(self, ins, outs, sems):
        send_sems, recv_sems, local_sems = sems
        x, y, c = lax.axis_index("x"), lax.axis_index("y"), lax.axis_index("c")
        chips = [(1 - x, y), (x, 1 - y), (1 - x, 1 - y)]

        def slot(i, px, py, pc):
            return outs[i].at[4 * px + 2 * py + pc]

        def copy(i, k, block, to, src=None):
            dst = slot(i, *block)
            return _remote(dst if src is None else src, dst, send_sems.at[i, k], recv_sems.at[i, k], to)

        n = self.n
        locals_ = [pltpu.make_async_copy(ins[i], slot(i, x, y, c), local_sems.at[i]) for i in range(n)]
        first = [copy(i, 1 + j, (x, y, c), (*chip, c), src=ins[i]) for j, chip in enumerate(chips) for i in range(n)]
        first += [copy(i, 0, (x, y, c), (x, y, 1 - c), src=ins[i]) for i in range(n)]
        return (x, y, c), chips, copy, locals_, first

    def start(self, ins, outs, sems):
        _, _, _, locals_, first = self._parts(ins, outs, sems)
        for lc in locals_:
            lc.start()
        for cp in first:
            cp.start()

    def finish(self, ins, outs, sems):
        (x, y, c), chips, copy, locals_, first = self._parts(ins, outs, sems)
        forwards = []
        for j, chip in enumerate(chips):
            for i in range(self.n):
                copy(i, 1 + j, (*chip, c), (x, y, c)).wait_recv()
                fwd = copy(i, 4 + j, (*chip, c), (x, y, 1 - c))
                fwd.start()
                forwards.append(fwd)
        for i in range(self.n):
            copy(i, 0, (x, y, 1 - c), (x, y, c)).wait_recv()
        for j, chip in enumerate(chips):
            for i in range(self.n):
                copy(i, 4 + j, (*chip, 1 - c), (x, y, c)).wait_recv()
        for cp in first + forwards:
            cp.wait_send()
        for lc in locals_:
            lc.wait()


class _ChipsPlan:
    def __init__(self, arrays):
        self.arrays = list(arrays)
        self.n = len(arrays)
        self.out_shapes = [jax.ShapeDtypeStruct(a.shape, a.dtype) for a in arrays]
        self.scratch = [pltpu.SemaphoreType.DMA((self.n, 3)), pltpu.SemaphoreType.DMA((self.n, 3)),
                        pltpu.SemaphoreType.DMA((self.n,))]

    def _parts(self, ins, outs, sems):
        send_sems, recv_sems, local_sems = sems
        x, y, c = lax.axis_index("x"), lax.axis_index("y"), lax.axis_index("c")
        mine = 2 * x + y
        chips = [(1 - x, y), (x, 1 - y), (1 - x, 1 - y)]
        n = self.n
        locals_ = [pltpu.make_async_copy(ins[i].at[mine], outs[i].at[mine], local_sems.at[i]) for i in range(n)]
        sends = [_remote(ins[i].at[2 * chip[0] + chip[1]], outs[i].at[mine], send_sems.at[i, j], recv_sems.at[i, j],
                         (*chip, c)) for j, chip in enumerate(chips) for i in range(n)]

        def arrivals():
            return [_remote(ins[i].at[2 * chip[0] + chip[1]], outs[i].at[2 * chip[0] + chip[1]], send_sems.at[i, j],
                            recv_sems.at[i, j], (*chip, c)) for j, chip in enumerate(chips) for i in range(n)]

        return locals_, sends, arrivals

    def start(self, ins, outs, sems):
        locals_, sends, _ = self._parts(ins, outs, sems)
        for lc in locals_:
            lc.start()
        for cp in sends:
            cp.start()

    def finish(self, ins, outs, sems):
        locals_, sends, arrivals = self._parts(ins, outs, sems)
        for cp in arrivals():
            cp.wait_recv()
        for cp in sends:
            cp.wait_send()
        for lc in locals_:
            lc.wait()


def _run_plan(plan, *, name):
    n = plan.n

    def body(*refs):
        ins, outs, sems = refs[:n], refs[n:2 * n], refs[2 * n:]
        plan.start(ins, outs, sems)
        plan.finish(ins, outs, sems)

    hbm = pl.BlockSpec(memory_space=pltpu.HBM)
    return pl.pallas_call(
        body, name=name, out_shape=tuple(plan.out_shapes), in_specs=[hbm] * n, out_specs=tuple([hbm] * n),
        scratch_shapes=plan.scratch,
    )(*plan.arrays)


def _host(body, plan, n_in, n_out, is_first, is_last):
    if plan is None:
        return body
    m = plan.n

    def wrapped(*refs):
        a, b = n_in, n_in + m
        c, d = b + n_out, b + n_out + m
        own_in, c_in, own_out, c_out, rest = refs[:a], refs[a:b], refs[b:c], refs[c:d], refs[d:]
        own_scr, c_sems = rest[:len(rest) - 3], rest[len(rest) - 3:]

        @pl.when(is_first())
        def _():
            plan.start(c_in, c_out, c_sems)

        body(*own_in, *own_out, *own_scr)

        @pl.when(is_last())
        def _():
            plan.finish(c_in, c_out, c_sems)

    return wrapped


def _host_args(plan):
    if plan is None:
        return [], [], [], [], []
    hbm = pl.BlockSpec(memory_space=pltpu.HBM)
    return [hbm] * plan.n, [hbm] * plan.n, list(plan.out_shapes), list(plan.scratch), list(plan.arrays)


def _gather_two_level(arrays, *, name):
    return _run_plan(_GatherPlan(arrays), name=name)


def _sibling_swap(arrays, *, name):
    n = len(arrays)
    out_shapes = [jax.ShapeDtypeStruct(a.shape[1:], a.dtype) for a in arrays]

    def body(*refs):
        ins, outs = refs[:n], refs[n:2 * n]
        send_sems, recv_sems = refs[2 * n:]
        x, y, c = lax.axis_index("x"), lax.axis_index("y"), lax.axis_index("c")
        copies = [_remote(ins[i].at[1 - c], outs[i], send_sems.at[i], recv_sems.at[i], (x, y, 1 - c))
                  for i in range(n)]
        for cp in copies:
            cp.start()
        for cp in copies:
            cp.wait_recv()
        for cp in copies:
            cp.wait_send()

    hbm = pl.BlockSpec(memory_space=pltpu.HBM)
    return pl.pallas_call(
        body, name=name, out_shape=tuple(out_shapes), in_specs=[hbm] * n, out_specs=tuple([hbm] * n),
        scratch_shapes=[pltpu.SemaphoreType.DMA((n,)), pltpu.SemaphoreType.DMA((n,))],
    )(*arrays)


def _chips_all_to_all(arrays, *, name):
    return _run_plan(_ChipsPlan(arrays), name=name)


def _pair_add(mine, got, *, name):
    _, nq, R, C = mine.shape
    tr = R
    for cand in (512, 256, 128, 64, 32, 16):
        if R % cand == 0 and cand * C * 2 * 3 * 2 <= 12 * 1024 * 1024:
            tr = cand
            break

    def body(core_ref, m_ref, g_ref, o_ref):
        o_ref[0] = (m_ref[0, 0].astype(F32) + g_ref[0].astype(F32)).astype(BF16)

    core = lax.axis_index("c").astype(jnp.int32).reshape(1)
    return pl.pallas_call(
        body, name=name,
        grid_spec=pltpu.PrefetchScalarGridSpec(
            num_scalar_prefetch=1, grid=(nq, R // tr),
            in_specs=[pl.BlockSpec((1, 1, tr, C), lambda q, i, core_ref: (core_ref[0], q, i, 0)),
                      pl.BlockSpec((1, tr, C), lambda q, i, core_ref: (q, i, 0))],
            out_specs=pl.BlockSpec((1, tr, C), lambda q, i, core_ref: (q, i, 0))),
        out_shape=jax.ShapeDtypeStruct((nq, R, C), BF16),
        compiler_params=_cp(("parallel", "parallel"), 40),
    )(core, mine, got)


def _mm(a, b, *, ta=False, tb=False, tm, tn, tk, out_dtype=F32, name):
    M = a.shape[1] if ta else a.shape[0]
    K = a.shape[0] if ta else a.shape[1]
    N = b.shape[0] if tb else b.shape[1]
    tm, tn, tk = min(tm, M), min(tn, N), min(tk, K)
    assert M % tm == 0 and N % tn == 0 and K % tk == 0, (name, M, N, K, tm, tn, tk)
    nk = K // tk
    dims = (((0 if ta else 1,), (1 if tb else 0,)), ((), ()))

    def body(a_ref, b_ref, o_ref, acc_ref):
        k = pl.program_id(2)

        @pl.when(k == 0)
        def _():
            acc_ref[...] = jnp.zeros_like(acc_ref)

        acc_ref[...] += lax.dot_general(a_ref[...].astype(BF16), b_ref[...].astype(BF16), dims,
                                        preferred_element_type=F32)

        @pl.when(k == nk - 1)
        def _():
            o_ref[...] = acc_ref[...].astype(out_dtype)

    a_spec = pl.BlockSpec((tk, tm), lambda i, j, k: (k, i)) if ta else pl.BlockSpec((tm, tk), lambda i, j, k: (i, k))
    b_spec = pl.BlockSpec((tn, tk), lambda i, j, k: (j, k)) if tb else pl.BlockSpec((tk, tn), lambda i, j, k: (k, j))
    blk = 2 * (tm * tk * a.dtype.itemsize + tk * tn * b.dtype.itemsize + tm * tn * jnp.dtype(out_dtype).itemsize)
    return pl.pallas_call(
        body, name=name, grid=(M // tm, N // tn, nk), in_specs=[a_spec, b_spec],
        out_specs=pl.BlockSpec((tm, tn), lambda i, j, k: (i, j)),
        out_shape=jax.ShapeDtypeStruct((M, N), out_dtype),
        scratch_shapes=[pltpu.VMEM((tm, tn), F32)],
        compiler_params=_cp(("parallel", "parallel", "arbitrary"), (blk + tm * tn * 4) // (1024 * 1024) + 12),
    )(a, b)


def _grid_ends(grid):
    def is_first():
        ok = pl.program_id(0) == 0
        for ax in range(1, len(grid)):
            ok = ok & (pl.program_id(ax) == 0)
        return ok

    def is_last():
        ok = pl.program_id(0) == grid[0] - 1
        for ax in range(1, len(grid)):
            ok = ok & (pl.program_id(ax) == grid[ax] - 1)
        return ok

    return is_first, is_last


def _semantics(plan, sem):
    return sem if plan is None else tuple("arbitrary" for _ in sem)


def _ffn_up(h, g8, u8, *, tm, tk, name, plan=None):
    S, D = h.shape
    nb, _, tn = g8.shape
    tm = min(tm, S)
    tk = min(tk, D)
    nk = D // tk

    def body(h_ref, g_ref, u_ref, au_ref, s_ref, acc_ref):
        k = pl.program_id(2)

        @pl.when(k == 0)
        def _():
            acc_ref[...] = jnp.zeros_like(acc_ref)

        hv = h_ref[...]
        acc_ref[:, :tn] += jnp.dot(hv, g_ref[0], preferred_element_type=F32)
        acc_ref[:, tn:] += jnp.dot(hv, u_ref[0], preferred_element_type=F32)

        @pl.when(k == nk - 1)
        def _():
            acc = acc_ref[...]
            a = acc[:, :tn]
            u = acc[:, tn:]
            au_ref[...] = acc.astype(BF16)
            s_ref[...] = (a * jax.nn.sigmoid(a) * u).astype(BF16)

    wspec = pl.BlockSpec((1, tk, tn), lambda i, j, k: (j, k, 0))
    grid = (S // tm, nb, nk)
    x_in, x_out, x_shapes, x_scr, x_ops = _host_args(plan)
    outs = pl.pallas_call(
        _host(body, plan, 3, 2, *_grid_ends(grid)), name=name, grid=grid,
        in_specs=[pl.BlockSpec((tm, tk), lambda i, j, k: (i, k)), wspec, wspec] + x_in,
        out_specs=tuple([pl.BlockSpec((tm, 2 * tn), lambda i, j, k: (i, j)),
                         pl.BlockSpec((tm, tn), lambda i, j, k: (i, j))] + x_out),
        out_shape=tuple([jax.ShapeDtypeStruct((S, 2 * nb * tn), BF16), jax.ShapeDtypeStruct((S, nb * tn), BF16)]
                        + x_shapes),
        scratch_shapes=[pltpu.VMEM((tm, 2 * tn), F32)] + x_scr,
        compiler_params=_cp(_semantics(plan, ("parallel", "parallel", "arbitrary")), 52),
    )(h, g8, u8, *x_ops)
    return outs[0], outs[1], tuple(outs[2:])


def _ffn_dh(dau, g8, u8, *, tm, tn, name, plan=None):
    S = dau.shape[0]
    nb, D, tf = g8.shape
    tm, tn = min(tm, S), min(tn, D)
    nk = 2 * nb
    nt = (((1,), (1,)), ((), ()))

    def body(a_ref, g_ref, u_ref, o_ref, acc_ref):
        k = pl.program_id(2)

        @pl.when(k == 0)
        def _():
            acc_ref[...] = jnp.zeros_like(acc_ref)

        @pl.when(k % 2 == 0)
        def _():
            acc_ref[...] += lax.dot_general(a_ref[...], g_ref[0], nt, preferred_element_type=F32)

        @pl.when(k % 2 == 1)
        def _():
            acc_ref[...] += lax.dot_general(a_ref[...], u_ref[0], nt, preferred_element_type=F32)

        @pl.when(k == nk - 1)
        def _():
            o_ref[...] = acc_ref[...]

    wspec = pl.BlockSpec((1, tn, tf), lambda i, n, k: (k // 2, n, 0))
    grid = (S // tm, D // tn, nk)
    x_in, x_out, x_shapes, x_scr, x_ops = _host_args(plan)
    outs = pl.pallas_call(
        _host(body, plan, 3, 1, *_grid_ends(grid)), name=name, grid=grid,
        in_specs=[pl.BlockSpec((tm, tf), lambda i, n, k: (i, k)), wspec, wspec] + x_in,
        out_specs=tuple([pl.BlockSpec((tm, tn), lambda i, n, k: (i, n))] + x_out),
        out_shape=tuple([jax.ShapeDtypeStruct((S, D), F32)] + x_shapes),
        scratch_shapes=[pltpu.VMEM((tm, tn), F32)] + x_scr,
        compiler_params=_cp(_semantics(plan, ("parallel", "parallel", "arbitrary")), 40),
    )(dau, g8, u8, *x_ops)
    return outs[0], tuple(outs[1:])


def _ffn_dwgu(h, dau, *, tm, tk, name, plan=None):
    S, D = h.shape
    tf = FF_TILE
    nt = dau.shape[1] // tf
    tm, tk = min(tm, D), min(tk, S)
    nk = S // tk
    ni = D // tm

    def body(a_ref, b_ref, o_ref, acc_ref):
        k = pl.program_id(2)

        @pl.when(k == 0)
        def _():
            acc_ref[...] = jnp.zeros_like(acc_ref)

        acc_ref[...] += lax.dot_general(a_ref[...], b_ref[...], (((0,), (0,)), ((), ())), preferred_element_type=F32)

        @pl.when(k == nk - 1)
        def _():
            o_ref[0, 0] = acc_ref[...].astype(BF16)

    grid = (ni, nt, nk)
    x_in, x_out, x_shapes, x_scr, x_ops = _host_args(plan)
    outs = pl.pallas_call(
        _host(body, plan, 2, 1, *_grid_ends(grid)), name=name, grid=grid,
        in_specs=[pl.BlockSpec((tk, tm), lambda i, j, k: (k, i)), pl.BlockSpec((tk, tf), lambda i, j, k: (k, j))] + x_in,
        out_specs=tuple([pl.BlockSpec((1, 1, tm, tf), lambda i, j, k: ((j // 2) % 2, j // 4, (j % 2) * ni + i, 0))]
                        + x_out),
        out_shape=tuple([jax.ShapeDtypeStruct((2, nt // 4, 2 * D, tf), BF16)] + x_shapes),
        scratch_shapes=[pltpu.VMEM((tm, tf), F32)] + x_scr,
        compiler_params=_cp(_semantics(plan, ("parallel", "parallel", "arbitrary")), 40),
    )(h, dau, *x_ops)
    return outs[0], tuple(outs[1:])


def _ffn_dwd(s, df, *, tn, tk, name, plan=None):
    S, D = df.shape
    tf = FF_TILE
    nb = s.shape[1] // tf
    tn, tk = min(tn, D), min(tk, S)
    nk = S // tk

    def body(a_ref, b_ref, o_ref, acc_ref):
        k = pl.program_id(2)

        @pl.when(k == 0)
        def _():
            acc_ref[...] = jnp.zeros_like(acc_ref)

        acc_ref[...] += lax.dot_general(a_ref[...], b_ref[...], (((0,), (0,)), ((), ())), preferred_element_type=F32)

        @pl.when(k == nk - 1)
        def _():
            o_ref[0, 0] = acc_ref[...].astype(BF16)

    grid = (nb, D // tn, nk)
    x_in, x_out, x_shapes, x_scr, x_ops = _host_args(plan)
    outs = pl.pallas_call(
        _host(body, plan, 2, 1, *_grid_ends(grid)), name=name, grid=grid,
        in_specs=[pl.BlockSpec((tk, tf), lambda j, n, k: (k, j)), pl.BlockSpec((tk, tn), lambda j, n, k: (k, n))] + x_in,
        out_specs=tuple([pl.BlockSpec((1, 1, tf, tn), lambda j, n, k: (j % 2, j // 2, 0, n))] + x_out),
        out_shape=tuple([jax.ShapeDtypeStruct((2, nb // 2, tf, D), BF16)] + x_shapes),
        scratch_shapes=[pltpu.VMEM((tf, tn), F32)] + x_scr,
        compiler_params=_cp(_semantics(plan, ("parallel", "parallel", "arbitrary")), 40),
    )(s, df, *x_ops)
    return outs[0], tuple(outs[1:])


def _ffn_down_bwd(df, d8, au, *, tm, tk, name, plan=None):
    S, D = df.shape
    nb, tn, _ = d8.shape
    F = nb * tn
    tm = min(tm, S)
    tk = min(tk, D)
    nk = D // tk

    def body(df_ref, w_ref, au_ref, dau_ref, acc_ref):
        k = pl.program_id(2)

        @pl.when(k == 0)
        def _():
            acc_ref[...] = jnp.zeros_like(acc_ref)

        acc_ref[...] += lax.dot_general(df_ref[...], w_ref[0], (((1,), (1,)), ((), ())), preferred_element_type=F32)

        @pl.when(k == nk - 1)
        def _():
            ds = acc_ref[...]
            au_v = au_ref[...].astype(F32)
            a = au_v[:, :tn]
            u = au_v[:, tn:]
            sg = jax.nn.sigmoid(a)
            da = ds * u * (sg * (1.0 + a * (1.0 - sg)))
            du = ds * (a * sg)
            dau_ref[:, :tn] = da.astype(BF16)
            dau_ref[:, tn:] = du.astype(BF16)

    grid = (S // tm, F // tn, nk)
    x_in, x_out, x_shapes, x_scr, x_ops = _host_args(plan)
    outs = pl.pallas_call(
        _host(body, plan, 3, 1, *_grid_ends(grid)), name=name, grid=grid,
        in_specs=[pl.BlockSpec((tm, tk), lambda i, j, k: (i, k)), pl.BlockSpec((1, tn, tk), lambda i, j, k: (j, 0, k)),
                  pl.BlockSpec((tm, 2 * tn), lambda i, j, k: (i, j))] + x_in,
        out_specs=tuple([pl.BlockSpec((tm, 2 * tn), lambda i, j, k: (i, j))] + x_out),
        out_shape=tuple([jax.ShapeDtypeStruct((S, 2 * F), BF16)] + x_shapes),
        scratch_shapes=[pltpu.VMEM((tm, tn), F32)] + x_scr,
        compiler_params=_cp(_semantics(plan, ("parallel", "parallel", "arbitrary")), 52),
    )(df, d8, au, *x_ops)
    return outs[0], tuple(outs[1:])


def _fold8(x):
    tm, w = x.shape
    return jnp.sum(x.reshape(tm // 8, 8, w), axis=0)


def _row_spec(tm, w):
    return pl.BlockSpec((tm, w), lambda i: (i, 0))


def _vec_spec(rows, w):
    return pl.BlockSpec((rows, w), lambda i: (0, 0))


def _pre_norm_mod(x, gain, shift, scale1p, *, name):
    S, D = x.shape
    tm = ROW_TILE

    def body(x_ref, g_ref, sh_ref, sc_ref, h_ref):
        xv = x_ref[...]
        rinv = lax.rsqrt(jnp.mean(xv * xv, axis=-1, keepdims=True) + NORM_EPS)
        h_ref[...] = ((xv * rinv) * g_ref[...] * sc_ref[...] + sh_ref[...]).astype(BF16)

    return pl.pallas_call(
        body, name=name, grid=(S // tm,),
        in_specs=[_row_spec(tm, D), _vec_spec(1, D), _vec_spec(1, D), _vec_spec(1, D)],
        out_specs=_row_spec(tm, D), out_shape=jax.ShapeDtypeStruct((S, D), BF16),
        compiler_params=_cp(("parallel",), 32),
    )(x, gain, shift, scale1p)


def _pre_norm_mod_bwd(dh, x, dres, gain, scale1p, *, name):
    S, D = x.shape
    tm = ROW_TILE
    n = S // tm

    def body(dh_ref, x_ref, dr_ref, g_ref, sc_ref, dx_ref, sums_ref, acc_ref):
        i = pl.program_id(0)

        @pl.when(i == 0)
        def _():
            acc_ref[...] = jnp.zeros_like(acc_ref)

        xv = x_ref[...]
        dhv = dh_ref[...]
        g = g_ref[...]
        rinv = lax.rsqrt(jnp.mean(xv * xv, axis=-1, keepdims=True) + NORM_EPS)
        xn = xv * rinv
        dn = dhv * sc_ref[...]
        dxn = dn * g
        dx_ref[...] = dr_ref[...] + rinv * (dxn - xn * jnp.mean(dxn * xn, axis=-1, keepdims=True))
        acc_ref[0] += _fold8(dhv)
        acc_ref[1] += _fold8(dhv * (xn * g))
        acc_ref[2] += _fold8(dn * xn)

        @pl.when(i == n - 1)
        def _():
            for q in range(3):
                sums_ref[q:q + 1, :] = jnp.sum(acc_ref[q], axis=0, keepdims=True)

    return pl.pallas_call(
        body, name=name, grid=(n,),
        in_specs=[_row_spec(tm, D), _row_spec(tm, D), _row_spec(tm, D), _vec_spec(1, D), _vec_spec(1, D)],
        out_specs=(_row_spec(tm, D), _vec_spec(3, D)),
        out_shape=(jax.ShapeDtypeStruct((S, D), F32), jax.ShapeDtypeStruct((3, D), F32)),
        scratch_shapes=[pltpu.VMEM((3, 8, D), F32)],
        compiler_params=_cp(("arbitrary",), 40),
    )(dh, x, dres, gain, scale1p)


def _post_norm_res(x, f, gain, gw, *, name):
    S, D = x.shape
    tm = ROW_TILE

    def body(x_ref, f_ref, g_ref, gw_ref, o_ref):
        fv = f_ref[...]
        rinv = lax.rsqrt(jnp.mean(fv * fv, axis=-1, keepdims=True) + NORM_EPS)
        o_ref[...] = x_ref[...] + gw_ref[...] * ((fv * rinv) * g_ref[...])

    return pl.pallas_call(
        body, name=name, grid=(S // tm,),
        in_specs=[_row_spec(tm, D), _row_spec(tm, D), _vec_spec(1, D), _vec_spec(1, D)],
        out_specs=_row_spec(tm, D), out_shape=jax.ShapeDtypeStruct((S, D), F32),
        compiler_params=_cp(("parallel",), 32),
    )(x, f, gain, gw)


def _post_norm_res_bwd(dxo, f, gain, gw, weight, *, name):
    S, D = f.shape
    tm = ROW_TILE
    n = S // tm

    def body(d_ref, f_ref, g_ref, gw_ref, df_ref, sums_ref, acc_ref):
        i = pl.program_id(0)

        @pl.when(i == 0)
        def _():
            acc_ref[...] = jnp.zeros_like(acc_ref)

        fv = f_ref[...]
        dv = d_ref[...]
        g = g_ref[...]
        rinv = lax.rsqrt(jnp.mean(fv * fv, axis=-1, keepdims=True) + NORM_EPS)
        fh = fv * rinv
        dy = dv * gw_ref[...]
        dfh = dy * g
        df_ref[...] = (rinv * (dfh - fh * jnp.mean(dfh * fh, axis=-1, keepdims=True))).astype(BF16)
        acc_ref[0] += _fold8(weight * dv * (fh * g))
        acc_ref[1] += _fold8(dy * fh)

        @pl.when(i == n - 1)
        def _():
            for q in range(2):
                sums_ref[q:q + 1, :] = jnp.sum(acc_ref[q], axis=0, keepdims=True)

    return pl.pallas_call(
        body, name=name, grid=(n,),
        in_specs=[_row_spec(tm, D), _row_spec(tm, D), _vec_spec(1, D), _vec_spec(1, D)],
        out_specs=(_row_spec(tm, D), _vec_spec(2, D)),
        out_shape=(jax.ShapeDtypeStruct((S, D), BF16), jax.ShapeDtypeStruct((2, D), F32)),
        scratch_shapes=[pltpu.VMEM((2, 8, D), F32)],
        compiler_params=_cp(("arbitrary",), 40),
    )(dxo, f, gain, gw)


def _loss_head(y, target, *, name):
    S, D = y.shape
    tm = ROW_TILE

    def body(y_ref, t_ref, l_ref, dy_ref):
        i = pl.program_id(0)

        @pl.when(i == 0)
        def _():
            l_ref[...] = jnp.zeros_like(l_ref)

        err = y_ref[...] - t_ref[...]
        dy_ref[...] = err * (1.0 / D)
        row = jnp.sum(err * err, axis=-1, keepdims=True) * (0.5 / D)
        l_ref[...] += jnp.sum(row, axis=0, keepdims=True)

    return pl.pallas_call(
        body, name=name, grid=(S // tm,),
        in_specs=[_row_spec(tm, D), _row_spec(tm, D)],
        out_specs=(_vec_spec(1, 1), _row_spec(tm, D)),
        out_shape=(jax.ShapeDtypeStruct((1, 1), F32), jax.ShapeDtypeStruct((S, D), F32)),
        compiler_params=_cp(("arbitrary",), 32),
    )(y, target)


def _shift_down(z, j, row):
    return jnp.where(row >= j, pltpu.roll(z, j, 0), 0.0)


def _shift_up(z, j, row, n):
    return jnp.where(row < n - j, pltpu.roll(z, n - j, 0), 0.0)


def _pool_fwd(p, pool_w, pool_scale, *, name):
    S = p.shape[0]
    C = POOL_GROUP

    def body(u_ref, w_ref, sc_ref, o_ref, y_ref):
        g = pl.program_id(0)
        u = u_ref[...]
        row = lax.broadcasted_iota(jnp.int32, (S, C), 0)
        s1 = u + _shift_down(u, 1, row)
        s2 = s1 + _shift_down(s1, 2, row)
        s3 = s2 + _shift_down(s2, 4, row)
        s4 = s3 + _shift_down(s3, 8, row)
        gi = jnp.zeros((S, C), jnp.int32) + g
        win = jnp.where(gi == 0, s1, jnp.where(gi == 1, s2, jnp.where(gi == 2, s3, s4)))
        width = jnp.where(gi == 0, 2, jnp.where(gi == 1, 4, jnp.where(gi == 2, 8, 16)))
        count = jnp.minimum(row + 1, width).astype(F32)
        o = win / count - u
        o_ref[...] = o
        y_ref[...] = jnp.dot(o.astype(BF16), w_ref[0].astype(BF16), preferred_element_type=F32) * sc_ref[...]

    col = pl.BlockSpec((S, C), lambda g: (0, g))
    return pl.pallas_call(
        body, name=name, grid=(POOL_GROUPS,),
        in_specs=[col, pl.BlockSpec((1, C, C), lambda g: (g, 0, 0)), pl.BlockSpec((1, C), lambda g: (0, g))],
        out_specs=(col, col),
        out_shape=(jax.ShapeDtypeStruct((S, POOL_GROUPS * C), F32), jax.ShapeDtypeStruct((S, POOL_GROUPS * C), F32)),
        compiler_params=_cp(("parallel",), 48),
    )(p, pool_w, pool_scale)


def _pool_bwd(dcat, o, pool_w, pool_scale, *, name):
    S = o.shape[0]
    C = POOL_GROUP

    def body(dy_ref, o_ref, w_ref, sc_ref, du_ref, dw_ref, dsc_ref):
        g = pl.program_id(0)
        dy = dy_ref[...]
        ob = o_ref[...].astype(BF16)
        wb = w_ref[0].astype(BF16)
        mixed = jnp.dot(ob, wb, preferred_element_type=F32)
        dsc_ref[...] = jnp.sum(_fold8(dy * mixed), axis=0, keepdims=True)
        dmix = (dy * sc_ref[...]).astype(BF16)
        dw_ref[0] = lax.dot_general(ob, dmix, (((0,), (0,)), ((), ())), preferred_element_type=F32)
        do = lax.dot_general(dmix, wb, (((1,), (1,)), ((), ())), preferred_element_type=F32)
        row = lax.broadcasted_iota(jnp.int32, (S, C), 0)
        gi = jnp.zeros((S, C), jnp.int32) + g
        width = jnp.where(gi == 0, 2, jnp.where(gi == 1, 4, jnp.where(gi == 2, 8, 16)))
        z = do / jnp.minimum(row + 1, width).astype(F32)
        s1 = z + _shift_up(z, 1, row, S)
        s2 = s1 + _shift_up(s1, 2, row, S)
        s3 = s2 + _shift_up(s2, 4, row, S)
        s4 = s3 + _shift_up(s3, 8, row, S)
        win = jnp.where(gi == 0, s1, jnp.where(gi == 1, s2, jnp.where(gi == 2, s3, s4)))
        du_ref[...] = (win - do).astype(BF16)

    col = pl.BlockSpec((S, C), lambda g: (0, g))
    return pl.pallas_call(
        body, name=name, grid=(POOL_GROUPS,),
        in_specs=[col, col, pl.BlockSpec((1, C, C), lambda g: (g, 0, 0)), pl.BlockSpec((1, C), lambda g: (0, g))],
        out_specs=(col, pl.BlockSpec((1, C, C), lambda g: (g, 0, 0)), pl.BlockSpec((1, C), lambda g: (0, g))),
        out_shape=(jax.ShapeDtypeStruct((S, POOL_GROUPS * C), BF16), jax.ShapeDtypeStruct((POOL_GROUPS, C, C), F32),
                   jax.ShapeDtypeStruct((1, POOL_GROUPS * C), F32)),
        compiler_params=_cp(("parallel",), 48),
    )(dcat, o, pool_w, pool_scale)


def _block_ones():
    r = lax.broadcasted_iota(jnp.int32, (128, 128), 0) // HEAD
    c = lax.broadcasted_iota(jnp.int32, (128, 128), 1) // HEAD
    return jnp.where(r == c, 1.0, 0.0).astype(BF16)


def _segsum(x, bd):
    outs = []
    for j in range(x.shape[1] // 128):
        xs = x[:, j * 128:(j + 1) * 128]
        hi = xs.astype(BF16)
        lo = (xs - hi.astype(F32)).astype(BF16)
        outs.append(jnp.dot(hi, bd, preferred_element_type=F32) + jnp.dot(lo, bd, preferred_element_type=F32))
    return jnp.concatenate(outs, axis=1)


def _prep_common(q, qprev, first, mu, wl, w0, a0, kkw, kaw, R):
    tm = q.shape[0]
    row = lax.broadcasted_iota(jnp.int32, q.shape, 0)
    last = qprev[7:8, :] * first
    prev = jnp.where(row == 0, last, pltpu.roll(q, 1, 0))
    ps = q + mu * (prev - q)
    r = ps[:, 0:R]
    k = ps[:, R:2 * R]
    v = ps[:, 2 * R:3 * R]
    lo_in = ps[:, 3 * R:3 * R + LORA_PAD]
    lane = lax.broadcasted_iota(jnp.int32, (tm, LORA_PAD), 1)
    m_w = lane < LORA_W
    m_a = lane < LORA_W + LORA_A
    m_g = lane < LORA_W + LORA_A + LORA_G
    act = jnp.where(m_w, jnp.tanh(lo_in), jnp.where(m_a, lo_in, jnp.where(m_g, jax.nn.sigmoid(lo_in), 0.0)))
    lo = jnp.dot(act.astype(BF16), wl, preferred_element_type=F32)
    wpre = w0 + lo[:, 0:R]
    apre = a0 + lo[:, R:2 * R]
    g = lo[:, 2 * R:3 * R]
    neg = -wpre
    softplus = jnp.maximum(neg, 0.0) + jnp.log(1.0 + jnp.exp(-jnp.abs(neg)))
    wlog = -softplus - 0.5
    ew = jnp.exp(wlog)
    decay = jnp.exp(-ew)
    a = jax.nn.sigmoid(apre)
    kk = k * kkw
    bd = _block_ones()
    n2 = _segsum(kk * kk, bd)
    nrm = jnp.maximum(jnp.sqrt(n2), 1e-12)
    kap = kk / nrm
    kmul = 1.0 + (a - 1.0) * kaw
    k2 = k * kmul
    return dict(prev=prev, r=r, k=k, v=v, act=act, m_w=m_w, m_a=m_a, m_g=m_g, wpre=wpre, g=g, ew=ew, decay=decay,
                a=a, n2=n2, nrm=nrm, kap=kap, kmul=kmul, k2=k2, bd=bd)


def _prev_rows_spec(tm, w):
    return pl.BlockSpec((8, w), lambda i: (jnp.maximum(i * (tm // 8) - 1, 0), 0))


def _rwkv_prep(q, mu, wl, w0, a0, kkw, kaw, *, name):
    S, QW = q.shape
    R = w0.shape[1]
    tm = ROW_TILE // 2

    def body(q_ref, qp_ref, mu_ref, wl_ref, w0_ref, a0_ref, kk_ref, ka_ref, r_ref, w_ref, k_ref, v_ref, kap_ref,
             b_ref, g_ref):
        first = jnp.where(pl.program_id(0) > 0, 1.0, 0.0)
        t = _prep_common(q_ref[...], qp_ref[...], first, mu_ref[...], wl_ref[...], w0_ref[...], a0_ref[...],
                         kk_ref[...], ka_ref[...], R)
        r_ref[...] = t["r"]
        w_ref[...] = t["decay"]
        k_ref[...] = t["k2"]
        v_ref[...] = t["v"]
        kap_ref[...] = t["kap"]
        b_ref[...] = t["kap"] * t["a"]
        g_ref[...] = t["g"]

    vec = _vec_spec(1, R)
    return pl.pallas_call(
        body, name=name, grid=(S // tm,),
        in_specs=[_row_spec(tm, QW), _prev_rows_spec(tm, QW), _vec_spec(1, QW), _vec_spec(LORA_PAD, 3 * R), vec, vec,
                  vec, vec],
        out_specs=tuple([_row_spec(tm, R)] * 7),
        out_shape=tuple([jax.ShapeDtypeStruct((S, R), F32)] * 7),
        compiler_params=_cp(("parallel",), 48),
    )(q, q, mu, wl, w0, a0, kkw, kaw)


def _rwkv_prep_bwd(q, mu, wl, w0, a0, kkw, kaw, grads, *, name):
    S, QW = q.shape
    R = w0.shape[1]
    tm = ROW_TILE // 2
    n = S // tm

    def body(q_ref, qp_ref, mu_ref, wl_ref, w0_ref, a0_ref, kk_ref, ka_ref, dr_ref, dw_ref, dk2_ref, dv_ref, dkap_ref,
             db_ref, dg_ref, drb_ref, dk2b_ref, dvb_ref, dps_ref, dwl_ref, sums_ref, acc_ref):
        i = pl.program_id(0)

        @pl.when(i == 0)
        def _():
            acc_ref[...] = jnp.zeros_like(acc_ref)
            dwl_ref[...] = jnp.zeros_like(dwl_ref)

        first = jnp.where(i > 0, 1.0, 0.0)
        wl = wl_ref[...]
        kkw = kk_ref[...]
        kaw = ka_ref[...]
        t = _prep_common(q_ref[...], qp_ref[...], first, mu_ref[...], wl, w0_ref[...], a0_ref[...], kkw, kaw, R)
        a, kap, k, act = t["a"], t["kap"], t["k"], t["act"]
        db = db_ref[...]
        dk2 = dk2_ref[...] + dk2b_ref[...]
        dkap = dkap_ref[...] + db * a
        da = db * kap + dk2 * k * kaw
        dk = dk2 * t["kmul"]
        proj = jnp.where(jnp.sqrt(t["n2"]) > 1e-12, _segsum(kap * dkap, t["bd"]), 0.0)
        dkk = (dkap - kap * proj) / t["nrm"]
        dk = dk + dkk * kkw
        dapre = da * a * (1.0 - a)
        dwlog = dw_ref[...] * t["decay"] * (-t["ew"])
        dwpre = dwlog * jax.nn.sigmoid(-t["wpre"])
        acc_ref[0] += _fold8(dwpre)
        acc_ref[1] += _fold8(dapre)
        acc_ref[2] += _fold8(dkk * k)
        acc_ref[3] += _fold8(dk2 * k * (a - 1.0))
        dlo = jnp.concatenate([dwpre, dapre, dg_ref[...]], axis=1).astype(BF16)
        dwl_ref[...] += lax.dot_general(act.astype(BF16), dlo, (((0,), (0,)), ((), ())), preferred_element_type=F32)
        dact = lax.dot_general(dlo, wl, (((1,), (1,)), ((), ())), preferred_element_type=F32)
        dlin = jnp.where(t["m_w"], dact * (1.0 - act * act),
                         jnp.where(t["m_a"], dact, jnp.where(t["m_g"], dact * act * (1.0 - act), 0.0)))
        dps_ref[:, 0:R] = dr_ref[...] + drb_ref[...]
        dps_ref[:, R:2 * R] = dk
        dps_ref[:, 2 * R:3 * R] = dv_ref[...] + dvb_ref[...]
        dps_ref[:, 3 * R:3 * R + LORA_PAD] = dlin
        dps_ref[:, 3 * R + LORA_PAD:] = jnp.zeros((tm, QW - 3 * R - LORA_PAD), F32)

        @pl.when(i == n - 1)
        def _():
            for j in range(4):
                sums_ref[j:j + 1, :] = jnp.sum(acc_ref[j], axis=0, keepdims=True)

    vec = _vec_spec(1, R)
    return pl.pallas_call(
        body, name=name, grid=(n,),
        in_specs=[_row_spec(tm, QW), _prev_rows_spec(tm, QW), _vec_spec(1, QW), _vec_spec(LORA_PAD, 3 * R), vec, vec,
                  vec, vec] + [_row_spec(tm, R)] * 10,
        out_specs=(_row_spec(tm, QW), _vec_spec(LORA_PAD, 3 * R), _vec_spec(4, R)),
        out_shape=(jax.ShapeDtypeStruct((S, QW), F32), jax.ShapeDtypeStruct((LORA_PAD, 3 * R), F32),
                   jax.ShapeDtypeStruct((4, R), F32)),
        scratch_shapes=[pltpu.VMEM((4, 8, R), F32)],
        compiler_params=_cp(("arbitrary",), 56),
    )(q, q, mu, wl, w0, a0, kkw, kaw, *grads)


def _tshift_bwd(dps, q, mu, *, name):
    S, QW = q.shape
    tm = ROW_TILE // 2
    n = S // tm

    def body(d_ref, dn_ref, q_ref, qp_ref, mu_ref, dq_ref, dmu_ref, acc_ref):
        i = pl.program_id(0)

        @pl.when(i == 0)
        def _():
            acc_ref[...] = jnp.zeros_like(acc_ref)

        mu = mu_ref[...]
        d = d_ref[...]
        qv = q_ref[...]
        row = lax.broadcasted_iota(jnp.int32, d.shape, 0)
        first = jnp.where(i > 0, 1.0, 0.0)
        notlast = jnp.where(i < n - 1, 1.0, 0.0)
        prev = jnp.where(row == 0, qp_ref[7:8, :] * first, pltpu.roll(qv, 1, 0))
        z = d * mu
        nxt = jnp.where(row == tm - 1, dn_ref[0:1, :] * mu * notlast, pltpu.roll(z, tm - 1, 0))
        dq_ref[...] = (d * (1.0 - mu) + nxt).astype(BF16)
        acc_ref[...] += _fold8(d * (prev - qv))

        @pl.when(i == n - 1)
        def _():
            dmu_ref[...] = jnp.sum(acc_ref[...], axis=0, keepdims=True)

    nblk8 = S // 8
    next_spec = pl.BlockSpec((8, QW), lambda i: (jnp.minimum((i + 1) * (tm // 8), nblk8 - 1), 0))
    return pl.pallas_call(
        body, name=name, grid=(n,),
        in_specs=[_row_spec(tm, QW), next_spec, _row_spec(tm, QW), _prev_rows_spec(tm, QW), _vec_spec(1, QW)],
        out_specs=(_row_spec(tm, QW), _vec_spec(1, QW)),
        out_shape=(jax.ShapeDtypeStruct((S, QW), BF16), jax.ShapeDtypeStruct((1, QW), F32)),
        scratch_shapes=[pltpu.VMEM((8, QW), F32)],
        compiler_params=_cp(("arbitrary",), 48),
    )(dps, dps, q, q, mu)


def _post_common(ysc, r, k2, v, lnw, lnb, rk):
    bd = _block_ones()
    mean = _segsum(ysc, bd) * (1.0 / HEAD)
    d = ysc - mean
    var = _segsum(d * d, bd) * (1.0 / HEAD)
    rstd = lax.rsqrt(var + LN_X_EPS)
    yh = d * rstd
    rkk = _segsum(r * k2 * rk, bd)
    z = yh * lnw + lnb + rkk * v
    return bd, rstd, yh, rkk, z


def _rwkv_post(ysc, r, k2, v, g, ypool, lnw, lnb, rk, *, name):
    S, R = ysc.shape
    PW = ypool.shape[1]
    tm = ROW_TILE

    def body(y_ref, r_ref, k_ref, v_ref, g_ref, yp_ref, lw_ref, lb_ref, rk_ref, cat_ref):
        _, _, _, _, z = _post_common(y_ref[...], r_ref[...], k_ref[...], v_ref[...], lw_ref[...], lb_ref[...],
                                     rk_ref[...])
        cat_ref[:, 0:PW] = yp_ref[...].astype(BF16)
        cat_ref[:, PW:] = (z * g_ref[...]).astype(BF16)

    vec = _vec_spec(1, R)
    return pl.pallas_call(
        body, name=name, grid=(S // tm,),
        in_specs=[_row_spec(tm, R)] * 5 + [_row_spec(tm, PW), vec, vec, vec],
        out_specs=_row_spec(tm, PW + R), out_shape=jax.ShapeDtypeStruct((S, PW + R), BF16),
        compiler_params=_cp(("parallel",), 48),
    )(ysc, r, k2, v, g, ypool, lnw, lnb, rk)


def _rwkv_post_bwd(dcat, ysc, r, k2, v, g, lnw, lnb, rk, *, name):
    S, R = ysc.shape
    tm = ROW_TILE
    n = S // tm

    def body(d_ref, y_ref, r_ref, k_ref, v_ref, g_ref, lw_ref, lb_ref, rk_ref, dy_ref, dg_ref, drb_ref, dkb_ref,
             dvb_ref, sums_ref, acc_ref):
        i = pl.program_id(0)

        @pl.when(i == 0)
        def _():
            acc_ref[...] = jnp.zeros_like(acc_ref)

        rv, kv, vv, lw, rkw = r_ref[...], k_ref[...], v_ref[...], lw_ref[...], rk_ref[...]
        bd, rstd, yh, rkk, z = _post_common(y_ref[...], rv, kv, vv, lw, lb_ref[...], rkw)
        dyr = d_ref[...]
        dg_ref[...] = dyr * z
        dz = dyr * g_ref[...]
        dyh = dz * lw
        dy_ref[...] = rstd * (dyh - _segsum(dyh, bd) * (1.0 / HEAD) - yh * (_segsum(dyh * yh, bd) * (1.0 / HEAD)))
        dvb_ref[...] = dz * rkk
        drkk = _segsum(dz * vv, bd)
        drb_ref[...] = drkk * kv * rkw
        dkb_ref[...] = drkk * rv * rkw
        acc_ref[0] += _fold8(dz * yh)
        acc_ref[1] += _fold8(dz)
        acc_ref[2] += _fold8(drkk * rv * kv)

        @pl.when(i == n - 1)
        def _():
            for j in range(3):
                sums_ref[j:j + 1, :] = jnp.sum(acc_ref[j], axis=0, keepdims=True)

    vec = _vec_spec(1, R)
    dspec = _row_spec(tm, R)
    return pl.pallas_call(
        body, name=name, grid=(n,),
        in_specs=[dspec] + [_row_spec(tm, R)] * 5 + [vec, vec, vec],
        out_specs=tuple([_row_spec(tm, R)] * 5) + (_vec_spec(3, R),),
        out_shape=tuple([jax.ShapeDtypeStruct((S, R), F32)] * 5) + (jax.ShapeDtypeStruct((3, R), F32),),
        scratch_shapes=[pltpu.VMEM((3, 8, R), F32)],
        compiler_params=_cp(("arbitrary",), 56),
    )(dcat, ysc, r, k2, v, g, lnw, lnb, rk)


SEL_ROWS = 64


def _column_selector():
    row = lax.broadcasted_iota(jnp.int32, (SEL_ROWS, 8 * 128), 0)
    col = lax.broadcasted_iota(jnp.int32, (SEL_ROWS, 8 * 128), 1)
    head, rest = row // 32, row % 32
    hit = (rest < 24) & (rest % 8 == col // 128) & (head == (col % 128) // HEAD)
    return jnp.where(hit, 1.0, 0.0).astype(BF16)


def _expand_columns(x, sel):
    hi = x.astype(BF16).astype(F32)
    r1 = x - hi
    mid = r1.astype(BF16).astype(F32)
    lo = (r1 - mid).astype(BF16).astype(F32)
    terms = jnp.concatenate([hi, mid, lo, jnp.zeros_like(x)], axis=0)
    both = jnp.concatenate([terms, pltpu.roll(terms, HEAD, 1)], axis=0)[:, 0:HEAD]
    return lax.dot_general(both.astype(BF16), sel, (((0,), (0,)), ((), ())), preferred_element_type=F32)


def _head_sum_weights():
    row = lax.broadcasted_iota(jnp.int32, (256, 256), 0)
    col = lax.broadcasted_iota(jnp.int32, (256, 256), 1)
    return jnp.where((row % 128) // HEAD == col // 128, 1.0, 0.0).astype(BF16)


def _head_sums_mxu(products, w2):
    rows = []
    for p in products:
        hi = p.astype(BF16)
        rows.append(jnp.concatenate([hi, (p - hi.astype(F32)).astype(BF16)], axis=1))
    out = jnp.dot(jnp.concatenate(rows, axis=0), w2, preferred_element_type=F32)
    return [(out[i * HEAD:(i + 1) * HEAD, 0:128], out[i * HEAD:(i + 1) * HEAD, 128:256]) for i in range(len(products))]


def _masked_rows(rows, negate=False):
    head_a = (lax.broadcasted_iota(jnp.int32, rows.shape, 1) % 128) < HEAD
    v = -rows if negate else rows
    return jnp.where(head_a, v, 0.0), jnp.where(head_a, 0.0, v)


def _lane_sums(x, row_a, row_b):
    return jnp.sum(x * row_a, axis=1, keepdims=True), jnp.sum(x * row_b, axis=1, keepdims=True)


def _scan_fwd(r, w, k, v, kap, b, *, name, plan=None):
    S, R = r.shape
    G, T = SCAN_G, SCAN_T
    NP = R // 128
    assert NP % G == 0 and S % T == 0
    GW = 128 * G

    def body(r_ref, w_ref, k_ref, v_ref, kap_ref, b_ref, sel_ref, w2_ref, y_ref, sa_ref, st_ref, s_scr, vc_scr,
             yt_scr, sat_scr):
        c = pl.program_id(1)

        @pl.when(c == 0)
        def _():
            s_scr[...] = jnp.zeros_like(s_scr)

        yt_scr[...] = jnp.zeros_like(yt_scr)
        sat_scr[...] = jnp.zeros_like(sat_scr)
        lane = lax.broadcasted_iota(jnp.int32, (HEAD, 128), 1)
        m_a = lane < HEAD

        def block(tb, carry):
            t0 = pl.multiple_of(tb * 8, 8)
            rb, wb, kb = r_ref[pl.ds(t0, 8), :], w_ref[pl.ds(t0, 8), :], k_ref[pl.ds(t0, 8), :]
            pb, bb, vb = kap_ref[pl.ds(t0, 8), :], b_ref[pl.ds(t0, 8), :], v_ref[pl.ds(t0, 8), :]
            for g in range(G):
                vc_scr[g] = _expand_columns(vb[:, g * 128:(g + 1) * 128], sel_ref[...])

            def put_y(g, parts, hot_y):
                yt_scr[g, 0:HEAD, :] = jnp.where(hot_y, parts[0], yt_scr[g, 0:HEAD, :])
                yt_scr[g, HEAD:, :] = jnp.where(hot_y, parts[1], yt_scr[g, HEAD:, :])

            def put_sa(g, parts, hot_t):
                sat_scr[g, 0:HEAD, :] = jnp.where(hot_t, parts[0], sat_scr[g, 0:HEAD, :])
                sat_scr[g, HEAD:, :] = jnp.where(hot_t, parts[1], sat_scr[g, HEAD:, :])

            npa, npb = _masked_rows(pb, negate=True)
            for j in range(8):
                t = t0 + j
                cols = slice(j * 128, (j + 1) * 128)
                sa_parts, products = [], []
                for g in range(G):
                    sl = slice(g * 128, (g + 1) * 128)
                    sa_parts.append(_lane_sums(s_scr[g], npa[j:j + 1, sl], npb[j:j + 1, sl]))
                hot_t = lane == t
                for g in range(G):
                    sl = slice(g * 128, (g + 1) * 128)
                    sa = jnp.where(m_a, sa_parts[g][0], sa_parts[g][1])
                    st = s_scr[g] * wb[j:j + 1, sl] + sa * bb[j:j + 1, sl] + vc_scr[g, :, cols] * kb[j:j + 1, sl]
                    s_scr[g] = st
                    st_ref[g, t] = st
                    put_sa(g, sa_parts[g], hot_t)
                    products.append(st * rb[j:j + 1, sl])
                for g, parts in enumerate(_head_sums_mxu(products, w2_ref[...])):
                    put_y(g, parts, hot_t)
            return carry

        lax.fori_loop(0, T // 8, block, 0)
        for g in range(G):
            y_ref[:, g * 128:(g + 1) * 128] = yt_scr[g].T[0:T, :]
            sa_ref[:, g * 128:(g + 1) * 128] = sat_scr[g].T[0:T, :]

    tspec = pl.BlockSpec((T, GW), lambda p, c: (c, p))
    sel_spec = pl.BlockSpec((SEL_ROWS, 8 * 128), lambda p, c: (0, 0))
    grid = (NP // G, S // T)
    x_in, x_out, x_shapes, x_scr, x_ops = _host_args(plan)
    w2_spec = pl.BlockSpec((256, 256), lambda p, c: (0, 0))
    outs = pl.pallas_call(
        _host(body, plan, 8, 3, *_grid_ends(grid)), name=name, grid=grid,
        in_specs=[tspec] * 6 + [sel_spec, w2_spec] + x_in,
        out_specs=tuple([tspec, tspec, pl.BlockSpec((G, T, HEAD, 128), lambda p, c: (p, c, 0, 0))] + x_out),
        out_shape=tuple([jax.ShapeDtypeStruct((S, R), F32), jax.ShapeDtypeStruct((S, R), F32),
                         jax.ShapeDtypeStruct((NP, S, HEAD, 128), F32)] + x_shapes),
        scratch_shapes=[pltpu.VMEM((G, HEAD, 128), F32), pltpu.VMEM((G, HEAD, 8 * 128), F32),
                        pltpu.VMEM((G, 128, 128), F32), pltpu.VMEM((G, 128, 128), F32)] + x_scr,
        compiler_params=_cp(_semantics(plan, ("parallel", "arbitrary")), 48),
    )(r, w, k, v, kap, b, _column_selector(), _head_sum_weights(), *x_ops)
    return outs[0], outs[1], outs[2], tuple(outs[3:])


def _scan_bwd(r, w, k, v, kap, b, sa, dy, states, *, name, plan=None):
    S, R = r.shape
    G, T = SCAN_G_BWD, SCAN_T_BWD
    NP = R // 128
    NC = S // T
    GW = 128 * G
    assert NP % G == 0

    def body(r_ref, w_ref, k_ref, v_ref, kap_ref, b_ref, sa_ref, dy_ref, st_ref, sp_ref, sel_ref, w2_ref, dr_ref,
             dw_ref, dk_ref, dv_ref, dkap_ref, db_ref, ds_scr, vc_scr, dyc_scr, sac_scr, dvt_scr):
        ci = pl.program_id(1)

        @pl.when(ci == 0)
        def _():
            ds_scr[...] = jnp.zeros_like(ds_scr)

        dvt_scr[...] = jnp.zeros_like(dvt_scr)
        lane = lax.broadcasted_iota(jnp.int32, (HEAD, 128), 1)
        m_a = lane < HEAD
        sub = lax.broadcasted_iota(jnp.int32, (8, 128), 0)
        zero_i = jnp.zeros((HEAD, 128), jnp.int32)
        has_prev = jnp.where(ci < NC - 1, 1.0, 0.0)

        def state_before(g, t):
            at_start = (zero_i + t) == 0
            return jnp.where(at_start, sp_ref[g, 0] * has_prev, st_ref[g, jnp.maximum(t - 1, 0)])

        def block(it, carry):
            tb = T // 8 - 1 - it
            t0 = pl.multiple_of(tb * 8, 8)
            rb, wb, kb = r_ref[pl.ds(t0, 8), :], w_ref[pl.ds(t0, 8), :], k_ref[pl.ds(t0, 8), :]
            pb, bb = kap_ref[pl.ds(t0, 8), :], b_ref[pl.ds(t0, 8), :]
            vb, dyb, sab = v_ref[pl.ds(t0, 8), :], dy_ref[pl.ds(t0, 8), :], sa_ref[pl.ds(t0, 8), :]
            for g in range(G):
                sl = slice(g * 128, (g + 1) * 128)
                vc_scr[g] = _expand_columns(vb[:, sl], sel_ref[...])
                dyc_scr[g] = _expand_columns(dyb[:, sl], sel_ref[...])
                sac_scr[g] = _expand_columns(sab[:, sl], sel_ref[...])
            outs = [[jnp.zeros((8, 128), F32) for _ in range(5)] for _ in range(G)]
            bba, bbb = _masked_rows(bb)
            def recurrence(j):
                cols = slice(j * 128, (j + 1) * 128)
                dsp, dsa_parts = [], []
                for g in range(G):
                    sl = slice(g * 128, (g + 1) * 128)
                    ds = ds_scr[g] + dyc_scr[g, :, cols] * rb[j:j + 1, sl]
                    dsp.append(ds)
                    dsa_parts.append(_lane_sums(ds, bba[j:j + 1, sl], bbb[j:j + 1, sl]))
                dsas = []
                for g in range(G):
                    sl = slice(g * 128, (g + 1) * 128)
                    dsa = jnp.where(m_a, dsa_parts[g][0], dsa_parts[g][1])
                    ds_scr[g] = dsp[g] * wb[j:j + 1, sl] - dsa * pb[j:j + 1, sl]
                    dsas.append(dsa)
                return dsp, dsas

            def side_work(j, dsp, dsas):
                t = t0 + j
                hot = lane == t
                cols = slice(j * 128, (j + 1) * 128)
                dvs = _head_sums_mxu([dsp[g] * kb[j:j + 1, g * 128:(g + 1) * 128] for g in range(G)], w2_ref[...])
                for g in range(G):
                    ds = dsp[g]
                    s_p = st_ref[g, t - 1] if j > 0 else state_before(g, t)
                    dr_row = jnp.sum(st_ref[g, t] * dyc_scr[g, :, cols], axis=0, keepdims=True)
                    dk_row = jnp.sum(ds * vc_scr[g, :, cols], axis=0, keepdims=True)
                    db_row = jnp.sum(ds * sac_scr[g, :, cols], axis=0, keepdims=True)
                    dw_row = jnp.sum(ds * s_p, axis=0, keepdims=True)
                    dkap_row = -jnp.sum(s_p * dsas[g], axis=0, keepdims=True)
                    dvt_scr[g, 0:HEAD, :] = jnp.where(hot, dvs[g][0], dvt_scr[g, 0:HEAD, :])
                    dvt_scr[g, HEAD:, :] = jnp.where(hot, dvs[g][1], dvt_scr[g, HEAD:, :])
                    pick = sub == j
                    for q, row in enumerate((dr_row, dw_row, dk_row, dkap_row, db_row)):
                        outs[g][q] = jnp.where(pick, row, outs[g][q])

            pending = None
            for j in range(7, -1, -1):
                done = recurrence(j)
                if pending is not None:
                    side_work(*pending)
                pending = (j, *done)
            side_work(*pending)
            for g in range(G):
                sl = slice(g * 128, (g + 1) * 128)
                for q, ref in enumerate((dr_ref, dw_ref, dk_ref, dkap_ref, db_ref)):
                    ref[pl.ds(t0, 8), sl] = outs[g][q]
            return carry

        lax.fori_loop(0, T // 8, block, 0)
        for g in range(G):
            dv_ref[:, g * 128:(g + 1) * 128] = dvt_scr[g].T[0:T, :]

    tspec = pl.BlockSpec((T, GW), lambda p, c: (NC - 1 - c, p))
    st_spec = pl.BlockSpec((G, T, HEAD, 128), lambda p, c: (p, NC - 1 - c, 0, 0))
    prev_spec = pl.BlockSpec((G, 1, HEAD, 128), lambda p, c: (p, jnp.maximum((NC - 1 - c) * T - 1, 0), 0, 0))
    sel_spec = pl.BlockSpec((SEL_ROWS, 8 * 128), lambda p, c: (0, 0))
    grid = (NP // G, NC)
    x_in, x_out, x_shapes, x_scr, x_ops = _host_args(plan)
    outs = pl.pallas_call(
        _host(body, plan, 12, 6, *_grid_ends(grid)), name=name, grid=grid,
        in_specs=[tspec] * 8 + [st_spec, prev_spec, sel_spec, pl.BlockSpec((256, 256), lambda p, c: (0, 0))] + x_in,
        out_specs=tuple([tspec] * 6 + x_out),
        out_shape=tuple([jax.ShapeDtypeStruct((S, R), F32)] * 6 + x_shapes),
        scratch_shapes=[pltpu.VMEM((G, HEAD, 128), F32), pltpu.VMEM((G, HEAD, 8 * 128), F32),
                        pltpu.VMEM((G, HEAD, 8 * 128), F32), pltpu.VMEM((G, HEAD, 8 * 128), F32),
                        pltpu.VMEM((G, 128, 128), F32)] + x_scr,
        compiler_params=_cp(_semantics(plan, ("parallel", "arbitrary")), 48),
    )(r, w, k, v, kap, b, sa, dy, states, states, _column_selector(), _head_sum_weights(), *x_ops)
    return tuple(outs[:6]), tuple(outs[6:])


def _sum_parts(parts, *, name):
    P, rows, W = parts.shape
    tr = rows
    for cand in (1024, 512, 256, 128, 64, 32, 16, 8):
        if rows % cand == 0:
            tr = cand
            break

    def body(p_ref, o_ref):
        acc = p_ref[0]
        for s in range(1, P):
            acc = acc + p_ref[s]
        o_ref[...] = acc

    return pl.pallas_call(
        body, name=name, grid=(rows // tr,),
        in_specs=[pl.BlockSpec((P, tr, W), lambda i: (0, i, 0))],
        out_specs=pl.BlockSpec((tr, W), lambda i: (i, 0)), out_shape=jax.ShapeDtypeStruct((rows, W), F32),
        compiler_params=_cp(("parallel",), 32),
    )(parts)


def _adamw(w, m, v, parts, *, name):
    R, C = w.shape
    P = parts.shape[0]
    tr = R
    for cand in (1024, 512, 256, 128, 64, 32, 16, 8):
        if R % cand == 0 and cand * C * 4 * (7 + P) <= 10 * 1024 * 1024:
            tr = cand
            break
    bc1 = 1.0 - ADAM_B1 ** ADAM_STEP
    bc2 = 1.0 - ADAM_B2 ** ADAM_STEP

    def body(w_ref, m_ref, v_ref, p_ref, g_ref, d_ref, nm_ref, nv_ref):
        g = p_ref[0].astype(F32)
        for s in range(1, P):
            g = g + p_ref[s].astype(F32)
        m1 = ADAM_B1 * m_ref[...] + (1.0 - ADAM_B1) * g
        v1 = ADAM_B2 * v_ref[...] + (1.0 - ADAM_B2) * (g * g)
        m_hat = m1 / bc1
        v_hat = v1 / bc2
        g_ref[...] = g
        d_ref[...] = -ADAM_LR * (m_hat / (jnp.sqrt(v_hat) + ADAM_EPS) + ADAM_WD * w_ref[...])
        nm_ref[...] = m1
        nv_ref[...] = v1

    spec = pl.BlockSpec((tr, C), lambda i: (i, 0))
    return pl.pallas_call(
        body, name=name, grid=(R // tr,),
        in_specs=[spec, spec, spec, pl.BlockSpec((P, tr, C), lambda i: (0, i, 0))],
        out_specs=(spec, spec, spec, spec), out_shape=tuple([jax.ShapeDtypeStruct((R, C), F32)] * 4),
        compiler_params=_cp(("parallel",), 40),
    )(w, m, v, parts)


def _cols_full(g8):
    n, rows, c = g8.shape
    return jnp.transpose(g8, (1, 0, 2)).reshape(rows, n * c)


def _cols_split(full):
    rows, cols = full.shape
    return jnp.transpose(full.reshape(rows, N_DEV, cols // N_DEV), (1, 0, 2))


def _pack(vals, rows_multiple=512):
    flat = jnp.concatenate([v.reshape(-1).astype(F32) for v in vals])
    n = flat.shape[0]
    unit = 128 * rows_multiple
    padded = ((n + unit - 1) // unit) * unit
    return jnp.pad(flat, (0, padded - n)).reshape(padded // 128, 128)


def _unpack(packed, shapes):
    flat = packed.reshape(-1)
    out, off = [], 0
    for shp in shapes:
        size = 1
        for d in shp:
            size *= d
        out.append(flat[off:off + size].reshape(shp))
        off += size
    return out


def _ffn_forward(x, weights, gpre, gpost, shift, scale1p, gw, tag, up_plan=None, down_from_plan=None):
    g8, u8, d8 = weights
    h = _pre_norm_mod(x, gpre, shift, scale1p, name=f"{tag}_pre")
    au, s, carried = _ffn_up(h, g8, u8, tm=1024, tk=2048, plan=up_plan,
                             name=f"{tag}_up" + ("_carry" if up_plan else ""))
    if down_from_plan is not None:
        d8 = carried[down_from_plan]
    f = _mm(s, d8.reshape(-1, d8.shape[2]), tm=1024, tn=1024, tk=2048, name=f"{tag}_down")
    xo = _post_norm_res(x, f, gpost, gw, name=f"{tag}_post")
    return xo, (h, au, s, f), d8, carried


def _ffn_backward(dxo, x, saved, weights, gpre, gpost, scale1p, gw, tag, plans=None, own_sums=None):
    g8, u8, d8 = weights
    h, au, s, f = saved
    plans = dict(plans or {})

    def nm(key):
        return f"{tag}_{key}" + ("_carry" if plans.get(key) is not None else "")

    df, post_sums = _post_norm_res_bwd(dxo, f, gpost, gw, MACARON, name=f"{tag}_post_bwd")
    dwd, got_dwd = _ffn_dwd(s, df, tn=1024, tk=1024, plan=plans.get("dwd"), name=nm("dwd"))
    if own_sums:
        plans["down_bwd"] = _ChipsPlan(own_sums([dwd], "d"))
    dau, got_down = _ffn_down_bwd(df, d8, au, tm=1024, tk=2048, plan=plans.get("down_bwd"), name=nm("down_bwd"))
    dwgu, got_dwgu = _ffn_dwgu(h, dau, tm=1024, tk=1024, plan=plans.get("dwgu"), name=nm("dwgu"))
    if own_sums:
        plans["dh"] = _ChipsPlan(own_sums([dwgu], "gu"))
    dh, got_dh = _ffn_dh(dau, g8, u8, tm=1024, tn=1024, plan=plans.get("dh"), name=nm("dh"))
    dx, pre_sums = _pre_norm_mod_bwd(dh, x, dxo, gpre, scale1p, name=f"{tag}_pre_bwd")
    carried = {"dwd": got_dwd, "down_bwd": got_down, "dwgu": got_dwgu, "dh": got_dh}
    return dx, dwgu, dwd, pre_sums, post_sums, carried


def kernel(x, c, w_ada, b_ada, norm_pre, norm_post, ffn1_w_gate, ffn1_w_up, ffn1_w_down, w_in, mu_shift, pool_w, pool_scale, w0, w2, a0, a2, g2, k_k, k_a, r_k, lnx_w, lnx_b, w_out, ffn2_w_gate, ffn2_w_up, ffn2_w_down, loss_target, m_w_ada, m_b_ada, m_norm_pre, m_norm_post, m_ffn1_w_gate, m_ffn1_w_up, m_ffn1_w_down, m_w_in, m_mu_shift, m_pool_w, m_pool_scale, m_w0, m_w2, m_a0, m_a2, m_g2, m_k_k, m_k_a, m_r_k, m_lnx_w, m_lnx_b, m_w_out, m_ffn2_w_gate, m_ffn2_w_up, m_ffn2_w_down, v_w_ada, v_b_ada, v_norm_pre, v_norm_post, v_ffn1_w_gate, v_ffn1_w_up, v_ffn1_w_down, v_w_in, v_mu_shift, v_pool_w, v_pool_scale, v_w0, v_w2, v_a0, v_a2, v_g2, v_k_k, v_k_a, v_r_k, v_lnx_w, v_lnx_b, v_w_out, v_ffn2_w_gate, v_ffn2_w_up, v_ffn2_w_down):
    names = ["w_ada", "b_ada", "norm_pre", "norm_post", "ffn1_w_gate", "ffn1_w_up", "ffn1_w_down", "w_in", "mu_shift",
             "pool_w", "pool_scale", "w0", "w2", "a0", "a2", "g2", "k_k", "k_a", "r_k", "lnx_w", "lnx_b", "w_out",
             "ffn2_w_gate", "ffn2_w_up", "ffn2_w_down"]
    env = dict(locals())
    W = {n: env[n][0] for n in names}
    M1 = {n: env["m_" + n][0] for n in names}
    V1 = {n: env["v_" + n][0] for n in names}

    me = _my_index()
    xs = x[0]
    tgt = loss_target[0]
    S, D = xs.shape
    F = W["ffn1_w_gate"].shape[1] * N_DEV
    R = W["w0"].shape[0]
    PW = D - R
    IN_W = W["w_in"].shape[1] * N_DEV
    P_W = F
    QW = P_W - PW
    NMOD = 9 * D
    ada_c = W["w_ada"].shape[1]

    c_all, npre8, npost8, w2_8, a2_8, g2_8 = _exchange(
        [c, W["norm_pre"], W["norm_post"], W["w2"].astype(BF16), W["a2"].astype(BF16), W["g2"].astype(BF16)],
        scatter=False, name="gather_small")
    c_all = c_all.reshape(N_DEV, D)
    gpre = _cols_full(npre8)
    gpost = _cols_full(npost8)
    wl = jnp.zeros((LORA_PAD, 3 * R), BF16)
    wl = wl.at[0:LORA_W, 0:R].set(_cols_full(w2_8))
    wl = wl.at[LORA_W:LORA_W + LORA_A, R:2 * R].set(_cols_full(a2_8))
    wl = wl.at[LORA_W + LORA_A:LORA_W + LORA_A + LORA_G, 2 * R:3 * R].set(_cols_full(g2_8))

    sc_all = jax.nn.silu(c_all)
    sc_pad = jnp.concatenate([sc_all, jnp.zeros((8, D), F32)], axis=0).astype(BF16)
    modcols = _mm(sc_pad, W["w_ada"], tm=16, tn=ada_c, tk=256, name="ada_fwd")[0:N_DEV]
    modcols = modcols + lax.dynamic_slice(W["b_ada"], (me * ada_c,), (ada_c,))[None, :]
    (mod8,) = _exchange([modcols], scatter=False, name="gather_mod")
    mod = lax.dynamic_index_in_dim(mod8, me, axis=1, keepdims=False).reshape(9, D)

    def mod_row(i):
        return mod[i:i + 1, :]

    f_pad = FF_TILE - F // N_DEV

    def ffn_shards(tag):
        return [jnp.pad(W[f"{tag}_w_gate"].astype(BF16), ((0, 0), (0, f_pad))),
                jnp.pad(W[f"{tag}_w_up"].astype(BF16), ((0, 0), (0, f_pad))),
                jnp.pad(W[f"{tag}_w_down"].astype(BF16), ((0, f_pad), (0, 0)))]

    ffn1_shards = ffn_shards("ffn1")
    g8_1, u8_1 = _gather_two_level(ffn1_shards[:2], name="gather_ffn_up")
    up_plan = _GatherPlan([W["w_in"].astype(BF16), ffn1_shards[2]])
    scan_plan = _GatherPlan(ffn_shards("ffn2") + [W["w_out"].astype(BF16)])

    mu_p = jnp.pad(W["mu_shift"], (0, QW - W["mu_shift"].shape[0]))[None, :]
    vec = lambda a: a.reshape(1, -1)
    w0r, a0r, kkr, kar = vec(W["w0"]), vec(W["a0"]), vec(W["k_k"]), vec(W["k_a"])
    lnw, lnb, rkr = vec(W["lnx_w"]), vec(W["lnx_b"]), vec(W["r_k"])
    pscale = vec(W["pool_scale"])

    sc1p = [1.0 + mod_row(3 * s + 1) for s in range(3)]
    shifts = [mod_row(3 * s) for s in range(3)]
    wgts = [MACARON, 1.0, MACARON]
    gws = [wgts[s] * (1.0 + mod_row(3 * s + 2)) for s in range(3)]
    gp = [gpre[s:s + 1] for s in range(3)]
    gq = [gpost[s:s + 1] for s in range(3)]

    x1, sv1, d8_1, (win8, _) = _ffn_forward(xs, (g8_1, u8_1, None), gp[0], gq[0], shifts[0], sc1p[0], gws[0], "ffn",
                                            up_plan=up_plan, down_from_plan=1)
    ffn1_w = (g8_1, u8_1, d8_1)
    w_in_p = jnp.pad(_cols_full(win8), ((0, 0), (0, P_W - IN_W)))

    h2 = _pre_norm_mod(x1, gp[1], shifts[1], sc1p[1], name="mix_pre")
    p = _mm(h2, w_in_p, tm=1024, tn=512, tk=2048, name="mix_in")
    q = p[:, PW:]
    o_pool, y_pool = _pool_fwd(p, W["pool_w"], pscale, name="pool_fwd")
    r_s, w_s, k_s, v_s, kap_s, b_s, g_s = _rwkv_prep(q, mu_p, wl, w0r, a0r, kkr, kar, name="rwkv_prep")
    y_scan, sa_s, states, gathered = _scan_fwd(r_s, w_s, k_s, v_s, kap_s, b_s, name="scan_fwd", plan=scan_plan)
    ffn2_w = gathered[:3]
    w_out_f = gathered[3].reshape(D, D)
    cat = _rwkv_post(y_scan, r_s, k_s, v_s, g_s, y_pool, lnw, lnb, rkr, name="rwkv_post")
    f2 = _mm(cat, w_out_f, tm=1024, tn=1024, tk=2048, name="mix_out")
    x2 = _post_norm_res(x1, f2, gq[1], gws[1], name="mix_post")

    x3, sv3, _, _ = _ffn_forward(x2, ffn2_w, gp[2], gq[2], shifts[2], sc1p[2], gws[2], "ffn")

    loss_part, dx3 = _loss_head(x3, tgt, name="loss_head")
    loss = lax.psum(loss_part[0, 0], MESH_AXES)

    def by_core_chip(blocks):
        shp = blocks.shape
        t = blocks.astype(BF16).reshape((N_DEV // 2, 2) + shp[1:])
        return jnp.swapaxes(t, 0, 1)

    def chip_sums(mine, tag):
        got = _sibling_swap(mine, name=f"{tag}_swap")
        return [_pair_add(m, g, name=f"{tag}_add{i}") for i, (m, g) in enumerate(zip(mine, got))]

    def ffn_parts(pgu, pd):
        fs = F // N_DEV
        return pgu[:, :D, :fs], pgu[:, D:, :fs], pd[:, :fs, :]

    dx2, dwgu2, dwd2, pre3, post3, _ = _ffn_backward(dx3, x2, sv3, ffn2_w, gp[2], gq[2], sc1p[2], gws[2], "ffn")
    sums2_gu, sums2_d = chip_sums([dwgu2, dwd2], "scatter_ffn")

    df2, post2 = _post_norm_res_bwd(dx2, f2, gq[1], gws[1], 1.0, name="mix_post_bwd")
    dw_out = _mm(cat, df2, ta=True, tm=1024, tn=1024, tk=1024, name="mix_dwout")
    dcat = _mm(df2, w_out_f, tb=True, tm=1024, tn=1024, tk=2048, name="mix_dcat")
    dyr = dcat[:, PW:]
    dysc, dg, dr_b, dk2_b, dv_b, post_sums = _rwkv_post_bwd(dyr, y_scan, r_s, k_s, v_s, g_s, lnw, lnb, rkr,
                                                             name="rwkv_post_bwd")
    (dr, dw, dk2, dv, dkap, db), parts2 = _scan_bwd(r_s, w_s, k_s, v_s, kap_s, b_s, sa_s, dysc, states,
                                                    name="scan_bwd", plan=_ChipsPlan([sums2_gu, sums2_d]))
    dps, dwl, prep_sums = _rwkv_prep_bwd(q, mu_p, wl, w0r, a0r, kkr, kar,
                                         (dr, dw, dk2, dv, dkap, db, dg, dr_b, dk2_b, dv_b), name="rwkv_prep_bwd")
    dq, dmu = _tshift_bwd(dps, q, mu_p, name="tshift_bwd")
    du_pool, dpool_w, dpool_scale = _pool_bwd(dcat, o_pool, W["pool_w"], pscale, name="pool_bwd")
    dp = jnp.concatenate([du_pool, dq], axis=1)
    dw_in = _mm(h2, dp, ta=True, tm=1024, tn=512, tk=1024, name="mix_dwin")
    dh2 = _mm(dp, w_in_p, tb=True, tm=1024, tn=1024, tk=2816, name="mix_dh")
    dx1, pre2 = _pre_norm_mod_bwd(dh2, x1, dx2, gp[1], sc1p[1], name="mix_pre_bwd")

    sums_mix = chip_sums([by_core_chip(_cols_split(dw_in[:, :IN_W])),
                          by_core_chip(dw_out.reshape(N_DEV, D // N_DEV, D))], "scatter_mixer")
    early = [dmu[0, :W["mu_shift"].shape[0]], dpool_w, dpool_scale, prep_sums[0], prep_sums[1], prep_sums[2],
             prep_sums[3], post_sums[2], post_sums[0], post_sums[1],
             dwl[0:LORA_W, 0:R], dwl[LORA_W:LORA_W + LORA_A, R:2 * R],
             dwl[LORA_W + LORA_A:LORA_W + LORA_A + LORA_G, 2 * R:3 * R]]
    early_shapes = [a.shape for a in early]
    dx0, _, _, pre1, post1, got = _ffn_backward(
        dx1, xs, sv1, ffn1_w, gp[0], gq[0], sc1p[0], gws[0], "ffn",
        plans={"dwd": _ChipsPlan(sums_mix), "dwgu": _GatherPlan([_pack(early)])},
        own_sums=lambda blocks, part: chip_sums(blocks, f"scatter_ffn_{part}"))

    pres, posts = [pre1, pre2, pre3], [post1, post2, post3]
    dmod = jnp.stack([jnp.stack([pres[s][0], pres[s][1], posts[s][0]]) for s in range(3)]).reshape(NMOD // 128, 128)
    late = [jnp.stack([pres[s][2] for s in range(3)]), jnp.stack([posts[s][1] for s in range(3)])]
    late_shapes = [a.shape for a in late]
    dmod8, late8 = _gather_two_level([dmod, _pack(late, rows_multiple=8)], name="gather_grads")
    g_b_ada = _sum_parts(dmod8, name="sum_dmod").reshape(NMOD)
    g_npre, g_npost = _unpack(_sum_parts(late8, name="sum_late"), late_shapes)
    (g_mu, g_pool_w, g_pool_scale, g_w0, g_a0, g_kk, g_ka, g_rk, g_lnw, g_lnb, g_w2, g_a2,
     g_g2) = _unpack(_sum_parts(got["dwgu"][0], name="sum_small"), early_shapes)

    dmod_all = dmod8.reshape(N_DEV, NMOD)
    dmod_cols = lax.dynamic_slice(dmod_all, (0, me * ada_c), (N_DEV, ada_c))
    dmod_cols = jnp.concatenate([dmod_cols, jnp.zeros_like(dmod_cols)], axis=0)
    g_w_ada = _mm(sc_pad, dmod_cols, ta=True, tm=D, tn=ada_c // 9, tk=16, name="ada_bwd")

    pg2, pu2, pd2 = ffn_parts(*parts2)
    pin, pout = got["dwd"]
    pg1, pu1, pd1 = ffn_parts(got["dh"][0], got["down_bwd"][0])

    res = {}

    def big(nm, parts, tag):
        res[nm] = _adamw(W[nm], M1[nm], V1[nm], parts, name=tag)

    big("ffn1_w_gate", pg1, "adamw_cols")
    big("ffn1_w_up", pu1, "adamw_cols")
    big("ffn1_w_down", pd1, "adamw_rows")
    big("ffn2_w_gate", pg2, "adamw_cols")
    big("ffn2_w_up", pu2, "adamw_cols")
    big("ffn2_w_down", pd2, "adamw_rows")
    big("w_in", pin, "adamw_w_in")
    big("w_out", pout, "adamw_w_out")
    big("w_ada", g_w_ada[None], "adamw_w_ada")

    def my_cols(full, width):
        return lax.dynamic_slice_in_dim(full, me * width, width, axis=full.ndim - 1)

    small_names = ["b_ada", "mu_shift", "pool_w", "pool_scale", "w0", "a0", "k_k", "k_a", "r_k", "lnx_w", "lnx_b",
                   "norm_pre", "norm_post", "w2", "a2", "g2"]
    small_grads = [g_b_ada, g_mu, g_pool_w, g_pool_scale, g_w0, g_a0, g_kk, g_ka, g_rk.reshape(W["r_k"].shape), g_lnw,
                   g_lnb, my_cols(g_npre, D // N_DEV), my_cols(g_npost, D // N_DEV), my_cols(g_w2, R // N_DEV),
                   my_cols(g_a2, R // N_DEV), my_cols(g_g2, R // N_DEV)]
    shapes = [W[n].shape for n in small_names]
    packed = _adamw(_pack([W[n] for n in small_names]), _pack([M1[n] for n in small_names]),
                    _pack([V1[n] for n in small_names]), _pack(small_grads)[None], name="adamw_small")
    unpacked = [_unpack(t, shapes) for t in packed]
    for i, nm in enumerate(small_names):
        res[nm] = tuple(unpacked[k][i] for k in range(4))

    outs = [loss, dx0[None]]
    for k in range(4):
        outs.extend(res[nm][k][None] for nm in names)
    return tuple(outs)
```

```python
import functools

import jax
import jax.numpy as jnp
from jax import lax
from jax.experimental import pallas as pl
from jax.experimental.pallas import tpu as pltpu

F32 = jnp.float32
BF16 = jnp.bfloat16
N_DEV = 8
MESH_AXES = ("x", "y", "c")

NORM_EPS = 1e-6
HEAD = 64
LN_X_EPS = 1e-5 * HEAD
POOL_GROUPS = 4
POOL_GROUP = 128
MACARON = 0.5
LORA_W, LORA_A, LORA_G = 64, 64, 224
LORA_PAD = 384
ADAM_LR, ADAM_B1, ADAM_B2, ADAM_EPS, ADAM_WD, ADAM_STEP = 0.001, 0.9, 0.999, 1e-08, 0.01, 10

FF_TILE = 768
ROW_TILE = 256
SCAN_T = 64
SCAN_T_BWD = 64
SCAN_G = 6
SCAN_G_BWD = 6
VMEM_CAP = 56 * 1024 * 1024


def _cp(sem, vmem_mb):
    return pltpu.CompilerParams(dimension_semantics=sem, vmem_limit_bytes=min(vmem_mb * 1024 * 1024, VMEM_CAP))


def _my_index():
    return 4 * lax.axis_index("x") + 2 * lax.axis_index("y") + lax.axis_index("c")


def _exchange(arrays, *, scatter, name):
    n = len(arrays)
    out_shapes = []
    for a in arrays:
        shp = a.shape if scatter else (N_DEV,) + a.shape
        out_shapes.append(jax.ShapeDtypeStruct(shp, a.dtype))

    def body(*refs):
        ins, outs = refs[:n], refs[n:2 * n]
        send_sems, recv_sems, local_sems = refs[2 * n:]
        me = _my_index()

        def dev(p):
            return (p // 4, (p // 2) % 2, p % 2)

        def copy(i, d):
            peer = (me + d) % N_DEV
            src = ins[i].at[peer] if scatter else ins[i]
            return pltpu.make_async_remote_copy(
                src_ref=src, dst_ref=outs[i].at[me], send_sem=send_sems.at[i, d - 1],
                recv_sem=recv_sems.at[i, d - 1], device_id=dev(peer), device_id_type=pl.DeviceIdType.MESH)

        def arrival(i, d):
            frm = (me + N_DEV - d) % N_DEV
            src = ins[i].at[frm] if scatter else ins[i]
            return pltpu.make_async_remote_copy(
                src_ref=src, dst_ref=outs[i].at[frm], send_sem=send_sems.at[i, d - 1],
                recv_sem=recv_sems.at[i, d - 1], device_id=dev(frm), device_id_type=pl.DeviceIdType.MESH)

        locals_ = []
        for i in range(n):
            src = ins[i].at[me] if scatter else ins[i]
            lc = pltpu.make_async_copy(src, outs[i].at[me], local_sems.at[i])
            lc.start()
            locals_.append(lc)
        sends = [copy(i, d) for d in range(1, N_DEV) for i in range(n)]
        for cp in sends:
            cp.start()
        for d in range(1, N_DEV):
            for i in range(n):
                arrival(i, d).wait_recv()
        for cp in sends:
            cp.wait_send()
        for lc in locals_:
            lc.wait()

    hbm = pl.BlockSpec(memory_space=pltpu.HBM)
    return pl.pallas_call(
        body, name=name, out_shape=tuple(out_shapes), in_specs=[hbm] * n, out_specs=tuple([hbm] * n),
        scratch_shapes=[pltpu.SemaphoreType.DMA((n, N_DEV - 1)), pltpu.SemaphoreType.DMA((n, N_DEV - 1)),
                        pltpu.SemaphoreType.DMA((n,))],
    )(*arrays)


def _remote(src, dst, send_sem, recv_sem, to):
    return pltpu.make_async_remote_copy(src_ref=src, dst_ref=dst, send_sem=send_sem, recv_sem=recv_sem,
                                        device_id=to, device_id_type=pl.DeviceIdType.MESH)


class _GatherPlan:
    def __init__(self, arrays):
        self.arrays = list(arrays)
        self.n = len(arrays)
        self.out_shapes = [jax.ShapeDtypeStruct((N_DEV,) + a.shape, a.dtype) for a in arrays]
        self.scratch = [pltpu.SemaphoreType.DMA((self.n, 7)), pltpu.SemaphoreType.DMA((self.n, 7)),
                        pltpu.SemaphoreType.DMA((self.n,))]

    def _parts(self, ins, outs, sems):
        send_sems, recv_sems, local_sems = sems
        x, y, c = lax.axis_index("x"), lax.axis_index("y"), lax.axis_index("c")
        chips = [(1 - x, y), (x, 1 - y), (1 - x, 1 - y)]

        def slot(i, px, py, pc):
            return outs[i].at[4 * px + 2 * py + pc]

        def copy(i, k, block, to, src=None):
            dst = slot(i, *block)
            return _remote(dst if src is None else src, dst, send_sems.at[i, k], recv_sems.at[i, k], to)

        n = self.n
        locals_ = [pltpu.make_async_copy(ins[i], slot(i, x, y, c), local_sems.at[i]) for i in range(n)]
        first = [copy(i, 1 + j, (x, y, c), (*chip, c), src=ins[i]) for j, chip in enumerate(chips) for i in range(n)]
        first += [copy(i, 0, (x, y, c), (x, y, 1 - c), src=ins[i]) for i in range(n)]
        return (x, y, c), chips, copy, locals_, first

    def start(self, ins, outs, sems):
        _, _, _, locals_, first = self._parts(ins, outs, sems)
        for lc in locals_:
            lc.start()
        for cp in first:
            cp.start()

    def finish(self, ins, outs, sems):
        (x, y, c), chips, copy, locals_, first = self._parts(ins, outs, sems)
        forwards = []
        for j, chip in enumerate(chips):
            for i in range(self.n):
                copy(i, 1 + j, (*chip, c), (x, y, c)).wait_recv()
                fwd = copy(i, 4 + j, (*chip, c), (x, y, 1 - c))
                fwd.start()
                forwards.append(fwd)
        for i in range(self.n):
            copy(i, 0, (x, y, 1 - c), (x, y, c)).wait_recv()
        for j, chip in enumerate(chips):
            for i in range(self.n):
                copy(i, 4 + j, (*chip, 1 - c), (x, y, c)).wait_recv()
        for cp in first + forwards:
            cp.wait_send()
        for lc in locals_:
            lc.wait()


class _ChipsPlan:
    def __init__(self, arrays):
        self.arrays = list(arrays)
        self.n = len(arrays)
        self.out_shapes = [jax.ShapeDtypeStruct(a.shape, a.dtype) for a in arrays]
        self.scratch = [pltpu.SemaphoreType.DMA((self.n, 3)), pltpu.SemaphoreType.DMA((self.n, 3)),
                        pltpu.SemaphoreType.DMA((self.n,))]

    def _parts(self, ins, outs, sems):
        send_sems, recv_sems, local_sems = sems
        x, y, c = lax.axis_index("x"), lax.axis_index("y"), lax.axis_index("c")
        mine = 2 * x + y
        chips = [(1 - x, y), (x, 1 - y), (1 - x, 1 - y)]
        n = self.n
        locals_ = [pltpu.make_async_copy(ins[i].at[mine], outs[i].at[mine], local_sems.at[i]) for i in range(n)]
        sends = [_remote(ins[i].at[2 * chip[0] + chip[1]], outs[i].at[mine], send_sems.at[i, j], recv_sems.at[i, j],
                         (*chip, c)) for j, chip in enumerate(chips) for i in range(n)]

        def arrivals():
            return [_remote(ins[i].at[2 * chip[0] + chip[1]], outs[i].at[2 * chip[0] + chip[1]], send_sems.at[i, j],
                            recv_sems.at[i, j], (*chip, c)) for j, chip in enumerate(chips) for i in range(n)]

        return locals_, sends, arrivals

    def start(self, ins, outs, sems):
        locals_, sends, _ = self._parts(ins, outs, sems)
        for lc in locals_:
            lc.start()
        for cp in sends:
            cp.start()

    def finish(self, ins, outs, sems):
        locals_, sends, arrivals = self._parts(ins, outs, sems)
        for cp in arrivals():
            cp.wait_recv()
        for cp in sends:
            cp.wait_send()
        for lc in locals_:
            lc.wait()


def _run_plan(plan, *, name):
    n = plan.n

    def body(*refs):
        ins, outs, sems = refs[:n], refs[n:2 * n], refs[2 * n:]
        plan.start(ins, outs, sems)
        plan.finish(ins, outs, sems)

    hbm = pl.BlockSpec(memory_space=pltpu.HBM)
    return pl.pallas_call(
        body, name=name, out_shape=tuple(plan.out_shapes), in_specs=[hbm] * n, out_specs=tuple([hbm] * n),
        scratch_shapes=plan.scratch,
    )(*plan.arrays)


def _host(body, plan, n_in, n_out, is_first, is_last):
    if plan is None:
        return body
    m = plan.n

    def wrapped(*refs):
        a, b = n_in, n_in + m
        c, d = b + n_out, b + n_out + m
        own_in, c_in, own_out, c_out, rest = refs[:a], refs[a:b], refs[b:c], refs[c:d], refs[d:]
        n_sems = len(plan.scratch)
        own_scr, c_sems = rest[:len(rest) - n_sems], rest[len(rest) - n_sems:]

        @pl.when(is_first())
        def _():
            plan.start(c_in, c_out, c_sems)

        body(*own_in, *own_out, *own_scr)

        @pl.when(is_last())
        def _():
            plan.finish(c_in, c_out, c_sems)

    return wrapped


def _host_args(plan):
    if plan is None:
        return [], [], [], [], []
    hbm = pl.BlockSpec(memory_space=pltpu.HBM)
    return [hbm] * plan.n, [hbm] * plan.n, list(plan.out_shapes), list(plan.scratch), list(plan.arrays)


def _gather_two_level(arrays, *, name):
    return _run_plan(_GatherPlan(arrays), name=name)


class _SwapPlan:
    def __init__(self, arrays):
        self.arrays = list(arrays)
        self.n = len(arrays)
        self.out_shapes = [jax.ShapeDtypeStruct(a.shape[1:], a.dtype) for a in arrays]
        self.scratch = [pltpu.SemaphoreType.DMA((self.n,)), pltpu.SemaphoreType.DMA((self.n,))]

    def _copies(self, ins, outs, sems):
        send_sems, recv_sems = sems
        x, y, c = lax.axis_index("x"), lax.axis_index("y"), lax.axis_index("c")
        return [_remote(ins[i].at[1 - c], outs[i], send_sems.at[i], recv_sems.at[i], (x, y, 1 - c))
                for i in range(self.n)]

    def start(self, ins, outs, sems):
        for cp in self._copies(ins, outs, sems):
            cp.start()

    def finish(self, ins, outs, sems):
        copies = self._copies(ins, outs, sems)
        for cp in copies:
            cp.wait_recv()
        for cp in copies:
            cp.wait_send()


def _sibling_swap(arrays, *, name):
    return _run_plan(_SwapPlan(arrays), name=name)


def _chips_all_to_all(arrays, *, name):
    return _run_plan(_ChipsPlan(arrays), name=name)


def _pair_add(mine, got, *, name):
    _, nq, R, C = mine.shape
    tr = R
    for cand in (512, 256, 128, 64, 32, 16):
        if R % cand == 0 and cand * C * 2 * 3 * 2 <= 12 * 1024 * 1024:
            tr = cand
            break

    def body(core_ref, m_ref, g_ref, o_ref):
        o_ref[0] = (m_ref[0, 0].astype(F32) + g_ref[0].astype(F32)).astype(BF16)

    core = lax.axis_index("c").astype(jnp.int32).reshape(1)
    return pl.pallas_call(
        body, name=name,
        grid_spec=pltpu.PrefetchScalarGridSpec(
            num_scalar_prefetch=1, grid=(nq, R // tr),
            in_specs=[pl.BlockSpec((1, 1, tr, C), lambda q, i, core_ref: (core_ref[0], q, i, 0)),
                      pl.BlockSpec((1, tr, C), lambda q, i, core_ref: (q, i, 0))],
            out_specs=pl.BlockSpec((1, tr, C), lambda q, i, core_ref: (q, i, 0))),
        out_shape=jax.ShapeDtypeStruct((nq, R, C), BF16),
        compiler_params=_cp(("parallel", "parallel"), 40),
    )(core, mine, got)


def _mm(a, b, *, ta=False, tb=False, tm, tn, tk, out_dtype=F32, name):
    M = a.shape[1] if ta else a.shape[0]
    K = a.shape[0] if ta else a.shape[1]
    N = b.shape[0] if tb else b.shape[1]
    tm, tn, tk = min(tm, M), min(tn, N), min(tk, K)
    assert M % tm == 0 and N % tn == 0 and K % tk == 0, (name, M, N, K, tm, tn, tk)
    nk = K // tk
    dims = (((0 if ta else 1,), (1 if tb else 0,)), ((), ()))

    def body(a_ref, b_ref, o_ref, acc_ref):
        k = pl.program_id(2)

        @pl.when(k == 0)
        def _():
            acc_ref[...] = jnp.zeros_like(acc_ref)

        acc_ref[...] += lax.dot_general(a_ref[...].astype(BF16), b_ref[...].astype(BF16), dims,
                                        preferred_element_type=F32)

        @pl.when(k == nk - 1)
        def _():
            o_ref[...] = acc_ref[...].astype(out_dtype)

    a_spec = pl.BlockSpec((tk, tm), lambda i, j, k: (k, i)) if ta else pl.BlockSpec((tm, tk), lambda i, j, k: (i, k))
    b_spec = pl.BlockSpec((tn, tk), lambda i, j, k: (j, k)) if tb else pl.BlockSpec((tk, tn), lambda i, j, k: (k, j))
    blk = 2 * (tm * tk * a.dtype.itemsize + tk * tn * b.dtype.itemsize + tm * tn * jnp.dtype(out_dtype).itemsize)
    return pl.pallas_call(
        body, name=name, grid=(M // tm, N // tn, nk), in_specs=[a_spec, b_spec],
        out_specs=pl.BlockSpec((tm, tn), lambda i, j, k: (i, j)),
        out_shape=jax.ShapeDtypeStruct((M, N), out_dtype),
        scratch_shapes=[pltpu.VMEM((tm, tn), F32)],
        compiler_params=_cp(("parallel", "parallel", "arbitrary"), (blk + tm * tn * 4) // (1024 * 1024) + 12),
    )(a, b)


def _grid_ends(grid):
    def is_first():
        ok = pl.program_id(0) == 0
        for ax in range(1, len(grid)):
            ok = ok & (pl.program_id(ax) == 0)
        return ok

    def is_last():
        ok = pl.program_id(0) == grid[0] - 1
        for ax in range(1, len(grid)):
            ok = ok & (pl.program_id(ax) == grid[ax] - 1)
        return ok

    return is_first, is_last


def _semantics(plan, sem):
    return sem if plan is None else tuple("arbitrary" for _ in sem)


def _ffn_up(h, g8, u8, *, tm, tk, name, plan=None):
    S, D = h.shape
    nb, _, tn = g8.shape
    tm = min(tm, S)
    tk = min(tk, D)
    nk = D // tk

    def body(h_ref, g_ref, u_ref, au_ref, s_ref, acc_ref):
        k = pl.program_id(2)

        @pl.when(k == 0)
        def _():
            acc_ref[...] = jnp.zeros_like(acc_ref)

        hv = h_ref[...]
        acc_ref[:, :tn] += jnp.dot(hv, g_ref[0], preferred_element_type=F32)
        acc_ref[:, tn:] += jnp.dot(hv, u_ref[0], preferred_element_type=F32)

        @pl.when(k == nk - 1)
        def _():
            acc = acc_ref[...]
            a = acc[:, :tn]
            u = acc[:, tn:]
            au_ref[...] = acc.astype(BF16)
            s_ref[...] = (a * jax.nn.sigmoid(a) * u).astype(BF16)

    wspec = pl.BlockSpec((1, tk, tn), lambda i, j, k: (j, k, 0))
    grid = (S // tm, nb, nk)
    x_in, x_out, x_shapes, x_scr, x_ops = _host_args(plan)
    outs = pl.pallas_call(
        _host(body, plan, 3, 2, *_grid_ends(grid)), name=name, grid=grid,
        in_specs=[pl.BlockSpec((tm, tk), lambda i, j, k: (i, k)), wspec, wspec] + x_in,
        out_specs=tuple([pl.BlockSpec((tm, 2 * tn), lambda i, j, k: (i, j)),
                         pl.BlockSpec((tm, tn), lambda i, j, k: (i, j))] + x_out),
        out_shape=tuple([jax.ShapeDtypeStruct((S, 2 * nb * tn), BF16), jax.ShapeDtypeStruct((S, nb * tn), BF16)]
                        + x_shapes),
        scratch_shapes=[pltpu.VMEM((tm, 2 * tn), F32)] + x_scr,
        compiler_params=_cp(_semantics(plan, ("parallel", "parallel", "arbitrary")), 52),
    )(h, g8, u8, *x_ops)
    return outs[0], outs[1], tuple(outs[2:])


def _ffn_dh(dau, g8, u8, *, tm, tn, name, plan=None):
    S = dau.shape[0]
    nb, D, tf = g8.shape
    tm, tn = min(tm, S), min(tn, D)
    nk = 2 * nb
    nt = (((1,), (1,)), ((), ()))

    def body(a_ref, g_ref, u_ref, o_ref, acc_ref):
        k = pl.program_id(2)

        @pl.when(k == 0)
        def _():
            acc_ref[...] = jnp.zeros_like(acc_ref)

        @pl.when(k % 2 == 0)
        def _():
            acc_ref[...] += lax.dot_general(a_ref[...], g_ref[0], nt, preferred_element_type=F32)

        @pl.when(k % 2 == 1)
        def _():
            acc_ref[...] += lax.dot_general(a_ref[...], u_ref[0], nt, preferred_element_type=F32)

        @pl.when(k == nk - 1)
        def _():
            o_ref[...] = acc_ref[...]

    wspec = pl.BlockSpec((1, tn, tf), lambda i, n, k: (k // 2, n, 0))
    grid = (S // tm, D // tn, nk)
    x_in, x_out, x_shapes, x_scr, x_ops = _host_args(plan)
    outs = pl.pallas_call(
        _host(body, plan, 3, 1, *_grid_ends(grid)), name=name, grid=grid,
        in_specs=[pl.BlockSpec((tm, tf), lambda i, n, k: (i, k)), wspec, wspec] + x_in,
        out_specs=tuple([pl.BlockSpec((tm, tn), lambda i, n, k: (i, n))] + x_out),
        out_shape=tuple([jax.ShapeDtypeStruct((S, D), F32)] + x_shapes),
        scratch_shapes=[pltpu.VMEM((tm, tn), F32)] + x_scr,
        compiler_params=_cp(_semantics(plan, ("parallel", "parallel", "arbitrary")), 40),
    )(dau, g8, u8, *x_ops)
    return outs[0], tuple(outs[1:])


def _ffn_dwgu(h, dau, *, tm, tk, name, plan=None):
    S, D = h.shape
    tf = FF_TILE
    nt = dau.shape[1] // tf
    tm, tk = min(tm, D), min(tk, S)
    nk = S // tk
    ni = D // tm

    def body(a_ref, b_ref, o_ref, acc_ref):
        k = pl.program_id(2)

        @pl.when(k == 0)
        def _():
            acc_ref[...] = jnp.zeros_like(acc_ref)

        acc_ref[...] += lax.dot_general(a_ref[...], b_ref[...], (((0,), (0,)), ((), ())), preferred_element_type=F32)

        @pl.when(k == nk - 1)
        def _():
            o_ref[0, 0] = acc_ref[...].astype(BF16)

    grid = (ni, nt, nk)
    x_in, x_out, x_shapes, x_scr, x_ops = _host_args(plan)
    outs = pl.pallas_call(
        _host(body, plan, 2, 1, *_grid_ends(grid)), name=name, grid=grid,
        in_specs=[pl.BlockSpec((tk, tm), lambda i, j, k: (k, i)), pl.BlockSpec((tk, tf), lambda i, j, k: (k, j))] + x_in,
        out_specs=tuple([pl.BlockSpec((1, 1, tm, tf), lambda i, j, k: ((j // 2) % 2, j // 4, (j % 2) * ni + i, 0))]
                        + x_out),
        out_shape=tuple([jax.ShapeDtypeStruct((2, nt // 4, 2 * D, tf), BF16)] + x_shapes),
        scratch_shapes=[pltpu.VMEM((tm, tf), F32)] + x_scr,
        compiler_params=_cp(_semantics(plan, ("parallel", "parallel", "arbitrary")), 40),
    )(h, dau, *x_ops)
    return outs[0], tuple(outs[1:])


def _ffn_dwd(s, df, *, tn, tk, name, plan=None):
    S, D = df.shape
    tf = FF_TILE
    nb = s.shape[1] // tf
    tn, tk = min(tn, D), min(tk, S)
    nk = S // tk

    def body(a_ref, b_ref, o_ref, acc_ref):
        k = pl.program_id(2)

        @pl.when(k == 0)
        def _():
            acc_ref[...] = jnp.zeros_like(acc_ref)

        acc_ref[...] += lax.dot_general(a_ref[...], b_ref[...], (((0,), (0,)), ((), ())), preferred_element_type=F32)

        @pl.when(k == nk - 1)
        def _():
            o_ref[0, 0] = acc_ref[...].astype(BF16)

    grid = (nb, D // tn, nk)
    x_in, x_out, x_shapes, x_scr, x_ops = _host_args(plan)
    outs = pl.pallas_call(
        _host(body, plan, 2, 1, *_grid_ends(grid)), name=name, grid=grid,
        in_specs=[pl.BlockSpec((tk, tf), lambda j, n, k: (k, j)), pl.BlockSpec((tk, tn), lambda j, n, k: (k, n))] + x_in,
        out_specs=tuple([pl.BlockSpec((1, 1, tf, tn), lambda j, n, k: (j % 2, j // 2, 0, n))] + x_out),
        out_shape=tuple([jax.ShapeDtypeStruct((2, nb // 2, tf, D), BF16)] + x_shapes),
        scratch_shapes=[pltpu.VMEM((tf, tn), F32)] + x_scr,
        compiler_params=_cp(_semantics(plan, ("parallel", "parallel", "arbitrary")), 40),
    )(s, df, *x_ops)
    return outs[0], tuple(outs[1:])


def _ffn_down_bwd(df, d8, au, *, tm, tk, name, plan=None):
    S, D = df.shape
    nb, tn, _ = d8.shape
    F = nb * tn
    tm = min(tm, S)
    tk = min(tk, D)
    nk = D // tk

    def body(df_ref, w_ref, au_ref, dau_ref, acc_ref):
        k = pl.program_id(2)

        @pl.when(k == 0)
        def _():
            acc_ref[...] = jnp.zeros_like(acc_ref)

        acc_ref[...] += lax.dot_general(df_ref[...], w_ref[0], (((1,), (1,)), ((), ())), preferred_element_type=F32)

        @pl.when(k == nk - 1)
        def _():
            ds = acc_ref[...]
            au_v = au_ref[...].astype(F32)
            a = au_v[:, :tn]
            u = au_v[:, tn:]
            sg = jax.nn.sigmoid(a)
            da = ds * u * (sg * (1.0 + a * (1.0 - sg)))
            du = ds * (a * sg)
            dau_ref[:, :tn] = da.astype(BF16)
            dau_ref[:, tn:] = du.astype(BF16)

    grid = (S // tm, F // tn, nk)
    x_in, x_out, x_shapes, x_scr, x_ops = _host_args(plan)
    outs = pl.pallas_call(
        _host(body, plan, 3, 1, *_grid_ends(grid)), name=name, grid=grid,
        in_specs=[pl.BlockSpec((tm, tk), lambda i, j, k: (i, k)), pl.BlockSpec((1, tn, tk), lambda i, j, k: (j, 0, k)),
                  pl.BlockSpec((tm, 2 * tn), lambda i, j, k: (i, j))] + x_in,
        out_specs=tuple([pl.BlockSpec((tm, 2 * tn), lambda i, j, k: (i, j))] + x_out),
        out_shape=tuple([jax.ShapeDtypeStruct((S, 2 * F), BF16)] + x_shapes),
        scratch_shapes=[pltpu.VMEM((tm, tn), F32)] + x_scr,
        compiler_params=_cp(_semantics(plan, ("parallel", "parallel", "arbitrary")), 52),
    )(df, d8, au, *x_ops)
    return outs[0], tuple(outs[1:])


def _fold8(x):
    tm, w = x.shape
    return jnp.sum(x.reshape(tm // 8, 8, w), axis=0)


def _row_spec(tm, w):
    return pl.BlockSpec((tm, w), lambda i: (i, 0))


def _vec_spec(rows, w):
    return pl.BlockSpec((rows, w), lambda i: (0, 0))


def _pre_norm_mod(x, gain, shift, scale1p, *, name):
    S, D = x.shape
    tm = ROW_TILE

    def body(x_ref, g_ref, sh_ref, sc_ref, h_ref):
        xv = x_ref[...]
        rinv = lax.rsqrt(jnp.mean(xv * xv, axis=-1, keepdims=True) + NORM_EPS)
        h_ref[...] = ((xv * rinv) * g_ref[...] * sc_ref[...] + sh_ref[...]).astype(BF16)

    return pl.pallas_call(
        body, name=name, grid=(S // tm,),
        in_specs=[_row_spec(tm, D), _vec_spec(1, D), _vec_spec(1, D), _vec_spec(1, D)],
        out_specs=_row_spec(tm, D), out_shape=jax.ShapeDtypeStruct((S, D), BF16),
        compiler_params=_cp(("parallel",), 32),
    )(x, gain, shift, scale1p)


def _pre_norm_mod_bwd(dh, x, dres, gain, scale1p, *, name):
    S, D = x.shape
    tm = ROW_TILE
    n = S // tm

    def body(dh_ref, x_ref, dr_ref, g_ref, sc_ref, dx_ref, sums_ref, acc_ref):
        i = pl.program_id(0)

        @pl.when(i == 0)
        def _():
            acc_ref[...] = jnp.zeros_like(acc_ref)

        xv = x_ref[...]
        dhv = dh_ref[...]
        g = g_ref[...]
        rinv = lax.rsqrt(jnp.mean(xv * xv, axis=-1, keepdims=True) + NORM_EPS)
        xn = xv * rinv
        dn = dhv * sc_ref[...]
        dxn = dn * g
        dx_ref[...] = dr_ref[...] + rinv * (dxn - xn * jnp.mean(dxn * xn, axis=-1, keepdims=True))
        acc_ref[0] += _fold8(dhv)
        acc_ref[1] += _fold8(dhv * (xn * g))
        acc_ref[2] += _fold8(dn * xn)

        @pl.when(i == n - 1)
        def _():
            for q in range(3):
                sums_ref[q:q + 1, :] = jnp.sum(acc_ref[q], axis=0, keepdims=True)

    return pl.pallas_call(
        body, name=name, grid=(n,),
        in_specs=[_row_spec(tm, D), _row_spec(tm, D), _row_spec(tm, D), _vec_spec(1, D), _vec_spec(1, D)],
        out_specs=(_row_spec(tm, D), _vec_spec(3, D)),
        out_shape=(jax.ShapeDtypeStruct((S, D), F32), jax.ShapeDtypeStruct((3, D), F32)),
        scratch_shapes=[pltpu.VMEM((3, 8, D), F32)],
        compiler_params=_cp(("arbitrary",), 40),
    )(dh, x, dres, gain, scale1p)


def _post_norm_res(x, f, gain, gw, *, name):
    S, D = x.shape
    tm = ROW_TILE

    def body(x_ref, f_ref, g_ref, gw_ref, o_ref):
        fv = f_ref[...]
        rinv = lax.rsqrt(jnp.mean(fv * fv, axis=-1, keepdims=True) + NORM_EPS)
        o_ref[...] = x_ref[...] + gw_ref[...] * ((fv * rinv) * g_ref[...])

    return pl.pallas_call(
        body, name=name, grid=(S // tm,),
        in_specs=[_row_spec(tm, D), _row_spec(tm, D), _vec_spec(1, D), _vec_spec(1, D)],
        out_specs=_row_spec(tm, D), out_shape=jax.ShapeDtypeStruct((S, D), F32),
        compiler_params=_cp(("parallel",), 32),
    )(x, f, gain, gw)


def _post_norm_res_bwd(dxo, f, gain, gw, weight, *, name):
    S, D = f.shape
    tm = ROW_TILE
    n = S // tm

    def body(d_ref, f_ref, g_ref, gw_ref, df_ref, sums_ref, acc_ref):
        i = pl.program_id(0)

        @pl.when(i == 0)
        def _():
            acc_ref[...] = jnp.zeros_like(acc_ref)

        fv = f_ref[...]
        dv = d_ref[...]
        g = g_ref[...]
        rinv = lax.rsqrt(jnp.mean(fv * fv, axis=-1, keepdims=True) + NORM_EPS)
        fh = fv * rinv
        dy = dv * gw_ref[...]
        dfh = dy * g
        df_ref[...] = (rinv * (dfh - fh * jnp.mean(dfh * fh, axis=-1, keepdims=True))).astype(BF16)
        acc_ref[0] += _fold8(weight * dv * (fh * g))
        acc_ref[1] += _fold8(dy * fh)

        @pl.when(i == n - 1)
        def _():
            for q in range(2):
                sums_ref[q:q + 1, :] = jnp.sum(acc_ref[q], axis=0, keepdims=True)

    return pl.pallas_call(
        body, name=name, grid=(n,),
        in_specs=[_row_spec(tm, D), _row_spec(tm, D), _vec_spec(1, D), _vec_spec(1, D)],
        out_specs=(_row_spec(tm, D), _vec_spec(2, D)),
        out_shape=(jax.ShapeDtypeStruct((S, D), BF16), jax.ShapeDtypeStruct((2, D), F32)),
        scratch_shapes=[pltpu.VMEM((2, 8, D), F32)],
        compiler_params=_cp(("arbitrary",), 40),
    )(dxo, f, gain, gw)


def _loss_head(y, target, *, name):
    S, D = y.shape
    tm = ROW_TILE

    def body(y_ref, t_ref, l_ref, dy_ref):
        i = pl.program_id(0)

        @pl.when(i == 0)
        def _():
            l_ref[...] = jnp.zeros_like(l_ref)

        err = y_ref[...] - t_ref[...]
        dy_ref[...] = err * (1.0 / D)
        row = jnp.sum(err * err, axis=-1, keepdims=True) * (0.5 / D)
        l_ref[...] += jnp.sum(row, axis=0, keepdims=True)

    return pl.pallas_call(
        body, name=name, grid=(S // tm,),
        in_specs=[_row_spec(tm, D), _row_spec(tm, D)],
        out_specs=(_vec_spec(1, 1), _row_spec(tm, D)),
        out_shape=(jax.ShapeDtypeStruct((1, 1), F32), jax.ShapeDtypeStruct((S, D), F32)),
        compiler_params=_cp(("arbitrary",), 32),
    )(y, target)


def _shift_down(z, j, row):
    return jnp.where(row >= j, pltpu.roll(z, j, 0), 0.0)


def _shift_up(z, j, row, n):
    return jnp.where(row < n - j, pltpu.roll(z, n - j, 0), 0.0)


def _pool_fwd(p, pool_w, pool_scale, *, name):
    S = p.shape[0]
    C = POOL_GROUP

    def body(u_ref, w_ref, sc_ref, o_ref, y_ref):
        g = pl.program_id(0)
        u = u_ref[...]
        row = lax.broadcasted_iota(jnp.int32, (S, C), 0)
        s1 = u + _shift_down(u, 1, row)
        s2 = s1 + _shift_down(s1, 2, row)
        s3 = s2 + _shift_down(s2, 4, row)
        s4 = s3 + _shift_down(s3, 8, row)
        gi = jnp.zeros((S, C), jnp.int32) + g
        win = jnp.where(gi == 0, s1, jnp.where(gi == 1, s2, jnp.where(gi == 2, s3, s4)))
        width = jnp.where(gi == 0, 2, jnp.where(gi == 1, 4, jnp.where(gi == 2, 8, 16)))
        count = jnp.minimum(row + 1, width).astype(F32)
        o = win / count - u
        o_ref[...] = o
        y_ref[...] = jnp.dot(o.astype(BF16), w_ref[0].astype(BF16), preferred_element_type=F32) * sc_ref[...]

    col = pl.BlockSpec((S, C), lambda g: (0, g))
    return pl.pallas_call(
        body, name=name, grid=(POOL_GROUPS,),
        in_specs=[col, pl.BlockSpec((1, C, C), lambda g: (g, 0, 0)), pl.BlockSpec((1, C), lambda g: (0, g))],
        out_specs=(col, col),
        out_shape=(jax.ShapeDtypeStruct((S, POOL_GROUPS * C), F32), jax.ShapeDtypeStruct((S, POOL_GROUPS * C), F32)),
        compiler_params=_cp(("parallel",), 48),
    )(p, pool_w, pool_scale)


def _pool_bwd(dcat, o, pool_w, pool_scale, *, name):
    S = o.shape[0]
    C = POOL_GROUP

    def body(dy_ref, o_ref, w_ref, sc_ref, du_ref, dw_ref, dsc_ref):
        g = pl.program_id(0)
        dy = dy_ref[...]
        ob = o_ref[...].astype(BF16)
        wb = w_ref[0].astype(BF16)
        mixed = jnp.dot(ob, wb, preferred_element_type=F32)
        dsc_ref[...] = jnp.sum(_fold8(dy * mixed), axis=0, keepdims=True)
        dmix = (dy * sc_ref[...]).astype(BF16)
        dw_ref[0] = lax.dot_general(ob, dmix, (((0,), (0,)), ((), ())), preferred_element_type=F32)
        do = lax.dot_general(dmix, wb, (((1,), (1,)), ((), ())), preferred_element_type=F32)
        row = lax.broadcasted_iota(jnp.int32, (S, C), 0)
        gi = jnp.zeros((S, C), jnp.int32) + g
        width = jnp.where(gi == 0, 2, jnp.where(gi == 1, 4, jnp.where(gi == 2, 8, 16)))
        z = do / jnp.minimum(row + 1, width).astype(F32)
        s1 = z + _shift_up(z, 1, row, S)
        s2 = s1 + _shift_up(s1, 2, row, S)
        s3 = s2 + _shift_up(s2, 4, row, S)
        s4 = s3 + _shift_up(s3, 8, row, S)
        win = jnp.where(gi == 0, s1, jnp.where(gi == 1, s2, jnp.where(gi == 2, s3, s4)))
        du_ref[...] = (win - do).astype(BF16)

    col = pl.BlockSpec((S, C), lambda g: (0, g))
    return pl.pallas_call(
        body, name=name, grid=(POOL_GROUPS,),
        in_specs=[col, col, pl.BlockSpec((1, C, C), lambda g: (g, 0, 0)), pl.BlockSpec((1, C), lambda g: (0, g))],
        out_specs=(col, pl.BlockSpec((1, C, C), lambda g: (g, 0, 0)), pl.BlockSpec((1, C), lambda g: (0, g))),
        out_shape=(jax.ShapeDtypeStruct((S, POOL_GROUPS * C), BF16), jax.ShapeDtypeStruct((POOL_GROUPS, C, C), F32),
                   jax.ShapeDtypeStruct((1, POOL_GROUPS * C), F32)),
        compiler_params=_cp(("parallel",), 48),
    )(dcat, o, pool_w, pool_scale)


def _block_ones():
    r = lax.broadcasted_iota(jnp.int32, (128, 128), 0) // HEAD
    c = lax.broadcasted_iota(jnp.int32, (128, 128), 1) // HEAD
    return jnp.where(r == c, 1.0, 0.0).astype(BF16)


def _segsum(x, bd):
    outs = []
    for j in range(x.shape[1] // 128):
        xs = x[:, j * 128:(j + 1) * 128]
        hi = xs.astype(BF16)
        lo = (xs - hi.astype(F32)).astype(BF16)
        outs.append(jnp.dot(hi, bd, preferred_element_type=F32) + jnp.dot(lo, bd, preferred_element_type=F32))
    return jnp.concatenate(outs, axis=1)


def _prep_common(q, qprev, first, mu, wl, w0, a0, kkw, kaw, R):
    tm = q.shape[0]
    row = lax.broadcasted_iota(jnp.int32, q.shape, 0)
    last = qprev[7:8, :] * first
    prev = jnp.where(row == 0, last, pltpu.roll(q, 1, 0))
    ps = q + mu * (prev - q)
    r = ps[:, 0:R]
    k = ps[:, R:2 * R]
    v = ps[:, 2 * R:3 * R]
    lo_in = ps[:, 3 * R:3 * R + LORA_PAD]
    lane = lax.broadcasted_iota(jnp.int32, (tm, LORA_PAD), 1)
    m_w = lane < LORA_W
    m_a = lane < LORA_W + LORA_A
    m_g = lane < LORA_W + LORA_A + LORA_G
    act = jnp.where(m_w, jnp.tanh(lo_in), jnp.where(m_a, lo_in, jnp.where(m_g, jax.nn.sigmoid(lo_in), 0.0)))
    lo = jnp.dot(act.astype(BF16), wl, preferred_element_type=F32)
    wpre = w0 + lo[:, 0:R]
    apre = a0 + lo[:, R:2 * R]
    g = lo[:, 2 * R:3 * R]
    neg = -wpre
    softplus = jnp.maximum(neg, 0.0) + jnp.log(1.0 + jnp.exp(-jnp.abs(neg)))
    wlog = -softplus - 0.5
    ew = jnp.exp(wlog)
    decay = jnp.exp(-ew)
    a = jax.nn.sigmoid(apre)
    kk = k * kkw
    bd = _block_ones()
    n2 = _segsum(kk * kk, bd)
    nrm = jnp.maximum(jnp.sqrt(n2), 1e-12)
    kap = kk / nrm
    kmul = 1.0 + (a - 1.0) * kaw
    k2 = k * kmul
    return dict(prev=prev, r=r, k=k, v=v, act=act, m_w=m_w, m_a=m_a, m_g=m_g, wpre=wpre, g=g, ew=ew, decay=decay,
                a=a, n2=n2, nrm=nrm, kap=kap, kmul=kmul, k2=k2, bd=bd)


def _prev_rows_spec(tm, w):
    return pl.BlockSpec((8, w), lambda i: (jnp.maximum(i * (tm // 8) - 1, 0), 0))


def _rwkv_prep(q, mu, wl, w0, a0, kkw, kaw, *, name):
    S, QW = q.shape
    R = w0.shape[1]
    tm = ROW_TILE // 2

    def body(q_ref, qp_ref, mu_ref, wl_ref, w0_ref, a0_ref, kk_ref, ka_ref, r_ref, w_ref, k_ref, v_ref, kap_ref,
             b_ref, g_ref):
        first = jnp.where(pl.program_id(0) > 0, 1.0, 0.0)
        t = _prep_common(q_ref[...], qp_ref[...], first, mu_ref[...], wl_ref[...], w0_ref[...], a0_ref[...],
                         kk_ref[...], ka_ref[...], R)
        r_ref[...] = t["r"]
        w_ref[...] = t["decay"]
        k_ref[...] = t["k2"]
        v_ref[...] = t["v"]
        kap_ref[...] = t["kap"]
        b_ref[...] = t["kap"] * t["a"]
        g_ref[...] = t["g"]

    vec = _vec_spec(1, R)
    return pl.pallas_call(
        body, name=name, grid=(S // tm,),
        in_specs=[_row_spec(tm, QW), _prev_rows_spec(tm, QW), _vec_spec(1, QW), _vec_spec(LORA_PAD, 3 * R), vec, vec,
                  vec, vec],
        out_specs=tuple([_row_spec(tm, R)] * 7),
        out_shape=tuple([jax.ShapeDtypeStruct((S, R), F32)] * 7),
        compiler_params=_cp(("parallel",), 48),
    )(q, q, mu, wl, w0, a0, kkw, kaw)


def _rwkv_prep_bwd(q, mu, wl, w0, a0, kkw, kaw, grads, *, name):
    S, QW = q.shape
    R = w0.shape[1]
    tm = ROW_TILE // 2
    n = S // tm

    def body(q_ref, qp_ref, mu_ref, wl_ref, w0_ref, a0_ref, kk_ref, ka_ref, dr_ref, dw_ref, dk2_ref, dv_ref, dkap_ref,
             db_ref, dg_ref, drb_ref, dk2b_ref, dvb_ref, dps_ref, dwl_ref, sums_ref, acc_ref):
        i = pl.program_id(0)

        @pl.when(i == 0)
        def _():
            acc_ref[...] = jnp.zeros_like(acc_ref)
            dwl_ref[...] = jnp.zeros_like(dwl_ref)

        first = jnp.where(i > 0, 1.0, 0.0)
        wl = wl_ref[...]
        kkw = kk_ref[...]
        kaw = ka_ref[...]
        t = _prep_common(q_ref[...], qp_ref[...], first, mu_ref[...], wl, w0_ref[...], a0_ref[...], kkw, kaw, R)
        a, kap, k, act = t["a"], t["kap"], t["k"], t["act"]
        db = db_ref[...]
        dk2 = dk2_ref[...] + dk2b_ref[...]
        dkap = dkap_ref[...] + db * a
        da = db * kap + dk2 * k * kaw
        dk = dk2 * t["kmul"]
        proj = jnp.where(jnp.sqrt(t["n2"]) > 1e-12, _segsum(kap * dkap, t["bd"]), 0.0)
        dkk = (dkap - kap * proj) / t["nrm"]
        dk = dk + dkk * kkw
        dapre = da * a * (1.0 - a)
        dwlog = dw_ref[...] * t["decay"] * (-t["ew"])
        dwpre = dwlog * jax.nn.sigmoid(-t["wpre"])
        acc_ref[0] += _fold8(dwpre)
        acc_ref[1] += _fold8(dapre)
        acc_ref[2] += _fold8(dkk * k)
        acc_ref[3] += _fold8(dk2 * k * (a - 1.0))
        dlo = jnp.concatenate([dwpre, dapre, dg_ref[...]], axis=1).astype(BF16)
        dwl_ref[...] += lax.dot_general(act.astype(BF16), dlo, (((0,), (0,)), ((), ())), preferred_element_type=F32)
        dact = lax.dot_general(dlo, wl, (((1,), (1,)), ((), ())), preferred_element_type=F32)
        dlin = jnp.where(t["m_w"], dact * (1.0 - act * act),
                         jnp.where(t["m_a"], dact, jnp.where(t["m_g"], dact * act * (1.0 - act), 0.0)))
        dps_ref[:, 0:R] = dr_ref[...] + drb_ref[...]
        dps_ref[:, R:2 * R] = dk
        dps_ref[:, 2 * R:3 * R] = dv_ref[...] + dvb_ref[...]
        dps_ref[:, 3 * R:3 * R + LORA_PAD] = dlin
        dps_ref[:, 3 * R + LORA_PAD:] = jnp.zeros((tm, QW - 3 * R - LORA_PAD), F32)

        @pl.when(i == n - 1)
        def _():
            for j in range(4):
                sums_ref[j:j + 1, :] = jnp.sum(acc_ref[j], axis=0, keepdims=True)

    vec = _vec_spec(1, R)
    return pl.pallas_call(
        body, name=name, grid=(n,),
        in_specs=[_row_spec(tm, QW), _prev_rows_spec(tm, QW), _vec_spec(1, QW), _vec_spec(LORA_PAD, 3 * R), vec, vec,
                  vec, vec] + [_row_spec(tm, R)] * 10,
        out_specs=(_row_spec(tm, QW), _vec_spec(LORA_PAD, 3 * R), _vec_spec(4, R)),
        out_shape=(jax.ShapeDtypeStruct((S, QW), F32), jax.ShapeDtypeStruct((LORA_PAD, 3 * R), F32),
                   jax.ShapeDtypeStruct((4, R), F32)),
        scratch_shapes=[pltpu.VMEM((4, 8, R), F32)],
        compiler_params=_cp(("arbitrary",), 56),
    )(q, q, mu, wl, w0, a0, kkw, kaw, *grads)


def _tshift_bwd(dps, q, mu, *, name):
    S, QW = q.shape
    tm = ROW_TILE // 2
    n = S // tm

    def body(d_ref, dn_ref, q_ref, qp_ref, mu_ref, dq_ref, dmu_ref, acc_ref):
        i = pl.program_id(0)

        @pl.when(i == 0)
        def _():
            acc_ref[...] = jnp.zeros_like(acc_ref)

        mu = mu_ref[...]
        d = d_ref[...]
        qv = q_ref[...]
        row = lax.broadcasted_iota(jnp.int32, d.shape, 0)
        first = jnp.where(i > 0, 1.0, 0.0)
        notlast = jnp.where(i < n - 1, 1.0, 0.0)
        prev = jnp.where(row == 0, qp_ref[7:8, :] * first, pltpu.roll(qv, 1, 0))
        z = d * mu
        nxt = jnp.where(row == tm - 1, dn_ref[0:1, :] * mu * notlast, pltpu.roll(z, tm - 1, 0))
        dq_ref[...] = (d * (1.0 - mu) + nxt).astype(BF16)
        acc_ref[...] += _fold8(d * (prev - qv))

        @pl.when(i == n - 1)
        def _():
            dmu_ref[...] = jnp.sum(acc_ref[...], axis=0, keepdims=True)

    nblk8 = S // 8
    next_spec = pl.BlockSpec((8, QW), lambda i: (jnp.minimum((i + 1) * (tm // 8), nblk8 - 1), 0))
    return pl.pallas_call(
        body, name=name, grid=(n,),
        in_specs=[_row_spec(tm, QW), next_spec, _row_spec(tm, QW), _prev_rows_spec(tm, QW), _vec_spec(1, QW)],
        out_specs=(_row_spec(tm, QW), _vec_spec(1, QW)),
        out_shape=(jax.ShapeDtypeStruct((S, QW), BF16), jax.ShapeDtypeStruct((1, QW), F32)),
        scratch_shapes=[pltpu.VMEM((8, QW), F32)],
        compiler_params=_cp(("arbitrary",), 48),
    )(dps, dps, q, q, mu)


def _post_common(ysc, r, k2, v, lnw, lnb, rk):
    bd = _block_ones()
    mean = _segsum(ysc, bd) * (1.0 / HEAD)
    d = ysc - mean
    var = _segsum(d * d, bd) * (1.0 / HEAD)
    rstd = lax.rsqrt(var + LN_X_EPS)
    yh = d * rstd
    rkk = _segsum(r * k2 * rk, bd)
    z = yh * lnw + lnb + rkk * v
    return bd, rstd, yh, rkk, z


def _rwkv_post(ysc, r, k2, v, g, ypool, lnw, lnb, rk, *, name):
    S, R = ysc.shape
    PW = ypool.shape[1]
    tm = ROW_TILE

    def body(y_ref, r_ref, k_ref, v_ref, g_ref, yp_ref, lw_ref, lb_ref, rk_ref, cat_ref):
        _, _, _, _, z = _post_common(y_ref[...], r_ref[...], k_ref[...], v_ref[...], lw_ref[...], lb_ref[...],
                                     rk_ref[...])
        cat_ref[:, 0:PW] = yp_ref[...].astype(BF16)
        cat_ref[:, PW:] = (z * g_ref[...]).astype(BF16)

    vec = _vec_spec(1, R)
    return pl.pallas_call(
        body, name=name, grid=(S // tm,),
        in_specs=[_row_spec(tm, R)] * 5 + [_row_spec(tm, PW), vec, vec, vec],
        out_specs=_row_spec(tm, PW + R), out_shape=jax.ShapeDtypeStruct((S, PW + R), BF16),
        compiler_params=_cp(("parallel",), 48),
    )(ysc, r, k2, v, g, ypool, lnw, lnb, rk)


def _rwkv_post_bwd(dcat, ysc, r, k2, v, g, lnw, lnb, rk, *, name):
    S, R = ysc.shape
    tm = ROW_TILE
    n = S // tm

    def body(d_ref, y_ref, r_ref, k_ref, v_ref, g_ref, lw_ref, lb_ref, rk_ref, dy_ref, dg_ref, drb_ref, dkb_ref,
             dvb_ref, sums_ref, acc_ref):
        i = pl.program_id(0)

        @pl.when(i == 0)
        def _():
            acc_ref[...] = jnp.zeros_like(acc_ref)

        rv, kv, vv, lw, rkw = r_ref[...], k_ref[...], v_ref[...], lw_ref[...], rk_ref[...]
        bd, rstd, yh, rkk, z = _post_common(y_ref[...], rv, kv, vv, lw, lb_ref[...], rkw)
        dyr = d_ref[...]
        dg_ref[...] = dyr * z
        dz = dyr * g_ref[...]
        dyh = dz * lw
        dy_ref[...] = rstd * (dyh - _segsum(dyh, bd) * (1.0 / HEAD) - yh * (_segsum(dyh * yh, bd) * (1.0 / HEAD)))
        dvb_ref[...] = dz * rkk
        drkk = _segsum(dz * vv, bd)
        drb_ref[...] = drkk * kv * rkw
        dkb_ref[...] = drkk * rv * rkw
        acc_ref[0] += _fold8(dz * yh)
        acc_ref[1] += _fold8(dz)
        acc_ref[2] += _fold8(drkk * rv * kv)

        @pl.when(i == n - 1)
        def _():
            for j in range(3):
                sums_ref[j:j + 1, :] = jnp.sum(acc_ref[j], axis=0, keepdims=True)

    vec = _vec_spec(1, R)
    dspec = _row_spec(tm, R)
    return pl.pallas_call(
        body, name=name, grid=(n,),
        in_specs=[dspec] + [_row_spec(tm, R)] * 5 + [vec, vec, vec],
        out_specs=tuple([_row_spec(tm, R)] * 5) + (_vec_spec(3, R),),
        out_shape=tuple([jax.ShapeDtypeStruct((S, R), F32)] * 5) + (jax.ShapeDtypeStruct((3, R), F32),),
        scratch_shapes=[pltpu.VMEM((3, 8, R), F32)],
        compiler_params=_cp(("arbitrary",), 56),
    )(dcat, ysc, r, k2, v, g, lnw, lnb, rk)


SEL_ROWS = 64


def _column_selector():
    row = lax.broadcasted_iota(jnp.int32, (SEL_ROWS, 8 * 128), 0)
    col = lax.broadcasted_iota(jnp.int32, (SEL_ROWS, 8 * 128), 1)
    head, rest = row // 32, row % 32
    hit = (rest < 24) & (rest % 8 == col // 128) & (head == (col % 128) // HEAD)
    return jnp.where(hit, 1.0, 0.0).astype(BF16)


def _expand_columns(x, sel):
    hi = x.astype(BF16).astype(F32)
    r1 = x - hi
    mid = r1.astype(BF16).astype(F32)
    lo = (r1 - mid).astype(BF16).astype(F32)
    terms = jnp.concatenate([hi, mid, lo, jnp.zeros_like(x)], axis=0)
    both = jnp.concatenate([terms, pltpu.roll(terms, HEAD, 1)], axis=0)[:, 0:HEAD]
    return lax.dot_general(both.astype(BF16), sel, (((0,), (0,)), ((), ())), preferred_element_type=F32)


def _head_sum_weights():
    row = lax.broadcasted_iota(jnp.int32, (256, 256), 0)
    col = lax.broadcasted_iota(jnp.int32, (256, 256), 1)
    return jnp.where((row % 128) // HEAD == col // 128, 1.0, 0.0).astype(BF16)


def _head_sums_mxu(products, w2):
    rows = []
    for p in products:
        hi = p.astype(BF16)
        rows.append(jnp.concatenate([hi, (p - hi.astype(F32)).astype(BF16)], axis=1))
    out = jnp.dot(jnp.concatenate(rows, axis=0), w2, preferred_element_type=F32)
    return [(out[i * HEAD:(i + 1) * HEAD, 0:128], out[i * HEAD:(i + 1) * HEAD, 128:256]) for i in range(len(products))]


def _masked_rows(rows, negate=False):
    head_a = (lax.broadcasted_iota(jnp.int32, rows.shape, 1) % 128) < HEAD
    v = -rows if negate else rows
    return jnp.where(head_a, v, 0.0), jnp.where(head_a, 0.0, v)


def _lane_sums(x, row_a, row_b):
    return jnp.sum(x * row_a, axis=1, keepdims=True), jnp.sum(x * row_b, axis=1, keepdims=True)


def _scan_fwd(r, w, k, v, kap, b, *, name, plan=None):
    S, R = r.shape
    G, T = SCAN_G, SCAN_T
    NP = R // 128
    assert NP % G == 0 and S % T == 0
    GW = 128 * G

    def body(r_ref, w_ref, k_ref, v_ref, kap_ref, b_ref, sel_ref, w2_ref, y_ref, sa_ref, st_ref, s_scr, vc_scr,
             yt_scr, sat_scr):
        c = pl.program_id(1)

        @pl.when(c == 0)
        def _():
            s_scr[...] = jnp.zeros_like(s_scr)

        yt_scr[...] = jnp.zeros_like(yt_scr)
        sat_scr[...] = jnp.zeros_like(sat_scr)
        lane = lax.broadcasted_iota(jnp.int32, (HEAD, 128), 1)
        m_a = lane < HEAD

        def block(tb, carry):
            t0 = pl.multiple_of(tb * 8, 8)
            rb, wb, kb = r_ref[pl.ds(t0, 8), :], w_ref[pl.ds(t0, 8), :], k_ref[pl.ds(t0, 8), :]
            pb, bb, vb = kap_ref[pl.ds(t0, 8), :], b_ref[pl.ds(t0, 8), :], v_ref[pl.ds(t0, 8), :]
            for g in range(G):
                vc_scr[g] = _expand_columns(vb[:, g * 128:(g + 1) * 128], sel_ref[...])

            def put_y(g, parts, hot_y):
                yt_scr[g, 0:HEAD, :] = jnp.where(hot_y, parts[0], yt_scr[g, 0:HEAD, :])
                yt_scr[g, HEAD:, :] = jnp.where(hot_y, parts[1], yt_scr[g, HEAD:, :])

            def put_sa(g, parts, hot_t):
                sat_scr[g, 0:HEAD, :] = jnp.where(hot_t, parts[0], sat_scr[g, 0:HEAD, :])
                sat_scr[g, HEAD:, :] = jnp.where(hot_t, parts[1], sat_scr[g, HEAD:, :])

            npa, npb = _masked_rows(pb, negate=True)
            for j in range(8):
                t = t0 + j
                cols = slice(j * 128, (j + 1) * 128)
                sa_parts, products = [], []
                for g in range(G):
                    sl = slice(g * 128, (g + 1) * 128)
                    sa_parts.append(_lane_sums(s_scr[g], npa[j:j + 1, sl], npb[j:j + 1, sl]))
                hot_t = lane == t
                for g in range(G):
                    sl = slice(g * 128, (g + 1) * 128)
                    sa = jnp.where(m_a, sa_parts[g][0], sa_parts[g][1])
                    st = s_scr[g] * wb[j:j + 1, sl] + sa * bb[j:j + 1, sl] + vc_scr[g, :, cols] * kb[j:j + 1, sl]
                    s_scr[g] = st
                    st_ref[g, t] = st
                    put_sa(g, sa_parts[g], hot_t)
                    products.append(st * rb[j:j + 1, sl])
                for g, parts in enumerate(_head_sums_mxu(products, w2_ref[...])):
                    put_y(g, parts, hot_t)
            return carry

        lax.fori_loop(0, T // 8, block, 0)
        for g in range(G):
            y_ref[:, g * 128:(g + 1) * 128] = yt_scr[g].T[0:T, :]
            sa_ref[:, g * 128:(g + 1) * 128] = sat_scr[g].T[0:T, :]

    tspec = pl.BlockSpec((T, GW), lambda p, c: (c, p))
    sel_spec = pl.BlockSpec((SEL_ROWS, 8 * 128), lambda p, c: (0, 0))
    grid = (NP // G, S // T)
    x_in, x_out, x_shapes, x_scr, x_ops = _host_args(plan)
    w2_spec = pl.BlockSpec((256, 256), lambda p, c: (0, 0))
    outs = pl.pallas_call(
        _host(body, plan, 8, 3, *_grid_ends(grid)), name=name, grid=grid,
        in_specs=[tspec] * 6 + [sel_spec, w2_spec] + x_in,
        out_specs=tuple([tspec, tspec, pl.BlockSpec((G, T, HEAD, 128), lambda p, c: (p, c, 0, 0))] + x_out),
        out_shape=tuple([jax.ShapeDtypeStruct((S, R), F32), jax.ShapeDtypeStruct((S, R), F32),
                         jax.ShapeDtypeStruct((NP, S, HEAD, 128), F32)] + x_shapes),
        scratch_shapes=[pltpu.VMEM((G, HEAD, 128), F32), pltpu.VMEM((G, HEAD, 8 * 128), F32),
                        pltpu.VMEM((G, 128, 128), F32), pltpu.VMEM((G, 128, 128), F32)] + x_scr,
        compiler_params=_cp(_semantics(plan, ("parallel", "arbitrary")), 48),
    )(r, w, k, v, kap, b, _column_selector(), _head_sum_weights(), *x_ops)
    return outs[0], outs[1], outs[2], tuple(outs[3:])


def _scan_bwd(r, w, k, v, kap, b, sa, dy, states, *, name, plan=None):
    S, R = r.shape
    G, T = SCAN_G_BWD, SCAN_T_BWD
    NP = R // 128
    NC = S // T
    GW = 128 * G
    assert NP % G == 0

    def body(r_ref, w_ref, k_ref, v_ref, kap_ref, b_ref, sa_ref, dy_ref, st_ref, sp_ref, sel_ref, w2_ref, dr_ref,
             dw_ref, dk_ref, dv_ref, dkap_ref, db_ref, ds_scr, vc_scr, dyc_scr, sac_scr, dvt_scr):
        ci = pl.program_id(1)

        @pl.when(ci == 0)
        def _():
            ds_scr[...] = jnp.zeros_like(ds_scr)

        dvt_scr[...] = jnp.zeros_like(dvt_scr)
        lane = lax.broadcasted_iota(jnp.int32, (HEAD, 128), 1)
        m_a = lane < HEAD
        sub = lax.broadcasted_iota(jnp.int32, (8, 128), 0)
        zero_i = jnp.zeros((HEAD, 128), jnp.int32)
        has_prev = jnp.where(ci < NC - 1, 1.0, 0.0)

        def state_before(g, t):
            at_start = (zero_i + t) == 0
            return jnp.where(at_start, sp_ref[g, 0] * has_prev, st_ref[g, jnp.maximum(t - 1, 0)])

        def block(it, carry):
            tb = T // 8 - 1 - it
            t0 = pl.multiple_of(tb * 8, 8)
            rb, wb, kb = r_ref[pl.ds(t0, 8), :], w_ref[pl.ds(t0, 8), :], k_ref[pl.ds(t0, 8), :]
            pb, bb = kap_ref[pl.ds(t0, 8), :], b_ref[pl.ds(t0, 8), :]
            vb, dyb, sab = v_ref[pl.ds(t0, 8), :], dy_ref[pl.ds(t0, 8), :], sa_ref[pl.ds(t0, 8), :]
            for g in range(G):
                sl = slice(g * 128, (g + 1) * 128)
                vc_scr[g] = _expand_columns(vb[:, sl], sel_ref[...])
                dyc_scr[g] = _expand_columns(dyb[:, sl], sel_ref[...])
                sac_scr[g] = _expand_columns(sab[:, sl], sel_ref[...])
            outs = [[jnp.zeros((8, 128), F32) for _ in range(5)] for _ in range(G)]
            bba, bbb = _masked_rows(bb)
            def recurrence(j):
                cols = slice(j * 128, (j + 1) * 128)
                dsp, dsa_parts = [], []
                for g in range(G):
                    sl = slice(g * 128, (g + 1) * 128)
                    ds = ds_scr[g] + dyc_scr[g, :, cols] * rb[j:j + 1, sl]
                    dsp.append(ds)
                    dsa_parts.append(_lane_sums(ds, bba[j:j + 1, sl], bbb[j:j + 1, sl]))
                dsas = []
                for g in range(G):
                    sl = slice(g * 128, (g + 1) * 128)
                    dsa = jnp.where(m_a, dsa_parts[g][0], dsa_parts[g][1])
                    ds_scr[g] = dsp[g] * wb[j:j + 1, sl] - dsa * pb[j:j + 1, sl]
                    dsas.append(dsa)
                return dsp, dsas

            def side_work(j, dsp, dsas):
                t = t0 + j
                hot = lane == t
                cols = slice(j * 128, (j + 1) * 128)
                dvs = _head_sums_mxu([dsp[g] * kb[j:j + 1, g * 128:(g + 1) * 128] for g in range(G)], w2_ref[...])
                for g in range(G):
                    ds = dsp[g]
                    s_p = st_ref[g, t - 1] if j > 0 else state_before(g, t)
                    dr_row = jnp.sum(st_ref[g, t] * dyc_scr[g, :, cols], axis=0, keepdims=True)
                    dk_row = jnp.sum(ds * vc_scr[g, :, cols], axis=0, keepdims=True)
                    db_row = jnp.sum(ds * sac_scr[g, :, cols], axis=0, keepdims=True)
                    dw_row = jnp.sum(ds * s_p, axis=0, keepdims=True)
                    dkap_row = -jnp.sum(s_p * dsas[g], axis=0, keepdims=True)
                    dvt_scr[g, 0:HEAD, :] = jnp.where(hot, dvs[g][0], dvt_scr[g, 0:HEAD, :])
                    dvt_scr[g, HEAD:, :] = jnp.where(hot, dvs[g][1], dvt_scr[g, HEAD:, :])
                    pick = sub == j
                    for q, row in enumerate((dr_row, dw_row, dk_row, dkap_row, db_row)):
                        outs[g][q] = jnp.where(pick, row, outs[g][q])

            pending = None
            for j in range(7, -1, -1):
                done = recurrence(j)
                if pending is not None:
                    side_work(*pending)
                pending = (j, *done)
            side_work(*pending)
            for g in range(G):
                sl = slice(g * 128, (g + 1) * 128)
                for q, ref in enumerate((dr_ref, dw_ref, dk_ref, dkap_ref, db_ref)):
                    ref[pl.ds(t0, 8), sl] = outs[g][q]
            return carry

        lax.fori_loop(0, T // 8, block, 0)
        for g in range(G):
            dv_ref[:, g * 128:(g + 1) * 128] = dvt_scr[g].T[0:T, :]

    tspec = pl.BlockSpec((T, GW), lambda p, c: (NC - 1 - c, p))
    st_spec = pl.BlockSpec((G, T, HEAD, 128), lambda p, c: (p, NC - 1 - c, 0, 0))
    prev_spec = pl.BlockSpec((G, 1, HEAD, 128), lambda p, c: (p, jnp.maximum((NC - 1 - c) * T - 1, 0), 0, 0))
    sel_spec = pl.BlockSpec((SEL_ROWS, 8 * 128), lambda p, c: (0, 0))
    grid = (NP // G, NC)
    x_in, x_out, x_shapes, x_scr, x_ops = _host_args(plan)
    outs = pl.pallas_call(
        _host(body, plan, 12, 6, *_grid_ends(grid)), name=name, grid=grid,
        in_specs=[tspec] * 8 + [st_spec, prev_spec, sel_spec, pl.BlockSpec((256, 256), lambda p, c: (0, 0))] + x_in,
        out_specs=tuple([tspec] * 6 + x_out),
        out_shape=tuple([jax.ShapeDtypeStruct((S, R), F32)] * 6 + x_shapes),
        scratch_shapes=[pltpu.VMEM((G, HEAD, 128), F32), pltpu.VMEM((G, HEAD, 8 * 128), F32),
                        pltpu.VMEM((G, HEAD, 8 * 128), F32), pltpu.VMEM((G, HEAD, 8 * 128), F32),
                        pltpu.VMEM((G, 128, 128), F32)] + x_scr,
        compiler_params=_cp(_semantics(plan, ("parallel", "arbitrary")), 48),
    )(r, w, k, v, kap, b, sa, dy, states, states, _column_selector(), _head_sum_weights(), *x_ops)
    return tuple(outs[:6]), tuple(outs[6:])


def _sum_parts(parts, *, name):
    P, rows, W = parts.shape
    tr = rows
    for cand in (1024, 512, 256, 128, 64, 32, 16, 8):
        if rows % cand == 0:
            tr = cand
            break

    def body(p_ref, o_ref):
        acc = p_ref[0]
        for s in range(1, P):
            acc = acc + p_ref[s]
        o_ref[...] = acc

    return pl.pallas_call(
        body, name=name, grid=(rows // tr,),
        in_specs=[pl.BlockSpec((P, tr, W), lambda i: (0, i, 0))],
        out_specs=pl.BlockSpec((tr, W), lambda i: (i, 0)), out_shape=jax.ShapeDtypeStruct((rows, W), F32),
        compiler_params=_cp(("parallel",), 32),
    )(parts)


def _adamw(w, m, v, parts, *, name):
    R, C = w.shape
    P = parts.shape[0]
    tr = R
    for cand in (1024, 512, 256, 128, 64, 32, 16, 8):
        if R % cand == 0 and cand * C * 4 * (7 + P) <= 10 * 1024 * 1024:
            tr = cand
            break
    bc1 = 1.0 - ADAM_B1 ** ADAM_STEP
    bc2 = 1.0 - ADAM_B2 ** ADAM_STEP

    def body(w_ref, m_ref, v_ref, p_ref, g_ref, d_ref, nm_ref, nv_ref):
        g = p_ref[0].astype(F32)
        for s in range(1, P):
            g = g + p_ref[s].astype(F32)
        m1 = ADAM_B1 * m_ref[...] + (1.0 - ADAM_B1) * g
        v1 = ADAM_B2 * v_ref[...] + (1.0 - ADAM_B2) * (g * g)
        m_hat = m1 / bc1
        v_hat = v1 / bc2
        g_ref[...] = g
        d_ref[...] = -ADAM_LR * (m_hat / (jnp.sqrt(v_hat) + ADAM_EPS) + ADAM_WD * w_ref[...])
        nm_ref[...] = m1
        nv_ref[...] = v1

    spec = pl.BlockSpec((tr, C), lambda i: (i, 0))
    return pl.pallas_call(
        body, name=name, grid=(R // tr,),
        in_specs=[spec, spec, spec, pl.BlockSpec((P, tr, C), lambda i: (0, i, 0))],
        out_specs=(spec, spec, spec, spec), out_shape=tuple([jax.ShapeDtypeStruct((R, C), F32)] * 4),
        compiler_params=_cp(("parallel",), 40),
    )(w, m, v, parts)


def _cols_full(g8):
    n, rows, c = g8.shape
    return jnp.transpose(g8, (1, 0, 2)).reshape(rows, n * c)


def _cols_split(full):
    rows, cols = full.shape
    return jnp.transpose(full.reshape(rows, N_DEV, cols // N_DEV), (1, 0, 2))


def _pack(vals, rows_multiple=512):
    flat = jnp.concatenate([v.reshape(-1).astype(F32) for v in vals])
    n = flat.shape[0]
    unit = 128 * rows_multiple
    padded = ((n + unit - 1) // unit) * unit
    return jnp.pad(flat, (0, padded - n)).reshape(padded // 128, 128)


def _unpack(packed, shapes):
    flat = packed.reshape(-1)
    out, off = [], 0
    for shp in shapes:
        size = 1
        for d in shp:
            size *= d
        out.append(flat[off:off + size].reshape(shp))
        off += size
    return out


def _ffn_forward(x, weights, gpre, gpost, shift, scale1p, gw, tag, up_plan=None, down_from_plan=None):
    g8, u8, d8 = weights
    h = _pre_norm_mod(x, gpre, shift, scale1p, name=f"{tag}_pre")
    au, s, carried = _ffn_up(h, g8, u8, tm=1024, tk=2048, plan=up_plan,
                             name=f"{tag}_up" + ("_carry" if up_plan else ""))
    if down_from_plan is not None:
        d8 = carried[down_from_plan]
    f = _mm(s, d8.reshape(-1, d8.shape[2]), tm=1024, tn=1024, tk=2048, name=f"{tag}_down")
    xo = _post_norm_res(x, f, gpost, gw, name=f"{tag}_post")
    return xo, (h, au, s, f), d8, carried


def _ffn_backward(dxo, x, saved, weights, gpre, gpost, scale1p, gw, tag, plans=None, own_sums=None,
                  swap_in_dh=False):
    g8, u8, d8 = weights
    h, au, s, f = saved
    plans = dict(plans or {})

    def nm(key):
        return f"{tag}_{key}" + ("_carry" if plans.get(key) is not None else "")

    df, post_sums = _post_norm_res_bwd(dxo, f, gpost, gw, MACARON, name=f"{tag}_post_bwd")
    dwd, got_dwd = _ffn_dwd(s, df, tn=1024, tk=1024, plan=plans.get("dwd"), name=nm("dwd"))
    if own_sums:
        plans["down_bwd"] = _ChipsPlan(own_sums([dwd], "d"))
    dau, got_down = _ffn_down_bwd(df, d8, au, tm=1024, tk=2048, plan=plans.get("down_bwd"), name=nm("down_bwd"))
    dwgu, got_dwgu = _ffn_dwgu(h, dau, tm=1024, tk=1024, plan=plans.get("dwgu"), name=nm("dwgu"))
    if own_sums:
        plans["dh"] = _ChipsPlan(own_sums([dwgu], "gu"))
    elif swap_in_dh:
        plans["dh"] = _SwapPlan([dwgu, dwd])
    dh, got_dh = _ffn_dh(dau, g8, u8, tm=1024, tn=1024, plan=plans.get("dh"),
                         name=nm("dh") + ("_swap" if swap_in_dh else ""))
    dx, pre_sums = _pre_norm_mod_bwd(dh, x, dxo, gpre, scale1p, name=f"{tag}_pre_bwd")
    carried = {"dwd": got_dwd, "down_bwd": got_down, "dwgu": got_dwgu, "dh": got_dh}
    return dx, dwgu, dwd, pre_sums, post_sums, carried


def kernel(x, c, w_ada, b_ada, norm_pre, norm_post, ffn1_w_gate, ffn1_w_up, ffn1_w_down, w_in, mu_shift, pool_w, pool_scale, w0, w2, a0, a2, g2, k_k, k_a, r_k, lnx_w, lnx_b, w_out, ffn2_w_gate, ffn2_w_up, ffn2_w_down, loss_target, m_w_ada, m_b_ada, m_norm_pre, m_norm_post, m_ffn1_w_gate, m_ffn1_w_up, m_ffn1_w_down, m_w_in, m_mu_shift, m_pool_w, m_pool_scale, m_w0, m_w2, m_a0, m_a2, m_g2, m_k_k, m_k_a, m_r_k, m_lnx_w, m_lnx_b, m_w_out, m_ffn2_w_gate, m_ffn2_w_up, m_ffn2_w_down, v_w_ada, v_b_ada, v_norm_pre, v_norm_post, v_ffn1_w_gate, v_ffn1_w_up, v_ffn1_w_down, v_w_in, v_mu_shift, v_pool_w, v_pool_scale, v_w0, v_w2, v_a0, v_a2, v_g2, v_k_k, v_k_a, v_r_k, v_lnx_w, v_lnx_b, v_w_out, v_ffn2_w_gate, v_ffn2_w_up, v_ffn2_w_down):
    names = ["w_ada", "b_ada", "norm_pre", "norm_post", "ffn1_w_gate", "ffn1_w_up", "ffn1_w_down", "w_in", "mu_shift",
             "pool_w", "pool_scale", "w0", "w2", "a0", "a2", "g2", "k_k", "k_a", "r_k", "lnx_w", "lnx_b", "w_out",
             "ffn2_w_gate", "ffn2_w_up", "ffn2_w_down"]
    env = dict(locals())
    W = {n: env[n][0] for n in names}
    M1 = {n: env["m_" + n][0] for n in names}
    V1 = {n: env["v_" + n][0] for n in names}

    me = _my_index()
    xs = x[0]
    tgt = loss_target[0]
    S, D = xs.shape
    F = W["ffn1_w_gate"].shape[1] * N_DEV
    R = W["w0"].shape[0]
    PW = D - R
    IN_W = W["w_in"].shape[1] * N_DEV
    P_W = F
    QW = P_W - PW
    NMOD = 9 * D
    ada_c = W["w_ada"].shape[1]

    c_all, npre8, npost8, w2_8, a2_8, g2_8 = _exchange(
        [c, W["norm_pre"], W["norm_post"], W["w2"].astype(BF16), W["a2"].astype(BF16), W["g2"].astype(BF16)],
        scatter=False, name="gather_small")
    c_all = c_all.reshape(N_DEV, D)
    gpre = _cols_full(npre8)
    gpost = _cols_full(npost8)
    wl = jnp.zeros((LORA_PAD, 3 * R), BF16)
    wl = wl.at[0:LORA_W, 0:R].set(_cols_full(w2_8))
    wl = wl.at[LORA_W:LORA_W + LORA_A, R:2 * R].set(_cols_full(a2_8))
    wl = wl.at[LORA_W + LORA_A:LORA_W + LORA_A + LORA_G, 2 * R:3 * R].set(_cols_full(g2_8))

    sc_all = jax.nn.silu(c_all)
    sc_pad = jnp.concatenate([sc_all, jnp.zeros((8, D), F32)], axis=0).astype(BF16)
    modcols = _mm(sc_pad, W["w_ada"], tm=16, tn=ada_c, tk=256, name="ada_fwd")[0:N_DEV]
    modcols = modcols + lax.dynamic_slice(W["b_ada"], (me * ada_c,), (ada_c,))[None, :]
    (mod8,) = _exchange([modcols], scatter=False, name="gather_mod")
    mod = lax.dynamic_index_in_dim(mod8, me, axis=1, keepdims=False).reshape(9, D)

    def mod_row(i):
        return mod[i:i + 1, :]

    f_pad = FF_TILE - F // N_DEV

    def ffn_shards(tag):
        return [jnp.pad(W[f"{tag}_w_gate"].astype(BF16), ((0, 0), (0, f_pad))),
                jnp.pad(W[f"{tag}_w_up"].astype(BF16), ((0, 0), (0, f_pad))),
                jnp.pad(W[f"{tag}_w_down"].astype(BF16), ((0, f_pad), (0, 0)))]

    ffn1_shards = ffn_shards("ffn1")
    g8_1, u8_1 = _gather_two_level(ffn1_shards[:2], name="gather_ffn_up")
    up_plan = _GatherPlan([W["w_in"].astype(BF16), ffn1_shards[2]])
    scan_plan = _GatherPlan(ffn_shards("ffn2") + [W["w_out"].astype(BF16)])

    mu_p = jnp.pad(W["mu_shift"], (0, QW - W["mu_shift"].shape[0]))[None, :]
    vec = lambda a: a.reshape(1, -1)
    w0r, a0r, kkr, kar = vec(W["w0"]), vec(W["a0"]), vec(W["k_k"]), vec(W["k_a"])
    lnw, lnb, rkr = vec(W["lnx_w"]), vec(W["lnx_b"]), vec(W["r_k"])
    pscale = vec(W["pool_scale"])

    sc1p = [1.0 + mod_row(3 * s + 1) for s in range(3)]
    shifts = [mod_row(3 * s) for s in range(3)]
    wgts = [MACARON, 1.0, MACARON]
    gws = [wgts[s] * (1.0 + mod_row(3 * s + 2)) for s in range(3)]
    gp = [gpre[s:s + 1] for s in range(3)]
    gq = [gpost[s:s + 1] for s in range(3)]

    x1, sv1, d8_1, (win8, _) = _ffn_forward(xs, (g8_1, u8_1, None), gp[0], gq[0], shifts[0], sc1p[0], gws[0], "ffn",
                                            up_plan=up_plan, down_from_plan=1)
    ffn1_w = (g8_1, u8_1, d8_1)
    w_in_p = jnp.pad(_cols_full(win8), ((0, 0), (0, P_W - IN_W)))

    h2 = _pre_norm_mod(x1, gp[1], shifts[1], sc1p[1], name="mix_pre")
    p = _mm(h2, w_in_p, tm=1024, tn=512, tk=2048, name="mix_in")
    q = p[:, PW:]
    o_pool, y_pool = _pool_fwd(p, W["pool_w"], pscale, name="pool_fwd")
    r_s, w_s, k_s, v_s, kap_s, b_s, g_s = _rwkv_prep(q, mu_p, wl, w0r, a0r, kkr, kar, name="rwkv_prep")
    y_scan, sa_s, states, gathered = _scan_fwd(r_s, w_s, k_s, v_s, kap_s, b_s, name="scan_fwd", plan=scan_plan)
    ffn2_w = gathered[:3]
    w_out_f = gathered[3].reshape(D, D)
    cat = _rwkv_post(y_scan, r_s, k_s, v_s, g_s, y_pool, lnw, lnb, rkr, name="rwkv_post")
    f2 = _mm(cat, w_out_f, tm=1024, tn=1024, tk=2048, name="mix_out")
    x2 = _post_norm_res(x1, f2, gq[1], gws[1], name="mix_post")

    x3, sv3, _, _ = _ffn_forward(x2, ffn2_w, gp[2], gq[2], shifts[2], sc1p[2], gws[2], "ffn")

    loss_part, dx3 = _loss_head(x3, tgt, name="loss_head")
    loss = lax.psum(loss_part[0, 0], MESH_AXES)

    def by_core_chip(blocks):
        shp = blocks.shape
        t = blocks.astype(BF16).reshape((N_DEV // 2, 2) + shp[1:])
        return jnp.swapaxes(t, 0, 1)

    def chip_sums(mine, tag):
        got = _sibling_swap(mine, name=f"{tag}_swap")
        return [_pair_add(m, g, name=f"{tag}_add{i}") for i, (m, g) in enumerate(zip(mine, got))]

    def ffn_parts(pgu, pd):
        fs = F // N_DEV
        return pgu[:, :D, :fs], pgu[:, D:, :fs], pd[:, :fs, :]

    dx2, dwgu2, dwd2, pre3, post3, got2 = _ffn_backward(dx3, x2, sv3, ffn2_w, gp[2], gq[2], sc1p[2], gws[2], "ffn",
                                                        swap_in_dh=True)
    sums2_gu = _pair_add(dwgu2, got2["dh"][0], name="scatter_ffn_add0")
    sums2_d = _pair_add(dwd2, got2["dh"][1], name="scatter_ffn_add1")

    df2, post2 = _post_norm_res_bwd(dx2, f2, gq[1], gws[1], 1.0, name="mix_post_bwd")
    dw_out = _mm(cat, df2, ta=True, tm=1024, tn=1024, tk=1024, name="mix_dwout")
    dcat = _mm(df2, w_out_f, tb=True, tm=1024, tn=1024, tk=2048, name="mix_dcat")
    dyr = dcat[:, PW:]
    dysc, dg, dr_b, dk2_b, dv_b, post_sums = _rwkv_post_bwd(dyr, y_scan, r_s, k_s, v_s, g_s, lnw, lnb, rkr,
                                                             name="rwkv_post_bwd")
    (dr, dw, dk2, dv, dkap, db), parts2 = _scan_bwd(r_s, w_s, k_s, v_s, kap_s, b_s, sa_s, dysc, states,
                                                    name="scan_bwd", plan=_ChipsPlan([sums2_gu, sums2_d]))
    dps, dwl, prep_sums = _rwkv_prep_bwd(q, mu_p, wl, w0r, a0r, kkr, kar,
                                         (dr, dw, dk2, dv, dkap, db, dg, dr_b, dk2_b, dv_b), name="rwkv_prep_bwd")
    dq, dmu = _tshift_bwd(dps, q, mu_p, name="tshift_bwd")
    du_pool, dpool_w, dpool_scale = _pool_bwd(dcat, o_pool, W["pool_w"], pscale, name="pool_bwd")
    dp = jnp.concatenate([du_pool, dq], axis=1)
    dw_in = _mm(h2, dp, ta=True, tm=1024, tn=512, tk=1024, name="mix_dwin")
    dh2 = _mm(dp, w_in_p, tb=True, tm=1024, tn=1024, tk=2816, name="mix_dh")
    dx1, pre2 = _pre_norm_mod_bwd(dh2, x1, dx2, gp[1], sc1p[1], name="mix_pre_bwd")

    sums_mix = chip_sums([by_core_chip(_cols_split(dw_in[:, :IN_W])),
                          by_core_chip(dw_out.reshape(N_DEV, D // N_DEV, D))], "scatter_mixer")
    early = [dmu[0, :W["mu_shift"].shape[0]], dpool_w, dpool_scale, prep_sums[0], prep_sums[1], prep_sums[2],
             prep_sums[3], post_sums[2], post_sums[0], post_sums[1],
             dwl[0:LORA_W, 0:R], dwl[LORA_W:LORA_W + LORA_A, R:2 * R],
             dwl[LORA_W + LORA_A:LORA_W + LORA_A + LORA_G, 2 * R:3 * R]]
    early_shapes = [a.shape for a in early]
    dx0, _, _, pre1, post1, got = _ffn_backward(
        dx1, xs, sv1, ffn1_w, gp[0], gq[0], sc1p[0], gws[0], "ffn",
        plans={"dwd": _ChipsPlan(sums_mix), "dwgu": _GatherPlan([_pack(early)])},
        own_sums=lambda blocks, part: chip_sums(blocks, f"scatter_ffn_{part}"))

    pres, posts = [pre1, pre2, pre3], [post1, post2, post3]
    dmod = jnp.stack([jnp.stack([pres[s][0], pres[s][1], posts[s][0]]) for s in range(3)]).reshape(NMOD // 128, 128)
    late = [jnp.stack([pres[s][2] for s in range(3)]), jnp.stack([posts[s][1] for s in range(3)])]
    late_shapes = [a.shape for a in late]
    dmod8, late8 = _gather_two_level([dmod, _pack(late, rows_multiple=8)], name="gather_grads")
    g_b_ada = _sum_parts(dmod8, name="sum_dmod").reshape(NMOD)
    g_npre, g_npost = _unpack(_sum_parts(late8, name="sum_late"), late_shapes)
    (g_mu, g_pool_w, g_pool_scale, g_w0, g_a0, g_kk, g_ka, g_rk, g_lnw, g_lnb, g_w2, g_a2,
     g_g2) = _unpack(_sum_parts(got["dwgu"][0], name="sum_small"), early_shapes)

    dmod_all = dmod8.reshape(N_DEV, NMOD)
    dmod_cols = lax.dynamic_slice(dmod_all, (0, me * ada_c), (N_DEV, ada_c))
    dmod_cols = jnp.concatenate([dmod_cols, jnp.zeros_like(dmod_cols)], axis=0)
    g_w_ada = _mm(sc_pad, dmod_cols, ta=True, tm=D, tn=ada_c // 9, tk=16, name="ada_bwd")

    pg2, pu2, pd2 = ffn_parts(*parts2)
    pin, pout = got["dwd"]
    pg1, pu1, pd1 = ffn_parts(got["dh"][0], got["down_bwd"][0])

    res = {}

    def big(nm, parts, tag):
        res[nm] = _adamw(W[nm], M1[nm], V1[nm], parts, name=tag)

    big("ffn1_w_gate", pg1, "adamw_cols")
    big("ffn1_w_up", pu1, "adamw_cols")
    big("ffn1_w_down", pd1, "adamw_rows")
    big("ffn2_w_gate", pg2, "adamw_cols")
    big("ffn2_w_up", pu2, "adamw_cols")
    big("ffn2_w_down", pd2, "adamw_rows")
    big("w_in", pin, "adamw_w_in")
    big("w_out", pout, "adamw_w_out")
    big("w_ada", g_w_ada[None], "adamw_w_ada")

    def my_cols(full, width):
        return lax.dynamic_slice_in_dim(full, me * width, width, axis=full.ndim - 1)

    small_names = ["b_ada", "mu_shift", "pool_w", "pool_scale", "w0", "a0", "k_k", "k_a", "r_k", "lnx_w", "lnx_b",
                   "norm_pre", "norm_post", "w2", "a2", "g2"]
    small_grads = [g_b_ada, g_mu, g_pool_w, g_pool_scale, g_w0, g_a0, g_kk, g_ka, g_rk.reshape(W["r_k"].shape), g_lnw,
                   g_lnb, my_cols(g_npre, D // N_DEV), my_cols(g_npost, D // N_DEV), my_cols(g_w2, R // N_DEV),
                   my_cols(g_a2, R // N_DEV), my_cols(g_g2, R // N_DEV)]
    shapes = [W[n].shape for n in small_names]
    packed = _adamw(_pack([W[n] for n in small_names]), _pack([M1[n] for n in small_names]),
                    _pack([V1[n] for n in small_names]), _pack(small_grads)[None], name="adamw_small")
    unpacked = [_unpack(t, shapes) for t in packed]
    for i, nm in enumerate(small_names):
        res[nm] = tuple(unpacked[k][i] for k in range(4))

    outs = [loss, dx0[None]]
    for k in range(4):
        outs.extend(res[nm][k][None] for nm in names)
    return tuple(outs)
```

```python
import functools

import jax
import jax.numpy as jnp
from jax import lax
from jax.experimental import pallas as pl
from jax.experimental.pallas import tpu as pltpu

F32 = jnp.float32
BF16 = jnp.bfloat16
N_DEV = 8
MESH_AXES = ("x", "y", "c")

NORM_EPS = 1e-6
HEAD = 64
LN_X_EPS = 1e-5 * HEAD
POOL_GROUPS = 4
POOL_GROUP = 128
MACARON = 0.5
LORA_W, LORA_A, LORA_G = 64, 64, 224
LORA_PAD = 384
ADAM_LR, ADAM_B1, ADAM_B2, ADAM_EPS, ADAM_WD, ADAM_STEP = 0.001, 0.9, 0.999, 1e-08, 0.01, 10

FF_TILE = 768
ROW_TILE = 256
SCAN_T = 64
SCAN_T_BWD = 64
SCAN_G = 6
SCAN_G_BWD = 6
VMEM_CAP = 56 * 1024 * 1024


def _cp(sem, vmem_mb):
    return pltpu.CompilerParams(dimension_semantics=sem, vmem_limit_bytes=min(vmem_mb * 1024 * 1024, VMEM_CAP))


def _my_index():
    return 4 * lax.axis_index("x") + 2 * lax.axis_index("y") + lax.axis_index("c")


def _exchange(arrays, *, scatter, name):
    n = len(arrays)
    out_shapes = []
    for a in arrays:
        shp = a.shape if scatter else (N_DEV,) + a.shape
        out_shapes.append(jax.ShapeDtypeStruct(shp, a.dtype))

    def body(*refs):
        ins, outs = refs[:n], refs[n:2 * n]
        send_sems, recv_sems, local_sems = refs[2 * n:]
        me = _my_index()

        def dev(p):
            return (p // 4, (p // 2) % 2, p % 2)

        def copy(i, d):
            peer = (me + d) % N_DEV
            src = ins[i].at[peer] if scatter else ins[i]
            return pltpu.make_async_remote_copy(
                src_ref=src, dst_ref=outs[i].at[me], send_sem=send_sems.at[i, d - 1],
                recv_sem=recv_sems.at[i, d - 1], device_id=dev(peer), device_id_type=pl.DeviceIdType.MESH)

        def arrival(i, d):
            frm = (me + N_DEV - d) % N_DEV
            src = ins[i].at[frm] if scatter else ins[i]
            return pltpu.make_async_remote_copy(
                src_ref=src, dst_ref=outs[i].at[frm], send_sem=send_sems.at[i, d - 1],
                recv_sem=recv_sems.at[i, d - 1], device_id=dev(frm), device_id_type=pl.DeviceIdType.MESH)

        locals_ = []
        for i in range(n):
            src = ins[i].at[me] if scatter else ins[i]
            lc = pltpu.make_async_copy(src, outs[i].at[me], local_sems.at[i])
            lc.start()
            locals_.append(lc)
        sends = [copy(i, d) for d in range(1, N_DEV) for i in range(n)]
        for cp in sends:
            cp.start()
        for d in range(1, N_DEV):
            for i in range(n):
                arrival(i, d).wait_recv()
        for cp in sends:
            cp.wait_send()
        for lc in locals_:
            lc.wait()

    hbm = pl.BlockSpec(memory_space=pltpu.HBM)
    return pl.pallas_call(
        body, name=name, out_shape=tuple(out_shapes), in_specs=[hbm] * n, out_specs=tuple([hbm] * n),
        scratch_shapes=[pltpu.SemaphoreType.DMA((n, N_DEV - 1)), pltpu.SemaphoreType.DMA((n, N_DEV - 1)),
                        pltpu.SemaphoreType.DMA((n,))],
    )(*arrays)


def _remote(src, dst, send_sem, recv_sem, to):
    return pltpu.make_async_remote_copy(src_ref=src, dst_ref=dst, send_sem=send_sem, recv_sem=recv_sem,
                                        device_id=to, device_id_type=pl.DeviceIdType.MESH)


class _GatherPlan:
    def __init__(self, arrays):
        self.arrays = list(arrays)
        self.n = len(arrays)
        self.out_shapes = [jax.ShapeDtypeStruct((N_DEV,) + a.shape, a.dtype) for a in arrays]
        self.scratch = [pltpu.SemaphoreType.DMA((self.n, 7)), pltpu.SemaphoreType.DMA((self.n, 7)),
                        pltpu.SemaphoreType.DMA((self.n,))]

    def _parts(self, ins, outs, sems):
        send_sems, recv_sems, local_sems = sems
        x, y, c = lax.axis_index("x"), lax.axis_index("y"), lax.axis_index("c")
        chips = [(1 - x, y), (x, 1 - y), (1 - x, 1 - y)]

        def slot(i, px, py, pc):
            return outs[i].at[4 * px + 2 * py + pc]

        def copy(i, k, block, to, src=None):
            dst = slot(i, *block)
            return _remote(dst if src is None else src, dst, send_sems.at[i, k], recv_sems.at[i, k], to)

        n = self.n
        locals_ = [pltpu.make_async_copy(ins[i], slot(i, x, y, c), local_sems.at[i]) for i in range(n)]
        first = [copy(i, 1 + j, (x, y, c), (*chip, c), src=ins[i]) for j, chip in enumerate(chips) for i in range(n)]
        first += [copy(i, 0, (x, y, c), (x, y, 1 - c), src=ins[i]) for i in range(n)]
        return (x, y, c), chips, copy, locals_, first

    def start(self, ins, outs, sems):
        _, _, _, locals_, first = self._parts(ins, outs, sems)
        for lc in locals_:
            lc.start()
        for cp in first:
            cp.start()

    def finish(self, ins, outs, sems):
        (x, y, c), chips, copy, locals_, first = self._parts(ins, outs, sems)
        forwards = []
        for j, chip in enumerate(chips):
            for i in range(self.n):
                copy(i, 1 + j, (*chip, c), (x, y, c)).wait_recv()
                fwd = copy(i, 4 + j, (*chip, c), (x, y, 1 - c))
                fwd.start()
                forwards.append(fwd)
        for i in range(self.n):
            copy(i, 0, (x, y, 1 - c), (x, y, c)).wait_recv()
        for j, chip in enumerate(chips):
            for i in range(self.n):
                copy(i, 4 + j, (*chip, 1 - c), (x, y, c)).wait_recv()
        for cp in first + forwards:
            cp.wait_send()
        for lc in locals_:
            lc.wait()


class _ChipsPlan:
    def __init__(self, arrays):
        self.arrays = list(arrays)
        self.n = len(arrays)
        self.out_shapes = [jax.ShapeDtypeStruct(a.shape, a.dtype) for a in arrays]
        self.scratch = [pltpu.SemaphoreType.DMA((self.n, 3)), pltpu.SemaphoreType.DMA((self.n, 3)),
                        pltpu.SemaphoreType.DMA((self.n,))]

    def _parts(self, ins, outs, sems):
        send_sems, recv_sems, local_sems = sems
        x, y, c = lax.axis_index("x"), lax.axis_index("y"), lax.axis_index("c")
        mine = 2 * x + y
        chips = [(1 - x, y), (x, 1 - y), (1 - x, 1 - y)]
        n = self.n
        locals_ = [pltpu.make_async_copy(ins[i].at[mine], outs[i].at[mine], local_sems.at[i]) for i in range(n)]
        sends = [_remote(ins[i].at[2 * chip[0] + chip[1]], outs[i].at[mine], send_sems.at[i, j], recv_sems.at[i, j],
                         (*chip, c)) for j, chip in enumerate(chips) for i in range(n)]

        def arrivals():
            return [_remote(ins[i].at[2 * chip[0] + chip[1]], outs[i].at[2 * chip[0] + chip[1]], send_sems.at[i, j],
                            recv_sems.at[i, j], (*chip, c)) for j, chip in enumerate(chips) for i in range(n)]

        return locals_, sends, arrivals

    def start(self, ins, outs, sems):
        locals_, sends, _ = self._parts(ins, outs, sems)
        for lc in locals_:
            lc.start()
        for cp in sends:
            cp.start()

    def finish(self, ins, outs, sems):
        locals_, sends, arrivals = self._parts(ins, outs, sems)
        for cp in arrivals():
            cp.wait_recv()
        for cp in sends:
            cp.wait_send()
        for lc in locals_:
            lc.wait()


def _run_plan(plan, *, name):
    n = plan.n

    def body(*refs):
        ins, outs, sems = refs[:n], refs[n:2 * n], refs[2 * n:]
        plan.start(ins, outs, sems)
        plan.finish(ins, outs, sems)

    hbm = pl.BlockSpec(memory_space=pltpu.HBM)
    return pl.pallas_call(
        body, name=name, out_shape=tuple(plan.out_shapes), in_specs=[hbm] * n, out_specs=tuple([hbm] * n),
        scratch_shapes=plan.scratch,
    )(*plan.arrays)


def _host(body, plan, n_in, n_out, is_first, is_last):
    if plan is None:
        return body
    m = plan.n

    def wrapped(*refs):
        a, b = n_in, n_in + m
        c, d = b + n_out, b + n_out + m
        own_in, c_in, own_out, c_out, rest = refs[:a], refs[a:b], refs[b:c], refs[c:d], refs[d:]
        n_sems = len(plan.scratch)
        own_scr, c_sems = rest[:len(rest) - n_sems], rest[len(rest) - n_sems:]

        @pl.when(is_first())
        def _():
            plan.start(c_in, c_out, c_sems)

        body(*own_in, *own_out, *own_scr)

        @pl.when(is_last())
        def _():
            plan.finish(c_in, c_out, c_sems)

    return wrapped


def _host_args(plan):
    if plan is None:
        return [], [], [], [], []
    hbm = pl.BlockSpec(memory_space=pltpu.HBM)
    return [hbm] * plan.n, [hbm] * plan.n, list(plan.out_shapes), list(plan.scratch), list(plan.arrays)


def _gather_two_level(arrays, *, name):
    return _run_plan(_GatherPlan(arrays), name=name)


class _SwapPlan:
    def __init__(self, arrays):
        self.arrays = list(arrays)
        self.n = len(arrays)
        self.out_shapes = [jax.ShapeDtypeStruct(a.shape[1:], a.dtype) for a in arrays]
        self.scratch = [pltpu.SemaphoreType.DMA((self.n,)), pltpu.SemaphoreType.DMA((self.n,))]

    def _copies(self, ins, outs, sems):
        send_sems, recv_sems = sems
        x, y, c = lax.axis_index("x"), lax.axis_index("y"), lax.axis_index("c")
        return [_remote(ins[i].at[1 - c], outs[i], send_sems.at[i], recv_sems.at[i], (x, y, 1 - c))
                for i in range(self.n)]

    def start(self, ins, outs, sems):
        for cp in self._copies(ins, outs, sems):
            cp.start()

    def finish(self, ins, outs, sems):
        copies = self._copies(ins, outs, sems)
        for cp in copies:
            cp.wait_recv()
        for cp in copies:
            cp.wait_send()


def _sibling_swap(arrays, *, name):
    return _run_plan(_SwapPlan(arrays), name=name)


def _chips_all_to_all(arrays, *, name):
    return _run_plan(_ChipsPlan(arrays), name=name)


def _pair_add(mine, got, *, name):
    _, nq, R, C = mine.shape
    tr = R
    for cand in (512, 256, 128, 64, 32, 16):
        if R % cand == 0 and cand * C * 2 * 3 * 2 <= 12 * 1024 * 1024:
            tr = cand
            break

    def body(core_ref, m_ref, g_ref, o_ref):
        o_ref[0] = (m_ref[0, 0].astype(F32) + g_ref[0].astype(F32)).astype(BF16)

    core = lax.axis_index("c").astype(jnp.int32).reshape(1)
    return pl.pallas_call(
        body, name=name,
        grid_spec=pltpu.PrefetchScalarGridSpec(
            num_scalar_prefetch=1, grid=(nq, R // tr),
            in_specs=[pl.BlockSpec((1, 1, tr, C), lambda q, i, core_ref: (core_ref[0], q, i, 0)),
                      pl.BlockSpec((1, tr, C), lambda q, i, core_ref: (q, i, 0))],
            out_specs=pl.BlockSpec((1, tr, C), lambda q, i, core_ref: (q, i, 0))),
        out_shape=jax.ShapeDtypeStruct((nq, R, C), BF16),
        compiler_params=_cp(("parallel", "parallel"), 40),
    )(core, mine, got)


def _mm(a, b, *, ta=False, tb=False, tm, tn, tk, out_dtype=F32, name, plan=None):
    M = a.shape[1] if ta else a.shape[0]
    K = a.shape[0] if ta else a.shape[1]
    N = b.shape[0] if tb else b.shape[1]
    tm, tn, tk = min(tm, M), min(tn, N), min(tk, K)
    assert M % tm == 0 and N % tn == 0 and K % tk == 0, (name, M, N, K, tm, tn, tk)
    nk = K // tk
    dims = (((0 if ta else 1,), (1 if tb else 0,)), ((), ()))

    def body(a_ref, b_ref, o_ref, acc_ref):
        k = pl.program_id(2)

        @pl.when(k == 0)
        def _():
            acc_ref[...] = jnp.zeros_like(acc_ref)

        acc_ref[...] += lax.dot_general(a_ref[...].astype(BF16), b_ref[...].astype(BF16), dims,
                                        preferred_element_type=F32)

        @pl.when(k == nk - 1)
        def _():
            o_ref[...] = acc_ref[...].astype(out_dtype)

    a_spec = pl.BlockSpec((tk, tm), lambda i, j, k: (k, i)) if ta else pl.BlockSpec((tm, tk), lambda i, j, k: (i, k))
    b_spec = pl.BlockSpec((tn, tk), lambda i, j, k: (j, k)) if tb else pl.BlockSpec((tk, tn), lambda i, j, k: (k, j))
    blk = 2 * (tm * tk * a.dtype.itemsize + tk * tn * b.dtype.itemsize + tm * tn * jnp.dtype(out_dtype).itemsize)
    grid = (M // tm, N // tn, nk)
    x_in, x_out, x_shapes, x_scr, x_ops = _host_args(plan)
    outs = pl.pallas_call(
        _host(body, plan, 2, 1, *_grid_ends(grid)), name=name, grid=grid, in_specs=[a_spec, b_spec] + x_in,
        out_specs=tuple([pl.BlockSpec((tm, tn), lambda i, j, k: (i, j))] + x_out),
        out_shape=tuple([jax.ShapeDtypeStruct((M, N), out_dtype)] + x_shapes),
        scratch_shapes=[pltpu.VMEM((tm, tn), F32)] + x_scr,
        compiler_params=_cp(_semantics(plan, ("parallel", "parallel", "arbitrary")),
                            (blk + tm * tn * 4) // (1024 * 1024) + 12),
    )(a, b, *x_ops)
    return outs[0] if plan is None else (outs[0], tuple(outs[1:]))


def _grid_ends(grid):
    def is_first():
        ok = pl.program_id(0) == 0
        for ax in range(1, len(grid)):
            ok = ok & (pl.program_id(ax) == 0)
        return ok

    def is_last():
        ok = pl.program_id(0) == grid[0] - 1
        for ax in range(1, len(grid)):
            ok = ok & (pl.program_id(ax) == grid[ax] - 1)
        return ok

    return is_first, is_last


def _semantics(plan, sem):
    return sem if plan is None else tuple("arbitrary" for _ in sem)


def _ffn_up(h, g8, u8, *, tm, tk, name, plan=None):
    S, D = h.shape
    nb, _, tn = g8.shape
    tm = min(tm, S)
    tk = min(tk, D)
    nk = D // tk

    def body(h_ref, g_ref, u_ref, au_ref, s_ref, acc_ref):
        k = pl.program_id(2)

        @pl.when(k == 0)
        def _():
            acc_ref[...] = jnp.zeros_like(acc_ref)

        hv = h_ref[...]
        acc_ref[:, :tn] += jnp.dot(hv, g_ref[0], preferred_element_type=F32)
        acc_ref[:, tn:] += jnp.dot(hv, u_ref[0], preferred_element_type=F32)

        @pl.when(k == nk - 1)
        def _():
            acc = acc_ref[...]
            a = acc[:, :tn]
            u = acc[:, tn:]
            au_ref[...] = acc.astype(BF16)
            s_ref[...] = (a * jax.nn.sigmoid(a) * u).astype(BF16)

    wspec = pl.BlockSpec((1, tk, tn), lambda i, j, k: (j, k, 0))
    grid = (S // tm, nb, nk)
    x_in, x_out, x_shapes, x_scr, x_ops = _host_args(plan)
    outs = pl.pallas_call(
        _host(body, plan, 3, 2, *_grid_ends(grid)), name=name, grid=grid,
        in_specs=[pl.BlockSpec((tm, tk), lambda i, j, k: (i, k)), wspec, wspec] + x_in,
        out_specs=tuple([pl.BlockSpec((tm, 2 * tn), lambda i, j, k: (i, j)),
                         pl.BlockSpec((tm, tn), lambda i, j, k: (i, j))] + x_out),
        out_shape=tuple([jax.ShapeDtypeStruct((S, 2 * nb * tn), BF16), jax.ShapeDtypeStruct((S, nb * tn), BF16)]
                        + x_shapes),
        scratch_shapes=[pltpu.VMEM((tm, 2 * tn), F32)] + x_scr,
        compiler_params=_cp(_semantics(plan, ("parallel", "parallel", "arbitrary")), 52),
    )(h, g8, u8, *x_ops)
    return outs[0], outs[1], tuple(outs[2:])


def _ffn_dh(dau, g8, u8, *, tm, tn, name, plan=None):
    S = dau.shape[0]
    nb, D, tf = g8.shape
    tm, tn = min(tm, S), min(tn, D)
    nk = 2 * nb
    nt = (((1,), (1,)), ((), ()))

    def body(a_ref, g_ref, u_ref, o_ref, acc_ref):
        k = pl.program_id(2)

        @pl.when(k == 0)
        def _():
            acc_ref[...] = jnp.zeros_like(acc_ref)

        @pl.when(k % 2 == 0)
        def _():
            acc_ref[...] += lax.dot_general(a_ref[...], g_ref[0], nt, preferred_element_type=F32)

        @pl.when(k % 2 == 1)
        def _():
            acc_ref[...] += lax.dot_general(a_ref[...], u_ref[0], nt, preferred_element_type=F32)

        @pl.when(k == nk - 1)
        def _():
            o_ref[...] = acc_ref[...]

    wspec = pl.BlockSpec((1, tn, tf), lambda i, n, k: (k // 2, n, 0))
    grid = (S // tm, D // tn, nk)
    x_in, x_out, x_shapes, x_scr, x_ops = _host_args(plan)
    outs = pl.pallas_call(
        _host(body, plan, 3, 1, *_grid_ends(grid)), name=name, grid=grid,
        in_specs=[pl.BlockSpec((tm, tf), lambda i, n, k: (i, k)), wspec, wspec] + x_in,
        out_specs=tuple([pl.BlockSpec((tm, tn), lambda i, n, k: (i, n))] + x_out),
        out_shape=tuple([jax.ShapeDtypeStruct((S, D), F32)] + x_shapes),
        scratch_shapes=[pltpu.VMEM((tm, tn), F32)] + x_scr,
        compiler_params=_cp(_semantics(plan, ("parallel", "parallel", "arbitrary")), 40),
    )(dau, g8, u8, *x_ops)
    return outs[0], tuple(outs[1:])


def _ffn_dwgu(h, dau, *, tm, tk, name, plan=None):
    S, D = h.shape
    tf = FF_TILE
    nt = dau.shape[1] // tf
    tm, tk = min(tm, D), min(tk, S)
    nk = S // tk
    ni = D // tm

    def body(a_ref, b_ref, o_ref, acc_ref):
        k = pl.program_id(2)

        @pl.when(k == 0)
        def _():
            acc_ref[...] = jnp.zeros_like(acc_ref)

        acc_ref[...] += lax.dot_general(a_ref[...], b_ref[...], (((0,), (0,)), ((), ())), preferred_element_type=F32)

        @pl.when(k == nk - 1)
        def _():
            o_ref[0, 0] = acc_ref[...].astype(BF16)

    grid = (ni, nt, nk)
    x_in, x_out, x_shapes, x_scr, x_ops = _host_args(plan)
    outs = pl.pallas_call(
        _host(body, plan, 2, 1, *_grid_ends(grid)), name=name, grid=grid,
        in_specs=[pl.BlockSpec((tk, tm), lambda i, j, k: (k, i)), pl.BlockSpec((tk, tf), lambda i, j, k: (k, j))] + x_in,
        out_specs=tuple([pl.BlockSpec((1, 1, tm, tf), lambda i, j, k: ((j // 2) % 2, j // 4, (j % 2) * ni + i, 0))]
                        + x_out),
        out_shape=tuple([jax.ShapeDtypeStruct((2, nt // 4, 2 * D, tf), BF16)] + x_shapes),
        scratch_shapes=[pltpu.VMEM((tm, tf), F32)] + x_scr,
        compiler_params=_cp(_semantics(plan, ("parallel", "parallel", "arbitrary")), 40),
    )(h, dau, *x_ops)
    return outs[0], tuple(outs[1:])


def _ffn_dwd(s, df, *, tn, tk, name, plan=None):
    S, D = df.shape
    tf = FF_TILE
    nb = s.shape[1] // tf
    tn, tk = min(tn, D), min(tk, S)
    nk = S // tk

    def body(a_ref, b_ref, o_ref, acc_ref):
        k = pl.program_id(2)

        @pl.when(k == 0)
        def _():
            acc_ref[...] = jnp.zeros_like(acc_ref)

        acc_ref[...] += lax.dot_general(a_ref[...], b_ref[...], (((0,), (0,)), ((), ())), preferred_element_type=F32)

        @pl.when(k == nk - 1)
        def _():
            o_ref[0, 0] = acc_ref[...].astype(BF16)

    grid = (nb, D // tn, nk)
    x_in, x_out, x_shapes, x_scr, x_ops = _host_args(plan)
    outs = pl.pallas_call(
        _host(body, plan, 2, 1, *_grid_ends(grid)), name=name, grid=grid,
        in_specs=[pl.BlockSpec((tk, tf), lambda j, n, k: (k, j)), pl.BlockSpec((tk, tn), lambda j, n, k: (k, n))] + x_in,
        out_specs=tuple([pl.BlockSpec((1, 1, tf, tn), lambda j, n, k: (j % 2, j // 2, 0, n))] + x_out),
        out_shape=tuple([jax.ShapeDtypeStruct((2, nb // 2, tf, D), BF16)] + x_shapes),
        scratch_shapes=[pltpu.VMEM((tf, tn), F32)] + x_scr,
        compiler_params=_cp(_semantics(plan, ("parallel", "parallel", "arbitrary")), 40),
    )(s, df, *x_ops)
    return outs[0], tuple(outs[1:])


def _ffn_down_bwd(df, d8, au, *, tm, tk, name, plan=None):
    S, D = df.shape
    nb, tn, _ = d8.shape
    F = nb * tn
    tm = min(tm, S)
    tk = min(tk, D)
    nk = D // tk

    def body(df_ref, w_ref, au_ref, dau_ref, acc_ref):
        k = pl.program_id(2)

        @pl.when(k == 0)
        def _():
            acc_ref[...] = jnp.zeros_like(acc_ref)

        acc_ref[...] += lax.dot_general(df_ref[...], w_ref[0], (((1,), (1,)), ((), ())), preferred_element_type=F32)

        @pl.when(k == nk - 1)
        def _():
            ds = acc_ref[...]
            au_v = au_ref[...].astype(F32)
            a = au_v[:, :tn]
            u = au_v[:, tn:]
            sg = jax.nn.sigmoid(a)
            da = ds * u * (sg * (1.0 + a * (1.0 - sg)))
            du = ds * (a * sg)
            dau_ref[:, :tn] = da.astype(BF16)
            dau_ref[:, tn:] = du.astype(BF16)

    grid = (S // tm, F // tn, nk)
    x_in, x_out, x_shapes, x_scr, x_ops = _host_args(plan)
    outs = pl.pallas_call(
        _host(body, plan, 3, 1, *_grid_ends(grid)), name=name, grid=grid,
        in_specs=[pl.BlockSpec((tm, tk), lambda i, j, k: (i, k)), pl.BlockSpec((1, tn, tk), lambda i, j, k: (j, 0, k)),
                  pl.BlockSpec((tm, 2 * tn), lambda i, j, k: (i, j))] + x_in,
        out_specs=tuple([pl.BlockSpec((tm, 2 * tn), lambda i, j, k: (i, j))] + x_out),
        out_shape=tuple([jax.ShapeDtypeStruct((S, 2 * F), BF16)] + x_shapes),
        scratch_shapes=[pltpu.VMEM((tm, tn), F32)] + x_scr,
        compiler_params=_cp(_semantics(plan, ("parallel", "parallel", "arbitrary")), 52),
    )(df, d8, au, *x_ops)
    return outs[0], tuple(outs[1:])


def _fold8(x):
    tm, w = x.shape
    return jnp.sum(x.reshape(tm // 8, 8, w), axis=0)


def _row_spec(tm, w):
    return pl.BlockSpec((tm, w), lambda i: (i, 0))


def _vec_spec(rows, w):
    return pl.BlockSpec((rows, w), lambda i: (0, 0))


def _pre_norm_mod(x, gain, shift, scale1p, *, name):
    S, D = x.shape
    tm = ROW_TILE

    def body(x_ref, g_ref, sh_ref, sc_ref, h_ref):
        xv = x_ref[...]
        rinv = lax.rsqrt(jnp.mean(xv * xv, axis=-1, keepdims=True) + NORM_EPS)
        h_ref[...] = ((xv * rinv) * g_ref[...] * sc_ref[...] + sh_ref[...]).astype(BF16)

    return pl.pallas_call(
        body, name=name, grid=(S // tm,),
        in_specs=[_row_spec(tm, D), _vec_spec(1, D), _vec_spec(1, D), _vec_spec(1, D)],
        out_specs=_row_spec(tm, D), out_shape=jax.ShapeDtypeStruct((S, D), BF16),
        compiler_params=_cp(("parallel",), 32),
    )(x, gain, shift, scale1p)


def _pre_norm_mod_bwd(dh, x, dres, gain, scale1p, *, name):
    S, D = x.shape
    tm = ROW_TILE
    n = S // tm

    def body(dh_ref, x_ref, dr_ref, g_ref, sc_ref, dx_ref, sums_ref, acc_ref):
        i = pl.program_id(0)

        @pl.when(i == 0)
        def _():
            acc_ref[...] = jnp.zeros_like(acc_ref)

        xv = x_ref[...]
        dhv = dh_ref[...]
        g = g_ref[...]
        rinv = lax.rsqrt(jnp.mean(xv * xv, axis=-1, keepdims=True) + NORM_EPS)
        xn = xv * rinv
        dn = dhv * sc_ref[...]
        dxn = dn * g
        dx_ref[...] = dr_ref[...] + rinv * (dxn - xn * jnp.mean(dxn * xn, axis=-1, keepdims=True))
        acc_ref[0] += _fold8(dhv)
        acc_ref[1] += _fold8(dhv * (xn * g))
        acc_ref[2] += _fold8(dn * xn)

        @pl.when(i == n - 1)
        def _():
            for q in range(3):
                sums_ref[q:q + 1, :] = jnp.sum(acc_ref[q], axis=0, keepdims=True)

    return pl.pallas_call(
        body, name=name, grid=(n,),
        in_specs=[_row_spec(tm, D), _row_spec(tm, D), _row_spec(tm, D), _vec_spec(1, D), _vec_spec(1, D)],
        out_specs=(_row_spec(tm, D), _vec_spec(3, D)),
        out_shape=(jax.ShapeDtypeStruct((S, D), F32), jax.ShapeDtypeStruct((3, D), F32)),
        scratch_shapes=[pltpu.VMEM((3, 8, D), F32)],
        compiler_params=_cp(("arbitrary",), 40),
    )(dh, x, dres, gain, scale1p)


def _post_norm_res(x, f, gain, gw, *, name):
    S, D = x.shape
    tm = ROW_TILE

    def body(x_ref, f_ref, g_ref, gw_ref, o_ref):
        fv = f_ref[...]
        rinv = lax.rsqrt(jnp.mean(fv * fv, axis=-1, keepdims=True) + NORM_EPS)
        o_ref[...] = x_ref[...] + gw_ref[...] * ((fv * rinv) * g_ref[...])

    return pl.pallas_call(
        body, name=name, grid=(S // tm,),
        in_specs=[_row_spec(tm, D), _row_spec(tm, D), _vec_spec(1, D), _vec_spec(1, D)],
        out_specs=_row_spec(tm, D), out_shape=jax.ShapeDtypeStruct((S, D), F32),
        compiler_params=_cp(("parallel",), 32),
    )(x, f, gain, gw)


def _post_norm_res_bwd(dxo, f, gain, gw, weight, *, name):
    S, D = f.shape
    tm = ROW_TILE
    n = S // tm

    def body(d_ref, f_ref, g_ref, gw_ref, df_ref, sums_ref, acc_ref):
        i = pl.program_id(0)

        @pl.when(i == 0)
        def _():
            acc_ref[...] = jnp.zeros_like(acc_ref)

        fv = f_ref[...]
        dv = d_ref[...]
        g = g_ref[...]
        rinv = lax.rsqrt(jnp.mean(fv * fv, axis=-1, keepdims=True) + NORM_EPS)
        fh = fv * rinv
        dy = dv * gw_ref[...]
        dfh = dy * g
        df_ref[...] = (rinv * (dfh - fh * jnp.mean(dfh * fh, axis=-1, keepdims=True))).astype(BF16)
        acc_ref[0] += _fold8(weight * dv * (fh * g))
        acc_ref[1] += _fold8(dy * fh)

        @pl.when(i == n - 1)
        def _():
            for q in range(2):
                sums_ref[q:q + 1, :] = jnp.sum(acc_ref[q], axis=0, keepdims=True)

    return pl.pallas_call(
        body, name=name, grid=(n,),
        in_specs=[_row_spec(tm, D), _row_spec(tm, D), _vec_spec(1, D), _vec_spec(1, D)],
        out_specs=(_row_spec(tm, D), _vec_spec(2, D)),
        out_shape=(jax.ShapeDtypeStruct((S, D), BF16), jax.ShapeDtypeStruct((2, D), F32)),
        scratch_shapes=[pltpu.VMEM((2, 8, D), F32)],
        compiler_params=_cp(("arbitrary",), 40),
    )(dxo, f, gain, gw)


def _loss_head(y, target, *, name):
    S, D = y.shape
    tm = ROW_TILE

    def body(y_ref, t_ref, l_ref, dy_ref):
        i = pl.program_id(0)

        @pl.when(i == 0)
        def _():
            l_ref[...] = jnp.zeros_like(l_ref)

        err = y_ref[...] - t_ref[...]
        dy_ref[...] = err * (1.0 / D)
        row = jnp.sum(err * err, axis=-1, keepdims=True) * (0.5 / D)
        l_ref[...] += jnp.sum(row, axis=0, keepdims=True)

    return pl.pallas_call(
        body, name=name, grid=(S // tm,),
        in_specs=[_row_spec(tm, D), _row_spec(tm, D)],
        out_specs=(_vec_spec(1, 1), _row_spec(tm, D)),
        out_shape=(jax.ShapeDtypeStruct((1, 1), F32), jax.ShapeDtypeStruct((S, D), F32)),
        compiler_params=_cp(("arbitrary",), 32),
    )(y, target)


def _shift_down(z, j, row):
    return jnp.where(row >= j, pltpu.roll(z, j, 0), 0.0)


def _shift_up(z, j, row, n):
    return jnp.where(row < n - j, pltpu.roll(z, n - j, 0), 0.0)


def _pool_fwd(p, pool_w, pool_scale, *, name):
    S = p.shape[0]
    C = POOL_GROUP

    def body(u_ref, w_ref, sc_ref, o_ref, y_ref):
        g = pl.program_id(0)
        u = u_ref[...]
        row = lax.broadcasted_iota(jnp.int32, (S, C), 0)
        s1 = u + _shift_down(u, 1, row)
        s2 = s1 + _shift_down(s1, 2, row)
        s3 = s2 + _shift_down(s2, 4, row)
        s4 = s3 + _shift_down(s3, 8, row)
        gi = jnp.zeros((S, C), jnp.int32) + g
        win = jnp.where(gi == 0, s1, jnp.where(gi == 1, s2, jnp.where(gi == 2, s3, s4)))
        width = jnp.where(gi == 0, 2, jnp.where(gi == 1, 4, jnp.where(gi == 2, 8, 16)))
        count = jnp.minimum(row + 1, width).astype(F32)
        o = win / count - u
        o_ref[...] = o
        y_ref[...] = jnp.dot(o.astype(BF16), w_ref[0].astype(BF16), preferred_element_type=F32) * sc_ref[...]

    col = pl.BlockSpec((S, C), lambda g: (0, g))
    return pl.pallas_call(
        body, name=name, grid=(POOL_GROUPS,),
        in_specs=[col, pl.BlockSpec((1, C, C), lambda g: (g, 0, 0)), pl.BlockSpec((1, C), lambda g: (0, g))],
        out_specs=(col, col),
        out_shape=(jax.ShapeDtypeStruct((S, POOL_GROUPS * C), F32), jax.ShapeDtypeStruct((S, POOL_GROUPS * C), F32)),
        compiler_params=_cp(("parallel",), 48),
    )(p, pool_w, pool_scale)


def _pool_bwd(dcat, o, pool_w, pool_scale, *, name):
    S = o.shape[0]
    C = POOL_GROUP

    def body(dy_ref, o_ref, w_ref, sc_ref, du_ref, dw_ref, dsc_ref):
        g = pl.program_id(0)
        dy = dy_ref[...]
        ob = o_ref[...].astype(BF16)
        wb = w_ref[0].astype(BF16)
        mixed = jnp.dot(ob, wb, preferred_element_type=F32)
        dsc_ref[...] = jnp.sum(_fold8(dy * mixed), axis=0, keepdims=True)
        dmix = (dy * sc_ref[...]).astype(BF16)
        dw_ref[0] = lax.dot_general(ob, dmix, (((0,), (0,)), ((), ())), preferred_element_type=F32)
        do = lax.dot_general(dmix, wb, (((1,), (1,)), ((), ())), preferred_element_type=F32)
        row = lax.broadcasted_iota(jnp.int32, (S, C), 0)
        gi = jnp.zeros((S, C), jnp.int32) + g
        width = jnp.where(gi == 0, 2, jnp.where(gi == 1, 4, jnp.where(gi == 2, 8, 16)))
        z = do / jnp.minimum(row + 1, width).astype(F32)
        s1 = z + _shift_up(z, 1, row, S)
        s2 = s1 + _shift_up(s1, 2, row, S)
        s3 = s2 + _shift_up(s2, 4, row, S)
        s4 = s3 + _shift_up(s3, 8, row, S)
        win = jnp.where(gi == 0, s1, jnp.where(gi == 1, s2, jnp.where(gi == 2, s3, s4)))
        du_ref[...] = (win - do).astype(BF16)

    col = pl.BlockSpec((S, C), lambda g: (0, g))
    return pl.pallas_call(
        body, name=name, grid=(POOL_GROUPS,),
        in_specs=[col, col, pl.BlockSpec((1, C, C), lambda g: (g, 0, 0)), pl.BlockSpec((1, C), lambda g: (0, g))],
        out_specs=(col, pl.BlockSpec((1, C, C), lambda g: (g, 0, 0)), pl.BlockSpec((1, C), lambda g: (0, g))),
        out_shape=(jax.ShapeDtypeStruct((S, POOL_GROUPS * C), BF16), jax.ShapeDtypeStruct((POOL_GROUPS, C, C), F32),
                   jax.ShapeDtypeStruct((1, POOL_GROUPS * C), F32)),
        compiler_params=_cp(("parallel",), 48),
    )(dcat, o, pool_w, pool_scale)


def _block_ones():
    r = lax.broadcasted_iota(jnp.int32, (128, 128), 0) // HEAD
    c = lax.broadcasted_iota(jnp.int32, (128, 128), 1) // HEAD
    return jnp.where(r == c, 1.0, 0.0).astype(BF16)


def _segsum(x, bd):
    outs = []
    for j in range(x.shape[1] // 128):
        xs = x[:, j * 128:(j + 1) * 128]
        hi = xs.astype(BF16)
        lo = (xs - hi.astype(F32)).astype(BF16)
        outs.append(jnp.dot(hi, bd, preferred_element_type=F32) + jnp.dot(lo, bd, preferred_element_type=F32))
    return jnp.concatenate(outs, axis=1)


def _prep_common(q, qprev, first, mu, wl, w0, a0, kkw, kaw, R):
    tm = q.shape[0]
    row = lax.broadcasted_iota(jnp.int32, q.shape, 0)
    last = qprev[7:8, :] * first
    prev = jnp.where(row == 0, last, pltpu.roll(q, 1, 0))
    ps = q + mu * (prev - q)
    r = ps[:, 0:R]
    k = ps[:, R:2 * R]
    v = ps[:, 2 * R:3 * R]
    lo_in = ps[:, 3 * R:3 * R + LORA_PAD]
    lane = lax.broadcasted_iota(jnp.int32, (tm, LORA_PAD), 1)
    m_w = lane < LORA_W
    m_a = lane < LORA_W + LORA_A
    m_g = lane < LORA_W + LORA_A + LORA_G
    act = jnp.where(m_w, jnp.tanh(lo_in), jnp.where(m_a, lo_in, jnp.where(m_g, jax.nn.sigmoid(lo_in), 0.0)))
    lo = jnp.dot(act.astype(BF16), wl, preferred_element_type=F32)
    wpre = w0 + lo[:, 0:R]
    apre = a0 + lo[:, R:2 * R]
    g = lo[:, 2 * R:3 * R]
    neg = -wpre
    softplus = jnp.maximum(neg, 0.0) + jnp.log(1.0 + jnp.exp(-jnp.abs(neg)))
    wlog = -softplus - 0.5
    ew = jnp.exp(wlog)
    decay = jnp.exp(-ew)
    a = jax.nn.sigmoid(apre)
    kk = k * kkw
    bd = _block_ones()
    n2 = _segsum(kk * kk, bd)
    nrm = jnp.maximum(jnp.sqrt(n2), 1e-12)
    kap = kk / nrm
    kmul = 1.0 + (a - 1.0) * kaw
    k2 = k * kmul
    return dict(prev=prev, r=r, k=k, v=v, act=act, m_w=m_w, m_a=m_a, m_g=m_g, wpre=wpre, g=g, ew=ew, decay=decay,
                a=a, n2=n2, nrm=nrm, kap=kap, kmul=kmul, k2=k2, bd=bd)


def _prev_rows_spec(tm, w):
    return pl.BlockSpec((8, w), lambda i: (jnp.maximum(i * (tm // 8) - 1, 0), 0))


def _rwkv_prep(q, mu, wl, w0, a0, kkw, kaw, *, name):
    S, QW = q.shape
    R = w0.shape[1]
    tm = ROW_TILE // 2

    def body(q_ref, qp_ref, mu_ref, wl_ref, w0_ref, a0_ref, kk_ref, ka_ref, r_ref, w_ref, k_ref, v_ref, kap_ref,
             b_ref, g_ref):
        first = jnp.where(pl.program_id(0) > 0, 1.0, 0.0)
        t = _prep_common(q_ref[...], qp_ref[...], first, mu_ref[...], wl_ref[...], w0_ref[...], a0_ref[...],
                         kk_ref[...], ka_ref[...], R)
        r_ref[...] = t["r"]
        w_ref[...] = t["decay"]
        k_ref[...] = t["k2"]
        v_ref[...] = t["v"]
        kap_ref[...] = t["kap"]
        b_ref[...] = t["kap"] * t["a"]
        g_ref[...] = t["g"]

    vec = _vec_spec(1, R)
    return pl.pallas_call(
        body, name=name, grid=(S // tm,),
        in_specs=[_row_spec(tm, QW), _prev_rows_spec(tm, QW), _vec_spec(1, QW), _vec_spec(LORA_PAD, 3 * R), vec, vec,
                  vec, vec],
        out_specs=tuple([_row_spec(tm, R)] * 7),
        out_shape=tuple([jax.ShapeDtypeStruct((S, R), F32)] * 7),
        compiler_params=_cp(("parallel",), 48),
    )(q, q, mu, wl, w0, a0, kkw, kaw)


def _rwkv_prep_bwd(q, mu, wl, w0, a0, kkw, kaw, grads, *, name):
    S, QW = q.shape
    R = w0.shape[1]
    tm = ROW_TILE // 2
    n = S // tm

    def body(q_ref, qp_ref, mu_ref, wl_ref, w0_ref, a0_ref, kk_ref, ka_ref, dr_ref, dw_ref, dk2_ref, dv_ref, dkap_ref,
             db_ref, dg_ref, drb_ref, dk2b_ref, dvb_ref, dps_ref, dwl_ref, sums_ref, acc_ref):
        i = pl.program_id(0)

        @pl.when(i == 0)
        def _():
            acc_ref[...] = jnp.zeros_like(acc_ref)
            dwl_ref[...] = jnp.zeros_like(dwl_ref)

        first = jnp.where(i > 0, 1.0, 0.0)
        wl = wl_ref[...]
        kkw = kk_ref[...]
        kaw = ka_ref[...]
        t = _prep_common(q_ref[...], qp_ref[...], first, mu_ref[...], wl, w0_ref[...], a0_ref[...], kkw, kaw, R)
        a, kap, k, act = t["a"], t["kap"], t["k"], t["act"]
        db = db_ref[...]
        dk2 = dk2_ref[...] + dk2b_ref[...]
        dkap = dkap_ref[...] + db * a
        da = db * kap + dk2 * k * kaw
        dk = dk2 * t["kmul"]
        proj = jnp.where(jnp.sqrt(t["n2"]) > 1e-12, _segsum(kap * dkap, t["bd"]), 0.0)
        dkk = (dkap - kap * proj) / t["nrm"]
        dk = dk + dkk * kkw
        dapre = da * a * (1.0 - a)
        dwlog = dw_ref[...] * t["decay"] * (-t["ew"])
        dwpre = dwlog * jax.nn.sigmoid(-t["wpre"])
        acc_ref[0] += _fold8(dwpre)
        acc_ref[1] += _fold8(dapre)
        acc_ref[2] += _fold8(dkk * k)
        acc_ref[3] += _fold8(dk2 * k * (a - 1.0))
        dlo = jnp.concatenate([dwpre, dapre, dg_ref[...]], axis=1).astype(BF16)
        dwl_ref[...] += lax.dot_general(act.astype(BF16), dlo, (((0,), (0,)), ((), ())), preferred_element_type=F32)
        dact = lax.dot_general(dlo, wl, (((1,), (1,)), ((), ())), preferred_element_type=F32)
        dlin = jnp.where(t["m_w"], dact * (1.0 - act * act),
                         jnp.where(t["m_a"], dact, jnp.where(t["m_g"], dact * act * (1.0 - act), 0.0)))
        dps_ref[:, 0:R] = dr_ref[...] + drb_ref[...]
        dps_ref[:, R:2 * R] = dk
        dps_ref[:, 2 * R:3 * R] = dv_ref[...] + dvb_ref[...]
        dps_ref[:, 3 * R:3 * R + LORA_PAD] = dlin
        dps_ref[:, 3 * R + LORA_PAD:] = jnp.zeros((tm, QW - 3 * R - LORA_PAD), F32)

        @pl.when(i == n - 1)
        def _():
            for j in range(4):
                sums_ref[j:j + 1, :] = jnp.sum(acc_ref[j], axis=0, keepdims=True)

    vec = _vec_spec(1, R)
    return pl.pallas_call(
        body, name=name, grid=(n,),
        in_specs=[_row_spec(tm, QW), _prev_rows_spec(tm, QW), _vec_spec(1, QW), _vec_spec(LORA_PAD, 3 * R), vec, vec,
                  vec, vec] + [_row_spec(tm, R)] * 10,
        out_specs=(_row_spec(tm, QW), _vec_spec(LORA_PAD, 3 * R), _vec_spec(4, R)),
        out_shape=(jax.ShapeDtypeStruct((S, QW), F32), jax.ShapeDtypeStruct((LORA_PAD, 3 * R), F32),
                   jax.ShapeDtypeStruct((4, R), F32)),
        scratch_shapes=[pltpu.VMEM((4, 8, R), F32)],
        compiler_params=_cp(("arbitrary",), 56),
    )(q, q, mu, wl, w0, a0, kkw, kaw, *grads)


def _tshift_bwd(dps, q, mu, *, name):
    S, QW = q.shape
    tm = ROW_TILE // 2
    n = S // tm

    def body(d_ref, dn_ref, q_ref, qp_ref, mu_ref, dq_ref, dmu_ref, acc_ref):
        i = pl.program_id(0)

        @pl.when(i == 0)
        def _():
            acc_ref[...] = jnp.zeros_like(acc_ref)

        mu = mu_ref[...]
        d = d_ref[...]
        qv = q_ref[...]
        row = lax.broadcasted_iota(jnp.int32, d.shape, 0)
        first = jnp.where(i > 0, 1.0, 0.0)
        notlast = jnp.where(i < n - 1, 1.0, 0.0)
        prev = jnp.where(row == 0, qp_ref[7:8, :] * first, pltpu.roll(qv, 1, 0))
        z = d * mu
        nxt = jnp.where(row == tm - 1, dn_ref[0:1, :] * mu * notlast, pltpu.roll(z, tm - 1, 0))
        dq_ref[...] = (d * (1.0 - mu) + nxt).astype(BF16)
        acc_ref[...] += _fold8(d * (prev - qv))

        @pl.when(i == n - 1)
        def _():
            dmu_ref[...] = jnp.sum(acc_ref[...], axis=0, keepdims=True)

    nblk8 = S // 8
    next_spec = pl.BlockSpec((8, QW), lambda i: (jnp.minimum((i + 1) * (tm // 8), nblk8 - 1), 0))
    return pl.pallas_call(
        body, name=name, grid=(n,),
        in_specs=[_row_spec(tm, QW), next_spec, _row_spec(tm, QW), _prev_rows_spec(tm, QW), _vec_spec(1, QW)],
        out_specs=(_row_spec(tm, QW), _vec_spec(1, QW)),
        out_shape=(jax.ShapeDtypeStruct((S, QW), BF16), jax.ShapeDtypeStruct((1, QW), F32)),
        scratch_shapes=[pltpu.VMEM((8, QW), F32)],
        compiler_params=_cp(("arbitrary",), 48),
    )(dps, dps, q, q, mu)


def _post_common(ysc, r, k2, v, lnw, lnb, rk):
    bd = _block_ones()
    mean = _segsum(ysc, bd) * (1.0 / HEAD)
    d = ysc - mean
    var = _segsum(d * d, bd) * (1.0 / HEAD)
    rstd = lax.rsqrt(var + LN_X_EPS)
    yh = d * rstd
    rkk = _segsum(r * k2 * rk, bd)
    z = yh * lnw + lnb + rkk * v
    return bd, rstd, yh, rkk, z


def _rwkv_post(ysc, r, k2, v, g, ypool, lnw, lnb, rk, *, name):
    S, R = ysc.shape
    PW = ypool.shape[1]
    tm = ROW_TILE

    def body(y_ref, r_ref, k_ref, v_ref, g_ref, yp_ref, lw_ref, lb_ref, rk_ref, cat_ref):
        _, _, _, _, z = _post_common(y_ref[...], r_ref[...], k_ref[...], v_ref[...], lw_ref[...], lb_ref[...],
                                     rk_ref[...])
        cat_ref[:, 0:PW] = yp_ref[...].astype(BF16)
        cat_ref[:, PW:] = (z * g_ref[...]).astype(BF16)

    vec = _vec_spec(1, R)
    return pl.pallas_call(
        body, name=name, grid=(S // tm,),
        in_specs=[_row_spec(tm, R)] * 5 + [_row_spec(tm, PW), vec, vec, vec],
        out_specs=_row_spec(tm, PW + R), out_shape=jax.ShapeDtypeStruct((S, PW + R), BF16),
        compiler_params=_cp(("parallel",), 48),
    )(ysc, r, k2, v, g, ypool, lnw, lnb, rk)


def _rwkv_post_bwd(dcat, ysc, r, k2, v, g, lnw, lnb, rk, *, name):
    S, R = ysc.shape
    tm = ROW_TILE
    n = S // tm

    def body(d_ref, y_ref, r_ref, k_ref, v_ref, g_ref, lw_ref, lb_ref, rk_ref, dy_ref, dg_ref, drb_ref, dkb_ref,
             dvb_ref, sums_ref, acc_ref):
        i = pl.program_id(0)

        @pl.when(i == 0)
        def _():
            acc_ref[...] = jnp.zeros_like(acc_ref)

        rv, kv, vv, lw, rkw = r_ref[...], k_ref[...], v_ref[...], lw_ref[...], rk_ref[...]
        bd, rstd, yh, rkk, z = _post_common(y_ref[...], rv, kv, vv, lw, lb_ref[...], rkw)
        dyr = d_ref[...]
        dg_ref[...] = dyr * z
        dz = dyr * g_ref[...]
        dyh = dz * lw
        dy_ref[...] = rstd * (dyh - _segsum(dyh, bd) * (1.0 / HEAD) - yh * (_segsum(dyh * yh, bd) * (1.0 / HEAD)))
        dvb_ref[...] = dz * rkk
        drkk = _segsum(dz * vv, bd)
        drb_ref[...] = drkk * kv * rkw
        dkb_ref[...] = drkk * rv * rkw
        acc_ref[0] += _fold8(dz * yh)
        acc_ref[1] += _fold8(dz)
        acc_ref[2] += _fold8(drkk * rv * kv)

        @pl.when(i == n - 1)
        def _():
            for j in range(3):
                sums_ref[j:j + 1, :] = jnp.sum(acc_ref[j], axis=0, keepdims=True)

    vec = _vec_spec(1, R)
    dspec = _row_spec(tm, R)
    return pl.pallas_call(
        body, name=name, grid=(n,),
        in_specs=[dspec] + [_row_spec(tm, R)] * 5 + [vec, vec, vec],
        out_specs=tuple([_row_spec(tm, R)] * 5) + (_vec_spec(3, R),),
        out_shape=tuple([jax.ShapeDtypeStruct((S, R), F32)] * 5) + (jax.ShapeDtypeStruct((3, R), F32),),
        scratch_shapes=[pltpu.VMEM((3, 8, R), F32)],
        compiler_params=_cp(("arbitrary",), 56),
    )(dcat, ysc, r, k2, v, g, lnw, lnb, rk)


SEL_ROWS = 64


def _column_selector():
    row = lax.broadcasted_iota(jnp.int32, (SEL_ROWS, 8 * 128), 0)
    col = lax.broadcasted_iota(jnp.int32, (SEL_ROWS, 8 * 128), 1)
    head, rest = row // 32, row % 32
    hit = (rest < 24) & (rest % 8 == col // 128) & (head == (col % 128) // HEAD)
    return jnp.where(hit, 1.0, 0.0).astype(BF16)


def _expand_columns(x, sel):
    hi = x.astype(BF16).astype(F32)
    r1 = x - hi
    mid = r1.astype(BF16).astype(F32)
    lo = (r1 - mid).astype(BF16).astype(F32)
    terms = jnp.concatenate([hi, mid, lo, jnp.zeros_like(x)], axis=0)
    both = jnp.concatenate([terms, pltpu.roll(terms, HEAD, 1)], axis=0)[:, 0:HEAD]
    return lax.dot_general(both.astype(BF16), sel, (((0,), (0,)), ((), ())), preferred_element_type=F32)


def _head_sum_weights():
    row = lax.broadcasted_iota(jnp.int32, (256, 256), 0)
    col = lax.broadcasted_iota(jnp.int32, (256, 256), 1)
    return jnp.where((row % 128) // HEAD == col // 128, 1.0, 0.0).astype(BF16)


def _head_sums_mxu(products, w2):
    rows = []
    for p in products:
        hi = p.astype(BF16)
        rows.append(jnp.concatenate([hi, (p - hi.astype(F32)).astype(BF16)], axis=1))
    out = jnp.dot(jnp.concatenate(rows, axis=0), w2, preferred_element_type=F32)
    return [(out[i * HEAD:(i + 1) * HEAD, 0:128], out[i * HEAD:(i + 1) * HEAD, 128:256]) for i in range(len(products))]


def _masked_rows(rows, negate=False):
    head_a = (lax.broadcasted_iota(jnp.int32, rows.shape, 1) % 128) < HEAD
    v = -rows if negate else rows
    return jnp.where(head_a, v, 0.0), jnp.where(head_a, 0.0, v)


def _lane_sums(x, row_a, row_b):
    return jnp.sum(x * row_a, axis=1, keepdims=True), jnp.sum(x * row_b, axis=1, keepdims=True)


def _scan_fwd(r, w, k, v, kap, b, *, name, plan=None):
    S, R = r.shape
    G, T = SCAN_G, SCAN_T
    NP = R // 128
    assert NP % G == 0 and S % T == 0
    GW = 128 * G

    def body(r_ref, w_ref, k_ref, v_ref, kap_ref, b_ref, sel_ref, w2_ref, y_ref, sa_ref, st_ref, s_scr, vc_scr,
             yt_scr, sat_scr):
        c = pl.program_id(1)

        @pl.when(c == 0)
        def _():
            s_scr[...] = jnp.zeros_like(s_scr)

        yt_scr[...] = jnp.zeros_like(yt_scr)
        sat_scr[...] = jnp.zeros_like(sat_scr)
        lane = lax.broadcasted_iota(jnp.int32, (HEAD, 128), 1)
        m_a = lane < HEAD

        def block(tb, carry):
            t0 = pl.multiple_of(tb * 8, 8)
            rb, wb, kb = r_ref[pl.ds(t0, 8), :], w_ref[pl.ds(t0, 8), :], k_ref[pl.ds(t0, 8), :]
            pb, bb, vb = kap_ref[pl.ds(t0, 8), :], b_ref[pl.ds(t0, 8), :], v_ref[pl.ds(t0, 8), :]
            for g in range(G):
                vc_scr[g] = _expand_columns(vb[:, g * 128:(g + 1) * 128], sel_ref[...])

            def put_y(g, parts, hot_y):
                yt_scr[g, 0:HEAD, :] = jnp.where(hot_y, parts[0], yt_scr[g, 0:HEAD, :])
                yt_scr[g, HEAD:, :] = jnp.where(hot_y, parts[1], yt_scr[g, HEAD:, :])

            def put_sa(g, parts, hot_t):
                sat_scr[g, 0:HEAD, :] = jnp.where(hot_t, parts[0], sat_scr[g, 0:HEAD, :])
                sat_scr[g, HEAD:, :] = jnp.where(hot_t, parts[1], sat_scr[g, HEAD:, :])

            npa, npb = _masked_rows(pb, negate=True)
            for j in range(8):
                t = t0 + j
                cols = slice(j * 128, (j + 1) * 128)
                sa_parts, products = [], []
                for g in range(G):
                    sl = slice(g * 128, (g + 1) * 128)
                    sa_parts.append(_lane_sums(s_scr[g], npa[j:j + 1, sl], npb[j:j + 1, sl]))
                hot_t = lane == t
                for g in range(G):
                    sl = slice(g * 128, (g + 1) * 128)
                    sa = jnp.where(m_a, sa_parts[g][0], sa_parts[g][1])
                    st = s_scr[g] * wb[j:j + 1, sl] + sa * bb[j:j + 1, sl] + vc_scr[g, :, cols] * kb[j:j + 1, sl]
                    s_scr[g] = st
                    st_ref[g, t] = st
                    put_sa(g, sa_parts[g], hot_t)
                    products.append(st * rb[j:j + 1, sl])
                for g, parts in enumerate(_head_sums_mxu(products, w2_ref[...])):
                    put_y(g, parts, hot_t)
            return carry

        lax.fori_loop(0, T // 8, block, 0)
        for g in range(G):
            y_ref[:, g * 128:(g + 1) * 128] = yt_scr[g].T[0:T, :]
            sa_ref[:, g * 128:(g + 1) * 128] = sat_scr[g].T[0:T, :]

    tspec = pl.BlockSpec((T, GW), lambda p, c: (c, p))
    sel_spec = pl.BlockSpec((SEL_ROWS, 8 * 128), lambda p, c: (0, 0))
    grid = (NP // G, S // T)
    x_in, x_out, x_shapes, x_scr, x_ops = _host_args(plan)
    w2_spec = pl.BlockSpec((256, 256), lambda p, c: (0, 0))
    outs = pl.pallas_call(
        _host(body, plan, 8, 3, *_grid_ends(grid)), name=name, grid=grid,
        in_specs=[tspec] * 6 + [sel_spec, w2_spec] + x_in,
        out_specs=tuple([tspec, tspec, pl.BlockSpec((G, T, HEAD, 128), lambda p, c: (p, c, 0, 0))] + x_out),
        out_shape=tuple([jax.ShapeDtypeStruct((S, R), F32), jax.ShapeDtypeStruct((S, R), F32),
                         jax.ShapeDtypeStruct((NP, S, HEAD, 128), F32)] + x_shapes),
        scratch_shapes=[pltpu.VMEM((G, HEAD, 128), F32), pltpu.VMEM((G, HEAD, 8 * 128), F32),
                        pltpu.VMEM((G, 128, 128), F32), pltpu.VMEM((G, 128, 128), F32)] + x_scr,
        compiler_params=_cp(_semantics(plan, ("parallel", "arbitrary")), 48),
    )(r, w, k, v, kap, b, _column_selector(), _head_sum_weights(), *x_ops)
    return outs[0], outs[1], outs[2], tuple(outs[3:])


def _scan_bwd(r, w, k, v, kap, b, sa, dy, states, *, name, plan=None):
    S, R = r.shape
    G, T = SCAN_G_BWD, SCAN_T_BWD
    NP = R // 128
    NC = S // T
    GW = 128 * G
    assert NP % G == 0

    def body(r_ref, w_ref, k_ref, v_ref, kap_ref, b_ref, sa_ref, dy_ref, st_ref, sp_ref, sel_ref, w2_ref, dr_ref,
             dw_ref, dk_ref, dv_ref, dkap_ref, db_ref, ds_scr, vc_scr, dyc_scr, sac_scr, dvt_scr):
        ci = pl.program_id(1)

        @pl.when(ci == 0)
        def _():
            ds_scr[...] = jnp.zeros_like(ds_scr)

        dvt_scr[...] = jnp.zeros_like(dvt_scr)
        lane = lax.broadcasted_iota(jnp.int32, (HEAD, 128), 1)
        m_a = lane < HEAD
        sub = lax.broadcasted_iota(jnp.int32, (8, 128), 0)
        zero_i = jnp.zeros((HEAD, 128), jnp.int32)
        has_prev = jnp.where(ci < NC - 1, 1.0, 0.0)

        def state_before(g, t):
            at_start = (zero_i + t) == 0
            return jnp.where(at_start, sp_ref[g, 0] * has_prev, st_ref[g, jnp.maximum(t - 1, 0)])

        def block(it, carry):
            tb = T // 8 - 1 - it
            t0 = pl.multiple_of(tb * 8, 8)
            rb, wb, kb = r_ref[pl.ds(t0, 8), :], w_ref[pl.ds(t0, 8), :], k_ref[pl.ds(t0, 8), :]
            pb, bb = kap_ref[pl.ds(t0, 8), :], b_ref[pl.ds(t0, 8), :]
            vb, dyb, sab = v_ref[pl.ds(t0, 8), :], dy_ref[pl.ds(t0, 8), :], sa_ref[pl.ds(t0, 8), :]
            for g in range(G):
                sl = slice(g * 128, (g + 1) * 128)
                vc_scr[g] = _expand_columns(vb[:, sl], sel_ref[...])
                dyc_scr[g] = _expand_columns(dyb[:, sl], sel_ref[...])
                sac_scr[g] = _expand_columns(sab[:, sl], sel_ref[...])
            outs = [[jnp.zeros((8, 128), F32) for _ in range(5)] for _ in range(G)]
            bba, bbb = _masked_rows(bb)
            def recurrence(j):
                cols = slice(j * 128, (j + 1) * 128)
                dsp, dsa_parts = [], []
                for g in range(G):
                    sl = slice(g * 128, (g + 1) * 128)
                    ds = ds_scr[g] + dyc_scr[g, :, cols] * rb[j:j + 1, sl]
                    dsp.append(ds)
                    dsa_parts.append(_lane_sums(ds, bba[j:j + 1, sl], bbb[j:j + 1, sl]))
                dsas = []
                for g in range(G):
                    sl = slice(g * 128, (g + 1) * 128)
                    dsa = jnp.where(m_a, dsa_parts[g][0], dsa_parts[g][1])
                    ds_scr[g] = dsp[g] * wb[j:j + 1, sl] - dsa * pb[j:j + 1, sl]
                    dsas.append(dsa)
                return dsp, dsas

            def side_work(j, dsp, dsas):
                t = t0 + j
                hot = lane == t
                cols = slice(j * 128, (j + 1) * 128)
                dvs = _head_sums_mxu([dsp[g] * kb[j:j + 1, g * 128:(g + 1) * 128] for g in range(G)], w2_ref[...])
                for g in range(G):
                    ds = dsp[g]
                    s_p = st_ref[g, t - 1] if j > 0 else state_before(g, t)
                    dr_row = jnp.sum(st_ref[g, t] * dyc_scr[g, :, cols], axis=0, keepdims=True)
                    dk_row = jnp.sum(ds * vc_scr[g, :, cols], axis=0, keepdims=True)
                    db_row = jnp.sum(ds * sac_scr[g, :, cols], axis=0, keepdims=True)
                    dw_row = jnp.sum(ds * s_p, axis=0, keepdims=True)
                    dkap_row = -jnp.sum(s_p * dsas[g], axis=0, keepdims=True)
                    dvt_scr[g, 0:HEAD, :] = jnp.where(hot, dvs[g][0], dvt_scr[g, 0:HEAD, :])
                    dvt_scr[g, HEAD:, :] = jnp.where(hot, dvs[g][1], dvt_scr[g, HEAD:, :])
                    pick = sub == j
                    for q, row in enumerate((dr_row, dw_row, dk_row, dkap_row, db_row)):
                        outs[g][q] = jnp.where(pick, row, outs[g][q])

            pending = None
            for j in range(7, -1, -1):
                done = recurrence(j)
                if pending is not None:
                    side_work(*pending)
                pending = (j, *done)
            side_work(*pending)
            for g in range(G):
                sl = slice(g * 128, (g + 1) * 128)
                for q, ref in enumerate((dr_ref, dw_ref, dk_ref, dkap_ref, db_ref)):
                    ref[pl.ds(t0, 8), sl] = outs[g][q]
            return carry

        lax.fori_loop(0, T // 8, block, 0)
        for g in range(G):
            dv_ref[:, g * 128:(g + 1) * 128] = dvt_scr[g].T[0:T, :]

    tspec = pl.BlockSpec((T, GW), lambda p, c: (NC - 1 - c, p))
    st_spec = pl.BlockSpec((G, T, HEAD, 128), lambda p, c: (p, NC - 1 - c, 0, 0))
    prev_spec = pl.BlockSpec((G, 1, HEAD, 128), lambda p, c: (p, jnp.maximum((NC - 1 - c) * T - 1, 0), 0, 0))
    sel_spec = pl.BlockSpec((SEL_ROWS, 8 * 128), lambda p, c: (0, 0))
    grid = (NP // G, NC)
    x_in, x_out, x_shapes, x_scr, x_ops = _host_args(plan)
    outs = pl.pallas_call(
        _host(body, plan, 12, 6, *_grid_ends(grid)), name=name, grid=grid,
        in_specs=[tspec] * 8 + [st_spec, prev_spec, sel_spec, pl.BlockSpec((256, 256), lambda p, c: (0, 0))] + x_in,
        out_specs=tuple([tspec] * 6 + x_out),
        out_shape=tuple([jax.ShapeDtypeStruct((S, R), F32)] * 6 + x_shapes),
        scratch_shapes=[pltpu.VMEM((G, HEAD, 128), F32), pltpu.VMEM((G, HEAD, 8 * 128), F32),
                        pltpu.VMEM((G, HEAD, 8 * 128), F32), pltpu.VMEM((G, HEAD, 8 * 128), F32),
                        pltpu.VMEM((G, 128, 128), F32)] + x_scr,
        compiler_params=_cp(_semantics(plan, ("parallel", "arbitrary")), 48),
    )(r, w, k, v, kap, b, sa, dy, states, states, _column_selector(), _head_sum_weights(), *x_ops)
    return tuple(outs[:6]), tuple(outs[6:])


def _sum_parts(parts, *, name):
    P, rows, W = parts.shape
    tr = rows
    for cand in (1024, 512, 256, 128, 64, 32, 16, 8):
        if rows % cand == 0:
            tr = cand
            break

    def body(p_ref, o_ref):
        acc = p_ref[0]
        for s in range(1, P):
            acc = acc + p_ref[s]
        o_ref[...] = acc

    return pl.pallas_call(
        body, name=name, grid=(rows // tr,),
        in_specs=[pl.BlockSpec((P, tr, W), lambda i: (0, i, 0))],
        out_specs=pl.BlockSpec((tr, W), lambda i: (i, 0)), out_shape=jax.ShapeDtypeStruct((rows, W), F32),
        compiler_params=_cp(("parallel",), 32),
    )(parts)


def _adamw(w, m, v, parts, *, name):
    R, C = w.shape
    P = parts.shape[0]
    tr = R
    for cand in (1024, 512, 256, 128, 64, 32, 16, 8):
        if R % cand == 0 and cand * C * 4 * (7 + P) <= 10 * 1024 * 1024:
            tr = cand
            break
    bc1 = 1.0 - ADAM_B1 ** ADAM_STEP
    bc2 = 1.0 - ADAM_B2 ** ADAM_STEP

    def body(w_ref, m_ref, v_ref, p_ref, g_ref, d_ref, nm_ref, nv_ref):
        g = p_ref[0].astype(F32)
        for s in range(1, P):
            g = g + p_ref[s].astype(F32)
        m1 = ADAM_B1 * m_ref[...] + (1.0 - ADAM_B1) * g
        v1 = ADAM_B2 * v_ref[...] + (1.0 - ADAM_B2) * (g * g)
        m_hat = m1 / bc1
        v_hat = v1 / bc2
        g_ref[...] = g
        d_ref[...] = -ADAM_LR * (m_hat / (jnp.sqrt(v_hat) + ADAM_EPS) + ADAM_WD * w_ref[...])
        nm_ref[...] = m1
        nv_ref[...] = v1

    spec = pl.BlockSpec((tr, C), lambda i: (i, 0))
    return pl.pallas_call(
        body, name=name, grid=(R // tr,),
        in_specs=[spec, spec, spec, pl.BlockSpec((P, tr, C), lambda i: (0, i, 0))],
        out_specs=(spec, spec, spec, spec), out_shape=tuple([jax.ShapeDtypeStruct((R, C), F32)] * 4),
        compiler_params=_cp(("parallel",), 40),
    )(w, m, v, parts)


def _cols_full(g8):
    n, rows, c = g8.shape
    return jnp.transpose(g8, (1, 0, 2)).reshape(rows, n * c)


def _cols_split(full):
    rows, cols = full.shape
    return jnp.transpose(full.reshape(rows, N_DEV, cols // N_DEV), (1, 0, 2))


def _pack(vals, rows_multiple=512):
    flat = jnp.concatenate([v.reshape(-1).astype(F32) for v in vals])
    n = flat.shape[0]
    unit = 128 * rows_multiple
    padded = ((n + unit - 1) // unit) * unit
    return jnp.pad(flat, (0, padded - n)).reshape(padded // 128, 128)


def _unpack(packed, shapes):
    flat = packed.reshape(-1)
    out, off = [], 0
    for shp in shapes:
        size = 1
        for d in shp:
            size *= d
        out.append(flat[off:off + size].reshape(shp))
        off += size
    return out


def _ffn_forward(x, weights, gpre, gpost, shift, scale1p, gw, tag, up_plan=None, down_from_plan=None,
                 down_plan=None):
    g8, u8, d8 = weights
    h = _pre_norm_mod(x, gpre, shift, scale1p, name=f"{tag}_pre")
    au, s, carried = _ffn_up(h, g8, u8, tm=1024, tk=2048, plan=up_plan,
                             name=f"{tag}_up" + ("_carry" if up_plan else ""))
    if down_from_plan is not None:
        d8 = carried[down_from_plan]
    wd = d8.reshape(-1, d8.shape[2])
    if down_plan is None:
        f, carried_down = _mm(s, wd, tm=1024, tn=1024, tk=2048, name=f"{tag}_down"), ()
    else:
        f, carried_down = _mm(s, wd, tm=1024, tn=1024, tk=2048, name=f"{tag}_down_carry", plan=down_plan)
    xo = _post_norm_res(x, f, gpost, gw, name=f"{tag}_post")
    return xo, (h, au, s, f), d8, carried, carried_down


def _ffn_backward(dxo, x, saved, weights, gpre, gpost, scale1p, gw, tag, plans=None, own_sums=None,
                  swap_in_dh=False):
    g8, u8, d8 = weights
    h, au, s, f = saved
    plans = dict(plans or {})

    def nm(key):
        return f"{tag}_{key}" + ("_carry" if plans.get(key) is not None else "")

    df, post_sums = _post_norm_res_bwd(dxo, f, gpost, gw, MACARON, name=f"{tag}_post_bwd")
    dwd, got_dwd = _ffn_dwd(s, df, tn=1024, tk=1024, plan=plans.get("dwd"), name=nm("dwd"))
    if own_sums:
        plans["down_bwd"] = _ChipsPlan(own_sums([dwd], "d"))
    dau, got_down = _ffn_down_bwd(df, d8, au, tm=1024, tk=2048, plan=plans.get("down_bwd"), name=nm("down_bwd"))
    dwgu, got_dwgu = _ffn_dwgu(h, dau, tm=1024, tk=1024, plan=plans.get("dwgu"), name=nm("dwgu"))
    if own_sums:
        plans["dh"] = _ChipsPlan(own_sums([dwgu], "gu"))
    elif swap_in_dh:
        plans["dh"] = _SwapPlan([dwgu, dwd])
    dh, got_dh = _ffn_dh(dau, g8, u8, tm=1024, tn=1024, plan=plans.get("dh"),
                         name=nm("dh") + ("_swap" if swap_in_dh else ""))
    dx, pre_sums = _pre_norm_mod_bwd(dh, x, dxo, gpre, scale1p, name=f"{tag}_pre_bwd")
    carried = {"dwd": got_dwd, "down_bwd": got_down, "dwgu": got_dwgu, "dh": got_dh}
    return dx, dwgu, dwd, pre_sums, post_sums, carried


def kernel(x, c, w_ada, b_ada, norm_pre, norm_post, ffn1_w_gate, ffn1_w_up, ffn1_w_down, w_in, mu_shift, pool_w, pool_scale, w0, w2, a0, a2, g2, k_k, k_a, r_k, lnx_w, lnx_b, w_out, ffn2_w_gate, ffn2_w_up, ffn2_w_down, loss_target, m_w_ada, m_b_ada, m_norm_pre, m_norm_post, m_ffn1_w_gate, m_ffn1_w_up, m_ffn1_w_down, m_w_in, m_mu_shift, m_pool_w, m_pool_scale, m_w0, m_w2, m_a0, m_a2, m_g2, m_k_k, m_k_a, m_r_k, m_lnx_w, m_lnx_b, m_w_out, m_ffn2_w_gate, m_ffn2_w_up, m_ffn2_w_down, v_w_ada, v_b_ada, v_norm_pre, v_norm_post, v_ffn1_w_gate, v_ffn1_w_up, v_ffn1_w_down, v_w_in, v_mu_shift, v_pool_w, v_pool_scale, v_w0, v_w2, v_a0, v_a2, v_g2, v_k_k, v_k_a, v_r_k, v_lnx_w, v_lnx_b, v_w_out, v_ffn2_w_gate, v_ffn2_w_up, v_ffn2_w_down):
    names = ["w_ada", "b_ada", "norm_pre", "norm_post", "ffn1_w_gate", "ffn1_w_up", "ffn1_w_down", "w_in", "mu_shift",
             "pool_w", "pool_scale", "w0", "w2", "a0", "a2", "g2", "k_k", "k_a", "r_k", "lnx_w", "lnx_b", "w_out",
             "ffn2_w_gate", "ffn2_w_up", "ffn2_w_down"]
    env = dict(locals())
    W = {n: env[n][0] for n in names}
    M1 = {n: env["m_" + n][0] for n in names}
    V1 = {n: env["v_" + n][0] for n in names}

    me = _my_index()
    xs = x[0]
    tgt = loss_target[0]
    S, D = xs.shape
    F = W["ffn1_w_gate"].shape[1] * N_DEV
    R = W["w0"].shape[0]
    PW = D - R
    IN_W = W["w_in"].shape[1] * N_DEV
    P_W = F
    QW = P_W - PW
    NMOD = 9 * D
    ada_c = W["w_ada"].shape[1]

    c_all, npre8, npost8, w2_8, a2_8, g2_8 = _exchange(
        [c, W["norm_pre"], W["norm_post"], W["w2"].astype(BF16), W["a2"].astype(BF16), W["g2"].astype(BF16)],
        scatter=False, name="gather_small")
    c_all = c_all.reshape(N_DEV, D)
    gpre = _cols_full(npre8)
    gpost = _cols_full(npost8)
    wl = jnp.zeros((LORA_PAD, 3 * R), BF16)
    wl = wl.at[0:LORA_W, 0:R].set(_cols_full(w2_8))
    wl = wl.at[LORA_W:LORA_W + LORA_A, R:2 * R].set(_cols_full(a2_8))
    wl = wl.at[LORA_W + LORA_A:LORA_W + LORA_A + LORA_G, 2 * R:3 * R].set(_cols_full(g2_8))

    sc_all = jax.nn.silu(c_all)
    sc_pad = jnp.concatenate([sc_all, jnp.zeros((8, D), F32)], axis=0).astype(BF16)
    modcols = _mm(sc_pad, W["w_ada"], tm=16, tn=ada_c, tk=256, name="ada_fwd")[0:N_DEV]
    modcols = modcols + lax.dynamic_slice(W["b_ada"], (me * ada_c,), (ada_c,))[None, :]
    (mod8,) = _exchange([modcols], scatter=False, name="gather_mod")
    mod = lax.dynamic_index_in_dim(mod8, me, axis=1, keepdims=False).reshape(9, D)

    def mod_row(i):
        return mod[i:i + 1, :]

    f_pad = FF_TILE - F // N_DEV

    def ffn_shards(tag):
        return [jnp.pad(W[f"{tag}_w_gate"].astype(BF16), ((0, 0), (0, f_pad))),
                jnp.pad(W[f"{tag}_w_up"].astype(BF16), ((0, 0), (0, f_pad))),
                jnp.pad(W[f"{tag}_w_down"].astype(BF16), ((0, f_pad), (0, 0)))]

    ffn1_shards = ffn_shards("ffn1")
    g8_1, u8_1 = _gather_two_level(ffn1_shards[:2], name="gather_ffn_up")
    up_plan = _GatherPlan([ffn1_shards[2]])
    down_plan = _GatherPlan([W["w_in"].astype(BF16)])
    scan_plan = _GatherPlan(ffn_shards("ffn2") + [W["w_out"].astype(BF16)])

    mu_p = jnp.pad(W["mu_shift"], (0, QW - W["mu_shift"].shape[0]))[None, :]
    vec = lambda a: a.reshape(1, -1)
    w0r, a0r, kkr, kar = vec(W["w0"]), vec(W["a0"]), vec(W["k_k"]), vec(W["k_a"])
    lnw, lnb, rkr = vec(W["lnx_w"]), vec(W["lnx_b"]), vec(W["r_k"])
    pscale = vec(W["pool_scale"])

    sc1p = [1.0 + mod_row(3 * s + 1) for s in range(3)]
    shifts = [mod_row(3 * s) for s in range(3)]
    wgts = [MACARON, 1.0, MACARON]
    gws = [wgts[s] * (1.0 + mod_row(3 * s + 2)) for s in range(3)]
    gp = [gpre[s:s + 1] for s in range(3)]
    gq = [gpost[s:s + 1] for s in range(3)]

    x1, sv1, d8_1, _, (win8,) = _ffn_forward(xs, (g8_1, u8_1, None), gp[0], gq[0], shifts[0], sc1p[0], gws[0], "ffn",
                                             up_plan=up_plan, down_from_plan=0, down_plan=down_plan)
    ffn1_w = (g8_1, u8_1, d8_1)
    w_in_p = jnp.pad(_cols_full(win8), ((0, 0), (0, P_W - IN_W)))

    h2 = _pre_norm_mod(x1, gp[1], shifts[1], sc1p[1], name="mix_pre")
    p = _mm(h2, w_in_p, tm=1024, tn=512, tk=2048, name="mix_in")
    q = p[:, PW:]
    o_pool, y_pool = _pool_fwd(p, W["pool_w"], pscale, name="pool_fwd")
    r_s, w_s, k_s, v_s, kap_s, b_s, g_s = _rwkv_prep(q, mu_p, wl, w0r, a0r, kkr, kar, name="rwkv_prep")
    y_scan, sa_s, states, gathered = _scan_fwd(r_s, w_s, k_s, v_s, kap_s, b_s, name="scan_fwd", plan=scan_plan)
    ffn2_w = gathered[:3]
    w_out_f = gathered[3].reshape(D, D)
    cat = _rwkv_post(y_scan, r_s, k_s, v_s, g_s, y_pool, lnw, lnb, rkr, name="rwkv_post")
    f2 = _mm(cat, w_out_f, tm=1024, tn=1024, tk=2048, name="mix_out")
    x2 = _post_norm_res(x1, f2, gq[1], gws[1], name="mix_post")

    x3, sv3, _, _, _ = _ffn_forward(x2, ffn2_w, gp[2], gq[2], shifts[2], sc1p[2], gws[2], "ffn")

    loss_part, dx3 = _loss_head(x3, tgt, name="loss_head")
    loss = lax.psum(loss_part[0, 0], MESH_AXES)

    def by_core_chip(blocks):
        shp = blocks.shape
        t = blocks.astype(BF16).reshape((N_DEV // 2, 2) + shp[1:])
        return jnp.swapaxes(t, 0, 1)

    def chip_sums(mine, tag):
        got = _sibling_swap(mine, name=f"{tag}_swap")
        return [_pair_add(m, g, name=f"{tag}_add{i}") for i, (m, g) in enumerate(zip(mine, got))]

    def ffn_parts(pgu, pd):
        fs = F // N_DEV
        return pgu[:, :D, :fs], pgu[:, D:, :fs], pd[:, :fs, :]

    dx2, dwgu2, dwd2, pre3, post3, got2 = _ffn_backward(dx3, x2, sv3, ffn2_w, gp[2], gq[2], sc1p[2], gws[2], "ffn",
                                                        swap_in_dh=True)
    sums2_gu = _pair_add(dwgu2, got2["dh"][0], name="scatter_ffn_add0")
    sums2_d = _pair_add(dwd2, got2["dh"][1], name="scatter_ffn_add1")

    df2, post2 = _post_norm_res_bwd(dx2, f2, gq[1], gws[1], 1.0, name="mix_post_bwd")
    dw_out = _mm(cat, df2, ta=True, tm=1024, tn=1024, tk=1024, name="mix_dwout")
    dcat = _mm(df2, w_out_f, tb=True, tm=1024, tn=1024, tk=2048, name="mix_dcat")
    dyr = dcat[:, PW:]
    dysc, dg, dr_b, dk2_b, dv_b, post_sums = _rwkv_post_bwd(dyr, y_scan, r_s, k_s, v_s, g_s, lnw, lnb, rkr,
                                                             name="rwkv_post_bwd")
    (dr, dw, dk2, dv, dkap, db), parts2 = _scan_bwd(r_s, w_s, k_s, v_s, kap_s, b_s, sa_s, dysc, states,
                                                    name="scan_bwd", plan=_ChipsPlan([sums2_gu, sums2_d]))
    dps, dwl, prep_sums = _rwkv_prep_bwd(q, mu_p, wl, w0r, a0r, kkr, kar,
                                         (dr, dw, dk2, dv, dkap, db, dg, dr_b, dk2_b, dv_b), name="rwkv_prep_bwd")
    dq, dmu = _tshift_bwd(dps, q, mu_p, name="tshift_bwd")
    du_pool, dpool_w, dpool_scale = _pool_bwd(dcat, o_pool, W["pool_w"], pscale, name="pool_bwd")
    dp = jnp.concatenate([du_pool, dq], axis=1)
    dw_in = _mm(h2, dp, ta=True, tm=1024, tn=512, tk=1024, name="mix_dwin")
    dh2 = _mm(dp, w_in_p, tb=True, tm=1024, tn=1024, tk=2816, name="mix_dh")
    dx1, pre2 = _pre_norm_mod_bwd(dh2, x1, dx2, gp[1], sc1p[1], name="mix_pre_bwd")

    sums_mix = chip_sums([by_core_chip(_cols_split(dw_in[:, :IN_W])),
                          by_core_chip(dw_out.reshape(N_DEV, D // N_DEV, D))], "scatter_mixer")
    early = [dmu[0, :W["mu_shift"].shape[0]], dpool_w, dpool_scale, prep_sums[0], prep_sums[1], prep_sums[2],
             prep_sums[3], post_sums[2], post_sums[0], post_sums[1],
             dwl[0:LORA_W, 0:R], dwl[LORA_W:LORA_W + LORA_A, R:2 * R],
             dwl[LORA_W + LORA_A:LORA_W + LORA_A + LORA_G, 2 * R:3 * R]]
    early_shapes = [a.shape for a in early]
    dx0, _, _, pre1, post1, got = _ffn_backward(
        dx1, xs, sv1, ffn1_w, gp[0], gq[0], sc1p[0], gws[0], "ffn",
        plans={"dwd": _ChipsPlan(sums_mix), "dwgu": _GatherPlan([_pack(early)])},
        own_sums=lambda blocks, part: chip_sums(blocks, f"scatter_ffn_{part}"))

    pres, posts = [pre1, pre2, pre3], [post1, post2, post3]
    dmod = jnp.stack([jnp.stack([pres[s][0], pres[s][1], posts[s][0]]) for s in range(3)]).reshape(NMOD // 128, 128)
    late = [jnp.stack([pres[s][2] for s in range(3)]), jnp.stack([posts[s][1] for s in range(3)])]
    late_shapes = [a.shape for a in late]
    dmod8, late8 = _gather_two_level([dmod, _pack(late, rows_multiple=8)], name="gather_grads")
    g_b_ada = _sum_parts(dmod8, name="sum_dmod").reshape(NMOD)
    g_npre, g_npost = _unpack(_sum_parts(late8, name="sum_late"), late_shapes)
    (g_mu, g_pool_w, g_pool_scale, g_w0, g_a0, g_kk, g_ka, g_rk, g_lnw, g_lnb, g_w2, g_a2,
     g_g2) = _unpack(_sum_parts(got["dwgu"][0], name="sum_small"), early_shapes)

    dmod_all = dmod8.reshape(N_DEV, NMOD)
    dmod_cols = lax.dynamic_slice(dmod_all, (0, me * ada_c), (N_DEV, ada_c))
    dmod_cols = jnp.concatenate([dmod_cols, jnp.zeros_like(dmod_cols)], axis=0)
    g_w_ada = _mm(sc_pad, dmod_cols, ta=True, tm=D, tn=ada_c // 9, tk=16, name="ada_bwd")

    pg2, pu2, pd2 = ffn_parts(*parts2)
    pin, pout = got["dwd"]
    pg1, pu1, pd1 = ffn_parts(got["dh"][0], got["down_bwd"][0])

    res = {}

    def big(nm, parts, tag):
        res[nm] = _adamw(W[nm], M1[nm], V1[nm], parts, name=tag)

    big("ffn1_w_gate", pg1, "adamw_cols")
    big("ffn1_w_up", pu1, "adamw_cols")
    big("ffn1_w_down", pd1, "adamw_rows")
    big("ffn2_w_gate", pg2, "adamw_cols")
    big("ffn2_w_up", pu2, "adamw_cols")
    big("ffn2_w_down", pd2, "adamw_rows")
    big("w_in", pin, "adamw_w_in")
    big("w_out", pout, "adamw_w_out")
    big("w_ada", g_w_ada[None], "adamw_w_ada")

    def my_cols(full, width):
        return lax.dynamic_slice_in_dim(full, me * width, width, axis=full.ndim - 1)

    small_names = ["b_ada", "mu_shift", "pool_w", "pool_scale", "w0", "a0", "k_k", "k_a", "r_k", "lnx_w", "lnx_b",
                   "norm_pre", "norm_post", "w2", "a2", "g2"]
    small_grads = [g_b_ada, g_mu, g_pool_w, g_pool_scale, g_w0, g_a0, g_kk, g_ka, g_rk.reshape(W["r_k"].shape), g_lnw,
                   g_lnb, my_cols(g_npre, D // N_DEV), my_cols(g_npost, D // N_DEV), my_cols(g_w2, R // N_DEV),
                   my_cols(g_a2, R // N_DEV), my_cols(g_g2, R // N_DEV)]
    shapes = [W[n].shape for n in small_names]
    packed = _adamw(_pack([W[n] for n in small_names]), _pack([M1[n] for n in small_names]),
                    _pack([V1[n] for n in small_names]), _pack(small_grads)[None], name="adamw_small")
    unpacked = [_unpack(t, shapes) for t in packed]
    for i, nm in enumerate(small_names):
        res[nm] = tuple(unpacked[k][i] for k in range(4))

    outs = [loss, dx0[None]]
    for k in range(4):
        outs.extend(res[nm][k][None] for nm in names)
    return tuple(outs)
```

```python
import functools

import jax
import jax.numpy as jnp
from jax import lax
from jax.experimental import pallas as pl
from jax.experimental.pallas import tpu as pltpu

F32 = jnp.float32
BF16 = jnp.bfloat16
N_DEV = 8
MESH_AXES = ("x", "y", "c")

NORM_EPS = 1e-6
HEAD = 64
LN_X_EPS = 1e-5 * HEAD
POOL_GROUPS = 4
POOL_GROUP = 128
MACARON = 0.5
LORA_W, LORA_A, LORA_G = 64, 64, 224
LORA_PAD = 384
ADAM_LR, ADAM_B1, ADAM_B2, ADAM_EPS, ADAM_WD, ADAM_STEP = 0.001, 0.9, 0.999, 1e-08, 0.01, 10

FF_TILE = 768
ROW_TILE = 256
SCAN_T = 64
SCAN_T_BWD = 64
SCAN_G = 6
SCAN_G_BWD = 6
VMEM_CAP = 56 * 1024 * 1024


def _cp(sem, vmem_mb):
    return pltpu.CompilerParams(dimension_semantics=sem, vmem_limit_bytes=min(vmem_mb * 1024 * 1024, VMEM_CAP))


def _my_index():
    return 4 * lax.axis_index("x") + 2 * lax.axis_index("y") + lax.axis_index("c")


def _exchange(arrays, *, scatter, name):
    n = len(arrays)
    out_shapes = []
    for a in arrays:
        shp = a.shape if scatter else (N_DEV,) + a.shape
        out_shapes.append(jax.ShapeDtypeStruct(shp, a.dtype))

    def body(*refs):
        ins, outs = refs[:n], refs[n:2 * n]
        send_sems, recv_sems, local_sems = refs[2 * n:]
        me = _my_index()

        def dev(p):
            return (p // 4, (p // 2) % 2, p % 2)

        def copy(i, d):
            peer = (me + d) % N_DEV
            src = ins[i].at[peer] if scatter else ins[i]
            return pltpu.make_async_remote_copy(
                src_ref=src, dst_ref=outs[i].at[me], send_sem=send_sems.at[i, d - 1],
                recv_sem=recv_sems.at[i, d - 1], device_id=dev(peer), device_id_type=pl.DeviceIdType.MESH)

        def arrival(i, d):
            frm = (me + N_DEV - d) % N_DEV
            src = ins[i].at[frm] if scatter else ins[i]
            return pltpu.make_async_remote_copy(
                src_ref=src, dst_ref=outs[i].at[frm], send_sem=send_sems.at[i, d - 1],
                recv_sem=recv_sems.at[i, d - 1], device_id=dev(frm), device_id_type=pl.DeviceIdType.MESH)

        locals_ = []
        for i in range(n):
            src = ins[i].at[me] if scatter else ins[i]
            lc = pltpu.make_async_copy(src, outs[i].at[me], local_sems.at[i])
            lc.start()
            locals_.append(lc)
        sends = [copy(i, d) for d in range(1, N_DEV) for i in range(n)]
        for cp in sends:
            cp.start()
        for d in range(1, N_DEV):
            for i in range(n):
                arrival(i, d).wait_recv()
        for cp in sends:
            cp.wait_send()
        for lc in locals_:
            lc.wait()

    hbm = pl.BlockSpec(memory_space=pltpu.HBM)
    return pl.pallas_call(
        body, name=name, out_shape=tuple(out_shapes), in_specs=[hbm] * n, out_specs=tuple([hbm] * n),
        scratch_shapes=[pltpu.SemaphoreType.DMA((n, N_DEV - 1)), pltpu.SemaphoreType.DMA((n, N_DEV - 1)),
                        pltpu.SemaphoreType.DMA((n,))],
    )(*arrays)


def _remote(src, dst, send_sem, recv_sem, to):
    return pltpu.make_async_remote_copy(src_ref=src, dst_ref=dst, send_sem=send_sem, recv_sem=recv_sem,
                                        device_id=to, device_id_type=pl.DeviceIdType.MESH)


class _GatherPlan:
    def __init__(self, arrays):
        self.arrays = list(arrays)
        self.n = len(arrays)
        self.out_shapes = [jax.ShapeDtypeStruct((N_DEV,) + a.shape, a.dtype) for a in arrays]
        self.scratch = [pltpu.SemaphoreType.DMA((self.n, 7)), pltpu.SemaphoreType.DMA((self.n, 7)),
                        pltpu.SemaphoreType.DMA((self.n,))]

    def _parts(self, ins, outs, sems):
        send_sems, recv_sems, local_sems = sems
        x, y, c = lax.axis_index("x"), lax.axis_index("y"), lax.axis_index("c")
        chips = [(1 - x, y), (x, 1 - y), (1 - x, 1 - y)]

        def slot(i, px, py, pc):
            return outs[i].at[4 * px + 2 * py + pc]

        def copy(i, k, block, to, src=None):
            dst = slot(i, *block)
            return _remote(dst if src is None else src, dst, send_sems.at[i, k], recv_sems.at[i, k], to)

        n = self.n
        locals_ = [pltpu.make_async_copy(ins[i], slot(i, x, y, c), local_sems.at[i]) for i in range(n)]
        first = [copy(i, 1 + j, (x, y, c), (*chip, c), src=ins[i]) for j, chip in enumerate(chips) for i in range(n)]
        first += [copy(i, 0, (x, y, c), (x, y, 1 - c), src=ins[i]) for i in range(n)]
        return (x, y, c), chips, copy, locals_, first

    def start(self, ins, outs, sems):
        _, _, _, locals_, first = self._parts(ins, outs, sems)
        for lc in locals_:
            lc.start()
        for cp in first:
            cp.start()

    def finish(self, ins, outs, sems):
        (x, y, c), chips, copy, locals_, first = self._parts(ins, outs, sems)
        forwards = []
        for j, chip in enumerate(chips):
            for i in range(self.n):
                copy(i, 1 + j, (*chip, c), (x, y, c)).wait_recv()
                fwd = copy(i, 4 + j, (*chip, c), (x, y, 1 - c))
                fwd.start()
                forwards.append(fwd)
        for i in range(self.n):
            copy(i, 0, (x, y, 1 - c), (x, y, c)).wait_recv()
        for j, chip in enumerate(chips):
            for i in range(self.n):
                copy(i, 4 + j, (*chip, 1 - c), (x, y, c)).wait_recv()
        for cp in first + forwards:
            cp.wait_send()
        for lc in locals_:
            lc.wait()


class _ChipsPlan:
    def __init__(self, arrays):
        self.arrays = list(arrays)
        self.n = len(arrays)
        self.out_shapes = [jax.ShapeDtypeStruct(a.shape, a.dtype) for a in arrays]
        self.scratch = [pltpu.SemaphoreType.DMA((self.n, 3)), pltpu.SemaphoreType.DMA((self.n, 3)),
                        pltpu.SemaphoreType.DMA((self.n,))]

    def _parts(self, ins, outs, sems):
        send_sems, recv_sems, local_sems = sems
        x, y, c = lax.axis_index("x"), lax.axis_index("y"), lax.axis_index("c")
        mine = 2 * x + y
        chips = [(1 - x, y), (x, 1 - y), (1 - x, 1 - y)]
        n = self.n
        locals_ = [pltpu.make_async_copy(ins[i].at[mine], outs[i].at[mine], local_sems.at[i]) for i in range(n)]
        sends = [_remote(ins[i].at[2 * chip[0] + chip[1]], outs[i].at[mine], send_sems.at[i, j], recv_sems.at[i, j],
                         (*chip, c)) for j, chip in enumerate(chips) for i in range(n)]

        def arrivals():
            return [_remote(ins[i].at[2 * chip[0] + chip[1]], outs[i].at[2 * chip[0] + chip[1]], send_sems.at[i, j],
                            recv_sems.at[i, j], (*chip, c)) for j, chip in enumerate(chips) for i in range(n)]

        return locals_, sends, arrivals

    def start(self, ins, outs, sems):
        locals_, sends, _ = self._parts(ins, outs, sems)
        for lc in locals_:
            lc.start()
        for cp in sends:
            cp.start()

    def finish(self, ins, outs, sems):
        locals_, sends, arrivals = self._parts(ins, outs, sems)
        for cp in arrivals():
            cp.wait_recv()
        for cp in sends:
            cp.wait_send()
        for lc in locals_:
            lc.wait()


def _run_plan(plan, *, name):
    n = plan.n

    def body(*refs):
        ins, outs, sems = refs[:n], refs[n:2 * n], refs[2 * n:]
        plan.start(ins, outs, sems)
        plan.finish(ins, outs, sems)

    hbm = pl.BlockSpec(memory_space=pltpu.HBM)
    return pl.pallas_call(
        body, name=name, out_shape=tuple(plan.out_shapes), in_specs=[hbm] * n, out_specs=tuple([hbm] * n),
        scratch_shapes=plan.scratch,
    )(*plan.arrays)


def _host(body, plan, n_in, n_out, is_first, is_last):
    if plan is None:
        return body
    m = plan.n

    def wrapped(*refs):
        a, b = n_in, n_in + m
        c, d = b + n_out, b + n_out + m
        own_in, c_in, own_out, c_out, rest = refs[:a], refs[a:b], refs[b:c], refs[c:d], refs[d:]
        n_sems = len(plan.scratch)
        own_scr, c_sems = rest[:len(rest) - n_sems], rest[len(rest) - n_sems:]

        @pl.when(is_first())
        def _():
            plan.start(c_in, c_out, c_sems)

        body(*own_in, *own_out, *own_scr)

        @pl.when(is_last())
        def _():
            plan.finish(c_in, c_out, c_sems)

    return wrapped


def _host_args(plan):
    if plan is None:
        return [], [], [], [], []
    hbm = pl.BlockSpec(memory_space=pltpu.HBM)
    return [hbm] * plan.n, [hbm] * plan.n, list(plan.out_shapes), list(plan.scratch), list(plan.arrays)


def _gather_two_level(arrays, *, name):
    return _run_plan(_GatherPlan(arrays), name=name)


class _SwapPlan:
    def __init__(self, arrays):
        self.arrays = list(arrays)
        self.n = len(arrays)
        self.out_shapes = [jax.ShapeDtypeStruct(a.shape[1:], a.dtype) for a in arrays]
        self.scratch = [pltpu.SemaphoreType.DMA((self.n,)), pltpu.SemaphoreType.DMA((self.n,))]

    def _copies(self, ins, outs, sems):
        send_sems, recv_sems = sems
        x, y, c = lax.axis_index("x"), lax.axis_index("y"), lax.axis_index("c")
        return [_remote(ins[i].at[1 - c], outs[i], send_sems.at[i], recv_sems.at[i], (x, y, 1 - c))
                for i in range(self.n)]

    def start(self, ins, outs, sems):
        for cp in self._copies(ins, outs, sems):
            cp.start()

    def finish(self, ins, outs, sems):
        copies = self._copies(ins, outs, sems)
        for cp in copies:
            cp.wait_recv()
        for cp in copies:
            cp.wait_send()


def _sibling_swap(arrays, *, name):
    return _run_plan(_SwapPlan(arrays), name=name)


def _chips_all_to_all(arrays, *, name):
    return _run_plan(_ChipsPlan(arrays), name=name)


def _pair_add(mine, got, *, name):
    _, nq, R, C = mine.shape
    tr = R
    for cand in (512, 256, 128, 64, 32, 16):
        if R % cand == 0 and cand * C * 2 * 3 * 2 <= 12 * 1024 * 1024:
            tr = cand
            break

    def body(core_ref, m_ref, g_ref, o_ref):
        o_ref[0] = (m_ref[0, 0].astype(F32) + g_ref[0].astype(F32)).astype(BF16)

    core = lax.axis_index("c").astype(jnp.int32).reshape(1)
    return pl.pallas_call(
        body, name=name,
        grid_spec=pltpu.PrefetchScalarGridSpec(
            num_scalar_prefetch=1, grid=(nq, R // tr),
            in_specs=[pl.BlockSpec((1, 1, tr, C), lambda q, i, core_ref: (core_ref[0], q, i, 0)),
                      pl.BlockSpec((1, tr, C), lambda q, i, core_ref: (q, i, 0))],
            out_specs=pl.BlockSpec((1, tr, C), lambda q, i, core_ref: (q, i, 0))),
        out_shape=jax.ShapeDtypeStruct((nq, R, C), BF16),
        compiler_params=_cp(("parallel", "parallel"), 40),
    )(core, mine, got)


def _mm(a, b, *, ta=False, tb=False, tm, tn, tk, out_dtype=F32, name, plan=None):
    M = a.shape[1] if ta else a.shape[0]
    K = a.shape[0] if ta else a.shape[1]
    N = b.shape[0] if tb else b.shape[1]
    tm, tn, tk = min(tm, M), min(tn, N), min(tk, K)
    assert M % tm == 0 and N % tn == 0 and K % tk == 0, (name, M, N, K, tm, tn, tk)
    nk = K // tk
    dims = (((0 if ta else 1,), (1 if tb else 0,)), ((), ()))

    def body(a_ref, b_ref, o_ref, acc_ref):
        k = pl.program_id(2)

        @pl.when(k == 0)
        def _():
            acc_ref[...] = jnp.zeros_like(acc_ref)

        acc_ref[...] += lax.dot_general(a_ref[...].astype(BF16), b_ref[...].astype(BF16), dims,
                                        preferred_element_type=F32)

        @pl.when(k == nk - 1)
        def _():
            o_ref[...] = acc_ref[...].astype(out_dtype)

    a_spec = pl.BlockSpec((tk, tm), lambda i, j, k: (k, i)) if ta else pl.BlockSpec((tm, tk), lambda i, j, k: (i, k))
    b_spec = pl.BlockSpec((tn, tk), lambda i, j, k: (j, k)) if tb else pl.BlockSpec((tk, tn), lambda i, j, k: (k, j))
    blk = 2 * (tm * tk * a.dtype.itemsize + tk * tn * b.dtype.itemsize + tm * tn * jnp.dtype(out_dtype).itemsize)
    grid = (M // tm, N // tn, nk)
    x_in, x_out, x_shapes, x_scr, x_ops = _host_args(plan)
    outs = pl.pallas_call(
        _host(body, plan, 2, 1, *_grid_ends(grid)), name=name, grid=grid, in_specs=[a_spec, b_spec] + x_in,
        out_specs=tuple([pl.BlockSpec((tm, tn), lambda i, j, k: (i, j))] + x_out),
        out_shape=tuple([jax.ShapeDtypeStruct((M, N), out_dtype)] + x_shapes),
        scratch_shapes=[pltpu.VMEM((tm, tn), F32)] + x_scr,
        compiler_params=_cp(_semantics(plan, ("parallel", "parallel", "arbitrary")),
                            (blk + tm * tn * 4) // (1024 * 1024) + 12),
    )(a, b, *x_ops)
    return outs[0] if plan is None else (outs[0], tuple(outs[1:]))


def _grid_ends(grid):
    def is_first():
        ok = pl.program_id(0) == 0
        for ax in range(1, len(grid)):
            ok = ok & (pl.program_id(ax) == 0)
        return ok

    def is_last():
        ok = pl.program_id(0) == grid[0] - 1
        for ax in range(1, len(grid)):
            ok = ok & (pl.program_id(ax) == grid[ax] - 1)
        return ok

    return is_first, is_last


def _semantics(plan, sem):
    return sem if plan is None else tuple("arbitrary" for _ in sem)


def _ffn_up(h, g8, u8, *, tm, tk, name, plan=None):
    S, D = h.shape
    nb, _, tn = g8.shape
    tm = min(tm, S)
    tk = min(tk, D)
    nk = D // tk

    def body(h_ref, g_ref, u_ref, au_ref, s_ref, acc_ref):
        k = pl.program_id(2)

        @pl.when(k == 0)
        def _():
            acc_ref[...] = jnp.zeros_like(acc_ref)

        hv = h_ref[...]
        acc_ref[:, :tn] += jnp.dot(hv, g_ref[0], preferred_element_type=F32)
        acc_ref[:, tn:] += jnp.dot(hv, u_ref[0], preferred_element_type=F32)

        @pl.when(k == nk - 1)
        def _():
            acc = acc_ref[...]
            a = acc[:, :tn]
            u = acc[:, tn:]
            au_ref[...] = acc.astype(BF16)
            s_ref[...] = (a * jax.nn.sigmoid(a) * u).astype(BF16)

    wspec = pl.BlockSpec((1, tk, tn), lambda i, j, k: (j, k, 0))
    grid = (S // tm, nb, nk)
    x_in, x_out, x_shapes, x_scr, x_ops = _host_args(plan)
    outs = pl.pallas_call(
        _host(body, plan, 3, 2, *_grid_ends(grid)), name=name, grid=grid,
        in_specs=[pl.BlockSpec((tm, tk), lambda i, j, k: (i, k)), wspec, wspec] + x_in,
        out_specs=tuple([pl.BlockSpec((tm, 2 * tn), lambda i, j, k: (i, j)),
                         pl.BlockSpec((tm, tn), lambda i, j, k: (i, j))] + x_out),
        out_shape=tuple([jax.ShapeDtypeStruct((S, 2 * nb * tn), BF16), jax.ShapeDtypeStruct((S, nb * tn), BF16)]
                        + x_shapes),
        scratch_shapes=[pltpu.VMEM((tm, 2 * tn), F32)] + x_scr,
        compiler_params=_cp(_semantics(plan, ("parallel", "parallel", "arbitrary")), 52),
    )(h, g8, u8, *x_ops)
    return outs[0], outs[1], tuple(outs[2:])


def _ffn_dh(dau, g8, u8, *, tm, tn, name, plan=None):
    S = dau.shape[0]
    nb, D, tf = g8.shape
    tm, tn = min(tm, S), min(tn, D)
    nk = 2 * nb
    nt = (((1,), (1,)), ((), ()))

    def body(a_ref, g_ref, u_ref, o_ref, acc_ref):
        k = pl.program_id(2)

        @pl.when(k == 0)
        def _():
            acc_ref[...] = jnp.zeros_like(acc_ref)

        @pl.when(k % 2 == 0)
        def _():
            acc_ref[...] += lax.dot_general(a_ref[...], g_ref[0], nt, preferred_element_type=F32)

        @pl.when(k % 2 == 1)
        def _():
            acc_ref[...] += lax.dot_general(a_ref[...], u_ref[0], nt, preferred_element_type=F32)

        @pl.when(k == nk - 1)
        def _():
            o_ref[...] = acc_ref[...]

    wspec = pl.BlockSpec((1, tn, tf), lambda i, n, k: (k // 2, n, 0))
    grid = (S // tm, D // tn, nk)
    x_in, x_out, x_shapes, x_scr, x_ops = _host_args(plan)
    outs = pl.pallas_call(
        _host(body, plan, 3, 1, *_grid_ends(grid)), name=name, grid=grid,
        in_specs=[pl.BlockSpec((tm, tf), lambda i, n, k: (i, k)), wspec, wspec] + x_in,
        out_specs=tuple([pl.BlockSpec((tm, tn), lambda i, n, k: (i, n))] + x_out),
        out_shape=tuple([jax.ShapeDtypeStruct((S, D), F32)] + x_shapes),
        scratch_shapes=[pltpu.VMEM((tm, tn), F32)] + x_scr,
        compiler_params=_cp(_semantics(plan, ("parallel", "parallel", "arbitrary")), 40),
    )(dau, g8, u8, *x_ops)
    return outs[0], tuple(outs[1:])


def _ffn_dwgu(h, dau, *, tm, tk, name, plan=None):
    S, D = h.shape
    tf = FF_TILE
    nt = dau.shape[1] // tf
    tm, tk = min(tm, D), min(tk, S)
    nk = S // tk
    ni = D // tm

    def body(a_ref, b_ref, o_ref, acc_ref):
        k = pl.program_id(2)

        @pl.when(k == 0)
        def _():
            acc_ref[...] = jnp.zeros_like(acc_ref)

        acc_ref[...] += lax.dot_general(a_ref[...], b_ref[...], (((0,), (0,)), ((), ())), preferred_element_type=F32)

        @pl.when(k == nk - 1)
        def _():
            o_ref[0, 0] = acc_ref[...].astype(BF16)

    grid = (ni, nt, nk)
    x_in, x_out, x_shapes, x_scr, x_ops = _host_args(plan)
    outs = pl.pallas_call(
        _host(body, plan, 2, 1, *_grid_ends(grid)), name=name, grid=grid,
        in_specs=[pl.BlockSpec((tk, tm), lambda i, j, k: (k, i)), pl.BlockSpec((tk, tf), lambda i, j, k: (k, j))] + x_in,
        out_specs=tuple([pl.BlockSpec((1, 1, tm, tf), lambda i, j, k: ((j // 2) % 2, j // 4, (j % 2) * ni + i, 0))]
                        + x_out),
        out_shape=tuple([jax.ShapeDtypeStruct((2, nt // 4, 2 * D, tf), BF16)] + x_shapes),
        scratch_shapes=[pltpu.VMEM((tm, tf), F32)] + x_scr,
        compiler_params=_cp(_semantics(plan, ("parallel", "parallel", "arbitrary")), 40),
    )(h, dau, *x_ops)
    return outs[0], tuple(outs[1:])


def _ffn_dwd(s, df, *, tn, tk, name, plan=None):
    S, D = df.shape
    tf = FF_TILE
    nb = s.shape[1] // tf
    tn, tk = min(tn, D), min(tk, S)
    nk = S // tk

    def body(a_ref, b_ref, o_ref, acc_ref):
        k = pl.program_id(2)

        @pl.when(k == 0)
        def _():
            acc_ref[...] = jnp.zeros_like(acc_ref)

        acc_ref[...] += lax.dot_general(a_ref[...], b_ref[...], (((0,), (0,)), ((), ())), preferred_element_type=F32)

        @pl.when(k == nk - 1)
        def _():
            o_ref[0, 0] = acc_ref[...].astype(BF16)

    grid = (nb, D // tn, nk)
    x_in, x_out, x_shapes, x_scr, x_ops = _host_args(plan)
    outs = pl.pallas_call(
        _host(body, plan, 2, 1, *_grid_ends(grid)), name=name, grid=grid,
        in_specs=[pl.BlockSpec((tk, tf), lambda j, n, k: (k, j)), pl.BlockSpec((tk, tn), lambda j, n, k: (k, n))] + x_in,
        out_specs=tuple([pl.BlockSpec((1, 1, tf, tn), lambda j, n, k: (j % 2, j // 2, 0, n))] + x_out),
        out_shape=tuple([jax.ShapeDtypeStruct((2, nb // 2, tf, D), BF16)] + x_shapes),
        scratch_shapes=[pltpu.VMEM((tf, tn), F32)] + x_scr,
        compiler_params=_cp(_semantics(plan, ("parallel", "parallel", "arbitrary")), 40),
    )(s, df, *x_ops)
    return outs[0], tuple(outs[1:])


def _ffn_down_bwd(df, d8, au, *, tm, tk, name, plan=None):
    S, D = df.shape
    nb, tn, _ = d8.shape
    F = nb * tn
    tm = min(tm, S)
    tk = min(tk, D)
    nk = D // tk

    def body(df_ref, w_ref, au_ref, dau_ref, acc_ref):
        k = pl.program_id(2)

        @pl.when(k == 0)
        def _():
            acc_ref[...] = jnp.zeros_like(acc_ref)

        acc_ref[...] += lax.dot_general(df_ref[...], w_ref[0], (((1,), (1,)), ((), ())), preferred_element_type=F32)

        @pl.when(k == nk - 1)
        def _():
            ds = acc_ref[...]
            au_v = au_ref[...].astype(F32)
            a = au_v[:, :tn]
            u = au_v[:, tn:]
            sg = jax.nn.sigmoid(a)
            da = ds * u * (sg * (1.0 + a * (1.0 - sg)))
            du = ds * (a * sg)
            dau_ref[:, :tn] = da.astype(BF16)
            dau_ref[:, tn:] = du.astype(BF16)

    grid = (S // tm, F // tn, nk)
    x_in, x_out, x_shapes, x_scr, x_ops = _host_args(plan)
    outs = pl.pallas_call(
        _host(body, plan, 3, 1, *_grid_ends(grid)), name=name, grid=grid,
        in_specs=[pl.BlockSpec((tm, tk), lambda i, j, k: (i, k)), pl.BlockSpec((1, tn, tk), lambda i, j, k: (j, 0, k)),
                  pl.BlockSpec((tm, 2 * tn), lambda i, j, k: (i, j))] + x_in,
        out_specs=tuple([pl.BlockSpec((tm, 2 * tn), lambda i, j, k: (i, j))] + x_out),
        out_shape=tuple([jax.ShapeDtypeStruct((S, 2 * F), BF16)] + x_shapes),
        scratch_shapes=[pltpu.VMEM((tm, tn), F32)] + x_scr,
        compiler_params=_cp(_semantics(plan, ("parallel", "parallel", "arbitrary")), 52),
    )(df, d8, au, *x_ops)
    return outs[0], tuple(outs[1:])


def _fold8(x):
    tm, w = x.shape
    return jnp.sum(x.reshape(tm // 8, 8, w), axis=0)


def _row_spec(tm, w):
    return pl.BlockSpec((tm, w), lambda i: (i, 0))


def _vec_spec(rows, w):
    return pl.BlockSpec((rows, w), lambda i: (0, 0))


def _pre_norm_mod(x, gain, shift, scale1p, *, name):
    S, D = x.shape
    tm = ROW_TILE

    def body(x_ref, g_ref, sh_ref, sc_ref, h_ref):
        xv = x_ref[...]
        rinv = lax.rsqrt(jnp.mean(xv * xv, axis=-1, keepdims=True) + NORM_EPS)
        h_ref[...] = ((xv * rinv) * g_ref[...] * sc_ref[...] + sh_ref[...]).astype(BF16)

    return pl.pallas_call(
        body, name=name, grid=(S // tm,),
        in_specs=[_row_spec(tm, D), _vec_spec(1, D), _vec_spec(1, D), _vec_spec(1, D)],
        out_specs=_row_spec(tm, D), out_shape=jax.ShapeDtypeStruct((S, D), BF16),
        compiler_params=_cp(("parallel",), 32),
    )(x, gain, shift, scale1p)


def _pre_norm_mod_bwd(dh, x, dres, gain, scale1p, *, name):
    S, D = x.shape
    tm = ROW_TILE
    n = S // tm

    def body(dh_ref, x_ref, dr_ref, g_ref, sc_ref, dx_ref, sums_ref, acc_ref):
        i = pl.program_id(0)

        @pl.when(i == 0)
        def _():
            acc_ref[...] = jnp.zeros_like(acc_ref)

        xv = x_ref[...]
        dhv = dh_ref[...]
        g = g_ref[...]
        rinv = lax.rsqrt(jnp.mean(xv * xv, axis=-1, keepdims=True) + NORM_EPS)
        xn = xv * rinv
        dn = dhv * sc_ref[...]
        dxn = dn * g
        dx_ref[...] = dr_ref[...] + rinv * (dxn - xn * jnp.mean(dxn * xn, axis=-1, keepdims=True))
        acc_ref[0] += _fold8(dhv)
        acc_ref[1] += _fold8(dhv * (xn * g))
        acc_ref[2] += _fold8(dn * xn)

        @pl.when(i == n - 1)
        def _():
            for q in range(3):
                sums_ref[q:q + 1, :] = jnp.sum(acc_ref[q], axis=0, keepdims=True)

    return pl.pallas_call(
        body, name=name, grid=(n,),
        in_specs=[_row_spec(tm, D), _row_spec(tm, D), _row_spec(tm, D), _vec_spec(1, D), _vec_spec(1, D)],
        out_specs=(_row_spec(tm, D), _vec_spec(3, D)),
        out_shape=(jax.ShapeDtypeStruct((S, D), F32), jax.ShapeDtypeStruct((3, D), F32)),
        scratch_shapes=[pltpu.VMEM((3, 8, D), F32)],
        compiler_params=_cp(("arbitrary",), 40),
    )(dh, x, dres, gain, scale1p)


def _post_norm_res(x, f, gain, gw, *, name):
    S, D = x.shape
    tm = ROW_TILE

    def body(x_ref, f_ref, g_ref, gw_ref, o_ref):
        fv = f_ref[...]
        rinv = lax.rsqrt(jnp.mean(fv * fv, axis=-1, keepdims=True) + NORM_EPS)
        o_ref[...] = x_ref[...] + gw_ref[...] * ((fv * rinv) * g_ref[...])

    return pl.pallas_call(
        body, name=name, grid=(S // tm,),
        in_specs=[_row_spec(tm, D), _row_spec(tm, D), _vec_spec(1, D), _vec_spec(1, D)],
        out_specs=_row_spec(tm, D), out_shape=jax.ShapeDtypeStruct((S, D), F32),
        compiler_params=_cp(("parallel",), 32),
    )(x, f, gain, gw)


def _post_norm_res_bwd(dxo, f, gain, gw, weight, *, name):
    S, D = f.shape
    tm = ROW_TILE
    n = S // tm

    def body(d_ref, f_ref, g_ref, gw_ref, df_ref, sums_ref, acc_ref):
        i = pl.program_id(0)

        @pl.when(i == 0)
        def _():
            acc_ref[...] = jnp.zeros_like(acc_ref)

        fv = f_ref[...]
        dv = d_ref[...]
        g = g_ref[...]
        rinv = lax.rsqrt(jnp.mean(fv * fv, axis=-1, keepdims=True) + NORM_EPS)
        fh = fv * rinv
        dy = dv * gw_ref[...]
        dfh = dy * g
        df_ref[...] = (rinv * (dfh - fh * jnp.mean(dfh * fh, axis=-1, keepdims=True))).astype(BF16)
        acc_ref[0] += _fold8(weight * dv * (fh * g))
        acc_ref[1] += _fold8(dy * fh)

        @pl.when(i == n - 1)
        def _():
            for q in range(2):
                sums_ref[q:q + 1, :] = jnp.sum(acc_ref[q], axis=0, keepdims=True)

    return pl.pallas_call(
        body, name=name, grid=(n,),
        in_specs=[_row_spec(tm, D), _row_spec(tm, D), _vec_spec(1, D), _vec_spec(1, D)],
        out_specs=(_row_spec(tm, D), _vec_spec(2, D)),
        out_shape=(jax.ShapeDtypeStruct((S, D), BF16), jax.ShapeDtypeStruct((2, D), F32)),
        scratch_shapes=[pltpu.VMEM((2, 8, D), F32)],
        compiler_params=_cp(("arbitrary",), 40),
    )(dxo, f, gain, gw)


def _loss_head(y, target, *, name):
    S, D = y.shape
    tm = ROW_TILE

    def body(y_ref, t_ref, l_ref, dy_ref):
        i = pl.program_id(0)

        @pl.when(i == 0)
        def _():
            l_ref[...] = jnp.zeros_like(l_ref)

        err = y_ref[...] - t_ref[...]
        dy_ref[...] = err * (1.0 / D)
        row = jnp.sum(err * err, axis=-1, keepdims=True) * (0.5 / D)
        l_ref[...] += jnp.sum(row, axis=0, keepdims=True)

    return pl.pallas_call(
        body, name=name, grid=(S // tm,),
        in_specs=[_row_spec(tm, D), _row_spec(tm, D)],
        out_specs=(_vec_spec(1, 1), _row_spec(tm, D)),
        out_shape=(jax.ShapeDtypeStruct((1, 1), F32), jax.ShapeDtypeStruct((S, D), F32)),
        compiler_params=_cp(("arbitrary",), 32),
    )(y, target)


def _shift_down(z, j, row):
    return jnp.where(row >= j, pltpu.roll(z, j, 0), 0.0)


def _shift_up(z, j, row, n):
    return jnp.where(row < n - j, pltpu.roll(z, n - j, 0), 0.0)


def _pool_fwd(p, pool_w, pool_scale, *, name):
    S = p.shape[0]
    C = POOL_GROUP

    def body(u_ref, w_ref, sc_ref, o_ref, y_ref):
        g = pl.program_id(0)
        u = u_ref[...]
        row = lax.broadcasted_iota(jnp.int32, (S, C), 0)
        s1 = u + _shift_down(u, 1, row)
        s2 = s1 + _shift_down(s1, 2, row)
        s3 = s2 + _shift_down(s2, 4, row)
        s4 = s3 + _shift_down(s3, 8, row)
        gi = jnp.zeros((S, C), jnp.int32) + g
        win = jnp.where(gi == 0, s1, jnp.where(gi == 1, s2, jnp.where(gi == 2, s3, s4)))
        width = jnp.where(gi == 0, 2, jnp.where(gi == 1, 4, jnp.where(gi == 2, 8, 16)))
        count = jnp.minimum(row + 1, width).astype(F32)
        o = win / count - u
        o_ref[...] = o
        y_ref[...] = jnp.dot(o.astype(BF16), w_ref[0].astype(BF16), preferred_element_type=F32) * sc_ref[...]

    col = pl.BlockSpec((S, C), lambda g: (0, g))
    return pl.pallas_call(
        body, name=name, grid=(POOL_GROUPS,),
        in_specs=[col, pl.BlockSpec((1, C, C), lambda g: (g, 0, 0)), pl.BlockSpec((1, C), lambda g: (0, g))],
        out_specs=(col, col),
        out_shape=(jax.ShapeDtypeStruct((S, POOL_GROUPS * C), F32), jax.ShapeDtypeStruct((S, POOL_GROUPS * C), F32)),
        compiler_params=_cp(("parallel",), 48),
    )(p, pool_w, pool_scale)


def _pool_bwd(dcat, o, pool_w, pool_scale, *, name):
    S = o.shape[0]
    C = POOL_GROUP

    def body(dy_ref, o_ref, w_ref, sc_ref, du_ref, dw_ref, dsc_ref):
        g = pl.program_id(0)
        dy = dy_ref[...]
        ob = o_ref[...].astype(BF16)
        wb = w_ref[0].astype(BF16)
        mixed = jnp.dot(ob, wb, preferred_element_type=F32)
        dsc_ref[...] = jnp.sum(_fold8(dy * mixed), axis=0, keepdims=True)
        dmix = (dy * sc_ref[...]).astype(BF16)
        dw_ref[0] = lax.dot_general(ob, dmix, (((0,), (0,)), ((), ())), preferred_element_type=F32)
        do = lax.dot_general(dmix, wb, (((1,), (1,)), ((), ())), preferred_element_type=F32)
        row = lax.broadcasted_iota(jnp.int32, (S, C), 0)
        gi = jnp.zeros((S, C), jnp.int32) + g
        width = jnp.where(gi == 0, 2, jnp.where(gi == 1, 4, jnp.where(gi == 2, 8, 16)))
        z = do / jnp.minimum(row + 1, width).astype(F32)
        s1 = z + _shift_up(z, 1, row, S)
        s2 = s1 + _shift_up(s1, 2, row, S)
        s3 = s2 + _shift_up(s2, 4, row, S)
        s4 = s3 + _shift_up(s3, 8, row, S)
        win = jnp.where(gi == 0, s1, jnp.where(gi == 1, s2, jnp.where(gi == 2, s3, s4)))
        du_ref[...] = (win - do).astype(BF16)

    col = pl.BlockSpec((S, C), lambda g: (0, g))
    return pl.pallas_call(
        body, name=name, grid=(POOL_GROUPS,),
        in_specs=[col, col, pl.BlockSpec((1, C, C), lambda g: (g, 0, 0)), pl.BlockSpec((1, C), lambda g: (0, g))],
        out_specs=(col, pl.BlockSpec((1, C, C), lambda g: (g, 0, 0)), pl.BlockSpec((1, C), lambda g: (0, g))),
        out_shape=(jax.ShapeDtypeStruct((S, POOL_GROUPS * C), BF16), jax.ShapeDtypeStruct((POOL_GROUPS, C, C), F32),
                   jax.ShapeDtypeStruct((1, POOL_GROUPS * C), F32)),
        compiler_params=_cp(("parallel",), 48),
    )(dcat, o, pool_w, pool_scale)


def _block_ones():
    r = lax.broadcasted_iota(jnp.int32, (128, 128), 0) // HEAD
    c = lax.broadcasted_iota(jnp.int32, (128, 128), 1) // HEAD
    return jnp.where(r == c, 1.0, 0.0).astype(BF16)


def _segsum(x, bd):
    outs = []
    for j in range(x.shape[1] // 128):
        xs = x[:, j * 128:(j + 1) * 128]
        hi = xs.astype(BF16)
        lo = (xs - hi.astype(F32)).astype(BF16)
        outs.append(jnp.dot(hi, bd, preferred_element_type=F32) + jnp.dot(lo, bd, preferred_element_type=F32))
    return jnp.concatenate(outs, axis=1)


def _prep_common(q, qprev, first, mu, wl, w0, a0, kkw, kaw, R):
    tm = q.shape[0]
    row = lax.broadcasted_iota(jnp.int32, q.shape, 0)
    last = qprev[7:8, :] * first
    prev = jnp.where(row == 0, last, pltpu.roll(q, 1, 0))
    ps = q + mu * (prev - q)
    r = ps[:, 0:R]
    k = ps[:, R:2 * R]
    v = ps[:, 2 * R:3 * R]
    lo_in = ps[:, 3 * R:3 * R + LORA_PAD]
    lane = lax.broadcasted_iota(jnp.int32, (tm, LORA_PAD), 1)
    m_w = lane < LORA_W
    m_a = lane < LORA_W + LORA_A
    m_g = lane < LORA_W + LORA_A + LORA_G
    act = jnp.where(m_w, jnp.tanh(lo_in), jnp.where(m_a, lo_in, jnp.where(m_g, jax.nn.sigmoid(lo_in), 0.0)))
    lo = jnp.dot(act.astype(BF16), wl, preferred_element_type=F32)
    wpre = w0 + lo[:, 0:R]
    apre = a0 + lo[:, R:2 * R]
    g = lo[:, 2 * R:3 * R]
    neg = -wpre
    softplus = jnp.maximum(neg, 0.0) + jnp.log(1.0 + jnp.exp(-jnp.abs(neg)))
    wlog = -softplus - 0.5
    ew = jnp.exp(wlog)
    decay = jnp.exp(-ew)
    a = jax.nn.sigmoid(apre)
    kk = k * kkw
    bd = _block_ones()
    n2 = _segsum(kk * kk, bd)
    nrm = jnp.maximum(jnp.sqrt(n2), 1e-12)
    kap = kk / nrm
    kmul = 1.0 + (a - 1.0) * kaw
    k2 = k * kmul
    return dict(prev=prev, r=r, k=k, v=v, act=act, m_w=m_w, m_a=m_a, m_g=m_g, wpre=wpre, g=g, ew=ew, decay=decay,
                a=a, n2=n2, nrm=nrm, kap=kap, kmul=kmul, k2=k2, bd=bd)


def _prev_rows_spec(tm, w):
    return pl.BlockSpec((8, w), lambda i: (jnp.maximum(i * (tm // 8) - 1, 0), 0))


def _rwkv_prep(q, mu, wl, w0, a0, kkw, kaw, *, name):
    S, QW = q.shape
    R = w0.shape[1]
    tm = ROW_TILE // 2

    def body(q_ref, qp_ref, mu_ref, wl_ref, w0_ref, a0_ref, kk_ref, ka_ref, r_ref, w_ref, k_ref, v_ref, kap_ref,
             b_ref, g_ref):
        first = jnp.where(pl.program_id(0) > 0, 1.0, 0.0)
        t = _prep_common(q_ref[...], qp_ref[...], first, mu_ref[...], wl_ref[...], w0_ref[...], a0_ref[...],
                         kk_ref[...], ka_ref[...], R)
        r_ref[...] = t["r"]
        w_ref[...] = t["decay"]
        k_ref[...] = t["k2"]
        v_ref[...] = t["v"]
        kap_ref[...] = t["kap"]
        b_ref[...] = t["kap"] * t["a"]
        g_ref[...] = t["g"]

    vec = _vec_spec(1, R)
    return pl.pallas_call(
        body, name=name, grid=(S // tm,),
        in_specs=[_row_spec(tm, QW), _prev_rows_spec(tm, QW), _vec_spec(1, QW), _vec_spec(LORA_PAD, 3 * R), vec, vec,
                  vec, vec],
        out_specs=tuple([_row_spec(tm, R)] * 7),
        out_shape=tuple([jax.ShapeDtypeStruct((S, R), F32)] * 7),
        compiler_params=_cp(("parallel",), 48),
    )(q, q, mu, wl, w0, a0, kkw, kaw)


def _rwkv_prep_bwd(q, mu, wl, w0, a0, kkw, kaw, grads, *, name):
    S, QW = q.shape
    R = w0.shape[1]
    tm = ROW_TILE // 2
    n = S // tm

    def body(q_ref, qp_ref, mu_ref, wl_ref, w0_ref, a0_ref, kk_ref, ka_ref, dr_ref, dw_ref, dk2_ref, dv_ref, dkap_ref,
             db_ref, dg_ref, drb_ref, dk2b_ref, dvb_ref, dps_ref, dwl_ref, sums_ref, acc_ref):
        i = pl.program_id(0)

        @pl.when(i == 0)
        def _():
            acc_ref[...] = jnp.zeros_like(acc_ref)
            dwl_ref[...] = jnp.zeros_like(dwl_ref)

        first = jnp.where(i > 0, 1.0, 0.0)
        wl = wl_ref[...]
        kkw = kk_ref[...]
        kaw = ka_ref[...]
        t = _prep_common(q_ref[...], qp_ref[...], first, mu_ref[...], wl, w0_ref[...], a0_ref[...], kkw, kaw, R)
        a, kap, k, act = t["a"], t["kap"], t["k"], t["act"]
        db = db_ref[...]
        dk2 = dk2_ref[...] + dk2b_ref[...]
        dkap = dkap_ref[...] + db * a
        da = db * kap + dk2 * k * kaw
        dk = dk2 * t["kmul"]
        proj = jnp.where(jnp.sqrt(t["n2"]) > 1e-12, _segsum(kap * dkap, t["bd"]), 0.0)
        dkk = (dkap - kap * proj) / t["nrm"]
        dk = dk + dkk * kkw
        dapre = da * a * (1.0 - a)
        dwlog = dw_ref[...] * t["decay"] * (-t["ew"])
        dwpre = dwlog * jax.nn.sigmoid(-t["wpre"])
        acc_ref[0] += _fold8(dwpre)
        acc_ref[1] += _fold8(dapre)
        acc_ref[2] += _fold8(dkk * k)
        acc_ref[3] += _fold8(dk2 * k * (a - 1.0))
        dlo = jnp.concatenate([dwpre, dapre, dg_ref[...]], axis=1).astype(BF16)
        dwl_ref[...] += lax.dot_general(act.astype(BF16), dlo, (((0,), (0,)), ((), ())), preferred_element_type=F32)
        dact = lax.dot_general(dlo, wl, (((1,), (1,)), ((), ())), preferred_element_type=F32)
        dlin = jnp.where(t["m_w"], dact * (1.0 - act * act),
                         jnp.where(t["m_a"], dact, jnp.where(t["m_g"], dact * act * (1.0 - act), 0.0)))
        dps_ref[:, 0:R] = dr_ref[...] + drb_ref[...]
        dps_ref[:, R:2 * R] = dk
        dps_ref[:, 2 * R:3 * R] = dv_ref[...] + dvb_ref[...]
        dps_ref[:, 3 * R:3 * R + LORA_PAD] = dlin
        dps_ref[:, 3 * R + LORA_PAD:] = jnp.zeros((tm, QW - 3 * R - LORA_PAD), F32)

        @pl.when(i == n - 1)
        def _():
            for j in range(4):
                sums_ref[j:j + 1, :] = jnp.sum(acc_ref[j], axis=0, keepdims=True)

    vec = _vec_spec(1, R)
    return pl.pallas_call(
        body, name=name, grid=(n,),
        in_specs=[_row_spec(tm, QW), _prev_rows_spec(tm, QW), _vec_spec(1, QW), _vec_spec(LORA_PAD, 3 * R), vec, vec,
                  vec, vec] + [_row_spec(tm, R)] * 10,
        out_specs=(_row_spec(tm, QW), _vec_spec(LORA_PAD, 3 * R), _vec_spec(4, R)),
        out_shape=(jax.ShapeDtypeStruct((S, QW), F32), jax.ShapeDtypeStruct((LORA_PAD, 3 * R), F32),
                   jax.ShapeDtypeStruct((4, R), F32)),
        scratch_shapes=[pltpu.VMEM((4, 8, R), F32)],
        compiler_params=_cp(("arbitrary",), 56),
    )(q, q, mu, wl, w0, a0, kkw, kaw, *grads)


def _tshift_bwd(dps, q, mu, *, name):
    S, QW = q.shape
    tm = ROW_TILE // 2
    n = S // tm

    def body(d_ref, dn_ref, q_ref, qp_ref, mu_ref, dq_ref, dmu_ref, acc_ref):
        i = pl.program_id(0)

        @pl.when(i == 0)
        def _():
            acc_ref[...] = jnp.zeros_like(acc_ref)

        mu = mu_ref[...]
        d = d_ref[...]
        qv = q_ref[...]
        row = lax.broadcasted_iota(jnp.int32, d.shape, 0)
        first = jnp.where(i > 0, 1.0, 0.0)
        notlast = jnp.where(i < n - 1, 1.0, 0.0)
        prev = jnp.where(row == 0, qp_ref[7:8, :] * first, pltpu.roll(qv, 1, 0))
        z = d * mu
        nxt = jnp.where(row == tm - 1, dn_ref[0:1, :] * mu * notlast, pltpu.roll(z, tm - 1, 0))
        dq_ref[...] = (d * (1.0 - mu) + nxt).astype(BF16)
        acc_ref[...] += _fold8(d * (prev - qv))

        @pl.when(i == n - 1)
        def _():
            dmu_ref[...] = jnp.sum(acc_ref[...], axis=0, keepdims=True)

    nblk8 = S // 8
    next_spec = pl.BlockSpec((8, QW), lambda i: (jnp.minimum((i + 1) * (tm // 8), nblk8 - 1), 0))
    return pl.pallas_call(
        body, name=name, grid=(n,),
        in_specs=[_row_spec(tm, QW), next_spec, _row_spec(tm, QW), _prev_rows_spec(tm, QW), _vec_spec(1, QW)],
        out_specs=(_row_spec(tm, QW), _vec_spec(1, QW)),
        out_shape=(jax.ShapeDtypeStruct((S, QW), BF16), jax.ShapeDtypeStruct((1, QW), F32)),
        scratch_shapes=[pltpu.VMEM((8, QW), F32)],
        compiler_params=_cp(("arbitrary",), 48),
    )(dps, dps, q, q, mu)


def _post_common(ysc, r, k2, v, lnw, lnb, rk):
    bd = _block_ones()
    mean = _segsum(ysc, bd) * (1.0 / HEAD)
    d = ysc - mean
    var = _segsum(d * d, bd) * (1.0 / HEAD)
    rstd = lax.rsqrt(var + LN_X_EPS)
    yh = d * rstd
    rkk = _segsum(r * k2 * rk, bd)
    z = yh * lnw + lnb + rkk * v
    return bd, rstd, yh, rkk, z


def _rwkv_post(ysc, r, k2, v, g, ypool, lnw, lnb, rk, *, name):
    S, R = ysc.shape
    PW = ypool.shape[1]
    tm = ROW_TILE

    def body(y_ref, r_ref, k_ref, v_ref, g_ref, yp_ref, lw_ref, lb_ref, rk_ref, cat_ref):
        _, _, _, _, z = _post_common(y_ref[...], r_ref[...], k_ref[...], v_ref[...], lw_ref[...], lb_ref[...],
                                     rk_ref[...])
        cat_ref[:, 0:PW] = yp_ref[...].astype(BF16)
        cat_ref[:, PW:] = (z * g_ref[...]).astype(BF16)

    vec = _vec_spec(1, R)
    return pl.pallas_call(
        body, name=name, grid=(S // tm,),
        in_specs=[_row_spec(tm, R)] * 5 + [_row_spec(tm, PW), vec, vec, vec],
        out_specs=_row_spec(tm, PW + R), out_shape=jax.ShapeDtypeStruct((S, PW + R), BF16),
        compiler_params=_cp(("parallel",), 48),
    )(ysc, r, k2, v, g, ypool, lnw, lnb, rk)


def _rwkv_post_bwd(dcat, ysc, r, k2, v, g, lnw, lnb, rk, *, name):
    S, R = ysc.shape
    tm = ROW_TILE
    n = S // tm

    def body(d_ref, y_ref, r_ref, k_ref, v_ref, g_ref, lw_ref, lb_ref, rk_ref, dy_ref, dg_ref, drb_ref, dkb_ref,
             dvb_ref, sums_ref, acc_ref):
        i = pl.program_id(0)

        @pl.when(i == 0)
        def _():
            acc_ref[...] = jnp.zeros_like(acc_ref)

        rv, kv, vv, lw, rkw = r_ref[...], k_ref[...], v_ref[...], lw_ref[...], rk_ref[...]
        bd, rstd, yh, rkk, z = _post_common(y_ref[...], rv, kv, vv, lw, lb_ref[...], rkw)
        dyr = d_ref[...]
        dg_ref[...] = dyr * z
        dz = dyr * g_ref[...]
        dyh = dz * lw
        dy_ref[...] = rstd * (dyh - _segsum(dyh, bd) * (1.0 / HEAD) - yh * (_segsum(dyh * yh, bd) * (1.0 / HEAD)))
        dvb_ref[...] = dz * rkk
        drkk = _segsum(dz * vv, bd)
        drb_ref[...] = drkk * kv * rkw
        dkb_ref[...] = drkk * rv * rkw
        acc_ref[0] += _fold8(dz * yh)
        acc_ref[1] += _fold8(dz)
        acc_ref[2] += _fold8(drkk * rv * kv)

        @pl.when(i == n - 1)
        def _():
            for j in range(3):
                sums_ref[j:j + 1, :] = jnp.sum(acc_ref[j], axis=0, keepdims=True)

    vec = _vec_spec(1, R)
    dspec = _row_spec(tm, R)
    return pl.pallas_call(
        body, name=name, grid=(n,),
        in_specs=[dspec] + [_row_spec(tm, R)] * 5 + [vec, vec, vec],
        out_specs=tuple([_row_spec(tm, R)] * 5) + (_vec_spec(3, R),),
        out_shape=tuple([jax.ShapeDtypeStruct((S, R), F32)] * 5) + (jax.ShapeDtypeStruct((3, R), F32),),
        scratch_shapes=[pltpu.VMEM((3, 8, R), F32)],
        compiler_params=_cp(("arbitrary",), 56),
    )(dcat, ysc, r, k2, v, g, lnw, lnb, rk)


SEL_ROWS = 64


def _column_selector():
    row = lax.broadcasted_iota(jnp.int32, (SEL_ROWS, 8 * 128), 0)
    col = lax.broadcasted_iota(jnp.int32, (SEL_ROWS, 8 * 128), 1)
    head, rest = row // 32, row % 32
    hit = (rest < 24) & (rest % 8 == col // 128) & (head == (col % 128) // HEAD)
    return jnp.where(hit, 1.0, 0.0).astype(BF16)


def _expand_columns(x, sel):
    hi = x.astype(BF16).astype(F32)
    r1 = x - hi
    mid = r1.astype(BF16).astype(F32)
    lo = (r1 - mid).astype(BF16).astype(F32)
    terms = jnp.concatenate([hi, mid, lo, jnp.zeros_like(x)], axis=0)
    both = jnp.concatenate([terms, pltpu.roll(terms, HEAD, 1)], axis=0)[:, 0:HEAD]
    return lax.dot_general(both.astype(BF16), sel, (((0,), (0,)), ((), ())), preferred_element_type=F32)


def _head_sum_weights():
    row = lax.broadcasted_iota(jnp.int32, (256, 256), 0)
    col = lax.broadcasted_iota(jnp.int32, (256, 256), 1)
    return jnp.where((row % 128) // HEAD == col // 128, 1.0, 0.0).astype(BF16)


def _head_sums_mxu(products, w2):
    rows = []
    for p in products:
        hi = p.astype(BF16)
        rows.append(jnp.concatenate([hi, (p - hi.astype(F32)).astype(BF16)], axis=1))
    out = jnp.dot(jnp.concatenate(rows, axis=0), w2, preferred_element_type=F32)
    return [(out[i * HEAD:(i + 1) * HEAD, 0:128], out[i * HEAD:(i + 1) * HEAD, 128:256]) for i in range(len(products))]


def _masked_rows(rows, negate=False):
    head_a = (lax.broadcasted_iota(jnp.int32, rows.shape, 1) % 128) < HEAD
    v = -rows if negate else rows
    return jnp.where(head_a, v, 0.0), jnp.where(head_a, 0.0, v)


def _lane_sums(x, row_a, row_b):
    return jnp.sum(x * row_a, axis=1, keepdims=True), jnp.sum(x * row_b, axis=1, keepdims=True)


def _scan_fwd(r, w, k, v, kap, b, *, name, plan=None):
    S, R = r.shape
    G, T = SCAN_G, SCAN_T
    NP = R // 128
    assert NP % G == 0 and S % T == 0
    GW = 128 * G

    def body(r_ref, w_ref, k_ref, v_ref, kap_ref, b_ref, sel_ref, w2_ref, y_ref, sa_ref, st_ref, s_scr, vc_scr,
             yt_scr, sat_scr):
        c = pl.program_id(1)

        @pl.when(c == 0)
        def _():
            s_scr[...] = jnp.zeros_like(s_scr)

        yt_scr[...] = jnp.zeros_like(yt_scr)
        sat_scr[...] = jnp.zeros_like(sat_scr)
        lane = lax.broadcasted_iota(jnp.int32, (HEAD, 128), 1)
        m_a = lane < HEAD

        def block(tb, carry):
            t0 = pl.multiple_of(tb * 8, 8)
            rb, wb, kb = r_ref[pl.ds(t0, 8), :], w_ref[pl.ds(t0, 8), :], k_ref[pl.ds(t0, 8), :]
            pb, bb, vb = kap_ref[pl.ds(t0, 8), :], b_ref[pl.ds(t0, 8), :], v_ref[pl.ds(t0, 8), :]
            for g in range(G):
                vc_scr[g] = _expand_columns(vb[:, g * 128:(g + 1) * 128], sel_ref[...])

            def put_y(g, parts, hot_y):
                yt_scr[g, 0:HEAD, :] = jnp.where(hot_y, parts[0], yt_scr[g, 0:HEAD, :])
                yt_scr[g, HEAD:, :] = jnp.where(hot_y, parts[1], yt_scr[g, HEAD:, :])

            def put_sa(g, parts, hot_t):
                sat_scr[g, 0:HEAD, :] = jnp.where(hot_t, parts[0], sat_scr[g, 0:HEAD, :])
                sat_scr[g, HEAD:, :] = jnp.where(hot_t, parts[1], sat_scr[g, HEAD:, :])

            npa, npb = _masked_rows(pb, negate=True)
            for j in range(8):
                t = t0 + j
                cols = slice(j * 128, (j + 1) * 128)
                sa_parts, products = [], []
                for g in range(G):
                    sl = slice(g * 128, (g + 1) * 128)
                    sa_parts.append(_lane_sums(s_scr[g], npa[j:j + 1, sl], npb[j:j + 1, sl]))
                hot_t = lane == t
                for g in range(G):
                    sl = slice(g * 128, (g + 1) * 128)
                    sa = jnp.where(m_a, sa_parts[g][0], sa_parts[g][1])
                    st = s_scr[g] * wb[j:j + 1, sl] + sa * bb[j:j + 1, sl] + vc_scr[g, :, cols] * kb[j:j + 1, sl]
                    s_scr[g] = st
                    st_ref[g, t] = st
                    put_sa(g, sa_parts[g], hot_t)
                    products.append(st * rb[j:j + 1, sl])
                for g, parts in enumerate(_head_sums_mxu(products, w2_ref[...])):
                    put_y(g, parts, hot_t)
            return carry

        lax.fori_loop(0, T // 8, block, 0)
        for g in range(G):
            y_ref[:, g * 128:(g + 1) * 128] = yt_scr[g].T[0:T, :]
            sa_ref[:, g * 128:(g + 1) * 128] = sat_scr[g].T[0:T, :]

    tspec = pl.BlockSpec((T, GW), lambda p, c: (c, p))
    sel_spec = pl.BlockSpec((SEL_ROWS, 8 * 128), lambda p, c: (0, 0))
    grid = (NP // G, S // T)
    x_in, x_out, x_shapes, x_scr, x_ops = _host_args(plan)
    w2_spec = pl.BlockSpec((256, 256), lambda p, c: (0, 0))
    outs = pl.pallas_call(
        _host(body, plan, 8, 3, *_grid_ends(grid)), name=name, grid=grid,
        in_specs=[tspec] * 6 + [sel_spec, w2_spec] + x_in,
        out_specs=tuple([tspec, tspec, pl.BlockSpec((G, T, HEAD, 128), lambda p, c: (p, c, 0, 0))] + x_out),
        out_shape=tuple([jax.ShapeDtypeStruct((S, R), F32), jax.ShapeDtypeStruct((S, R), F32),
                         jax.ShapeDtypeStruct((NP, S, HEAD, 128), F32)] + x_shapes),
        scratch_shapes=[pltpu.VMEM((G, HEAD, 128), F32), pltpu.VMEM((G, HEAD, 8 * 128), F32),
                        pltpu.VMEM((G, 128, 128), F32), pltpu.VMEM((G, 128, 128), F32)] + x_scr,
        compiler_params=_cp(_semantics(plan, ("parallel", "arbitrary")), 48),
    )(r, w, k, v, kap, b, _column_selector(), _head_sum_weights(), *x_ops)
    return outs[0], outs[1], outs[2], tuple(outs[3:])


def _scan_bwd(r, w, k, v, kap, b, sa, dy, states, *, name, plan=None):
    S, R = r.shape
    G, T = SCAN_G_BWD, SCAN_T_BWD
    NP = R // 128
    NC = S // T
    GW = 128 * G
    assert NP % G == 0

    def body(r_ref, w_ref, k_ref, v_ref, kap_ref, b_ref, sa_ref, dy_ref, st_ref, sp_ref, sel_ref, w2_ref, dr_ref,
             dw_ref, dk_ref, dv_ref, dkap_ref, db_ref, ds_scr, vc_scr, dyc_scr, sac_scr, dvt_scr):
        ci = pl.program_id(1)

        @pl.when(ci == 0)
        def _():
            ds_scr[...] = jnp.zeros_like(ds_scr)

        dvt_scr[...] = jnp.zeros_like(dvt_scr)
        lane = lax.broadcasted_iota(jnp.int32, (HEAD, 128), 1)
        m_a = lane < HEAD
        sub = lax.broadcasted_iota(jnp.int32, (8, 128), 0)
        zero_i = jnp.zeros((HEAD, 128), jnp.int32)
        has_prev = jnp.where(ci < NC - 1, 1.0, 0.0)

        def state_before(g, t):
            at_start = (zero_i + t) == 0
            return jnp.where(at_start, sp_ref[g, 0] * has_prev, st_ref[g, jnp.maximum(t - 1, 0)])

        def block(it, carry):
            tb = T // 8 - 1 - it
            t0 = pl.multiple_of(tb * 8, 8)
            rb, wb, kb = r_ref[pl.ds(t0, 8), :], w_ref[pl.ds(t0, 8), :], k_ref[pl.ds(t0, 8), :]
            pb, bb = kap_ref[pl.ds(t0, 8), :], b_ref[pl.ds(t0, 8), :]
            vb, dyb, sab = v_ref[pl.ds(t0, 8), :], dy_ref[pl.ds(t0, 8), :], sa_ref[pl.ds(t0, 8), :]
            for g in range(G):
                sl = slice(g * 128, (g + 1) * 128)
                vc_scr[g] = _expand_columns(vb[:, sl], sel_ref[...])
                dyc_scr[g] = _expand_columns(dyb[:, sl], sel_ref[...])
                sac_scr[g] = _expand_columns(sab[:, sl], sel_ref[...])
            outs = [[jnp.zeros((8, 128), F32) for _ in range(5)] for _ in range(G)]
            bba, bbb = _masked_rows(bb)
            def recurrence(j):
                cols = slice(j * 128, (j + 1) * 128)
                dsp, dsa_parts = [], []
                for g in range(G):
                    sl = slice(g * 128, (g + 1) * 128)
                    ds = ds_scr[g] + dyc_scr[g, :, cols] * rb[j:j + 1, sl]
                    dsp.append(ds)
                    dsa_parts.append(_lane_sums(ds, bba[j:j + 1, sl], bbb[j:j + 1, sl]))
                dsas = []
                for g in range(G):
                    sl = slice(g * 128, (g + 1) * 128)
                    dsa = jnp.where(m_a, dsa_parts[g][0], dsa_parts[g][1])
                    ds_scr[g] = dsp[g] * wb[j:j + 1, sl] - dsa * pb[j:j + 1, sl]
                    dsas.append(dsa)
                return dsp, dsas

            def side_work(j, dsp, dsas):
                t = t0 + j
                hot = lane == t
                cols = slice(j * 128, (j + 1) * 128)
                dvs = _head_sums_mxu([dsp[g] * kb[j:j + 1, g * 128:(g + 1) * 128] for g in range(G)], w2_ref[...])
                for g in range(G):
                    ds = dsp[g]
                    s_p = st_ref[g, t - 1] if j > 0 else state_before(g, t)
                    dr_row = jnp.sum(st_ref[g, t] * dyc_scr[g, :, cols], axis=0, keepdims=True)
                    dk_row = jnp.sum(ds * vc_scr[g, :, cols], axis=0, keepdims=True)
                    db_row = jnp.sum(ds * sac_scr[g, :, cols], axis=0, keepdims=True)
                    dw_row = jnp.sum(ds * s_p, axis=0, keepdims=True)
                    dkap_row = -jnp.sum(s_p * dsas[g], axis=0, keepdims=True)
                    dvt_scr[g, 0:HEAD, :] = jnp.where(hot, dvs[g][0], dvt_scr[g, 0:HEAD, :])
                    dvt_scr[g, HEAD:, :] = jnp.where(hot, dvs[g][1], dvt_scr[g, HEAD:, :])
                    pick = sub == j
                    for q, row in enumerate((dr_row, dw_row, dk_row, dkap_row, db_row)):
                        outs[g][q] = jnp.where(pick, row, outs[g][q])

            pending = None
            for j in range(7, -1, -1):
                done = recurrence(j)
                if pending is not None:
                    side_work(*pending)
                pending = (j, *done)
            side_work(*pending)
            for g in range(G):
                sl = slice(g * 128, (g + 1) * 128)
                for q, ref in enumerate((dr_ref, dw_ref, dk_ref, dkap_ref, db_ref)):
                    ref[pl.ds(t0, 8), sl] = outs[g][q]
            return carry

        lax.fori_loop(0, T // 8, block, 0)
        for g in range(G):
            dv_ref[:, g * 128:(g + 1) * 128] = dvt_scr[g].T[0:T, :]

    tspec = pl.BlockSpec((T, GW), lambda p, c: (NC - 1 - c, p))
    st_spec = pl.BlockSpec((G, T, HEAD, 128), lambda p, c: (p, NC - 1 - c, 0, 0))
    prev_spec = pl.BlockSpec((G, 1, HEAD, 128), lambda p, c: (p, jnp.maximum((NC - 1 - c) * T - 1, 0), 0, 0))
    sel_spec = pl.BlockSpec((SEL_ROWS, 8 * 128), lambda p, c: (0, 0))
    grid = (NP // G, NC)
    x_in, x_out, x_shapes, x_scr, x_ops = _host_args(plan)
    outs = pl.pallas_call(
        _host(body, plan, 12, 6, *_grid_ends(grid)), name=name, grid=grid,
        in_specs=[tspec] * 8 + [st_spec, prev_spec, sel_spec, pl.BlockSpec((256, 256), lambda p, c: (0, 0))] + x_in,
        out_specs=tuple([tspec] * 6 + x_out),
        out_shape=tuple([jax.ShapeDtypeStruct((S, R), F32)] * 6 + x_shapes),
        scratch_shapes=[pltpu.VMEM((G, HEAD, 128), F32), pltpu.VMEM((G, HEAD, 8 * 128), F32),
                        pltpu.VMEM((G, HEAD, 8 * 128), F32), pltpu.VMEM((G, HEAD, 8 * 128), F32),
                        pltpu.VMEM((G, 128, 128), F32)] + x_scr,
        compiler_params=_cp(_semantics(plan, ("parallel", "arbitrary")), 48),
    )(r, w, k, v, kap, b, sa, dy, states, states, _column_selector(), _head_sum_weights(), *x_ops)
    return tuple(outs[:6]), tuple(outs[6:])


def _sum_parts(parts, *, name):
    P, rows, W = parts.shape
    tr = rows
    for cand in (1024, 512, 256, 128, 64, 32, 16, 8):
        if rows % cand == 0:
            tr = cand
            break

    def body(p_ref, o_ref):
        acc = p_ref[0]
        for s in range(1, P):
            acc = acc + p_ref[s]
        o_ref[...] = acc

    return pl.pallas_call(
        body, name=name, grid=(rows // tr,),
        in_specs=[pl.BlockSpec((P, tr, W), lambda i: (0, i, 0))],
        out_specs=pl.BlockSpec((tr, W), lambda i: (i, 0)), out_shape=jax.ShapeDtypeStruct((rows, W), F32),
        compiler_params=_cp(("parallel",), 32),
    )(parts)


def _adamw(w, m, v, parts, *, name):
    R, C = w.shape
    P = parts.shape[0]
    tr = R
    for cand in (1024, 512, 256, 128, 64, 32, 16, 8):
        if R % cand == 0 and cand * C * 4 * (7 + P) <= 10 * 1024 * 1024:
            tr = cand
            break
    bc1 = 1.0 - ADAM_B1 ** ADAM_STEP
    bc2 = 1.0 - ADAM_B2 ** ADAM_STEP

    def body(w_ref, m_ref, v_ref, p_ref, g_ref, d_ref, nm_ref, nv_ref):
        g = p_ref[0].astype(F32)
        for s in range(1, P):
            g = g + p_ref[s].astype(F32)
        m1 = ADAM_B1 * m_ref[...] + (1.0 - ADAM_B1) * g
        v1 = ADAM_B2 * v_ref[...] + (1.0 - ADAM_B2) * (g * g)
        m_hat = m1 / bc1
        v_hat = v1 / bc2
        g_ref[...] = g
        d_ref[...] = -ADAM_LR * (m_hat / (jnp.sqrt(v_hat) + ADAM_EPS) + ADAM_WD * w_ref[...])
        nm_ref[...] = m1
        nv_ref[...] = v1

    spec = pl.BlockSpec((tr, C), lambda i: (i, 0))
    return pl.pallas_call(
        body, name=name, grid=(R // tr,),
        in_specs=[spec, spec, spec, pl.BlockSpec((P, tr, C), lambda i: (0, i, 0))],
        out_specs=(spec, spec, spec, spec), out_shape=tuple([jax.ShapeDtypeStruct((R, C), F32)] * 4),
        compiler_params=_cp(("parallel",), 40),
    )(w, m, v, parts)


def _cols_full(g8):
    n, rows, c = g8.shape
    return jnp.transpose(g8, (1, 0, 2)).reshape(rows, n * c)


def _cols_split(full):
    rows, cols = full.shape
    return jnp.transpose(full.reshape(rows, N_DEV, cols // N_DEV), (1, 0, 2))


def _pack(vals, rows_multiple=512):
    flat = jnp.concatenate([v.reshape(-1).astype(F32) for v in vals])
    n = flat.shape[0]
    unit = 128 * rows_multiple
    padded = ((n + unit - 1) // unit) * unit
    return jnp.pad(flat, (0, padded - n)).reshape(padded // 128, 128)


def _unpack(packed, shapes):
    flat = packed.reshape(-1)
    out, off = [], 0
    for shp in shapes:
        size = 1
        for d in shp:
            size *= d
        out.append(flat[off:off + size].reshape(shp))
        off += size
    return out


def _ffn_forward(x, weights, gpre, gpost, shift, scale1p, gw, tag, up_plan=None, down_from_plan=None,
                 down_plan=None):
    g8, u8, d8 = weights
    h = _pre_norm_mod(x, gpre, shift, scale1p, name=f"{tag}_pre")
    au, s, carried = _ffn_up(h, g8, u8, tm=1024, tk=2048, plan=up_plan,
                             name=f"{tag}_up" + ("_carry" if up_plan else ""))
    if down_from_plan is not None:
        d8 = carried[down_from_plan]
    wd = d8.reshape(-1, d8.shape[2])
    if down_plan is None:
        f, carried_down = _mm(s, wd, tm=1024, tn=1024, tk=2048, name=f"{tag}_down"), ()
    else:
        f, carried_down = _mm(s, wd, tm=1024, tn=1024, tk=2048, name=f"{tag}_down_carry", plan=down_plan)
    xo = _post_norm_res(x, f, gpost, gw, name=f"{tag}_post")
    return xo, (h, au, s, f), d8, carried, carried_down


def _ffn_backward(dxo, x, saved, weights, gpre, gpost, scale1p, gw, tag, plans=None, own_sums=None,
                  swap_in_dh=False):
    g8, u8, d8 = weights
    h, au, s, f = saved
    plans = dict(plans or {})

    def nm(key):
        return f"{tag}_{key}" + ("_carry" if plans.get(key) is not None else "")

    df, post_sums = _post_norm_res_bwd(dxo, f, gpost, gw, MACARON, name=f"{tag}_post_bwd")
    dwd, got_dwd = _ffn_dwd(s, df, tn=1024, tk=1024, plan=plans.get("dwd"), name=nm("dwd"))
    if own_sums:
        plans["down_bwd"] = _ChipsPlan(own_sums([dwd], "d"))
    dau, got_down = _ffn_down_bwd(df, d8, au, tm=1024, tk=2048, plan=plans.get("down_bwd"), name=nm("down_bwd"))
    dwgu, got_dwgu = _ffn_dwgu(h, dau, tm=1024, tk=1024, plan=plans.get("dwgu"), name=nm("dwgu"))
    if own_sums:
        plans["dh"] = _ChipsPlan(own_sums([dwgu], "gu"))
    elif swap_in_dh:
        plans["dh"] = _SwapPlan([dwgu, dwd])
    dh, got_dh = _ffn_dh(dau, g8, u8, tm=1024, tn=1024, plan=plans.get("dh"),
                         name=nm("dh") + ("_swap" if swap_in_dh else ""))
    dx, pre_sums = _pre_norm_mod_bwd(dh, x, dxo, gpre, scale1p, name=f"{tag}_pre_bwd")
    carried = {"dwd": got_dwd, "down_bwd": got_down, "dwgu": got_dwgu, "dh": got_dh}
    return dx, dwgu, dwd, pre_sums, post_sums, carried


def kernel(x, c, w_ada, b_ada, norm_pre, norm_post, ffn1_w_gate, ffn1_w_up, ffn1_w_down, w_in, mu_shift, pool_w, pool_scale, w0, w2, a0, a2, g2, k_k, k_a, r_k, lnx_w, lnx_b, w_out, ffn2_w_gate, ffn2_w_up, ffn2_w_down, loss_target, m_w_ada, m_b_ada, m_norm_pre, m_norm_post, m_ffn1_w_gate, m_ffn1_w_up, m_ffn1_w_down, m_w_in, m_mu_shift, m_pool_w, m_pool_scale, m_w0, m_w2, m_a0, m_a2, m_g2, m_k_k, m_k_a, m_r_k, m_lnx_w, m_lnx_b, m_w_out, m_ffn2_w_gate, m_ffn2_w_up, m_ffn2_w_down, v_w_ada, v_b_ada, v_norm_pre, v_norm_post, v_ffn1_w_gate, v_ffn1_w_up, v_ffn1_w_down, v_w_in, v_mu_shift, v_pool_w, v_pool_scale, v_w0, v_w2, v_a0, v_a2, v_g2, v_k_k, v_k_a, v_r_k, v_lnx_w, v_lnx_b, v_w_out, v_ffn2_w_gate, v_ffn2_w_up, v_ffn2_w_down):
    names = ["w_ada", "b_ada", "norm_pre", "norm_post", "ffn1_w_gate", "ffn1_w_up", "ffn1_w_down", "w_in", "mu_shift",
             "pool_w", "pool_scale", "w0", "w2", "a0", "a2", "g2", "k_k", "k_a", "r_k", "lnx_w", "lnx_b", "w_out",
             "ffn2_w_gate", "ffn2_w_up", "ffn2_w_down"]
    env = dict(locals())
    W = {n: env[n][0] for n in names}
    M1 = {n: env["m_" + n][0] for n in names}
    V1 = {n: env["v_" + n][0] for n in names}

    me = _my_index()
    xs = x[0]
    tgt = loss_target[0]
    S, D = xs.shape
    F = W["ffn1_w_gate"].shape[1] * N_DEV
    R = W["w0"].shape[0]
    PW = D - R
    IN_W = W["w_in"].shape[1] * N_DEV
    P_W = F
    QW = P_W - PW
    NMOD = 9 * D
    ada_c = W["w_ada"].shape[1]

    c_all, npre8, npost8, w2_8, a2_8, g2_8 = _exchange(
        [c, W["norm_pre"], W["norm_post"], W["w2"].astype(BF16), W["a2"].astype(BF16), W["g2"].astype(BF16)],
        scatter=False, name="gather_small")
    c_all = c_all.reshape(N_DEV, D)
    gpre = _cols_full(npre8)
    gpost = _cols_full(npost8)
    wl = jnp.zeros((LORA_PAD, 3 * R), BF16)
    wl = wl.at[0:LORA_W, 0:R].set(_cols_full(w2_8))
    wl = wl.at[LORA_W:LORA_W + LORA_A, R:2 * R].set(_cols_full(a2_8))
    wl = wl.at[LORA_W + LORA_A:LORA_W + LORA_A + LORA_G, 2 * R:3 * R].set(_cols_full(g2_8))

    sc_all = jax.nn.silu(c_all)
    sc_pad = jnp.concatenate([sc_all, jnp.zeros((8, D), F32)], axis=0).astype(BF16)
    modcols = _mm(sc_pad, W["w_ada"], tm=16, tn=ada_c, tk=256, name="ada_fwd")[0:N_DEV]
    modcols = modcols + lax.dynamic_slice(W["b_ada"], (me * ada_c,), (ada_c,))[None, :]
    (mod8,) = _exchange([modcols], scatter=False, name="gather_mod")
    mod = lax.dynamic_index_in_dim(mod8, me, axis=1, keepdims=False).reshape(9, D)

    def mod_row(i):
        return mod[i:i + 1, :]

    f_pad = FF_TILE - F // N_DEV

    def ffn_shards(tag):
        return [jnp.pad(W[f"{tag}_w_gate"].astype(BF16), ((0, 0), (0, f_pad))),
                jnp.pad(W[f"{tag}_w_up"].astype(BF16), ((0, 0), (0, f_pad))),
                jnp.pad(W[f"{tag}_w_down"].astype(BF16), ((0, f_pad), (0, 0)))]

    ffn1_shards = ffn_shards("ffn1")
    g8_1, u8_1 = _gather_two_level(ffn1_shards[:2], name="gather_ffn_up")
    up_plan = _GatherPlan([ffn1_shards[2]])
    down_plan = _GatherPlan([W["w_in"].astype(BF16)])
    scan_plan = _GatherPlan(ffn_shards("ffn2") + [W["w_out"].astype(BF16)])

    mu_p = jnp.pad(W["mu_shift"], (0, QW - W["mu_shift"].shape[0]))[None, :]
    vec = lambda a: a.reshape(1, -1)
    w0r, a0r, kkr, kar = vec(W["w0"]), vec(W["a0"]), vec(W["k_k"]), vec(W["k_a"])
    lnw, lnb, rkr = vec(W["lnx_w"]), vec(W["lnx_b"]), vec(W["r_k"])
    pscale = vec(W["pool_scale"])

    sc1p = [1.0 + mod_row(3 * s + 1) for s in range(3)]
    shifts = [mod_row(3 * s) for s in range(3)]
    wgts = [MACARON, 1.0, MACARON]
    gws = [wgts[s] * (1.0 + mod_row(3 * s + 2)) for s in range(3)]
    gp = [gpre[s:s + 1] for s in range(3)]
    gq = [gpost[s:s + 1] for s in range(3)]

    x1, sv1, d8_1, _, (win8,) = _ffn_forward(xs, (g8_1, u8_1, None), gp[0], gq[0], shifts[0], sc1p[0], gws[0], "ffn",
                                             up_plan=up_plan, down_from_plan=0, down_plan=down_plan)
    ffn1_w = (g8_1, u8_1, d8_1)
    w_in_p = jnp.pad(_cols_full(win8), ((0, 0), (0, P_W - IN_W)))

    h2 = _pre_norm_mod(x1, gp[1], shifts[1], sc1p[1], name="mix_pre")
    p = _mm(h2, w_in_p, tm=1024, tn=512, tk=2048, name="mix_in")
    q = p[:, PW:]
    o_pool, y_pool = _pool_fwd(p, W["pool_w"], pscale, name="pool_fwd")
    r_s, w_s, k_s, v_s, kap_s, b_s, g_s = _rwkv_prep(q, mu_p, wl, w0r, a0r, kkr, kar, name="rwkv_prep")
    y_scan, sa_s, states, gathered = _scan_fwd(r_s, w_s, k_s, v_s, kap_s, b_s, name="scan_fwd", plan=scan_plan)
    ffn2_w = gathered[:3]
    w_out_f = gathered[3].reshape(D, D)
    cat = _rwkv_post(y_scan, r_s, k_s, v_s, g_s, y_pool, lnw, lnb, rkr, name="rwkv_post")
    f2 = _mm(cat, w_out_f, tm=1024, tn=1024, tk=2048, name="mix_out")
    x2 = _post_norm_res(x1, f2, gq[1], gws[1], name="mix_post")

    x3, sv3, _, _, _ = _ffn_forward(x2, ffn2_w, gp[2], gq[2], shifts[2], sc1p[2], gws[2], "ffn")

    loss_part, dx3 = _loss_head(x3, tgt, name="loss_head")
    loss = lax.psum(loss_part[0, 0], MESH_AXES)

    def by_core_chip(blocks):
        shp = blocks.shape
        t = blocks.astype(BF16).reshape((N_DEV // 2, 2) + shp[1:])
        return jnp.swapaxes(t, 0, 1)

    def chip_sums(mine, tag):
        got = _sibling_swap(mine, name=f"{tag}_swap")
        return [_pair_add(m, g, name=f"{tag}_add{i}") for i, (m, g) in enumerate(zip(mine, got))]

    def ffn_parts(pgu, pd):
        fs = F // N_DEV
        return pgu[:, :D, :fs], pgu[:, D:, :fs], pd[:, :fs, :]

    dx2, dwgu2, dwd2, pre3, post3, got2 = _ffn_backward(dx3, x2, sv3, ffn2_w, gp[2], gq[2], sc1p[2], gws[2], "ffn",
                                                        swap_in_dh=True)
    sums2_gu = _pair_add(dwgu2, got2["dh"][0], name="scatter_ffn_add0")
    sums2_d = _pair_add(dwd2, got2["dh"][1], name="scatter_ffn_add1")

    df2, post2 = _post_norm_res_bwd(dx2, f2, gq[1], gws[1], 1.0, name="mix_post_bwd")
    dw_out = _mm(cat, df2, ta=True, tm=1024, tn=1024, tk=1024, name="mix_dwout")
    dcat = _mm(df2, w_out_f, tb=True, tm=1024, tn=1024, tk=2048, name="mix_dcat")
    dyr = dcat[:, PW:]
    dysc, dg, dr_b, dk2_b, dv_b, post_sums = _rwkv_post_bwd(dyr, y_scan, r_s, k_s, v_s, g_s, lnw, lnb, rkr,
                                                             name="rwkv_post_bwd")
    (dr, dw, dk2, dv, dkap, db), parts2 = _scan_bwd(r_s, w_s, k_s, v_s, kap_s, b_s, sa_s, dysc, states,
                                                    name="scan_bwd", plan=_ChipsPlan([sums2_gu, sums2_d]))
    dps, dwl, prep_sums = _rwkv_prep_bwd(q, mu_p, wl, w0r, a0r, kkr, kar,
                                         (dr, dw, dk2, dv, dkap, db, dg, dr_b, dk2_b, dv_b), name="rwkv_prep_bwd")
    dq, dmu = _tshift_bwd(dps, q, mu_p, name="tshift_bwd")
    du_pool, dpool_w, dpool_scale = _pool_bwd(dcat, o_pool, W["pool_w"], pscale, name="pool_bwd")
    dp = jnp.concatenate([du_pool, dq], axis=1)
    dw_in = _mm(h2, dp, ta=True, tm=1024, tn=512, tk=1024, name="mix_dwin")
    mine_mix = [by_core_chip(_cols_split(dw_in[:, :IN_W])), by_core_chip(dw_out.reshape(N_DEV, D // N_DEV, D))]
    dh2, got_mix = _mm(dp, w_in_p, tb=True, tm=1024, tn=1024, tk=2816, name="mix_dh_carry",
                       plan=_SwapPlan(mine_mix))
    dx1, pre2 = _pre_norm_mod_bwd(dh2, x1, dx2, gp[1], sc1p[1], name="mix_pre_bwd")
    sums_mix = [_pair_add(m, g, name=f"scatter_mixer_add{i}") for i, (m, g) in enumerate(zip(mine_mix, got_mix))]
    early = [dmu[0, :W["mu_shift"].shape[0]], dpool_w, dpool_scale, prep_sums[0], prep_sums[1], prep_sums[2],
             prep_sums[3], post_sums[2], post_sums[0], post_sums[1],
             dwl[0:LORA_W, 0:R], dwl[LORA_W:LORA_W + LORA_A, R:2 * R],
             dwl[LORA_W + LORA_A:LORA_W + LORA_A + LORA_G, 2 * R:3 * R]]
    early_shapes = [a.shape for a in early]
    dx0, _, _, pre1, post1, got = _ffn_backward(
        dx1, xs, sv1, ffn1_w, gp[0], gq[0], sc1p[0], gws[0], "ffn",
        plans={"dwd": _ChipsPlan(sums_mix), "dwgu": _GatherPlan([_pack(early)])},
        own_sums=lambda blocks, part: chip_sums(blocks, f"scatter_ffn_{part}"))

    pres, posts = [pre1, pre2, pre3], [post1, post2, post3]
    dmod = jnp.stack([jnp.stack([pres[s][0], pres[s][1], posts[s][0]]) for s in range(3)]).reshape(NMOD // 128, 128)
    late = [jnp.stack([pres[s][2] for s in range(3)]), jnp.stack([posts[s][1] for s in range(3)])]
    late_shapes = [a.shape for a in late]
    dmod8, late8 = _gather_two_level([dmod, _pack(late, rows_multiple=8)], name="gather_grads")
    g_b_ada = _sum_parts(dmod8, name="sum_dmod").reshape(NMOD)
    g_npre, g_npost = _unpack(_sum_parts(late8, name="sum_late"), late_shapes)
    (g_mu, g_pool_w, g_pool_scale, g_w0, g_a0, g_kk, g_ka, g_rk, g_lnw, g_lnb, g_w2, g_a2,
     g_g2) = _unpack(_sum_parts(got["dwgu"][0], name="sum_small"), early_shapes)

    dmod_all = dmod8.reshape(N_DEV, NMOD)
    dmod_cols = lax.dynamic_slice(dmod_all, (0, me * ada_c), (N_DEV, ada_c))
    dmod_cols = jnp.concatenate([dmod_cols, jnp.zeros_like(dmod_cols)], axis=0)
    g_w_ada = _mm(sc_pad, dmod_cols, ta=True, tm=D, tn=ada_c // 9, tk=16, name="ada_bwd")

    pg2, pu2, pd2 = ffn_parts(*parts2)
    pin, pout = got["dwd"]
    pg1, pu1, pd1 = ffn_parts(got["dh"][0], got["down_bwd"][0])

    res = {}

    def big(nm, parts, tag):
        res[nm] = _adamw(W[nm], M1[nm], V1[nm], parts, name=tag)

    big("ffn1_w_gate", pg1, "adamw_cols")
    big("ffn1_w_up", pu1, "adamw_cols")
    big("ffn1_w_down", pd1, "adamw_rows")
    big("ffn2_w_gate", pg2, "adamw_cols")
    big("ffn2_w_up", pu2, "adamw_cols")
    big("ffn2_w_down", pd2, "adamw_rows")
    big("w_in", pin, "adamw_w_in")
    big("w_out", pout, "adamw_w_out")
    big("w_ada", g_w_ada[None], "adamw_w_ada")

    def my_cols(full, width):
        return lax.dynamic_slice_in_dim(full, me * width, width, axis=full.ndim - 1)

    small_names = ["b_ada", "mu_shift", "pool_w", "pool_scale", "w0", "a0", "k_k", "k_a", "r_k", "lnx_w", "lnx_b",
                   "norm_pre", "norm_post", "w2", "a2", "g2"]
    small_grads = [g_b_ada, g_mu, g_pool_w, g_pool_scale, g_w0, g_a0, g_kk, g_ka, g_rk.reshape(W["r_k"].shape), g_lnw,
                   g_lnb, my_cols(g_npre, D // N_DEV), my_cols(g_npost, D // N_DEV), my_cols(g_w2, R // N_DEV),
                   my_cols(g_a2, R // N_DEV), my_cols(g_g2, R // N_DEV)]
    shapes = [W[n].shape for n in small_names]
    packed = _adamw(_pack([W[n] for n in small_names]), _pack([M1[n] for n in small_names]),
                    _pack([V1[n] for n in small_names]), _pack(small_grads)[None], name="adamw_small")
    unpacked = [_unpack(t, shapes) for t in packed]
    for i, nm in enumerate(small_names):
        res[nm] = tuple(unpacked[k][i] for k in range(4))

    outs = [loss, dx0[None]]
    for k in range(4):
        outs.extend(res[nm][k][None] for nm in names)
    return tuple(outs)
```

```python
import functools

import jax
import jax.numpy as jnp
from jax import lax
from jax.experimental import pallas as pl
from jax.experimental.pallas import tpu as pltpu

F32 = jnp.float32
BF16 = jnp.bfloat16
N_DEV = 8
MESH_AXES = ("x", "y", "c")

NORM_EPS = 1e-6
HEAD = 64
LN_X_EPS = 1e-5 * HEAD
POOL_GROUPS = 4
POOL_GROUP = 128
MACARON = 0.5
LORA_W, LORA_A, LORA_G = 64, 64, 224
LORA_PAD = 384
ADAM_LR, ADAM_B1, ADAM_B2, ADAM_EPS, ADAM_WD, ADAM_STEP = 0.001, 0.9, 0.999, 1e-08, 0.01, 10

FF_TILE = 768
ROW_TILE = 256
SCAN_T = 64
SCAN_T_BWD = 64
SCAN_G = 6
SCAN_G_BWD = 6
VMEM_CAP = 56 * 1024 * 1024


def _cp(sem, vmem_mb):
    return pltpu.CompilerParams(dimension_semantics=sem, vmem_limit_bytes=min(vmem_mb * 1024 * 1024, VMEM_CAP))


def _my_index():
    return 4 * lax.axis_index("x") + 2 * lax.axis_index("y") + lax.axis_index("c")


def _exchange(arrays, *, scatter, name):
    n = len(arrays)
    out_shapes = []
    for a in arrays:
        shp = a.shape if scatter else (N_DEV,) + a.shape
        out_shapes.append(jax.ShapeDtypeStruct(shp, a.dtype))

    def body(*refs):
        ins, outs = refs[:n], refs[n:2 * n]
        send_sems, recv_sems, local_sems = refs[2 * n:]
        me = _my_index()

        def dev(p):
            return (p // 4, (p // 2) % 2, p % 2)

        def copy(i, d):
            peer = (me + d) % N_DEV
            src = ins[i].at[peer] if scatter else ins[i]
            return pltpu.make_async_remote_copy(
                src_ref=src, dst_ref=outs[i].at[me], send_sem=send_sems.at[i, d - 1],
                recv_sem=recv_sems.at[i, d - 1], device_id=dev(peer), device_id_type=pl.DeviceIdType.MESH)

        def arrival(i, d):
            frm = (me + N_DEV - d) % N_DEV
            src = ins[i].at[frm] if scatter else ins[i]
            return pltpu.make_async_remote_copy(
                src_ref=src, dst_ref=outs[i].at[frm], send_sem=send_sems.at[i, d - 1],
                recv_sem=recv_sems.at[i, d - 1], device_id=dev(frm), device_id_type=pl.DeviceIdType.MESH)

        locals_ = []
        for i in range(n):
            src = ins[i].at[me] if scatter else ins[i]
            lc = pltpu.make_async_copy(src, outs[i].at[me], local_sems.at[i])
            lc.start()
            locals_.append(lc)
        sends = [copy(i, d) for d in range(1, N_DEV) for i in range(n)]
        for cp in sends:
            cp.start()
        for d in range(1, N_DEV):
            for i in range(n):
                arrival(i, d).wait_recv()
        for cp in sends:
            cp.wait_send()
        for lc in locals_:
            lc.wait()

    hbm = pl.BlockSpec(memory_space=pltpu.HBM)
    return pl.pallas_call(
        body, name=name, out_shape=tuple(out_shapes), in_specs=[hbm] * n, out_specs=tuple([hbm] * n),
        scratch_shapes=[pltpu.SemaphoreType.DMA((n, N_DEV - 1)), pltpu.SemaphoreType.DMA((n, N_DEV - 1)),
                        pltpu.SemaphoreType.DMA((n,))],
    )(*arrays)


def _remote(src, dst, send_sem, recv_sem, to):
    return pltpu.make_async_remote_copy(src_ref=src, dst_ref=dst, send_sem=send_sem, recv_sem=recv_sem,
                                        device_id=to, device_id_type=pl.DeviceIdType.MESH)


class _GatherPlan:
    def __init__(self, arrays):
        self.arrays = list(arrays)
        self.n = len(arrays)
        self.out_shapes = [jax.ShapeDtypeStruct((N_DEV,) + a.shape, a.dtype) for a in arrays]
        self.scratch = [pltpu.SemaphoreType.DMA((self.n, 7)), pltpu.SemaphoreType.DMA((self.n, 7)),
                        pltpu.SemaphoreType.DMA((self.n,))]

    def _parts(self, ins, outs, sems):
        send_sems, recv_sems, local_sems = sems
        x, y, c = lax.axis_index("x"), lax.axis_index("y"), lax.axis_index("c")
        chips = [(1 - x, y), (x, 1 - y), (1 - x, 1 - y)]

        def slot(i, px, py, pc):
            return outs[i].at[4 * px + 2 * py + pc]

        def copy(i, k, block, to, src=None):
            dst = slot(i, *block)
            return _remote(dst if src is None else src, dst, send_sems.at[i, k], recv_sems.at[i, k], to)

        n = self.n
        locals_ = [pltpu.make_async_copy(ins[i], slot(i, x, y, c), local_sems.at[i]) for i in range(n)]
        first = [copy(i, 1 + j, (x, y, c), (*chip, c), src=ins[i]) for j, chip in enumerate(chips) for i in range(n)]
        first += [copy(i, 0, (x, y, c), (x, y, 1 - c), src=ins[i]) for i in range(n)]
        return (x, y, c), chips, copy, locals_, first

    def start(self, ins, outs, sems):
        _, _, _, locals_, first = self._parts(ins, outs, sems)
        for lc in locals_:
            lc.start()
        for cp in first:
            cp.start()

    def finish(self, ins, outs, sems):
        (x, y, c), chips, copy, locals_, first = self._parts(ins, outs, sems)
        forwards = []
        for j, chip in enumerate(chips):
            for i in range(self.n):
                copy(i, 1 + j, (*chip, c), (x, y, c)).wait_recv()
                fwd = copy(i, 4 + j, (*chip, c), (x, y, 1 - c))
                fwd.start()
                forwards.append(fwd)
        for i in range(self.n):
            copy(i, 0, (x, y, 1 - c), (x, y, c)).wait_recv()
        for j, chip in enumerate(chips):
            for i in range(self.n):
                copy(i, 4 + j, (*chip, 1 - c), (x, y, c)).wait_recv()
        for cp in first + forwards:
            cp.wait_send()
        for lc in locals_:
            lc.wait()


class _ChipsPlan:
    def __init__(self, arrays):
        self.arrays = list(arrays)
        self.n = len(arrays)
        self.out_shapes = [jax.ShapeDtypeStruct(a.shape, a.dtype) for a in arrays]
        self.scratch = [pltpu.SemaphoreType.DMA((self.n, 3)), pltpu.SemaphoreType.DMA((self.n, 3)),
                        pltpu.SemaphoreType.DMA((self.n,))]

    def _parts(self, ins, outs, sems):
        send_sems, recv_sems, local_sems = sems
        x, y, c = lax.axis_index("x"), lax.axis_index("y"), lax.axis_index("c")
        mine = 2 * x + y
        chips = [(1 - x, y), (x, 1 - y), (1 - x, 1 - y)]
        n = self.n
        locals_ = [pltpu.make_async_copy(ins[i].at[mine], outs[i].at[mine], local_sems.at[i]) for i in range(n)]
        sends = [_remote(ins[i].at[2 * chip[0] + chip[1]], outs[i].at[mine], send_sems.at[i, j], recv_sems.at[i, j],
                         (*chip, c)) for j, chip in enumerate(chips) for i in range(n)]

        def arrivals():
            return [_remote(ins[i].at[2 * chip[0] + chip[1]], outs[i].at[2 * chip[0] + chip[1]], send_sems.at[i, j],
                            recv_sems.at[i, j], (*chip, c)) for j, chip in enumerate(chips) for i in range(n)]

        return locals_, sends, arrivals

    def start(self, ins, outs, sems):
        locals_, sends, _ = self._parts(ins, outs, sems)
        for lc in locals_:
            lc.start()
        for cp in sends:
            cp.start()

    def finish(self, ins, outs, sems):
        locals_, sends, arrivals = self._parts(ins, outs, sems)
        for cp in arrivals():
            cp.wait_recv()
        for cp in sends:
            cp.wait_send()
        for lc in locals_:
            lc.wait()


def _run_plan(plan, *, name):
    n = plan.n

    def body(*refs):
        ins, outs, sems = refs[:n], refs[n:2 * n], refs[2 * n:]
        plan.start(ins, outs, sems)
        plan.finish(ins, outs, sems)

    hbm = pl.BlockSpec(memory_space=pltpu.HBM)
    return pl.pallas_call(
        body, name=name, out_shape=tuple(plan.out_shapes), in_specs=[hbm] * n, out_specs=tuple([hbm] * n),
        scratch_shapes=plan.scratch,
    )(*plan.arrays)


def _host(body, plan, n_in, n_out, is_first, is_last):
    if plan is None:
        return body
    m = plan.n

    def wrapped(*refs):
        a, b = n_in, n_in + m
        c, d = b + n_out, b + n_out + m
        own_in, c_in, own_out, c_out, rest = refs[:a], refs[a:b], refs[b:c], refs[c:d], refs[d:]
        n_sems = len(plan.scratch)
        own_scr, c_sems = rest[:len(rest) - n_sems], rest[len(rest) - n_sems:]

        @pl.when(is_first())
        def _():
            plan.start(c_in, c_out, c_sems)

        body(*own_in, *own_out, *own_scr)

        @pl.when(is_last())
        def _():
            plan.finish(c_in, c_out, c_sems)

    return wrapped


def _host_args(plan):
    if plan is None:
        return [], [], [], [], []
    hbm = pl.BlockSpec(memory_space=pltpu.HBM)
    return [hbm] * plan.n, [hbm] * plan.n, list(plan.out_shapes), list(plan.scratch), list(plan.arrays)


def _gather_two_level(arrays, *, name):
    return _run_plan(_GatherPlan(arrays), name=name)


class _SwapPlan:
    def __init__(self, arrays):
        self.arrays = list(arrays)
        self.n = len(arrays)
        self.out_shapes = [jax.ShapeDtypeStruct(a.shape[1:], a.dtype) for a in arrays]
        self.scratch = [pltpu.SemaphoreType.DMA((self.n,)), pltpu.SemaphoreType.DMA((self.n,))]

    def _copies(self, ins, outs, sems):
        send_sems, recv_sems = sems
        x, y, c = lax.axis_index("x"), lax.axis_index("y"), lax.axis_index("c")
        return [_remote(ins[i].at[1 - c], outs[i], send_sems.at[i], recv_sems.at[i], (x, y, 1 - c))
                for i in range(self.n)]

    def start(self, ins, outs, sems):
        for cp in self._copies(ins, outs, sems):
            cp.start()

    def finish(self, ins, outs, sems):
        copies = self._copies(ins, outs, sems)
        for cp in copies:
            cp.wait_recv()
        for cp in copies:
            cp.wait_send()


def _sibling_swap(arrays, *, name):
    return _run_plan(_SwapPlan(arrays), name=name)


def _chips_all_to_all(arrays, *, name):
    return _run_plan(_ChipsPlan(arrays), name=name)


def _pair_add(mine, got, *, name):
    _, nq, R, C = mine.shape
    tr = R
    for cand in (512, 256, 128, 64, 32, 16):
        if R % cand == 0 and cand * C * 2 * 3 * 2 <= 12 * 1024 * 1024:
            tr = cand
            break

    def body(core_ref, m_ref, g_ref, o_ref):
        o_ref[0] = (m_ref[0, 0].astype(F32) + g_ref[0].astype(F32)).astype(BF16)

    core = lax.axis_index("c").astype(jnp.int32).reshape(1)
    return pl.pallas_call(
        body, name=name,
        grid_spec=pltpu.PrefetchScalarGridSpec(
            num_scalar_prefetch=1, grid=(nq, R // tr),
            in_specs=[pl.BlockSpec((1, 1, tr, C), lambda q, i, core_ref: (core_ref[0], q, i, 0)),
                      pl.BlockSpec((1, tr, C), lambda q, i, core_ref: (q, i, 0))],
            out_specs=pl.BlockSpec((1, tr, C), lambda q, i, core_ref: (q, i, 0))),
        out_shape=jax.ShapeDtypeStruct((nq, R, C), BF16),
        compiler_params=_cp(("parallel", "parallel"), 40),
    )(core, mine, got)


def _mm(a, b, *, ta=False, tb=False, tm, tn, tk, out_dtype=F32, name, plan=None):
    M = a.shape[1] if ta else a.shape[0]
    K = a.shape[0] if ta else a.shape[1]
    N = b.shape[0] if tb else b.shape[1]
    tm, tn, tk = min(tm, M), min(tn, N), min(tk, K)
    assert M % tm == 0 and N % tn == 0 and K % tk == 0, (name, M, N, K, tm, tn, tk)
    nk = K // tk
    dims = (((0 if ta else 1,), (1 if tb else 0,)), ((), ()))

    def body(a_ref, b_ref, o_ref, acc_ref):
        k = pl.program_id(2)

        @pl.when(k == 0)
        def _():
            acc_ref[...] = jnp.zeros_like(acc_ref)

        acc_ref[...] += lax.dot_general(a_ref[...].astype(BF16), b_ref[...].astype(BF16), dims,
                                        preferred_element_type=F32)

        @pl.when(k == nk - 1)
        def _():
            o_ref[...] = acc_ref[...].astype(out_dtype)

    a_spec = pl.BlockSpec((tk, tm), lambda i, j, k: (k, i)) if ta else pl.BlockSpec((tm, tk), lambda i, j, k: (i, k))
    b_spec = pl.BlockSpec((tn, tk), lambda i, j, k: (j, k)) if tb else pl.BlockSpec((tk, tn), lambda i, j, k: (k, j))
    blk = 2 * (tm * tk * a.dtype.itemsize + tk * tn * b.dtype.itemsize + tm * tn * jnp.dtype(out_dtype).itemsize)
    grid = (M // tm, N // tn, nk)
    x_in, x_out, x_shapes, x_scr, x_ops = _host_args(plan)
    outs = pl.pallas_call(
        _host(body, plan, 2, 1, *_grid_ends(grid)), name=name, grid=grid, in_specs=[a_spec, b_spec] + x_in,
        out_specs=tuple([pl.BlockSpec((tm, tn), lambda i, j, k: (i, j))] + x_out),
        out_shape=tuple([jax.ShapeDtypeStruct((M, N), out_dtype)] + x_shapes),
        scratch_shapes=[pltpu.VMEM((tm, tn), F32)] + x_scr,
        compiler_params=_cp(_semantics(plan, ("parallel", "parallel", "arbitrary")),
                            (blk + tm * tn * 4) // (1024 * 1024) + 12),
    )(a, b, *x_ops)
    return outs[0] if plan is None else (outs[0], tuple(outs[1:]))


def _grid_ends(grid):
    def is_first():
        ok = pl.program_id(0) == 0
        for ax in range(1, len(grid)):
            ok = ok & (pl.program_id(ax) == 0)
        return ok

    def is_last():
        ok = pl.program_id(0) == grid[0] - 1
        for ax in range(1, len(grid)):
            ok = ok & (pl.program_id(ax) == grid[ax] - 1)
        return ok

    return is_first, is_last


def _semantics(plan, sem):
    return sem if plan is None else tuple("arbitrary" for _ in sem)


def _ffn_up(h, g8, u8, *, tm, tk, name, plan=None):
    S, D = h.shape
    nb, _, tn = g8.shape
    tm = min(tm, S)
    tk = min(tk, D)
    nk = D // tk

    def body(h_ref, g_ref, u_ref, au_ref, s_ref, acc_ref):
        k = pl.program_id(2)

        @pl.when(k == 0)
        def _():
            acc_ref[...] = jnp.zeros_like(acc_ref)

        hv = h_ref[...]
        acc_ref[:, :tn] += jnp.dot(hv, g_ref[0], preferred_element_type=F32)
        acc_ref[:, tn:] += jnp.dot(hv, u_ref[0], preferred_element_type=F32)

        @pl.when(k == nk - 1)
        def _():
            acc = acc_ref[...]
            a = acc[:, :tn]
            u = acc[:, tn:]
            au_ref[...] = acc.astype(BF16)
            s_ref[...] = (a * jax.nn.sigmoid(a) * u).astype(BF16)

    wspec = pl.BlockSpec((1, tk, tn), lambda i, j, k: (j, k, 0))
    grid = (S // tm, nb, nk)
    x_in, x_out, x_shapes, x_scr, x_ops = _host_args(plan)
    outs = pl.pallas_call(
        _host(body, plan, 3, 2, *_grid_ends(grid)), name=name, grid=grid,
        in_specs=[pl.BlockSpec((tm, tk), lambda i, j, k: (i, k)), wspec, wspec] + x_in,
        out_specs=tuple([pl.BlockSpec((tm, 2 * tn), lambda i, j, k: (i, j)),
                         pl.BlockSpec((tm, tn), lambda i, j, k: (i, j))] + x_out),
        out_shape=tuple([jax.ShapeDtypeStruct((S, 2 * nb * tn), BF16), jax.ShapeDtypeStruct((S, nb * tn), BF16)]
                        + x_shapes),
        scratch_shapes=[pltpu.VMEM((tm, 2 * tn), F32)] + x_scr,
        compiler_params=_cp(_semantics(plan, ("parallel", "parallel", "arbitrary")), 52),
    )(h, g8, u8, *x_ops)
    return outs[0], outs[1], tuple(outs[2:])


def _ffn_dh(dau, g8, u8, *, tm, tn, name, plan=None):
    S = dau.shape[0]
    nb, D, tf = g8.shape
    tm, tn = min(tm, S), min(tn, D)
    nk = 2 * nb
    nt = (((1,), (1,)), ((), ()))

    def body(a_ref, g_ref, u_ref, o_ref, acc_ref):
        k = pl.program_id(2)

        @pl.when(k == 0)
        def _():
            acc_ref[...] = jnp.zeros_like(acc_ref)

        @pl.when(k % 2 == 0)
        def _():
            acc_ref[...] += lax.dot_general(a_ref[...], g_ref[0], nt, preferred_element_type=F32)

        @pl.when(k % 2 == 1)
        def _():
            acc_ref[...] += lax.dot_general(a_ref[...], u_ref[0], nt, preferred_element_type=F32)

        @pl.when(k == nk - 1)
        def _():
            o_ref[...] = acc_ref[...]

    wspec = pl.BlockSpec((1, tn, tf), lambda i, n, k: (k // 2, n, 0))
    grid = (S // tm, D // tn, nk)
    x_in, x_out, x_shapes, x_scr, x_ops = _host_args(plan)
    outs = pl.pallas_call(
        _host(body, plan, 3, 1, *_grid_ends(grid)), name=name, grid=grid,
        in_specs=[pl.BlockSpec((tm, tf), lambda i, n, k: (i, k)), wspec, wspec] + x_in,
        out_specs=tuple([pl.BlockSpec((tm, tn), lambda i, n, k: (i, n))] + x_out),
        out_shape=tuple([jax.ShapeDtypeStruct((S, D), F32)] + x_shapes),
        scratch_shapes=[pltpu.VMEM((tm, tn), F32)] + x_scr,
        compiler_params=_cp(_semantics(plan, ("parallel", "parallel", "arbitrary")), 40),
    )(dau, g8, u8, *x_ops)
    return outs[0], tuple(outs[1:])


def _ffn_dwgu(h, dau, *, tm, tk, name, plan=None):
    S, D = h.shape
    tf = FF_TILE
    nt = dau.shape[1] // tf
    tm, tk = min(tm, D), min(tk, S)
    nk = S // tk
    ni = D // tm

    def body(a_ref, b_ref, o_ref, acc_ref):
        k = pl.program_id(2)

        @pl.when(k == 0)
        def _():
            acc_ref[...] = jnp.zeros_like(acc_ref)

        acc_ref[...] += lax.dot_general(a_ref[...], b_ref[...], (((0,), (0,)), ((), ())), preferred_element_type=F32)

        @pl.when(k == nk - 1)
        def _():
            o_ref[0, 0] = acc_ref[...].astype(BF16)

    grid = (ni, nt, nk)
    x_in, x_out, x_shapes, x_scr, x_ops = _host_args(plan)
    outs = pl.pallas_call(
        _host(body, plan, 2, 1, *_grid_ends(grid)), name=name, grid=grid,
        in_specs=[pl.BlockSpec((tk, tm), lambda i, j, k: (k, i)), pl.BlockSpec((tk, tf), lambda i, j, k: (k, j))] + x_in,
        out_specs=tuple([pl.BlockSpec((1, 1, tm, tf), lambda i, j, k: ((j // 2) % 2, j // 4, (j % 2) * ni + i, 0))]
                        + x_out),
        out_shape=tuple([jax.ShapeDtypeStruct((2, nt // 4, 2 * D, tf), BF16)] + x_shapes),
        scratch_shapes=[pltpu.VMEM((tm, tf), F32)] + x_scr,
        compiler_params=_cp(_semantics(plan, ("parallel", "parallel", "arbitrary")), 40),
    )(h, dau, *x_ops)
    return outs[0], tuple(outs[1:])


def _ffn_dwd(s, df, *, tn, tk, name, plan=None):
    S, D = df.shape
    tf = FF_TILE
    nb = s.shape[1] // tf
    tn, tk = min(tn, D), min(tk, S)
    nk = S // tk

    def body(a_ref, b_ref, o_ref, acc_ref):
        k = pl.program_id(2)

        @pl.when(k == 0)
        def _():
            acc_ref[...] = jnp.zeros_like(acc_ref)

        acc_ref[...] += lax.dot_general(a_ref[...], b_ref[...], (((0,), (0,)), ((), ())), preferred_element_type=F32)

        @pl.when(k == nk - 1)
        def _():
            o_ref[0, 0] = acc_ref[...].astype(BF16)

    grid = (nb, D // tn, nk)
    x_in, x_out, x_shapes, x_scr, x_ops = _host_args(plan)
    outs = pl.pallas_call(
        _host(body, plan, 2, 1, *_grid_ends(grid)), name=name, grid=grid,
        in_specs=[pl.BlockSpec((tk, tf), lambda j, n, k: (k, j)), pl.BlockSpec((tk, tn), lambda j, n, k: (k, n))] + x_in,
        out_specs=tuple([pl.BlockSpec((1, 1, tf, tn), lambda j, n, k: (j % 2, j // 2, 0, n))] + x_out),
        out_shape=tuple([jax.ShapeDtypeStruct((2, nb // 2, tf, D), BF16)] + x_shapes),
        scratch_shapes=[pltpu.VMEM((tf, tn), F32)] + x_scr,
        compiler_params=_cp(_semantics(plan, ("parallel", "parallel", "arbitrary")), 40),
    )(s, df, *x_ops)
    return outs[0], tuple(outs[1:])


def _ffn_down_bwd(df, d8, au, *, tm, tk, name, plan=None):
    S, D = df.shape
    nb, tn, _ = d8.shape
    F = nb * tn
    tm = min(tm, S)
    tk = min(tk, D)
    nk = D // tk

    def body(df_ref, w_ref, au_ref, dau_ref, acc_ref):
        k = pl.program_id(2)

        @pl.when(k == 0)
        def _():
            acc_ref[...] = jnp.zeros_like(acc_ref)

        acc_ref[...] += lax.dot_general(df_ref[...], w_ref[0], (((1,), (1,)), ((), ())), preferred_element_type=F32)

        @pl.when(k == nk - 1)
        def _():
            ds = acc_ref[...]
            au_v = au_ref[...].astype(F32)
            a = au_v[:, :tn]
            u = au_v[:, tn:]
            sg = jax.nn.sigmoid(a)
            da = ds * u * (sg * (1.0 + a * (1.0 - sg)))
            du = ds * (a * sg)
            dau_ref[:, :tn] = da.astype(BF16)
            dau_ref[:, tn:] = du.astype(BF16)

    grid = (S // tm, F // tn, nk)
    x_in, x_out, x_shapes, x_scr, x_ops = _host_args(plan)
    outs = pl.pallas_call(
        _host(body, plan, 3, 1, *_grid_ends(grid)), name=name, grid=grid,
        in_specs=[pl.BlockSpec((tm, tk), lambda i, j, k: (i, k)), pl.BlockSpec((1, tn, tk), lambda i, j, k: (j, 0, k)),
                  pl.BlockSpec((tm, 2 * tn), lambda i, j, k: (i, j))] + x_in,
        out_specs=tuple([pl.BlockSpec((tm, 2 * tn), lambda i, j, k: (i, j))] + x_out),
        out_shape=tuple([jax.ShapeDtypeStruct((S, 2 * F), BF16)] + x_shapes),
        scratch_shapes=[pltpu.VMEM((tm, tn), F32)] + x_scr,
        compiler_params=_cp(_semantics(plan, ("parallel", "parallel", "arbitrary")), 52),
    )(df, d8, au, *x_ops)
    return outs[0], tuple(outs[1:])


def _fold8(x):
    tm, w = x.shape
    return jnp.sum(x.reshape(tm // 8, 8, w), axis=0)


def _row_spec(tm, w):
    return pl.BlockSpec((tm, w), lambda i: (i, 0))


def _vec_spec(rows, w):
    return pl.BlockSpec((rows, w), lambda i: (0, 0))


def _pre_norm_mod(x, gain, shift, scale1p, *, name):
    S, D = x.shape
    tm = ROW_TILE

    def body(x_ref, g_ref, sh_ref, sc_ref, h_ref):
        xv = x_ref[...]
        rinv = lax.rsqrt(jnp.mean(xv * xv, axis=-1, keepdims=True) + NORM_EPS)
        h_ref[...] = ((xv * rinv) * g_ref[...] * sc_ref[...] + sh_ref[...]).astype(BF16)

    return pl.pallas_call(
        body, name=name, grid=(S // tm,),
        in_specs=[_row_spec(tm, D), _vec_spec(1, D), _vec_spec(1, D), _vec_spec(1, D)],
        out_specs=_row_spec(tm, D), out_shape=jax.ShapeDtypeStruct((S, D), BF16),
        compiler_params=_cp(("parallel",), 32),
    )(x, gain, shift, scale1p)


def _pre_norm_mod_bwd(dh, x, dres, gain, scale1p, *, name):
    S, D = x.shape
    tm = ROW_TILE
    n = S // tm

    def body(dh_ref, x_ref, dr_ref, g_ref, sc_ref, dx_ref, sums_ref, acc_ref):
        i = pl.program_id(0)

        @pl.when(i == 0)
        def _():
            acc_ref[...] = jnp.zeros_like(acc_ref)

        xv = x_ref[...]
        dhv = dh_ref[...]
        g = g_ref[...]
        rinv = lax.rsqrt(jnp.mean(xv * xv, axis=-1, keepdims=True) + NORM_EPS)
        xn = xv * rinv
        dn = dhv * sc_ref[...]
        dxn = dn * g
        dx_ref[...] = dr_ref[...] + rinv * (dxn - xn * jnp.mean(dxn * xn, axis=-1, keepdims=True))
        acc_ref[0] += _fold8(dhv)
        acc_ref[1] += _fold8(dhv * (xn * g))
        acc_ref[2] += _fold8(dn * xn)

        @pl.when(i == n - 1)
        def _():
            for q in range(3):
                sums_ref[q:q + 1, :] = jnp.sum(acc_ref[q], axis=0, keepdims=True)

    return pl.pallas_call(
        body, name=name, grid=(n,),
        in_specs=[_row_spec(tm, D), _row_spec(tm, D), _row_spec(tm, D), _vec_spec(1, D), _vec_spec(1, D)],
        out_specs=(_row_spec(tm, D), _vec_spec(3, D)),
        out_shape=(jax.ShapeDtypeStruct((S, D), F32), jax.ShapeDtypeStruct((3, D), F32)),
        scratch_shapes=[pltpu.VMEM((3, 8, D), F32)],
        compiler_params=_cp(("arbitrary",), 40),
    )(dh, x, dres, gain, scale1p)


def _post_norm_res(x, f, gain, gw, *, name):
    S, D = x.shape
    tm = ROW_TILE

    def body(x_ref, f_ref, g_ref, gw_ref, o_ref):
        fv = f_ref[...]
        rinv = lax.rsqrt(jnp.mean(fv * fv, axis=-1, keepdims=True) + NORM_EPS)
        o_ref[...] = x_ref[...] + gw_ref[...] * ((fv * rinv) * g_ref[...])

    return pl.pallas_call(
        body, name=name, grid=(S // tm,),
        in_specs=[_row_spec(tm, D), _row_spec(tm, D), _vec_spec(1, D), _vec_spec(1, D)],
        out_specs=_row_spec(tm, D), out_shape=jax.ShapeDtypeStruct((S, D), F32),
        compiler_params=_cp(("parallel",), 32),
    )(x, f, gain, gw)


def _post_norm_res_bwd(dxo, f, gain, gw, weight, *, name):
    S, D = f.shape
    tm = ROW_TILE
    n = S // tm

    def body(d_ref, f_ref, g_ref, gw_ref, df_ref, sums_ref, acc_ref):
        i = pl.program_id(0)

        @pl.when(i == 0)
        def _():
            acc_ref[...] = jnp.zeros_like(acc_ref)

        fv = f_ref[...]
        dv = d_ref[...]
        g = g_ref[...]
        rinv = lax.rsqrt(jnp.mean(fv * fv, axis=-1, keepdims=True) + NORM_EPS)
        fh = fv * rinv
        dy = dv * gw_ref[...]
        dfh = dy * g
        df_ref[...] = (rinv * (dfh - fh * jnp.mean(dfh * fh, axis=-1, keepdims=True))).astype(BF16)
        acc_ref[0] += _fold8(weight * dv * (fh * g))
        acc_ref[1] += _fold8(dy * fh)

        @pl.when(i == n - 1)
        def _():
            for q in range(2):
                sums_ref[q:q + 1, :] = jnp.sum(acc_ref[q], axis=0, keepdims=True)

    return pl.pallas_call(
        body, name=name, grid=(n,),
        in_specs=[_row_spec(tm, D), _row_spec(tm, D), _vec_spec(1, D), _vec_spec(1, D)],
        out_specs=(_row_spec(tm, D), _vec_spec(2, D)),
        out_shape=(jax.ShapeDtypeStruct((S, D), BF16), jax.ShapeDtypeStruct((2, D), F32)),
        scratch_shapes=[pltpu.VMEM((2, 8, D), F32)],
        compiler_params=_cp(("arbitrary",), 40),
    )(dxo, f, gain, gw)


def _loss_head(y, target, *, name):
    S, D = y.shape
    tm = ROW_TILE

    def body(y_ref, t_ref, l_ref, dy_ref):
        i = pl.program_id(0)

        @pl.when(i == 0)
        def _():
            l_ref[...] = jnp.zeros_like(l_ref)

        err = y_ref[...] - t_ref[...]
        dy_ref[...] = err * (1.0 / D)
        row = jnp.sum(err * err, axis=-1, keepdims=True) * (0.5 / D)
        l_ref[...] += jnp.sum(row, axis=0, keepdims=True)

    return pl.pallas_call(
        body, name=name, grid=(S // tm,),
        in_specs=[_row_spec(tm, D), _row_spec(tm, D)],
        out_specs=(_vec_spec(1, 1), _row_spec(tm, D)),
        out_shape=(jax.ShapeDtypeStruct((1, 1), F32), jax.ShapeDtypeStruct((S, D), F32)),
        compiler_params=_cp(("arbitrary",), 32),
    )(y, target)


def _shift_down(z, j, row):
    return jnp.where(row >= j, pltpu.roll(z, j, 0), 0.0)


def _shift_up(z, j, row, n):
    return jnp.where(row < n - j, pltpu.roll(z, n - j, 0), 0.0)


def _pool_fwd(p, pool_w, pool_scale, *, name):
    S = p.shape[0]
    C = POOL_GROUP

    def body(u_ref, w_ref, sc_ref, o_ref, y_ref):
        g = pl.program_id(0)
        u = u_ref[...]
        row = lax.broadcasted_iota(jnp.int32, (S, C), 0)
        s1 = u + _shift_down(u, 1, row)
        s2 = s1 + _shift_down(s1, 2, row)
        s3 = s2 + _shift_down(s2, 4, row)
        s4 = s3 + _shift_down(s3, 8, row)
        gi = jnp.zeros((S, C), jnp.int32) + g
        win = jnp.where(gi == 0, s1, jnp.where(gi == 1, s2, jnp.where(gi == 2, s3, s4)))
        width = jnp.where(gi == 0, 2, jnp.where(gi == 1, 4, jnp.where(gi == 2, 8, 16)))
        count = jnp.minimum(row + 1, width).astype(F32)
        o = win / count - u
        o_ref[...] = o
        y_ref[...] = jnp.dot(o.astype(BF16), w_ref[0].astype(BF16), preferred_element_type=F32) * sc_ref[...]

    col = pl.BlockSpec((S, C), lambda g: (0, g))
    return pl.pallas_call(
        body, name=name, grid=(POOL_GROUPS,),
        in_specs=[col, pl.BlockSpec((1, C, C), lambda g: (g, 0, 0)), pl.BlockSpec((1, C), lambda g: (0, g))],
        out_specs=(col, col),
        out_shape=(jax.ShapeDtypeStruct((S, POOL_GROUPS * C), F32), jax.ShapeDtypeStruct((S, POOL_GROUPS * C), F32)),
        compiler_params=_cp(("parallel",), 48),
    )(p, pool_w, pool_scale)


def _pool_bwd(dcat, o, pool_w, pool_scale, *, name):
    S = o.shape[0]
    C = POOL_GROUP

    def body(dy_ref, o_ref, w_ref, sc_ref, du_ref, dw_ref, dsc_ref):
        g = pl.program_id(0)
        dy = dy_ref[...]
        ob = o_ref[...].astype(BF16)
        wb = w_ref[0].astype(BF16)
        mixed = jnp.dot(ob, wb, preferred_element_type=F32)
        dsc_ref[...] = jnp.sum(_fold8(dy * mixed), axis=0, keepdims=True)
        dmix = (dy * sc_ref[...]).astype(BF16)
        dw_ref[0] = lax.dot_general(ob, dmix, (((0,), (0,)), ((), ())), preferred_element_type=F32)
        do = lax.dot_general(dmix, wb, (((1,), (1,)), ((), ())), preferred_element_type=F32)
        row = lax.broadcasted_iota(jnp.int32, (S, C), 0)
        gi = jnp.zeros((S, C), jnp.int32) + g
        width = jnp.where(gi == 0, 2, jnp.where(gi == 1, 4, jnp.where(gi == 2, 8, 16)))
        z = do / jnp.minimum(row + 1, width).astype(F32)
        s1 = z + _shift_up(z, 1, row, S)
        s2 = s1 + _shift_up(s1, 2, row, S)
        s3 = s2 + _shift_up(s2, 4, row, S)
        s4 = s3 + _shift_up(s3, 8, row, S)
        win = jnp.where(gi == 0, s1, jnp.where(gi == 1, s2, jnp.where(gi == 2, s3, s4)))
        du_ref[...] = (win - do).astype(BF16)

    col = pl.BlockSpec((S, C), lambda g: (0, g))
    return pl.pallas_call(
        body, name=name, grid=(POOL_GROUPS,),
        in_specs=[col, col, pl.BlockSpec((1, C, C), lambda g: (g, 0, 0)), pl.BlockSpec((1, C), lambda g: (0, g))],
        out_specs=(col, pl.BlockSpec((1, C, C), lambda g: (g, 0, 0)), pl.BlockSpec((1, C), lambda g: (0, g))),
        out_shape=(jax.ShapeDtypeStruct((S, POOL_GROUPS * C), BF16), jax.ShapeDtypeStruct((POOL_GROUPS, C, C), F32),
                   jax.ShapeDtypeStruct((1, POOL_GROUPS * C), F32)),
        compiler_params=_cp(("parallel",), 48),
    )(dcat, o, pool_w, pool_scale)


def _block_ones():
    r = lax.broadcasted_iota(jnp.int32, (128, 128), 0) // HEAD
    c = lax.broadcasted_iota(jnp.int32, (128, 128), 1) // HEAD
    return jnp.where(r == c, 1.0, 0.0).astype(BF16)


def _segsum(x, bd):
    outs = []
    for j in range(x.shape[1] // 128):
        xs = x[:, j * 128:(j + 1) * 128]
        hi = xs.astype(BF16)
        lo = (xs - hi.astype(F32)).astype(BF16)
        outs.append(jnp.dot(hi, bd, preferred_element_type=F32) + jnp.dot(lo, bd, preferred_element_type=F32))
    return jnp.concatenate(outs, axis=1)


def _prep_common(q, qprev, first, mu, wl, w0, a0, kkw, kaw, R):
    tm = q.shape[0]
    row = lax.broadcasted_iota(jnp.int32, q.shape, 0)
    last = qprev[7:8, :] * first
    prev = jnp.where(row == 0, last, pltpu.roll(q, 1, 0))
    ps = q + mu * (prev - q)
    r = ps[:, 0:R]
    k = ps[:, R:2 * R]
    v = ps[:, 2 * R:3 * R]
    lo_in = ps[:, 3 * R:3 * R + LORA_PAD]
    lane = lax.broadcasted_iota(jnp.int32, (tm, LORA_PAD), 1)
    m_w = lane < LORA_W
    m_a = lane < LORA_W + LORA_A
    m_g = lane < LORA_W + LORA_A + LORA_G
    act = jnp.where(m_w, jnp.tanh(lo_in), jnp.where(m_a, lo_in, jnp.where(m_g, jax.nn.sigmoid(lo_in), 0.0)))
    lo = jnp.dot(act.astype(BF16), wl, preferred_element_type=F32)
    wpre = w0 + lo[:, 0:R]
    apre = a0 + lo[:, R:2 * R]
    g = lo[:, 2 * R:3 * R]
    neg = -wpre
    softplus = jnp.maximum(neg, 0.0) + jnp.log(1.0 + jnp.exp(-jnp.abs(neg)))
    wlog = -softplus - 0.5
    ew = jnp.exp(wlog)
    decay = jnp.exp(-ew)
    a = jax.nn.sigmoid(apre)
    kk = k * kkw
    bd = _block_ones()
    n2 = _segsum(kk * kk, bd)
    nrm = jnp.maximum(jnp.sqrt(n2), 1e-12)
    kap = kk / nrm
    kmul = 1.0 + (a - 1.0) * kaw
    k2 = k * kmul
    return dict(prev=prev, r=r, k=k, v=v, act=act, m_w=m_w, m_a=m_a, m_g=m_g, wpre=wpre, g=g, ew=ew, decay=decay,
                a=a, n2=n2, nrm=nrm, kap=kap, kmul=kmul, k2=k2, bd=bd)


def _prev_rows_spec(tm, w):
    return pl.BlockSpec((8, w), lambda i: (jnp.maximum(i * (tm // 8) - 1, 0), 0))


def _rwkv_prep(q, mu, wl, w0, a0, kkw, kaw, *, name):
    S, QW = q.shape
    R = w0.shape[1]
    tm = ROW_TILE // 2

    def body(q_ref, qp_ref, mu_ref, wl_ref, w0_ref, a0_ref, kk_ref, ka_ref, r_ref, w_ref, k_ref, v_ref, kap_ref,
             b_ref, g_ref):
        first = jnp.where(pl.program_id(0) > 0, 1.0, 0.0)
        t = _prep_common(q_ref[...], qp_ref[...], first, mu_ref[...], wl_ref[...], w0_ref[...], a0_ref[...],
                         kk_ref[...], ka_ref[...], R)
        r_ref[...] = t["r"]
        w_ref[...] = t["decay"]
        k_ref[...] = t["k2"]
        v_ref[...] = t["v"]
        kap_ref[...] = t["kap"]
        b_ref[...] = t["kap"] * t["a"]
        g_ref[...] = t["g"]

    vec = _vec_spec(1, R)
    return pl.pallas_call(
        body, name=name, grid=(S // tm,),
        in_specs=[_row_spec(tm, QW), _prev_rows_spec(tm, QW), _vec_spec(1, QW), _vec_spec(LORA_PAD, 3 * R), vec, vec,
                  vec, vec],
        out_specs=tuple([_row_spec(tm, R)] * 7),
        out_shape=tuple([jax.ShapeDtypeStruct((S, R), F32)] * 7),
        compiler_params=_cp(("parallel",), 48),
    )(q, q, mu, wl, w0, a0, kkw, kaw)


def _rwkv_prep_bwd(q, mu, wl, w0, a0, kkw, kaw, grads, *, name):
    S, QW = q.shape
    R = w0.shape[1]
    tm = ROW_TILE // 2
    n = S // tm

    def body(q_ref, qp_ref, mu_ref, wl_ref, w0_ref, a0_ref, kk_ref, ka_ref, dr_ref, dw_ref, dk2_ref, dv_ref, dkap_ref,
             db_ref, dg_ref, drb_ref, dk2b_ref, dvb_ref, dps_ref, dwl_ref, sums_ref, acc_ref):
        i = pl.program_id(0)

        @pl.when(i == 0)
        def _():
            acc_ref[...] = jnp.zeros_like(acc_ref)
            dwl_ref[...] = jnp.zeros_like(dwl_ref)

        first = jnp.where(i > 0, 1.0, 0.0)
        wl = wl_ref[...]
        kkw = kk_ref[...]
        kaw = ka_ref[...]
        t = _prep_common(q_ref[...], qp_ref[...], first, mu_ref[...], wl, w0_ref[...], a0_ref[...], kkw, kaw, R)
        a, kap, k, act = t["a"], t["kap"], t["k"], t["act"]
        db = db_ref[...]
        dk2 = dk2_ref[...] + dk2b_ref[...]
        dkap = dkap_ref[...] + db * a
        da = db * kap + dk2 * k * kaw
        dk = dk2 * t["kmul"]
        proj = jnp.where(jnp.sqrt(t["n2"]) > 1e-12, _segsum(kap * dkap, t["bd"]), 0.0)
        dkk = (dkap - kap * proj) / t["nrm"]
        dk = dk + dkk * kkw
        dapre = da * a * (1.0 - a)
        dwlog = dw_ref[...] * t["decay"] * (-t["ew"])
        dwpre = dwlog * jax.nn.sigmoid(-t["wpre"])
        acc_ref[0] += _fold8(dwpre)
        acc_ref[1] += _fold8(dapre)
        acc_ref[2] += _fold8(dkk * k)
        acc_ref[3] += _fold8(dk2 * k * (a - 1.0))
        dlo = jnp.concatenate([dwpre, dapre, dg_ref[...]], axis=1).astype(BF16)
        dwl_ref[...] += lax.dot_general(act.astype(BF16), dlo, (((0,), (0,)), ((), ())), preferred_element_type=F32)
        dact = lax.dot_general(dlo, wl, (((1,), (1,)), ((), ())), preferred_element_type=F32)
        dlin = jnp.where(t["m_w"], dact * (1.0 - act * act),
                         jnp.where(t["m_a"], dact, jnp.where(t["m_g"], dact * act * (1.0 - act), 0.0)))
        dps_ref[:, 0:R] = dr_ref[...] + drb_ref[...]
        dps_ref[:, R:2 * R] = dk
        dps_ref[:, 2 * R:3 * R] = dv_ref[...] + dvb_ref[...]
        dps_ref[:, 3 * R:3 * R + LORA_PAD] = dlin
        dps_ref[:, 3 * R + LORA_PAD:] = jnp.zeros((tm, QW - 3 * R - LORA_PAD), F32)

        @pl.when(i == n - 1)
        def _():
            for j in range(4):
                sums_ref[j:j + 1, :] = jnp.sum(acc_ref[j], axis=0, keepdims=True)

    vec = _vec_spec(1, R)
    return pl.pallas_call(
        body, name=name, grid=(n,),
        in_specs=[_row_spec(tm, QW), _prev_rows_spec(tm, QW), _vec_spec(1, QW), _vec_spec(LORA_PAD, 3 * R), vec, vec,
                  vec, vec] + [_row_spec(tm, R)] * 10,
        out_specs=(_row_spec(tm, QW), _vec_spec(LORA_PAD, 3 * R), _vec_spec(4, R)),
        out_shape=(jax.ShapeDtypeStruct((S, QW), F32), jax.ShapeDtypeStruct((LORA_PAD, 3 * R), F32),
                   jax.ShapeDtypeStruct((4, R), F32)),
        scratch_shapes=[pltpu.VMEM((4, 8, R), F32)],
        compiler_params=_cp(("arbitrary",), 56),
    )(q, q, mu, wl, w0, a0, kkw, kaw, *grads)


def _tshift_bwd(dps, q, mu, *, name):
    S, QW = q.shape
    tm = ROW_TILE // 2
    n = S // tm

    def body(d_ref, dn_ref, q_ref, qp_ref, mu_ref, dq_ref, dmu_ref, acc_ref):
        i = pl.program_id(0)

        @pl.when(i == 0)
        def _():
            acc_ref[...] = jnp.zeros_like(acc_ref)

        mu = mu_ref[...]
        d = d_ref[...]
        qv = q_ref[...]
        row = lax.broadcasted_iota(jnp.int32, d.shape, 0)
        first = jnp.where(i > 0, 1.0, 0.0)
        notlast = jnp.where(i < n - 1, 1.0, 0.0)
        prev = jnp.where(row == 0, qp_ref[7:8, :] * first, pltpu.roll(qv, 1, 0))
        z = d * mu
        nxt = jnp.where(row == tm - 1, dn_ref[0:1, :] * mu * notlast, pltpu.roll(z, tm - 1, 0))
        dq_ref[...] = (d * (1.0 - mu) + nxt).astype(BF16)
        acc_ref[...] += _fold8(d * (prev - qv))

        @pl.when(i == n - 1)
        def _():
            dmu_ref[...] = jnp.sum(acc_ref[...], axis=0, keepdims=True)

    nblk8 = S // 8
    next_spec = pl.BlockSpec((8, QW), lambda i: (jnp.minimum((i + 1) * (tm // 8), nblk8 - 1), 0))
    return pl.pallas_call(
        body, name=name, grid=(n,),
        in_specs=[_row_spec(tm, QW), next_spec, _row_spec(tm, QW), _prev_rows_spec(tm, QW), _vec_spec(1, QW)],
        out_specs=(_row_spec(tm, QW), _vec_spec(1, QW)),
        out_shape=(jax.ShapeDtypeStruct((S, QW), BF16), jax.ShapeDtypeStruct((1, QW), F32)),
        scratch_shapes=[pltpu.VMEM((8, QW), F32)],
        compiler_params=_cp(("arbitrary",), 48),
    )(dps, dps, q, q, mu)


def _post_common(ysc, r, k2, v, lnw, lnb, rk):
    bd = _block_ones()
    mean = _segsum(ysc, bd) * (1.0 / HEAD)
    d = ysc - mean
    var = _segsum(d * d, bd) * (1.0 / HEAD)
    rstd = lax.rsqrt(var + LN_X_EPS)
    yh = d * rstd
    rkk = _segsum(r * k2 * rk, bd)
    z = yh * lnw + lnb + rkk * v
    return bd, rstd, yh, rkk, z


def _rwkv_post(ysc, r, k2, v, g, ypool, lnw, lnb, rk, *, name):
    S, R = ysc.shape
    PW = ypool.shape[1]
    tm = ROW_TILE

    def body(y_ref, r_ref, k_ref, v_ref, g_ref, yp_ref, lw_ref, lb_ref, rk_ref, cat_ref):
        _, _, _, _, z = _post_common(y_ref[...], r_ref[...], k_ref[...], v_ref[...], lw_ref[...], lb_ref[...],
                                     rk_ref[...])
        cat_ref[:, 0:PW] = yp_ref[...].astype(BF16)
        cat_ref[:, PW:] = (z * g_ref[...]).astype(BF16)

    vec = _vec_spec(1, R)
    return pl.pallas_call(
        body, name=name, grid=(S // tm,),
        in_specs=[_row_spec(tm, R)] * 5 + [_row_spec(tm, PW), vec, vec, vec],
        out_specs=_row_spec(tm, PW + R), out_shape=jax.ShapeDtypeStruct((S, PW + R), BF16),
        compiler_params=_cp(("parallel",), 48),
    )(ysc, r, k2, v, g, ypool, lnw, lnb, rk)


def _rwkv_post_bwd(dcat, ysc, r, k2, v, g, lnw, lnb, rk, *, name):
    S, R = ysc.shape
    tm = ROW_TILE
    n = S // tm

    def body(d_ref, y_ref, r_ref, k_ref, v_ref, g_ref, lw_ref, lb_ref, rk_ref, dy_ref, dg_ref, drb_ref, dkb_ref,
             dvb_ref, sums_ref, acc_ref):
        i = pl.program_id(0)

        @pl.when(i == 0)
        def _():
            acc_ref[...] = jnp.zeros_like(acc_ref)

        rv, kv, vv, lw, rkw = r_ref[...], k_ref[...], v_ref[...], lw_ref[...], rk_ref[...]
        bd, rstd, yh, rkk, z = _post_common(y_ref[...], rv, kv, vv, lw, lb_ref[...], rkw)
        dyr = d_ref[...]
        dg_ref[...] = dyr * z
        dz = dyr * g_ref[...]
        dyh = dz * lw
        dy_ref[...] = rstd * (dyh - _segsum(dyh, bd) * (1.0 / HEAD) - yh * (_segsum(dyh * yh, bd) * (1.0 / HEAD)))
        dvb_ref[...] = dz * rkk
        drkk = _segsum(dz * vv, bd)
        drb_ref[...] = drkk * kv * rkw
        dkb_ref[...] = drkk * rv * rkw
        acc_ref[0] += _fold8(dz * yh)
        acc_ref[1] += _fold8(dz)
        acc_ref[2] += _fold8(drkk * rv * kv)

        @pl.when(i == n - 1)
        def _():
            for j in range(3):
                sums_ref[j:j + 1, :] = jnp.sum(acc_ref[j], axis=0, keepdims=True)

    vec = _vec_spec(1, R)
    dspec = _row_spec(tm, R)
    return pl.pallas_call(
        body, name=name, grid=(n,),
        in_specs=[dspec] + [_row_spec(tm, R)] * 5 + [vec, vec, vec],
        out_specs=tuple([_row_spec(tm, R)] * 5) + (_vec_spec(3, R),),
        out_shape=tuple([jax.ShapeDtypeStruct((S, R), F32)] * 5) + (jax.ShapeDtypeStruct((3, R), F32),),
        scratch_shapes=[pltpu.VMEM((3, 8, R), F32)],
        compiler_params=_cp(("arbitrary",), 56),
    )(dcat, ysc, r, k2, v, g, lnw, lnb, rk)


SEL_ROWS = 64


def _column_selector():
    row = lax.broadcasted_iota(jnp.int32, (SEL_ROWS, 8 * 128), 0)
    col = lax.broadcasted_iota(jnp.int32, (SEL_ROWS, 8 * 128), 1)
    head, rest = row // 32, row % 32
    hit = (rest < 24) & (rest % 8 == col // 128) & (head == (col % 128) // HEAD)
    return jnp.where(hit, 1.0, 0.0).astype(BF16)


def _expand_columns(x, sel):
    hi = x.astype(BF16).astype(F32)
    r1 = x - hi
    mid = r1.astype(BF16).astype(F32)
    lo = (r1 - mid).astype(BF16).astype(F32)
    terms = jnp.concatenate([hi, mid, lo, jnp.zeros_like(x)], axis=0)
    both = jnp.concatenate([terms, pltpu.roll(terms, HEAD, 1)], axis=0)[:, 0:HEAD]
    return lax.dot_general(both.astype(BF16), sel, (((0,), (0,)), ((), ())), preferred_element_type=F32)


def _head_sum_weights():
    row = lax.broadcasted_iota(jnp.int32, (256, 256), 0)
    col = lax.broadcasted_iota(jnp.int32, (256, 256), 1)
    return jnp.where((row % 128) // HEAD == col // 128, 1.0, 0.0).astype(BF16)


def _head_sums_mxu(products, w2):
    rows = []
    for p in products:
        hi = p.astype(BF16)
        rows.append(jnp.concatenate([hi, (p - hi.astype(F32)).astype(BF16)], axis=1))
    out = jnp.dot(jnp.concatenate(rows, axis=0), w2, preferred_element_type=F32)
    return [(out[i * HEAD:(i + 1) * HEAD, 0:128], out[i * HEAD:(i + 1) * HEAD, 128:256]) for i in range(len(products))]


def _masked_rows(rows, negate=False):
    head_a = (lax.broadcasted_iota(jnp.int32, rows.shape, 1) % 128) < HEAD
    v = -rows if negate else rows
    return jnp.where(head_a, v, 0.0), jnp.where(head_a, 0.0, v)


def _lane_sums(x, row_a, row_b):
    return jnp.sum(x * row_a, axis=1, keepdims=True), jnp.sum(x * row_b, axis=1, keepdims=True)


def _scan_fwd(r, w, k, v, kap, b, *, name, plan=None):
    S, R = r.shape
    G, T = SCAN_G, SCAN_T
    NP = R // 128
    assert NP % G == 0 and S % T == 0
    GW = 128 * G

    def body(r_ref, w_ref, k_ref, v_ref, kap_ref, b_ref, sel_ref, w2_ref, y_ref, sa_ref, st_ref, s_scr, vc_scr,
             yt_scr, sat_scr):
        c = pl.program_id(1)

        @pl.when(c == 0)
        def _():
            s_scr[...] = jnp.zeros_like(s_scr)

        yt_scr[...] = jnp.zeros_like(yt_scr)
        sat_scr[...] = jnp.zeros_like(sat_scr)
        lane = lax.broadcasted_iota(jnp.int32, (HEAD, 128), 1)
        m_a = lane < HEAD

        def block(tb, carry):
            t0 = pl.multiple_of(tb * 8, 8)
            rb, wb, kb = r_ref[pl.ds(t0, 8), :], w_ref[pl.ds(t0, 8), :], k_ref[pl.ds(t0, 8), :]
            pb, bb, vb = kap_ref[pl.ds(t0, 8), :], b_ref[pl.ds(t0, 8), :], v_ref[pl.ds(t0, 8), :]
            for g in range(G):
                vc_scr[g] = _expand_columns(vb[:, g * 128:(g + 1) * 128], sel_ref[...])

            def put_y(g, parts, hot_y):
                yt_scr[g, 0:HEAD, :] = jnp.where(hot_y, parts[0], yt_scr[g, 0:HEAD, :])
                yt_scr[g, HEAD:, :] = jnp.where(hot_y, parts[1], yt_scr[g, HEAD:, :])

            def put_sa(g, parts, hot_t):
                sat_scr[g, 0:HEAD, :] = jnp.where(hot_t, parts[0], sat_scr[g, 0:HEAD, :])
                sat_scr[g, HEAD:, :] = jnp.where(hot_t, parts[1], sat_scr[g, HEAD:, :])

            npa, npb = _masked_rows(pb, negate=True)
            for j in range(8):
                t = t0 + j
                cols = slice(j * 128, (j + 1) * 128)
                sa_parts, products = [], []
                for g in range(G):
                    sl = slice(g * 128, (g + 1) * 128)
                    sa_parts.append(_lane_sums(s_scr[g], npa[j:j + 1, sl], npb[j:j + 1, sl]))
                hot_t = lane == t
                for g in range(G):
                    sl = slice(g * 128, (g + 1) * 128)
                    sa = jnp.where(m_a, sa_parts[g][0], sa_parts[g][1])
                    st = s_scr[g] * wb[j:j + 1, sl] + sa * bb[j:j + 1, sl] + vc_scr[g, :, cols] * kb[j:j + 1, sl]
                    s_scr[g] = st
                    st_ref[g, t] = st
                    put_sa(g, sa_parts[g], hot_t)
                    products.append(st * rb[j:j + 1, sl])
                for g, parts in enumerate(_head_sums_mxu(products, w2_ref[...])):
                    put_y(g, parts, hot_t)
            return carry

        lax.fori_loop(0, T // 8, block, 0)
        for g in range(G):
            y_ref[:, g * 128:(g + 1) * 128] = yt_scr[g].T[0:T, :]
            sa_ref[:, g * 128:(g + 1) * 128] = sat_scr[g].T[0:T, :]

    tspec = pl.BlockSpec((T, GW), lambda p, c: (c, p))
    sel_spec = pl.BlockSpec((SEL_ROWS, 8 * 128), lambda p, c: (0, 0))
    grid = (NP // G, S // T)
    x_in, x_out, x_shapes, x_scr, x_ops = _host_args(plan)
    w2_spec = pl.BlockSpec((256, 256), lambda p, c: (0, 0))
    outs = pl.pallas_call(
        _host(body, plan, 8, 3, *_grid_ends(grid)), name=name, grid=grid,
        in_specs=[tspec] * 6 + [sel_spec, w2_spec] + x_in,
        out_specs=tuple([tspec, tspec, pl.BlockSpec((G, T, HEAD, 128), lambda p, c: (p, c, 0, 0))] + x_out),
        out_shape=tuple([jax.ShapeDtypeStruct((S, R), F32), jax.ShapeDtypeStruct((S, R), F32),
                         jax.ShapeDtypeStruct((NP, S, HEAD, 128), F32)] + x_shapes),
        scratch_shapes=[pltpu.VMEM((G, HEAD, 128), F32), pltpu.VMEM((G, HEAD, 8 * 128), F32),
                        pltpu.VMEM((G, 128, 128), F32), pltpu.VMEM((G, 128, 128), F32)] + x_scr,
        compiler_params=_cp(_semantics(plan, ("parallel", "arbitrary")), 48),
    )(r, w, k, v, kap, b, _column_selector(), _head_sum_weights(), *x_ops)
    return outs[0], outs[1], outs[2], tuple(outs[3:])


def _scan_bwd(r, w, k, v, kap, b, sa, dy, states, *, name, plan=None):
    S, R = r.shape
    G, T = SCAN_G_BWD, SCAN_T_BWD
    NP = R // 128
    NC = S // T
    GW = 128 * G
    assert NP % G == 0

    def body(r_ref, w_ref, k_ref, v_ref, kap_ref, b_ref, sa_ref, dy_ref, st_ref, sp_ref, sel_ref, w2_ref, dr_ref,
             dw_ref, dk_ref, dv_ref, dkap_ref, db_ref, ds_scr, vc_scr, dyc_scr, sac_scr, dvt_scr):
        ci = pl.program_id(1)

        @pl.when(ci == 0)
        def _():
            ds_scr[...] = jnp.zeros_like(ds_scr)

        dvt_scr[...] = jnp.zeros_like(dvt_scr)
        lane = lax.broadcasted_iota(jnp.int32, (HEAD, 128), 1)
        m_a = lane < HEAD
        sub = lax.broadcasted_iota(jnp.int32, (8, 128), 0)
        zero_i = jnp.zeros((HEAD, 128), jnp.int32)
        has_prev = jnp.where(ci < NC - 1, 1.0, 0.0)

        def state_before(g, t):
            at_start = (zero_i + t) == 0
            return jnp.where(at_start, sp_ref[g, 0] * has_prev, st_ref[g, jnp.maximum(t - 1, 0)])

        def block(it, carry):
            tb = T // 8 - 1 - it
            t0 = pl.multiple_of(tb * 8, 8)
            rb, wb, kb = r_ref[pl.ds(t0, 8), :], w_ref[pl.ds(t0, 8), :], k_ref[pl.ds(t0, 8), :]
            pb, bb = kap_ref[pl.ds(t0, 8), :], b_ref[pl.ds(t0, 8), :]
            vb, dyb, sab = v_ref[pl.ds(t0, 8), :], dy_ref[pl.ds(t0, 8), :], sa_ref[pl.ds(t0, 8), :]
            for g in range(G):
                sl = slice(g * 128, (g + 1) * 128)
                vc_scr[g] = _expand_columns(vb[:, sl], sel_ref[...])
                dyc_scr[g] = _expand_columns(dyb[:, sl], sel_ref[...])
                sac_scr[g] = _expand_columns(sab[:, sl], sel_ref[...])
            outs = [[jnp.zeros((8, 128), F32) for _ in range(5)] for _ in range(G)]
            bba, bbb = _masked_rows(bb)
            def recurrence(j):
                cols = slice(j * 128, (j + 1) * 128)
                dsp, dsa_parts = [], []
                for g in range(G):
                    sl = slice(g * 128, (g + 1) * 128)
                    ds = ds_scr[g] + dyc_scr[g, :, cols] * rb[j:j + 1, sl]
                    dsp.append(ds)
                    dsa_parts.append(_lane_sums(ds, bba[j:j + 1, sl], bbb[j:j + 1, sl]))
                dsas = []
                for g in range(G):
                    sl = slice(g * 128, (g + 1) * 128)
                    dsa = jnp.where(m_a, dsa_parts[g][0], dsa_parts[g][1])
                    ds_scr[g] = dsp[g] * wb[j:j + 1, sl] - dsa * pb[j:j + 1, sl]
                    dsas.append(dsa)
                return dsp, dsas

            def side_work(j, dsp, dsas):
                t = t0 + j
                hot = lane == t
                cols = slice(j * 128, (j + 1) * 128)
                dvs = _head_sums_mxu([dsp[g] * kb[j:j + 1, g * 128:(g + 1) * 128] for g in range(G)], w2_ref[...])
                for g in range(G):
                    ds = dsp[g]
                    s_p = st_ref[g, t - 1] if j > 0 else state_before(g, t)
                    dr_row = jnp.sum(st_ref[g, t] * dyc_scr[g, :, cols], axis=0, keepdims=True)
                    dk_row = jnp.sum(ds * vc_scr[g, :, cols], axis=0, keepdims=True)
                    db_row = jnp.sum(ds * sac_scr[g, :, cols], axis=0, keepdims=True)
                    dw_row = jnp.sum(ds * s_p, axis=0, keepdims=True)
                    dkap_row = -jnp.sum(s_p * dsas[g], axis=0, keepdims=True)
                    dvt_scr[g, 0:HEAD, :] = jnp.where(hot, dvs[g][0], dvt_scr[g, 0:HEAD, :])
                    dvt_scr[g, HEAD:, :] = jnp.where(hot, dvs[g][1], dvt_scr[g, HEAD:, :])
                    pick = sub == j
                    for q, row in enumerate((dr_row, dw_row, dk_row, dkap_row, db_row)):
                        outs[g][q] = jnp.where(pick, row, outs[g][q])

            pending = None
            for j in range(7, -1, -1):
                done = recurrence(j)
                if pending is not None:
                    side_work(*pending)
                pending = (j, *done)
            side_work(*pending)
            for g in range(G):
                sl = slice(g * 128, (g + 1) * 128)
                for q, ref in enumerate((dr_ref, dw_ref, dk_ref, dkap_ref, db_ref)):
                    ref[pl.ds(t0, 8), sl] = outs[g][q]
            return carry

        lax.fori_loop(0, T // 8, block, 0)
        for g in range(G):
            dv_ref[:, g * 128:(g + 1) * 128] = dvt_scr[g].T[0:T, :]

    tspec = pl.BlockSpec((T, GW), lambda p, c: (NC - 1 - c, p))
    st_spec = pl.BlockSpec((G, T, HEAD, 128), lambda p, c: (p, NC - 1 - c, 0, 0))
    prev_spec = pl.BlockSpec((G, 1, HEAD, 128), lambda p, c: (p, jnp.maximum((NC - 1 - c) * T - 1, 0), 0, 0))
    sel_spec = pl.BlockSpec((SEL_ROWS, 8 * 128), lambda p, c: (0, 0))
    grid = (NP // G, NC)
    x_in, x_out, x_shapes, x_scr, x_ops = _host_args(plan)
    outs = pl.pallas_call(
        _host(body, plan, 12, 6, *_grid_ends(grid)), name=name, grid=grid,
        in_specs=[tspec] * 8 + [st_spec, prev_spec, sel_spec, pl.BlockSpec((256, 256), lambda p, c: (0, 0))] + x_in,
        out_specs=tuple([tspec] * 6 + x_out),
        out_shape=tuple([jax.ShapeDtypeStruct((S, R), F32)] * 6 + x_shapes),
        scratch_shapes=[pltpu.VMEM((G, HEAD, 128), F32), pltpu.VMEM((G, HEAD, 8 * 128), F32),
                        pltpu.VMEM((G, HEAD, 8 * 128), F32), pltpu.VMEM((G, HEAD, 8 * 128), F32),
                        pltpu.VMEM((G, 128, 128), F32)] + x_scr,
        compiler_params=_cp(_semantics(plan, ("parallel", "arbitrary")), 48),
    )(r, w, k, v, kap, b, sa, dy, states, states, _column_selector(), _head_sum_weights(), *x_ops)
    return tuple(outs[:6]), tuple(outs[6:])


def _sum_parts(parts, *, name):
    P, rows, W = parts.shape
    tr = rows
    for cand in (1024, 512, 256, 128, 64, 32, 16, 8):
        if rows % cand == 0:
            tr = cand
            break

    def body(p_ref, o_ref):
        acc = p_ref[0]
        for s in range(1, P):
            acc = acc + p_ref[s]
        o_ref[...] = acc

    return pl.pallas_call(
        body, name=name, grid=(rows // tr,),
        in_specs=[pl.BlockSpec((P, tr, W), lambda i: (0, i, 0))],
        out_specs=pl.BlockSpec((tr, W), lambda i: (i, 0)), out_shape=jax.ShapeDtypeStruct((rows, W), F32),
        compiler_params=_cp(("parallel",), 32),
    )(parts)


def _adamw(w, m, v, parts, *, name):
    R, C = w.shape
    P = parts.shape[0]
    tr = R
    for cand in (1024, 512, 256, 128, 64, 32, 16, 8):
        if R % cand == 0 and cand * C * 4 * (7 + P) <= 10 * 1024 * 1024:
            tr = cand
            break
    bc1 = 1.0 - ADAM_B1 ** ADAM_STEP
    bc2 = 1.0 - ADAM_B2 ** ADAM_STEP

    def body(w_ref, m_ref, v_ref, p_ref, g_ref, d_ref, nm_ref, nv_ref):
        g = p_ref[0].astype(F32)
        for s in range(1, P):
            g = g + p_ref[s].astype(F32)
        m1 = ADAM_B1 * m_ref[...] + (1.0 - ADAM_B1) * g
        v1 = ADAM_B2 * v_ref[...] + (1.0 - ADAM_B2) * (g * g)
        m_hat = m1 / bc1
        v_hat = v1 / bc2
        g_ref[...] = g
        d_ref[...] = -ADAM_LR * (m_hat / (jnp.sqrt(v_hat) + ADAM_EPS) + ADAM_WD * w_ref[...])
        nm_ref[...] = m1
        nv_ref[...] = v1

    spec = pl.BlockSpec((tr, C), lambda i: (i, 0))
    return pl.pallas_call(
        body, name=name, grid=(R // tr,),
        in_specs=[spec, spec, spec, pl.BlockSpec((P, tr, C), lambda i: (0, i, 0))],
        out_specs=(spec, spec, spec, spec), out_shape=tuple([jax.ShapeDtypeStruct((R, C), F32)] * 4),
        compiler_params=_cp(("parallel",), 40),
    )(w, m, v, parts)


def _cols_full(g8):
    n, rows, c = g8.shape
    return jnp.transpose(g8, (1, 0, 2)).reshape(rows, n * c)


def _cols_split(full):
    rows, cols = full.shape
    return jnp.transpose(full.reshape(rows, N_DEV, cols // N_DEV), (1, 0, 2))


def _pack(vals, rows_multiple=512):
    flat = jnp.concatenate([v.reshape(-1).astype(F32) for v in vals])
    n = flat.shape[0]
    unit = 128 * rows_multiple
    padded = ((n + unit - 1) // unit) * unit
    return jnp.pad(flat, (0, padded - n)).reshape(padded // 128, 128)


def _unpack(packed, shapes):
    flat = packed.reshape(-1)
    out, off = [], 0
    for shp in shapes:
        size = 1
        for d in shp:
            size *= d
        out.append(flat[off:off + size].reshape(shp))
        off += size
    return out


def _ffn_forward(x, weights, gpre, gpost, shift, scale1p, gw, tag, up_plan=None, down_from_plan=None,
                 down_plan=None):
    g8, u8, d8 = weights
    h = _pre_norm_mod(x, gpre, shift, scale1p, name=f"{tag}_pre")
    au, s, carried = _ffn_up(h, g8, u8, tm=1024, tk=2048, plan=up_plan,
                             name=f"{tag}_up" + ("_carry" if up_plan else ""))
    if down_from_plan is not None:
        d8 = carried[down_from_plan]
    wd = d8.reshape(-1, d8.shape[2])
    if down_plan is None:
        f, carried_down = _mm(s, wd, tm=1024, tn=1024, tk=2048, name=f"{tag}_down"), ()
    else:
        f, carried_down = _mm(s, wd, tm=1024, tn=1024, tk=2048, name=f"{tag}_down_carry", plan=down_plan)
    xo = _post_norm_res(x, f, gpost, gw, name=f"{tag}_post")
    return xo, (h, au, s, f), d8, carried, carried_down


def _ffn_backward(dxo, x, saved, weights, gpre, gpost, scale1p, gw, tag, plans=None, own_sums=None,
                  swap_in_dh=False):
    g8, u8, d8 = weights
    h, au, s, f = saved
    plans = dict(plans or {})

    def nm(key):
        return f"{tag}_{key}" + ("_carry" if plans.get(key) is not None else "")

    df, post_sums = _post_norm_res_bwd(dxo, f, gpost, gw, MACARON, name=f"{tag}_post_bwd")
    dwd, got_dwd = _ffn_dwd(s, df, tn=1024, tk=1024, plan=plans.get("dwd"), name=nm("dwd"))
    if own_sums:
        plans["down_bwd"] = _ChipsPlan(own_sums([dwd], "d"))
    dau, got_down = _ffn_down_bwd(df, d8, au, tm=1024, tk=2048, plan=plans.get("down_bwd"), name=nm("down_bwd"))
    dwgu, got_dwgu = _ffn_dwgu(h, dau, tm=1024, tk=1024, plan=plans.get("dwgu"), name=nm("dwgu"))
    if own_sums:
        plans["dh"] = _ChipsPlan(own_sums([dwgu], "gu"))
    elif swap_in_dh:
        plans["dh"] = _SwapPlan([dwgu, dwd])
    dh, got_dh = _ffn_dh(dau, g8, u8, tm=1024, tn=1024, plan=plans.get("dh"),
                         name=nm("dh") + ("_swap" if swap_in_dh else ""))
    dx, pre_sums = _pre_norm_mod_bwd(dh, x, dxo, gpre, scale1p, name=f"{tag}_pre_bwd")
    carried = {"dwd": got_dwd, "down_bwd": got_down, "dwgu": got_dwgu, "dh": got_dh}
    return dx, dwgu, dwd, pre_sums, post_sums, carried


def kernel(x, c, w_ada, b_ada, norm_pre, norm_post, ffn1_w_gate, ffn1_w_up, ffn1_w_down, w_in, mu_shift, pool_w, pool_scale, w0, w2, a0, a2, g2, k_k, k_a, r_k, lnx_w, lnx_b, w_out, ffn2_w_gate, ffn2_w_up, ffn2_w_down, loss_target, m_w_ada, m_b_ada, m_norm_pre, m_norm_post, m_ffn1_w_gate, m_ffn1_w_up, m_ffn1_w_down, m_w_in, m_mu_shift, m_pool_w, m_pool_scale, m_w0, m_w2, m_a0, m_a2, m_g2, m_k_k, m_k_a, m_r_k, m_lnx_w, m_lnx_b, m_w_out, m_ffn2_w_gate, m_ffn2_w_up, m_ffn2_w_down, v_w_ada, v_b_ada, v_norm_pre, v_norm_post, v_ffn1_w_gate, v_ffn1_w_up, v_ffn1_w_down, v_w_in, v_mu_shift, v_pool_w, v_pool_scale, v_w0, v_w2, v_a0, v_a2, v_g2, v_k_k, v_k_a, v_r_k, v_lnx_w, v_lnx_b, v_w_out, v_ffn2_w_gate, v_ffn2_w_up, v_ffn2_w_down):
    names = ["w_ada", "b_ada", "norm_pre", "norm_post", "ffn1_w_gate", "ffn1_w_up", "ffn1_w_down", "w_in", "mu_shift",
             "pool_w", "pool_scale", "w0", "w2", "a0", "a2", "g2", "k_k", "k_a", "r_k", "lnx_w", "lnx_b", "w_out",
             "ffn2_w_gate", "ffn2_w_up", "ffn2_w_down"]
    env = dict(locals())
    W = {n: env[n][0] for n in names}
    M1 = {n: env["m_" + n][0] for n in names}
    V1 = {n: env["v_" + n][0] for n in names}

    me = _my_index()
    xs = x[0]
    tgt = loss_target[0]
    S, D = xs.shape
    F = W["ffn1_w_gate"].shape[1] * N_DEV
    R = W["w0"].shape[0]
    PW = D - R
    IN_W = W["w_in"].shape[1] * N_DEV
    P_W = F
    QW = P_W - PW
    NMOD = 9 * D
    ada_c = W["w_ada"].shape[1]

    first_pad = ((0, 0), (0, FF_TILE - W["ffn1_w_gate"].shape[1]))
    c_all, npre8, npost8, w2_8, a2_8, g2_8, g8_1, u8_1 = _gather_two_level(
        [c, W["norm_pre"], W["norm_post"], W["w2"].astype(BF16), W["a2"].astype(BF16), W["g2"].astype(BF16),
         jnp.pad(W["ffn1_w_gate"].astype(BF16), first_pad), jnp.pad(W["ffn1_w_up"].astype(BF16), first_pad)],
        name="gather_first")
    c_all = c_all.reshape(N_DEV, D)
    gpre = _cols_full(npre8)
    gpost = _cols_full(npost8)
    wl = jnp.zeros((LORA_PAD, 3 * R), BF16)
    wl = wl.at[0:LORA_W, 0:R].set(_cols_full(w2_8))
    wl = wl.at[LORA_W:LORA_W + LORA_A, R:2 * R].set(_cols_full(a2_8))
    wl = wl.at[LORA_W + LORA_A:LORA_W + LORA_A + LORA_G, 2 * R:3 * R].set(_cols_full(g2_8))

    sc_all = jax.nn.silu(c_all)
    sc_pad = jnp.concatenate([sc_all, jnp.zeros((8, D), F32)], axis=0).astype(BF16)
    modcols = _mm(sc_pad, W["w_ada"], tm=16, tn=ada_c, tk=256, name="ada_fwd")[0:N_DEV]
    modcols = modcols + lax.dynamic_slice(W["b_ada"], (me * ada_c,), (ada_c,))[None, :]
    (mod8,) = _exchange([modcols], scatter=False, name="gather_mod")
    mod = lax.dynamic_index_in_dim(mod8, me, axis=1, keepdims=False).reshape(9, D)

    def mod_row(i):
        return mod[i:i + 1, :]

    f_pad = FF_TILE - F // N_DEV

    def ffn_shards(tag):
        return [jnp.pad(W[f"{tag}_w_gate"].astype(BF16), ((0, 0), (0, f_pad))),
                jnp.pad(W[f"{tag}_w_up"].astype(BF16), ((0, 0), (0, f_pad))),
                jnp.pad(W[f"{tag}_w_down"].astype(BF16), ((0, f_pad), (0, 0)))]

    ffn1_shards = ffn_shards("ffn1")
    up_plan = _GatherPlan([ffn1_shards[2]])
    down_plan = _GatherPlan([W["w_in"].astype(BF16)])
    scan_plan = _GatherPlan(ffn_shards("ffn2") + [W["w_out"].astype(BF16)])

    mu_p = jnp.pad(W["mu_shift"], (0, QW - W["mu_shift"].shape[0]))[None, :]
    vec = lambda a: a.reshape(1, -1)
    w0r, a0r, kkr, kar = vec(W["w0"]), vec(W["a0"]), vec(W["k_k"]), vec(W["k_a"])
    lnw, lnb, rkr = vec(W["lnx_w"]), vec(W["lnx_b"]), vec(W["r_k"])
    pscale = vec(W["pool_scale"])

    sc1p = [1.0 + mod_row(3 * s + 1) for s in range(3)]
    shifts = [mod_row(3 * s) for s in range(3)]
    wgts = [MACARON, 1.0, MACARON]
    gws = [wgts[s] * (1.0 + mod_row(3 * s + 2)) for s in range(3)]
    gp = [gpre[s:s + 1] for s in range(3)]
    gq = [gpost[s:s + 1] for s in range(3)]

    x1, sv1, d8_1, _, (win8,) = _ffn_forward(xs, (g8_1, u8_1, None), gp[0], gq[0], shifts[0], sc1p[0], gws[0], "ffn",
                                             up_plan=up_plan, down_from_plan=0, down_plan=down_plan)
    ffn1_w = (g8_1, u8_1, d8_1)
    w_in_p = jnp.pad(_cols_full(win8), ((0, 0), (0, P_W - IN_W)))

    h2 = _pre_norm_mod(x1, gp[1], shifts[1], sc1p[1], name="mix_pre")
    p = _mm(h2, w_in_p, tm=1024, tn=512, tk=2048, name="mix_in")
    q = p[:, PW:]
    o_pool, y_pool = _pool_fwd(p, W["pool_w"], pscale, name="pool_fwd")
    r_s, w_s, k_s, v_s, kap_s, b_s, g_s = _rwkv_prep(q, mu_p, wl, w0r, a0r, kkr, kar, name="rwkv_prep")
    y_scan, sa_s, states, gathered = _scan_fwd(r_s, w_s, k_s, v_s, kap_s, b_s, name="scan_fwd", plan=scan_plan)
    ffn2_w = gathered[:3]
    w_out_f = gathered[3].reshape(D, D)
    cat = _rwkv_post(y_scan, r_s, k_s, v_s, g_s, y_pool, lnw, lnb, rkr, name="rwkv_post")
    f2 = _mm(cat, w_out_f, tm=1024, tn=1024, tk=2048, name="mix_out")
    x2 = _post_norm_res(x1, f2, gq[1], gws[1], name="mix_post")

    x3, sv3, _, _, _ = _ffn_forward(x2, ffn2_w, gp[2], gq[2], shifts[2], sc1p[2], gws[2], "ffn")

    loss_part, dx3 = _loss_head(x3, tgt, name="loss_head")
    loss = lax.psum(loss_part[0, 0], MESH_AXES)

    def by_core_chip(blocks):
        shp = blocks.shape
        t = blocks.astype(BF16).reshape((N_DEV // 2, 2) + shp[1:])
        return jnp.swapaxes(t, 0, 1)

    def chip_sums(mine, tag):
        got = _sibling_swap(mine, name=f"{tag}_swap")
        return [_pair_add(m, g, name=f"{tag}_add{i}") for i, (m, g) in enumerate(zip(mine, got))]

    def ffn_parts(pgu, pd):
        fs = F // N_DEV
        return pgu[:, :D, :fs], pgu[:, D:, :fs], pd[:, :fs, :]

    dx2, dwgu2, dwd2, pre3, post3, got2 = _ffn_backward(dx3, x2, sv3, ffn2_w, gp[2], gq[2], sc1p[2], gws[2], "ffn",
                                                        swap_in_dh=True)
    sums2_gu = _pair_add(dwgu2, got2["dh"][0], name="scatter_ffn_add0")
    sums2_d = _pair_add(dwd2, got2["dh"][1], name="scatter_ffn_add1")

    df2, post2 = _post_norm_res_bwd(dx2, f2, gq[1], gws[1], 1.0, name="mix_post_bwd")
    dw_out = _mm(cat, df2, ta=True, tm=1024, tn=1024, tk=1024, name="mix_dwout")
    dcat = _mm(df2, w_out_f, tb=True, tm=1024, tn=1024, tk=2048, name="mix_dcat")
    dyr = dcat[:, PW:]
    dysc, dg, dr_b, dk2_b, dv_b, post_sums = _rwkv_post_bwd(dyr, y_scan, r_s, k_s, v_s, g_s, lnw, lnb, rkr,
                                                             name="rwkv_post_bwd")
    (dr, dw, dk2, dv, dkap, db), parts2 = _scan_bwd(r_s, w_s, k_s, v_s, kap_s, b_s, sa_s, dysc, states,
                                                    name="scan_bwd", plan=_ChipsPlan([sums2_gu, sums2_d]))
    dps, dwl, prep_sums = _rwkv_prep_bwd(q, mu_p, wl, w0r, a0r, kkr, kar,
                                         (dr, dw, dk2, dv, dkap, db, dg, dr_b, dk2_b, dv_b), name="rwkv_prep_bwd")
    dq, dmu = _tshift_bwd(dps, q, mu_p, name="tshift_bwd")
    du_pool, dpool_w, dpool_scale = _pool_bwd(dcat, o_pool, W["pool_w"], pscale, name="pool_bwd")
    dp = jnp.concatenate([du_pool, dq], axis=1)
    dw_in = _mm(h2, dp, ta=True, tm=1024, tn=512, tk=1024, name="mix_dwin")
    dh2 = _mm(dp, w_in_p, tb=True, tm=1024, tn=1024, tk=2816, name="mix_dh")
    dx1, pre2 = _pre_norm_mod_bwd(dh2, x1, dx2, gp[1], sc1p[1], name="mix_pre_bwd")

    sums_mix = chip_sums([by_core_chip(_cols_split(dw_in[:, :IN_W])),
                          by_core_chip(dw_out.reshape(N_DEV, D // N_DEV, D))], "scatter_mixer")
    early = [dmu[0, :W["mu_shift"].shape[0]], dpool_w, dpool_scale, prep_sums[0], prep_sums[1], prep_sums[2],
             prep_sums[3], post_sums[2], post_sums[0], post_sums[1],
             dwl[0:LORA_W, 0:R], dwl[LORA_W:LORA_W + LORA_A, R:2 * R],
             dwl[LORA_W + LORA_A:LORA_W + LORA_A + LORA_G, 2 * R:3 * R]]
    early_shapes = [a.shape for a in early]
    dx0, _, _, pre1, post1, got = _ffn_backward(
        dx1, xs, sv1, ffn1_w, gp[0], gq[0], sc1p[0], gws[0], "ffn",
        plans={"dwd": _ChipsPlan(sums_mix), "dwgu": _GatherPlan([_pack(early)])},
        own_sums=lambda blocks, part: chip_sums(blocks, f"scatter_ffn_{part}"))

    pres, posts = [pre1, pre2, pre3], [post1, post2, post3]
    dmod = jnp.stack([jnp.stack([pres[s][0], pres[s][1], posts[s][0]]) for s in range(3)]).reshape(NMOD // 128, 128)
    late = [jnp.stack([pres[s][2] for s in range(3)]), jnp.stack([posts[s][1] for s in range(3)])]
    late_shapes = [a.shape for a in late]
    dmod8, late8 = _gather_two_level([dmod, _pack(late, rows_multiple=8)], name="gather_grads")
    g_b_ada = _sum_parts(dmod8, name="sum_dmod").reshape(NMOD)
    g_npre, g_npost = _unpack(_sum_parts(late8, name="sum_late"), late_shapes)
    (g_mu, g_pool_w, g_pool_scale, g_w0, g_a0, g_kk, g_ka, g_rk, g_lnw, g_lnb, g_w2, g_a2,
     g_g2) = _unpack(_sum_parts(got["dwgu"][0], name="sum_small"), early_shapes)

    dmod_all = dmod8.reshape(N_DEV, NMOD)
    dmod_cols = lax.dynamic_slice(dmod_all, (0, me * ada_c), (N_DEV, ada_c))
    dmod_cols = jnp.concatenate([dmod_cols, jnp.zeros_like(dmod_cols)], axis=0)
    g_w_ada = _mm(sc_pad, dmod_cols, ta=True, tm=D, tn=ada_c // 9, tk=16, name="ada_bwd")

    pg2, pu2, pd2 = ffn_parts(*parts2)
    pin, pout = got["dwd"]
    pg1, pu1, pd1 = ffn_parts(got["dh"][0], got["down_bwd"][0])

    res = {}

    def big(nm, parts, tag):
        res[nm] = _adamw(W[nm], M1[nm], V1[nm], parts, name=tag)

    big("ffn1_w_gate", pg1, "adamw_cols")
    big("ffn1_w_up", pu1, "adamw_cols")
    big("ffn1_w_down", pd1, "adamw_rows")
    big("ffn2_w_gate", pg2, "adamw_cols")
    big("ffn2_w_up", pu2, "adamw_cols")
    big("ffn2_w_down", pd2, "adamw_rows")
    big("w_in", pin, "adamw_w_in")
    big("w_out", pout, "adamw_w_out")
    big("w_ada", g_w_ada[None], "adamw_w_ada")

    def my_cols(full, width):
        return lax.dynamic_slice_in_dim(full, me * width, width, axis=full.ndim - 1)

    small_names = ["b_ada", "mu_shift", "pool_w", "pool_scale", "w0", "a0", "k_k", "k_a", "r_k", "lnx_w", "lnx_b",
                   "norm_pre", "norm_post", "w2", "a2", "g2"]
    small_grads = [g_b_ada, g_mu, g_pool_w, g_pool_scale, g_w0, g_a0, g_kk, g_ka, g_rk.reshape(W["r_k"].shape), g_lnw,
                   g_lnb, my_cols(g_npre, D // N_DEV), my_cols(g_npost, D // N_DEV), my_cols(g_w2, R // N_DEV),
                   my_cols(g_a2, R // N_DEV), my_cols(g_g2, R // N_DEV)]
    shapes = [W[n].shape for n in small_names]
    packed = _adamw(_pack([W[n] for n in small_names]), _pack([M1[n] for n in small_names]),
                    _pack([V1[n] for n in small_names]), _pack(small_grads)[None], name="adamw_small")
    unpacked = [_unpack(t, shapes) for t in packed]
    for i, nm in enumerate(small_names):
        res[nm] = tuple(unpacked[k][i] for k in range(4))

    outs = [loss, dx0[None]]
    for k in range(4):
        outs.extend(res[nm][k][None] for nm in names)
    return tuple(outs)
```
